```python
import math
import jax, jax.numpy as jnp
from jax import lax
import numpy as np

D_MODEL = 1024
BATCH = 16
SEQ = 256
DEPTH = 2
DEC_BATCH = 2
DEC_SEQ = 1024
PAST_LEN = 512

GRID_W = 64
EPS = 1e-6
CONV_W = 512
CONV_K = 3
LRU_W = 1024
LRU_BLOCKS = 8
LRU_BW = LRU_W // LRU_BLOCKS
LRU_CONV_K = 4
LRU_C = 8.0
MLA_HEADS = 8
Q_RANK = 384
KV_RANK = 256
NOPE_DIM = 128
ROPE_DIM = 64
V_DIM = 128
QK_DIM = NOPE_DIM + ROPE_DIM
ROPE_THETA = 10000.0
Q_BLOCK = 128
HY_W = 512
HY_SHORT_K = 3
HY_BANDS = 16
HY_EMB = 1 + 2 * HY_BANDS
HY_HIDDEN = 64
HY_FAST_DECAY = 0.3
HY_SLOW_DECAY = 1.5
HY_TARGET = 1e-2
D_FF = 2816
N_EXPERTS = 8
TOP_K = 2
D_FF_EXPERT = 1408
IN0 = 3 * CONV_W + 2 * LRU_W
MIX0 = CONV_W + LRU_W
IN1 = Q_RANK + KV_RANK + ROPE_DIM + 3 * HY_W
MIX1 = MLA_HEADS * V_DIM + HY_W

kernel_name = 'hybrid_diffusion_prefix_trunk_step'

F32 = jnp.float32


def rmsnorm(x, g):
    xf = x.astype(F32)
    y = xf * lax.rsqrt(jnp.mean(xf * xf, axis=-1, keepdims=True) + EPS)
    return (y * g.astype(F32)).astype(x.dtype)


def adaln(cond, w_mod, b_mod):
    m = jax.nn.silu(cond) @ w_mod + b_mod
    return jnp.split(m, 6, axis=-1)


def modulate(x, shift, scale):
    return x * (1.0 + scale[:, None, :]) + shift[:, None, :]


def dwconv(x, w, pad_left, pad_right):
    ch = x.shape[-1]
    return lax.conv_general_dilated(
        x, w[:, None, :].astype(x.dtype), window_strides=(1,),
        padding=[(pad_left, pad_right)], dimension_numbers=('NWC', 'WIO', 'NWC'),
        feature_group_count=ch)


def linear_scan(a, b, h0, reverse):
    if h0 is not None:
        idx = -1 if reverse else 0
        b = b.at[:, idx].add(a[:, idx] * h0)

    def comb(left, right):
        al, bl = left
        ar, br = right
        return al * ar, ar * bl + br

    _, h = lax.associative_scan(comb, (a, b), axis=1, reverse=reverse)
    return h


def rglru_dir(x, wa, ba, wi, bi, lam, h0, reverse):
    bsz, L, _ = x.shape
    xb = x.reshape(bsz, L, LRU_BLOCKS, LRU_BW)
    r = jax.nn.sigmoid(jnp.einsum('blnh,nhk->blnk', xb, wa).reshape(bsz, L, LRU_W) + ba)
    i = jax.nn.sigmoid(jnp.einsum('blnh,nhk->blnk', xb, wi).reshape(bsz, L, LRU_W) + bi)
    log_a = -LRU_C * r * jax.nn.softplus(-lam.astype(F32))
    a = jnp.exp(log_a)
    mult = jnp.sqrt(-jnp.expm1(2.0 * log_a))
    return linear_scan(a, mult * (i * x), h0, reverse)


def even_mixer(h, h0, w_in, conv_a, lru_conv_w, lru_conv_b, wa, ba, wi, bi, lam, w_out):
    u = h @ w_in
    ua, ub = u[..., :3 * CONV_W], u[..., 3 * CONV_W:]
    b_g, c_g, x_a = jnp.split(ua, 3, axis=-1)
    y_a = b_g * dwconv(c_g * x_a, conv_a, 1, 1)
    gate, x_b = jnp.split(ub, 2, axis=-1)
    xc = (dwconv(x_b, lru_conv_w, 2, 1) + lru_conv_b).astype(F32)
    h0f = None if h0 is None else h0[:, 0].astype(F32)
    h0b = None if h0 is None else h0[:, 1].astype(F32)
    hf = rglru_dir(xc, wa[0], ba[0], wi[0], bi[0], lam[0], h0f, False)
    hb = rglru_dir(xc, wa[1], ba[1], wi[1], bi[1], lam[1], h0b, True)
    y_b = (hf + hb).astype(h.dtype) * jax.nn.gelu(gate)
    y = jnp.concatenate([y_a, y_b], axis=-1) @ w_out
    if h0 is None:
        return y, jnp.stack([hf[:, -1], hb[:, 0]], axis=1).astype(h.dtype)
    return y, None


def axial_rope(x):
    L = x.shape[1]
    rows = L // GRID_W
    row = jnp.repeat(jnp.arange(rows, dtype=F32), GRID_W)
    col = jnp.tile(jnp.arange(GRID_W, dtype=F32), rows)
    half = ROPE_DIM // 2
    n_freq = half // 2
    inv = ROPE_THETA ** (-jnp.arange(n_freq, dtype=F32) / n_freq)
    ang = jnp.concatenate([row[:, None] * inv, col[:, None] * inv], axis=-1)
    shape = (L,) + (1,) * (x.ndim - 3) + (half,)
    cos, sin = jnp.cos(ang).reshape(shape), jnp.sin(ang).reshape(shape)
    xf = x.astype(F32)
    x1, x2 = xf[..., :half], xf[..., half:]
    return jnp.concatenate([x1 * cos - x2 * sin, x1 * sin + x2 * cos], axis=-1).astype(x.dtype)


def attend(q, k, v):
    bsz, lq, nh, dk = q.shape
    nb = lq // Q_BLOCK
    scale = 1.0 / math.sqrt(dk)
    qb = q.reshape(bsz, nb, Q_BLOCK, nh, dk).transpose(1, 0, 2, 3, 4)

    def blk(qi):
        s = jnp.einsum('bqhd,bkhd->bhqk', qi, k).astype(F32) * scale
        p = jax.nn.softmax(s, axis=-1)
        return jnp.einsum('bhqk,bkhd->bqhd', p.astype(v.dtype), v)

    o = lax.map(blk, qb)
    return o.transpose(1, 0, 2, 3, 4).reshape(bsz, lq, nh * v.shape[-1])


def mla_kv(ckv, krope, w_kv_up):
    bsz, L, _ = ckv.shape
    kv = (ckv @ w_kv_up).reshape(bsz, L, MLA_HEADS, NOPE_DIM + V_DIM)
    k_nope, v = kv[..., :NOPE_DIM], kv[..., NOPE_DIM:]
    k_pe = jnp.broadcast_to(krope[:, :, None, :], (bsz, L, MLA_HEADS, ROPE_DIM))
    return jnp.concatenate([k_nope, k_pe.astype(k_nope.dtype)], axis=-1), v


def hyena_filter(L, w1, b1, w2, b2, w3):
    t = jnp.linspace(0.0, 1.0, L, dtype=F32)[:, None]
    w = 2.0 * math.pi * jnp.arange(L, dtype=F32)[:, None] / L
    f = jnp.linspace(1e-4, HY_BANDS - 1, HY_BANDS, dtype=F32)[None, :]
    z = jnp.concatenate([t, jnp.cos(f * w), -jnp.sin(f * w)], axis=-1)
    hid = jnp.sin(z @ w1.astype(F32) + b1.astype(F32))
    hid = jnp.sin(hid @ w2.astype(F32) + b2.astype(F32))
    hf = hid @ w3.astype(F32)
    max_decay = math.log(HY_TARGET) / HY_FAST_DECAY
    min_decay = math.log(HY_TARGET) / HY_SLOW_DECAY
    deltas = jnp.linspace(min_decay, max_decay, HY_W, dtype=F32)
    decay = jnp.exp(-t * jnp.abs(deltas))
    h_fwd = hf[:, :HY_W] * decay
    h_bwd = hf[:, HY_W:] * decay
    k = jnp.concatenate([h_fwd, jnp.zeros((1, HY_W), F32), jnp.flip(h_bwd[1:], axis=0)], axis=0)
    return k / jnp.sum(jnp.abs(k), axis=0, keepdims=True)


def long_conv(u, k, bias):
    L = u.shape[1]
    uf = jnp.fft.rfft(u.astype(F32), n=2 * L, axis=1)
    kf = jnp.fft.rfft(k, n=2 * L, axis=0)
    y = jnp.fft.irfft(uf * kf[None], n=2 * L, axis=1)[:, :L]
    return (y + u.astype(F32) * bias.astype(F32)).astype(u.dtype)


def odd_mixer(h, ctx, w_in, q_norm, kv_norm, w_q_up, w_kv_up, hy_short_w, hy_short_b,
              f_w1, f_b1, f_w2, f_b2, f_w3, hy_bias, w_out):
    bsz, L, _ = h.shape
    u = h @ w_in
    o1, o2, o3 = Q_RANK, Q_RANK + KV_RANK, Q_RANK + KV_RANK + ROPE_DIM
    cq, ckv_raw, kr, uh = u[..., :o1], u[..., o1:o2], u[..., o2:o3], u[..., o3:]
    q = (rmsnorm(cq, q_norm) @ w_q_up).reshape(bsz, L, MLA_HEADS, QK_DIM)
    ckv = rmsnorm(ckv_raw, kv_norm)
    if ctx is None:
        k, v = mla_kv(ckv, kr, w_kv_up)
        cache = (ckv, kr)
    else:
        q = jnp.concatenate([q[..., :NOPE_DIM], axial_rope(q[..., NOPE_DIM:])], axis=-1)
        k_l, v_l = mla_kv(ckv, axial_rope(kr), w_kv_up)
        k_c, v_c = mla_kv(ctx[0].astype(h.dtype), ctx[1].astype(h.dtype), w_kv_up)
        k = jnp.concatenate([k_c, k_l], axis=1)
        v = jnp.concatenate([v_c, v_l], axis=1)
        cache = None
    y_c = attend(q, k, v)
    uc = dwconv(uh, hy_short_w, 1, 1) + hy_short_b
    x0, x1, vv = jnp.split(uc, 3, axis=-1)
    kfilt = hyena_filter(L, f_w1, f_b1, f_w2, f_b2, f_w3)
    y_d = x0 * long_conv(x1 * vv, kfilt, hy_bias)
    y = jnp.concatenate([y_c, y_d], axis=-1) @ w_out
    return y, cache


def swiglu(h, w_gate, w_up, w_down):
    return (jax.nn.silu(h @ w_gate) * (h @ w_up)) @ w_down


def moe_swiglu(h, w_router, b_router, e_gate, e_up, e_down):
    bsz, L, d = h.shape
    ht = h.reshape(-1, d)
    logits = (ht @ w_router).astype(F32) + b_router.astype(F32)
    probs = jax.nn.softmax(logits, axis=-1)
    top_p, top_i = lax.top_k(probs, TOP_K)
    top_p = top_p / jnp.sum(top_p, axis=-1, keepdims=True)
    gates = jnp.sum(jax.nn.one_hot(top_i, N_EXPERTS, dtype=F32) * top_p[..., None], axis=1)
    hg = jnp.einsum('td,edf->tef', ht, e_gate)
    hu = jnp.einsum('td,edf->tef', ht, e_up)
    act = jax.nn.silu(hg) * hu * gates[..., None].astype(ht.dtype)
    out = jnp.einsum('tef,efd->td', act, e_down)
    return out.reshape(bsz, L, d)


def setup_inputs(seed: int = 0) -> dict:
    key = jax.random.key(seed)
    ks = iter(jax.random.split(key, 64))
    D = D_MODEL

    def nrm(shape, scale):
        return jax.random.normal(next(ks), shape, F32) * scale

    def gain(n):
        return 1.0 + nrm((n,), 0.1)

    inp = {}
    inp['x_prompt'] = nrm((BATCH, SEQ, D), 1.0)
    inp['x_sample'] = nrm((DEC_BATCH, DEC_SEQ, D), 1.0)
    inp['state_l0_lru'] = nrm((DEC_BATCH, 2, LRU_W), 0.5)
    inp['cache_l1_ckv'] = nrm((DEC_BATCH, PAST_LEN, KV_RANK), 1.0)
    inp['cache_l1_krope'] = nrm((DEC_BATCH, PAST_LEN, ROPE_DIM), 1.0)
    inp['c'] = nrm((DEC_BATCH, D), 1.0)
    inp['c_ctx'] = nrm((D,), 1.0)
    inp['l0_norm1'] = gain(D)
    inp['l0_norm2'] = gain(D)
    inp['l0_w_mod'] = nrm((D, 6 * D), 0.5 * D ** -0.5)
    inp['l0_b_mod'] = nrm((6 * D,), 0.02)
    inp['l0_w_in'] = nrm((D, IN0), D ** -0.5)
    inp['l0_conv_a'] = nrm((CONV_K, CONV_W), CONV_K ** -0.5)
    inp['l0_lru_conv_w'] = nrm((LRU_CONV_K, LRU_W), LRU_CONV_K ** -0.5)
    inp['l0_lru_conv_b'] = nrm((LRU_W,), 0.02)
    inp['l0_lru_wa'] = nrm((2, LRU_BLOCKS, LRU_BW, LRU_BW), LRU_BW ** -0.5)
    inp['l0_lru_ba'] = nrm((2, LRU_W), 0.02)
    inp['l0_lru_wi'] = nrm((2, LRU_BLOCKS, LRU_BW, LRU_BW), LRU_BW ** -0.5)
    inp['l0_lru_bi'] = nrm((2, LRU_W), 0.02)
    a0 = jax.random.uniform(next(ks), (2, LRU_W), F32, 0.9, 0.999)
    inp['l0_lru_lambda'] = jnp.log(a0) - jnp.log1p(-a0)
    inp['l0_w_out'] = nrm((MIX0, D), MIX0 ** -0.5)
    inp['l0_ffn_gate'] = nrm((D, D_FF), D ** -0.5)
    inp['l0_ffn_up'] = nrm((D, D_FF), D ** -0.5)
    inp['l0_ffn_down'] = nrm((D_FF, D), D_FF ** -0.5)
    inp['l1_norm1'] = gain(D)
    inp['l1_norm2'] = gain(D)
    inp['l1_w_mod'] = nrm((D, 6 * D), 0.5 * D ** -0.5)
    inp['l1_b_mod'] = nrm((6 * D,), 0.02)
    inp['l1_w_in'] = nrm((D, IN1), D ** -0.5)
    inp['l1_q_norm'] = gain(Q_RANK)
    inp['l1_kv_norm'] = gain(KV_RANK)
    inp['l1_w_q_up'] = nrm((Q_RANK, MLA_HEADS * QK_DIM), Q_RANK ** -0.5)
    inp['l1_w_kv_up'] = nrm((KV_RANK, MLA_HEADS * (NOPE_DIM + V_DIM)), KV_RANK ** -0.5)
    inp['l1_hy_short_w'] = nrm((HY_SHORT_K, 3 * HY_W), HY_SHORT_K ** -0.5)
    inp['l1_hy_short_b'] = nrm((3 * HY_W,), 0.02)
    inp['l1_hy_f_w1'] = nrm((HY_EMB, HY_HIDDEN), 1.0)
    inp['l1_hy_f_b1'] = nrm((HY_HIDDEN,), 0.1)
    inp['l1_hy_f_w2'] = nrm((HY_HIDDEN, HY_HIDDEN), HY_HIDDEN ** -0.5)
    inp['l1_hy_f_b2'] = nrm((HY_HIDDEN,), 0.1)
    inp['l1_hy_f_w3'] = nrm((HY_HIDDEN, 2 * HY_W), HY_HIDDEN ** -0.5)
    inp['l1_hy_bias'] = nrm((HY_W,), 1.0)
    inp['l1_w_out'] = nrm((MIX1, D), MIX1 ** -0.5)
    inp['l1_router_w'] = nrm((D, N_EXPERTS), D ** -0.5)
    inp['l1_router_b'] = nrm((N_EXPERTS,), 0.01)
    inp['l1_exp_gate'] = nrm((N_EXPERTS, D, D_FF_EXPERT), D ** -0.5)
    inp['l1_exp_up'] = nrm((N_EXPERTS, D, D_FF_EXPERT), D ** -0.5)
    inp['l1_exp_down'] = nrm((N_EXPERTS, D_FF_EXPERT, D), D_FF_EXPERT ** -0.5)
    inp['final_norm'] = gain(D)
    return inp


def reference(x_prompt, x_sample, state_l0_lru, cache_l1_ckv, cache_l1_krope, c, c_ctx,
              l0_norm1, l0_norm2, l0_w_mod, l0_b_mod, l0_w_in, l0_conv_a,
              l0_lru_conv_w, l0_lru_conv_b, l0_lru_wa, l0_lru_ba, l0_lru_wi, l0_lru_bi,
              l0_lru_lambda, l0_w_out, l0_ffn_gate, l0_ffn_up, l0_ffn_down,
              l1_norm1, l1_norm2, l1_w_mod, l1_b_mod, l1_w_in, l1_q_norm, l1_kv_norm,
              l1_w_q_up, l1_w_kv_up, l1_hy_short_w, l1_hy_short_b, l1_hy_f_w1, l1_hy_f_b1,
              l1_hy_f_w2, l1_hy_f_b2, l1_hy_f_w3, l1_hy_bias, l1_w_out, l1_router_w,
              l1_router_b, l1_exp_gate, l1_exp_up, l1_exp_down, final_norm):
    sub_params = ((l0_norm1, l0_norm2, l0_w_mod, l0_b_mod),
                  (l1_norm1, l1_norm2, l1_w_mod, l1_b_mod))
    mix_params = ((l0_w_in, l0_conv_a, l0_lru_conv_w, l0_lru_conv_b, l0_lru_wa, l0_lru_ba,
                   l0_lru_wi, l0_lru_bi, l0_lru_lambda, l0_w_out),
                  (l1_w_in, l1_q_norm, l1_kv_norm, l1_w_q_up, l1_w_kv_up, l1_hy_short_w,
                   l1_hy_short_b, l1_hy_f_w1, l1_hy_f_b1, l1_hy_f_w2, l1_hy_f_b2, l1_hy_f_w3,
                   l1_hy_bias, l1_w_out))
    ffn_params = ((l0_ffn_gate, l0_ffn_up, l0_ffn_down),
                  (l1_router_w, l1_router_b, l1_exp_gate, l1_exp_up, l1_exp_down))
    cached = (state_l0_lru, (cache_l1_ckv, cache_l1_krope))

    xp, xs = x_prompt, x_sample
    new_lru = new_ckv = new_krope = None
    for layer in range(DEPTH):
        n1, n2, w_mod, b_mod = sub_params[layer]
        pm = adaln(c_ctx[None, :], w_mod, b_mod)
        sm = adaln(c, w_mod, b_mod)
        hp = modulate(rmsnorm(xp, n1), pm[0], pm[1])
        hs = modulate(rmsnorm(xs, n1), sm[0], sm[1])
        if layer % 2 == 0:
            yp, new_lru = even_mixer(hp, None, *mix_params[layer])
            ys, _ = even_mixer(hs, cached[layer], *mix_params[layer])
        else:
            yp, ctx_cache = odd_mixer(hp, None, *mix_params[layer])
            new_ckv, new_krope = ctx_cache
            ys, _ = odd_mixer(hs, cached[layer], *mix_params[layer])
        xp = xp + pm[2][:, None, :] * yp
        xs = xs + sm[2][:, None, :] * ys
        hp = modulate(rmsnorm(xp, n2), pm[3], pm[4])
        hs = modulate(rmsnorm(xs, n2), sm[3], sm[4])
        if layer % 2 == 0:
            fp, fs = swiglu(hp, *ffn_params[layer]), swiglu(hs, *ffn_params[layer])
        else:
            fp, fs = moe_swiglu(hp, *ffn_params[layer]), moe_swiglu(hs, *ffn_params[layer])
        xp = xp + pm[5][:, None, :] * fp
        xs = xs + sm[5][:, None, :] * fs

    y_prompt = rmsnorm(xp, final_norm)
    y_sample = rmsnorm(xs, final_norm)
    return (y_prompt, y_sample, new_lru, new_ckv, new_krope)
```

```python
import functools
import math

import jax
import jax.numpy as jnp
from jax import lax
from jax.experimental import pallas as pl
from jax.experimental.pallas import tpu as pltpu

F32 = jnp.float32
BF16 = jnp.bfloat16
HIGHEST = lax.Precision.HIGHEST

D = 1024
GRID_W = 64
EPS = 1e-6
CONV_W = 512
LRU_W = 1024
LRU_BW = 128
LRU_C = 8.0
MLA_HEADS = 8
Q_RANK = 384
KV_RANK = 256
NOPE = 128
ROPE = 64
VDIM = 128
QK_DIM = NOPE + ROPE
ROPE_THETA = 10000.0
HY_W = 512
HY_BANDS = 16
HY_TARGET = 1e-2
HY_FAST_DECAY = 0.3
HY_SLOW_DECAY = 1.5
D_FF = 2816
N_EXPERTS = 8
D_FF_EXPERT = 1408
IN0 = 3 * CONV_W + 2 * LRU_W
IN1 = Q_RANK + KV_RANK + ROPE + 3 * HY_W

V7X_LANES = 128
V7X_SUBLANES = 8
V7X_VMEM_LIMIT_BYTES = 56 * 1024 * 1024

TM = 512
TN_IN0 = 512
TN_OUT = 512
TF_FFN = 256
MOE_CHUNK = 256
LRU_CB = 256
HY_CB = 256
TQ = 256
TM_IN1 = 256


def _cparams(sem):
    return pltpu.CompilerParams(dimension_semantics=sem, vmem_limit_bytes=V7X_VMEM_LIMIT_BYTES)


def _silu(x):
    return x * jax.nn.sigmoid(x)


def _norm_mod(x, g, shift, scale):
    ms = jnp.mean(x * x, axis=-1, keepdims=True)
    y = x * lax.rsqrt(ms + EPS) * g
    return y * (1.0 + scale) + shift


def _mod_spec(comp, cond, tm, width, col_fn):
    row0, seg = cond
    assert seg % tm == 0
    return pl.BlockSpec((None, 1, width), lambda *ids: (comp * 3 + row0 + (ids[0] * tm) // seg, 0, col_fn(*ids)))


def _adaln_kernel(c_ref, w_ref, b_ref, o_ref):
    a = _silu(c_ref[...])
    o_ref[...] = jnp.dot(a, w_ref[...], preferred_element_type=F32, precision=HIGHEST) + b_ref[...]


def adaln_table(cond8, w_mod, b_mod):
    tn = 1536
    m = pl.pallas_call(
        _adaln_kernel,
        out_shape=jax.ShapeDtypeStruct((V7X_SUBLANES, 6 * D), F32),
        grid=(6 * D // tn,),
        in_specs=[pl.BlockSpec((V7X_SUBLANES, D), lambda j: (0, 0)),
                  pl.BlockSpec((D, tn), lambda j: (0, j)),
                  pl.BlockSpec((1, tn), lambda j: (0, j))],
        out_specs=pl.BlockSpec((V7X_SUBLANES, tn), lambda j: (0, j)),
        compiler_params=_cparams(("arbitrary",)),
        name="adaln",
    )(cond8, w_mod, b_mod.reshape(1, 6 * D))
    return m[:3].reshape(3, 6, D).transpose(1, 0, 2).reshape(18, 1, D)


def _in0_kernel(x_ref, g_ref, sh_ref, sc_ref, w_ref, o_ref, h_sc):
    @pl.when(pl.program_id(1) == 0)
    def _():
        h_sc[...] = _norm_mod(x_ref[...], g_ref[...], sh_ref[...], sc_ref[...]).astype(BF16)

    o_ref[...] = jnp.dot(h_sc[...], w_ref[...].astype(BF16), preferred_element_type=F32)


def in0_proj(x, g, modtab, cond, w_in):
    tn = TN_IN0
    tokens = x.shape[0]
    n = w_in.shape[1]
    zero = lambda i, j: 0
    return pl.pallas_call(
        _in0_kernel,
        out_shape=jax.ShapeDtypeStruct((tokens, n), F32),
        grid=(tokens // TM, n // tn),
        in_specs=[pl.BlockSpec((TM, D), lambda i, j: (i, 0)),
                  pl.BlockSpec((1, D), lambda i, j: (0, 0)),
                  _mod_spec(0, cond, TM, D, zero),
                  _mod_spec(1, cond, TM, D, zero),
                  pl.BlockSpec((D, tn), lambda i, j: (0, j))],
        out_specs=pl.BlockSpec((TM, tn), lambda i, j: (i, j)),
        scratch_shapes=[pltpu.VMEM((TM, D), BF16)],
        compiler_params=_cparams(("parallel", "arbitrary")),
        name="in0_proj",
    )(x, g.reshape(1, D), modtab, modtab, w_in)


def _shift_rows(v, d, t):
    n = v.shape[0]
    if d > 0:
        return jnp.where(t < d, 0.0, pltpu.roll(v, d, 0))
    return jnp.where(t >= n + d, 0.0, pltpu.roll(v, n + d, 0))


def _conv_a_kernel(b_ref, c_ref, x_ref, w_ref, o_ref):
    v = c_ref[...] * x_ref[...]
    t = lax.broadcasted_iota(jnp.int32, v.shape, 0)
    w = w_ref[...]
    y = w[0:1] * _shift_rows(v, 1, t) + w[1:2] * v + w[2:3] * _shift_rows(v, -1, t)
    o_ref[...] = (b_ref[...] * y).astype(o_ref.dtype)


def conv_a(u, seq_len, conv_w):
    tokens = u.shape[0]
    return pl.pallas_call(
        _conv_a_kernel,
        out_shape=jax.ShapeDtypeStruct((tokens, CONV_W), BF16),
        grid=(tokens // seq_len,),
        in_specs=[pl.BlockSpec((seq_len, CONV_W), lambda s: (s, 0)),
                  pl.BlockSpec((seq_len, CONV_W), lambda s: (s, 1)),
                  pl.BlockSpec((seq_len, CONV_W), lambda s: (s, 2)),
                  pl.BlockSpec((3, CONV_W), lambda s: (0, 0))],
        out_specs=pl.BlockSpec((seq_len, CONV_W), lambda s: (s, 0)),
        compiler_params=_cparams(("parallel",)),
        name="conv_a",
    )(u, u, u, conv_w)


def _group_scan(a, b, reverse):
    n, c = a.shape
    a3 = a.reshape(n // V7X_SUBLANES, V7X_SUBLANES, c)
    b3 = b.reshape(n // V7X_SUBLANES, V7X_SUBLANES, c)
    t8 = lax.broadcasted_iota(jnp.int32, a3.shape, 1)
    for d in (1, 2, 4):
        if reverse:
            keep = t8 < V7X_SUBLANES - d
            shift = V7X_SUBLANES - d
        else:
            keep = t8 >= d
            shift = d
        a_sh = jnp.where(keep, pltpu.roll(a3, shift, 1), 1.0)
        b_sh = jnp.where(keep, pltpu.roll(b3, shift, 1), 0.0)
        b3 = a3 * b_sh + b3
        a3 = a3 * a_sh
    return a3.reshape(n, c), b3.reshape(n, c)


def _rglru_kernel(gate_ref, xb_ref, cw_ref, cb_ref, wcat_ref, ba_ref, bi_ref, lam_ref, h0_ref,
                  y_ref, st_ref, af_sc, bf_sc, ab_sc, bb_sc, hf_sc, hb_sc):
    n, cb = xb_ref.shape
    xb = xb_ref[...]
    t = lax.broadcasted_iota(jnp.int32, xb.shape, 0)
    cw = cw_ref[...]
    xc = (cb_ref[...] + cw[0:1] * _shift_rows(xb, 2, t) + cw[1:2] * _shift_rows(xb, 1, t)
          + cw[2:3] * xb + cw[3:4] * _shift_rows(xb, -1, t))
    xcb = xc.astype(BF16)
    g = [jnp.dot(xcb[:, k * LRU_BW:(k + 1) * LRU_BW], wcat_ref[k].astype(BF16), preferred_element_type=F32)
         for k in range(cb // LRU_BW)]

    def direction(d):
        ga = jnp.concatenate([gk[:, (2 * d) * LRU_BW:(2 * d + 1) * LRU_BW] for gk in g], axis=1)
        gi = jnp.concatenate([gk[:, (2 * d + 1) * LRU_BW:(2 * d + 2) * LRU_BW] for gk in g], axis=1)
        r = jax.nn.sigmoid(ga + ba_ref[d:d + 1, :])
        i = jax.nn.sigmoid(gi + bi_ref[d:d + 1, :])
        log_a = (-LRU_C * jax.nn.softplus(-lam_ref[d:d + 1, :])) * r
        a = jnp.exp(log_a)
        mult = jnp.sqrt(1.0 - a * a)
        return a, mult * (i * xc)

    a_f, b_f = direction(0)
    a_f, b_f = _group_scan(a_f, b_f, reverse=False)
    af_sc[...] = a_f
    bf_sc[...] = b_f
    a_b, b_b = direction(1)
    a_b, b_b = _group_scan(a_b, b_b, reverse=True)
    ab_sc[...] = a_b
    bb_sc[...] = b_b

    ng = n // V7X_SUBLANES
    h0 = h0_ref[...]
    init = (jnp.broadcast_to(h0[0:1], (V7X_SUBLANES, cb)), jnp.broadcast_to(h0[1:2], (V7X_SUBLANES, cb)))

    def step(k, carry):
        hf_in, hb_in = carry
        rf = pl.multiple_of(k * V7X_SUBLANES, V7X_SUBLANES)
        rb = pl.multiple_of((ng - 1 - k) * V7X_SUBLANES, V7X_SUBLANES)
        hf = af_sc[pl.ds(rf, V7X_SUBLANES), :] * hf_in + bf_sc[pl.ds(rf, V7X_SUBLANES), :]
        hb = ab_sc[pl.ds(rb, V7X_SUBLANES), :] * hb_in + bb_sc[pl.ds(rb, V7X_SUBLANES), :]
        hf_sc[pl.ds(rf, V7X_SUBLANES), :] = hf
        hb_sc[pl.ds(rb, V7X_SUBLANES), :] = hb
        return (jnp.broadcast_to(hf[V7X_SUBLANES - 1:V7X_SUBLANES], hf.shape), jnp.broadcast_to(hb[0:1], hb.shape))

    hf_last, hb_first = lax.fori_loop(0, ng, step, init)
    st_ref[0:1, :] = hf_last[0:1]
    st_ref[1:2, :] = hb_first[0:1]

    gt = gate_ref[...]
    gelu = 0.5 * gt * (1.0 + jnp.tanh(math.sqrt(2.0 / math.pi) * (gt + 0.044715 * (gt * gt * gt))))
    y_ref[...] = ((hf_sc[...] + hb_sc[...]) * gelu).astype(y_ref.dtype)


def rglru(u, seq_len, conv_w, conv_b, wcat, ba, bi, lam, h0):
    tokens = u.shape[0]
    nseq = tokens // seq_len
    cb = LRU_CB
    gate_blk0 = 3 * CONV_W // cb
    xb_blk0 = (3 * CONV_W + LRU_W) // cb
    seq_scr = lambda: pltpu.VMEM((seq_len, cb), F32)
    return pl.pallas_call(
        _rglru_kernel,
        out_shape=(jax.ShapeDtypeStruct((tokens, LRU_W), BF16), jax.ShapeDtypeStruct((nseq, 2, LRU_W), F32)),
        grid=(nseq, LRU_W // cb),
        in_specs=[pl.BlockSpec((seq_len, cb), lambda s, c: (s, gate_blk0 + c)),
                  pl.BlockSpec((seq_len, cb), lambda s, c: (s, xb_blk0 + c)),
                  pl.BlockSpec((4, cb), lambda s, c: (0, c)),
                  pl.BlockSpec((1, cb), lambda s, c: (0, c)),
                  pl.BlockSpec((cb // LRU_BW, LRU_BW, 4 * LRU_BW), lambda s, c: (c, 0, 0)),
                  pl.BlockSpec((2, cb), lambda s, c: (0, c)),
                  pl.BlockSpec((2, cb), lambda s, c: (0, c)),
                  pl.BlockSpec((2, cb), lambda s, c: (0, c)),
                  pl.BlockSpec((None, 2, cb), lambda s, c: (s, 0, c))],
        out_specs=(pl.BlockSpec((seq_len, cb), lambda s, c: (s, c)),
                   pl.BlockSpec((None, 2, cb), lambda s, c: (s, 0, c))),
        scratch_shapes=[seq_scr() for _ in range(6)],
        compiler_params=_cparams(("parallel", "parallel")),
        name="rglru",
    )(u, u, conv_w, conv_b.reshape(1, LRU_W), wcat, ba, bi, lam, h0)


def _out_res_kernel(p0_ref, p1_ref, p2_ref, w0_ref, w1_ref, w2_ref, x_ref, gt_ref, o_ref):
    y = jnp.dot(p0_ref[...], w0_ref[...].astype(BF16), preferred_element_type=F32)
    y += jnp.dot(p1_ref[...], w1_ref[...].astype(BF16), preferred_element_type=F32)
    y += jnp.dot(p2_ref[...], w2_ref[...].astype(BF16), preferred_element_type=F32)
    o_ref[...] = x_ref[...] + gt_ref[...] * y


def out_res(parts, w_out, x, modtab, cond, gate_comp):
    tokens = x.shape[0]
    tn = TN_OUT
    kb = 512
    lhs_specs = [pl.BlockSpec((TM, kb), (lambda i, j, cbk=cbk: (i, cbk))) for _, cbk in parts]
    w_specs = [pl.BlockSpec((kb, tn), (lambda i, j, r=r: (r, j))) for r in range(3)]
    return pl.pallas_call(
        _out_res_kernel,
        out_shape=jax.ShapeDtypeStruct((tokens, D), F32),
        grid=(tokens // TM, D // tn),
        in_specs=lhs_specs + w_specs + [pl.BlockSpec((TM, tn), lambda i, j: (i, j)),
                                        _mod_spec(gate_comp, cond, TM, tn, lambda i, j: j)],
        out_specs=pl.BlockSpec((TM, tn), lambda i, j: (i, j)),
        compiler_params=_cparams(("parallel", "parallel")),
        name="out_res",
    )(*[a for a, _ in parts], w_out, w_out, w_out, x, modtab)


def _ffn_kernel(x_ref, g_ref, sh_ref, sc_ref, gt_ref, wg_ref, wu_ref, wd_ref, o_ref,
                h_sc, acc_sc, wg_sc, wu_sc, wd_sc):
    i = pl.program_id(0)
    f = pl.program_id(1)

    @pl.when(i == 0)
    def _():
        wg_sc[f] = wg_ref[...].astype(BF16)
        wu_sc[f] = wu_ref[...].astype(BF16)
        wd_sc[f] = wd_ref[...].astype(BF16)

    @pl.when(f == 0)
    def _():
        h_sc[...] = _norm_mod(x_ref[...], g_ref[...], sh_ref[...], sc_ref[...]).astype(BF16)
        acc_sc[...] = jnp.zeros_like(acc_sc)

    h = h_sc[...]
    hg = jnp.dot(h, wg_sc[f], preferred_element_type=F32)
    hu = jnp.dot(h, wu_sc[f], preferred_element_type=F32)
    act = (_silu(hg) * hu).astype(BF16)
    acc_sc[...] += jnp.dot(act, wd_sc[f], preferred_element_type=F32)

    @pl.when(f == pl.num_programs(1) - 1)
    def _():
        o_ref[...] = x_ref[...] + gt_ref[...] * acc_sc[...]


def ffn_res(x, g, modtab, cond, w_gate, w_up, w_down):
    tokens = x.shape[0]
    tf = TF_FFN
    nf = D_FF // tf
    zero = lambda i, f: 0
    wcol = lambda i, f: (0, jnp.where(i == 0, f, nf - 1))
    wrow = lambda i, f: (jnp.where(i == 0, f, nf - 1), 0)
    return pl.pallas_call(
        _ffn_kernel,
        out_shape=jax.ShapeDtypeStruct((tokens, D), F32),
        grid=(tokens // TM, nf),
        in_specs=[pl.BlockSpec((TM, D), lambda i, f: (i, 0)),
                  pl.BlockSpec((1, D), lambda i, f: (0, 0)),
                  _mod_spec(3, cond, TM, D, zero),
                  _mod_spec(4, cond, TM, D, zero),
                  _mod_spec(5, cond, TM, D, zero),
                  pl.BlockSpec((D, tf), wcol),
                  pl.BlockSpec((D, tf), wcol),
                  pl.BlockSpec((tf, D), wrow)],
        out_specs=pl.BlockSpec((TM, D), lambda i, f: (i, 0)),
        scratch_shapes=[pltpu.VMEM((TM, D), BF16), pltpu.VMEM((TM, D), F32),
                        pltpu.VMEM((nf, D, tf), BF16), pltpu.VMEM((nf, D, tf), BF16), pltpu.VMEM((nf, tf, D), BF16)],
        compiler_params=_cparams(("arbitrary", "arbitrary")),
        name="ffn_res",
    )(x, g.reshape(1, D), modtab, modtab, modtab, w_gate, w_up, w_down)


def _rms(x, g):
    return x * lax.rsqrt(jnp.mean(x * x, axis=-1, keepdims=True) + EPS) * g


def _in1_kernel(x_ref, g_ref, sh_ref, sc_ref, w_ref, qn_ref, kvn_ref, wq_ref, wkv_ref,
                qnope_ref, qpe_ref, ckv_ref, kr_ref, kv_ref, uh_ref, w_sc, wq_sc, wkv_sc):
    @pl.when(pl.program_id(0) == 0)
    def _():
        w_sc[...] = w_ref[...].astype(BF16)
        wq_sc[...] = wq_ref[...].astype(BF16)
        wkv_sc[...] = wkv_ref[...].astype(BF16)

    h = _norm_mod(x_ref[...], g_ref[...], sh_ref[...], sc_ref[...]).astype(BF16)
    u = jnp.dot(h, w_sc[...], preferred_element_type=F32)
    o1, o2, o3 = Q_RANK, Q_RANK + KV_RANK, Q_RANK + KV_RANK + ROPE
    cq = _rms(u[:, :o1], qn_ref[...])
    q = jnp.dot(cq.astype(BF16), wq_sc[...], preferred_element_type=F32)
    qnope_ref[...] = q[:, :MLA_HEADS * NOPE].astype(qnope_ref.dtype)
    qpe_ref[...] = q[:, MLA_HEADS * NOPE:]
    ckv = _rms(u[:, o1:o2], kvn_ref[...])
    ckv_ref[...] = ckv
    kv_ref[...] = jnp.dot(ckv.astype(BF16), wkv_sc[...], preferred_element_type=F32).astype(kv_ref.dtype)
    kr_ref[...] = u[:, o2:o3]
    uh_ref[...] = u[:, o3:]


def in1_proj(x, g, modtab, cond, w_in, q_norm, kv_norm, wq_perm, w_kv_up):
    tokens = x.shape[0]
    tm = TM_IN1
    nkv = MLA_HEADS * (NOPE + VDIM)
    const = lambda i: (0, 0)
    zero = lambda i: 0
    once = pl.Buffered(1)
    outs = (jax.ShapeDtypeStruct((tokens, MLA_HEADS * NOPE), BF16),
            jax.ShapeDtypeStruct((tokens, MLA_HEADS * ROPE), F32),
            jax.ShapeDtypeStruct((tokens, KV_RANK), F32),
            jax.ShapeDtypeStruct((tokens, ROPE), F32),
            jax.ShapeDtypeStruct((tokens, nkv), BF16),
            jax.ShapeDtypeStruct((tokens, 3 * HY_W), F32))
    row = lambda w: pl.BlockSpec((tm, w), lambda i: (i, 0))
    return pl.pallas_call(
        _in1_kernel,
        out_shape=outs,
        grid=(tokens // tm,),
        in_specs=[row(D),
                  pl.BlockSpec((1, D), const),
                  _mod_spec(0, cond, tm, D, zero),
                  _mod_spec(1, cond, tm, D, zero),
                  pl.BlockSpec((D, IN1), const, pipeline_mode=once),
                  pl.BlockSpec((1, Q_RANK), const),
                  pl.BlockSpec((1, KV_RANK), const),
                  pl.BlockSpec((Q_RANK, MLA_HEADS * QK_DIM), const, pipeline_mode=once),
                  pl.BlockSpec((KV_RANK, nkv), const, pipeline_mode=once)],
        out_specs=tuple(row(o.shape[1]) for o in outs),
        scratch_shapes=[pltpu.VMEM((D, IN1), BF16), pltpu.VMEM((Q_RANK, MLA_HEADS * QK_DIM), BF16),
                        pltpu.VMEM((KV_RANK, nkv), BF16)],
        compiler_params=_cparams(("arbitrary",)),
        name="in1_proj",
    )(x, g.reshape(1, D), modtab, modtab, w_in, q_norm.reshape(1, Q_RANK), kv_norm.reshape(1, KV_RANK),
      wq_perm, w_kv_up)


def _mm_kernel(a_ref, w_ref, o_ref):
    o_ref[...] = jnp.dot(a_ref[...].astype(BF16), w_ref[...].astype(BF16),
                         preferred_element_type=F32).astype(o_ref.dtype)


def kv_up(ckv, w_kv_up):
    rows = ckv.shape[0]
    n = w_kv_up.shape[1]
    return pl.pallas_call(
        _mm_kernel,
        out_shape=jax.ShapeDtypeStruct((rows, n), BF16),
        grid=(rows // TM,),
        in_specs=[pl.BlockSpec((TM, KV_RANK), lambda i: (i, 0)), pl.BlockSpec((KV_RANK, n), lambda i: (0, 0))],
        out_specs=pl.BlockSpec((TM, n), lambda i: (i, 0)),
        compiler_params=_cparams(("parallel",)),
        name="kv_up",
    )(ckv, w_kv_up)


_NT = (((1,), (1,)), ((), ()))
_SCALE = 1.0 / math.sqrt(QK_DIM)


def _rope_tables(pos):
    n = pos.shape[0]
    lane = lax.broadcasted_iota(jnp.int32, (n, ROPE), 1)
    j = lane & (ROPE // 2 - 1)
    n_freq = ROPE // 4
    inv = jnp.exp((j & (n_freq - 1)).astype(F32) * (-math.log(ROPE_THETA) / n_freq))
    p = jnp.where(j < n_freq, pos >> (GRID_W.bit_length() - 1), pos & (GRID_W - 1)).astype(F32)
    ang = p * inv
    return jnp.cos(ang), jnp.sin(ang)


def _rot_half_matrix():
    i = lax.broadcasted_iota(jnp.int32, (ROPE, ROPE), 0)
    j = lax.broadcasted_iota(jnp.int32, (ROPE, ROPE), 1)
    half = ROPE // 2
    return jnp.where(i == j + half, -1.0, jnp.where(i + half == j, 1.0, 0.0)).astype(F32)


def _rope(x, cos, sin, rot):
    xr = jnp.dot(x, rot, preferred_element_type=F32, precision=HIGHEST)
    return x * cos + xr * sin


def _attend_heads(qn_ref, qpe_fn, kv_refs, kpe_list, o_ref, q_rows):
    for h in range(MLA_HEADS):
        qn = qn_ref[q_rows, h * NOPE:(h + 1) * NOPE]
        qp = qpe_fn(h)
        s = []
        for kv_ref, kpe in zip(kv_refs, kpe_list):
            kn = kv_ref[:, h * (NOPE + VDIM):h * (NOPE + VDIM) + NOPE]
            sk = lax.dot_general(qn, kn, _NT, preferred_element_type=F32)
            sk += lax.dot_general(qp, kpe, _NT, preferred_element_type=F32)
            s.append(sk * _SCALE)
        m = functools.reduce(jnp.maximum, [jnp.max(sk, axis=-1, keepdims=True) for sk in s])
        p = [jnp.exp(sk - m) for sk in s]
        l = functools.reduce(jnp.add, [jnp.sum(pk, axis=-1, keepdims=True) for pk in p])
        o = None
        for kv_ref, pk in zip(kv_refs, p):
            v = kv_ref[:, h * (NOPE + VDIM) + NOPE:(h + 1) * (NOPE + VDIM)]
            ok = jnp.dot(pk.astype(BF16), v, preferred_element_type=F32)
            o = ok if o is None else o + ok
        o_ref[q_rows, h * VDIM:(h + 1) * VDIM] = (o / l).astype(o_ref.dtype)


def _attn_ctx_kernel(qn_ref, qpe_ref, kv_ref, kr_ref, o_ref):
    kpe = kr_ref[...].astype(BF16)
    rows = slice(None)
    _attend_heads(qn_ref, lambda h: qpe_ref[:, h * ROPE:(h + 1) * ROPE].astype(BF16), [kv_ref], [kpe], o_ref, rows)


def attn_ctx(qnope, qpe, kv, kr, seq_len):
    tokens = qnope.shape[0]
    blk = lambda w: pl.BlockSpec((seq_len, w), lambda s: (s, 0))
    return pl.pallas_call(
        _attn_ctx_kernel,
        out_shape=jax.ShapeDtypeStruct((tokens, MLA_HEADS * VDIM), BF16),
        grid=(tokens // seq_len,),
        in_specs=[blk(MLA_HEADS * NOPE), blk(MLA_HEADS * ROPE), blk(MLA_HEADS * (NOPE + VDIM)), blk(ROPE)],
        out_specs=blk(MLA_HEADS * VDIM),
        compiler_params=_cparams(("parallel",)),
        name="attn_ctx",
    )(qnope, qpe, kv, kr)


def _attn_lat_kernel(qn_ref, qpe_ref, kvc_ref, krc_ref, kvl_ref, krl_ref, o_ref):
    tq = qn_ref.shape[0]
    n_lat = krl_ref.shape[0]
    rot = _rot_half_matrix()
    q0 = pl.program_id(1) * tq
    cq, sq = _rope_tables(q0 + lax.broadcasted_iota(jnp.int32, (tq, 1), 0))
    ck, sk = _rope_tables(lax.broadcasted_iota(jnp.int32, (n_lat, 1), 0))
    kpe_lat = _rope(krl_ref[...], ck, sk, rot).astype(BF16)
    kpe_ctx = krc_ref[...].astype(BF16)

    def qpe(h):
        return _rope(qpe_ref[:, h * ROPE:(h + 1) * ROPE], cq, sq, rot).astype(BF16)

    _attend_heads(qn_ref, qpe, [kvc_ref, kvl_ref], [kpe_ctx, kpe_lat], o_ref, slice(None))


def attn_lat(qnope, qpe, kv_ctx, kr_ctx, kv_lat, kr_lat, seq_len, ctx_len):
    tokens = qnope.shape[0]
    nq = seq_len // TQ
    qblk = lambda w: pl.BlockSpec((TQ, w), lambda b, i: (b * nq + i, 0))
    seq = lambda n, w: pl.BlockSpec((n, w), lambda b, i: (b, 0))
    nkv = MLA_HEADS * (NOPE + VDIM)
    return pl.pallas_call(
        _attn_lat_kernel,
        out_shape=jax.ShapeDtypeStruct((tokens, MLA_HEADS * VDIM), BF16),
        grid=(tokens // seq_len, nq),
        in_specs=[qblk(MLA_HEADS * NOPE), qblk(MLA_HEADS * ROPE), seq(ctx_len, nkv), seq(ctx_len, ROPE),
                  seq(seq_len, nkv), seq(seq_len, ROPE)],
        out_specs=qblk(MLA_HEADS * VDIM),
        compiler_params=_cparams(("parallel", "parallel")),
        name="attn_lat",
    )(qnope, qpe, kv_ctx, kr_ctx, kv_lat, kr_lat)


def _dft_kernel(o_ref):
    tr, n = o_ref.shape[1], o_ref.shape[2]
    f = pl.program_id(0) * tr + lax.broadcasted_iota(jnp.int32, (tr, n), 0)
    s = lax.broadcasted_iota(jnp.int32, (tr, n), 1)
    ang = ((f * s) & (2 * n - 1)).astype(F32) * (math.pi / n)
    o_ref[0] = jnp.cos(ang).astype(o_ref.dtype)
    o_ref[1] = jnp.sin(ang).astype(o_ref.dtype)


def dft_tables(n):
    tr = 128
    return pl.pallas_call(
        _dft_kernel,
        out_shape=jax.ShapeDtypeStruct((2, n, n), BF16),
        grid=(n // tr,),
        out_specs=pl.BlockSpec((2, tr, n), lambda i: (0, i, 0)),
        compiler_params=_cparams(("parallel",)),
        name="dft_tables",
    )()


def _split_dot(table, x):
    hi = x.astype(BF16)
    lo = (x - hi.astype(F32)).astype(BF16)
    return (jnp.dot(table, hi, preferred_element_type=F32) + jnp.dot(table, lo, preferred_element_type=F32))


def _hy_filter_kernel(cs_ref, w1_ref, b1_ref, w2_ref, b2_ref, w3_ref, kr_ref, ks_ref, kny_ref):
    n = cs_ref.shape[1]
    row = lax.broadcasted_iota(jnp.int32, (n, V7X_LANES), 0).astype(F32)
    lane = lax.broadcasted_iota(jnp.int32, (n, V7X_LANES), 1)
    t = row * (1.0 / (n - 1))
    w = (2.0 * math.pi) * row / n
    band = jnp.where(lane <= HY_BANDS, lane - 1, lane - 1 - HY_BANDS).astype(F32)
    freq = 1e-4 + band * ((HY_BANDS - 1 - 1e-4) / (HY_BANDS - 1))
    z = jnp.where(lane == 0, t,
                  jnp.where(lane <= HY_BANDS, jnp.cos(freq * w),
                            jnp.where(lane <= 2 * HY_BANDS, -jnp.sin(freq * w), 0.0)))
    hid = jnp.sin(jnp.dot(z, w1_ref[...], preferred_element_type=F32, precision=HIGHEST) + b1_ref[...])
    hid = jnp.sin(jnp.dot(hid, w2_ref[...], preferred_element_type=F32, precision=HIGHEST) + b2_ref[...])
    hf = jnp.dot(hid, w3_ref[...], preferred_element_type=F32, precision=HIGHEST)

    rowc = lax.broadcasted_iota(jnp.int32, (n, HY_W), 0)
    chan = lax.broadcasted_iota(jnp.int32, (n, HY_W), 1).astype(F32)
    max_decay = math.log(HY_TARGET) / HY_FAST_DECAY
    min_decay = math.log(HY_TARGET) / HY_SLOW_DECAY
    deltas = min_decay + chan * ((max_decay - min_decay) / (HY_W - 1))
    decay = jnp.exp(-(rowc.astype(F32) * (1.0 / (n - 1))) * jnp.abs(deltas))
    h_fwd = hf[:, :HY_W] * decay
    h_bwd = jnp.where(rowc == 0, 0.0, hf[:, HY_W:] * decay)
    norm = jnp.sum(jnp.abs(h_fwd) + jnp.abs(h_bwd), axis=0, keepdims=True)
    even = (h_fwd + h_bwd) / norm
    odd = (h_fwd - h_bwd) / norm
    cf = jnp.where(rowc == 0, 1.0, 2.0) * (1.0 / (2 * n))
    kr_ref[...] = cf * _split_dot(cs_ref[0], even)
    ks_ref[...] = cf * _split_dot(cs_ref[1], odd)
    sgn = jnp.where((rowc & 1) == 1, -1.0, 1.0)
    kny_ref[...] = jnp.sum(sgn * even, axis=0, keepdims=True) * (1.0 / (2 * n))


def hy_filter(cs, w1p, b1p, w2p, b2p, w3p):
    n = cs.shape[1]
    full = lambda a: pl.BlockSpec(a.shape, lambda: (0,) * a.ndim)
    args = (cs, w1p, b1p, w2p, b2p, w3p)
    return pl.pallas_call(
        _hy_filter_kernel,
        out_shape=(jax.ShapeDtypeStruct((n, HY_W), F32), jax.ShapeDtypeStruct((n, HY_W), F32),
                   jax.ShapeDtypeStruct((1, HY_W), F32)),
        in_specs=[full(a) for a in args],
        out_specs=(pl.BlockSpec((n, HY_W), lambda: (0, 0)), pl.BlockSpec((n, HY_W), lambda: (0, 0)),
                   pl.BlockSpec((1, HY_W), lambda: (0, 0))),
        compiler_params=pltpu.CompilerParams(vmem_limit_bytes=V7X_VMEM_LIMIT_BYTES),
        name="hy_filter",
    )(*args)


def _hyena_kernel(u0_ref, u1_ref, u2_ref, sw_ref, sb_ref, cs_ref, kr_ref, ks_ref, kny_ref, bias_ref, o_ref):
    n, cb = u0_ref.shape
    t = lax.broadcasted_iota(jnp.int32, (n, cb), 0)

    def short_conv(u_ref, k):
        u = u_ref[...]
        w = sw_ref[:, k * cb:(k + 1) * cb]
        return (sb_ref[:, k * cb:(k + 1) * cb] + w[0:1] * _shift_rows(u, 1, t) + w[1:2] * u
                + w[2:3] * _shift_rows(u, -1, t))

    x0 = short_conv(u0_ref, 0)
    z = short_conv(u1_ref, 1) * short_conv(u2_ref, 2)
    zb = z.astype(BF16)
    c, s = cs_ref[0], cs_ref[1]
    ur = jnp.dot(c, zb, preferred_element_type=F32)
    us = jnp.dot(s, zb, preferred_element_type=F32)
    sgn = jnp.where((t & 1) == 1, -1.0, 1.0)
    uny = jnp.sum(sgn * z, axis=0, keepdims=True)
    kr, ks = kr_ref[...], ks_ref[...]
    yr = (ur * kr - us * ks).astype(BF16)
    ys = (ur * ks + us * kr).astype(BF16)
    y = jnp.dot(c, yr, preferred_element_type=F32) + jnp.dot(s, ys, preferred_element_type=F32)
    y = y + sgn * (uny * kny_ref[...])
    o_ref[...] = (x0 * (y + bias_ref[...] * z)).astype(o_ref.dtype)


def hyena(uh, seq_len, short_w, short_b, cs, kr, ks, kny, bias):
    tokens = uh.shape[0]
    cb = HY_CB
    nc = HY_W // cb
    ublk = lambda k: pl.BlockSpec((seq_len, cb), lambda s, c: (s, k * nc + c))
    chan = lambda rows: pl.BlockSpec((rows, cb), lambda s, c: (0, c))
    return pl.pallas_call(
        _hyena_kernel,
        out_shape=jax.ShapeDtypeStruct((tokens, HY_W), BF16),
        grid=(tokens // seq_len, nc),
        in_specs=[ublk(0), ublk(1), ublk(2),
                  pl.BlockSpec((None, 3, 3 * cb), lambda s, c: (c, 0, 0)),
                  pl.BlockSpec((None, 1, 3 * cb), lambda s, c: (c, 0, 0)),
                  pl.BlockSpec((2, seq_len, seq_len), lambda s, c: (0, 0, 0)),
                  chan(seq_len), chan(seq_len), chan(1), chan(1)],
        out_specs=pl.BlockSpec((seq_len, cb), lambda s, c: (s, c)),
        compiler_params=_cparams(("parallel", "parallel")),
        name="hyena",
    )(uh, uh, uh, short_w, short_b, cs, kr, ks, kny, bias)


def _moe_kernel(x_ref, g_ref, sh_ref, sc_ref, gt_ref, wr_ref, br_ref, wg_ref, wu_ref, wd_ref, o_ref,
                h_sc, acc_sc, gates_sc):
    e = pl.program_id(1)
    tm = x_ref.shape[0]
    lane = lax.broadcasted_iota(jnp.int32, (tm, V7X_LANES), 1)

    @pl.when(e == 0)
    def _():
        h = _norm_mod(x_ref[...], g_ref[...], sh_ref[...], sc_ref[...])
        h_sc[...] = h.astype(BF16)
        logits = jnp.dot(h, wr_ref[...], preferred_element_type=F32, precision=HIGHEST) + br_ref[...]
        valid = lane < N_EXPERTS
        lg = jnp.where(valid, logits, -jnp.inf)
        ex = jnp.exp(lg - jnp.max(lg, axis=-1, keepdims=True))
        p = ex / jnp.sum(ex, axis=-1, keepdims=True)
        p1 = jnp.max(p, axis=-1, keepdims=True)
        i1 = jnp.min(jnp.where((p == p1) & valid, lane, V7X_LANES), axis=-1, keepdims=True)
        rest = jnp.where((lane == i1) | (~valid), -1.0, p)
        p2 = jnp.max(rest, axis=-1, keepdims=True)
        i2 = jnp.min(jnp.where(rest == p2, lane, V7X_LANES), axis=-1, keepdims=True)
        gates_sc[...] = jnp.where(lane == i1, p1, jnp.where(lane == i2, p2, 0.0)) / (p1 + p2)
        acc_sc[...] = jnp.zeros_like(acc_sc)

    gate_e = jnp.sum(jnp.where(lane == e, gates_sc[...], 0.0), axis=-1, keepdims=True)
    h = h_sc[...]
    y = None
    for c0 in range(0, D_FF_EXPERT, MOE_CHUNK):
        c1 = min(c0 + MOE_CHUNK, D_FF_EXPERT)
        hg = jnp.dot(h, wg_ref[:, c0:c1].astype(BF16), preferred_element_type=F32)
        hu = jnp.dot(h, wu_ref[:, c0:c1].astype(BF16), preferred_element_type=F32)
        act = (_silu(hg) * hu * gate_e).astype(BF16)
        yc = jnp.dot(act, wd_ref[c0:c1, :].astype(BF16), preferred_element_type=F32)
        y = yc if y is None else y + yc
    acc_sc[...] += y

    @pl.when(e == pl.num_programs(1) - 1)
    def _():
        o_ref[...] = x_ref[...] + gt_ref[...] * acc_sc[...]


def moe_res(x, g, modtab, cond, wr_pad, br_pad, e_gate, e_up, e_down):
    tokens = x.shape[0]
    zero = lambda i, e: 0
    return pl.pallas_call(
        _moe_kernel,
        out_shape=jax.ShapeDtypeStruct((tokens, D), F32),
        grid=(tokens // TM, N_EXPERTS),
        in_specs=[pl.BlockSpec((TM, D), lambda i, e: (i, 0)),
                  pl.BlockSpec((1, D), lambda i, e: (0, 0)),
                  _mod_spec(3, cond, TM, D, zero),
                  _mod_spec(4, cond, TM, D, zero),
                  _mod_spec(5, cond, TM, D, zero),
                  pl.BlockSpec((D, V7X_LANES), lambda i, e: (0, 0)),
                  pl.BlockSpec((1, V7X_LANES), lambda i, e: (0, 0)),
                  pl.BlockSpec((None, D, D_FF_EXPERT), lambda i, e: (e, 0, 0)),
                  pl.BlockSpec((None, D, D_FF_EXPERT), lambda i, e: (e, 0, 0)),
                  pl.BlockSpec((None, D_FF_EXPERT, D), lambda i, e: (e, 0, 0))],
        out_specs=pl.BlockSpec((TM, D), lambda i, e: (i, 0)),
        scratch_shapes=[pltpu.VMEM((TM, D), BF16), pltpu.VMEM((TM, D), F32), pltpu.VMEM((TM, V7X_LANES), F32)],
        compiler_params=_cparams(("parallel", "arbitrary")),
        name="moe_res",
    )(x, g.reshape(1, D), modtab, modtab, modtab, wr_pad, br_pad, e_gate, e_up, e_down)


def _final_norm_kernel(x_ref, g_ref, o_ref):
    o_ref[...] = _rms(x_ref[...], g_ref[...])


def final_rmsnorm(x, g):
    tokens = x.shape[0]
    return pl.pallas_call(
        _final_norm_kernel,
        out_shape=jax.ShapeDtypeStruct((tokens, D), F32),
        grid=(tokens // TM,),
        in_specs=[pl.BlockSpec((TM, D), lambda i: (i, 0)), pl.BlockSpec((1, D), lambda i: (0, 0))],
        out_specs=pl.BlockSpec((TM, D), lambda i: (i, 0)),
        compiler_params=_cparams(("parallel",)),
        name="final_norm",
    )(x, g.reshape(1, D))


def _pad_to(a, shape):
    return jnp.pad(a, [(0, t - s) for s, t in zip(a.shape, shape)])


def _regroup_chunks(a, cb):
    r = a.shape[0]
    return a.reshape(r, 3, HY_W // cb, cb).transpose(2, 0, 1, 3).reshape(HY_W // cb, r, 3 * cb)


def kernel(x_prompt, x_sample, state_l0_lru, cache_l1_ckv, cache_l1_krope, c, c_ctx, l0_norm1, l0_norm2, l0_w_mod, l0_b_mod, l0_w_in, l0_conv_a, l0_lru_conv_w, l0_lru_conv_b, l0_lru_wa, l0_lru_ba, l0_lru_wi, l0_lru_bi, l0_lru_lambda, l0_w_out, l0_ffn_gate, l0_ffn_up, l0_ffn_down, l1_norm1, l1_norm2, l1_w_mod, l1_b_mod, l1_w_in, l1_q_norm, l1_kv_norm, l1_w_q_up, l1_w_kv_up, l1_hy_short_w, l1_hy_short_b, l1_hy_f_w1, l1_hy_f_b1, l1_hy_f_w2, l1_hy_f_b2, l1_hy_f_w3, l1_hy_bias, l1_w_out, l1_router_w, l1_router_b, l1_exp_gate, l1_exp_up, l1_exp_down, final_norm):
    batch, seq, _ = x_prompt.shape
    dec_batch, dec_seq, _ = x_sample.shape
    past_len = cache_l1_ckv.shape[1]

    cond8 = jnp.concatenate([c_ctx[None, :], c, jnp.zeros((V7X_SUBLANES - 1 - dec_batch, D), F32)], axis=0)
    wcat = jnp.concatenate([l0_lru_wa[0], l0_lru_wi[0], l0_lru_wa[1], l0_lru_wi[1]], axis=-1)
    wq = l1_w_q_up.reshape(Q_RANK, MLA_HEADS, QK_DIM)
    wq_perm = jnp.concatenate([wq[:, :, :NOPE].reshape(Q_RANK, MLA_HEADS * NOPE),
                               wq[:, :, NOPE:].reshape(Q_RANK, MLA_HEADS * ROPE)], axis=1)
    hid = V7X_LANES
    w1p = _pad_to(l1_hy_f_w1, (hid, hid))
    b1p = _pad_to(l1_hy_f_b1.reshape(1, -1), (1, hid))
    w2p = _pad_to(l1_hy_f_w2, (hid, hid))
    b2p = _pad_to(l1_hy_f_b2.reshape(1, -1), (1, hid))
    w3p = _pad_to(l1_hy_f_w3, (hid, 2 * HY_W))
    short_w = _regroup_chunks(l1_hy_short_w, HY_CB)
    short_b = _regroup_chunks(l1_hy_short_b.reshape(1, -1), HY_CB)
    hy_bias = l1_hy_bias.reshape(1, HY_W)
    wr_pad = _pad_to(l1_router_w, (D, V7X_LANES))
    br_pad = _pad_to(l1_router_b.reshape(1, -1), (1, V7X_LANES))

    mod0 = adaln_table(cond8, l0_w_mod, l0_b_mod)
    mod1 = adaln_table(cond8, l1_w_mod, l1_b_mod)

    kv_ctx = kv_up(cache_l1_ckv.reshape(dec_batch * past_len, KV_RANK), l1_w_kv_up)
    kr_ctx = cache_l1_krope.reshape(dec_batch * past_len, ROPE)

    def trunk(x, seq_len, cond, h0, latent):
        u = in0_proj(x, l0_norm1, mod0, cond, l0_w_in)
        ya = conv_a(u, seq_len, l0_conv_a)
        yb, lru_state = rglru(u, seq_len, l0_lru_conv_w, l0_lru_conv_b, wcat, l0_lru_ba, l0_lru_bi,
                              l0_lru_lambda, h0)
        x = out_res([(ya, 0), (yb, 0), (yb, 1)], l0_w_out, x, mod0, cond, 2)
        x = ffn_res(x, l0_norm2, mod0, cond, l0_ffn_gate, l0_ffn_up, l0_ffn_down)
        qnope, qpe, ckv, kr, kv, uh = in1_proj(x, l1_norm1, mod1, cond, l1_w_in, l1_q_norm, l1_kv_norm,
                                               wq_perm, l1_w_kv_up)
        if latent:
            yc = attn_lat(qnope, qpe, kv_ctx, kr_ctx, kv, kr, seq_len, past_len)
        else:
            yc = attn_ctx(qnope, qpe, kv, kr, seq_len)
        cs = dft_tables(seq_len)
        k_r, k_s, k_ny = hy_filter(cs, w1p, b1p, w2p, b2p, w3p)
        yd = hyena(uh, seq_len, short_w, short_b, cs, k_r, k_s, k_ny, hy_bias)
        x = out_res([(yc, 0), (yc, 1), (yd, 0)], l1_w_out, x, mod1, cond, 2)
        x = moe_res(x, l1_norm2, mod1, cond, wr_pad, br_pad, l1_exp_gate, l1_exp_up, l1_exp_down)
        return final_rmsnorm(x, final_norm), lru_state, ckv, kr

    zeros_state = jnp.zeros((batch, 2, LRU_W), F32)
    y_p, new_lru, new_ckv, new_kr = trunk(x_prompt.reshape(batch * seq, D), seq, (0, batch * seq),
                                          zeros_state, latent=False)
    y_s, _, _, _ = trunk(x_sample.reshape(dec_batch * dec_seq, D), dec_seq, (1, dec_seq),
                         state_l0_lru, latent=True)
    return (y_p.reshape(batch, seq, D), y_s.reshape(dec_batch, dec_seq, D), new_lru,
            new_ckv.reshape(batch, seq, KV_RANK), new_kr.reshape(batch, seq, ROPE))
```

```python
import functools
import math

import jax
import jax.numpy as jnp
from jax import lax
from jax.experimental import pallas as pl
from jax.experimental.pallas import tpu as pltpu

F32 = jnp.float32
BF16 = jnp.bfloat16
HIGHEST = lax.Precision.HIGHEST

D = 1024
GRID_W = 64
EPS = 1e-6
CONV_W = 512
LRU_W = 1024
LRU_BW = 128
LRU_C = 8.0
MLA_HEADS = 8
Q_RANK = 384
KV_RANK = 256
NOPE = 128
ROPE = 64
VDIM = 128
QK_DIM = NOPE + ROPE
ROPE_THETA = 10000.0
HY_W = 512
HY_BANDS = 16
HY_TARGET = 1e-2
HY_FAST_DECAY = 0.3
HY_SLOW_DECAY = 1.5
D_FF = 2816
N_EXPERTS = 8
D_FF_EXPERT = 1408
IN0 = 3 * CONV_W + 2 * LRU_W
IN1 = Q_RANK + KV_RANK + ROPE + 3 * HY_W

V7X_LANES = 128
V7X_SUBLANES = 8
V7X_VMEM_LIMIT_BYTES = 56 * 1024 * 1024

TM = 512
TN_IN0 = 512
TN_OUT = 512
TF_FFN = 256
MOE_CHUNK = 256
TM_ROUTE = 512
TM_EXPERT = 256
TM_COMBINE = 256
LRU_CB = 256
HY_CB = 256
TQ = 256
TM_IN1 = 256


def _cparams(sem):
    return pltpu.CompilerParams(dimension_semantics=sem, vmem_limit_bytes=V7X_VMEM_LIMIT_BYTES)


def _silu(x):
    return x * jax.nn.sigmoid(x)


def _norm_mod(x, g, shift, scale):
    ms = jnp.mean(x * x, axis=-1, keepdims=True)
    y = x * lax.rsqrt(ms + EPS) * g
    return y * (1.0 + scale) + shift


def _mod_spec(comp, cond, tm, width, col_fn):
    row0, seg = cond
    assert seg % tm == 0
    return pl.BlockSpec((None, 1, width), lambda *ids: (comp * 3 + row0 + (ids[0] * tm) // seg, 0, col_fn(*ids)))


def _adaln_kernel(c_ref, w_ref, b_ref, o_ref):
    a = _silu(c_ref[...])
    o_ref[...] = jnp.dot(a, w_ref[...], preferred_element_type=F32, precision=HIGHEST) + b_ref[...]


def adaln_table(cond8, w_mod, b_mod):
    tn = 1536
    m = pl.pallas_call(
        _adaln_kernel,
        out_shape=jax.ShapeDtypeStruct((V7X_SUBLANES, 6 * D), F32),
        grid=(6 * D // tn,),
        in_specs=[pl.BlockSpec((V7X_SUBLANES, D), lambda j: (0, 0)),
                  pl.BlockSpec((D, tn), lambda j: (0, j)),
                  pl.BlockSpec((1, tn), lambda j: (0, j))],
        out_specs=pl.BlockSpec((V7X_SUBLANES, tn), lambda j: (0, j)),
        compiler_params=_cparams(("arbitrary",)),
        name="adaln",
    )(cond8, w_mod, b_mod.reshape(1, 6 * D))
    return m[:3].reshape(3, 6, D).transpose(1, 0, 2).reshape(18, 1, D)


def _in0_kernel(x_ref, g_ref, sh_ref, sc_ref, w_ref, o_ref, h_sc):
    @pl.when(pl.program_id(1) == 0)
    def _():
        h_sc[...] = _norm_mod(x_ref[...], g_ref[...], sh_ref[...], sc_ref[...]).astype(BF16)

    o_ref[...] = jnp.dot(h_sc[...], w_ref[...].astype(BF16), preferred_element_type=F32)


def in0_proj(x, g, modtab, cond, w_in):
    tn = TN_IN0
    tokens = x.shape[0]
    n = w_in.shape[1]
    zero = lambda i, j: 0
    return pl.pallas_call(
        _in0_kernel,
        out_shape=jax.ShapeDtypeStruct((tokens, n), F32),
        grid=(tokens // TM, n // tn),
        in_specs=[pl.BlockSpec((TM, D), lambda i, j: (i, 0)),
                  pl.BlockSpec((1, D), lambda i, j: (0, 0)),
                  _mod_spec(0, cond, TM, D, zero),
                  _mod_spec(1, cond, TM, D, zero),
                  pl.BlockSpec((D, tn), lambda i, j: (0, j))],
        out_specs=pl.BlockSpec((TM, tn), lambda i, j: (i, j)),
        scratch_shapes=[pltpu.VMEM((TM, D), BF16)],
        compiler_params=_cparams(("parallel", "arbitrary")),
        name="in0_proj",
    )(x, g.reshape(1, D), modtab, modtab, w_in)


def _shift_rows(v, d, t):
    n = v.shape[0]
    if d > 0:
        return jnp.where(t < d, 0.0, pltpu.roll(v, d, 0))
    return jnp.where(t >= n + d, 0.0, pltpu.roll(v, n + d, 0))


def _conv_a_kernel(b_ref, c_ref, x_ref, w_ref, o_ref):
    v = c_ref[...] * x_ref[...]
    t = lax.broadcasted_iota(jnp.int32, v.shape, 0)
    w = w_ref[...]
    y = w[0:1] * _shift_rows(v, 1, t) + w[1:2] * v + w[2:3] * _shift_rows(v, -1, t)
    o_ref[...] = (b_ref[...] * y).astype(o_ref.dtype)


def conv_a(u, seq_len, conv_w):
    tokens = u.shape[0]
    return pl.pallas_call(
        _conv_a_kernel,
        out_shape=jax.ShapeDtypeStruct((tokens, CONV_W), BF16),
        grid=(tokens // seq_len,),
        in_specs=[pl.BlockSpec((seq_len, CONV_W), lambda s: (s, 0)),
                  pl.BlockSpec((seq_len, CONV_W), lambda s: (s, 1)),
                  pl.BlockSpec((seq_len, CONV_W), lambda s: (s, 2)),
                  pl.BlockSpec((3, CONV_W), lambda s: (0, 0))],
        out_specs=pl.BlockSpec((seq_len, CONV_W), lambda s: (s, 0)),
        compiler_params=_cparams(("parallel",)),
        name="conv_a",
    )(u, u, u, conv_w)


def _group_scan(a, b, reverse):
    n, c = a.shape
    a3 = a.reshape(n // V7X_SUBLANES, V7X_SUBLANES, c)
    b3 = b.reshape(n // V7X_SUBLANES, V7X_SUBLANES, c)
    t8 = lax.broadcasted_iota(jnp.int32, a3.shape, 1)
    for d in (1, 2, 4):
        if reverse:
            keep = t8 < V7X_SUBLANES - d
            shift = V7X_SUBLANES - d
        else:
            keep = t8 >= d
            shift = d
        a_sh = jnp.where(keep, pltpu.roll(a3, shift, 1), 1.0)
        b_sh = jnp.where(keep, pltpu.roll(b3, shift, 1), 0.0)
        b3 = a3 * b_sh + b3
        a3 = a3 * a_sh
    return a3.reshape(n, c), b3.reshape(n, c)


def _rglru_kernel(gate_ref, xb_ref, cw_ref, cb_ref, wcat_ref, ba_ref, bi_ref, lam_ref, h0_ref,
                  y_ref, st_ref, af_sc, bf_sc, ab_sc, bb_sc, hf_sc, hb_sc):
    n, cb = xb_ref.shape
    xb = xb_ref[...]
    t = lax.broadcasted_iota(jnp.int32, xb.shape, 0)
    cw = cw_ref[...]
    xc = (cb_ref[...] + cw[0:1] * _shift_rows(xb, 2, t) + cw[1:2] * _shift_rows(xb, 1, t)
          + cw[2:3] * xb + cw[3:4] * _shift_rows(xb, -1, t))
    xcb = xc.astype(BF16)
    g = [jnp.dot(xcb[:, k * LRU_BW:(k + 1) * LRU_BW], wcat_ref[k].astype(BF16), preferred_element_type=F32)
         for k in range(cb // LRU_BW)]

    def direction(d):
        ga = jnp.concatenate([gk[:, (2 * d) * LRU_BW:(2 * d + 1) * LRU_BW] for gk in g], axis=1)
        gi = jnp.concatenate([gk[:, (2 * d + 1) * LRU_BW:(2 * d + 2) * LRU_BW] for gk in g], axis=1)
        r = jax.nn.sigmoid(ga + ba_ref[d:d + 1, :])
        i = jax.nn.sigmoid(gi + bi_ref[d:d + 1, :])
        log_a = (-LRU_C * jax.nn.softplus(-lam_ref[d:d + 1, :])) * r
        a = jnp.exp(log_a)
        mult = jnp.sqrt(1.0 - a * a)
        return a, mult * (i * xc)

    a_f, b_f = direction(0)
    a_f, b_f = _group_scan(a_f, b_f, reverse=False)
    af_sc[...] = a_f
    bf_sc[...] = b_f
    a_b, b_b = direction(1)
    a_b, b_b = _group_scan(a_b, b_b, reverse=True)
    ab_sc[...] = a_b
    bb_sc[...] = b_b

    ng = n // V7X_SUBLANES
    h0 = h0_ref[...]
    init = (jnp.broadcast_to(h0[0:1], (V7X_SUBLANES, cb)), jnp.broadcast_to(h0[1:2], (V7X_SUBLANES, cb)))

    def step(k, carry):
        hf_in, hb_in = carry
        rf = pl.multiple_of(k * V7X_SUBLANES, V7X_SUBLANES)
        rb = pl.multiple_of((ng - 1 - k) * V7X_SUBLANES, V7X_SUBLANES)
        hf = af_sc[pl.ds(rf, V7X_SUBLANES), :] * hf_in + bf_sc[pl.ds(rf, V7X_SUBLANES), :]
        hb = ab_sc[pl.ds(rb, V7X_SUBLANES), :] * hb_in + bb_sc[pl.ds(rb, V7X_SUBLANES), :]
        hf_sc[pl.ds(rf, V7X_SUBLANES), :] = hf
        hb_sc[pl.ds(rb, V7X_SUBLANES), :] = hb
        return (jnp.broadcast_to(hf[V7X_SUBLANES - 1:V7X_SUBLANES], hf.shape), jnp.broadcast_to(hb[0:1], hb.shape))

    hf_last, hb_first = lax.fori_loop(0, ng, step, init)
    st_ref[0:1, :] = hf_last[0:1]
    st_ref[1:2, :] = hb_first[0:1]

    gt = gate_ref[...]
    gelu = 0.5 * gt * (1.0 + jnp.tanh(math.sqrt(2.0 / math.pi) * (gt + 0.044715 * (gt * gt * gt))))
    y_ref[...] = ((hf_sc[...] + hb_sc[...]) * gelu).astype(y_ref.dtype)


def rglru(u, seq_len, conv_w, conv_b, wcat, ba, bi, lam, h0):
    tokens = u.shape[0]
    nseq = tokens // seq_len
    cb = LRU_CB
    gate_blk0 = 3 * CONV_W // cb
    xb_blk0 = (3 * CONV_W + LRU_W) // cb
    seq_scr = lambda: pltpu.VMEM((seq_len, cb), F32)
    return pl.pallas_call(
        _rglru_kernel,
        out_shape=(jax.ShapeDtypeStruct((tokens, LRU_W), BF16), jax.ShapeDtypeStruct((nseq, 2, LRU_W), F32)),
        grid=(nseq, LRU_W // cb),
        in_specs=[pl.BlockSpec((seq_len, cb), lambda s, c: (s, gate_blk0 + c)),
                  pl.BlockSpec((seq_len, cb), lambda s, c: (s, xb_blk0 + c)),
                  pl.BlockSpec((4, cb), lambda s, c: (0, c)),
                  pl.BlockSpec((1, cb), lambda s, c: (0, c)),
                  pl.BlockSpec((cb // LRU_BW, LRU_BW, 4 * LRU_BW), lambda s, c: (c, 0, 0)),
                  pl.BlockSpec((2, cb), lambda s, c: (0, c)),
                  pl.BlockSpec((2, cb), lambda s, c: (0, c)),
                  pl.BlockSpec((2, cb), lambda s, c: (0, c)),
                  pl.BlockSpec((None, 2, cb), lambda s, c: (s, 0, c))],
        out_specs=(pl.BlockSpec((seq_len, cb), lambda s, c: (s, c)),
                   pl.BlockSpec((None, 2, cb), lambda s, c: (s, 0, c))),
        scratch_shapes=[seq_scr() for _ in range(6)],
        compiler_params=_cparams(("parallel", "parallel")),
        name="rglru",
    )(u, u, conv_w, conv_b.reshape(1, LRU_W), wcat, ba, bi, lam, h0)


def _out_res_kernel(p0_ref, p1_ref, p2_ref, w0_ref, w1_ref, w2_ref, x_ref, gt_ref, o_ref):
    y = jnp.dot(p0_ref[...], w0_ref[...].astype(BF16), preferred_element_type=F32)
    y += jnp.dot(p1_ref[...], w1_ref[...].astype(BF16), preferred_element_type=F32)
    y += jnp.dot(p2_ref[...], w2_ref[...].astype(BF16), preferred_element_type=F32)
    o_ref[...] = x_ref[...] + gt_ref[...] * y


def out_res(parts, w_out, x, modtab, cond, gate_comp):
    tokens = x.shape[0]
    tn = TN_OUT
    kb = 512
    lhs_specs = [pl.BlockSpec((TM, kb), (lambda i, j, cbk=cbk: (i, cbk))) for _, cbk in parts]
    w_specs = [pl.BlockSpec((kb, tn), (lambda i, j, r=r: (r, j))) for r in range(3)]
    return pl.pallas_call(
        _out_res_kernel,
        out_shape=jax.ShapeDtypeStruct((tokens, D), F32),
        grid=(tokens // TM, D // tn),
        in_specs=lhs_specs + w_specs + [pl.BlockSpec((TM, tn), lambda i, j: (i, j)),
                                        _mod_spec(gate_comp, cond, TM, tn, lambda i, j: j)],
        out_specs=pl.BlockSpec((TM, tn), lambda i, j: (i, j)),
        compiler_params=_cparams(("parallel", "parallel")),
        name="out_res",
    )(*[a for a, _ in parts], w_out, w_out, w_out, x, modtab)


def _ffn_kernel(x_ref, g_ref, sh_ref, sc_ref, gt_ref, wg_ref, wu_ref, wd_ref, o_ref,
                h_sc, acc_sc, wg_sc, wu_sc, wd_sc):
    i = pl.program_id(0)
    f = pl.program_id(1)

    @pl.when(i == 0)
    def _():
        wg_sc[f] = wg_ref[...].astype(BF16)
        wu_sc[f] = wu_ref[...].astype(BF16)
        wd_sc[f] = wd_ref[...].astype(BF16)

    @pl.when(f == 0)
    def _():
        h_sc[...] = _norm_mod(x_ref[...], g_ref[...], sh_ref[...], sc_ref[...]).astype(BF16)
        acc_sc[...] = jnp.zeros_like(acc_sc)

    h = h_sc[...]
    hg = jnp.dot(h, wg_sc[f], preferred_element_type=F32)
    hu = jnp.dot(h, wu_sc[f], preferred_element_type=F32)
    act = (_silu(hg) * hu).astype(BF16)
    acc_sc[...] += jnp.dot(act, wd_sc[f], preferred_element_type=F32)

    @pl.when(f == pl.num_programs(1) - 1)
    def _():
        o_ref[...] = x_ref[...] + gt_ref[...] * acc_sc[...]


def ffn_res(x, g, modtab, cond, w_gate, w_up, w_down):
    tokens = x.shape[0]
    tf = TF_FFN
    nf = D_FF // tf
    zero = lambda i, f: 0
    wcol = lambda i, f: (0, jnp.where(i == 0, f, nf - 1))
    wrow = lambda i, f: (jnp.where(i == 0, f, nf - 1), 0)
    return pl.pallas_call(
        _ffn_kernel,
        out_shape=jax.ShapeDtypeStruct((tokens, D), F32),
        grid=(tokens // TM, nf),
        in_specs=[pl.BlockSpec((TM, D), lambda i, f: (i, 0)),
                  pl.BlockSpec((1, D), lambda i, f: (0, 0)),
                  _mod_spec(3, cond, TM, D, zero),
                  _mod_spec(4, cond, TM, D, zero),
                  _mod_spec(5, cond, TM, D, zero),
                  pl.BlockSpec((D, tf), wcol),
                  pl.BlockSpec((D, tf), wcol),
                  pl.BlockSpec((tf, D), wrow)],
        out_specs=pl.BlockSpec((TM, D), lambda i, f: (i, 0)),
        scratch_shapes=[pltpu.VMEM((TM, D), BF16), pltpu.VMEM((TM, D), F32),
                        pltpu.VMEM((nf, D, tf), BF16), pltpu.VMEM((nf, D, tf), BF16), pltpu.VMEM((nf, tf, D), BF16)],
        compiler_params=_cparams(("arbitrary", "arbitrary")),
        name="ffn_res",
    )(x, g.reshape(1, D), modtab, modtab, modtab, w_gate, w_up, w_down)


def _rms(x, g):
    return x * lax.rsqrt(jnp.mean(x * x, axis=-1, keepdims=True) + EPS) * g


def _in1_kernel(x_ref, g_ref, sh_ref, sc_ref, w_ref, qn_ref, kvn_ref, wq_ref, wkv_ref,
                qnope_ref, qpe_ref, ckv_ref, kr_ref, kv_ref, uh_ref, w_sc, wq_sc, wkv_sc):
    @pl.when(pl.program_id(0) == 0)
    def _():
        w_sc[...] = w_ref[...].astype(BF16)
        wq_sc[...] = wq_ref[...].astype(BF16)
        wkv_sc[...] = wkv_ref[...].astype(BF16)

    h = _norm_mod(x_ref[...], g_ref[...], sh_ref[...], sc_ref[...]).astype(BF16)
    u = jnp.dot(h, w_sc[...], preferred_element_type=F32)
    o1, o2, o3 = Q_RANK, Q_RANK + KV_RANK, Q_RANK + KV_RANK + ROPE
    cq = _rms(u[:, :o1], qn_ref[...])
    q = jnp.dot(cq.astype(BF16), wq_sc[...], preferred_element_type=F32)
    qnope_ref[...] = q[:, :MLA_HEADS * NOPE].astype(qnope_ref.dtype)
    qpe_ref[...] = q[:, MLA_HEADS * NOPE:]
    ckv = _rms(u[:, o1:o2], kvn_ref[...])
    ckv_ref[...] = ckv
    kv_ref[...] = jnp.dot(ckv.astype(BF16), wkv_sc[...], preferred_element_type=F32).astype(kv_ref.dtype)
    kr_ref[...] = u[:, o2:o3]
    uh_ref[...] = u[:, o3:]


def in1_proj(x, g, modtab, cond, w_in, q_norm, kv_norm, wq_perm, w_kv_up):
    tokens = x.shape[0]
    tm = TM_IN1
    nkv = MLA_HEADS * (NOPE + VDIM)
    const = lambda i: (0, 0)
    zero = lambda i: 0
    once = pl.Buffered(1)
    outs = (jax.ShapeDtypeStruct((tokens, MLA_HEADS * NOPE), BF16),
            jax.ShapeDtypeStruct((tokens, MLA_HEADS * ROPE), F32),
            jax.ShapeDtypeStruct((tokens, KV_RANK), F32),
            jax.ShapeDtypeStruct((tokens, ROPE), F32),
            jax.ShapeDtypeStruct((tokens, nkv), BF16),
            jax.ShapeDtypeStruct((tokens, 3 * HY_W), F32))
    row = lambda w: pl.BlockSpec((tm, w), lambda i: (i, 0))
    return pl.pallas_call(
        _in1_kernel,
        out_shape=outs,
        grid=(tokens // tm,),
        in_specs=[row(D),
                  pl.BlockSpec((1, D), const),
                  _mod_spec(0, cond, tm, D, zero),
                  _mod_spec(1, cond, tm, D, zero),
                  pl.BlockSpec((D, IN1), const, pipeline_mode=once),
                  pl.BlockSpec((1, Q_RANK), const),
                  pl.BlockSpec((1, KV_RANK), const),
                  pl.BlockSpec((Q_RANK, MLA_HEADS * QK_DIM), const, pipeline_mode=once),
                  pl.BlockSpec((KV_RANK, nkv), const, pipeline_mode=once)],
        out_specs=tuple(row(o.shape[1]) for o in outs),
        scratch_shapes=[pltpu.VMEM((D, IN1), BF16), pltpu.VMEM((Q_RANK, MLA_HEADS * QK_DIM), BF16),
                        pltpu.VMEM((KV_RANK, nkv), BF16)],
        compiler_params=_cparams(("arbitrary",)),
        name="in1_proj",
    )(x, g.reshape(1, D), modtab, modtab, w_in, q_norm.reshape(1, Q_RANK), kv_norm.reshape(1, KV_RANK),
      wq_perm, w_kv_up)


def _mm_kernel(a_ref, w_ref, o_ref):
    o_ref[...] = jnp.dot(a_ref[...].astype(BF16), w_ref[...].astype(BF16),
                         preferred_element_type=F32).astype(o_ref.dtype)


def kv_up(ckv, w_kv_up):
    rows = ckv.shape[0]
    n = w_kv_up.shape[1]
    return pl.pallas_call(
        _mm_kernel,
        out_shape=jax.ShapeDtypeStruct((rows, n), BF16),
        grid=(rows // TM,),
        in_specs=[pl.BlockSpec((TM, KV_RANK), lambda i: (i, 0)), pl.BlockSpec((KV_RANK, n), lambda i: (0, 0))],
        out_specs=pl.BlockSpec((TM, n), lambda i: (i, 0)),
        compiler_params=_cparams(("parallel",)),
        name="kv_up",
    )(ckv, w_kv_up)


_NT = (((1,), (1,)), ((), ()))
_SCALE = 1.0 / math.sqrt(QK_DIM)


def _rope_tables(pos):
    n = pos.shape[0]
    lane = lax.broadcasted_iota(jnp.int32, (n, ROPE), 1)
    j = lane & (ROPE // 2 - 1)
    n_freq = ROPE // 4
    inv = jnp.exp((j & (n_freq - 1)).astype(F32) * (-math.log(ROPE_THETA) / n_freq))
    p = jnp.where(j < n_freq, pos >> (GRID_W.bit_length() - 1), pos & (GRID_W - 1)).astype(F32)
    ang = p * inv
    return jnp.cos(ang), jnp.sin(ang)


def _rot_half_matrix():
    i = lax.broadcasted_iota(jnp.int32, (ROPE, ROPE), 0)
    j = lax.broadcasted_iota(jnp.int32, (ROPE, ROPE), 1)
    half = ROPE // 2
    return jnp.where(i == j + half, -1.0, jnp.where(i + half == j, 1.0, 0.0)).astype(F32)


def _rope(x, cos, sin, rot):
    xr = jnp.dot(x, rot, preferred_element_type=F32, precision=HIGHEST)
    return x * cos + xr * sin


def _attend_heads(qn_ref, qpe_fn, kv_refs, kpe_list, o_ref, q_rows):
    for h in range(MLA_HEADS):
        qn = qn_ref[q_rows, h * NOPE:(h + 1) * NOPE]
        qp = qpe_fn(h)
        s = []
        for kv_ref, kpe in zip(kv_refs, kpe_list):
            kn = kv_ref[:, h * (NOPE + VDIM):h * (NOPE + VDIM) + NOPE]
            sk = lax.dot_general(qn, kn, _NT, preferred_element_type=F32)
            sk += lax.dot_general(qp, kpe, _NT, preferred_element_type=F32)
            s.append(sk * _SCALE)
        m = functools.reduce(jnp.maximum, [jnp.max(sk, axis=-1, keepdims=True) for sk in s])
        p = [jnp.exp(sk - m) for sk in s]
        l = functools.reduce(jnp.add, [jnp.sum(pk, axis=-1, keepdims=True) for pk in p])
        o = None
        for kv_ref, pk in zip(kv_refs, p):
            v = kv_ref[:, h * (NOPE + VDIM) + NOPE:(h + 1) * (NOPE + VDIM)]
            ok = jnp.dot(pk.astype(BF16), v, preferred_element_type=F32)
            o = ok if o is None else o + ok
        o_ref[q_rows, h * VDIM:(h + 1) * VDIM] = (o / l).astype(o_ref.dtype)


def _attn_ctx_kernel(qn_ref, qpe_ref, kv_ref, kr_ref, o_ref):
    kpe = kr_ref[...].astype(BF16)
    rows = slice(None)
    _attend_heads(qn_ref, lambda h: qpe_ref[:, h * ROPE:(h + 1) * ROPE].astype(BF16), [kv_ref], [kpe], o_ref, rows)


def attn_ctx(qnope, qpe, kv, kr, seq_len):
    tokens = qnope.shape[0]
    blk = lambda w: pl.BlockSpec((seq_len, w), lambda s: (s, 0))
    return pl.pallas_call(
        _attn_ctx_kernel,
        out_shape=jax.ShapeDtypeStruct((tokens, MLA_HEADS * VDIM), BF16),
        grid=(tokens // seq_len,),
        in_specs=[blk(MLA_HEADS * NOPE), blk(MLA_HEADS * ROPE), blk(MLA_HEADS * (NOPE + VDIM)), blk(ROPE)],
        out_specs=blk(MLA_HEADS * VDIM),
        compiler_params=_cparams(("parallel",)),
        name="attn_ctx",
    )(qnope, qpe, kv, kr)


def _attn_lat_kernel(qn_ref, qpe_ref, kvc_ref, krc_ref, kvl_ref, krl_ref, o_ref):
    tq = qn_ref.shape[0]
    n_lat = krl_ref.shape[0]
    rot = _rot_half_matrix()
    q0 = pl.program_id(1) * tq
    cq, sq = _rope_tables(q0 + lax.broadcasted_iota(jnp.int32, (tq, 1), 0))
    ck, sk = _rope_tables(lax.broadcasted_iota(jnp.int32, (n_lat, 1), 0))
    kpe_lat = _rope(krl_ref[...], ck, sk, rot).astype(BF16)
    kpe_ctx = krc_ref[...].astype(BF16)

    def qpe(h):
        return _rope(qpe_ref[:, h * ROPE:(h + 1) * ROPE], cq, sq, rot).astype(BF16)

    _attend_heads(qn_ref, qpe, [kvc_ref, kvl_ref], [kpe_ctx, kpe_lat], o_ref, slice(None))


def attn_lat(qnope, qpe, kv_ctx, kr_ctx, kv_lat, kr_lat, seq_len, ctx_len):
    tokens = qnope.shape[0]
    nq = seq_len // TQ
    qblk = lambda w: pl.BlockSpec((TQ, w), lambda b, i: (b * nq + i, 0))
    seq = lambda n, w: pl.BlockSpec((n, w), lambda b, i: (b, 0))
    nkv = MLA_HEADS * (NOPE + VDIM)
    return pl.pallas_call(
        _attn_lat_kernel,
        out_shape=jax.ShapeDtypeStruct((tokens, MLA_HEADS * VDIM), BF16),
        grid=(tokens // seq_len, nq),
        in_specs=[qblk(MLA_HEADS * NOPE), qblk(MLA_HEADS * ROPE), seq(ctx_len, nkv), seq(ctx_len, ROPE),
                  seq(seq_len, nkv), seq(seq_len, ROPE)],
        out_specs=qblk(MLA_HEADS * VDIM),
        compiler_params=_cparams(("parallel", "parallel")),
        name="attn_lat",
    )(qnope, qpe, kv_ctx, kr_ctx, kv_lat, kr_lat)


def _dft_kernel(o_ref):
    tr, n = o_ref.shape[1], o_ref.shape[2]
    f = pl.program_id(0) * tr + lax.broadcasted_iota(jnp.int32, (tr, n), 0)
    s = lax.broadcasted_iota(jnp.int32, (tr, n), 1)
    ang = ((f * s) & (2 * n - 1)).astype(F32) * (math.pi / n)
    o_ref[0] = jnp.cos(ang).astype(o_ref.dtype)
    o_ref[1] = jnp.sin(ang).astype(o_ref.dtype)


def dft_tables(n):
    tr = 128
    return pl.pallas_call(
        _dft_kernel,
        out_shape=jax.ShapeDtypeStruct((2, n, n), BF16),
        grid=(n // tr,),
        out_specs=pl.BlockSpec((2, tr, n), lambda i: (0, i, 0)),
        compiler_params=_cparams(("parallel",)),
        name="dft_tables",
    )()


def _split_dot(table, x):
    hi = x.astype(BF16)
    lo = (x - hi.astype(F32)).astype(BF16)
    return (jnp.dot(table, hi, preferred_element_type=F32) + jnp.dot(table, lo, preferred_element_type=F32))


def _hy_filter_kernel(cs_ref, w1_ref, b1_ref, w2_ref, b2_ref, w3_ref, kr_ref, ks_ref, kny_ref):
    n = cs_ref.shape[1]
    row = lax.broadcasted_iota(jnp.int32, (n, V7X_LANES), 0).astype(F32)
    lane = lax.broadcasted_iota(jnp.int32, (n, V7X_LANES), 1)
    t = row * (1.0 / (n - 1))
    w = (2.0 * math.pi) * row / n
    band = jnp.where(lane <= HY_BANDS, lane - 1, lane - 1 - HY_BANDS).astype(F32)
    freq = 1e-4 + band * ((HY_BANDS - 1 - 1e-4) / (HY_BANDS - 1))
    z = jnp.where(lane == 0, t,
                  jnp.where(lane <= HY_BANDS, jnp.cos(freq * w),
                            jnp.where(lane <= 2 * HY_BANDS, -jnp.sin(freq * w), 0.0)))
    hid = jnp.sin(jnp.dot(z, w1_ref[...], preferred_element_type=F32, precision=HIGHEST) + b1_ref[...])
    hid = jnp.sin(jnp.dot(hid, w2_ref[...], preferred_element_type=F32, precision=HIGHEST) + b2_ref[...])
    hf = jnp.dot(hid, w3_ref[...], preferred_element_type=F32, precision=HIGHEST)

    rowc = lax.broadcasted_iota(jnp.int32, (n, HY_W), 0)
    chan = lax.broadcasted_iota(jnp.int32, (n, HY_W), 1).astype(F32)
    max_decay = math.log(HY_TARGET) / HY_FAST_DECAY
    min_decay = math.log(HY_TARGET) / HY_SLOW_DECAY
    deltas = min_decay + chan * ((max_decay - min_decay) / (HY_W - 1))
    decay = jnp.exp(-(rowc.astype(F32) * (1.0 / (n - 1))) * jnp.abs(deltas))
    h_fwd = hf[:, :HY_W] * decay
    h_bwd = jnp.where(rowc == 0, 0.0, hf[:, HY_W:] * decay)
    norm = jnp.sum(jnp.abs(h_fwd) + jnp.abs(h_bwd), axis=0, keepdims=True)
    even = (h_fwd + h_bwd) / norm
    odd = (h_fwd - h_bwd) / norm
    cf = jnp.where(rowc == 0, 1.0, 2.0) * (1.0 / (2 * n))
    kr_ref[...] = cf * _split_dot(cs_ref[0], even)
    ks_ref[...] = cf * _split_dot(cs_ref[1], odd)
    sgn = jnp.where((rowc & 1) == 1, -1.0, 1.0)
    kny_ref[...] = jnp.sum(sgn * even, axis=0, keepdims=True) * (1.0 / (2 * n))


def hy_filter(cs, w1p, b1p, w2p, b2p, w3p):
    n = cs.shape[1]
    full = lambda a: pl.BlockSpec(a.shape, lambda: (0,) * a.ndim)
    args = (cs, w1p, b1p, w2p, b2p, w3p)
    return pl.pallas_call(
        _hy_filter_kernel,
        out_shape=(jax.ShapeDtypeStruct((n, HY_W), F32), jax.ShapeDtypeStruct((n, HY_W), F32),
                   jax.ShapeDtypeStruct((1, HY_W), F32)),
        in_specs=[full(a) for a in args],
        out_specs=(pl.BlockSpec((n, HY_W), lambda: (0, 0)), pl.BlockSpec((n, HY_W), lambda: (0, 0)),
                   pl.BlockSpec((1, HY_W), lambda: (0, 0))),
        compiler_params=pltpu.CompilerParams(vmem_limit_bytes=V7X_VMEM_LIMIT_BYTES),
        name="hy_filter",
    )(*args)


def _hyena_kernel(u0_ref, u1_ref, u2_ref, sw_ref, sb_ref, cs_ref, kr_ref, ks_ref, kny_ref, bias_ref, o_ref):
    n, cb = u0_ref.shape
    t = lax.broadcasted_iota(jnp.int32, (n, cb), 0)

    def short_conv(u_ref, k):
        u = u_ref[...]
        w = sw_ref[:, k * cb:(k + 1) * cb]
        return (sb_ref[:, k * cb:(k + 1) * cb] + w[0:1] * _shift_rows(u, 1, t) + w[1:2] * u
                + w[2:3] * _shift_rows(u, -1, t))

    x0 = short_conv(u0_ref, 0)
    z = short_conv(u1_ref, 1) * short_conv(u2_ref, 2)
    zb = z.astype(BF16)
    c, s = cs_ref[0], cs_ref[1]
    ur = jnp.dot(c, zb, preferred_element_type=F32)
    us = jnp.dot(s, zb, preferred_element_type=F32)
    sgn = jnp.where((t & 1) == 1, -1.0, 1.0)
    uny = jnp.sum(sgn * z, axis=0, keepdims=True)
    kr, ks = kr_ref[...], ks_ref[...]
    yr = (ur * kr - us * ks).astype(BF16)
    ys = (ur * ks + us * kr).astype(BF16)
    y = jnp.dot(c, yr, preferred_element_type=F32) + jnp.dot(s, ys, preferred_element_type=F32)
    y = y + sgn * (uny * kny_ref[...])
    o_ref[...] = (x0 * (y + bias_ref[...] * z)).astype(o_ref.dtype)


def hyena(uh, seq_len, short_w, short_b, cs, kr, ks, kny, bias):
    tokens = uh.shape[0]
    cb = HY_CB
    nc = HY_W // cb
    ublk = lambda k: pl.BlockSpec((seq_len, cb), lambda s, c: (s, k * nc + c))
    chan = lambda rows: pl.BlockSpec((rows, cb), lambda s, c: (0, c))
    return pl.pallas_call(
        _hyena_kernel,
        out_shape=jax.ShapeDtypeStruct((tokens, HY_W), BF16),
        grid=(tokens // seq_len, nc),
        in_specs=[ublk(0), ublk(1), ublk(2),
                  pl.BlockSpec((None, 3, 3 * cb), lambda s, c: (c, 0, 0)),
                  pl.BlockSpec((None, 1, 3 * cb), lambda s, c: (c, 0, 0)),
                  pl.BlockSpec((2, seq_len, seq_len), lambda s, c: (0, 0, 0)),
                  chan(seq_len), chan(seq_len), chan(1), chan(1)],
        out_specs=pl.BlockSpec((seq_len, cb), lambda s, c: (s, c)),
        compiler_params=_cparams(("parallel", "parallel")),
        name="hyena",
    )(uh, uh, uh, short_w, short_b, cs, kr, ks, kny, bias)


META_E1, META_E2, META_R1, META_R2, META_G1, META_G2 = range(6)


def _route_kernel(x_ref, g_ref, sh_ref, sc_ref, wr_ref, br_ref, h_ref, meta_ref, cnt_ref, run_sc):
    tm = x_ref.shape[0]
    lane = lax.broadcasted_iota(jnp.int32, (tm, V7X_LANES), 1)

    @pl.when(pl.program_id(0) == 0)
    def _():
        run_sc[...] = jnp.zeros_like(run_sc)

    h = _norm_mod(x_ref[...], g_ref[...], sh_ref[...], sc_ref[...])
    h_ref[...] = h
    logits = jnp.dot(h, wr_ref[...], preferred_element_type=F32, precision=HIGHEST) + br_ref[...]
    valid = lane < N_EXPERTS
    lg = jnp.where(valid, logits, -jnp.inf)
    ex = jnp.exp(lg - jnp.max(lg, axis=-1, keepdims=True))
    p = ex / jnp.sum(ex, axis=-1, keepdims=True)
    p1 = jnp.max(p, axis=-1, keepdims=True)
    i1 = jnp.min(jnp.where((p == p1) & valid, lane, V7X_LANES), axis=-1, keepdims=True)
    rest = jnp.where((lane == i1) | (~valid), -1.0, p)
    p2 = jnp.max(rest, axis=-1, keepdims=True)
    i2 = jnp.min(jnp.where(rest == p2, lane, V7X_LANES), axis=-1, keepdims=True)
    m1 = lane == i1
    m2 = lane == i2
    chosen = jnp.where(m1 | m2, 1.0, 0.0)
    r = lax.broadcasted_iota(jnp.int32, (tm, tm), 0)
    c = lax.broadcasted_iota(jnp.int32, (tm, tm), 1)
    tri = jnp.where(c < r, 1.0, 0.0).astype(BF16)
    before = jnp.dot(tri, chosen.astype(BF16), preferred_element_type=F32) + run_sc[0:1, :]
    rank1 = jnp.sum(jnp.where(m1, before, 0.0), axis=-1, keepdims=True)
    rank2 = jnp.sum(jnp.where(m2, before, 0.0), axis=-1, keepdims=True)
    inv = 1.0 / (p1 + p2)
    vals = (i1.astype(F32), i2.astype(F32), rank1, rank2, p1 * inv, p2 * inv)
    meta = jnp.zeros((tm, V7X_LANES), F32)
    for k, v in enumerate(vals):
        meta = jnp.where(lane == k, v, meta)
    meta_ref[...] = meta
    run_sc[...] = run_sc[...] + jnp.sum(chosen, axis=0, keepdims=True)
    cnt_ref[...] = run_sc[...]


def moe_route(x, g, modtab, cond, wr_pad, br_pad):
    tokens = x.shape[0]
    tm = TM_ROUTE
    zero = lambda i: 0
    const = lambda i: (0, 0)
    return pl.pallas_call(
        _route_kernel,
        out_shape=(jax.ShapeDtypeStruct((tokens, D), F32),
                   jax.ShapeDtypeStruct((tokens, V7X_LANES), F32),
                   jax.ShapeDtypeStruct((V7X_SUBLANES, V7X_LANES), F32)),
        grid=(tokens // tm,),
        in_specs=[pl.BlockSpec((tm, D), lambda i: (i, 0)),
                  pl.BlockSpec((1, D), const),
                  _mod_spec(3, cond, tm, D, zero),
                  _mod_spec(4, cond, tm, D, zero),
                  pl.BlockSpec((D, V7X_LANES), const),
                  pl.BlockSpec((1, V7X_LANES), const)],
        out_specs=(pl.BlockSpec((tm, D), lambda i: (i, 0)),
                   pl.BlockSpec((tm, V7X_LANES), lambda i: (i, 0)),
                   pl.BlockSpec((V7X_SUBLANES, V7X_LANES), const)),
        scratch_shapes=[pltpu.VMEM((V7X_SUBLANES, V7X_LANES), F32)],
        compiler_params=_cparams(("arbitrary",)),
        name="moe_route",
    )(x, g.reshape(1, D), modtab, modtab, wr_pad, br_pad)


def _row_copy(src_ref, src_row, dst_ref, dst_row, sem):
    return pltpu.make_async_copy(src_ref.at[pl.ds(src_row, 1)], dst_ref.at[pl.ds(dst_row, 1)], sem)


def _dispatch_kernel(pos_ref, h_ref, hs_in_ref, hs_ref, sem):
    del hs_in_ref
    tm = h_ref.shape[0]
    n_tok = pos_ref.shape[0] // 2
    base = pl.program_id(0) * tm

    def issue(r, carry):
        _row_copy(h_ref, r, hs_ref, pos_ref[base + r], sem).start()
        _row_copy(h_ref, r, hs_ref, pos_ref[n_tok + base + r], sem).start()
        return carry

    lax.fori_loop(0, tm, issue, 0, unroll=8)
    for _ in range(2):
        pltpu.make_async_copy(h_ref, hs_ref.at[pl.ds(0, tm)], sem).wait()


def moe_dispatch(pos, h, hs):
    tokens = h.shape[0]
    tm = TM_ROUTE
    return pl.pallas_call(
        _dispatch_kernel,
        out_shape=jax.ShapeDtypeStruct(hs.shape, hs.dtype),
        grid_spec=pltpu.PrefetchScalarGridSpec(
            num_scalar_prefetch=1,
            grid=(tokens // tm,),
            in_specs=[pl.BlockSpec((tm, D), lambda i, pos: (i, 0)),
                      pl.BlockSpec(memory_space=pl.ANY)],
            out_specs=pl.BlockSpec(memory_space=pl.ANY),
            scratch_shapes=[pltpu.SemaphoreType.DMA(())]),
        input_output_aliases={2: 0},
        compiler_params=_cparams(("arbitrary",)),
        name="moe_dispatch",
    )(pos, h, hs)


def _experts_kernel(te_ref, nu_ref, hs_ref, wg_ref, wu_ref, wd_ref, y_ref, wg_sc, wu_sc, wd_sc):
    j = pl.program_id(0)
    e = te_ref[j]
    e_prev = te_ref[jnp.maximum(j - 1, 0)]

    @pl.when((j == 0) | (e != e_prev))
    def _():
        wg_sc[...] = wg_ref[...].astype(BF16)
        wu_sc[...] = wu_ref[...].astype(BF16)
        wd_sc[...] = wd_ref[...].astype(BF16)

    @pl.when(j < nu_ref[0])
    def _():
        h = hs_ref[...].astype(BF16)
        y = None
        for c0 in range(0, D_FF_EXPERT, MOE_CHUNK):
            c1 = min(c0 + MOE_CHUNK, D_FF_EXPERT)
            hg = jnp.dot(h, wg_sc[:, c0:c1], preferred_element_type=F32)
            hu = jnp.dot(h, wu_sc[:, c0:c1], preferred_element_type=F32)
            act = (_silu(hg) * hu).astype(BF16)
            yc = jnp.dot(act, wd_sc[c0:c1, :], preferred_element_type=F32)
            y = yc if y is None else y + yc
        y_ref[...] = y

    @pl.when(j >= nu_ref[0])
    def _():
        y_ref[...] = jnp.zeros_like(y_ref)


def moe_experts(tile_expert, n_used, hs, e_gate, e_up, e_down):
    rows = hs.shape[0]
    tmr = TM_EXPERT
    wspec = lambda shape: pl.BlockSpec((None,) + shape, lambda j, te, nu: (te[j], 0, 0))
    return pl.pallas_call(
        _experts_kernel,
        out_shape=jax.ShapeDtypeStruct((rows, D), F32),
        grid_spec=pltpu.PrefetchScalarGridSpec(
            num_scalar_prefetch=2,
            grid=(rows // tmr,),
            in_specs=[pl.BlockSpec((tmr, D), lambda j, te, nu: (j, 0)),
                      wspec((D, D_FF_EXPERT)), wspec((D, D_FF_EXPERT)), wspec((D_FF_EXPERT, D))],
            out_specs=pl.BlockSpec((tmr, D), lambda j, te, nu: (j, 0)),
            scratch_shapes=[pltpu.VMEM((D, D_FF_EXPERT), BF16), pltpu.VMEM((D, D_FF_EXPERT), BF16),
                            pltpu.VMEM((D_FF_EXPERT, D), BF16)]),
        compiler_params=_cparams(("arbitrary",)),
        name="moe_experts",
    )(tile_expert, n_used, hs, e_gate, e_up, e_down)


def _combine_kernel(pos_ref, x_ref, meta_ref, gt_ref, fg_ref, y_ref, o_ref, b1_sc, b2_sc, sem):
    tm = x_ref.shape[0]
    n_tok = pos_ref.shape[0] // 2
    base = pl.program_id(0) * tm

    def issue(r, carry):
        _row_copy(y_ref, pos_ref[base + r], b1_sc, r, sem).start()
        _row_copy(y_ref, pos_ref[n_tok + base + r], b2_sc, r, sem).start()
        return carry

    lax.fori_loop(0, tm, issue, 0, unroll=8)
    pltpu.make_async_copy(y_ref.at[pl.ds(0, tm)], b1_sc, sem).wait()
    pltpu.make_async_copy(y_ref.at[pl.ds(0, tm)], b2_sc, sem).wait()

    meta = meta_ref[...]
    lane = lax.broadcasted_iota(jnp.int32, meta.shape, 1)
    g1 = jnp.sum(jnp.where(lane == META_G1, meta, 0.0), axis=-1, keepdims=True)
    g2 = jnp.sum(jnp.where(lane == META_G2, meta, 0.0), axis=-1, keepdims=True)
    x = x_ref[...] + gt_ref[...] * (g1 * b1_sc[...] + g2 * b2_sc[...])
    o_ref[...] = _rms(x, fg_ref[...])


def moe_combine(pos, x, meta, modtab, cond, final_g, y):
    tokens = x.shape[0]
    tm = TM_COMBINE
    return pl.pallas_call(
        _combine_kernel,
        out_shape=jax.ShapeDtypeStruct((tokens, D), F32),
        grid_spec=pltpu.PrefetchScalarGridSpec(
            num_scalar_prefetch=1,
            grid=(tokens // tm,),
            in_specs=[pl.BlockSpec((tm, D), lambda i, pos: (i, 0)),
                      pl.BlockSpec((tm, V7X_LANES), lambda i, pos: (i, 0)),
                      _mod_spec(5, cond, tm, D, lambda i, pos: 0),
                      pl.BlockSpec((1, D), lambda i, pos: (0, 0)),
                      pl.BlockSpec(memory_space=pl.ANY)],
            out_specs=pl.BlockSpec((tm, D), lambda i, pos: (i, 0)),
            scratch_shapes=[pltpu.VMEM((tm, D), F32), pltpu.VMEM((tm, D), F32), pltpu.SemaphoreType.DMA(())]),
        compiler_params=_cparams(("arbitrary",)),
        name="moe_combine",
    )(pos, x, meta, modtab, final_g.reshape(1, D), y)


def moe_plan(metas, counts):
    tmr = TM_EXPERT
    cnts = [c[0, :N_EXPERTS].astype(jnp.int32) for c in counts]
    total = functools.reduce(jnp.add, cnts)
    padded = ((total + tmr - 1) // tmr) * tmr
    ends = jnp.cumsum(padded)
    starts = ends - padded
    n_rows = sum(m.shape[0] for m in metas) * 2 + N_EXPERTS * tmr
    n_tiles = n_rows // tmr
    tile_start = jnp.arange(n_tiles, dtype=jnp.int32) * tmr
    tile_expert = jnp.minimum(jnp.sum(tile_start[:, None] >= ends[None, :], axis=1), N_EXPERTS - 1).astype(jnp.int32)
    n_used = (ends[-1] // tmr).astype(jnp.int32).reshape(1)
    pos = []
    base = jnp.zeros((N_EXPERTS,), jnp.int32)
    for m, c in zip(metas, cnts):
        first = starts + base
        sel = lambda col: m[:, col].astype(jnp.int32)
        lookup = lambda e: jnp.sum(jnp.where(e[:, None] == jnp.arange(N_EXPERTS)[None, :], first[None, :], 0), axis=1)
        p1 = lookup(sel(META_E1)) + sel(META_R1)
        p2 = lookup(sel(META_E2)) + sel(META_R2)
        pos.append(jnp.concatenate([p1, p2]).astype(jnp.int32))
        base = base + c
    return pos, tile_expert, n_used, n_rows


def _pad_to(a, shape):
    return jnp.pad(a, [(0, t - s) for s, t in zip(a.shape, shape)])


def _regroup_chunks(a, cb):
    r = a.shape[0]
    return a.reshape(r, 3, HY_W // cb, cb).transpose(2, 0, 1, 3).reshape(HY_W // cb, r, 3 * cb)


def kernel(x_prompt, x_sample, state_l0_lru, cache_l1_ckv, cache_l1_krope, c, c_ctx, l0_norm1, l0_norm2, l0_w_mod, l0_b_mod, l0_w_in, l0_conv_a, l0_lru_conv_w, l0_lru_conv_b, l0_lru_wa, l0_lru_ba, l0_lru_wi, l0_lru_bi, l0_lru_lambda, l0_w_out, l0_ffn_gate, l0_ffn_up, l0_ffn_down, l1_norm1, l1_norm2, l1_w_mod, l1_b_mod, l1_w_in, l1_q_norm, l1_kv_norm, l1_w_q_up, l1_w_kv_up, l1_hy_short_w, l1_hy_short_b, l1_hy_f_w1, l1_hy_f_b1, l1_hy_f_w2, l1_hy_f_b2, l1_hy_f_w3, l1_hy_bias, l1_w_out, l1_router_w, l1_router_b, l1_exp_gate, l1_exp_up, l1_exp_down, final_norm):
    batch, seq, _ = x_prompt.shape
    dec_batch, dec_seq, _ = x_sample.shape
    past_len = cache_l1_ckv.shape[1]

    cond8 = jnp.concatenate([c_ctx[None, :], c, jnp.zeros((V7X_SUBLANES - 1 - dec_batch, D), F32)], axis=0)
    wcat = jnp.concatenate([l0_lru_wa[0], l0_lru_wi[0], l0_lru_wa[1], l0_lru_wi[1]], axis=-1)
    wq = l1_w_q_up.reshape(Q_RANK, MLA_HEADS, QK_DIM)
    wq_perm = jnp.concatenate([wq[:, :, :NOPE].reshape(Q_RANK, MLA_HEADS * NOPE),
                               wq[:, :, NOPE:].reshape(Q_RANK, MLA_HEADS * ROPE)], axis=1)
    hid = V7X_LANES
    w1p = _pad_to(l1_hy_f_w1, (hid, hid))
    b1p = _pad_to(l1_hy_f_b1.reshape(1, -1), (1, hid))
    w2p = _pad_to(l1_hy_f_w2, (hid, hid))
    b2p = _pad_to(l1_hy_f_b2.reshape(1, -1), (1, hid))
    w3p = _pad_to(l1_hy_f_w3, (hid, 2 * HY_W))
    short_w = _regroup_chunks(l1_hy_short_w, HY_CB)
    short_b = _regroup_chunks(l1_hy_short_b.reshape(1, -1), HY_CB)
    hy_bias = l1_hy_bias.reshape(1, HY_W)
    wr_pad = _pad_to(l1_router_w, (D, V7X_LANES))
    br_pad = _pad_to(l1_router_b.reshape(1, -1), (1, V7X_LANES))

    mod0 = adaln_table(cond8, l0_w_mod, l0_b_mod)
    mod1 = adaln_table(cond8, l1_w_mod, l1_b_mod)

    kv_ctx = kv_up(cache_l1_ckv.reshape(dec_batch * past_len, KV_RANK), l1_w_kv_up)
    kr_ctx = cache_l1_krope.reshape(dec_batch * past_len, ROPE)

    def trunk(x, seq_len, cond, h0, latent):
        u = in0_proj(x, l0_norm1, mod0, cond, l0_w_in)
        ya = conv_a(u, seq_len, l0_conv_a)
        yb, lru_state = rglru(u, seq_len, l0_lru_conv_w, l0_lru_conv_b, wcat, l0_lru_ba, l0_lru_bi,
                              l0_lru_lambda, h0)
        x = out_res([(ya, 0), (yb, 0), (yb, 1)], l0_w_out, x, mod0, cond, 2)
        x = ffn_res(x, l0_norm2, mod0, cond, l0_ffn_gate, l0_ffn_up, l0_ffn_down)
        qnope, qpe, ckv, kr, kv, uh = in1_proj(x, l1_norm1, mod1, cond, l1_w_in, l1_q_norm, l1_kv_norm,
                                               wq_perm, l1_w_kv_up)
        if latent:
            yc = attn_lat(qnope, qpe, kv_ctx, kr_ctx, kv, kr, seq_len, past_len)
        else:
            yc = attn_ctx(qnope, qpe, kv, kr, seq_len)
        cs = dft_tables(seq_len)
        k_r, k_s, k_ny = hy_filter(cs, w1p, b1p, w2p, b2p, w3p)
        yd = hyena(uh, seq_len, short_w, short_b, cs, k_r, k_s, k_ny, hy_bias)
        x = out_res([(yc, 0), (yc, 1), (yd, 0)], l1_w_out, x, mod1, cond, 2)
        return x, lru_state, ckv, kr

    conds = ((0, batch * seq), (1, dec_seq))
    zeros_state = jnp.zeros((batch, 2, LRU_W), F32)
    x_p, new_lru, new_ckv, new_kr = trunk(x_prompt.reshape(batch * seq, D), seq, conds[0], zeros_state, latent=False)
    x_s, _, _, _ = trunk(x_sample.reshape(dec_batch * dec_seq, D), dec_seq, conds[1], state_l0_lru, latent=True)

    xs = (x_p, x_s)
    routed = [moe_route(x, l1_norm2, mod1, cond, wr_pad, br_pad) for x, cond in zip(xs, conds)]
    pos, tile_expert, n_used, n_rows = moe_plan([r[1] for r in routed], [r[2] for r in routed])
    hs = jnp.zeros((n_rows, D), F32)
    for p, r in zip(pos, routed):
        hs = moe_dispatch(p, r[0], hs)
    y_rows = moe_experts(tile_expert, n_used, hs, l1_exp_gate, l1_exp_up, l1_exp_down)
    y_p, y_s = [moe_combine(p, x, r[1], mod1, cond, final_norm, y_rows)
                for p, x, r, cond in zip(pos, xs, routed, conds)]
    return (y_p.reshape(batch, seq, D), y_s.reshape(dec_batch, dec_seq, D), new_lru,
            new_ckv.reshape(batch, seq, KV_RANK), new_kr.reshape(batch, seq, ROPE))
```

```python
import functools
import math

import jax
import jax.numpy as jnp
from jax import lax
from jax.experimental import pallas as pl
from jax.experimental.pallas import tpu as pltpu

F32 = jnp.float32
BF16 = jnp.bfloat16
HIGHEST = lax.Precision.HIGHEST

D = 1024
GRID_W = 64
EPS = 1e-6
CONV_W = 512
LRU_W = 1024
LRU_BW = 128
LRU_C = 8.0
MLA_HEADS = 8
Q_RANK = 384
KV_RANK = 256
NOPE = 128
ROPE = 64
VDIM = 128
QK_DIM = NOPE + ROPE
ROPE_THETA = 10000.0
HY_W = 512
HY_BANDS = 16
HY_TARGET = 1e-2
HY_FAST_DECAY = 0.3
HY_SLOW_DECAY = 1.5
D_FF = 2816
N_EXPERTS = 8
D_FF_EXPERT = 1408
IN0 = 3 * CONV_W + 2 * LRU_W
IN1 = Q_RANK + KV_RANK + ROPE + 3 * HY_W

V7X_LANES = 128
V7X_SUBLANES = 8
V7X_VMEM_LIMIT_BYTES = 56 * 1024 * 1024

TM = 512
TN_IN0 = 512
TF_FFN = 256
MOE_CHUNK = 256
TM_ROUTE = 512
TM_EXPERT = 256
TM_COMBINE = 256
LRU_CB = 256
HY_CB = 256
TQ = 256
TM_IN1 = 256


def _cparams(sem):
    return pltpu.CompilerParams(dimension_semantics=sem, vmem_limit_bytes=V7X_VMEM_LIMIT_BYTES)


def _silu(x):
    return x * jax.nn.sigmoid(x)


def _norm_mod(x, g, shift, scale):
    ms = jnp.mean(x * x, axis=-1, keepdims=True)
    y = x * lax.rsqrt(ms + EPS) * g
    return y * (1.0 + scale) + shift


def _mod_spec(comp, cond, tm, width, col_fn, tile_fn=lambda *ids: ids[0]):
    row0, seg = cond
    assert seg % tm == 0
    return pl.BlockSpec((None, 1, width),
                        lambda *ids: (comp * 3 + row0 + (tile_fn(*ids) * tm) // seg, 0, col_fn(*ids)))


def _adaln_kernel(c_ref, w_ref, b_ref, o_ref):
    a = _silu(c_ref[...])
    o_ref[...] = jnp.dot(a, w_ref[...], preferred_element_type=F32, precision=HIGHEST) + b_ref[...]


def adaln_table(cond8, w_mod, b_mod):
    tn = 1536
    m = pl.pallas_call(
        _adaln_kernel,
        out_shape=jax.ShapeDtypeStruct((V7X_SUBLANES, 6 * D), F32),
        grid=(6 * D // tn,),
        in_specs=[pl.BlockSpec((V7X_SUBLANES, D), lambda j: (0, 0)),
                  pl.BlockSpec((D, tn), lambda j: (0, j)),
                  pl.BlockSpec((1, tn), lambda j: (0, j))],
        out_specs=pl.BlockSpec((V7X_SUBLANES, tn), lambda j: (0, j)),
        compiler_params=_cparams(("arbitrary",)),
        name="adaln",
    )(cond8, w_mod, b_mod.reshape(1, 6 * D))
    return m[:3].reshape(3, 6, D).transpose(1, 0, 2).reshape(18, 1, D)


def _tile_of(n_load):
    return lambda s: jnp.maximum(s - n_load, 0)


def _block_of(n_load):
    return lambda s: jnp.minimum(s, n_load - 1)


def _in0_kernel(x_ref, g_ref, sh_ref, sc_ref, w_ref, o_ref, w_sc):
    s = pl.program_id(0)
    n_load, _, tn = w_sc.shape

    @pl.when(s < n_load)
    def _():
        w_sc[s] = w_ref[...].astype(BF16)

    @pl.when(s >= n_load)
    def _():
        h = _norm_mod(x_ref[...], g_ref[...], sh_ref[...], sc_ref[...]).astype(BF16)
        for j in range(n_load):
            o_ref[:, j * tn:(j + 1) * tn] = jnp.dot(h, w_sc[j], preferred_element_type=F32).astype(o_ref.dtype)


def in0_proj(x, g, modtab, cond, w_in):
    tn = TN_IN0
    tokens = x.shape[0]
    n = w_in.shape[1]
    n_load = n // tn
    tile = _tile_of(n_load)
    blk = _block_of(n_load)
    zero = lambda s: 0
    return pl.pallas_call(
        _in0_kernel,
        out_shape=jax.ShapeDtypeStruct((tokens, n), BF16),
        grid=(n_load + tokens // TM,),
        in_specs=[pl.BlockSpec((TM, D), lambda s: (tile(s), 0)),
                  pl.BlockSpec((1, D), lambda s: (0, 0)),
                  _mod_spec(0, cond, TM, D, zero, tile),
                  _mod_spec(1, cond, TM, D, zero, tile),
                  pl.BlockSpec((D, tn), lambda s: (0, blk(s)))],
        out_specs=pl.BlockSpec((TM, n), lambda s: (tile(s), 0)),
        scratch_shapes=[pltpu.VMEM((n_load, D, tn), BF16)],
        compiler_params=_cparams(("arbitrary",)),
        name="in0_proj",
    )(x, g.reshape(1, D), modtab, modtab, w_in)


def _shift_rows(v, d, t):
    n = v.shape[0]
    if d > 0:
        return jnp.where(t < d, 0.0, pltpu.roll(v, d, 0))
    return jnp.where(t >= n + d, 0.0, pltpu.roll(v, n + d, 0))


def _conv_a_kernel(b_ref, c_ref, x_ref, w_ref, o_ref):
    v = c_ref[...].astype(F32) * x_ref[...].astype(F32)
    t = lax.broadcasted_iota(jnp.int32, v.shape, 0)
    w = w_ref[...]
    y = w[0:1] * _shift_rows(v, 1, t) + w[1:2] * v + w[2:3] * _shift_rows(v, -1, t)
    o_ref[...] = (b_ref[...].astype(F32) * y).astype(o_ref.dtype)


def conv_a(u, seq_len, conv_w):
    tokens = u.shape[0]
    return pl.pallas_call(
        _conv_a_kernel,
        out_shape=jax.ShapeDtypeStruct((tokens, CONV_W), BF16),
        grid=(tokens // seq_len,),
        in_specs=[pl.BlockSpec((seq_len, CONV_W), lambda s: (s, 0)),
                  pl.BlockSpec((seq_len, CONV_W), lambda s: (s, 1)),
                  pl.BlockSpec((seq_len, CONV_W), lambda s: (s, 2)),
                  pl.BlockSpec((3, CONV_W), lambda s: (0, 0))],
        out_specs=pl.BlockSpec((seq_len, CONV_W), lambda s: (s, 0)),
        compiler_params=_cparams(("parallel",)),
        name="conv_a",
    )(u, u, u, conv_w)


def _group_scan(a, b, reverse):
    n, c = a.shape
    a3 = a.reshape(n // V7X_SUBLANES, V7X_SUBLANES, c)
    b3 = b.reshape(n // V7X_SUBLANES, V7X_SUBLANES, c)
    t8 = lax.broadcasted_iota(jnp.int32, a3.shape, 1)
    for d in (1, 2, 4):
        if reverse:
            keep = t8 < V7X_SUBLANES - d
            shift = V7X_SUBLANES - d
        else:
            keep = t8 >= d
            shift = d
        a_sh = jnp.where(keep, pltpu.roll(a3, shift, 1), 1.0)
        b_sh = jnp.where(keep, pltpu.roll(b3, shift, 1), 0.0)
        b3 = a3 * b_sh + b3
        a3 = a3 * a_sh
    return a3.reshape(n, c), b3.reshape(n, c)


def _rglru_kernel(gate_ref, xb_ref, cw_ref, cb_ref, wcat_ref, ba_ref, bi_ref, lam_ref, h0_ref,
                  y_ref, st_ref, af_sc, bf_sc, ab_sc, bb_sc, hf_sc, hb_sc):
    n, cb = xb_ref.shape
    xb = xb_ref[...].astype(F32)
    t = lax.broadcasted_iota(jnp.int32, xb.shape, 0)
    cw = cw_ref[...]
    xc = (cb_ref[...] + cw[0:1] * _shift_rows(xb, 2, t) + cw[1:2] * _shift_rows(xb, 1, t)
          + cw[2:3] * xb + cw[3:4] * _shift_rows(xb, -1, t))
    xcb = xc.astype(BF16)
    g = [jnp.dot(xcb[:, k * LRU_BW:(k + 1) * LRU_BW], wcat_ref[k].astype(BF16), preferred_element_type=F32)
         for k in range(cb // LRU_BW)]

    def direction(d):
        ga = jnp.concatenate([gk[:, (2 * d) * LRU_BW:(2 * d + 1) * LRU_BW] for gk in g], axis=1)
        gi = jnp.concatenate([gk[:, (2 * d + 1) * LRU_BW:(2 * d + 2) * LRU_BW] for gk in g], axis=1)
        r = jax.nn.sigmoid(ga + ba_ref[d:d + 1, :])
        i = jax.nn.sigmoid(gi + bi_ref[d:d + 1, :])
        log_a = (-LRU_C * jax.nn.softplus(-lam_ref[d:d + 1, :])) * r
        a = jnp.exp(log_a)
        mult = jnp.sqrt(1.0 - a * a)
        return a, mult * (i * xc)

    a_f, b_f = direction(0)
    a_f, b_f = _group_scan(a_f, b_f, reverse=False)
    af_sc[...] = a_f
    bf_sc[...] = b_f
    a_b, b_b = direction(1)
    a_b, b_b = _group_scan(a_b, b_b, reverse=True)
    ab_sc[...] = a_b
    bb_sc[...] = b_b

    ng = n // V7X_SUBLANES
    h0 = h0_ref[...]
    init = (jnp.broadcast_to(h0[0:1], (V7X_SUBLANES, cb)), jnp.broadcast_to(h0[1:2], (V7X_SUBLANES, cb)))

    def step(k, carry):
        hf_in, hb_in = carry
        rf = pl.multiple_of(k * V7X_SUBLANES, V7X_SUBLANES)
        rb = pl.multiple_of((ng - 1 - k) * V7X_SUBLANES, V7X_SUBLANES)
        hf = af_sc[pl.ds(rf, V7X_SUBLANES), :] * hf_in + bf_sc[pl.ds(rf, V7X_SUBLANES), :]
        hb = ab_sc[pl.ds(rb, V7X_SUBLANES), :] * hb_in + bb_sc[pl.ds(rb, V7X_SUBLANES), :]
        hf_sc[pl.ds(rf, V7X_SUBLANES), :] = hf
        hb_sc[pl.ds(rb, V7X_SUBLANES), :] = hb
        return (jnp.broadcast_to(hf[V7X_SUBLANES - 1:V7X_SUBLANES], hf.shape), jnp.broadcast_to(hb[0:1], hb.shape))

    hf_last, hb_first = lax.fori_loop(0, ng, step, init)
    st_ref[0:1, :] = hf_last[0:1]
    st_ref[1:2, :] = hb_first[0:1]

    gt = gate_ref[...].astype(F32)
    gelu = 0.5 * gt * (1.0 + jnp.tanh(math.sqrt(2.0 / math.pi) * (gt + 0.044715 * (gt * gt * gt))))
    y_ref[...] = ((hf_sc[...] + hb_sc[...]) * gelu).astype(y_ref.dtype)


def rglru(u, seq_len, conv_w, conv_b, wcat, ba, bi, lam, h0):
    tokens = u.shape[0]
    nseq = tokens // seq_len
    cb = LRU_CB
    gate_blk0 = 3 * CONV_W // cb
    xb_blk0 = (3 * CONV_W + LRU_W) // cb
    seq_scr = lambda: pltpu.VMEM((seq_len, cb), F32)
    return pl.pallas_call(
        _rglru_kernel,
        out_shape=(jax.ShapeDtypeStruct((tokens, LRU_W), BF16), jax.ShapeDtypeStruct((nseq, 2, LRU_W), F32)),
        grid=(nseq, LRU_W // cb),
        in_specs=[pl.BlockSpec((seq_len, cb), lambda s, c: (s, gate_blk0 + c)),
                  pl.BlockSpec((seq_len, cb), lambda s, c: (s, xb_blk0 + c)),
                  pl.BlockSpec((4, cb), lambda s, c: (0, c)),
                  pl.BlockSpec((1, cb), lambda s, c: (0, c)),
                  pl.BlockSpec((cb // LRU_BW, LRU_BW, 4 * LRU_BW), lambda s, c: (c, 0, 0)),
                  pl.BlockSpec((2, cb), lambda s, c: (0, c)),
                  pl.BlockSpec((2, cb), lambda s, c: (0, c)),
                  pl.BlockSpec((2, cb), lambda s, c: (0, c)),
                  pl.BlockSpec((None, 2, cb), lambda s, c: (s, 0, c))],
        out_specs=(pl.BlockSpec((seq_len, cb), lambda s, c: (s, c)),
                   pl.BlockSpec((None, 2, cb), lambda s, c: (s, 0, c))),
        scratch_shapes=[seq_scr() for _ in range(6)],
        compiler_params=_cparams(("parallel", "parallel")),
        name="rglru",
    )(u, u, conv_w, conv_b.reshape(1, LRU_W), wcat, ba, bi, lam, h0)


def _out_res_kernel(p0_ref, p1_ref, p2_ref, w_ref, x_ref, gt_ref, o_ref, w_sc):
    s = pl.program_id(0)
    n_load, kb, _ = w_sc.shape

    @pl.when(s < n_load)
    def _():
        w_sc[s] = w_ref[...].astype(BF16)

    @pl.when(s >= n_load)
    def _():
        y = jnp.dot(p0_ref[...], w_sc[0], preferred_element_type=F32)
        y += jnp.dot(p1_ref[...], w_sc[1], preferred_element_type=F32)
        y += jnp.dot(p2_ref[...], w_sc[2], preferred_element_type=F32)
        o_ref[...] = x_ref[...] + gt_ref[...] * y


def out_res(parts, w_out, x, modtab, cond, gate_comp):
    tokens = x.shape[0]
    kb = 512
    n_load = len(parts)
    tile = _tile_of(n_load)
    blk = _block_of(n_load)
    lhs_specs = [pl.BlockSpec((TM, kb), (lambda s, cbk=cbk: (tile(s), cbk))) for _, cbk in parts]
    return pl.pallas_call(
        _out_res_kernel,
        out_shape=jax.ShapeDtypeStruct((tokens, D), F32),
        grid=(n_load + tokens // TM,),
        in_specs=lhs_specs + [pl.BlockSpec((kb, D), lambda s: (blk(s), 0)),
                              pl.BlockSpec((TM, D), lambda s: (tile(s), 0)),
                              _mod_spec(gate_comp, cond, TM, D, lambda s: 0, tile)],
        out_specs=pl.BlockSpec((TM, D), lambda s: (tile(s), 0)),
        scratch_shapes=[pltpu.VMEM((n_load, kb, D), BF16)],
        compiler_params=_cparams(("arbitrary",)),
        name="out_res",
    )(*[a for a, _ in parts], w_out, x, modtab)


def _ffn_kernel(x_ref, g_ref, sh_ref, sc_ref, gt_ref, wg_ref, wu_ref, wd_ref, o_ref, wg_sc, wu_sc, wd_sc):
    s = pl.program_id(0)
    n_load = wg_sc.shape[0]

    @pl.when(s < n_load)
    def _():
        wg_sc[s] = wg_ref[...].astype(BF16)
        wu_sc[s] = wu_ref[...].astype(BF16)
        wd_sc[s] = wd_ref[...].astype(BF16)

    @pl.when(s >= n_load)
    def _():
        x = x_ref[...]
        h = _norm_mod(x, g_ref[...], sh_ref[...], sc_ref[...]).astype(BF16)
        y = None
        for f in range(n_load):
            hg = jnp.dot(h, wg_sc[f], preferred_element_type=F32)
            hu = jnp.dot(h, wu_sc[f], preferred_element_type=F32)
            act = (_silu(hg) * hu).astype(BF16)
            yf = jnp.dot(act, wd_sc[f], preferred_element_type=F32)
            y = yf if y is None else y + yf
        o_ref[...] = x + gt_ref[...] * y


def ffn_res(x, g, modtab, cond, w_gate, w_up, w_down):
    tokens = x.shape[0]
    tf = TF_FFN
    n_load = D_FF // tf
    tile = _tile_of(n_load)
    blk = _block_of(n_load)
    zero = lambda s: 0
    return pl.pallas_call(
        _ffn_kernel,
        out_shape=jax.ShapeDtypeStruct((tokens, D), F32),
        grid=(n_load + tokens // TM,),
        in_specs=[pl.BlockSpec((TM, D), lambda s: (tile(s), 0)),
                  pl.BlockSpec((1, D), lambda s: (0, 0)),
                  _mod_spec(3, cond, TM, D, zero, tile),
                  _mod_spec(4, cond, TM, D, zero, tile),
                  _mod_spec(5, cond, TM, D, zero, tile),
                  pl.BlockSpec((D, tf), lambda s: (0, blk(s))),
                  pl.BlockSpec((D, tf), lambda s: (0, blk(s))),
                  pl.BlockSpec((tf, D), lambda s: (blk(s), 0))],
        out_specs=pl.BlockSpec((TM, D), lambda s: (tile(s), 0)),
        scratch_shapes=[pltpu.VMEM((n_load, D, tf), BF16), pltpu.VMEM((n_load, D, tf), BF16),
                        pltpu.VMEM((n_load, tf, D), BF16)],
        compiler_params=_cparams(("arbitrary",)),
        name="ffn_res",
    )(x, g.reshape(1, D), modtab, modtab, modtab, w_gate, w_up, w_down)


def _rms(x, g):
    return x * lax.rsqrt(jnp.mean(x * x, axis=-1, keepdims=True) + EPS) * g


def _in1_kernel(x_ref, g_ref, sh_ref, sc_ref, w_ref, qn_ref, kvn_ref, wq_ref, wkv_ref,
                qnope_ref, qpe_ref, ckv_ref, kr_ref, kv_ref, uh_ref, w_sc, wq_sc, wkv_sc):
    @pl.when(pl.program_id(0) == 0)
    def _():
        w_sc[...] = w_ref[...].astype(BF16)
        wq_sc[...] = wq_ref[...].astype(BF16)
        wkv_sc[...] = wkv_ref[...].astype(BF16)

    h = _norm_mod(x_ref[...], g_ref[...], sh_ref[...], sc_ref[...]).astype(BF16)
    u = jnp.dot(h, w_sc[...], preferred_element_type=F32)
    o1, o2, o3 = Q_RANK, Q_RANK + KV_RANK, Q_RANK + KV_RANK + ROPE
    cq = _rms(u[:, :o1], qn_ref[...])
    q = jnp.dot(cq.astype(BF16), wq_sc[...], preferred_element_type=F32)
    qnope_ref[...] = q[:, :MLA_HEADS * NOPE].astype(qnope_ref.dtype)
    qpe_ref[...] = q[:, MLA_HEADS * NOPE:]
    ckv = _rms(u[:, o1:o2], kvn_ref[...])
    ckv_ref[...] = ckv
    kv_ref[...] = jnp.dot(ckv.astype(BF16), wkv_sc[...], preferred_element_type=F32).astype(kv_ref.dtype)
    kr_ref[...] = u[:, o2:o3]
    uh_ref[...] = u[:, o3:]


def in1_proj(x, g, modtab, cond, w_in, q_norm, kv_norm, wq_perm, w_kv_up):
    tokens = x.shape[0]
    tm = TM_IN1
    nkv = MLA_HEADS * (NOPE + VDIM)
    const = lambda i: (0, 0)
    zero = lambda i: 0
    once = pl.Buffered(1)
    outs = (jax.ShapeDtypeStruct((tokens, MLA_HEADS * NOPE), BF16),
            jax.ShapeDtypeStruct((tokens, MLA_HEADS * ROPE), F32),
            jax.ShapeDtypeStruct((tokens, KV_RANK), F32),
            jax.ShapeDtypeStruct((tokens, ROPE), F32),
            jax.ShapeDtypeStruct((tokens, nkv), BF16),
            jax.ShapeDtypeStruct((tokens, 3 * HY_W), F32))
    row = lambda w: pl.BlockSpec((tm, w), lambda i: (i, 0))
    return pl.pallas_call(
        _in1_kernel,
        out_shape=outs,
        grid=(tokens // tm,),
        in_specs=[row(D),
                  pl.BlockSpec((1, D), const),
                  _mod_spec(0, cond, tm, D, zero),
                  _mod_spec(1, cond, tm, D, zero),
                  pl.BlockSpec((D, IN1), const, pipeline_mode=once),
                  pl.BlockSpec((1, Q_RANK), const),
                  pl.BlockSpec((1, KV_RANK), const),
                  pl.BlockSpec((Q_RANK, MLA_HEADS * QK_DIM), const, pipeline_mode=once),
                  pl.BlockSpec((KV_RANK, nkv), const, pipeline_mode=once)],
        out_specs=tuple(row(o.shape[1]) for o in outs),
        scratch_shapes=[pltpu.VMEM((D, IN1), BF16), pltpu.VMEM((Q_RANK, MLA_HEADS * QK_DIM), BF16),
                        pltpu.VMEM((KV_RANK, nkv), BF16)],
        compiler_params=_cparams(("arbitrary",)),
        name="in1_proj",
    )(x, g.reshape(1, D), modtab, modtab, w_in, q_norm.reshape(1, Q_RANK), kv_norm.reshape(1, KV_RANK),
      wq_perm, w_kv_up)


def _mm_kernel(a_ref, w_ref, o_ref):
    o_ref[...] = jnp.dot(a_ref[...].astype(BF16), w_ref[...].astype(BF16),
                         preferred_element_type=F32).astype(o_ref.dtype)


def kv_up(ckv, w_kv_up):
    rows = ckv.shape[0]
    n = w_kv_up.shape[1]
    return pl.pallas_call(
        _mm_kernel,
        out_shape=jax.ShapeDtypeStruct((rows, n), BF16),
        grid=(rows // TM,),
        in_specs=[pl.BlockSpec((TM, KV_RANK), lambda i: (i, 0)), pl.BlockSpec((KV_RANK, n), lambda i: (0, 0))],
        out_specs=pl.BlockSpec((TM, n), lambda i: (i, 0)),
        compiler_params=_cparams(("parallel",)),
        name="kv_up",
    )(ckv, w_kv_up)


_NT = (((1,), (1,)), ((), ()))
_SCALE = 1.0 / math.sqrt(QK_DIM)


def _rope_tables(pos):
    n = pos.shape[0]
    lane = lax.broadcasted_iota(jnp.int32, (n, ROPE), 1)
    j = lane & (ROPE // 2 - 1)
    n_freq = ROPE // 4
    inv = jnp.exp((j & (n_freq - 1)).astype(F32) * (-math.log(ROPE_THETA) / n_freq))
    p = jnp.where(j < n_freq, pos >> (GRID_W.bit_length() - 1), pos & (GRID_W - 1)).astype(F32)
    ang = p * inv
    return jnp.cos(ang), jnp.sin(ang)


def _rot_half_matrix():
    i = lax.broadcasted_iota(jnp.int32, (ROPE, ROPE), 0)
    j = lax.broadcasted_iota(jnp.int32, (ROPE, ROPE), 1)
    half = ROPE // 2
    return jnp.where(i == j + half, -1.0, jnp.where(i + half == j, 1.0, 0.0)).astype(F32)


def _rope(x, cos, sin, rot):
    xr = jnp.dot(x, rot, preferred_element_type=F32, precision=HIGHEST)
    return x * cos + xr * sin


def _attend_heads(qn_ref, qpe_fn, kv_refs, kpe_list, o_ref, q_rows):
    for h in range(MLA_HEADS):
        qn = qn_ref[q_rows, h * NOPE:(h + 1) * NOPE]
        qp = qpe_fn(h)
        s = []
        for kv_ref, kpe in zip(kv_refs, kpe_list):
            kn = kv_ref[:, h * (NOPE + VDIM):h * (NOPE + VDIM) + NOPE]
            sk = lax.dot_general(qn, kn, _NT, preferred_element_type=F32)
            sk += lax.dot_general(qp, kpe, _NT, preferred_element_type=F32)
            s.append(sk * _SCALE)
        m = functools.reduce(jnp.maximum, [jnp.max(sk, axis=-1, keepdims=True) for sk in s])
        p = [jnp.exp(sk - m) for sk in s]
        l = functools.reduce(jnp.add, [jnp.sum(pk, axis=-1, keepdims=True) for pk in p])
        o = None
        for kv_ref, pk in zip(kv_refs, p):
            v = kv_ref[:, h * (NOPE + VDIM) + NOPE:(h + 1) * (NOPE + VDIM)]
            ok = jnp.dot(pk.astype(BF16), v, preferred_element_type=F32)
            o = ok if o is None else o + ok
        o_ref[q_rows, h * VDIM:(h + 1) * VDIM] = (o / l).astype(o_ref.dtype)


def _attn_ctx_kernel(qn_ref, qpe_ref, kv_ref, kr_ref, o_ref):
    kpe = kr_ref[...].astype(BF16)
    rows = slice(None)
    _attend_heads(qn_ref, lambda h: qpe_ref[:, h * ROPE:(h + 1) * ROPE].astype(BF16), [kv_ref], [kpe], o_ref, rows)


def attn_ctx(qnope, qpe, kv, kr, seq_len):
    tokens = qnope.shape[0]
    blk = lambda w: pl.BlockSpec((seq_len, w), lambda s: (s, 0))
    return pl.pallas_call(
        _attn_ctx_kernel,
        out_shape=jax.ShapeDtypeStruct((tokens, MLA_HEADS * VDIM), BF16),
        grid=(tokens // seq_len,),
        in_specs=[blk(MLA_HEADS * NOPE), blk(MLA_HEADS * ROPE), blk(MLA_HEADS * (NOPE + VDIM)), blk(ROPE)],
        out_specs=blk(MLA_HEADS * VDIM),
        compiler_params=_cparams(("parallel",)),
        name="attn_ctx",
    )(qnope, qpe, kv, kr)


def _attn_lat_kernel(qn_ref, qpe_ref, kvc_ref, krc_ref, kvl_ref, krl_ref, o_ref):
    tq = qn_ref.shape[0]
    n_lat = krl_ref.shape[0]
    rot = _rot_half_matrix()
    q0 = pl.program_id(1) * tq
    cq, sq = _rope_tables(q0 + lax.broadcasted_iota(jnp.int32, (tq, 1), 0))
    ck, sk = _rope_tables(lax.broadcasted_iota(jnp.int32, (n_lat, 1), 0))
    kpe_lat = _rope(krl_ref[...], ck, sk, rot).astype(BF16)
    kpe_ctx = krc_ref[...].astype(BF16)

    def qpe(h):
        return _rope(qpe_ref[:, h * ROPE:(h + 1) * ROPE], cq, sq, rot).astype(BF16)

    _attend_heads(qn_ref, qpe, [kvc_ref, kvl_ref], [kpe_ctx, kpe_lat], o_ref, slice(None))


def attn_lat(qnope, qpe, kv_ctx, kr_ctx, kv_lat, kr_lat, seq_len, ctx_len):
    tokens = qnope.shape[0]
    nq = seq_len // TQ
    qblk = lambda w: pl.BlockSpec((TQ, w), lambda b, i: (b * nq + i, 0))
    seq = lambda n, w: pl.BlockSpec((n, w), lambda b, i: (b, 0))
    nkv = MLA_HEADS * (NOPE + VDIM)
    return pl.pallas_call(
        _attn_lat_kernel,
        out_shape=jax.ShapeDtypeStruct((tokens, MLA_HEADS * VDIM), BF16),
        grid=(tokens // seq_len, nq),
        in_specs=[qblk(MLA_HEADS * NOPE), qblk(MLA_HEADS * ROPE), seq(ctx_len, nkv), seq(ctx_len, ROPE),
                  seq(seq_len, nkv), seq(seq_len, ROPE)],
        out_specs=qblk(MLA_HEADS * VDIM),
        compiler_params=_cparams(("parallel", "parallel")),
        name="attn_lat",
    )(qnope, qpe, kv_ctx, kr_ctx, kv_lat, kr_lat)


def _dft_kernel(o_ref):
    tr, n = o_ref.shape[1], o_ref.shape[2]
    f = pl.program_id(0) * tr + lax.broadcasted_iota(jnp.int32, (tr, n), 0)
    s = lax.broadcasted_iota(jnp.int32, (tr, n), 1)
    ang = ((f * s) & (2 * n - 1)).astype(F32) * (math.pi / n)
    o_ref[0] = jnp.cos(ang).astype(o_ref.dtype)
    o_ref[1] = jnp.sin(ang).astype(o_ref.dtype)


def dft_tables(n):
    tr = 128
    return pl.pallas_call(
        _dft_kernel,
        out_shape=jax.ShapeDtypeStruct((2, n, n), BF16),
        grid=(n // tr,),
        out_specs=pl.BlockSpec((2, tr, n), lambda i: (0, i, 0)),
        compiler_params=_cparams(("parallel",)),
        name="dft_tables",
    )()


def _split_dot(table, x):
    hi = x.astype(BF16)
    lo = (x - hi.astype(F32)).astype(BF16)
    return (jnp.dot(table, hi, preferred_element_type=F32) + jnp.dot(table, lo, preferred_element_type=F32))


def _hy_filter_kernel(cs_ref, w1_ref, b1_ref, w2_ref, b2_ref, w3_ref, kr_ref, ks_ref, kny_ref):
    n = cs_ref.shape[1]
    row = lax.broadcasted_iota(jnp.int32, (n, V7X_LANES), 0).astype(F32)
    lane = lax.broadcasted_iota(jnp.int32, (n, V7X_LANES), 1)
    t = row * (1.0 / (n - 1))
    w = (2.0 * math.pi) * row / n
    band = jnp.where(lane <= HY_BANDS, lane - 1, lane - 1 - HY_BANDS).astype(F32)
    freq = 1e-4 + band * ((HY_BANDS - 1 - 1e-4) / (HY_BANDS - 1))
    z = jnp.where(lane == 0, t,
                  jnp.where(lane <= HY_BANDS, jnp.cos(freq * w),
                            jnp.where(lane <= 2 * HY_BANDS, -jnp.sin(freq * w), 0.0)))
    hid = jnp.sin(jnp.dot(z, w1_ref[...], preferred_element_type=F32, precision=HIGHEST) + b1_ref[...])
    hid = jnp.sin(jnp.dot(hid, w2_ref[...], preferred_element_type=F32, precision=HIGHEST) + b2_ref[...])
    hf = jnp.dot(hid, w3_ref[...], preferred_element_type=F32, precision=HIGHEST)

    rowc = lax.broadcasted_iota(jnp.int32, (n, HY_W), 0)
    chan = lax.broadcasted_iota(jnp.int32, (n, HY_W), 1).astype(F32)
    max_decay = math.log(HY_TARGET) / HY_FAST_DECAY
    min_decay = math.log(HY_TARGET) / HY_SLOW_DECAY
    deltas = min_decay + chan * ((max_decay - min_decay) / (HY_W - 1))
    decay = jnp.exp(-(rowc.astype(F32) * (1.0 / (n - 1))) * jnp.abs(deltas))
    h_fwd = hf[:, :HY_W] * decay
    h_bwd = jnp.where(rowc == 0, 0.0, hf[:, HY_W:] * decay)
    norm = jnp.sum(jnp.abs(h_fwd) + jnp.abs(h_bwd), axis=0, keepdims=True)
    even = (h_fwd + h_bwd) / norm
    odd = (h_fwd - h_bwd) / norm
    cf = jnp.where(rowc == 0, 1.0, 2.0) * (1.0 / (2 * n))
    kr_ref[...] = cf * _split_dot(cs_ref[0], even)
    ks_ref[...] = cf * _split_dot(cs_ref[1], odd)
    sgn = jnp.where((rowc & 1) == 1, -1.0, 1.0)
    kny_ref[...] = jnp.sum(sgn * even, axis=0, keepdims=True) * (1.0 / (2 * n))


def hy_filter(cs, w1p, b1p, w2p, b2p, w3p):
    n = cs.shape[1]
    full = lambda a: pl.BlockSpec(a.shape, lambda: (0,) * a.ndim)
    args = (cs, w1p, b1p, w2p, b2p, w3p)
    return pl.pallas_call(
        _hy_filter_kernel,
        out_shape=(jax.ShapeDtypeStruct((n, HY_W), F32), jax.ShapeDtypeStruct((n, HY_W), F32),
                   jax.ShapeDtypeStruct((1, HY_W), F32)),
        in_specs=[full(a) for a in args],
        out_specs=(pl.BlockSpec((n, HY_W), lambda: (0, 0)), pl.BlockSpec((n, HY_W), lambda: (0, 0)),
                   pl.BlockSpec((1, HY_W), lambda: (0, 0))),
        compiler_params=pltpu.CompilerParams(vmem_limit_bytes=V7X_VMEM_LIMIT_BYTES),
        name="hy_filter",
    )(*args)


def _hyena_kernel(u0_ref, u1_ref, u2_ref, sw_ref, sb_ref, cs_ref, kr_ref, ks_ref, kny_ref, bias_ref, o_ref):
    n, cb = u0_ref.shape
    t = lax.broadcasted_iota(jnp.int32, (n, cb), 0)

    def short_conv(u_ref, k):
        u = u_ref[...]
        w = sw_ref[:, k * cb:(k + 1) * cb]
        return (sb_ref[:, k * cb:(k + 1) * cb] + w[0:1] * _shift_rows(u, 1, t) + w[1:2] * u
                + w[2:3] * _shift_rows(u, -1, t))

    x0 = short_conv(u0_ref, 0)
    z = short_conv(u1_ref, 1) * short_conv(u2_ref, 2)
    zb = z.astype(BF16)
    c, s = cs_ref[0], cs_ref[1]
    ur = jnp.dot(c, zb, preferred_element_type=F32)
    us = jnp.dot(s, zb, preferred_element_type=F32)
    sgn = jnp.where((t & 1) == 1, -1.0, 1.0)
    uny = jnp.sum(sgn * z, axis=0, keepdims=True)
    kr, ks = kr_ref[...], ks_ref[...]
    yr = (ur * kr - us * ks).astype(BF16)
    ys = (ur * ks + us * kr).astype(BF16)
    y = jnp.dot(c, yr, preferred_element_type=F32) + jnp.dot(s, ys, preferred_element_type=F32)
    y = y + sgn * (uny * kny_ref[...])
    o_ref[...] = (x0 * (y + bias_ref[...] * z)).astype(o_ref.dtype)


def hyena(uh, seq_len, short_w, short_b, cs, kr, ks, kny, bias):
    tokens = uh.shape[0]
    cb = HY_CB
    nc = HY_W // cb
    ublk = lambda k: pl.BlockSpec((seq_len, cb), lambda s, c: (s, k * nc + c))
    chan = lambda rows: pl.BlockSpec((rows, cb), lambda s, c: (0, c))
    return pl.pallas_call(
        _hyena_kernel,
        out_shape=jax.ShapeDtypeStruct((tokens, HY_W), BF16),
        grid=(tokens // seq_len, nc),
        in_specs=[ublk(0), ublk(1), ublk(2),
                  pl.BlockSpec((None, 3, 3 * cb), lambda s, c: (c, 0, 0)),
                  pl.BlockSpec((None, 1, 3 * cb), lambda s, c: (c, 0, 0)),
                  pl.BlockSpec((2, seq_len, seq_len), lambda s, c: (0, 0, 0)),
                  chan(seq_len), chan(seq_len), chan(1), chan(1)],
        out_specs=pl.BlockSpec((seq_len, cb), lambda s, c: (s, c)),
        compiler_params=_cparams(("parallel", "parallel")),
        name="hyena",
    )(uh, uh, uh, short_w, short_b, cs, kr, ks, kny, bias)


META_E1, META_E2, META_R1, META_R2, META_G1, META_G2 = range(6)


def _route_kernel(x_ref, g_ref, sh_ref, sc_ref, wr_ref, br_ref, h_ref, meta_ref, cnt_ref, run_sc):
    tm = x_ref.shape[0]
    lane = lax.broadcasted_iota(jnp.int32, (tm, V7X_LANES), 1)

    @pl.when(pl.program_id(0) == 0)
    def _():
        run_sc[...] = jnp.zeros_like(run_sc)

    h = _norm_mod(x_ref[...], g_ref[...], sh_ref[...], sc_ref[...])
    h_ref[...] = h
    logits = jnp.dot(h, wr_ref[...], preferred_element_type=F32, precision=HIGHEST) + br_ref[...]
    valid = lane < N_EXPERTS
    lg = jnp.where(valid, logits, -jnp.inf)
    ex = jnp.exp(lg - jnp.max(lg, axis=-1, keepdims=True))
    p = ex / jnp.sum(ex, axis=-1, keepdims=True)
    p1 = jnp.max(p, axis=-1, keepdims=True)
    i1 = jnp.min(jnp.where((p == p1) & valid, lane, V7X_LANES), axis=-1, keepdims=True)
    rest = jnp.where((lane == i1) | (~valid), -1.0, p)
    p2 = jnp.max(rest, axis=-1, keepdims=True)
    i2 = jnp.min(jnp.where(rest == p2, lane, V7X_LANES), axis=-1, keepdims=True)
    m1 = lane == i1
    m2 = lane == i2
    chosen = jnp.where(m1 | m2, 1.0, 0.0)
    r = lax.broadcasted_iota(jnp.int32, (tm, tm), 0)
    c = lax.broadcasted_iota(jnp.int32, (tm, tm), 1)
    tri = jnp.where(c < r, 1.0, 0.0).astype(BF16)
    before = jnp.dot(tri, chosen.astype(BF16), preferred_element_type=F32) + run_sc[0:1, :]
    rank1 = jnp.sum(jnp.where(m1, before, 0.0), axis=-1, keepdims=True)
    rank2 = jnp.sum(jnp.where(m2, before, 0.0), axis=-1, keepdims=True)
    inv = 1.0 / (p1 + p2)
    vals = (i1.astype(F32), i2.astype(F32), rank1, rank2, p1 * inv, p2 * inv)
    meta = jnp.zeros((tm, V7X_LANES), F32)
    for k, v in enumerate(vals):
        meta = jnp.where(lane == k, v, meta)
    meta_ref[...] = meta
    run_sc[...] = run_sc[...] + jnp.sum(chosen, axis=0, keepdims=True)
    cnt_ref[...] = run_sc[...]


def moe_route(x, g, modtab, cond, wr_pad, br_pad):
    tokens = x.shape[0]
    tm = TM_ROUTE
    zero = lambda i: 0
    const = lambda i: (0, 0)
    return pl.pallas_call(
        _route_kernel,
        out_shape=(jax.ShapeDtypeStruct((tokens, D), F32),
                   jax.ShapeDtypeStruct((tokens, V7X_LANES), F32),
                   jax.ShapeDtypeStruct((V7X_SUBLANES, V7X_LANES), F32)),
        grid=(tokens // tm,),
        in_specs=[pl.BlockSpec((tm, D), lambda i: (i, 0)),
                  pl.BlockSpec((1, D), const),
                  _mod_spec(3, cond, tm, D, zero),
                  _mod_spec(4, cond, tm, D, zero),
                  pl.BlockSpec((D, V7X_LANES), const),
                  pl.BlockSpec((1, V7X_LANES), const)],
        out_specs=(pl.BlockSpec((tm, D), lambda i: (i, 0)),
                   pl.BlockSpec((tm, V7X_LANES), lambda i: (i, 0)),
                   pl.BlockSpec((V7X_SUBLANES, V7X_LANES), const)),
        scratch_shapes=[pltpu.VMEM((V7X_SUBLANES, V7X_LANES), F32)],
        compiler_params=_cparams(("arbitrary",)),
        name="moe_route",
    )(x, g.reshape(1, D), modtab, modtab, wr_pad, br_pad)


def _row_copy(src_ref, src_row, dst_ref, dst_row, sem):
    return pltpu.make_async_copy(src_ref.at[pl.ds(src_row, 1)], dst_ref.at[pl.ds(dst_row, 1)], sem)


def _dispatch_kernel(pos_ref, h_ref, hs_in_ref, hs_ref, sem):
    del hs_in_ref
    tm = h_ref.shape[0]
    n_tok = pos_ref.shape[0] // 2
    base = pl.program_id(0) * tm

    def issue(r, carry):
        _row_copy(h_ref, r, hs_ref, pos_ref[base + r], sem).start(priority=0)
        _row_copy(h_ref, r, hs_ref, pos_ref[n_tok + base + r], sem).start(priority=1)
        return carry

    lax.fori_loop(0, tm, issue, 0, unroll=8)
    for _ in range(2):
        pltpu.make_async_copy(h_ref, hs_ref.at[pl.ds(0, tm)], sem).wait()


def moe_dispatch(pos, h, hs):
    tokens = h.shape[0]
    tm = TM_ROUTE
    return pl.pallas_call(
        _dispatch_kernel,
        out_shape=jax.ShapeDtypeStruct(hs.shape, hs.dtype),
        grid_spec=pltpu.PrefetchScalarGridSpec(
            num_scalar_prefetch=1,
            grid=(tokens // tm,),
            in_specs=[pl.BlockSpec((tm, D), lambda i, pos: (i, 0)),
                      pl.BlockSpec(memory_space=pl.ANY)],
            out_specs=pl.BlockSpec(memory_space=pl.ANY),
            scratch_shapes=[pltpu.SemaphoreType.DMA(())]),
        input_output_aliases={2: 0},
        compiler_params=_cparams(("arbitrary",)),
        name="moe_dispatch",
    )(pos, h, hs)


def _experts_kernel(te_ref, sg_ref, su_ref, sd_ref, nu_ref, hs_ref, wg_ref, wu_ref, wd_ref, y_ref,
                    wg_sc, wu_sc, wd_sc):
    del sg_ref, su_ref, sd_ref
    j = pl.program_id(0)
    e = te_ref[j]
    e_prev = te_ref[jnp.maximum(j - 1, 0)]

    @pl.when((j == 0) | (e != e_prev))
    def _():
        wg_sc[...] = wg_ref[...].astype(BF16)
        wu_sc[...] = wu_ref[...].astype(BF16)
        wd_sc[...] = wd_ref[...].astype(BF16)

    @pl.when(j < nu_ref[0])
    def _():
        h = hs_ref[...].astype(BF16)
        y = None
        for c0 in range(0, D_FF_EXPERT, MOE_CHUNK):
            c1 = min(c0 + MOE_CHUNK, D_FF_EXPERT)
            hg = jnp.dot(h, wg_sc[:, c0:c1], preferred_element_type=F32)
            hu = jnp.dot(h, wu_sc[:, c0:c1], preferred_element_type=F32)
            act = (_silu(hg) * hu).astype(BF16)
            yc = jnp.dot(act, wd_sc[c0:c1, :], preferred_element_type=F32)
            y = yc if y is None else y + yc
        y_ref[...] = y

    @pl.when(j >= nu_ref[0])
    def _():
        y_ref[...] = jnp.zeros_like(y_ref)


def moe_experts(tile_expert, stages, n_used, hs, e_gate, e_up, e_down):
    rows = hs.shape[0]
    tmr = TM_EXPERT
    wspec = lambda shape, k: pl.BlockSpec((None,) + shape, lambda j, *pf: (pf[1 + k][j], 0, 0))
    return pl.pallas_call(
        _experts_kernel,
        out_shape=jax.ShapeDtypeStruct((rows, D), F32),
        grid_spec=pltpu.PrefetchScalarGridSpec(
            num_scalar_prefetch=5,
            grid=(rows // tmr,),
            in_specs=[pl.BlockSpec((tmr, D), lambda j, *pf: (j, 0)),
                      wspec((D, D_FF_EXPERT), 0), wspec((D, D_FF_EXPERT), 1), wspec((D_FF_EXPERT, D), 2)],
            out_specs=pl.BlockSpec((tmr, D), lambda j, *pf: (j, 0)),
            scratch_shapes=[pltpu.VMEM((D, D_FF_EXPERT), BF16), pltpu.VMEM((D, D_FF_EXPERT), BF16),
                            pltpu.VMEM((D_FF_EXPERT, D), BF16)]),
        compiler_params=_cparams(("arbitrary",)),
        name="moe_experts",
    )(tile_expert, *stages, n_used, hs, e_gate, e_up, e_down)


def _combine_kernel(pos_ref, x_ref, meta_ref, gt_ref, fg_ref, y_ref, o_ref, b1_sc, b2_sc, sem):
    tm = x_ref.shape[0]
    n_tok = pos_ref.shape[0] // 2
    base = pl.program_id(0) * tm

    def issue(r, carry):
        _row_copy(y_ref, pos_ref[base + r], b1_sc, r, sem).start(priority=0)
        _row_copy(y_ref, pos_ref[n_tok + base + r], b2_sc, r, sem).start(priority=1)
        return carry

    lax.fori_loop(0, tm, issue, 0, unroll=8)
    pltpu.make_async_copy(y_ref.at[pl.ds(0, tm)], b1_sc, sem).wait()
    pltpu.make_async_copy(y_ref.at[pl.ds(0, tm)], b2_sc, sem).wait()

    meta = meta_ref[...]
    lane = lax.broadcasted_iota(jnp.int32, meta.shape, 1)
    g1 = jnp.sum(jnp.where(lane == META_G1, meta, 0.0), axis=-1, keepdims=True)
    g2 = jnp.sum(jnp.where(lane == META_G2, meta, 0.0), axis=-1, keepdims=True)
    x = x_ref[...] + gt_ref[...] * (g1 * b1_sc[...] + g2 * b2_sc[...])
    o_ref[...] = _rms(x, fg_ref[...])


def moe_combine(pos, x, meta, modtab, cond, final_g, y):
    tokens = x.shape[0]
    tm = TM_COMBINE
    return pl.pallas_call(
        _combine_kernel,
        out_shape=jax.ShapeDtypeStruct((tokens, D), F32),
        grid_spec=pltpu.PrefetchScalarGridSpec(
            num_scalar_prefetch=1,
            grid=(tokens // tm,),
            in_specs=[pl.BlockSpec((tm, D), lambda i, pos: (i, 0)),
                      pl.BlockSpec((tm, V7X_LANES), lambda i, pos: (i, 0)),
                      _mod_spec(5, cond, tm, D, lambda i, pos: 0),
                      pl.BlockSpec((1, D), lambda i, pos: (0, 0)),
                      pl.BlockSpec(memory_space=pl.ANY)],
            out_specs=pl.BlockSpec((tm, D), lambda i, pos: (i, 0)),
            scratch_shapes=[pltpu.VMEM((tm, D), F32), pltpu.VMEM((tm, D), F32), pltpu.SemaphoreType.DMA(())]),
        compiler_params=_cparams(("arbitrary",)),
        name="moe_combine",
    )(pos, x, meta, modtab, final_g.reshape(1, D), y)


def moe_plan(metas, counts):
    tmr = TM_EXPERT
    cnts = [c[0, :N_EXPERTS].astype(jnp.int32) for c in counts]
    total = functools.reduce(jnp.add, cnts)
    padded = ((total + tmr - 1) // tmr) * tmr
    ends = jnp.cumsum(padded)
    starts = ends - padded
    n_rows = sum(m.shape[0] for m in metas) * 2 + N_EXPERTS * tmr
    n_tiles = n_rows // tmr
    tile_start = jnp.arange(n_tiles, dtype=jnp.int32) * tmr
    tile_expert = jnp.minimum(jnp.sum(tile_start[:, None] >= ends[None, :], axis=1), N_EXPERTS - 1).astype(jnp.int32)
    n_used = (ends[-1] // tmr).astype(jnp.int32).reshape(1)
    eid = jnp.arange(N_EXPERTS, dtype=jnp.int32)
    later = jnp.where((eid[None, :] > eid[:, None]) & (padded[None, :] > 0), eid[None, :], N_EXPERTS)
    nxt = jnp.min(later, axis=1)
    next_used = jnp.where(nxt == N_EXPERTS, eid, nxt)
    pick = lambda table: jnp.sum(jnp.where(tile_expert[:, None] == eid[None, :], table[None, :], 0), axis=1)
    k_in_group = (tile_start - pick(starts)) // tmr
    tile_next = pick(next_used)
    stages = [jnp.where(k_in_group < k, tile_expert, tile_next).astype(jnp.int32) for k in (1, 2, 3)]
    pos = []
    base = jnp.zeros((N_EXPERTS,), jnp.int32)
    for m, c in zip(metas, cnts):
        first = starts + base
        sel = lambda col: m[:, col].astype(jnp.int32)
        lookup = lambda e: jnp.sum(jnp.where(e[:, None] == jnp.arange(N_EXPERTS)[None, :], first[None, :], 0), axis=1)
        p1 = lookup(sel(META_E1)) + sel(META_R1)
        p2 = lookup(sel(META_E2)) + sel(META_R2)
        pos.append(jnp.concatenate([p1, p2]).astype(jnp.int32))
        base = base + c
    return pos, tile_expert, stages, n_used, n_rows


def _pad_to(a, shape):
    return jnp.pad(a, [(0, t - s) for s, t in zip(a.shape, shape)])


def _regroup_chunks(a, cb):
    r = a.shape[0]
    return a.reshape(r, 3, HY_W // cb, cb).transpose(2, 0, 1, 3).reshape(HY_W // cb, r, 3 * cb)


def kernel(x_prompt, x_sample, state_l0_lru, cache_l1_ckv, cache_l1_krope, c, c_ctx, l0_norm1, l0_norm2, l0_w_mod, l0_b_mod, l0_w_in, l0_conv_a, l0_lru_conv_w, l0_lru_conv_b, l0_lru_wa, l0_lru_ba, l0_lru_wi, l0_lru_bi, l0_lru_lambda, l0_w_out, l0_ffn_gate, l0_ffn_up, l0_ffn_down, l1_norm1, l1_norm2, l1_w_mod, l1_b_mod, l1_w_in, l1_q_norm, l1_kv_norm, l1_w_q_up, l1_w_kv_up, l1_hy_short_w, l1_hy_short_b, l1_hy_f_w1, l1_hy_f_b1, l1_hy_f_w2, l1_hy_f_b2, l1_hy_f_w3, l1_hy_bias, l1_w_out, l1_router_w, l1_router_b, l1_exp_gate, l1_exp_up, l1_exp_down, final_norm):
    batch, seq, _ = x_prompt.shape
    dec_batch, dec_seq, _ = x_sample.shape
    past_len = cache_l1_ckv.shape[1]

    cond8 = jnp.concatenate([c_ctx[None, :], c, jnp.zeros((V7X_SUBLANES - 1 - dec_batch, D), F32)], axis=0)
    wcat = jnp.concatenate([l0_lru_wa[0], l0_lru_wi[0], l0_lru_wa[1], l0_lru_wi[1]], axis=-1)
    wq = l1_w_q_up.reshape(Q_RANK, MLA_HEADS, QK_DIM)
    wq_perm = jnp.concatenate([wq[:, :, :NOPE].reshape(Q_RANK, MLA_HEADS * NOPE),
                               wq[:, :, NOPE:].reshape(Q_RANK, MLA_HEADS * ROPE)], axis=1)
    hid = V7X_LANES
    w1p = _pad_to(l1_hy_f_w1, (hid, hid))
    b1p = _pad_to(l1_hy_f_b1.reshape(1, -1), (1, hid))
    w2p = _pad_to(l1_hy_f_w2, (hid, hid))
    b2p = _pad_to(l1_hy_f_b2.reshape(1, -1), (1, hid))
    w3p = _pad_to(l1_hy_f_w3, (hid, 2 * HY_W))
    short_w = _regroup_chunks(l1_hy_short_w, HY_CB)
    short_b = _regroup_chunks(l1_hy_short_b.reshape(1, -1), HY_CB)
    hy_bias = l1_hy_bias.reshape(1, HY_W)
    wr_pad = _pad_to(l1_router_w, (D, V7X_LANES))
    br_pad = _pad_to(l1_router_b.reshape(1, -1), (1, V7X_LANES))

    mod0 = adaln_table(cond8, l0_w_mod, l0_b_mod)
    mod1 = adaln_table(cond8, l1_w_mod, l1_b_mod)

    kv_ctx = kv_up(cache_l1_ckv.reshape(dec_batch * past_len, KV_RANK), l1_w_kv_up)
    kr_ctx = cache_l1_krope.reshape(dec_batch * past_len, ROPE)

    def trunk(x, seq_len, cond, h0, latent):
        u = in0_proj(x, l0_norm1, mod0, cond, l0_w_in)
        ya = conv_a(u, seq_len, l0_conv_a)
        yb, lru_state = rglru(u, seq_len, l0_lru_conv_w, l0_lru_conv_b, wcat, l0_lru_ba, l0_lru_bi,
                              l0_lru_lambda, h0)
        x = out_res([(ya, 0), (yb, 0), (yb, 1)], l0_w_out, x, mod0, cond, 2)
        x = ffn_res(x, l0_norm2, mod0, cond, l0_ffn_gate, l0_ffn_up, l0_ffn_down)
        qnope, qpe, ckv, kr, kv, uh = in1_proj(x, l1_norm1, mod1, cond, l1_w_in, l1_q_norm, l1_kv_norm,
                                               wq_perm, l1_w_kv_up)
        if latent:
            yc = attn_lat(qnope, qpe, kv_ctx, kr_ctx, kv, kr, seq_len, past_len)
        else:
            yc = attn_ctx(qnope, qpe, kv, kr, seq_len)
        cs = dft_tables(seq_len)
        k_r, k_s, k_ny = hy_filter(cs, w1p, b1p, w2p, b2p, w3p)
        yd = hyena(uh, seq_len, short_w, short_b, cs, k_r, k_s, k_ny, hy_bias)
        x = out_res([(yc, 0), (yc, 1), (yd, 0)], l1_w_out, x, mod1, cond, 2)
        return x, lru_state, ckv, kr

    conds = ((0, batch * seq), (1, dec_seq))
    zeros_state = jnp.zeros((batch, 2, LRU_W), F32)
    x_p, new_lru, new_ckv, new_kr = trunk(x_prompt.reshape(batch * seq, D), seq, conds[0], zeros_state, latent=False)
    x_s, _, _, _ = trunk(x_sample.reshape(dec_batch * dec_seq, D), dec_seq, conds[1], state_l0_lru, latent=True)

    xs = (x_p, x_s)
    routed = [moe_route(x, l1_norm2, mod1, cond, wr_pad, br_pad) for x, cond in zip(xs, conds)]
    pos, tile_expert, stages, n_used, n_rows = moe_plan([r[1] for r in routed], [r[2] for r in routed])
    hs = jnp.zeros((n_rows, D), F32)
    for p, r in zip(pos, routed):
        hs = moe_dispatch(p, r[0], hs)
    y_rows = moe_experts(tile_expert, stages, n_used, hs, l1_exp_gate, l1_exp_up, l1_exp_down)
    y_p, y_s = [moe_combine(p, x, r[1], mod1, cond, final_norm, y_rows)
                for p, x, r, cond in zip(pos, xs, routed, conds)]
    return (y_p.reshape(batch, seq, D), y_s.reshape(dec_batch, dec_seq, D), new_lru,
            new_ckv.reshape(batch, seq, KV_RANK), new_kr.reshape(batch, seq, ROPE))
```

```python
import functools
import math

import jax
import jax.numpy as jnp
from jax import lax
from jax.experimental import pallas as pl
from jax.experimental.pallas import tpu as pltpu

F32 = jnp.float32
BF16 = jnp.bfloat16
HIGHEST = lax.Precision.HIGHEST

D = 1024
GRID_W = 64
EPS = 1e-6
CONV_W = 512
LRU_W = 1024
LRU_BW = 128
LRU_C = 8.0
MLA_HEADS = 8
Q_RANK = 384
KV_RANK = 256
NOPE = 128
ROPE = 64
VDIM = 128
QK_DIM = NOPE + ROPE
ROPE_THETA = 10000.0
HY_W = 512
HY_BANDS = 16
HY_TARGET = 1e-2
HY_FAST_DECAY = 0.3
HY_SLOW_DECAY = 1.5
D_FF = 2816
N_EXPERTS = 8
D_FF_EXPERT = 1408
IN0 = 3 * CONV_W + 2 * LRU_W
IN1 = Q_RANK + KV_RANK + ROPE + 3 * HY_W

V7X_LANES = 128
V7X_SUBLANES = 8
V7X_VMEM_LIMIT_BYTES = 56 * 1024 * 1024

TM = 512
TN_IN0 = 512
TF_FFN = 256
MOE_CHUNK = 256
TM_ROUTE = 512
TM_EXPERT = 256
TM_COMBINE = 256
LRU_CB = 256
HY_CB = 256
TQ = 256
ATTN_CTX_SEQS = 4
CONV_A_ROWS = 1024
TM_IN1 = 256


def _cparams(sem):
    return pltpu.CompilerParams(dimension_semantics=sem, vmem_limit_bytes=V7X_VMEM_LIMIT_BYTES)


def _sigmoid(x):
    return 0.5 * jnp.tanh(0.5 * x) + 0.5


def _silu(x):
    return x * _sigmoid(x)


def _norm_mod(x, g, shift, scale):
    ms = jnp.mean(x * x, axis=-1, keepdims=True)
    y = x * lax.rsqrt(ms + EPS) * g
    return y * (1.0 + scale) + shift


def _mod_spec(comp, cond, tm, width, col_fn, tile_fn=lambda *ids: ids[0]):
    row0, seg = cond
    assert seg % tm == 0
    return pl.BlockSpec((None, 1, width),
                        lambda *ids: (comp * 3 + row0 + (tile_fn(*ids) * tm) // seg, 0, col_fn(*ids)))


def _adaln_kernel(c_ref, w_ref, b_ref, o_ref):
    a = _silu(c_ref[...])
    w = w_ref[...]
    a_hi = a.astype(BF16)
    a_lo = (a - a_hi.astype(F32)).astype(BF16)
    w_hi = w.astype(BF16)
    w_lo = (w - w_hi.astype(F32)).astype(BF16)
    n = a.shape[0]
    y = jnp.dot(jnp.concatenate([a_hi, a_lo], axis=0), w_hi, preferred_element_type=F32)
    y = y[:n] + y[n:] + jnp.dot(a_hi, w_lo, preferred_element_type=F32)
    o_ref[...] = y + b_ref[...]


def adaln_table(cond8, w_mod, b_mod):
    tn = 1536
    m = pl.pallas_call(
        _adaln_kernel,
        out_shape=jax.ShapeDtypeStruct((V7X_SUBLANES, 6 * D), F32),
        grid=(6 * D // tn,),
        in_specs=[pl.BlockSpec((V7X_SUBLANES, D), lambda j: (0, 0)),
                  pl.BlockSpec((D, tn), lambda j: (0, j)),
                  pl.BlockSpec((1, tn), lambda j: (0, j))],
        out_specs=pl.BlockSpec((V7X_SUBLANES, tn), lambda j: (0, j)),
        compiler_params=_cparams(("arbitrary",)),
        name="adaln",
    )(cond8, w_mod, b_mod.reshape(1, 6 * D))
    return m[:3].reshape(3, 6, D).transpose(1, 0, 2).reshape(18, 1, D)


def _tile_of(n_load):
    return lambda s: jnp.maximum(s - n_load, 0)


def _block_of(n_load):
    return lambda s: jnp.minimum(s, n_load - 1)


def _in0_kernel(x_ref, g_ref, sh_ref, sc_ref, w_ref, o_ref, w_sc):
    s = pl.program_id(0)
    n_load, _, tn = w_sc.shape

    @pl.when(s < n_load)
    def _():
        w_sc[s] = w_ref[...].astype(BF16)

    @pl.when(s >= n_load)
    def _():
        h = _norm_mod(x_ref[...], g_ref[...], sh_ref[...], sc_ref[...]).astype(BF16)
        for j in range(n_load):
            o_ref[:, j * tn:(j + 1) * tn] = jnp.dot(h, w_sc[j], preferred_element_type=F32).astype(o_ref.dtype)


def in0_proj(x, g, modtab, cond, w_in):
    tn = TN_IN0
    tokens = x.shape[0]
    n = w_in.shape[1]
    n_load = n // tn
    tile = _tile_of(n_load)
    blk = _block_of(n_load)
    zero = lambda s: 0
    return pl.pallas_call(
        _in0_kernel,
        out_shape=jax.ShapeDtypeStruct((tokens, n), BF16),
        grid=(n_load + tokens // TM,),
        in_specs=[pl.BlockSpec((TM, D), lambda s: (tile(s), 0)),
                  pl.BlockSpec((1, D), lambda s: (0, 0)),
                  _mod_spec(0, cond, TM, D, zero, tile),
                  _mod_spec(1, cond, TM, D, zero, tile),
                  pl.BlockSpec((D, tn), lambda s: (0, blk(s)))],
        out_specs=pl.BlockSpec((TM, n), lambda s: (tile(s), 0)),
        scratch_shapes=[pltpu.VMEM((n_load, D, tn), BF16)],
        compiler_params=_cparams(("arbitrary",)),
        name="in0_proj",
    )(x, g.reshape(1, D), modtab, modtab, w_in)


def _shift_rows(v, d, t, seq_len=None):
    n = v.shape[0]
    seq_len = n if seq_len is None else seq_len
    if d > 0:
        return jnp.where(t < d, 0.0, pltpu.roll(v, d, 0))
    return jnp.where(t >= seq_len + d, 0.0, pltpu.roll(v, n + d, 0))


def _conv_a_kernel(b_ref, c_ref, x_ref, w_ref, o_ref, *, seq_len):
    v = c_ref[...].astype(F32) * x_ref[...].astype(F32)
    t = lax.broadcasted_iota(jnp.int32, v.shape, 0) & (seq_len - 1)
    w = w_ref[...]
    y = w[0:1] * _shift_rows(v, 1, t, seq_len) + w[1:2] * v + w[2:3] * _shift_rows(v, -1, t, seq_len)
    o_ref[...] = (b_ref[...].astype(F32) * y).astype(o_ref.dtype)


def conv_a(u, seq_len, conv_w):
    tokens = u.shape[0]
    rows = max(seq_len, CONV_A_ROWS)
    assert seq_len & (seq_len - 1) == 0 and rows % seq_len == 0
    return pl.pallas_call(
        functools.partial(_conv_a_kernel, seq_len=seq_len),
        out_shape=jax.ShapeDtypeStruct((tokens, CONV_W), BF16),
        grid=(tokens // rows,),
        in_specs=[pl.BlockSpec((rows, CONV_W), lambda s: (s, 0)),
                  pl.BlockSpec((rows, CONV_W), lambda s: (s, 1)),
                  pl.BlockSpec((rows, CONV_W), lambda s: (s, 2)),
                  pl.BlockSpec((3, CONV_W), lambda s: (0, 0))],
        out_specs=pl.BlockSpec((rows, CONV_W), lambda s: (s, 0)),
        compiler_params=_cparams(("parallel",)),
        name="conv_a",
    )(u, u, u, conv_w)


def _group_scan(a, b, reverse):
    n, c = a.shape
    a3 = a.reshape(n // V7X_SUBLANES, V7X_SUBLANES, c)
    b3 = b.reshape(n // V7X_SUBLANES, V7X_SUBLANES, c)
    t8 = lax.broadcasted_iota(jnp.int32, a3.shape, 1)
    for d in (1, 2, 4):
        if reverse:
            keep = t8 < V7X_SUBLANES - d
            shift = V7X_SUBLANES - d
        else:
            keep = t8 >= d
            shift = d
        a_sh = jnp.where(keep, pltpu.roll(a3, shift, 1), 1.0)
        b_sh = jnp.where(keep, pltpu.roll(b3, shift, 1), 0.0)
        b3 = a3 * b_sh + b3
        a3 = a3 * a_sh
    return a3.reshape(n, c), b3.reshape(n, c)


def _rglru_kernel(gate_ref, xb_ref, cw_ref, cb_ref, wcat_ref, ba_ref, bi_ref, lam_ref, h0_ref,
                  y_ref, st_ref, af_sc, bf_sc, ab_sc, bb_sc, hf_sc, hb_sc):
    n, cb = xb_ref.shape
    xb = xb_ref[...].astype(F32)
    t = lax.broadcasted_iota(jnp.int32, xb.shape, 0)
    cw = cw_ref[...]
    xc = (cb_ref[...] + cw[0:1] * _shift_rows(xb, 2, t) + cw[1:2] * _shift_rows(xb, 1, t)
          + cw[2:3] * xb + cw[3:4] * _shift_rows(xb, -1, t))
    xcb = xc.astype(BF16)
    g = [jnp.dot(xcb[:, k * LRU_BW:(k + 1) * LRU_BW], wcat_ref[k].astype(BF16), preferred_element_type=F32)
         for k in range(cb // LRU_BW)]

    def direction(d):
        ga = jnp.concatenate([gk[:, (2 * d) * LRU_BW:(2 * d + 1) * LRU_BW] for gk in g], axis=1)
        gi = jnp.concatenate([gk[:, (2 * d + 1) * LRU_BW:(2 * d + 2) * LRU_BW] for gk in g], axis=1)
        r = _sigmoid(ga + ba_ref[d:d + 1, :])
        i = _sigmoid(gi + bi_ref[d:d + 1, :])
        log_a = (-LRU_C * jax.nn.softplus(-lam_ref[d:d + 1, :])) * r
        a = jnp.exp(log_a)
        m = 1.0 - a * a
        mult = m * lax.rsqrt(jnp.maximum(m, 1e-30))
        return a, mult * (i * xc)

    a_f, b_f = direction(0)
    a_f, b_f = _group_scan(a_f, b_f, reverse=False)
    af_sc[...] = a_f
    bf_sc[...] = b_f
    a_b, b_b = direction(1)
    a_b, b_b = _group_scan(a_b, b_b, reverse=True)
    ab_sc[...] = a_b
    bb_sc[...] = b_b

    ng = n // V7X_SUBLANES
    h0 = h0_ref[...]
    init = (jnp.broadcast_to(h0[0:1], (V7X_SUBLANES, cb)), jnp.broadcast_to(h0[1:2], (V7X_SUBLANES, cb)))

    def step(k, carry):
        hf_in, hb_in = carry
        rf = pl.multiple_of(k * V7X_SUBLANES, V7X_SUBLANES)
        rb = pl.multiple_of((ng - 1 - k) * V7X_SUBLANES, V7X_SUBLANES)
        hf = af_sc[pl.ds(rf, V7X_SUBLANES), :] * hf_in + bf_sc[pl.ds(rf, V7X_SUBLANES), :]
        hb = ab_sc[pl.ds(rb, V7X_SUBLANES), :] * hb_in + bb_sc[pl.ds(rb, V7X_SUBLANES), :]
        hf_sc[pl.ds(rf, V7X_SUBLANES), :] = hf
        hb_sc[pl.ds(rb, V7X_SUBLANES), :] = hb
        return (jnp.broadcast_to(hf[V7X_SUBLANES - 1:V7X_SUBLANES], hf.shape), jnp.broadcast_to(hb[0:1], hb.shape))

    hf_last, hb_first = lax.fori_loop(0, ng, step, init)
    st_ref[0:1, :] = hf_last[0:1]
    st_ref[1:2, :] = hb_first[0:1]

    gt = gate_ref[...].astype(F32)
    gelu = 0.5 * gt * (1.0 + jnp.tanh(math.sqrt(2.0 / math.pi) * (gt + 0.044715 * (gt * gt * gt))))
    y_ref[...] = ((hf_sc[...] + hb_sc[...]) * gelu).astype(y_ref.dtype)


def rglru(u, seq_len, conv_w, conv_b, wcat, ba, bi, lam, h0):
    tokens = u.shape[0]
    nseq = tokens // seq_len
    cb = LRU_CB
    gate_blk0 = 3 * CONV_W // cb
    xb_blk0 = (3 * CONV_W + LRU_W) // cb
    seq_scr = lambda: pltpu.VMEM((seq_len, cb), F32)
    return pl.pallas_call(
        _rglru_kernel,
        out_shape=(jax.ShapeDtypeStruct((tokens, LRU_W), BF16), jax.ShapeDtypeStruct((nseq, 2, LRU_W), F32)),
        grid=(nseq, LRU_W // cb),
        in_specs=[pl.BlockSpec((seq_len, cb), lambda s, c: (s, gate_blk0 + c)),
                  pl.BlockSpec((seq_len, cb), lambda s, c: (s, xb_blk0 + c)),
                  pl.BlockSpec((4, cb), lambda s, c: (0, c)),
                  pl.BlockSpec((1, cb), lambda s, c: (0, c)),
                  pl.BlockSpec((cb // LRU_BW, LRU_BW, 4 * LRU_BW), lambda s, c: (c, 0, 0)),
                  pl.BlockSpec((2, cb), lambda s, c: (0, c)),
                  pl.BlockSpec((2, cb), lambda s, c: (0, c)),
                  pl.BlockSpec((2, cb), lambda s, c: (0, c)),
                  pl.BlockSpec((None, 2, cb), lambda s, c: (s, 0, c))],
        out_specs=(pl.BlockSpec((seq_len, cb), lambda s, c: (s, c)),
                   pl.BlockSpec((None, 2, cb), lambda s, c: (s, 0, c))),
        scratch_shapes=[seq_scr() for _ in range(6)],
        compiler_params=_cparams(("parallel", "parallel")),
        name="rglru",
    )(u, u, conv_w, conv_b.reshape(1, LRU_W), wcat, ba, bi, lam, h0)


def _out_res_kernel(p0_ref, p1_ref, p2_ref, w_ref, x_ref, gt_ref, o_ref, w_sc):
    s = pl.program_id(0)
    n_load, kb, _ = w_sc.shape

    @pl.when(s < n_load)
    def _():
        w_sc[s] = w_ref[...].astype(BF16)

    @pl.when(s >= n_load)
    def _():
        y = jnp.dot(p0_ref[...], w_sc[0], preferred_element_type=F32)
        y += jnp.dot(p1_ref[...], w_sc[1], preferred_element_type=F32)
        y += jnp.dot(p2_ref[...], w_sc[2], preferred_element_type=F32)
        o_ref[...] = x_ref[...] + gt_ref[...] * y


def out_res(parts, w_out, x, modtab, cond, gate_comp):
    tokens = x.shape[0]
    kb = 512
    n_load = len(parts)
    tile = _tile_of(n_load)
    blk = _block_of(n_load)
    lhs_specs = [pl.BlockSpec((TM, kb), (lambda s, cbk=cbk: (tile(s), cbk))) for _, cbk in parts]
    return pl.pallas_call(
        _out_res_kernel,
        out_shape=jax.ShapeDtypeStruct((tokens, D), F32),
        grid=(n_load + tokens // TM,),
        in_specs=lhs_specs + [pl.BlockSpec((kb, D), lambda s: (blk(s), 0)),
                              pl.BlockSpec((TM, D), lambda s: (tile(s), 0)),
                              _mod_spec(gate_comp, cond, TM, D, lambda s: 0, tile)],
        out_specs=pl.BlockSpec((TM, D), lambda s: (tile(s), 0)),
        scratch_shapes=[pltpu.VMEM((n_load, kb, D), BF16)],
        compiler_params=_cparams(("arbitrary",)),
        name="out_res",
    )(*[a for a, _ in parts], w_out, x, modtab)


def _ffn_kernel(x_ref, g_ref, sh_ref, sc_ref, gt_ref, wg_ref, wu_ref, wd_ref, o_ref, wg_sc, wu_sc, wd_sc):
    s = pl.program_id(0)
    n_load = wg_sc.shape[0]

    @pl.when(s < n_load)
    def _():
        wg_sc[s] = wg_ref[...].astype(BF16)
        wu_sc[s] = wu_ref[...].astype(BF16)
        wd_sc[s] = wd_ref[...].astype(BF16)

    @pl.when(s >= n_load)
    def _():
        x = x_ref[...]
        h = _norm_mod(x, g_ref[...], sh_ref[...], sc_ref[...]).astype(BF16)
        y = None
        for f in range(n_load):
            hg = jnp.dot(h, wg_sc[f], preferred_element_type=F32)
            hu = jnp.dot(h, wu_sc[f], preferred_element_type=F32)
            act = (_silu(hg) * hu).astype(BF16)
            yf = jnp.dot(act, wd_sc[f], preferred_element_type=F32)
            y = yf if y is None else y + yf
        o_ref[...] = x + gt_ref[...] * y


def ffn_res(x, g, modtab, cond, w_gate, w_up, w_down):
    tokens = x.shape[0]
    tf = TF_FFN
    n_load = D_FF // tf
    tile = _tile_of(n_load)
    blk = _block_of(n_load)
    zero = lambda s: 0
    return pl.pallas_call(
        _ffn_kernel,
        out_shape=jax.ShapeDtypeStruct((tokens, D), F32),
        grid=(n_load + tokens // TM,),
        in_specs=[pl.BlockSpec((TM, D), lambda s: (tile(s), 0)),
                  pl.BlockSpec((1, D), lambda s: (0, 0)),
                  _mod_spec(3, cond, TM, D, zero, tile),
                  _mod_spec(4, cond, TM, D, zero, tile),
                  _mod_spec(5, cond, TM, D, zero, tile),
                  pl.BlockSpec((D, tf), lambda s: (0, blk(s))),
                  pl.BlockSpec((D, tf), lambda s: (0, blk(s))),
                  pl.BlockSpec((tf, D), lambda s: (blk(s), 0))],
        out_specs=pl.BlockSpec((TM, D), lambda s: (tile(s), 0)),
        scratch_shapes=[pltpu.VMEM((n_load, D, tf), BF16), pltpu.VMEM((n_load, D, tf), BF16),
                        pltpu.VMEM((n_load, tf, D), BF16)],
        compiler_params=_cparams(("arbitrary",)),
        name="ffn_res",
    )(x, g.reshape(1, D), modtab, modtab, modtab, w_gate, w_up, w_down)


def _rms(x, g):
    return x * lax.rsqrt(jnp.mean(x * x, axis=-1, keepdims=True) + EPS) * g


def _in1_kernel(x_ref, g_ref, sh_ref, sc_ref, w_ref, qn_ref, kvn_ref, wq_ref, wkv_ref,
                qnope_ref, qpe_ref, ckv_ref, kr_ref, kv_ref, uh_ref, w_sc, wq_sc, wkv_sc):
    @pl.when(pl.program_id(0) == 0)
    def _():
        w_sc[...] = w_ref[...].astype(BF16)
        wq_sc[...] = wq_ref[...].astype(BF16)
        wkv_sc[...] = wkv_ref[...].astype(BF16)

    h = _norm_mod(x_ref[...], g_ref[...], sh_ref[...], sc_ref[...]).astype(BF16)
    u = jnp.dot(h, w_sc[...], preferred_element_type=F32)
    o1, o2, o3 = Q_RANK, Q_RANK + KV_RANK, Q_RANK + KV_RANK + ROPE
    cq = _rms(u[:, :o1], qn_ref[...])
    q = jnp.dot(cq.astype(BF16), wq_sc[...], preferred_element_type=F32) * _SCALE
    qnope_ref[...] = q[:, :MLA_HEADS * NOPE].astype(qnope_ref.dtype)
    qpe_ref[...] = q[:, MLA_HEADS * NOPE:]
    ckv = _rms(u[:, o1:o2], kvn_ref[...])
    ckv_ref[...] = ckv
    kv_ref[...] = jnp.dot(ckv.astype(BF16), wkv_sc[...], preferred_element_type=F32).astype(kv_ref.dtype)
    kr_ref[...] = u[:, o2:o3]
    uh_ref[...] = u[:, o3:]


def in1_proj(x, g, modtab, cond, w_in, q_norm, kv_norm, wq_perm, w_kv_up):
    tokens = x.shape[0]
    tm = TM_IN1
    nkv = MLA_HEADS * (NOPE + VDIM)
    const = lambda i: (0, 0)
    zero = lambda i: 0
    once = pl.Buffered(1)
    outs = (jax.ShapeDtypeStruct((tokens, MLA_HEADS * NOPE), BF16),
            jax.ShapeDtypeStruct((tokens, MLA_HEADS * ROPE), F32),
            jax.ShapeDtypeStruct((tokens, KV_RANK), F32),
            jax.ShapeDtypeStruct((tokens, ROPE), F32),
            jax.ShapeDtypeStruct((tokens, nkv), BF16),
            jax.ShapeDtypeStruct((tokens, 3 * HY_W), F32))
    row = lambda w: pl.BlockSpec((tm, w), lambda i: (i, 0))
    return pl.pallas_call(
        _in1_kernel,
        out_shape=outs,
        grid=(tokens // tm,),
        in_specs=[row(D),
                  pl.BlockSpec((1, D), const),
                  _mod_spec(0, cond, tm, D, zero),
                  _mod_spec(1, cond, tm, D, zero),
                  pl.BlockSpec((D, IN1), const, pipeline_mode=once),
                  pl.BlockSpec((1, Q_RANK), const),
                  pl.BlockSpec((1, KV_RANK), const),
                  pl.BlockSpec((Q_RANK, MLA_HEADS * QK_DIM), const, pipeline_mode=once),
                  pl.BlockSpec((KV_RANK, nkv), const, pipeline_mode=once)],
        out_specs=tuple(row(o.shape[1]) for o in outs),
        scratch_shapes=[pltpu.VMEM((D, IN1), BF16), pltpu.VMEM((Q_RANK, MLA_HEADS * QK_DIM), BF16),
                        pltpu.VMEM((KV_RANK, nkv), BF16)],
        compiler_params=_cparams(("arbitrary",)),
        name="in1_proj",
    )(x, g.reshape(1, D), modtab, modtab, w_in, q_norm.reshape(1, Q_RANK), kv_norm.reshape(1, KV_RANK),
      wq_perm, w_kv_up)


def _mm_kernel(a_ref, w_ref, o_ref):
    o_ref[...] = jnp.dot(a_ref[...].astype(BF16), w_ref[...].astype(BF16),
                         preferred_element_type=F32).astype(o_ref.dtype)


def kv_up(ckv, w_kv_up):
    rows = ckv.shape[0]
    n = w_kv_up.shape[1]
    return pl.pallas_call(
        _mm_kernel,
        out_shape=jax.ShapeDtypeStruct((rows, n), BF16),
        grid=(rows // TM,),
        in_specs=[pl.BlockSpec((TM, KV_RANK), lambda i: (i, 0)), pl.BlockSpec((KV_RANK, n), lambda i: (0, 0))],
        out_specs=pl.BlockSpec((TM, n), lambda i: (i, 0)),
        compiler_params=_cparams(("parallel",)),
        name="kv_up",
    )(ckv, w_kv_up)


_NT = (((1,), (1,)), ((), ()))
_SCALE = 1.0 / math.sqrt(QK_DIM)


def _rope_tables(pos):
    n = pos.shape[0]
    lane = lax.broadcasted_iota(jnp.int32, (n, ROPE), 1)
    j = lane & (ROPE // 2 - 1)
    n_freq = ROPE // 4
    inv = jnp.exp((j & (n_freq - 1)).astype(F32) * (-math.log(ROPE_THETA) / n_freq))
    p = jnp.where(j < n_freq, pos >> (GRID_W.bit_length() - 1), pos & (GRID_W - 1)).astype(F32)
    ang = p * inv
    return jnp.cos(ang), jnp.sin(ang)


def _rot_half_matrix():
    i = lax.broadcasted_iota(jnp.int32, (ROPE, ROPE), 0)
    j = lax.broadcasted_iota(jnp.int32, (ROPE, ROPE), 1)
    half = ROPE // 2
    return jnp.where(i == j + half, -1.0, jnp.where(i + half == j, 1.0, 0.0)).astype(F32)


def _rope(x, cos, sin, rot):
    xr = jnp.dot(x, rot, preferred_element_type=F32, precision=HIGHEST)
    return x * cos + xr * sin


def _ones_column(n):
    lane = lax.broadcasted_iota(jnp.int32, (n, VDIM), 1)
    return jnp.where(lane == 0, 1.0, 0.0).astype(BF16)


def _head_attention(qcat, kcat, vaug):
    s = lax.dot_general(qcat, kcat, _NT, preferred_element_type=F32)
    p = jnp.exp(s - jnp.max(s, axis=-1, keepdims=True)).astype(BF16)
    oa = jnp.dot(p, vaug, preferred_element_type=F32)
    return oa[:, :VDIM] / oa[:, VDIM:VDIM + 1]


def _attn_ctx_kernel(qn_ref, qpe_ref, kv_ref, kr_ref, o_ref, *, seq_len):
    ones = _ones_column(seq_len)
    for s0 in range(0, qn_ref.shape[0], seq_len):
        rows = slice(s0, s0 + seq_len)
        kpe = kr_ref[rows, :].astype(BF16)
        for h in range(MLA_HEADS):
            c0 = h * (NOPE + VDIM)
            qcat = jnp.concatenate([qn_ref[rows, h * NOPE:(h + 1) * NOPE],
                                    qpe_ref[rows, h * ROPE:(h + 1) * ROPE].astype(BF16)], axis=1)
            kcat = jnp.concatenate([kv_ref[rows, c0:c0 + NOPE], kpe], axis=1)
            vaug = jnp.concatenate([kv_ref[rows, c0 + NOPE:c0 + NOPE + VDIM], ones], axis=1)
            o_ref[rows, h * VDIM:(h + 1) * VDIM] = _head_attention(qcat, kcat, vaug).astype(o_ref.dtype)


def attn_ctx(qnope, qpe, kv, kr, seq_len):
    tokens = qnope.shape[0]
    rows = ATTN_CTX_SEQS * seq_len
    blk = lambda w: pl.BlockSpec((rows, w), lambda s: (s, 0))
    return pl.pallas_call(
        functools.partial(_attn_ctx_kernel, seq_len=seq_len),
        out_shape=jax.ShapeDtypeStruct((tokens, MLA_HEADS * VDIM), BF16),
        grid=(tokens // rows,),
        in_specs=[blk(MLA_HEADS * NOPE), blk(MLA_HEADS * ROPE), blk(MLA_HEADS * (NOPE + VDIM)), blk(ROPE)],
        out_specs=blk(MLA_HEADS * VDIM),
        compiler_params=_cparams(("parallel",)),
        name="attn_ctx",
    )(qnope, qpe, kv, kr)


def _attn_lat_kernel(qn_ref, qpe_ref, kvc_ref, krc_ref, kvl_ref, krl_ref, o_ref, kcat_sc, vaug_sc):
    tq = qn_ref.shape[0]
    n_ctx = krc_ref.shape[0]
    n_lat = krl_ref.shape[0]
    rot = _rot_half_matrix()

    @pl.when(pl.program_id(1) == 0)
    def _():
        ck, sk = _rope_tables(lax.broadcasted_iota(jnp.int32, (n_lat, 1), 0))
        kpe_lat = _rope(krl_ref[...], ck, sk, rot).astype(BF16)
        kpe_ctx = krc_ref[...].astype(BF16)
        ones_c, ones_l = _ones_column(n_ctx), _ones_column(n_lat)
        for h in range(MLA_HEADS):
            c0 = h * (NOPE + VDIM)
            for r0, nr, kv_ref, kpe, ones in ((0, n_ctx, kvc_ref, kpe_ctx, ones_c), (n_ctx, n_lat, kvl_ref, kpe_lat, ones_l)):
                kcat_sc[h, r0:r0 + nr, 0:NOPE] = kv_ref[:, c0:c0 + NOPE]
                kcat_sc[h, r0:r0 + nr, NOPE:QK_DIM] = kpe
                vaug_sc[h, r0:r0 + nr, 0:VDIM] = kv_ref[:, c0 + NOPE:c0 + NOPE + VDIM]
                vaug_sc[h, r0:r0 + nr, VDIM:2 * VDIM] = ones

    q0 = pl.program_id(1) * tq
    cq, sq = _rope_tables(q0 + lax.broadcasted_iota(jnp.int32, (tq, 1), 0))
    for h in range(MLA_HEADS):
        qp = _rope(qpe_ref[:, h * ROPE:(h + 1) * ROPE], cq, sq, rot).astype(BF16)
        qcat = jnp.concatenate([qn_ref[:, h * NOPE:(h + 1) * NOPE], qp], axis=1)
        o_ref[:, h * VDIM:(h + 1) * VDIM] = _head_attention(qcat, kcat_sc[h], vaug_sc[h]).astype(o_ref.dtype)


def attn_lat(qnope, qpe, kv_ctx, kr_ctx, kv_lat, kr_lat, seq_len, ctx_len):
    tokens = qnope.shape[0]
    nq = seq_len // TQ
    qblk = lambda w: pl.BlockSpec((TQ, w), lambda b, i: (b * nq + i, 0))
    seq = lambda n, w: pl.BlockSpec((n, w), lambda b, i: (b, 0))
    nkv = MLA_HEADS * (NOPE + VDIM)
    n_keys = ctx_len + seq_len
    return pl.pallas_call(
        _attn_lat_kernel,
        out_shape=jax.ShapeDtypeStruct((tokens, MLA_HEADS * VDIM), BF16),
        grid=(tokens // seq_len, nq),
        in_specs=[qblk(MLA_HEADS * NOPE), qblk(MLA_HEADS * ROPE), seq(ctx_len, nkv), seq(ctx_len, ROPE),
                  seq(seq_len, nkv), seq(seq_len, ROPE)],
        out_specs=qblk(MLA_HEADS * VDIM),
        scratch_shapes=[pltpu.VMEM((MLA_HEADS, n_keys, QK_DIM), BF16),
                        pltpu.VMEM((MLA_HEADS, n_keys, 2 * VDIM), BF16)],
        compiler_params=_cparams(("parallel", "arbitrary")),
        name="attn_lat",
    )(qnope, qpe, kv_ctx, kr_ctx, kv_lat, kr_lat)


def _dft_kernel(o_ref):
    tr, n = o_ref.shape[1], o_ref.shape[2]
    f = pl.program_id(0) * tr + lax.broadcasted_iota(jnp.int32, (tr, n), 0)
    s = lax.broadcasted_iota(jnp.int32, (tr, n), 1)
    ang = ((f * s) & (2 * n - 1)).astype(F32) * (math.pi / n)
    o_ref[0] = jnp.cos(ang).astype(o_ref.dtype)
    o_ref[1] = jnp.sin(ang).astype(o_ref.dtype)


def dft_tables(n):
    tr = 128
    return pl.pallas_call(
        _dft_kernel,
        out_shape=jax.ShapeDtypeStruct((2, n, n), BF16),
        grid=(n // tr,),
        out_specs=pl.BlockSpec((2, tr, n), lambda i: (0, i, 0)),
        compiler_params=_cparams(("parallel",)),
        name="dft_tables",
    )()


def _split_dot(table, x):
    hi = x.astype(BF16)
    lo = (x - hi.astype(F32)).astype(BF16)
    return (jnp.dot(table, hi, preferred_element_type=F32) + jnp.dot(table, lo, preferred_element_type=F32))


def _hy_filter_kernel(cs_ref, w1_ref, b1_ref, w2_ref, b2_ref, w3_ref, kr_ref, ks_ref, kny_ref):
    n = cs_ref.shape[1]
    row = lax.broadcasted_iota(jnp.int32, (n, V7X_LANES), 0).astype(F32)
    lane = lax.broadcasted_iota(jnp.int32, (n, V7X_LANES), 1)
    t = row * (1.0 / (n - 1))
    w = (2.0 * math.pi) * row / n
    band = jnp.where(lane <= HY_BANDS, lane - 1, lane - 1 - HY_BANDS).astype(F32)
    freq = 1e-4 + band * ((HY_BANDS - 1 - 1e-4) / (HY_BANDS - 1))
    z = jnp.where(lane == 0, t,
                  jnp.where(lane <= HY_BANDS, jnp.cos(freq * w),
                            jnp.where(lane <= 2 * HY_BANDS, -jnp.sin(freq * w), 0.0)))
    hid = jnp.sin(jnp.dot(z, w1_ref[...], preferred_element_type=F32, precision=HIGHEST) + b1_ref[...])
    hid = jnp.sin(jnp.dot(hid, w2_ref[...], preferred_element_type=F32, precision=HIGHEST) + b2_ref[...])
    hf = jnp.dot(hid, w3_ref[...], preferred_element_type=F32, precision=HIGHEST)

    rowc = lax.broadcasted_iota(jnp.int32, (n, HY_W), 0)
    chan = lax.broadcasted_iota(jnp.int32, (n, HY_W), 1).astype(F32)
    max_decay = math.log(HY_TARGET) / HY_FAST_DECAY
    min_decay = math.log(HY_TARGET) / HY_SLOW_DECAY
    deltas = min_decay + chan * ((max_decay - min_decay) / (HY_W - 1))
    decay = jnp.exp(-(rowc.astype(F32) * (1.0 / (n - 1))) * jnp.abs(deltas))
    h_fwd = hf[:, :HY_W] * decay
    h_bwd = jnp.where(rowc == 0, 0.0, hf[:, HY_W:] * decay)
    norm = jnp.sum(jnp.abs(h_fwd) + jnp.abs(h_bwd), axis=0, keepdims=True)
    even = (h_fwd + h_bwd) / norm
    odd = (h_fwd - h_bwd) / norm
    cf = jnp.where(rowc == 0, 1.0, 2.0) * (1.0 / (2 * n))
    kr_ref[...] = cf * _split_dot(cs_ref[0], even)
    ks_ref[...] = cf * _split_dot(cs_ref[1], odd)
    sgn = jnp.where((rowc & 1) == 1, -1.0, 1.0)
    kny_ref[...] = jnp.sum(sgn * even, axis=0, keepdims=True) * (1.0 / (2 * n))


def hy_filter(cs, w1p, b1p, w2p, b2p, w3p):
    n = cs.shape[1]
    full = lambda a: pl.BlockSpec(a.shape, lambda: (0,) * a.ndim)
    args = (cs, w1p, b1p, w2p, b2p, w3p)
    return pl.pallas_call(
        _hy_filter_kernel,
        out_shape=(jax.ShapeDtypeStruct((n, HY_W), F32), jax.ShapeDtypeStruct((n, HY_W), F32),
                   jax.ShapeDtypeStruct((1, HY_W), F32)),
        in_specs=[full(a) for a in args],
        out_specs=(pl.BlockSpec((n, HY_W), lambda: (0, 0)), pl.BlockSpec((n, HY_W), lambda: (0, 0)),
                   pl.BlockSpec((1, HY_W), lambda: (0, 0))),
        compiler_params=pltpu.CompilerParams(vmem_limit_bytes=V7X_VMEM_LIMIT_BYTES),
        name="hy_filter",
    )(*args)


def _hyena_kernel(u0_ref, u1_ref, u2_ref, sw_ref, sb_ref, cs_ref, kr_ref, ks_ref, kny_ref, bias_ref, o_ref):
    n, cb = u0_ref.shape
    t = lax.broadcasted_iota(jnp.int32, (n, cb), 0)

    def short_conv(u_ref, k):
        u = u_ref[...]
        w = sw_ref[:, k * cb:(k + 1) * cb]
        return (sb_ref[:, k * cb:(k + 1) * cb] + w[0:1] * _shift_rows(u, 1, t) + w[1:2] * u
                + w[2:3] * _shift_rows(u, -1, t))

    x0 = short_conv(u0_ref, 0)
    z = short_conv(u1_ref, 1) * short_conv(u2_ref, 2)
    zb = z.astype(BF16)
    c, s = cs_ref[0], cs_ref[1]
    ur = jnp.dot(c, zb, preferred_element_type=F32)
    us = jnp.dot(s, zb, preferred_element_type=F32)
    sgn = jnp.where((t & 1) == 1, -1.0, 1.0)
    uny = jnp.sum(sgn * z, axis=0, keepdims=True)
    kr, ks = kr_ref[...], ks_ref[...]
    yr = (ur * kr - us * ks).astype(BF16)
    ys = (ur * ks + us * kr).astype(BF16)
    y = jnp.dot(c, yr, preferred_element_type=F32) + jnp.dot(s, ys, preferred_element_type=F32)
    y = y + sgn * (uny * kny_ref[...])
    o_ref[...] = (x0 * (y + bias_ref[...] * z)).astype(o_ref.dtype)


def hyena(uh, seq_len, short_w, short_b, cs, kr, ks, kny, bias):
    tokens = uh.shape[0]
    cb = HY_CB
    nc = HY_W // cb
    ublk = lambda k: pl.BlockSpec((seq_len, cb), lambda s, c: (s, k * nc + c))
    chan = lambda rows: pl.BlockSpec((rows, cb), lambda s, c: (0, c))
    return pl.pallas_call(
        _hyena_kernel,
        out_shape=jax.ShapeDtypeStruct((tokens, HY_W), BF16),
        grid=(tokens // seq_len, nc),
        in_specs=[ublk(0), ublk(1), ublk(2),
                  pl.BlockSpec((None, 3, 3 * cb), lambda s, c: (c, 0, 0)),
                  pl.BlockSpec((None, 1, 3 * cb), lambda s, c: (c, 0, 0)),
                  pl.BlockSpec((2, seq_len, seq_len), lambda s, c: (0, 0, 0)),
                  chan(seq_len), chan(seq_len), chan(1), chan(1)],
        out_specs=pl.BlockSpec((seq_len, cb), lambda s, c: (s, c)),
        compiler_params=_cparams(("parallel", "parallel")),
        name="hyena",
    )(uh, uh, uh, short_w, short_b, cs, kr, ks, kny, bias)


META_E1, META_E2, META_R1, META_R2, META_G1, META_G2 = range(6)


def _route_kernel(x_ref, g_ref, sh_ref, sc_ref, wr_ref, br_ref, h_ref, meta_ref, cnt_ref, run_sc):
    tm = x_ref.shape[0]
    lane = lax.broadcasted_iota(jnp.int32, (tm, V7X_LANES), 1)

    @pl.when(pl.program_id(0) == 0)
    def _():
        run_sc[...] = jnp.zeros_like(run_sc)

    h = _norm_mod(x_ref[...], g_ref[...], sh_ref[...], sc_ref[...])
    h_ref[...] = h
    logits = jnp.dot(h, wr_ref[...], preferred_element_type=F32, precision=HIGHEST) + br_ref[...]
    valid = lane < N_EXPERTS
    lg = jnp.where(valid, logits, -jnp.inf)
    ex = jnp.exp(lg - jnp.max(lg, axis=-1, keepdims=True))
    p = ex / jnp.sum(ex, axis=-1, keepdims=True)
    p1 = jnp.max(p, axis=-1, keepdims=True)
    i1 = jnp.min(jnp.where((p == p1) & valid, lane, V7X_LANES), axis=-1, keepdims=True)
    rest = jnp.where((lane == i1) | (~valid), -1.0, p)
    p2 = jnp.max(rest, axis=-1, keepdims=True)
    i2 = jnp.min(jnp.where(rest == p2, lane, V7X_LANES), axis=-1, keepdims=True)
    m1 = lane == i1
    m2 = lane == i2
    chosen = jnp.where(m1 | m2, 1.0, 0.0)
    r = lax.broadcasted_iota(jnp.int32, (tm, tm), 0)
    c = lax.broadcasted_iota(jnp.int32, (tm, tm), 1)
    tri = jnp.where(c < r, 1.0, 0.0).astype(BF16)
    before = jnp.dot(tri, chosen.astype(BF16), preferred_element_type=F32) + run_sc[0:1, :]
    rank1 = jnp.sum(jnp.where(m1, before, 0.0), axis=-1, keepdims=True)
    rank2 = jnp.sum(jnp.where(m2, before, 0.0), axis=-1, keepdims=True)
    inv = 1.0 / (p1 + p2)
    vals = (i1.astype(F32), i2.astype(F32), rank1, rank2, p1 * inv, p2 * inv)
    meta = jnp.zeros((tm, V7X_LANES), F32)
    for k, v in enumerate(vals):
        meta = jnp.where(lane == k, v, meta)
    meta_ref[...] = meta
    run_sc[...] = run_sc[...] + jnp.sum(chosen, axis=0, keepdims=True)
    cnt_ref[...] = run_sc[...]


def moe_route(x, g, modtab, cond, wr_pad, br_pad):
    tokens = x.shape[0]
    tm = TM_ROUTE
    zero = lambda i: 0
    const = lambda i: (0, 0)
    return pl.pallas_call(
        _route_kernel,
        out_shape=(jax.ShapeDtypeStruct((tokens, D), F32),
                   jax.ShapeDtypeStruct((tokens, V7X_LANES), F32),
                   jax.ShapeDtypeStruct((V7X_SUBLANES, V7X_LANES), F32)),
        grid=(tokens // tm,),
        in_specs=[pl.BlockSpec((tm, D), lambda i: (i, 0)),
                  pl.BlockSpec((1, D), const),
                  _mod_spec(3, cond, tm, D, zero),
                  _mod_spec(4, cond, tm, D, zero),
                  pl.BlockSpec((D, V7X_LANES), const),
                  pl.BlockSpec((1, V7X_LANES), const)],
        out_specs=(pl.BlockSpec((tm, D), lambda i: (i, 0)),
                   pl.BlockSpec((tm, V7X_LANES), lambda i: (i, 0)),
                   pl.BlockSpec((V7X_SUBLANES, V7X_LANES), const)),
        scratch_shapes=[pltpu.VMEM((V7X_SUBLANES, V7X_LANES), F32)],
        compiler_params=_cparams(("arbitrary",)),
        name="moe_route",
    )(x, g.reshape(1, D), modtab, modtab, wr_pad, br_pad)


def _row_copy(src_ref, src_row, dst_ref, dst_row, sem):
    return pltpu.make_async_copy(src_ref.at[pl.ds(src_row, 1)], dst_ref.at[pl.ds(dst_row, 1)], sem)


def _dispatch_kernel(pos_ref, h_ref, hs_in_ref, hs_ref, sem):
    del hs_in_ref
    tm = h_ref.shape[0]
    n_tok = pos_ref.shape[0] // 2
    base = pl.program_id(0) * tm

    def issue(r, carry):
        _row_copy(h_ref, r, hs_ref, pos_ref[base + r], sem).start(priority=0)
        _row_copy(h_ref, r, hs_ref, pos_ref[n_tok + base + r], sem).start(priority=1)
        return carry

    lax.fori_loop(0, tm, issue, 0, unroll=8)
    for _ in range(2):
        pltpu.make_async_copy(h_ref, hs_ref.at[pl.ds(0, tm)], sem).wait()


def moe_dispatch(pos, h, hs):
    tokens = h.shape[0]
    tm = TM_ROUTE
    return pl.pallas_call(
        _dispatch_kernel,
        out_shape=jax.ShapeDtypeStruct(hs.shape, hs.dtype),
        grid_spec=pltpu.PrefetchScalarGridSpec(
            num_scalar_prefetch=1,
            grid=(tokens // tm,),
            in_specs=[pl.BlockSpec((tm, D), lambda i, pos: (i, 0)),
                      pl.BlockSpec(memory_space=pl.ANY)],
            out_specs=pl.BlockSpec(memory_space=pl.ANY),
            scratch_shapes=[pltpu.SemaphoreType.DMA(())]),
        input_output_aliases={2: 0},
        compiler_params=_cparams(("arbitrary",)),
        name="moe_dispatch",
    )(pos, h, hs)


def _experts_kernel(te_ref, sg_ref, su_ref, sd_ref, nu_ref, hs_ref, wg_ref, wu_ref, wd_ref, y_ref,
                    wg_sc, wu_sc, wd_sc):
    del sg_ref, su_ref, sd_ref
    j = pl.program_id(0)
    e = te_ref[j]
    e_prev = te_ref[jnp.maximum(j - 1, 0)]

    @pl.when((j == 0) | (e != e_prev))
    def _():
        wg_sc[...] = wg_ref[...].astype(BF16)
        wu_sc[...] = wu_ref[...].astype(BF16)
        wd_sc[...] = wd_ref[...].astype(BF16)

    @pl.when(j < nu_ref[0])
    def _():
        h = hs_ref[...].astype(BF16)
        y = None
        for c0 in range(0, D_FF_EXPERT, MOE_CHUNK):
            c1 = min(c0 + MOE_CHUNK, D_FF_EXPERT)
            hg = jnp.dot(h, wg_sc[:, c0:c1], preferred_element_type=F32)
            hu = jnp.dot(h, wu_sc[:, c0:c1], preferred_element_type=F32)
            act = (_silu(hg) * hu).astype(BF16)
            yc = jnp.dot(act, wd_sc[c0:c1, :], preferred_element_type=F32)
            y = yc if y is None else y + yc
        y_ref[...] = y

    @pl.when(j >= nu_ref[0])
    def _():
        y_ref[...] = jnp.zeros_like(y_ref)


def moe_experts(tile_expert, stages, n_used, hs, e_gate, e_up, e_down):
    rows = hs.shape[0]
    tmr = TM_EXPERT
    wspec = lambda shape, k: pl.BlockSpec((None,) + shape, lambda j, *pf: (pf[1 + k][j], 0, 0))
    return pl.pallas_call(
        _experts_kernel,
        out_shape=jax.ShapeDtypeStruct((rows, D), F32),
        grid_spec=pltpu.PrefetchScalarGridSpec(
            num_scalar_prefetch=5,
            grid=(rows // tmr,),
            in_specs=[pl.BlockSpec((tmr, D), lambda j, *pf: (j, 0)),
                      wspec((D, D_FF_EXPERT), 0), wspec((D, D_FF_EXPERT), 1), wspec((D_FF_EXPERT, D), 2)],
            out_specs=pl.BlockSpec((tmr, D), lambda j, *pf: (j, 0)),
            scratch_shapes=[pltpu.VMEM((D, D_FF_EXPERT), BF16), pltpu.VMEM((D, D_FF_EXPERT), BF16),
                            pltpu.VMEM((D_FF_EXPERT, D), BF16)]),
        compiler_params=_cparams(("arbitrary",)),
        name="moe_experts",
    )(tile_expert, *stages, n_used, hs, e_gate, e_up, e_down)


def _combine_kernel(pos_ref, x_ref, meta_ref, gt_ref, fg_ref, y_ref, o_ref, b1_sc, b2_sc, sem):
    tm = x_ref.shape[0]
    n_tok = pos_ref.shape[0] // 2
    base = pl.program_id(0) * tm

    def issue(r, carry):
        _row_copy(y_ref, pos_ref[base + r], b1_sc, r, sem).start(priority=0)
        _row_copy(y_ref, pos_ref[n_tok + base + r], b2_sc, r, sem).start(priority=1)
        return carry

    lax.fori_loop(0, tm, issue, 0, unroll=8)
    pltpu.make_async_copy(y_ref.at[pl.ds(0, tm)], b1_sc, sem).wait()
    pltpu.make_async_copy(y_ref.at[pl.ds(0, tm)], b2_sc, sem).wait()

    meta = meta_ref[...]
    lane = lax.broadcasted_iota(jnp.int32, meta.shape, 1)
    g1 = jnp.sum(jnp.where(lane == META_G1, meta, 0.0), axis=-1, keepdims=True)
    g2 = jnp.sum(jnp.where(lane == META_G2, meta, 0.0), axis=-1, keepdims=True)
    x = x_ref[...] + gt_ref[...] * (g1 * b1_sc[...] + g2 * b2_sc[...])
    o_ref[...] = _rms(x, fg_ref[...])


def moe_combine(pos, x, meta, modtab, cond, final_g, y):
    tokens = x.shape[0]
    tm = TM_COMBINE
    return pl.pallas_call(
        _combine_kernel,
        out_shape=jax.ShapeDtypeStruct((tokens, D), F32),
        grid_spec=pltpu.PrefetchScalarGridSpec(
            num_scalar_prefetch=1,
            grid=(tokens // tm,),
            in_specs=[pl.BlockSpec((tm, D), lambda i, pos: (i, 0)),
                      pl.BlockSpec((tm, V7X_LANES), lambda i, pos: (i, 0)),
                      _mod_spec(5, cond, tm, D, lambda i, pos: 0),
                      pl.BlockSpec((1, D), lambda i, pos: (0, 0)),
                      pl.BlockSpec(memory_space=pl.ANY)],
            out_specs=pl.BlockSpec((tm, D), lambda i, pos: (i, 0)),
            scratch_shapes=[pltpu.VMEM((tm, D), F32), pltpu.VMEM((tm, D), F32), pltpu.SemaphoreType.DMA(())]),
        compiler_params=_cparams(("arbitrary",)),
        name="moe_combine",
    )(pos, x, meta, modtab, final_g.reshape(1, D), y)


def moe_plan(metas, counts):
    tmr = TM_EXPERT
    cnts = [c[0, :N_EXPERTS].astype(jnp.int32) for c in counts]
    total = functools.reduce(jnp.add, cnts)
    padded = ((total + tmr - 1) // tmr) * tmr
    ends = jnp.cumsum(padded)
    starts = ends - padded
    n_rows = sum(m.shape[0] for m in metas) * 2 + N_EXPERTS * tmr
    n_tiles = n_rows // tmr
    tile_start = jnp.arange(n_tiles, dtype=jnp.int32) * tmr
    tile_expert = jnp.minimum(jnp.sum(tile_start[:, None] >= ends[None, :], axis=1), N_EXPERTS - 1).astype(jnp.int32)
    n_used = (ends[-1] // tmr).astype(jnp.int32).reshape(1)
    eid = jnp.arange(N_EXPERTS, dtype=jnp.int32)
    later = jnp.where((eid[None, :] > eid[:, None]) & (padded[None, :] > 0), eid[None, :], N_EXPERTS)
    nxt = jnp.min(later, axis=1)
    next_used = jnp.where(nxt == N_EXPERTS, eid, nxt)
    pick = lambda table: jnp.sum(jnp.where(tile_expert[:, None] == eid[None, :], table[None, :], 0), axis=1)
    k_in_group = (tile_start - pick(starts)) // tmr
    tile_next = pick(next_used)
    stages = [jnp.where(k_in_group < k, tile_expert, tile_next).astype(jnp.int32) for k in (1, 2, 3)]
    pos = []
    base = jnp.zeros((N_EXPERTS,), jnp.int32)
    for m, c in zip(metas, cnts):
        first = starts + base
        sel = lambda col: m[:, col].astype(jnp.int32)
        lookup = lambda e: jnp.sum(jnp.where(e[:, None] == jnp.arange(N_EXPERTS)[None, :], first[None, :], 0), axis=1)
        p1 = lookup(sel(META_E1)) + sel(META_R1)
        p2 = lookup(sel(META_E2)) + sel(META_R2)
        pos.append(jnp.concatenate([p1, p2]).astype(jnp.int32))
        base = base + c
    return pos, tile_expert, stages, n_used, n_rows


def _pad_to(a, shape):
    return jnp.pad(a, [(0, t - s) for s, t in zip(a.shape, shape)])


def _regroup_chunks(a, cb):
    r = a.shape[0]
    return a.reshape(r, 3, HY_W // cb, cb).transpose(2, 0, 1, 3).reshape(HY_W // cb, r, 3 * cb)


def kernel(x_prompt, x_sample, state_l0_lru, cache_l1_ckv, cache_l1_krope, c, c_ctx, l0_norm1, l0_norm2, l0_w_mod, l0_b_mod, l0_w_in, l0_conv_a, l0_lru_conv_w, l0_lru_conv_b, l0_lru_wa, l0_lru_ba, l0_lru_wi, l0_lru_bi, l0_lru_lambda, l0_w_out, l0_ffn_gate, l0_ffn_up, l0_ffn_down, l1_norm1, l1_norm2, l1_w_mod, l1_b_mod, l1_w_in, l1_q_norm, l1_kv_norm, l1_w_q_up, l1_w_kv_up, l1_hy_short_w, l1_hy_short_b, l1_hy_f_w1, l1_hy_f_b1, l1_hy_f_w2, l1_hy_f_b2, l1_hy_f_w3, l1_hy_bias, l1_w_out, l1_router_w, l1_router_b, l1_exp_gate, l1_exp_up, l1_exp_down, final_norm):
    batch, seq, _ = x_prompt.shape
    dec_batch, dec_seq, _ = x_sample.shape
    past_len = cache_l1_ckv.shape[1]

    cond8 = jnp.concatenate([c_ctx[None, :], c, jnp.zeros((V7X_SUBLANES - 1 - dec_batch, D), F32)], axis=0)
    wcat = jnp.concatenate([l0_lru_wa[0], l0_lru_wi[0], l0_lru_wa[1], l0_lru_wi[1]], axis=-1)
    wq = l1_w_q_up.reshape(Q_RANK, MLA_HEADS, QK_DIM)
    wq_perm = jnp.concatenate([wq[:, :, :NOPE].reshape(Q_RANK, MLA_HEADS * NOPE),
                               wq[:, :, NOPE:].reshape(Q_RANK, MLA_HEADS * ROPE)], axis=1)
    hid = V7X_LANES
    w1p = _pad_to(l1_hy_f_w1, (hid, hid))
    b1p = _pad_to(l1_hy_f_b1.reshape(1, -1), (1, hid))
    w2p = _pad_to(l1_hy_f_w2, (hid, hid))
    b2p = _pad_to(l1_hy_f_b2.reshape(1, -1), (1, hid))
    w3p = _pad_to(l1_hy_f_w3, (hid, 2 * HY_W))
    short_w = _regroup_chunks(l1_hy_short_w, HY_CB)
    short_b = _regroup_chunks(l1_hy_short_b.reshape(1, -1), HY_CB)
    hy_bias = l1_hy_bias.reshape(1, HY_W)
    wr_pad = _pad_to(l1_router_w, (D, V7X_LANES))
    br_pad = _pad_to(l1_router_b.reshape(1, -1), (1, V7X_LANES))

    mod0 = adaln_table(cond8, l0_w_mod, l0_b_mod)
    mod1 = adaln_table(cond8, l1_w_mod, l1_b_mod)

    kv_ctx = kv_up(cache_l1_ckv.reshape(dec_batch * past_len, KV_RANK), l1_w_kv_up)
    kr_ctx = cache_l1_krope.reshape(dec_batch * past_len, ROPE)

    def trunk(x, seq_len, cond, h0, latent):
        u = in0_proj(x, l0_norm1, mod0, cond, l0_w_in)
        ya = conv_a(u, seq_len, l0_conv_a)
        yb, lru_state = rglru(u, seq_len, l0_lru_conv_w, l0_lru_conv_b, wcat, l0_lru_ba, l0_lru_bi,
                              l0_lru_lambda, h0)
        x = out_res([(ya, 0), (yb, 0), (yb, 1)], l0_w_out, x, mod0, cond, 2)
        x = ffn_res(x, l0_norm2, mod0, cond, l0_ffn_gate, l0_ffn_up, l0_ffn_down)
        qnope, qpe, ckv, kr, kv, uh = in1_proj(x, l1_norm1, mod1, cond, l1_w_in, l1_q_norm, l1_kv_norm,
                                               wq_perm, l1_w_kv_up)
        if latent:
            yc = attn_lat(qnope, qpe, kv_ctx, kr_ctx, kv, kr, seq_len, past_len)
        else:
            yc = attn_ctx(qnope, qpe, kv, kr, seq_len)
        cs = dft_tables(seq_len)
        k_r, k_s, k_ny = hy_filter(cs, w1p, b1p, w2p, b2p, w3p)
        yd = hyena(uh, seq_len, short_w, short_b, cs, k_r, k_s, k_ny, hy_bias)
        x = out_res([(yc, 0), (yc, 1), (yd, 0)], l1_w_out, x, mod1, cond, 2)
        return x, lru_state, ckv, kr

    conds = ((0, batch * seq), (1, dec_seq))
    zeros_state = jnp.zeros((batch, 2, LRU_W), F32)
    x_p, new_lru, new_ckv, new_kr = trunk(x_prompt.reshape(batch * seq, D), seq, conds[0], zeros_state, latent=False)
    x_s, _, _, _ = trunk(x_sample.reshape(dec_batch * dec_seq, D), dec_seq, conds[1], state_l0_lru, latent=True)

    xs = (x_p, x_s)
    routed = [moe_route(x, l1_norm2, mod1, cond, wr_pad, br_pad) for x, cond in zip(xs, conds)]
    pos, tile_expert, stages, n_used, n_rows = moe_plan([r[1] for r in routed], [r[2] for r in routed])
    hs = jnp.zeros((n_rows, D), F32)
    for p, r in zip(pos, routed):
        hs = moe_dispatch(p, r[0], hs)
    y_rows = moe_experts(tile_expert, stages, n_used, hs, l1_exp_gate, l1_exp_up, l1_exp_down)
    y_p, y_s = [moe_combine(p, x, r[1], mod1, cond, final_norm, y_rows)
                for p, x, r, cond in zip(pos, xs, routed, conds)]
    return (y_p.reshape(batch, seq, D), y_s.reshape(dec_batch, dec_seq, D), new_lru,
            new_ckv.reshape(batch, seq, KV_RANK), new_kr.reshape(batch, seq, ROPE))
```

```python
import functools
import math

import jax
import jax.numpy as jnp
from jax import lax
from jax.experimental import pallas as pl
from jax.experimental.pallas import tpu as pltpu

F32 = jnp.float32
BF16 = jnp.bfloat16
HIGHEST = lax.Precision.HIGHEST

D = 1024
GRID_W = 64
EPS = 1e-6
CONV_W = 512
LRU_W = 1024
LRU_BW = 128
LRU_C = 8.0
MLA_HEADS = 8
Q_RANK = 384
KV_RANK = 256
NOPE = 128
ROPE = 64
VDIM = 128
QK_DIM = NOPE + ROPE
ROPE_THETA = 10000.0
HY_W = 512
HY_BANDS = 16
HY_TARGET = 1e-2
HY_FAST_DECAY = 0.3
HY_SLOW_DECAY = 1.5
D_FF = 2816
N_EXPERTS = 8
D_FF_EXPERT = 1408
IN0 = 3 * CONV_W + 2 * LRU_W
IN1 = Q_RANK + KV_RANK + ROPE + 3 * HY_W

V7X_LANES = 128
V7X_SUBLANES = 8
V7X_VMEM_LIMIT_BYTES = 56 * 1024 * 1024

TM = 512
TN_IN0 = 512
TF_FFN = 256
MOE_CHUNK = 256
TM_ROUTE = 512
TM_EXPERT = 256
TM_COMBINE = 256
LRU_CB = 256
HY_CB = 256
TQ = 256
ATTN_CTX_SEQS = 4
CONV_A_ROWS = 1024
LRU_ROWS = 1024
HY_ROWS = 1024
TM_IN1 = 256


def _cparams(sem):
    return pltpu.CompilerParams(dimension_semantics=sem, vmem_limit_bytes=V7X_VMEM_LIMIT_BYTES)


def _sigmoid(x):
    return 0.5 * jnp.tanh(0.5 * x) + 0.5


def _silu(x):
    return x * _sigmoid(x)


def _norm_mod(x, g, shift, scale):
    ms = jnp.mean(x * x, axis=-1, keepdims=True)
    y = x * lax.rsqrt(ms + EPS) * g
    return y * (1.0 + scale) + shift


def _mod_spec(comp, cond, tm, width, col_fn, tile_fn=lambda *ids: ids[0]):
    row0, seg = cond
    assert seg % tm == 0
    return pl.BlockSpec((None, 1, width),
                        lambda *ids: (comp * 3 + row0 + (tile_fn(*ids) * tm) // seg, 0, col_fn(*ids)))


def _dot3(a, b):
    a_hi = a.astype(BF16)
    a_lo = (a - a_hi.astype(F32)).astype(BF16)
    b_hi = b.astype(BF16)
    b_lo = (b - b_hi.astype(F32)).astype(BF16)
    n = a.shape[0]
    y = jnp.dot(jnp.concatenate([a_hi, a_lo], axis=0), b_hi, preferred_element_type=F32)
    return y[:n] + y[n:] + jnp.dot(a_hi, b_lo, preferred_element_type=F32)


def _adaln_kernel(c_ref, w_ref, b_ref, o_ref):
    o_ref[...] = _dot3(_silu(c_ref[...]), w_ref[...]) + b_ref[...]


def adaln_table(cond8, w_mod, b_mod):
    tn = 1536
    m = pl.pallas_call(
        _adaln_kernel,
        out_shape=jax.ShapeDtypeStruct((V7X_SUBLANES, 6 * D), F32),
        grid=(6 * D // tn,),
        in_specs=[pl.BlockSpec((V7X_SUBLANES, D), lambda j: (0, 0)),
                  pl.BlockSpec((D, tn), lambda j: (0, j)),
                  pl.BlockSpec((1, tn), lambda j: (0, j))],
        out_specs=pl.BlockSpec((V7X_SUBLANES, tn), lambda j: (0, j)),
        compiler_params=_cparams(("arbitrary",)),
        name="adaln",
    )(cond8, w_mod, b_mod.reshape(1, 6 * D))
    return m[:3].reshape(3, 6, D).transpose(1, 0, 2).reshape(18, 1, D)


def _tile_of(n_load):
    return lambda s: jnp.maximum(s - n_load, 0)


def _block_of(n_load):
    return lambda s: jnp.minimum(s, n_load - 1)


def _in0_kernel(x_ref, g_ref, sh_ref, sc_ref, w_ref, o_ref, w_sc):
    s = pl.program_id(0)
    n_load, _, tn = w_sc.shape

    @pl.when(s < n_load)
    def _():
        w_sc[s] = w_ref[...].astype(BF16)

    @pl.when(s >= n_load)
    def _():
        h = _norm_mod(x_ref[...], g_ref[...], sh_ref[...], sc_ref[...]).astype(BF16)
        for j in range(n_load):
            o_ref[:, j * tn:(j + 1) * tn] = jnp.dot(h, w_sc[j], preferred_element_type=F32).astype(o_ref.dtype)


def in0_proj(x, g, modtab, cond, w_in):
    tn = TN_IN0
    tokens = x.shape[0]
    n = w_in.shape[1]
    n_load = n // tn
    tile = _tile_of(n_load)
    blk = _block_of(n_load)
    zero = lambda s: 0
    return pl.pallas_call(
        _in0_kernel,
        out_shape=jax.ShapeDtypeStruct((tokens, n), BF16),
        grid=(n_load + tokens // TM,),
        in_specs=[pl.BlockSpec((TM, D), lambda s: (tile(s), 0)),
                  pl.BlockSpec((1, D), lambda s: (0, 0)),
                  _mod_spec(0, cond, TM, D, zero, tile),
                  _mod_spec(1, cond, TM, D, zero, tile),
                  pl.BlockSpec((D, tn), lambda s: (0, blk(s)))],
        out_specs=pl.BlockSpec((TM, n), lambda s: (tile(s), 0)),
        scratch_shapes=[pltpu.VMEM((n_load, D, tn), BF16)],
        compiler_params=_cparams(("arbitrary",)),
        name="in0_proj",
    )(x, g.reshape(1, D), modtab, modtab, w_in)


def _shift_rows(v, d, t, seq_len=None):
    n = v.shape[0]
    seq_len = n if seq_len is None else seq_len
    if d > 0:
        return jnp.where(t < d, 0.0, pltpu.roll(v, d, 0))
    return jnp.where(t >= seq_len + d, 0.0, pltpu.roll(v, n + d, 0))


def _conv_a_kernel(b_ref, c_ref, x_ref, w_ref, o_ref, *, seq_len):
    v = c_ref[...].astype(F32) * x_ref[...].astype(F32)
    t = lax.broadcasted_iota(jnp.int32, v.shape, 0) & (seq_len - 1)
    w = w_ref[...]
    y = w[0:1] * _shift_rows(v, 1, t, seq_len) + w[1:2] * v + w[2:3] * _shift_rows(v, -1, t, seq_len)
    o_ref[...] = (b_ref[...].astype(F32) * y).astype(o_ref.dtype)


def conv_a(u, seq_len, conv_w):
    tokens = u.shape[0]
    rows = max(seq_len, CONV_A_ROWS)
    assert seq_len & (seq_len - 1) == 0 and rows % seq_len == 0
    return pl.pallas_call(
        functools.partial(_conv_a_kernel, seq_len=seq_len),
        out_shape=jax.ShapeDtypeStruct((tokens, CONV_W), BF16),
        grid=(tokens // rows,),
        in_specs=[pl.BlockSpec((rows, CONV_W), lambda s: (s, 0)),
                  pl.BlockSpec((rows, CONV_W), lambda s: (s, 1)),
                  pl.BlockSpec((rows, CONV_W), lambda s: (s, 2)),
                  pl.BlockSpec((3, CONV_W), lambda s: (0, 0))],
        out_specs=pl.BlockSpec((rows, CONV_W), lambda s: (s, 0)),
        compiler_params=_cparams(("parallel",)),
        name="conv_a",
    )(u, u, u, conv_w)


def _group_scan(a, b, reverse):
    n, c = a.shape
    a3 = a.reshape(n // V7X_SUBLANES, V7X_SUBLANES, c)
    b3 = b.reshape(n // V7X_SUBLANES, V7X_SUBLANES, c)
    t8 = lax.broadcasted_iota(jnp.int32, a3.shape, 1)
    for d in (1, 2, 4):
        if reverse:
            keep = t8 < V7X_SUBLANES - d
            shift = V7X_SUBLANES - d
        else:
            keep = t8 >= d
            shift = d
        a_sh = jnp.where(keep, pltpu.roll(a3, shift, 1), 1.0)
        b_sh = jnp.where(keep, pltpu.roll(b3, shift, 1), 0.0)
        b3 = a3 * b_sh + b3
        a3 = a3 * a_sh
    return a3.reshape(n, c), b3.reshape(n, c)


def _rglru_kernel(gate_ref, xb_ref, cw_ref, cb_ref, wcat_ref, ba_ref, bi_ref, lam_ref, h0_ref,
                  y_ref, st_ref, af_sc, bf_sc, ab_sc, bb_sc, hf_sc, hb_sc, *, seq_len):
    n, cb = xb_ref.shape
    n_seq = n // seq_len
    xb = xb_ref[...].astype(F32)
    t = lax.broadcasted_iota(jnp.int32, xb.shape, 0) & (seq_len - 1)
    cw = cw_ref[...]
    sh = lambda d: _shift_rows(xb, d, t, seq_len)
    xc = cb_ref[...] + cw[0:1] * sh(2) + cw[1:2] * sh(1) + cw[2:3] * xb + cw[3:4] * sh(-1)
    xcb = xc.astype(BF16)
    g = [jnp.dot(xcb[:, k * LRU_BW:(k + 1) * LRU_BW], wcat_ref[k].astype(BF16), preferred_element_type=F32)
         for k in range(cb // LRU_BW)]

    def direction(d):
        ga = jnp.concatenate([gk[:, (2 * d) * LRU_BW:(2 * d + 1) * LRU_BW] for gk in g], axis=1)
        gi = jnp.concatenate([gk[:, (2 * d + 1) * LRU_BW:(2 * d + 2) * LRU_BW] for gk in g], axis=1)
        r = _sigmoid(ga + ba_ref[d:d + 1, :])
        i = _sigmoid(gi + bi_ref[d:d + 1, :])
        log_a = (-LRU_C * jax.nn.softplus(-lam_ref[d:d + 1, :])) * r
        a = jnp.exp(log_a)
        m = 1.0 - a * a
        mult = m * lax.rsqrt(jnp.maximum(m, 1e-30))
        return a, mult * (i * xc)

    a_f, b_f = direction(0)
    a_f, b_f = _group_scan(a_f, b_f, reverse=False)
    af_sc[...] = a_f
    bf_sc[...] = b_f
    a_b, b_b = direction(1)
    a_b, b_b = _group_scan(a_b, b_b, reverse=True)
    ab_sc[...] = a_b
    bb_sc[...] = b_b

    ng = seq_len // V7X_SUBLANES
    bcast = lambda row: jnp.broadcast_to(row, (V7X_SUBLANES, cb))
    init = tuple((bcast(h0_ref[q, 0:1, :]), bcast(h0_ref[q, 1:2, :])) for q in range(n_seq))

    def step(k, carry):
        out = []
        for q, (hf_in, hb_in) in enumerate(carry):
            rf = pl.multiple_of(q * seq_len + k * V7X_SUBLANES, V7X_SUBLANES)
            rb = pl.multiple_of(q * seq_len + (ng - 1 - k) * V7X_SUBLANES, V7X_SUBLANES)
            hf = af_sc[pl.ds(rf, V7X_SUBLANES), :] * hf_in + bf_sc[pl.ds(rf, V7X_SUBLANES), :]
            hb = ab_sc[pl.ds(rb, V7X_SUBLANES), :] * hb_in + bb_sc[pl.ds(rb, V7X_SUBLANES), :]
            hf_sc[pl.ds(rf, V7X_SUBLANES), :] = hf
            hb_sc[pl.ds(rb, V7X_SUBLANES), :] = hb
            out.append((bcast(hf[V7X_SUBLANES - 1:V7X_SUBLANES]), bcast(hb[0:1])))
        return tuple(out)

    final = lax.fori_loop(0, ng, step, init)
    for q, (hf_last, hb_first) in enumerate(final):
        st_ref[q, 0:1, :] = hf_last[0:1]
        st_ref[q, 1:2, :] = hb_first[0:1]

    gt = gate_ref[...].astype(F32)
    gelu = 0.5 * gt * (1.0 + jnp.tanh(math.sqrt(2.0 / math.pi) * (gt + 0.044715 * (gt * gt * gt))))
    y_ref[...] = ((hf_sc[...] + hb_sc[...]) * gelu).astype(y_ref.dtype)


def rglru(u, seq_len, conv_w, conv_b, wcat, ba, bi, lam, h0):
    tokens = u.shape[0]
    nseq = tokens // seq_len
    cb = LRU_CB
    rows = max(seq_len, LRU_ROWS)
    assert seq_len & (seq_len - 1) == 0 and rows % seq_len == 0
    per_blk = rows // seq_len
    gate_blk0 = 3 * CONV_W // cb
    xb_blk0 = (3 * CONV_W + LRU_W) // cb
    seq_scr = lambda: pltpu.VMEM((rows, cb), F32)
    return pl.pallas_call(
        functools.partial(_rglru_kernel, seq_len=seq_len),
        out_shape=(jax.ShapeDtypeStruct((tokens, LRU_W), BF16), jax.ShapeDtypeStruct((nseq, 2, LRU_W), F32)),
        grid=(tokens // rows, LRU_W // cb),
        in_specs=[pl.BlockSpec((rows, cb), lambda s, c: (s, gate_blk0 + c)),
                  pl.BlockSpec((rows, cb), lambda s, c: (s, xb_blk0 + c)),
                  pl.BlockSpec((4, cb), lambda s, c: (0, c)),
                  pl.BlockSpec((1, cb), lambda s, c: (0, c)),
                  pl.BlockSpec((cb // LRU_BW, LRU_BW, 4 * LRU_BW), lambda s, c: (c, 0, 0)),
                  pl.BlockSpec((2, cb), lambda s, c: (0, c)),
                  pl.BlockSpec((2, cb), lambda s, c: (0, c)),
                  pl.BlockSpec((2, cb), lambda s, c: (0, c)),
                  pl.BlockSpec((per_blk, 2, cb), lambda s, c: (s, 0, c))],
        out_specs=(pl.BlockSpec((rows, cb), lambda s, c: (s, c)),
                   pl.BlockSpec((per_blk, 2, cb), lambda s, c: (s, 0, c))),
        scratch_shapes=[seq_scr() for _ in range(6)],
        compiler_params=_cparams(("parallel", "parallel")),
        name="rglru",
    )(u, u, conv_w, conv_b.reshape(1, LRU_W), wcat, ba, bi, lam, h0)


def _out_res_kernel(p0_ref, p1_ref, p2_ref, w_ref, x_ref, gt_ref, o_ref, w_sc):
    s = pl.program_id(0)
    n_load, kb, _ = w_sc.shape

    @pl.when(s < n_load)
    def _():
        w_sc[s] = w_ref[...].astype(BF16)

    @pl.when(s >= n_load)
    def _():
        y = jnp.dot(p0_ref[...], w_sc[0], preferred_element_type=F32)
        y += jnp.dot(p1_ref[...], w_sc[1], preferred_element_type=F32)
        y += jnp.dot(p2_ref[...], w_sc[2], preferred_element_type=F32)
        o_ref[...] = x_ref[...] + gt_ref[...] * y


def out_res(parts, w_out, x, modtab, cond, gate_comp):
    tokens = x.shape[0]
    kb = 512
    n_load = len(parts)
    tile = _tile_of(n_load)
    blk = _block_of(n_load)
    lhs_specs = [pl.BlockSpec((TM, kb), (lambda s, cbk=cbk: (tile(s), cbk))) for _, cbk in parts]
    return pl.pallas_call(
        _out_res_kernel,
        out_shape=jax.ShapeDtypeStruct((tokens, D), F32),
        grid=(n_load + tokens // TM,),
        in_specs=lhs_specs + [pl.BlockSpec((kb, D), lambda s: (blk(s), 0)),
                              pl.BlockSpec((TM, D), lambda s: (tile(s), 0)),
                              _mod_spec(gate_comp, cond, TM, D, lambda s: 0, tile)],
        out_specs=pl.BlockSpec((TM, D), lambda s: (tile(s), 0)),
        scratch_shapes=[pltpu.VMEM((n_load, kb, D), BF16)],
        compiler_params=_cparams(("arbitrary",)),
        name="out_res",
    )(*[a for a, _ in parts], w_out, x, modtab)


def _ffn_kernel(x_ref, g_ref, sh_ref, sc_ref, gt_ref, wg_ref, wu_ref, wd_ref, o_ref, wg_sc, wu_sc, wd_sc):
    s = pl.program_id(0)
    n_load = wg_sc.shape[0]

    @pl.when(s < n_load)
    def _():
        wg_sc[s] = wg_ref[...].astype(BF16)
        wu_sc[s] = wu_ref[...].astype(BF16)
        wd_sc[s] = wd_ref[...].astype(BF16)

    @pl.when(s >= n_load)
    def _():
        x = x_ref[...]
        h = _norm_mod(x, g_ref[...], sh_ref[...], sc_ref[...]).astype(BF16)
        y = None
        for f in range(n_load):
            hg = jnp.dot(h, wg_sc[f], preferred_element_type=F32)
            hu = jnp.dot(h, wu_sc[f], preferred_element_type=F32)
            act = (_silu(hg) * hu).astype(BF16)
            yf = jnp.dot(act, wd_sc[f], preferred_element_type=F32)
            y = yf if y is None else y + yf
        o_ref[...] = x + gt_ref[...] * y


def ffn_res(x, g, modtab, cond, w_gate, w_up, w_down):
    tokens = x.shape[0]
    tf = TF_FFN
    n_load = D_FF // tf
    tile = _tile_of(n_load)
    blk = _block_of(n_load)
    zero = lambda s: 0
    return pl.pallas_call(
        _ffn_kernel,
        out_shape=jax.ShapeDtypeStruct((tokens, D), F32),
        grid=(n_load + tokens // TM,),
        in_specs=[pl.BlockSpec((TM, D), lambda s: (tile(s), 0)),
                  pl.BlockSpec((1, D), lambda s: (0, 0)),
                  _mod_spec(3, cond, TM, D, zero, tile),
                  _mod_spec(4, cond, TM, D, zero, tile),
                  _mod_spec(5, cond, TM, D, zero, tile),
                  pl.BlockSpec((D, tf), lambda s: (0, blk(s))),
                  pl.BlockSpec((D, tf), lambda s: (0, blk(s))),
                  pl.BlockSpec((tf, D), lambda s: (blk(s), 0))],
        out_specs=pl.BlockSpec((TM, D), lambda s: (tile(s), 0)),
        scratch_shapes=[pltpu.VMEM((n_load, D, tf), BF16), pltpu.VMEM((n_load, D, tf), BF16),
                        pltpu.VMEM((n_load, tf, D), BF16)],
        compiler_params=_cparams(("arbitrary",)),
        name="ffn_res",
    )(x, g.reshape(1, D), modtab, modtab, modtab, w_gate, w_up, w_down)


def _rms(x, g):
    return x * lax.rsqrt(jnp.mean(x * x, axis=-1, keepdims=True) + EPS) * g


def _in1_kernel(x_ref, g_ref, sh_ref, sc_ref, w_ref, qn_ref, kvn_ref, wq_ref, wkv_ref,
                qnope_ref, qpe_ref, ckv_ref, kr_ref, kv_ref, uh_ref, w_sc, wq_sc, wkv_sc):
    @pl.when(pl.program_id(0) == 0)
    def _():
        w_sc[...] = w_ref[...].astype(BF16)
        wq_sc[...] = wq_ref[...].astype(BF16)
        wkv_sc[...] = wkv_ref[...].astype(BF16)

    h = _norm_mod(x_ref[...], g_ref[...], sh_ref[...], sc_ref[...]).astype(BF16)
    u = jnp.dot(h, w_sc[...], preferred_element_type=F32)
    o1, o2, o3 = Q_RANK, Q_RANK + KV_RANK, Q_RANK + KV_RANK + ROPE
    cq = _rms(u[:, :o1], qn_ref[...])
    q = jnp.dot(cq.astype(BF16), wq_sc[...], preferred_element_type=F32) * _SCALE
    qnope_ref[...] = q[:, :MLA_HEADS * NOPE].astype(qnope_ref.dtype)
    qpe_ref[...] = q[:, MLA_HEADS * NOPE:]
    ckv = _rms(u[:, o1:o2], kvn_ref[...])
    ckv_ref[...] = ckv
    kv_ref[...] = jnp.dot(ckv.astype(BF16), wkv_sc[...], preferred_element_type=F32).astype(kv_ref.dtype)
    kr_ref[...] = u[:, o2:o3]
    uh_ref[...] = u[:, o3:]


def in1_proj(x, g, modtab, cond, w_in, q_norm, kv_norm, wq_perm, w_kv_up):
    tokens = x.shape[0]
    tm = TM_IN1
    nkv = MLA_HEADS * (NOPE + VDIM)
    const = lambda i: (0, 0)
    zero = lambda i: 0
    once = pl.Buffered(1)
    outs = (jax.ShapeDtypeStruct((tokens, MLA_HEADS * NOPE), BF16),
            jax.ShapeDtypeStruct((tokens, MLA_HEADS * ROPE), F32),
            jax.ShapeDtypeStruct((tokens, KV_RANK), F32),
            jax.ShapeDtypeStruct((tokens, ROPE), F32),
            jax.ShapeDtypeStruct((tokens, nkv), BF16),
            jax.ShapeDtypeStruct((tokens, 3 * HY_W), F32))
    row = lambda w: pl.BlockSpec((tm, w), lambda i: (i, 0))
    return pl.pallas_call(
        _in1_kernel,
        out_shape=outs,
        grid=(tokens // tm,),
        in_specs=[row(D),
                  pl.BlockSpec((1, D), const),
                  _mod_spec(0, cond, tm, D, zero),
                  _mod_spec(1, cond, tm, D, zero),
                  pl.BlockSpec((D, IN1), const, pipeline_mode=once),
                  pl.BlockSpec((1, Q_RANK), const),
                  pl.BlockSpec((1, KV_RANK), const),
                  pl.BlockSpec((Q_RANK, MLA_HEADS * QK_DIM), const, pipeline_mode=once),
                  pl.BlockSpec((KV_RANK, nkv), const, pipeline_mode=once)],
        out_specs=tuple(row(o.shape[1]) for o in outs),
        scratch_shapes=[pltpu.VMEM((D, IN1), BF16), pltpu.VMEM((Q_RANK, MLA_HEADS * QK_DIM), BF16),
                        pltpu.VMEM((KV_RANK, nkv), BF16)],
        compiler_params=_cparams(("arbitrary",)),
        name="in1_proj",
    )(x, g.reshape(1, D), modtab, modtab, w_in, q_norm.reshape(1, Q_RANK), kv_norm.reshape(1, KV_RANK),
      wq_perm, w_kv_up)


def _mm_kernel(a_ref, w_ref, o_ref):
    o_ref[...] = jnp.dot(a_ref[...].astype(BF16), w_ref[...].astype(BF16),
                         preferred_element_type=F32).astype(o_ref.dtype)


def kv_up(ckv, w_kv_up):
    rows = ckv.shape[0]
    n = w_kv_up.shape[1]
    return pl.pallas_call(
        _mm_kernel,
        out_shape=jax.ShapeDtypeStruct((rows, n), BF16),
        grid=(rows // TM,),
        in_specs=[pl.BlockSpec((TM, KV_RANK), lambda i: (i, 0)), pl.BlockSpec((KV_RANK, n), lambda i: (0, 0))],
        out_specs=pl.BlockSpec((TM, n), lambda i: (i, 0)),
        compiler_params=_cparams(("parallel",)),
        name="kv_up",
    )(ckv, w_kv_up)


_NT = (((1,), (1,)), ((), ()))
_SCALE = 1.0 / math.sqrt(QK_DIM)


def _rope_tables(pos):
    n = pos.shape[0]
    lane = lax.broadcasted_iota(jnp.int32, (n, ROPE), 1)
    j = lane & (ROPE // 2 - 1)
    n_freq = ROPE // 4
    inv = jnp.exp((j & (n_freq - 1)).astype(F32) * (-math.log(ROPE_THETA) / n_freq))
    p = jnp.where(j < n_freq, pos >> (GRID_W.bit_length() - 1), pos & (GRID_W - 1)).astype(F32)
    ang = p * inv
    return jnp.cos(ang), jnp.sin(ang)


def _rot_half_matrix():
    i = lax.broadcasted_iota(jnp.int32, (ROPE, ROPE), 0)
    j = lax.broadcasted_iota(jnp.int32, (ROPE, ROPE), 1)
    half = ROPE // 2
    return jnp.where(i == j + half, -1.0, jnp.where(i + half == j, 1.0, 0.0)).astype(F32)


def _rope(x, cos, sin, rot):
    xr = jnp.dot(x, rot, preferred_element_type=F32, precision=HIGHEST)
    return x * cos + xr * sin


def _ones_column(n):
    lane = lax.broadcasted_iota(jnp.int32, (n, VDIM), 1)
    return jnp.where(lane == 0, 1.0, 0.0).astype(BF16)


def _head_attention(qcat, kcat, vaug):
    s = lax.dot_general(qcat, kcat, _NT, preferred_element_type=F32)
    p = jnp.exp(s - jnp.max(s, axis=-1, keepdims=True)).astype(BF16)
    oa = jnp.dot(p, vaug, preferred_element_type=F32)
    return oa[:, :VDIM] / oa[:, VDIM:VDIM + 1]


def _attn_ctx_kernel(qn_ref, qpe_ref, kv_ref, kr_ref, o_ref, *, seq_len):
    ones = _ones_column(seq_len)
    for s0 in range(0, qn_ref.shape[0], seq_len):
        rows = slice(s0, s0 + seq_len)
        kpe = kr_ref[rows, :].astype(BF16)
        for h in range(MLA_HEADS):
            c0 = h * (NOPE + VDIM)
            qcat = jnp.concatenate([qn_ref[rows, h * NOPE:(h + 1) * NOPE],
                                    qpe_ref[rows, h * ROPE:(h + 1) * ROPE].astype(BF16)], axis=1)
            kcat = jnp.concatenate([kv_ref[rows, c0:c0 + NOPE], kpe], axis=1)
            vaug = jnp.concatenate([kv_ref[rows, c0 + NOPE:c0 + NOPE + VDIM], ones], axis=1)
            o_ref[rows, h * VDIM:(h + 1) * VDIM] = _head_attention(qcat, kcat, vaug).astype(o_ref.dtype)


def attn_ctx(qnope, qpe, kv, kr, seq_len):
    tokens = qnope.shape[0]
    rows = ATTN_CTX_SEQS * seq_len
    blk = lambda w: pl.BlockSpec((rows, w), lambda s: (s, 0))
    return pl.pallas_call(
        functools.partial(_attn_ctx_kernel, seq_len=seq_len),
        out_shape=jax.ShapeDtypeStruct((tokens, MLA_HEADS * VDIM), BF16),
        grid=(tokens // rows,),
        in_specs=[blk(MLA_HEADS * NOPE), blk(MLA_HEADS * ROPE), blk(MLA_HEADS * (NOPE + VDIM)), blk(ROPE)],
        out_specs=blk(MLA_HEADS * VDIM),
        compiler_params=_cparams(("parallel",)),
        name="attn_ctx",
    )(qnope, qpe, kv, kr)


def _attn_lat_kernel(qn_ref, qpe_ref, kvc_ref, krc_ref, kvl_ref, krl_ref, o_ref, kcat_sc, vaug_sc):
    tq = qn_ref.shape[0]
    n_ctx = krc_ref.shape[0]
    n_lat = krl_ref.shape[0]
    rot = _rot_half_matrix()

    @pl.when(pl.program_id(1) == 0)
    def _():
        ck, sk = _rope_tables(lax.broadcasted_iota(jnp.int32, (n_lat, 1), 0))
        kpe_lat = _rope(krl_ref[...], ck, sk, rot).astype(BF16)
        kpe_ctx = krc_ref[...].astype(BF16)
        ones_c, ones_l = _ones_column(n_ctx), _ones_column(n_lat)
        for h in range(MLA_HEADS):
            c0 = h * (NOPE + VDIM)
            for r0, nr, kv_ref, kpe, ones in ((0, n_ctx, kvc_ref, kpe_ctx, ones_c), (n_ctx, n_lat, kvl_ref, kpe_lat, ones_l)):
                kcat_sc[h, r0:r0 + nr, 0:NOPE] = kv_ref[:, c0:c0 + NOPE]
                kcat_sc[h, r0:r0 + nr, NOPE:QK_DIM] = kpe
                vaug_sc[h, r0:r0 + nr, 0:VDIM] = kv_ref[:, c0 + NOPE:c0 + NOPE + VDIM]
                vaug_sc[h, r0:r0 + nr, VDIM:2 * VDIM] = ones

    q0 = pl.program_id(1) * tq
    cq, sq = _rope_tables(q0 + lax.broadcasted_iota(jnp.int32, (tq, 1), 0))
    for h in range(MLA_HEADS):
        qp = _rope(qpe_ref[:, h * ROPE:(h + 1) * ROPE], cq, sq, rot).astype(BF16)
        qcat = jnp.concatenate([qn_ref[:, h * NOPE:(h + 1) * NOPE], qp], axis=1)
        o_ref[:, h * VDIM:(h + 1) * VDIM] = _head_attention(qcat, kcat_sc[h], vaug_sc[h]).astype(o_ref.dtype)


def attn_lat(qnope, qpe, kv_ctx, kr_ctx, kv_lat, kr_lat, seq_len, ctx_len):
    tokens = qnope.shape[0]
    nq = seq_len // TQ
    qblk = lambda w: pl.BlockSpec((TQ, w), lambda b, i: (b * nq + i, 0))
    seq = lambda n, w: pl.BlockSpec((n, w), lambda b, i: (b, 0))
    nkv = MLA_HEADS * (NOPE + VDIM)
    n_keys = ctx_len + seq_len
    return pl.pallas_call(
        _attn_lat_kernel,
        out_shape=jax.ShapeDtypeStruct((tokens, MLA_HEADS * VDIM), BF16),
        grid=(tokens // seq_len, nq),
        in_specs=[qblk(MLA_HEADS * NOPE), qblk(MLA_HEADS * ROPE), seq(ctx_len, nkv), seq(ctx_len, ROPE),
                  seq(seq_len, nkv), seq(seq_len, ROPE)],
        out_specs=qblk(MLA_HEADS * VDIM),
        scratch_shapes=[pltpu.VMEM((MLA_HEADS, n_keys, QK_DIM), BF16),
                        pltpu.VMEM((MLA_HEADS, n_keys, 2 * VDIM), BF16)],
        compiler_params=_cparams(("parallel", "arbitrary")),
        name="attn_lat",
    )(qnope, qpe, kv_ctx, kr_ctx, kv_lat, kr_lat)


def _dft_kernel(o_ref):
    tr, n = o_ref.shape[1], o_ref.shape[2]
    nb = n // V7X_LANES
    f = pl.program_id(0) * tr + lax.broadcasted_iota(jnp.int32, (tr, V7X_LANES), 0)
    j = lax.broadcasted_iota(jnp.int32, (tr, V7X_LANES), 1)

    def cos_sin(m):
        ang = (m & (2 * n - 1)).astype(F32) * (math.pi / n)
        return jnp.cos(ang), jnp.sin(ang)

    cj, sj = cos_sin(f * j)
    cb, sb = cos_sin(f * (j * V7X_LANES))
    for b in range(nb):
        cbb, sbb = cb[:, b:b + 1], sb[:, b:b + 1]
        cols = slice(b * V7X_LANES, (b + 1) * V7X_LANES)
        o_ref[0, :, cols] = (cbb * cj - sbb * sj).astype(o_ref.dtype)
        o_ref[1, :, cols] = (sbb * cj + cbb * sj).astype(o_ref.dtype)


def dft_tables(n):
    tr = 128
    return pl.pallas_call(
        _dft_kernel,
        out_shape=jax.ShapeDtypeStruct((2, n, n), BF16),
        grid=(n // tr,),
        out_specs=pl.BlockSpec((2, tr, n), lambda i: (0, i, 0)),
        compiler_params=_cparams(("parallel",)),
        name="dft_tables",
    )()


def _split_dot(table, x):
    hi = x.astype(BF16)
    lo = (x - hi.astype(F32)).astype(BF16)
    return (jnp.dot(table, hi, preferred_element_type=F32) + jnp.dot(table, lo, preferred_element_type=F32))


def _hy_filter_kernel(cs_ref, w1_ref, b1_ref, w2_ref, b2_ref, w3_ref, kr_ref, ks_ref, kny_ref):
    n = cs_ref.shape[1]
    row = lax.broadcasted_iota(jnp.int32, (n, V7X_LANES), 0).astype(F32)
    lane = lax.broadcasted_iota(jnp.int32, (n, V7X_LANES), 1)
    t = row * (1.0 / (n - 1))
    w = (2.0 * math.pi) * row / n
    band = jnp.where(lane <= HY_BANDS, lane - 1, lane - 1 - HY_BANDS).astype(F32)
    freq = 1e-4 + band * ((HY_BANDS - 1 - 1e-4) / (HY_BANDS - 1))
    arg = jnp.where(lane <= HY_BANDS, freq * w + 0.5 * math.pi, -(freq * w))
    z = jnp.where(lane == 0, t, jnp.where(lane <= 2 * HY_BANDS, jnp.sin(arg), 0.0))
    hid = jnp.sin(_dot3(z, w1_ref[...]) + b1_ref[...])
    hid = jnp.sin(_dot3(hid, w2_ref[...]) + b2_ref[...])
    hf = _dot3(hid, w3_ref[...])

    rowc = lax.broadcasted_iota(jnp.int32, (n, HY_W), 0)
    chan = lax.broadcasted_iota(jnp.int32, (n, HY_W), 1).astype(F32)
    max_decay = math.log(HY_TARGET) / HY_FAST_DECAY
    min_decay = math.log(HY_TARGET) / HY_SLOW_DECAY
    deltas = min_decay + chan * ((max_decay - min_decay) / (HY_W - 1))
    decay = jnp.exp(-(rowc.astype(F32) * (1.0 / (n - 1))) * jnp.abs(deltas))
    h_fwd = hf[:, :HY_W] * decay
    h_bwd = jnp.where(rowc == 0, 0.0, hf[:, HY_W:] * decay)
    norm = jnp.sum(jnp.abs(h_fwd) + jnp.abs(h_bwd), axis=0, keepdims=True)
    even = (h_fwd + h_bwd) / norm
    odd = (h_fwd - h_bwd) / norm
    cf = jnp.where(rowc == 0, 1.0, 2.0) * (1.0 / (2 * n))
    kr_ref[...] = cf * _split_dot(cs_ref[0], even)
    ks_ref[...] = cf * _split_dot(cs_ref[1], odd)
    sgn = jnp.where((rowc & 1) == 1, -1.0, 1.0)
    kny_ref[...] = jnp.sum(sgn * even, axis=0, keepdims=True) * (1.0 / (2 * n))


def hy_filter(cs, w1p, b1p, w2p, b2p, w3p):
    n = cs.shape[1]
    full = lambda a: pl.BlockSpec(a.shape, lambda: (0,) * a.ndim)
    args = (cs, w1p, b1p, w2p, b2p, w3p)
    return pl.pallas_call(
        _hy_filter_kernel,
        out_shape=(jax.ShapeDtypeStruct((n, HY_W), F32), jax.ShapeDtypeStruct((n, HY_W), F32),
                   jax.ShapeDtypeStruct((1, HY_W), F32)),
        in_specs=[full(a) for a in args],
        out_specs=(pl.BlockSpec((n, HY_W), lambda: (0, 0)), pl.BlockSpec((n, HY_W), lambda: (0, 0)),
                   pl.BlockSpec((1, HY_W), lambda: (0, 0))),
        compiler_params=pltpu.CompilerParams(vmem_limit_bytes=V7X_VMEM_LIMIT_BYTES),
        name="hy_filter",
    )(*args)


def _hyena_kernel(u0_ref, u1_ref, u2_ref, sw_ref, sb_ref, cs_ref, kr_ref, ks_ref, kny_ref, bias_ref, o_ref,
                  *, seq_len):
    n, cb = u0_ref.shape
    n_seq = n // seq_len
    t = lax.broadcasted_iota(jnp.int32, (n, cb), 0) & (seq_len - 1)

    def short_conv(u_ref, k):
        u = u_ref[...]
        w = sw_ref[:, k * cb:(k + 1) * cb]
        return (sb_ref[:, k * cb:(k + 1) * cb] + w[0:1] * _shift_rows(u, 1, t, seq_len) + w[1:2] * u
                + w[2:3] * _shift_rows(u, -1, t, seq_len))

    x0 = short_conv(u0_ref, 0)
    z = short_conv(u1_ref, 1) * short_conv(u2_ref, 2)
    wide = lambda a: jnp.concatenate([a[q * seq_len:(q + 1) * seq_len] for q in range(n_seq)], axis=1)
    rep = lambda a: jnp.concatenate([a] * n_seq, axis=1)
    zw = wide(z)
    zb = zw.astype(BF16)
    c, s = cs_ref[0], cs_ref[1]
    ur = jnp.dot(c, zb, preferred_element_type=F32)
    us = jnp.dot(s, zb, preferred_element_type=F32)
    sgn = jnp.where((lax.broadcasted_iota(jnp.int32, zw.shape, 0) & 1) == 1, -1.0, 1.0)
    uny = jnp.sum(sgn * zw, axis=0, keepdims=True)
    kr, ks = rep(kr_ref[...]), rep(ks_ref[...])
    yr = (ur * kr - us * ks).astype(BF16)
    ys = (ur * ks + us * kr).astype(BF16)
    yw = jnp.dot(c, yr, preferred_element_type=F32) + jnp.dot(s, ys, preferred_element_type=F32)
    yw = yw + sgn * (uny * rep(kny_ref[...]))
    y = jnp.concatenate([yw[:, q * cb:(q + 1) * cb] for q in range(n_seq)], axis=0)
    o_ref[...] = (x0 * (y + bias_ref[...] * z)).astype(o_ref.dtype)


def hyena(uh, seq_len, short_w, short_b, cs, kr, ks, kny, bias):
    tokens = uh.shape[0]
    cb = HY_CB
    nc = HY_W // cb
    rows = max(seq_len, HY_ROWS)
    assert seq_len & (seq_len - 1) == 0 and rows % seq_len == 0
    ublk = lambda k: pl.BlockSpec((rows, cb), lambda s, c: (s, k * nc + c))
    chan = lambda r: pl.BlockSpec((r, cb), lambda s, c: (0, c))
    return pl.pallas_call(
        functools.partial(_hyena_kernel, seq_len=seq_len),
        out_shape=jax.ShapeDtypeStruct((tokens, HY_W), BF16),
        grid=(tokens // rows, nc),
        in_specs=[ublk(0), ublk(1), ublk(2),
                  pl.BlockSpec((None, 3, 3 * cb), lambda s, c: (c, 0, 0)),
                  pl.BlockSpec((None, 1, 3 * cb), lambda s, c: (c, 0, 0)),
                  pl.BlockSpec((2, seq_len, seq_len), lambda s, c: (0, 0, 0)),
                  chan(seq_len), chan(seq_len), chan(1), chan(1)],
        out_specs=pl.BlockSpec((rows, cb), lambda s, c: (s, c)),
        compiler_params=_cparams(("parallel", "parallel")),
        name="hyena",
    )(uh, uh, uh, short_w, short_b, cs, kr, ks, kny, bias)


META_E1, META_E2, META_R1, META_R2, META_G1, META_G2 = range(6)


def _route_kernel(x_ref, g_ref, sh_ref, sc_ref, wr_ref, br_ref, h_ref, meta_ref, meta_t_ref, cnt_ref, run_sc):
    tm = x_ref.shape[0]
    lane = lax.broadcasted_iota(jnp.int32, (tm, V7X_LANES), 1)

    @pl.when(pl.program_id(0) == 0)
    def _():
        run_sc[...] = jnp.zeros_like(run_sc)

    h = _norm_mod(x_ref[...], g_ref[...], sh_ref[...], sc_ref[...])
    h_ref[...] = h
    logits = _dot3(h, wr_ref[...]) + br_ref[...]
    valid = lane < N_EXPERTS
    lg = jnp.where(valid, logits, -jnp.inf)
    ex = jnp.exp(lg - jnp.max(lg, axis=-1, keepdims=True))
    p = ex / jnp.sum(ex, axis=-1, keepdims=True)
    p1 = jnp.max(p, axis=-1, keepdims=True)
    i1 = jnp.min(jnp.where((p == p1) & valid, lane, V7X_LANES), axis=-1, keepdims=True)
    rest = jnp.where((lane == i1) | (~valid), -1.0, p)
    p2 = jnp.max(rest, axis=-1, keepdims=True)
    i2 = jnp.min(jnp.where(rest == p2, lane, V7X_LANES), axis=-1, keepdims=True)
    m1 = lane == i1
    m2 = lane == i2
    chosen = jnp.where(m1 | m2, 1.0, 0.0)
    r = lax.broadcasted_iota(jnp.int32, (tm, tm), 0)
    c = lax.broadcasted_iota(jnp.int32, (tm, tm), 1)
    tri = jnp.where(c < r, 1.0, 0.0).astype(BF16)
    before = jnp.dot(tri, chosen.astype(BF16), preferred_element_type=F32) + run_sc[0:1, :]
    rank1 = jnp.sum(jnp.where(m1, before, 0.0), axis=-1, keepdims=True)
    rank2 = jnp.sum(jnp.where(m2, before, 0.0), axis=-1, keepdims=True)
    inv = 1.0 / (p1 + p2)
    vals = (i1.astype(F32), i2.astype(F32), rank1, rank2, p1 * inv, p2 * inv)
    meta = jnp.zeros((tm, V7X_LANES), F32)
    for k, v in enumerate(vals):
        meta = jnp.where(lane == k, v, meta)
    meta_ref[...] = meta
    meta_t_ref[...] = meta.T[:V7X_SUBLANES]
    run_sc[...] = run_sc[...] + jnp.sum(chosen, axis=0, keepdims=True)
    cnt_ref[...] = run_sc[...]


def moe_route(x, g, modtab, cond, wr_pad, br_pad):
    tokens = x.shape[0]
    tm = TM_ROUTE
    zero = lambda i: 0
    const = lambda i: (0, 0)
    return pl.pallas_call(
        _route_kernel,
        out_shape=(jax.ShapeDtypeStruct((tokens, D), F32),
                   jax.ShapeDtypeStruct((tokens, V7X_LANES), F32),
                   jax.ShapeDtypeStruct((V7X_SUBLANES, tokens), F32),
                   jax.ShapeDtypeStruct((V7X_SUBLANES, V7X_LANES), F32)),
        grid=(tokens // tm,),
        in_specs=[pl.BlockSpec((tm, D), lambda i: (i, 0)),
                  pl.BlockSpec((1, D), const),
                  _mod_spec(3, cond, tm, D, zero),
                  _mod_spec(4, cond, tm, D, zero),
                  pl.BlockSpec((D, V7X_LANES), const),
                  pl.BlockSpec((1, V7X_LANES), const)],
        out_specs=(pl.BlockSpec((tm, D), lambda i: (i, 0)),
                   pl.BlockSpec((tm, V7X_LANES), lambda i: (i, 0)),
                   pl.BlockSpec((V7X_SUBLANES, tm), lambda i: (0, i)),
                   pl.BlockSpec((V7X_SUBLANES, V7X_LANES), const)),
        scratch_shapes=[pltpu.VMEM((V7X_SUBLANES, V7X_LANES), F32)],
        compiler_params=_cparams(("arbitrary",)),
        name="moe_route",
    )(x, g.reshape(1, D), modtab, modtab, wr_pad, br_pad)


def _row_copy(src_ref, src_row, dst_ref, dst_row, sem):
    return pltpu.make_async_copy(src_ref.at[pl.ds(src_row, 1)], dst_ref.at[pl.ds(dst_row, 1)], sem)


def _dispatch_kernel(pos_ref, h_ref, hs_in_ref, hs_ref, sem):
    del hs_in_ref
    tm = h_ref.shape[0]
    n_tok = pos_ref.shape[0] // 2
    base = pl.program_id(0) * tm

    def issue(r, carry):
        _row_copy(h_ref, r, hs_ref, pos_ref[base + r], sem).start(priority=0)
        _row_copy(h_ref, r, hs_ref, pos_ref[n_tok + base + r], sem).start(priority=1)
        return carry

    lax.fori_loop(0, tm, issue, 0, unroll=8)
    for _ in range(2):
        pltpu.make_async_copy(h_ref, hs_ref.at[pl.ds(0, tm)], sem).wait()


def moe_dispatch(pos, h, hs):
    tokens = h.shape[0]
    tm = TM_ROUTE
    return pl.pallas_call(
        _dispatch_kernel,
        out_shape=jax.ShapeDtypeStruct(hs.shape, hs.dtype),
        grid_spec=pltpu.PrefetchScalarGridSpec(
            num_scalar_prefetch=1,
            grid=(tokens // tm,),
            in_specs=[pl.BlockSpec((tm, D), lambda i, pos: (i, 0)),
                      pl.BlockSpec(memory_space=pl.ANY)],
            out_specs=pl.BlockSpec(memory_space=pl.ANY),
            scratch_shapes=[pltpu.SemaphoreType.DMA(())]),
        input_output_aliases={2: 0},
        compiler_params=_cparams(("arbitrary",)),
        name="moe_dispatch",
    )(pos, h, hs)


def _experts_kernel(te_ref, sg_ref, su_ref, sd_ref, nu_ref, hs_ref, wg_ref, wu_ref, wd_ref, y_ref,
                    wg_sc, wu_sc, wd_sc):
    del sg_ref, su_ref, sd_ref
    j = pl.program_id(0)
    e = te_ref[j]
    e_prev = te_ref[jnp.maximum(j - 1, 0)]

    @pl.when((j == 0) | (e != e_prev))
    def _():
        wg_sc[...] = wg_ref[...].astype(BF16)
        wu_sc[...] = wu_ref[...].astype(BF16)
        wd_sc[...] = wd_ref[...].astype(BF16)

    @pl.when(j < nu_ref[0])
    def _():
        h = hs_ref[...].astype(BF16)
        y = None
        for c0 in range(0, D_FF_EXPERT, MOE_CHUNK):
            c1 = min(c0 + MOE_CHUNK, D_FF_EXPERT)
            hg = jnp.dot(h, wg_sc[:, c0:c1], preferred_element_type=F32)
            hu = jnp.dot(h, wu_sc[:, c0:c1], preferred_element_type=F32)
            act = (_silu(hg) * hu).astype(BF16)
            yc = jnp.dot(act, wd_sc[c0:c1, :], preferred_element_type=F32)
            y = yc if y is None else y + yc
        y_ref[...] = y

    @pl.when(j >= nu_ref[0])
    def _():
        y_ref[...] = jnp.zeros_like(y_ref)


def moe_experts(tile_expert, stages, n_used, hs, e_gate, e_up, e_down):
    rows = hs.shape[0]
    tmr = TM_EXPERT
    wspec = lambda shape, k: pl.BlockSpec((None,) + shape, lambda j, *pf: (pf[1 + k][j], 0, 0))
    return pl.pallas_call(
        _experts_kernel,
        out_shape=jax.ShapeDtypeStruct((rows, D), F32),
        grid_spec=pltpu.PrefetchScalarGridSpec(
            num_scalar_prefetch=5,
            grid=(rows // tmr,),
            in_specs=[pl.BlockSpec((tmr, D), lambda j, *pf: (j, 0)),
                      wspec((D, D_FF_EXPERT), 0), wspec((D, D_FF_EXPERT), 1), wspec((D_FF_EXPERT, D), 2)],
            out_specs=pl.BlockSpec((tmr, D), lambda j, *pf: (j, 0)),
            scratch_shapes=[pltpu.VMEM((D, D_FF_EXPERT), BF16), pltpu.VMEM((D, D_FF_EXPERT), BF16),
                            pltpu.VMEM((D_FF_EXPERT, D), BF16)]),
        compiler_params=_cparams(("arbitrary",)),
        name="moe_experts",
    )(tile_expert, *stages, n_used, hs, e_gate, e_up, e_down)


def _combine_kernel(pos_ref, x_ref, meta_ref, gt_ref, fg_ref, y_ref, o_ref, b1_sc, b2_sc, sem):
    tm = x_ref.shape[0]
    n_tok = pos_ref.shape[0] // 2
    base = pl.program_id(0) * tm

    def issue(r, carry):
        _row_copy(y_ref, pos_ref[base + r], b1_sc, r, sem).start(priority=0)
        _row_copy(y_ref, pos_ref[n_tok + base + r], b2_sc, r, sem).start(priority=1)
        return carry

    lax.fori_loop(0, tm, issue, 0, unroll=8)
    pltpu.make_async_copy(y_ref.at[pl.ds(0, tm)], b1_sc, sem).wait()
    pltpu.make_async_copy(y_ref.at[pl.ds(0, tm)], b2_sc, sem).wait()

    meta = meta_ref[...]
    lane = lax.broadcasted_iota(jnp.int32, meta.shape, 1)
    g1 = jnp.sum(jnp.where(lane == META_G1, meta, 0.0), axis=-1, keepdims=True)
    g2 = jnp.sum(jnp.where(lane == META_G2, meta, 0.0), axis=-1, keepdims=True)
    x = x_ref[...] + gt_ref[...] * (g1 * b1_sc[...] + g2 * b2_sc[...])
    o_ref[...] = _rms(x, fg_ref[...])


def moe_combine(pos, x, meta, modtab, cond, final_g, y):
    tokens = x.shape[0]
    tm = TM_COMBINE
    return pl.pallas_call(
        _combine_kernel,
        out_shape=jax.ShapeDtypeStruct((tokens, D), F32),
        grid_spec=pltpu.PrefetchScalarGridSpec(
            num_scalar_prefetch=1,
            grid=(tokens // tm,),
            in_specs=[pl.BlockSpec((tm, D), lambda i, pos: (i, 0)),
                      pl.BlockSpec((tm, V7X_LANES), lambda i, pos: (i, 0)),
                      _mod_spec(5, cond, tm, D, lambda i, pos: 0),
                      pl.BlockSpec((1, D), lambda i, pos: (0, 0)),
                      pl.BlockSpec(memory_space=pl.ANY)],
            out_specs=pl.BlockSpec((tm, D), lambda i, pos: (i, 0)),
            scratch_shapes=[pltpu.VMEM((tm, D), F32), pltpu.VMEM((tm, D), F32), pltpu.SemaphoreType.DMA(())]),
        compiler_params=_cparams(("arbitrary",)),
        name="moe_combine",
    )(pos, x, meta, modtab, final_g.reshape(1, D), y)


def moe_plan(metas, counts):
    tmr = TM_EXPERT
    cnts = [c[0, :N_EXPERTS].astype(jnp.int32) for c in counts]
    total = functools.reduce(jnp.add, cnts)
    padded = ((total + tmr - 1) // tmr) * tmr
    ends = jnp.cumsum(padded)
    starts = ends - padded
    n_rows = sum(m.shape[1] for m in metas) * 2 + N_EXPERTS * tmr
    n_tiles = n_rows // tmr
    tile_start = jnp.arange(n_tiles, dtype=jnp.int32) * tmr
    tile_expert = jnp.minimum(jnp.sum(tile_start[:, None] >= ends[None, :], axis=1), N_EXPERTS - 1).astype(jnp.int32)
    n_used = (ends[-1] // tmr).astype(jnp.int32).reshape(1)
    eid = jnp.arange(N_EXPERTS, dtype=jnp.int32)
    later = jnp.where((eid[None, :] > eid[:, None]) & (padded[None, :] > 0), eid[None, :], N_EXPERTS)
    nxt = jnp.min(later, axis=1)
    next_used = jnp.where(nxt == N_EXPERTS, eid, nxt)
    pick = lambda table: jnp.sum(jnp.where(tile_expert[:, None] == eid[None, :], table[None, :], 0), axis=1)
    k_in_group = (tile_start - pick(starts)) // tmr
    tile_next = pick(next_used)
    stages = [jnp.where(k_in_group < k, tile_expert, tile_next).astype(jnp.int32) for k in (1, 2, 3)]
    pos = []
    base = jnp.zeros((N_EXPERTS,), jnp.int32)
    for m, c in zip(metas, cnts):
        first = starts + base
        sel = lambda field: m[field].astype(jnp.int32)
        lookup = lambda e: jnp.sum(jnp.where(e[:, None] == jnp.arange(N_EXPERTS)[None, :], first[None, :], 0), axis=1)
        p1 = lookup(sel(META_E1)) + sel(META_R1)
        p2 = lookup(sel(META_E2)) + sel(META_R2)
        pos.append(jnp.concatenate([p1, p2]).astype(jnp.int32))
        base = base + c
    return pos, tile_expert, stages, n_used, n_rows


def _pad_to(a, shape):
    return jnp.pad(a, [(0, t - s) for s, t in zip(a.shape, shape)])


def _regroup_chunks(a, cb):
    r = a.shape[0]
    return a.reshape(r, 3, HY_W // cb, cb).transpose(2, 0, 1, 3).reshape(HY_W // cb, r, 3 * cb)


def kernel(x_prompt, x_sample, state_l0_lru, cache_l1_ckv, cache_l1_krope, c, c_ctx, l0_norm1, l0_norm2, l0_w_mod, l0_b_mod, l0_w_in, l0_conv_a, l0_lru_conv_w, l0_lru_conv_b, l0_lru_wa, l0_lru_ba, l0_lru_wi, l0_lru_bi, l0_lru_lambda, l0_w_out, l0_ffn_gate, l0_ffn_up, l0_ffn_down, l1_norm1, l1_norm2, l1_w_mod, l1_b_mod, l1_w_in, l1_q_norm, l1_kv_norm, l1_w_q_up, l1_w_kv_up, l1_hy_short_w, l1_hy_short_b, l1_hy_f_w1, l1_hy_f_b1, l1_hy_f_w2, l1_hy_f_b2, l1_hy_f_w3, l1_hy_bias, l1_w_out, l1_router_w, l1_router_b, l1_exp_gate, l1_exp_up, l1_exp_down, final_norm):
    batch, seq, _ = x_prompt.shape
    dec_batch, dec_seq, _ = x_sample.shape
    past_len = cache_l1_ckv.shape[1]

    cond8 = jnp.concatenate([c_ctx[None, :], c, jnp.zeros((V7X_SUBLANES - 1 - dec_batch, D), F32)], axis=0)
    wcat = jnp.concatenate([l0_lru_wa[0], l0_lru_wi[0], l0_lru_wa[1], l0_lru_wi[1]], axis=-1)
    wq = l1_w_q_up.reshape(Q_RANK, MLA_HEADS, QK_DIM)
    wq_perm = jnp.concatenate([wq[:, :, :NOPE].reshape(Q_RANK, MLA_HEADS * NOPE),
                               wq[:, :, NOPE:].reshape(Q_RANK, MLA_HEADS * ROPE)], axis=1)
    hid = V7X_LANES
    w1p = _pad_to(l1_hy_f_w1, (hid, hid))
    b1p = _pad_to(l1_hy_f_b1.reshape(1, -1), (1, hid))
    w2p = _pad_to(l1_hy_f_w2, (hid, hid))
    b2p = _pad_to(l1_hy_f_b2.reshape(1, -1), (1, hid))
    w3p = _pad_to(l1_hy_f_w3, (hid, 2 * HY_W))
    short_w = _regroup_chunks(l1_hy_short_w, HY_CB)
    short_b = _regroup_chunks(l1_hy_short_b.reshape(1, -1), HY_CB)
    hy_bias = l1_hy_bias.reshape(1, HY_W)
    wr_pad = _pad_to(l1_router_w, (D, V7X_LANES))
    br_pad = _pad_to(l1_router_b.reshape(1, -1), (1, V7X_LANES))

    mod0 = adaln_table(cond8, l0_w_mod, l0_b_mod)
    mod1 = adaln_table(cond8, l1_w_mod, l1_b_mod)

    kv_ctx = kv_up(cache_l1_ckv.reshape(dec_batch * past_len, KV_RANK), l1_w_kv_up)
    kr_ctx = cache_l1_krope.reshape(dec_batch * past_len, ROPE)

    def trunk(x, seq_len, cond, h0, latent):
        u = in0_proj(x, l0_norm1, mod0, cond, l0_w_in)
        ya = conv_a(u, seq_len, l0_conv_a)
        yb, lru_state = rglru(u, seq_len, l0_lru_conv_w, l0_lru_conv_b, wcat, l0_lru_ba, l0_lru_bi,
                              l0_lru_lambda, h0)
        x = out_res([(ya, 0), (yb, 0), (yb, 1)], l0_w_out, x, mod0, cond, 2)
        x = ffn_res(x, l0_norm2, mod0, cond, l0_ffn_gate, l0_ffn_up, l0_ffn_down)
        qnope, qpe, ckv, kr, kv, uh = in1_proj(x, l1_norm1, mod1, cond, l1_w_in, l1_q_norm, l1_kv_norm,
                                               wq_perm, l1_w_kv_up)
        if latent:
            yc = attn_lat(qnope, qpe, kv_ctx, kr_ctx, kv, kr, seq_len, past_len)
        else:
            yc = attn_ctx(qnope, qpe, kv, kr, seq_len)
        cs = dft_tables(seq_len)
        k_r, k_s, k_ny = hy_filter(cs, w1p, b1p, w2p, b2p, w3p)
        yd = hyena(uh, seq_len, short_w, short_b, cs, k_r, k_s, k_ny, hy_bias)
        x = out_res([(yc, 0), (yc, 1), (yd, 0)], l1_w_out, x, mod1, cond, 2)
        return x, lru_state, ckv, kr

    conds = ((0, batch * seq), (1, dec_seq))
    zeros_state = jnp.zeros((batch, 2, LRU_W), F32)
    x_p, new_lru, new_ckv, new_kr = trunk(x_prompt.reshape(batch * seq, D), seq, conds[0], zeros_state, latent=False)
    x_s, _, _, _ = trunk(x_sample.reshape(dec_batch * dec_seq, D), dec_seq, conds[1], state_l0_lru, latent=True)

    xs = (x_p, x_s)
    routed = [moe_route(x, l1_norm2, mod1, cond, wr_pad, br_pad) for x, cond in zip(xs, conds)]
    pos, tile_expert, stages, n_used, n_rows = moe_plan([r[2] for r in routed], [r[3] for r in routed])
    hs = jnp.zeros((n_rows, D), F32)
    for p, r in zip(pos, routed):
        hs = moe_dispatch(p, r[0], hs)
    y_rows = moe_experts(tile_expert, stages, n_used, hs, l1_exp_gate, l1_exp_up, l1_exp_down)
    y_p, y_s = [moe_combine(p, x, r[1], mod1, cond, final_norm, y_rows)
                for p, x, r, cond in zip(pos, xs, routed, conds)]
    return (y_p.reshape(batch, seq, D), y_s.reshape(dec_batch, dec_seq, D), new_lru,
            new_ckv.reshape(batch, seq, KV_RANK), new_kr.reshape(batch, seq, ROPE))
```

```python
import functools
import math

import jax
import jax.numpy as jnp
from jax import lax
from jax.experimental import pallas as pl
from jax.experimental.pallas import tpu as pltpu

F32 = jnp.float32
BF16 = jnp.bfloat16
HIGHEST = lax.Precision.HIGHEST

D = 1024
GRID_W = 64
EPS = 1e-6
CONV_W = 512
LRU_W = 1024
LRU_BW = 128
LRU_C = 8.0
MLA_HEADS = 8
Q_RANK = 384
KV_RANK = 256
NOPE = 128
ROPE = 64
VDIM = 128
QK_DIM = NOPE + ROPE
ROPE_THETA = 10000.0
HY_W = 512
HY_BANDS = 16
HY_TARGET = 1e-2
HY_FAST_DECAY = 0.3
HY_SLOW_DECAY = 1.5
D_FF = 2816
N_EXPERTS = 8
D_FF_EXPERT = 1408
IN0 = 3 * CONV_W + 2 * LRU_W
IN1 = Q_RANK + KV_RANK + ROPE + 3 * HY_W

V7X_LANES = 128
V7X_SUBLANES = 8
V7X_VMEM_LIMIT_BYTES = 56 * 1024 * 1024

TM = 512
TN_IN0 = 512
TF_FFN = 256
MOE_CHUNK = 256
TM_ROUTE = 512
TM_EXPERT = 256
TM_COMBINE = 256
LRU_CB = 256
HY_CB = 256
TQ = 256
ATTN_CTX_SEQS = 4
CONV_A_ROWS = 1024
LRU_ROWS = 1024
HY_ROWS = 1024
TM_IN1 = 256


def _cparams(sem):
    return pltpu.CompilerParams(dimension_semantics=sem, vmem_limit_bytes=V7X_VMEM_LIMIT_BYTES)


def _sigmoid(x):
    return 0.5 * jnp.tanh(0.5 * x) + 0.5


def _silu(x):
    return x * _sigmoid(x)


def _norm_mod(x, g, shift, scale):
    ms = jnp.mean(x * x, axis=-1, keepdims=True)
    y = x * lax.rsqrt(ms + EPS) * g
    return y * (1.0 + scale) + shift


def _mod_spec(comp, cond, tm, width, col_fn, tile_fn=lambda *ids: ids[0]):
    row0, seg = cond
    assert seg % tm == 0
    return pl.BlockSpec((None, 1, width),
                        lambda *ids: (comp * 3 + row0 + (tile_fn(*ids) * tm) // seg, 0, col_fn(*ids)))


def _dot3(a, b):
    a_hi = a.astype(BF16)
    a_lo = (a - a_hi.astype(F32)).astype(BF16)
    b_hi = b.astype(BF16)
    b_lo = (b - b_hi.astype(F32)).astype(BF16)
    n = a.shape[0]
    y = jnp.dot(jnp.concatenate([a_hi, a_lo], axis=0), b_hi, preferred_element_type=F32)
    return y[:n] + y[n:] + jnp.dot(a_hi, b_lo, preferred_element_type=F32)


def _adaln_kernel(c_ref, w_ref, b_ref, o_ref):
    o_ref[...] = _dot3(_silu(c_ref[...]), w_ref[...]) + b_ref[...]


def adaln_table(cond8, w_mod, b_mod):
    tn = 1536
    m = pl.pallas_call(
        _adaln_kernel,
        out_shape=jax.ShapeDtypeStruct((V7X_SUBLANES, 6 * D), F32),
        grid=(6 * D // tn,),
        in_specs=[pl.BlockSpec((V7X_SUBLANES, D), lambda j: (0, 0)),
                  pl.BlockSpec((D, tn), lambda j: (0, j)),
                  pl.BlockSpec((1, tn), lambda j: (0, j))],
        out_specs=pl.BlockSpec((V7X_SUBLANES, tn), lambda j: (0, j)),
        compiler_params=_cparams(("arbitrary",)),
        name="adaln",
    )(cond8, w_mod, b_mod.reshape(1, 6 * D))
    return m[:3].reshape(3, 6, D).transpose(1, 0, 2).reshape(18, 1, D)


def _tile_of(n_load):
    return lambda s: jnp.maximum(s - n_load, 0)


def _block_of(n_load):
    return lambda s: jnp.minimum(s, n_load - 1)


def _in0_kernel(x_ref, g_ref, sh_ref, sc_ref, w_ref, o_ref, w_sc):
    s = pl.program_id(0)
    n_load, _, tn = w_sc.shape

    @pl.when(s < n_load)
    def _():
        w_sc[s] = w_ref[...].astype(BF16)

    @pl.when(s >= n_load)
    def _():
        h = _norm_mod(x_ref[...], g_ref[...], sh_ref[...], sc_ref[...]).astype(BF16)
        for j in range(n_load):
            o_ref[:, j * tn:(j + 1) * tn] = jnp.dot(h, w_sc[j], preferred_element_type=F32).astype(o_ref.dtype)


def in0_proj(x, g, modtab, cond, w_in):
    tn = TN_IN0
    tokens = x.shape[0]
    n = w_in.shape[1]
    n_load = n // tn
    tile = _tile_of(n_load)
    blk = _block_of(n_load)
    zero = lambda s: 0
    return pl.pallas_call(
        _in0_kernel,
        out_shape=jax.ShapeDtypeStruct((tokens, n), BF16),
        grid=(n_load + tokens // TM,),
        in_specs=[pl.BlockSpec((TM, D), lambda s: (tile(s), 0)),
                  pl.BlockSpec((1, D), lambda s: (0, 0)),
                  _mod_spec(0, cond, TM, D, zero, tile),
                  _mod_spec(1, cond, TM, D, zero, tile),
                  pl.BlockSpec((D, tn), lambda s: (0, blk(s)))],
        out_specs=pl.BlockSpec((TM, n), lambda s: (tile(s), 0)),
        scratch_shapes=[pltpu.VMEM((n_load, D, tn), BF16)],
        compiler_params=_cparams(("arbitrary",)),
        name="in0_proj",
    )(x, g.reshape(1, D), modtab, modtab, w_in)


def _shift_rows(v, d, t, seq_len=None):
    n = v.shape[0]
    seq_len = n if seq_len is None else seq_len
    if d > 0:
        return jnp.where(t < d, 0.0, pltpu.roll(v, d, 0))
    return jnp.where(t >= seq_len + d, 0.0, pltpu.roll(v, n + d, 0))


def _conv_a_kernel(b_ref, c_ref, x_ref, w_ref, o_ref, *, seq_len):
    v = c_ref[...].astype(F32) * x_ref[...].astype(F32)
    t = lax.broadcasted_iota(jnp.int32, v.shape, 0) & (seq_len - 1)
    w = w_ref[...]
    y = w[0:1] * _shift_rows(v, 1, t, seq_len) + w[1:2] * v + w[2:3] * _shift_rows(v, -1, t, seq_len)
    o_ref[...] = (b_ref[...].astype(F32) * y).astype(o_ref.dtype)


def conv_a(u, seq_len, conv_w):
    tokens = u.shape[0]
    rows = max(seq_len, CONV_A_ROWS)
    assert seq_len & (seq_len - 1) == 0 and rows % seq_len == 0
    return pl.pallas_call(
        functools.partial(_conv_a_kernel, seq_len=seq_len),
        out_shape=jax.ShapeDtypeStruct((tokens, CONV_W), BF16),
        grid=(tokens // rows,),
        in_specs=[pl.BlockSpec((rows, CONV_W), lambda s: (s, 0)),
                  pl.BlockSpec((rows, CONV_W), lambda s: (s, 1)),
                  pl.BlockSpec((rows, CONV_W), lambda s: (s, 2)),
                  pl.BlockSpec((3, CONV_W), lambda s: (0, 0))],
        out_specs=pl.BlockSpec((rows, CONV_W), lambda s: (s, 0)),
        compiler_params=_cparams(("parallel",)),
        name="conv_a",
    )(u, u, u, conv_w)


def _group_scan(a, b, reverse):
    n, c = a.shape
    a3 = a.reshape(n // V7X_SUBLANES, V7X_SUBLANES, c)
    b3 = b.reshape(n // V7X_SUBLANES, V7X_SUBLANES, c)
    t8 = lax.broadcasted_iota(jnp.int32, a3.shape, 1)
    for d in (1, 2, 4):
        if reverse:
            keep = t8 < V7X_SUBLANES - d
            shift = V7X_SUBLANES - d
        else:
            keep = t8 >= d
            shift = d
        a_sh = jnp.where(keep, pltpu.roll(a3, shift, 1), 1.0)
        b_sh = jnp.where(keep, pltpu.roll(b3, shift, 1), 0.0)
        b3 = a3 * b_sh + b3
        a3 = a3 * a_sh
    return a3.reshape(n, c), b3.reshape(n, c)


def _rglru_kernel(gate_ref, xb_ref, cw_ref, cb_ref, wcat_ref, ba_ref, bi_ref, lam_ref, h0_ref,
                  y_ref, st_ref, af_sc, bf_sc, ab_sc, bb_sc, hf_sc, hb_sc, *, seq_len):
    n, cb = xb_ref.shape
    n_seq = n // seq_len
    xb = xb_ref[...].astype(F32)
    t = lax.broadcasted_iota(jnp.int32, xb.shape, 0) & (seq_len - 1)
    cw = cw_ref[...]
    sh = lambda d: _shift_rows(xb, d, t, seq_len)
    xc = cb_ref[...] + cw[0:1] * sh(2) + cw[1:2] * sh(1) + cw[2:3] * xb + cw[3:4] * sh(-1)
    xcb = xc.astype(BF16)
    g = [jnp.dot(xcb[:, k * LRU_BW:(k + 1) * LRU_BW], wcat_ref[k].astype(BF16), preferred_element_type=F32)
         for k in range(cb // LRU_BW)]

    def direction(d):
        ga = jnp.concatenate([gk[:, (2 * d) * LRU_BW:(2 * d + 1) * LRU_BW] for gk in g], axis=1)
        gi = jnp.concatenate([gk[:, (2 * d + 1) * LRU_BW:(2 * d + 2) * LRU_BW] for gk in g], axis=1)
        r = _sigmoid(ga + ba_ref[d:d + 1, :])
        i = _sigmoid(gi + bi_ref[d:d + 1, :])
        log_a = (-LRU_C * jax.nn.softplus(-lam_ref[d:d + 1, :])) * r
        a = jnp.exp(log_a)
        m = 1.0 - a * a
        mult = m * lax.rsqrt(jnp.maximum(m, 1e-30))
        return a, mult * (i * xc)

    a_f, b_f = direction(0)
    a_f, b_f = _group_scan(a_f, b_f, reverse=False)
    af_sc[...] = a_f
    bf_sc[...] = b_f
    a_b, b_b = direction(1)
    a_b, b_b = _group_scan(a_b, b_b, reverse=True)
    ab_sc[...] = a_b
    bb_sc[...] = b_b

    ng = seq_len // V7X_SUBLANES
    bcast = lambda row: jnp.broadcast_to(row, (V7X_SUBLANES, cb))
    init = tuple((bcast(h0_ref[q, 0:1, :]), bcast(h0_ref[q, 1:2, :])) for q in range(n_seq))

    def step(k, carry):
        out = []
        for q, (hf_in, hb_in) in enumerate(carry):
            rf = pl.multiple_of(q * seq_len + k * V7X_SUBLANES, V7X_SUBLANES)
            rb = pl.multiple_of(q * seq_len + (ng - 1 - k) * V7X_SUBLANES, V7X_SUBLANES)
            hf = af_sc[pl.ds(rf, V7X_SUBLANES), :] * hf_in + bf_sc[pl.ds(rf, V7X_SUBLANES), :]
            hb = ab_sc[pl.ds(rb, V7X_SUBLANES), :] * hb_in + bb_sc[pl.ds(rb, V7X_SUBLANES), :]
            hf_sc[pl.ds(rf, V7X_SUBLANES), :] = hf
            hb_sc[pl.ds(rb, V7X_SUBLANES), :] = hb
            out.append((bcast(hf[V7X_SUBLANES - 1:V7X_SUBLANES]), bcast(hb[0:1])))
        return tuple(out)

    final = lax.fori_loop(0, ng, step, init)
    for q, (hf_last, hb_first) in enumerate(final):
        st_ref[q, 0:1, :] = hf_last[0:1]
        st_ref[q, 1:2, :] = hb_first[0:1]

    gt = gate_ref[...].astype(F32)
    gelu = 0.5 * gt * (1.0 + jnp.tanh(math.sqrt(2.0 / math.pi) * (gt + 0.044715 * (gt * gt * gt))))
    y_ref[...] = ((hf_sc[...] + hb_sc[...]) * gelu).astype(y_ref.dtype)


def rglru(u, seq_len, conv_w, conv_b, wcat, ba, bi, lam, h0):
    tokens = u.shape[0]
    nseq = tokens // seq_len
    cb = LRU_CB
    rows = max(seq_len, LRU_ROWS)
    assert seq_len & (seq_len - 1) == 0 and rows % seq_len == 0
    per_blk = rows // seq_len
    gate_blk0 = 3 * CONV_W // cb
    xb_blk0 = (3 * CONV_W + LRU_W) // cb
    seq_scr = lambda: pltpu.VMEM((rows, cb), F32)
    return pl.pallas_call(
        functools.partial(_rglru_kernel, seq_len=seq_len),
        out_shape=(jax.ShapeDtypeStruct((tokens, LRU_W), BF16), jax.ShapeDtypeStruct((nseq, 2, LRU_W), F32)),
        grid=(tokens // rows, LRU_W // cb),
        in_specs=[pl.BlockSpec((rows, cb), lambda s, c: (s, gate_blk0 + c)),
                  pl.BlockSpec((rows, cb), lambda s, c: (s, xb_blk0 + c)),
                  pl.BlockSpec((4, cb), lambda s, c: (0, c)),
                  pl.BlockSpec((1, cb), lambda s, c: (0, c)),
                  pl.BlockSpec((cb // LRU_BW, LRU_BW, 4 * LRU_BW), lambda s, c: (c, 0, 0)),
                  pl.BlockSpec((2, cb), lambda s, c: (0, c)),
                  pl.BlockSpec((2, cb), lambda s, c: (0, c)),
                  pl.BlockSpec((2, cb), lambda s, c: (0, c)),
                  pl.BlockSpec((per_blk, 2, cb), lambda s, c: (s, 0, c))],
        out_specs=(pl.BlockSpec((rows, cb), lambda s, c: (s, c)),
                   pl.BlockSpec((per_blk, 2, cb), lambda s, c: (s, 0, c))),
        scratch_shapes=[seq_scr() for _ in range(6)],
        compiler_params=_cparams(("parallel", "parallel")),
        name="rglru",
    )(u, u, conv_w, conv_b.reshape(1, LRU_W), wcat, ba, bi, lam, h0)


def _out_res_kernel(p0_ref, p1_ref, p2_ref, w_ref, x_ref, gt_ref, o_ref, w_sc):
    s = pl.program_id(0)
    n_load, kb, _ = w_sc.shape

    @pl.when(s < n_load)
    def _():
        w_sc[s] = w_ref[...].astype(BF16)

    @pl.when(s >= n_load)
    def _():
        y = jnp.dot(p0_ref[...], w_sc[0], preferred_element_type=F32)
        y += jnp.dot(p1_ref[...], w_sc[1], preferred_element_type=F32)
        y += jnp.dot(p2_ref[...], w_sc[2], preferred_element_type=F32)
        o_ref[...] = x_ref[...] + gt_ref[...] * y


def out_res(parts, w_out, x, modtab, cond, gate_comp):
    tokens = x.shape[0]
    kb = 512
    n_load = len(parts)
    tile = _tile_of(n_load)
    blk = _block_of(n_load)
    lhs_specs = [pl.BlockSpec((TM, kb), (lambda s, cbk=cbk: (tile(s), cbk))) for _, cbk in parts]
    return pl.pallas_call(
        _out_res_kernel,
        out_shape=jax.ShapeDtypeStruct((tokens, D), F32),
        grid=(n_load + tokens // TM,),
        in_specs=lhs_specs + [pl.BlockSpec((kb, D), lambda s: (blk(s), 0)),
                              pl.BlockSpec((TM, D), lambda s: (tile(s), 0)),
                              _mod_spec(gate_comp, cond, TM, D, lambda s: 0, tile)],
        out_specs=pl.BlockSpec((TM, D), lambda s: (tile(s), 0)),
        scratch_shapes=[pltpu.VMEM((n_load, kb, D), BF16)],
        compiler_params=_cparams(("arbitrary",)),
        name="out_res",
    )(*[a for a, _ in parts], w_out, x, modtab)


def _ffn_kernel(x_ref, g_ref, sh_ref, sc_ref, gt_ref, wg_ref, wu_ref, wd_ref, o_ref, wg_sc, wu_sc, wd_sc):
    s = pl.program_id(0)
    n_load = wg_sc.shape[0]

    @pl.when(s < n_load)
    def _():
        wg_sc[s] = wg_ref[...].astype(BF16)
        wu_sc[s] = wu_ref[...].astype(BF16)
        wd_sc[s] = wd_ref[...].astype(BF16)

    @pl.when(s >= n_load)
    def _():
        x = x_ref[...]
        h = _norm_mod(x, g_ref[...], sh_ref[...], sc_ref[...]).astype(BF16)
        y = None
        for f in range(n_load):
            hg = jnp.dot(h, wg_sc[f], preferred_element_type=F32)
            hu = jnp.dot(h, wu_sc[f], preferred_element_type=F32)
            act = (_silu(hg) * hu).astype(BF16)
            yf = jnp.dot(act, wd_sc[f], preferred_element_type=F32)
            y = yf if y is None else y + yf
        o_ref[...] = x + gt_ref[...] * y


def ffn_res(x, g, modtab, cond, w_gate, w_up, w_down):
    tokens = x.shape[0]
    tf = TF_FFN
    n_load = D_FF // tf
    tile = _tile_of(n_load)
    blk = _block_of(n_load)
    zero = lambda s: 0
    return pl.pallas_call(
        _ffn_kernel,
        out_shape=jax.ShapeDtypeStruct((tokens, D), F32),
        grid=(n_load + tokens // TM,),
        in_specs=[pl.BlockSpec((TM, D), lambda s: (tile(s), 0)),
                  pl.BlockSpec((1, D), lambda s: (0, 0)),
                  _mod_spec(3, cond, TM, D, zero, tile),
                  _mod_spec(4, cond, TM, D, zero, tile),
                  _mod_spec(5, cond, TM, D, zero, tile),
                  pl.BlockSpec((D, tf), lambda s: (0, blk(s))),
                  pl.BlockSpec((D, tf), lambda s: (0, blk(s))),
                  pl.BlockSpec((tf, D), lambda s: (blk(s), 0))],
        out_specs=pl.BlockSpec((TM, D), lambda s: (tile(s), 0)),
        scratch_shapes=[pltpu.VMEM((n_load, D, tf), BF16), pltpu.VMEM((n_load, D, tf), BF16),
                        pltpu.VMEM((n_load, tf, D), BF16)],
        compiler_params=_cparams(("arbitrary",)),
        name="ffn_res",
    )(x, g.reshape(1, D), modtab, modtab, modtab, w_gate, w_up, w_down)


def _rms(x, g):
    return x * lax.rsqrt(jnp.mean(x * x, axis=-1, keepdims=True) + EPS) * g


def _in1_kernel(x_ref, g_ref, sh_ref, sc_ref, w_ref, qn_ref, kvn_ref, wq_ref, wkv_ref,
                qnope_ref, qpe_ref, ckv_ref, kr_ref, kv_ref, uh_ref, w_sc, wq_sc, wkv_sc):
    @pl.when(pl.program_id(0) == 0)
    def _():
        w_sc[...] = w_ref[...].astype(BF16)
        wq_sc[...] = wq_ref[...].astype(BF16)
        wkv_sc[...] = wkv_ref[...].astype(BF16)

    h = _norm_mod(x_ref[...], g_ref[...], sh_ref[...], sc_ref[...]).astype(BF16)
    u = jnp.dot(h, w_sc[...], preferred_element_type=F32)
    o1, o2, o3 = Q_RANK, Q_RANK + KV_RANK, Q_RANK + KV_RANK + ROPE
    cq = _rms(u[:, :o1], qn_ref[...])
    q = jnp.dot(cq.astype(BF16), wq_sc[...], preferred_element_type=F32) * _SCALE
    qnope_ref[...] = q[:, :MLA_HEADS * NOPE].astype(qnope_ref.dtype)
    qpe_ref[...] = q[:, MLA_HEADS * NOPE:]
    ckv = _rms(u[:, o1:o2], kvn_ref[...])
    ckv_ref[...] = ckv
    kv_ref[...] = jnp.dot(ckv.astype(BF16), wkv_sc[...], preferred_element_type=F32).astype(kv_ref.dtype)
    kr_ref[...] = u[:, o2:o3]
    uh_ref[...] = u[:, o3:]


def in1_proj(x, g, modtab, cond, w_in, q_norm, kv_norm, wq_perm, w_kv_up):
    tokens = x.shape[0]
    tm = TM_IN1
    nkv = MLA_HEADS * (NOPE + VDIM)
    const = lambda i: (0, 0)
    zero = lambda i: 0
    once = pl.Buffered(1)
    outs = (jax.ShapeDtypeStruct((tokens, MLA_HEADS * NOPE), BF16),
            jax.ShapeDtypeStruct((tokens, MLA_HEADS * ROPE), F32),
            jax.ShapeDtypeStruct((tokens, KV_RANK), F32),
            jax.ShapeDtypeStruct((tokens, ROPE), F32),
            jax.ShapeDtypeStruct((tokens, nkv), BF16),
            jax.ShapeDtypeStruct((tokens, 3 * HY_W), F32))
    row = lambda w: pl.BlockSpec((tm, w), lambda i: (i, 0))
    return pl.pallas_call(
        _in1_kernel,
        out_shape=outs,
        grid=(tokens // tm,),
        in_specs=[row(D),
                  pl.BlockSpec((1, D), const),
                  _mod_spec(0, cond, tm, D, zero),
                  _mod_spec(1, cond, tm, D, zero),
                  pl.BlockSpec((D, IN1), const, pipeline_mode=once),
                  pl.BlockSpec((1, Q_RANK), const),
                  pl.BlockSpec((1, KV_RANK), const),
                  pl.BlockSpec((Q_RANK, MLA_HEADS * QK_DIM), const, pipeline_mode=once),
                  pl.BlockSpec((KV_RANK, nkv), const, pipeline_mode=once)],
        out_specs=tuple(row(o.shape[1]) for o in outs),
        scratch_shapes=[pltpu.VMEM((D, IN1), BF16), pltpu.VMEM((Q_RANK, MLA_HEADS * QK_DIM), BF16),
                        pltpu.VMEM((KV_RANK, nkv), BF16)],
        compiler_params=_cparams(("arbitrary",)),
        name="in1_proj",
    )(x, g.reshape(1, D), modtab, modtab, w_in, q_norm.reshape(1, Q_RANK), kv_norm.reshape(1, KV_RANK),
      wq_perm, w_kv_up)


def _mm_kernel(a_ref, w_ref, o_ref):
    o_ref[...] = jnp.dot(a_ref[...].astype(BF16), w_ref[...].astype(BF16),
                         preferred_element_type=F32).astype(o_ref.dtype)


def kv_up(ckv, w_kv_up):
    rows = ckv.shape[0]
    n = w_kv_up.shape[1]
    return pl.pallas_call(
        _mm_kernel,
        out_shape=jax.ShapeDtypeStruct((rows, n), BF16),
        grid=(rows // TM,),
        in_specs=[pl.BlockSpec((TM, KV_RANK), lambda i: (i, 0)), pl.BlockSpec((KV_RANK, n), lambda i: (0, 0))],
        out_specs=pl.BlockSpec((TM, n), lambda i: (i, 0)),
        compiler_params=_cparams(("parallel",)),
        name="kv_up",
    )(ckv, w_kv_up)


_NT = (((1,), (1,)), ((), ()))
_SCALE = 1.0 / math.sqrt(QK_DIM)


def _fill_rope_tables(cos_ref, sin_ref):
    n, width = cos_ref.shape
    n_grid_rows = n // GRID_W
    n_freq = ROPE // 4

    def trig(count):
        lane = lax.broadcasted_iota(jnp.int32, (count, width), 1)
        j = lane & (ROPE // 2 - 1)
        inv = jnp.exp((j & (n_freq - 1)).astype(F32) * (-math.log(ROPE_THETA) / n_freq))
        ang = lax.broadcasted_iota(jnp.int32, (count, width), 0).astype(F32) * inv
        return jnp.cos(ang), jnp.sin(ang), j < n_freq

    cos_c, sin_c, by_row = trig(GRID_W)
    cos_r, sin_r, _ = trig(n_grid_rows)
    for r in range(n_grid_rows):
        rows = slice(r * GRID_W, (r + 1) * GRID_W)
        cos_ref[rows, :] = jnp.where(by_row, jnp.broadcast_to(cos_r[r:r + 1], cos_c.shape), cos_c)
        sin_ref[rows, :] = jnp.where(by_row, jnp.broadcast_to(sin_r[r:r + 1], sin_c.shape), sin_c)


def _rope(x, cos, sin):
    width = x.shape[1]
    lane = lax.broadcasted_iota(jnp.int32, x.shape, 1)
    first_half = (lane & (ROPE - 1)) < ROPE // 2
    xr = jnp.where(first_half, -pltpu.roll(x, width - ROPE // 2, 1), pltpu.roll(x, ROPE // 2, 1))
    return x * cos + xr * sin


def _ones_column(n):
    lane = lax.broadcasted_iota(jnp.int32, (n, VDIM), 1)
    return jnp.where(lane == 0, 1.0, 0.0).astype(BF16)


def _head_attention(qcat, kcat, vaug):
    s = lax.dot_general(qcat, kcat, _NT, preferred_element_type=F32)
    p = jnp.exp(s - jnp.max(s, axis=-1, keepdims=True)).astype(BF16)
    oa = jnp.dot(p, vaug, preferred_element_type=F32)
    return oa[:, :VDIM] / oa[:, VDIM:VDIM + 1]


def _attn_ctx_kernel(qn_ref, qpe_ref, kv_ref, kr_ref, o_ref, *, seq_len):
    n = qn_ref.shape[0]
    n_seq = n // seq_len
    ones = _ones_column(n)
    kpe = kr_ref[...].astype(BF16)
    per_seq = lambda a: a.reshape(n_seq, seq_len, a.shape[-1])
    for h in range(MLA_HEADS):
        c0 = h * (NOPE + VDIM)
        qcat = per_seq(jnp.concatenate([qn_ref[:, h * NOPE:(h + 1) * NOPE],
                                        qpe_ref[:, h * ROPE:(h + 1) * ROPE].astype(BF16)], axis=1))
        kcat = per_seq(jnp.concatenate([kv_ref[:, c0:c0 + NOPE], kpe], axis=1))
        vaug = per_seq(jnp.concatenate([kv_ref[:, c0 + NOPE:c0 + NOPE + VDIM], ones], axis=1))
        s = jnp.einsum("bqd,bkd->bqk", qcat, kcat, preferred_element_type=F32)
        p = jnp.exp(s - jnp.max(s, axis=-1, keepdims=True)).astype(BF16)
        oa = jnp.einsum("bqk,bkd->bqd", p, vaug, preferred_element_type=F32)
        o = oa[:, :, :VDIM] / oa[:, :, VDIM:VDIM + 1]
        o_ref[:, h * VDIM:(h + 1) * VDIM] = o.reshape(n, VDIM).astype(o_ref.dtype)


def attn_ctx(qnope, qpe, kv, kr, seq_len):
    tokens = qnope.shape[0]
    rows = ATTN_CTX_SEQS * seq_len
    blk = lambda w: pl.BlockSpec((rows, w), lambda s: (s, 0))
    return pl.pallas_call(
        functools.partial(_attn_ctx_kernel, seq_len=seq_len),
        out_shape=jax.ShapeDtypeStruct((tokens, MLA_HEADS * VDIM), BF16),
        grid=(tokens // rows,),
        in_specs=[blk(MLA_HEADS * NOPE), blk(MLA_HEADS * ROPE), blk(MLA_HEADS * (NOPE + VDIM)), blk(ROPE)],
        out_specs=blk(MLA_HEADS * VDIM),
        compiler_params=_cparams(("parallel",)),
        name="attn_ctx",
    )(qnope, qpe, kv, kr)


def _attn_lat_kernel(qn_ref, qpe_ref, kvc_ref, krc_ref, kvl_ref, krl_ref, o_ref, kcat_sc, vaug_sc, cos_sc, sin_sc):
    tq = qn_ref.shape[0]
    n_ctx = krc_ref.shape[0]
    n_lat = krl_ref.shape[0]

    @pl.when(pl.program_id(1) == 0)
    def _():
        _fill_rope_tables(cos_sc, sin_sc)
        kr2 = jnp.concatenate([krl_ref[...], krl_ref[...]], axis=1)
        kpe_lat = _rope(kr2, cos_sc[...], sin_sc[...])[:, :ROPE].astype(BF16)
        kpe_ctx = krc_ref[...].astype(BF16)
        ones_c, ones_l = _ones_column(n_ctx), _ones_column(n_lat)
        for h in range(MLA_HEADS):
            c0 = h * (NOPE + VDIM)
            for r0, nr, kv_ref, kpe, ones in ((0, n_ctx, kvc_ref, kpe_ctx, ones_c), (n_ctx, n_lat, kvl_ref, kpe_lat, ones_l)):
                kcat_sc[h, r0:r0 + nr, 0:NOPE] = kv_ref[:, c0:c0 + NOPE]
                kcat_sc[h, r0:r0 + nr, NOPE:QK_DIM] = kpe
                vaug_sc[h, r0:r0 + nr, 0:VDIM] = kv_ref[:, c0 + NOPE:c0 + NOPE + VDIM]
                vaug_sc[h, r0:r0 + nr, VDIM:2 * VDIM] = ones

    q0 = pl.multiple_of(pl.program_id(1) * tq, tq)
    rep = lambda a: jnp.concatenate([a] * (MLA_HEADS // 2), axis=1)
    qp_all = _rope(qpe_ref[...], rep(cos_sc[pl.ds(q0, tq), :]), rep(sin_sc[pl.ds(q0, tq), :])).astype(BF16)
    for h in range(MLA_HEADS):
        qcat = jnp.concatenate([qn_ref[:, h * NOPE:(h + 1) * NOPE], qp_all[:, h * ROPE:(h + 1) * ROPE]], axis=1)
        o_ref[:, h * VDIM:(h + 1) * VDIM] = _head_attention(qcat, kcat_sc[h], vaug_sc[h]).astype(o_ref.dtype)


def attn_lat(qnope, qpe, kv_ctx, kr_ctx, kv_lat, kr_lat, seq_len, ctx_len):
    tokens = qnope.shape[0]
    nq = seq_len // TQ
    qblk = lambda w: pl.BlockSpec((TQ, w), lambda b, i: (b * nq + i, 0))
    seq = lambda n, w: pl.BlockSpec((n, w), lambda b, i: (b, 0))
    nkv = MLA_HEADS * (NOPE + VDIM)
    n_keys = ctx_len + seq_len
    return pl.pallas_call(
        _attn_lat_kernel,
        out_shape=jax.ShapeDtypeStruct((tokens, MLA_HEADS * VDIM), BF16),
        grid=(tokens // seq_len, nq),
        in_specs=[qblk(MLA_HEADS * NOPE), qblk(MLA_HEADS * ROPE), seq(ctx_len, nkv), seq(ctx_len, ROPE),
                  seq(seq_len, nkv), seq(seq_len, ROPE)],
        out_specs=qblk(MLA_HEADS * VDIM),
        scratch_shapes=[pltpu.VMEM((MLA_HEADS, n_keys, QK_DIM), BF16),
                        pltpu.VMEM((MLA_HEADS, n_keys, 2 * VDIM), BF16),
                        pltpu.VMEM((seq_len, 2 * ROPE), F32), pltpu.VMEM((seq_len, 2 * ROPE), F32)],
        compiler_params=_cparams(("parallel", "arbitrary")),
        name="attn_lat",
    )(qnope, qpe, kv_ctx, kr_ctx, kv_lat, kr_lat)


def _dft_kernel(o_ref):
    tr, n = o_ref.shape[1], o_ref.shape[2]
    nb = n // V7X_LANES
    f = pl.program_id(0) * tr + lax.broadcasted_iota(jnp.int32, (tr, V7X_LANES), 0)
    j = lax.broadcasted_iota(jnp.int32, (tr, V7X_LANES), 1)

    def cos_sin(m):
        ang = (m & (2 * n - 1)).astype(F32) * (math.pi / n)
        return jnp.cos(ang), jnp.sin(ang)

    cj, sj = cos_sin(f * j)
    cb, sb = cos_sin(f * (j * V7X_LANES))
    for b in range(nb):
        cbb, sbb = cb[:, b:b + 1], sb[:, b:b + 1]
        cols = slice(b * V7X_LANES, (b + 1) * V7X_LANES)
        o_ref[0, :, cols] = (cbb * cj - sbb * sj).astype(o_ref.dtype)
        o_ref[1, :, cols] = (sbb * cj + cbb * sj).astype(o_ref.dtype)


def dft_tables(n):
    tr = 128
    return pl.pallas_call(
        _dft_kernel,
        out_shape=jax.ShapeDtypeStruct((2, n, n), BF16),
        grid=(n // tr,),
        out_specs=pl.BlockSpec((2, tr, n), lambda i: (0, i, 0)),
        compiler_params=_cparams(("parallel",)),
        name="dft_tables",
    )()


def _split_dot(table, x):
    hi = x.astype(BF16)
    lo = (x - hi.astype(F32)).astype(BF16)
    return (jnp.dot(table, hi, preferred_element_type=F32) + jnp.dot(table, lo, preferred_element_type=F32))


def _hy_filter_kernel(cs_ref, w1_ref, b1_ref, w2_ref, b2_ref, w3_ref, kr_ref, ks_ref, kny_ref):
    n = cs_ref.shape[1]
    row = lax.broadcasted_iota(jnp.int32, (n, V7X_LANES), 0).astype(F32)
    lane = lax.broadcasted_iota(jnp.int32, (n, V7X_LANES), 1)
    t = row * (1.0 / (n - 1))
    w = (2.0 * math.pi) * row / n
    band = jnp.where(lane <= HY_BANDS, lane - 1, lane - 1 - HY_BANDS).astype(F32)
    freq = 1e-4 + band * ((HY_BANDS - 1 - 1e-4) / (HY_BANDS - 1))
    arg = jnp.where(lane <= HY_BANDS, freq * w + 0.5 * math.pi, -(freq * w))
    z = jnp.where(lane == 0, t, jnp.where(lane <= 2 * HY_BANDS, jnp.sin(arg), 0.0))
    hid = jnp.sin(_dot3(z, w1_ref[...]) + b1_ref[...])
    hid = jnp.sin(_dot3(hid, w2_ref[...]) + b2_ref[...])
    hf = _dot3(hid, w3_ref[...])

    rowc = lax.broadcasted_iota(jnp.int32, (n, HY_W), 0)
    chan = lax.broadcasted_iota(jnp.int32, (n, HY_W), 1).astype(F32)
    max_decay = math.log(HY_TARGET) / HY_FAST_DECAY
    min_decay = math.log(HY_TARGET) / HY_SLOW_DECAY
    deltas = min_decay + chan * ((max_decay - min_decay) / (HY_W - 1))
    decay = jnp.exp(-(rowc.astype(F32) * (1.0 / (n - 1))) * jnp.abs(deltas))
    h_fwd = hf[:, :HY_W] * decay
    h_bwd = jnp.where(rowc == 0, 0.0, hf[:, HY_W:] * decay)
    norm = jnp.sum(jnp.abs(h_fwd) + jnp.abs(h_bwd), axis=0, keepdims=True)
    even = (h_fwd + h_bwd) / norm
    odd = (h_fwd - h_bwd) / norm
    cf = jnp.where(rowc == 0, 1.0, 2.0) * (1.0 / (2 * n))
    kr_ref[...] = cf * _split_dot(cs_ref[0], even)
    ks_ref[...] = cf * _split_dot(cs_ref[1], odd)
    sgn = jnp.where((rowc & 1) == 1, -1.0, 1.0)
    kny_ref[...] = jnp.sum(sgn * even, axis=0, keepdims=True) * (1.0 / (2 * n))


def hy_filter(cs, w1p, b1p, w2p, b2p, w3p):
    n = cs.shape[1]
    full = lambda a: pl.BlockSpec(a.shape, lambda: (0,) * a.ndim)
    args = (cs, w1p, b1p, w2p, b2p, w3p)
    return pl.pallas_call(
        _hy_filter_kernel,
        out_shape=(jax.ShapeDtypeStruct((n, HY_W), F32), jax.ShapeDtypeStruct((n, HY_W), F32),
                   jax.ShapeDtypeStruct((1, HY_W), F32)),
        in_specs=[full(a) for a in args],
        out_specs=(pl.BlockSpec((n, HY_W), lambda: (0, 0)), pl.BlockSpec((n, HY_W), lambda: (0, 0)),
                   pl.BlockSpec((1, HY_W), lambda: (0, 0))),
        compiler_params=pltpu.CompilerParams(vmem_limit_bytes=V7X_VMEM_LIMIT_BYTES),
        name="hy_filter",
    )(*args)


def _hyena_kernel(u0_ref, u1_ref, u2_ref, sw_ref, sb_ref, cs_ref, kr_ref, ks_ref, kny_ref, bias_ref, o_ref,
                  *, seq_len):
    n, cb = u0_ref.shape
    n_seq = n // seq_len
    t = lax.broadcasted_iota(jnp.int32, (n, cb), 0) & (seq_len - 1)

    def short_conv(u_ref, k):
        u = u_ref[...]
        w = sw_ref[:, k * cb:(k + 1) * cb]
        return (sb_ref[:, k * cb:(k + 1) * cb] + w[0:1] * _shift_rows(u, 1, t, seq_len) + w[1:2] * u
                + w[2:3] * _shift_rows(u, -1, t, seq_len))

    x0 = short_conv(u0_ref, 0)
    z = short_conv(u1_ref, 1) * short_conv(u2_ref, 2)
    wide = lambda a: jnp.concatenate([a[q * seq_len:(q + 1) * seq_len] for q in range(n_seq)], axis=1)
    rep = lambda a: jnp.concatenate([a] * n_seq, axis=1)
    zw = wide(z)
    zb = zw.astype(BF16)
    c, s = cs_ref[0], cs_ref[1]
    ur = jnp.dot(c, zb, preferred_element_type=F32)
    us = jnp.dot(s, zb, preferred_element_type=F32)
    sgn = jnp.where((lax.broadcasted_iota(jnp.int32, zw.shape, 0) & 1) == 1, -1.0, 1.0)
    uny = jnp.sum(sgn * zw, axis=0, keepdims=True)
    kr, ks = rep(kr_ref[...]), rep(ks_ref[...])
    yr = (ur * kr - us * ks).astype(BF16)
    ys = (ur * ks + us * kr).astype(BF16)
    yw = jnp.dot(c, yr, preferred_element_type=F32) + jnp.dot(s, ys, preferred_element_type=F32)
    yw = yw + sgn * (uny * rep(kny_ref[...]))
    y = jnp.concatenate([yw[:, q * cb:(q + 1) * cb] for q in range(n_seq)], axis=0)
    o_ref[...] = (x0 * (y + bias_ref[...] * z)).astype(o_ref.dtype)


def hyena(uh, seq_len, short_w, short_b, cs, kr, ks, kny, bias):
    tokens = uh.shape[0]
    cb = HY_CB
    nc = HY_W // cb
    rows = max(seq_len, HY_ROWS)
    assert seq_len & (seq_len - 1) == 0 and rows % seq_len == 0
    ublk = lambda k: pl.BlockSpec((rows, cb), lambda s, c: (s, k * nc + c))
    chan = lambda r: pl.BlockSpec((r, cb), lambda s, c: (0, c))
    return pl.pallas_call(
        functools.partial(_hyena_kernel, seq_len=seq_len),
        out_shape=jax.ShapeDtypeStruct((tokens, HY_W), BF16),
        grid=(tokens // rows, nc),
        in_specs=[ublk(0), ublk(1), ublk(2),
                  pl.BlockSpec((None, 3, 3 * cb), lambda s, c: (c, 0, 0)),
                  pl.BlockSpec((None, 1, 3 * cb), lambda s, c: (c, 0, 0)),
                  pl.BlockSpec((2, seq_len, seq_len), lambda s, c: (0, 0, 0)),
                  chan(seq_len), chan(seq_len), chan(1), chan(1)],
        out_specs=pl.BlockSpec((rows, cb), lambda s, c: (s, c)),
        compiler_params=_cparams(("parallel", "parallel")),
        name="hyena",
    )(uh, uh, uh, short_w, short_b, cs, kr, ks, kny, bias)


META_E1, META_E2, META_R1, META_R2, META_G1, META_G2 = range(6)


def _route_kernel(x_ref, g_ref, sh_ref, sc_ref, wr_ref, br_ref, h_ref, meta_ref, meta_t_ref, cnt_ref, run_sc):
    tm = x_ref.shape[0]
    lane = lax.broadcasted_iota(jnp.int32, (tm, V7X_LANES), 1)

    @pl.when(pl.program_id(0) == 0)
    def _():
        run_sc[...] = jnp.zeros_like(run_sc)

    h = _norm_mod(x_ref[...], g_ref[...], sh_ref[...], sc_ref[...])
    h_ref[...] = h
    logits = _dot3(h, wr_ref[...]) + br_ref[...]
    valid = lane < N_EXPERTS
    lg = jnp.where(valid, logits, -jnp.inf)
    ex = jnp.exp(lg - jnp.max(lg, axis=-1, keepdims=True))
    p = ex / jnp.sum(ex, axis=-1, keepdims=True)
    p1 = jnp.max(p, axis=-1, keepdims=True)
    i1 = jnp.min(jnp.where((p == p1) & valid, lane, V7X_LANES), axis=-1, keepdims=True)
    rest = jnp.where((lane == i1) | (~valid), -1.0, p)
    p2 = jnp.max(rest, axis=-1, keepdims=True)
    i2 = jnp.min(jnp.where(rest == p2, lane, V7X_LANES), axis=-1, keepdims=True)
    m1 = lane == i1
    m2 = lane == i2
    chosen = jnp.where(m1 | m2, 1.0, 0.0)
    r = lax.broadcasted_iota(jnp.int32, (tm, tm), 0)
    c = lax.broadcasted_iota(jnp.int32, (tm, tm), 1)
    tri = jnp.where(c < r, 1.0, 0.0).astype(BF16)
    before = jnp.dot(tri, chosen.astype(BF16), preferred_element_type=F32) + run_sc[0:1, :]
    rank1 = jnp.sum(jnp.where(m1, before, 0.0), axis=-1, keepdims=True)
    rank2 = jnp.sum(jnp.where(m2, before, 0.0), axis=-1, keepdims=True)
    inv = 1.0 / (p1 + p2)
    vals = (i1.astype(F32), i2.astype(F32), rank1, rank2, p1 * inv, p2 * inv)
    meta = jnp.zeros((tm, V7X_LANES), F32)
    for k, v in enumerate(vals):
        meta = jnp.where(lane == k, v, meta)
    meta_ref[...] = meta
    meta_t_ref[...] = meta.T[:V7X_SUBLANES]
    run_sc[...] = run_sc[...] + jnp.sum(chosen, axis=0, keepdims=True)
    cnt_ref[...] = run_sc[...]


def moe_route(x, g, modtab, cond, wr_pad, br_pad):
    tokens = x.shape[0]
    tm = TM_ROUTE
    zero = lambda i: 0
    const = lambda i: (0, 0)
    return pl.pallas_call(
        _route_kernel,
        out_shape=(jax.ShapeDtypeStruct((tokens, D), F32),
                   jax.ShapeDtypeStruct((tokens, V7X_LANES), F32),
                   jax.ShapeDtypeStruct((V7X_SUBLANES, tokens), F32),
                   jax.ShapeDtypeStruct((V7X_SUBLANES, V7X_LANES), F32)),
        grid=(tokens // tm,),
        in_specs=[pl.BlockSpec((tm, D), lambda i: (i, 0)),
                  pl.BlockSpec((1, D), const),
                  _mod_spec(3, cond, tm, D, zero),
                  _mod_spec(4, cond, tm, D, zero),
                  pl.BlockSpec((D, V7X_LANES), const),
                  pl.BlockSpec((1, V7X_LANES), const)],
        out_specs=(pl.BlockSpec((tm, D), lambda i: (i, 0)),
                   pl.BlockSpec((tm, V7X_LANES), lambda i: (i, 0)),
                   pl.BlockSpec((V7X_SUBLANES, tm), lambda i: (0, i)),
                   pl.BlockSpec((V7X_SUBLANES, V7X_LANES), const)),
        scratch_shapes=[pltpu.VMEM((V7X_SUBLANES, V7X_LANES), F32)],
        compiler_params=_cparams(("arbitrary",)),
        name="moe_route",
    )(x, g.reshape(1, D), modtab, modtab, wr_pad, br_pad)


def _row_copy(src_ref, src_row, dst_ref, dst_row, sem):
    return pltpu.make_async_copy(src_ref.at[pl.ds(src_row, 1)], dst_ref.at[pl.ds(dst_row, 1)], sem)


def _dispatch_kernel(pos_ref, h_ref, hs_in_ref, hs_ref, sem):
    del hs_in_ref
    tm = h_ref.shape[0]
    n_tok = pos_ref.shape[0] // 2
    base = pl.program_id(0) * tm

    def issue(r, carry):
        _row_copy(h_ref, r, hs_ref, pos_ref[base + r], sem).start(priority=0)
        _row_copy(h_ref, r, hs_ref, pos_ref[n_tok + base + r], sem).start(priority=1)
        return carry

    lax.fori_loop(0, tm, issue, 0, unroll=8)
    for _ in range(2):
        pltpu.make_async_copy(h_ref, hs_ref.at[pl.ds(0, tm)], sem).wait()


def moe_dispatch(pos, h, hs):
    tokens = h.shape[0]
    tm = TM_ROUTE
    return pl.pallas_call(
        _dispatch_kernel,
        out_shape=jax.ShapeDtypeStruct(hs.shape, hs.dtype),
        grid_spec=pltpu.PrefetchScalarGridSpec(
            num_scalar_prefetch=1,
            grid=(tokens // tm,),
            in_specs=[pl.BlockSpec((tm, D), lambda i, pos: (i, 0)),
                      pl.BlockSpec(memory_space=pl.ANY)],
            out_specs=pl.BlockSpec(memory_space=pl.ANY),
            scratch_shapes=[pltpu.SemaphoreType.DMA(())]),
        input_output_aliases={2: 0},
        compiler_params=_cparams(("arbitrary",)),
        name="moe_dispatch",
    )(pos, h, hs)


def _experts_kernel(te_ref, sg_ref, su_ref, sd_ref, nu_ref, hs_ref, wg_ref, wu_ref, wd_ref, y_ref,
                    wg_sc, wu_sc, wd_sc):
    del sg_ref, su_ref, sd_ref
    j = pl.program_id(0)
    e = te_ref[j]
    e_prev = te_ref[jnp.maximum(j - 1, 0)]

    @pl.when((j == 0) | (e != e_prev))
    def _():
        wg_sc[...] = wg_ref[...].astype(BF16)
        wu_sc[...] = wu_ref[...].astype(BF16)
        wd_sc[...] = wd_ref[...].astype(BF16)

    @pl.when(j < nu_ref[0])
    def _():
        h = hs_ref[...].astype(BF16)
        y = None
        for c0 in range(0, D_FF_EXPERT, MOE_CHUNK):
            c1 = min(c0 + MOE_CHUNK, D_FF_EXPERT)
            hg = jnp.dot(h, wg_sc[:, c0:c1], preferred_element_type=F32)
            hu = jnp.dot(h, wu_sc[:, c0:c1], preferred_element_type=F32)
            act = (_silu(hg) * hu).astype(BF16)
            yc = jnp.dot(act, wd_sc[c0:c1, :], preferred_element_type=F32)
            y = yc if y is None else y + yc
        y_ref[...] = y

    @pl.when(j >= nu_ref[0])
    def _():
        y_ref[...] = jnp.zeros_like(y_ref)


def moe_experts(tile_expert, stages, n_used, hs, e_gate, e_up, e_down):
    rows = hs.shape[0]
    tmr = TM_EXPERT
    wspec = lambda shape, k: pl.BlockSpec((None,) + shape, lambda j, *pf: (pf[1 + k][j], 0, 0))
    return pl.pallas_call(
        _experts_kernel,
        out_shape=jax.ShapeDtypeStruct((rows, D), F32),
        grid_spec=pltpu.PrefetchScalarGridSpec(
            num_scalar_prefetch=5,
            grid=(rows // tmr,),
            in_specs=[pl.BlockSpec((tmr, D), lambda j, *pf: (j, 0)),
                      wspec((D, D_FF_EXPERT), 0), wspec((D, D_FF_EXPERT), 1), wspec((D_FF_EXPERT, D), 2)],
            out_specs=pl.BlockSpec((tmr, D), lambda j, *pf: (j, 0)),
            scratch_shapes=[pltpu.VMEM((D, D_FF_EXPERT), BF16), pltpu.VMEM((D, D_FF_EXPERT), BF16),
                            pltpu.VMEM((D_FF_EXPERT, D), BF16)]),
        compiler_params=_cparams(("arbitrary",)),
        name="moe_experts",
    )(tile_expert, *stages, n_used, hs, e_gate, e_up, e_down)


def _combine_kernel(pos_ref, x_ref, meta_ref, gt_ref, fg_ref, y_ref, o_ref, b1_sc, b2_sc, sem):
    tm = x_ref.shape[0]
    n_tok = pos_ref.shape[0] // 2
    base = pl.program_id(0) * tm

    def issue(r, carry):
        _row_copy(y_ref, pos_ref[base + r], b1_sc, r, sem).start(priority=0)
        _row_copy(y_ref, pos_ref[n_tok + base + r], b2_sc, r, sem).start(priority=1)
        return carry

    lax.fori_loop(0, tm, issue, 0, unroll=8)
    pltpu.make_async_copy(y_ref.at[pl.ds(0, tm)], b1_sc, sem).wait()
    pltpu.make_async_copy(y_ref.at[pl.ds(0, tm)], b2_sc, sem).wait()

    meta = meta_ref[...]
    lane = lax.broadcasted_iota(jnp.int32, meta.shape, 1)
    g1 = jnp.sum(jnp.where(lane == META_G1, meta, 0.0), axis=-1, keepdims=True)
    g2 = jnp.sum(jnp.where(lane == META_G2, meta, 0.0), axis=-1, keepdims=True)
    x = x_ref[...] + gt_ref[...] * (g1 * b1_sc[...] + g2 * b2_sc[...])
    o_ref[...] = _rms(x, fg_ref[...])


def moe_combine(pos, x, meta, modtab, cond, final_g, y):
    tokens = x.shape[0]
    tm = TM_COMBINE
    return pl.pallas_call(
        _combine_kernel,
        out_shape=jax.ShapeDtypeStruct((tokens, D), F32),
        grid_spec=pltpu.PrefetchScalarGridSpec(
            num_scalar_prefetch=1,
            grid=(tokens // tm,),
            in_specs=[pl.BlockSpec((tm, D), lambda i, pos: (i, 0)),
                      pl.BlockSpec((tm, V7X_LANES), lambda i, pos: (i, 0)),
                      _mod_spec(5, cond, tm, D, lambda i, pos: 0),
                      pl.BlockSpec((1, D), lambda i, pos: (0, 0)),
                      pl.BlockSpec(memory_space=pl.ANY)],
            out_specs=pl.BlockSpec((tm, D), lambda i, pos: (i, 0)),
            scratch_shapes=[pltpu.VMEM((tm, D), F32), pltpu.VMEM((tm, D), F32), pltpu.SemaphoreType.DMA(())]),
        compiler_params=_cparams(("arbitrary",)),
        name="moe_combine",
    )(pos, x, meta, modtab, final_g.reshape(1, D), y)


def moe_plan(metas, counts):
    tmr = TM_EXPERT
    cnts = [c[0, :N_EXPERTS].astype(jnp.int32) for c in counts]
    total = functools.reduce(jnp.add, cnts)
    padded = ((total + tmr - 1) // tmr) * tmr
    ends = jnp.cumsum(padded)
    starts = ends - padded
    n_rows = sum(m.shape[1] for m in metas) * 2 + N_EXPERTS * tmr
    n_tiles = n_rows // tmr
    tile_start = jnp.arange(n_tiles, dtype=jnp.int32) * tmr
    tile_expert = jnp.minimum(jnp.sum(tile_start[:, None] >= ends[None, :], axis=1), N_EXPERTS - 1).astype(jnp.int32)
    n_used = (ends[-1] // tmr).astype(jnp.int32).reshape(1)
    eid = jnp.arange(N_EXPERTS, dtype=jnp.int32)
    later = jnp.where((eid[None, :] > eid[:, None]) & (padded[None, :] > 0), eid[None, :], N_EXPERTS)
    nxt = jnp.min(later, axis=1)
    next_used = jnp.where(nxt == N_EXPERTS, eid, nxt)
    pick = lambda table: jnp.sum(jnp.where(tile_expert[:, None] == eid[None, :], table[None, :], 0), axis=1)
    k_in_group = (tile_start - pick(starts)) // tmr
    tile_next = pick(next_used)
    stages = [jnp.where(k_in_group < k, tile_expert, tile_next).astype(jnp.int32) for k in (1, 2, 3)]
    pos = []
    base = jnp.zeros((N_EXPERTS,), jnp.int32)
    for m, c in zip(metas, cnts):
        first = starts + base
        sel = lambda field: m[field].astype(jnp.int32)
        lookup = lambda e: jnp.sum(jnp.where(e[:, None] == jnp.arange(N_EXPERTS)[None, :], first[None, :], 0), axis=1)
        p1 = lookup(sel(META_E1)) + sel(META_R1)
        p2 = lookup(sel(META_E2)) + sel(META_R2)
        pos.append(jnp.concatenate([p1, p2]).astype(jnp.int32))
        base = base + c
    return pos, tile_expert, stages, n_used, n_rows


def _pad_to(a, shape):
    return jnp.pad(a, [(0, t - s) for s, t in zip(a.shape, shape)])


def _regroup_chunks(a, cb):
    r = a.shape[0]
    return a.reshape(r, 3, HY_W // cb, cb).transpose(2, 0, 1, 3).reshape(HY_W // cb, r, 3 * cb)


def kernel(x_prompt, x_sample, state_l0_lru, cache_l1_ckv, cache_l1_krope, c, c_ctx, l0_norm1, l0_norm2, l0_w_mod, l0_b_mod, l0_w_in, l0_conv_a, l0_lru_conv_w, l0_lru_conv_b, l0_lru_wa, l0_lru_ba, l0_lru_wi, l0_lru_bi, l0_lru_lambda, l0_w_out, l0_ffn_gate, l0_ffn_up, l0_ffn_down, l1_norm1, l1_norm2, l1_w_mod, l1_b_mod, l1_w_in, l1_q_norm, l1_kv_norm, l1_w_q_up, l1_w_kv_up, l1_hy_short_w, l1_hy_short_b, l1_hy_f_w1, l1_hy_f_b1, l1_hy_f_w2, l1_hy_f_b2, l1_hy_f_w3, l1_hy_bias, l1_w_out, l1_router_w, l1_router_b, l1_exp_gate, l1_exp_up, l1_exp_down, final_norm):
    batch, seq, _ = x_prompt.shape
    dec_batch, dec_seq, _ = x_sample.shape
    past_len = cache_l1_ckv.shape[1]

    cond8 = jnp.concatenate([c_ctx[None, :], c, jnp.zeros((V7X_SUBLANES - 1 - dec_batch, D), F32)], axis=0)
    wcat = jnp.concatenate([l0_lru_wa[0], l0_lru_wi[0], l0_lru_wa[1], l0_lru_wi[1]], axis=-1)
    wq = l1_w_q_up.reshape(Q_RANK, MLA_HEADS, QK_DIM)
    wq_perm = jnp.concatenate([wq[:, :, :NOPE].reshape(Q_RANK, MLA_HEADS * NOPE),
                               wq[:, :, NOPE:].reshape(Q_RANK, MLA_HEADS * ROPE)], axis=1)
    hid = V7X_LANES
    w1p = _pad_to(l1_hy_f_w1, (hid, hid))
    b1p = _pad_to(l1_hy_f_b1.reshape(1, -1), (1, hid))
    w2p = _pad_to(l1_hy_f_w2, (hid, hid))
    b2p = _pad_to(l1_hy_f_b2.reshape(1, -1), (1, hid))
    w3p = _pad_to(l1_hy_f_w3, (hid, 2 * HY_W))
    short_w = _regroup_chunks(l1_hy_short_w, HY_CB)
    short_b = _regroup_chunks(l1_hy_short_b.reshape(1, -1), HY_CB)
    hy_bias = l1_hy_bias.reshape(1, HY_W)
    wr_pad = _pad_to(l1_router_w, (D, V7X_LANES))
    br_pad = _pad_to(l1_router_b.reshape(1, -1), (1, V7X_LANES))

    mod0 = adaln_table(cond8, l0_w_mod, l0_b_mod)
    mod1 = adaln_table(cond8, l1_w_mod, l1_b_mod)

    kv_ctx = kv_up(cache_l1_ckv.reshape(dec_batch * past_len, KV_RANK), l1_w_kv_up)
    kr_ctx = cache_l1_krope.reshape(dec_batch * past_len, ROPE)

    def trunk(x, seq_len, cond, h0, latent):
        u = in0_proj(x, l0_norm1, mod0, cond, l0_w_in)
        ya = conv_a(u, seq_len, l0_conv_a)
        yb, lru_state = rglru(u, seq_len, l0_lru_conv_w, l0_lru_conv_b, wcat, l0_lru_ba, l0_lru_bi,
                              l0_lru_lambda, h0)
        x = out_res([(ya, 0), (yb, 0), (yb, 1)], l0_w_out, x, mod0, cond, 2)
        x = ffn_res(x, l0_norm2, mod0, cond, l0_ffn_gate, l0_ffn_up, l0_ffn_down)
        qnope, qpe, ckv, kr, kv, uh = in1_proj(x, l1_norm1, mod1, cond, l1_w_in, l1_q_norm, l1_kv_norm,
                                               wq_perm, l1_w_kv_up)
        if latent:
            yc = attn_lat(qnope, qpe, kv_ctx, kr_ctx, kv, kr, seq_len, past_len)
        else:
            yc = attn_ctx(qnope, qpe, kv, kr, seq_len)
        cs = dft_tables(seq_len)
        k_r, k_s, k_ny = hy_filter(cs, w1p, b1p, w2p, b2p, w3p)
        yd = hyena(uh, seq_len, short_w, short_b, cs, k_r, k_s, k_ny, hy_bias)
        x = out_res([(yc, 0), (yc, 1), (yd, 0)], l1_w_out, x, mod1, cond, 2)
        return x, lru_state, ckv, kr

    conds = ((0, batch * seq), (1, dec_seq))
    zeros_state = jnp.zeros((batch, 2, LRU_W), F32)
    x_p, new_lru, new_ckv, new_kr = trunk(x_prompt.reshape(batch * seq, D), seq, conds[0], zeros_state, latent=False)
    x_s, _, _, _ = trunk(x_sample.reshape(dec_batch * dec_seq, D), dec_seq, conds[1], state_l0_lru, latent=True)

    xs = (x_p, x_s)
    routed = [moe_route(x, l1_norm2, mod1, cond, wr_pad, br_pad) for x, cond in zip(xs, conds)]
    pos, tile_expert, stages, n_used, n_rows = moe_plan([r[2] for r in routed], [r[3] for r in routed])
    hs = jnp.zeros((n_rows, D), F32)
    for p, r in zip(pos, routed):
        hs = moe_dispatch(p, r[0], hs)
    y_rows = moe_experts(tile_expert, stages, n_used, hs, l1_exp_gate, l1_exp_up, l1_exp_down)
    y_p, y_s = [moe_combine(p, x, r[1], mod1, cond, final_norm, y_rows)
                for p, x, r, cond in zip(pos, xs, routed, conds)]
    return (y_p.reshape(batch, seq, D), y_s.reshape(dec_batch, dec_seq, D), new_lru,
            new_ckv.reshape(batch, seq, KV_RANK), new_kr.reshape(batch, seq, ROPE))
```

```python
import functools
import math

import jax
import jax.numpy as jnp
from jax import lax
from jax.experimental import pallas as pl
from jax.experimental.pallas import tpu as pltpu

F32 = jnp.float32
BF16 = jnp.bfloat16
HIGHEST = lax.Precision.HIGHEST

D = 1024
GRID_W = 64
EPS = 1e-6
CONV_W = 512
LRU_W = 1024
LRU_BW = 128
LRU_C = 8.0
MLA_HEADS = 8
Q_RANK = 384
KV_RANK = 256
NOPE = 128
ROPE = 64
VDIM = 128
QK_DIM = NOPE + ROPE
ROPE_THETA = 10000.0
HY_W = 512
HY_BANDS = 16
HY_TARGET = 1e-2
HY_FAST_DECAY = 0.3
HY_SLOW_DECAY = 1.5
D_FF = 2816
N_EXPERTS = 8
D_FF_EXPERT = 1408
IN0 = 3 * CONV_W + 2 * LRU_W
IN1 = Q_RANK + KV_RANK + ROPE + 3 * HY_W

V7X_LANES = 128
V7X_SUBLANES = 8
V7X_VMEM_LIMIT_BYTES = 56 * 1024 * 1024

TM = 512
TN_IN0 = 512
TF_FFN = 256
MOE_CHUNK = 256
TM_ROUTE = 512
TM_EXPERT = 512
TM_COMBINE = 512
LRU_CB = 256
HY_CB = 256
TQ = 256
ATTN_CTX_SEQS = 4
CONV_A_ROWS = 1024
LRU_ROWS = 1024
HY_ROWS = 1024
TM_IN1 = 512


def _cparams(sem):
    return pltpu.CompilerParams(dimension_semantics=sem, vmem_limit_bytes=V7X_VMEM_LIMIT_BYTES)


def _sigmoid(x):
    return 0.5 * jnp.tanh(0.5 * x) + 0.5


def _silu(x):
    return x * _sigmoid(x)


def _norm_mod(x, g, shift, scale):
    ms = jnp.mean(x * x, axis=-1, keepdims=True)
    y = x * lax.rsqrt(ms + EPS) * g
    return y * (1.0 + scale) + shift


def _mod_spec(comp, cond, tm, width, col_fn, tile_fn=lambda *ids: ids[0]):
    row0, seg = cond
    assert seg % tm == 0
    return pl.BlockSpec((None, 1, width),
                        lambda *ids: (comp * 3 + row0 + (tile_fn(*ids) * tm) // seg, 0, col_fn(*ids)))


def _dot3(a, b):
    a_hi = a.astype(BF16)
    a_lo = (a - a_hi.astype(F32)).astype(BF16)
    b_hi = b.astype(BF16)
    b_lo = (b - b_hi.astype(F32)).astype(BF16)
    n = a.shape[0]
    y = jnp.dot(jnp.concatenate([a_hi, a_lo], axis=0), b_hi, preferred_element_type=F32)
    return y[:n] + y[n:] + jnp.dot(a_hi, b_lo, preferred_element_type=F32)


def _adaln_kernel(c_ref, w_ref, b_ref, o_ref):
    o_ref[...] = _dot3(_silu(c_ref[...]), w_ref[...]) + b_ref[...]


def adaln_table(cond8, w_mod, b_mod):
    tn = 1536
    m = pl.pallas_call(
        _adaln_kernel,
        out_shape=jax.ShapeDtypeStruct((V7X_SUBLANES, 6 * D), F32),
        grid=(6 * D // tn,),
        in_specs=[pl.BlockSpec((V7X_SUBLANES, D), lambda j: (0, 0)),
                  pl.BlockSpec((D, tn), lambda j: (0, j)),
                  pl.BlockSpec((1, tn), lambda j: (0, j))],
        out_specs=pl.BlockSpec((V7X_SUBLANES, tn), lambda j: (0, j)),
        compiler_params=_cparams(("arbitrary",)),
        name="adaln",
    )(cond8, w_mod, b_mod.reshape(1, 6 * D))
    return m[:3].reshape(3, 6, D).transpose(1, 0, 2).reshape(18, 1, D)


def _tile_of(n_load):
    return lambda s: jnp.maximum(s - n_load, 0)


def _block_of(n_load):
    return lambda s: jnp.minimum(s, n_load - 1)


def _in0_kernel(x_ref, g_ref, sh_ref, sc_ref, w_ref, o_ref, w_sc):
    s = pl.program_id(0)
    n_load, _, tn = w_sc.shape

    @pl.when(s < n_load)
    def _():
        w_sc[s] = w_ref[...].astype(BF16)

    @pl.when(s >= n_load)
    def _():
        h = _norm_mod(x_ref[...], g_ref[...], sh_ref[...], sc_ref[...]).astype(BF16)
        for j in range(n_load):
            o_ref[:, j * tn:(j + 1) * tn] = jnp.dot(h, w_sc[j], preferred_element_type=F32).astype(o_ref.dtype)


def in0_proj(x, g, modtab, cond, w_in):
    tn = TN_IN0
    tokens = x.shape[0]
    n = w_in.shape[1]
    n_load = n // tn
    tile = _tile_of(n_load)
    blk = _block_of(n_load)
    zero = lambda s: 0
    return pl.pallas_call(
        _in0_kernel,
        out_shape=jax.ShapeDtypeStruct((tokens, n), BF16),
        grid=(n_load + tokens // TM,),
        in_specs=[pl.BlockSpec((TM, D), lambda s: (tile(s), 0)),
                  pl.BlockSpec((1, D), lambda s: (0, 0)),
                  _mod_spec(0, cond, TM, D, zero, tile),
                  _mod_spec(1, cond, TM, D, zero, tile),
                  pl.BlockSpec((D, tn), lambda s: (0, blk(s)))],
        out_specs=pl.BlockSpec((TM, n), lambda s: (tile(s), 0)),
        scratch_shapes=[pltpu.VMEM((n_load, D, tn), BF16)],
        compiler_params=_cparams(("arbitrary",)),
        name="in0_proj",
    )(x, g.reshape(1, D), modtab, modtab, w_in)


def _shift_rows(v, d, t, seq_len=None):
    n = v.shape[0]
    seq_len = n if seq_len is None else seq_len
    if d > 0:
        return jnp.where(t < d, 0.0, pltpu.roll(v, d, 0))
    return jnp.where(t >= seq_len + d, 0.0, pltpu.roll(v, n + d, 0))


def _conv_a_kernel(b_ref, c_ref, x_ref, w_ref, o_ref, *, seq_len):
    v = c_ref[...].astype(F32) * x_ref[...].astype(F32)
    t = lax.broadcasted_iota(jnp.int32, v.shape, 0) & (seq_len - 1)
    w = w_ref[...]
    y = w[0:1] * _shift_rows(v, 1, t, seq_len) + w[1:2] * v + w[2:3] * _shift_rows(v, -1, t, seq_len)
    o_ref[...] = (b_ref[...].astype(F32) * y).astype(o_ref.dtype)


def conv_a(u, seq_len, conv_w):
    tokens = u.shape[0]
    rows = max(seq_len, CONV_A_ROWS)
    assert seq_len & (seq_len - 1) == 0 and rows % seq_len == 0
    return pl.pallas_call(
        functools.partial(_conv_a_kernel, seq_len=seq_len),
        out_shape=jax.ShapeDtypeStruct((tokens, CONV_W), BF16),
        grid=(tokens // rows,),
        in_specs=[pl.BlockSpec((rows, CONV_W), lambda s: (s, 0)),
                  pl.BlockSpec((rows, CONV_W), lambda s: (s, 1)),
                  pl.BlockSpec((rows, CONV_W), lambda s: (s, 2)),
                  pl.BlockSpec((3, CONV_W), lambda s: (0, 0))],
        out_specs=pl.BlockSpec((rows, CONV_W), lambda s: (s, 0)),
        compiler_params=_cparams(("parallel",)),
        name="conv_a",
    )(u, u, u, conv_w)


def _group_scan(a, b, reverse):
    n, c = a.shape
    a3 = a.reshape(n // V7X_SUBLANES, V7X_SUBLANES, c)
    b3 = b.reshape(n // V7X_SUBLANES, V7X_SUBLANES, c)
    t8 = lax.broadcasted_iota(jnp.int32, a3.shape, 1)
    for d in (1, 2, 4):
        if reverse:
            keep = t8 < V7X_SUBLANES - d
            shift = V7X_SUBLANES - d
        else:
            keep = t8 >= d
            shift = d
        a_sh = jnp.where(keep, pltpu.roll(a3, shift, 1), 1.0)
        b_sh = jnp.where(keep, pltpu.roll(b3, shift, 1), 0.0)
        b3 = a3 * b_sh + b3
        a3 = a3 * a_sh
    return a3.reshape(n, c), b3.reshape(n, c)


def _rglru_kernel(gate_ref, xb_ref, cw_ref, cb_ref, wcat_ref, ba_ref, bi_ref, lam_ref, h0_ref,
                  y_ref, st_ref, af_sc, bf_sc, ab_sc, bb_sc, hf_sc, hb_sc, *, seq_len):
    n, cb = xb_ref.shape
    n_seq = n // seq_len
    xb = xb_ref[...].astype(F32)
    t = lax.broadcasted_iota(jnp.int32, xb.shape, 0) & (seq_len - 1)
    cw = cw_ref[...]
    sh = lambda d: _shift_rows(xb, d, t, seq_len)
    xc = cb_ref[...] + cw[0:1] * sh(2) + cw[1:2] * sh(1) + cw[2:3] * xb + cw[3:4] * sh(-1)
    xcb = xc.astype(BF16)
    g = [jnp.dot(xcb[:, k * LRU_BW:(k + 1) * LRU_BW], wcat_ref[k].astype(BF16), preferred_element_type=F32)
         for k in range(cb // LRU_BW)]

    def direction(d):
        ga = jnp.concatenate([gk[:, (2 * d) * LRU_BW:(2 * d + 1) * LRU_BW] for gk in g], axis=1)
        gi = jnp.concatenate([gk[:, (2 * d + 1) * LRU_BW:(2 * d + 2) * LRU_BW] for gk in g], axis=1)
        r = _sigmoid(ga + ba_ref[d:d + 1, :])
        i = _sigmoid(gi + bi_ref[d:d + 1, :])
        log_a = (-LRU_C * jax.nn.softplus(-lam_ref[d:d + 1, :])) * r
        a = jnp.exp(log_a)
        m = 1.0 - a * a
        mult = m * lax.rsqrt(jnp.maximum(m, 1e-30))
        return a, mult * (i * xc)

    a_f, b_f = direction(0)
    a_f, b_f = _group_scan(a_f, b_f, reverse=False)
    af_sc[...] = a_f
    bf_sc[...] = b_f
    a_b, b_b = direction(1)
    a_b, b_b = _group_scan(a_b, b_b, reverse=True)
    ab_sc[...] = a_b
    bb_sc[...] = b_b

    ng = seq_len // V7X_SUBLANES
    bcast = lambda row: jnp.broadcast_to(row, (V7X_SUBLANES, cb))
    init = tuple((bcast(h0_ref[q, 0:1, :]), bcast(h0_ref[q, 1:2, :])) for q in range(n_seq))

    def step(k, carry):
        out = []
        for q, (hf_in, hb_in) in enumerate(carry):
            rf = pl.multiple_of(q * seq_len + k * V7X_SUBLANES, V7X_SUBLANES)
            rb = pl.multiple_of(q * seq_len + (ng - 1 - k) * V7X_SUBLANES, V7X_SUBLANES)
            hf = af_sc[pl.ds(rf, V7X_SUBLANES), :] * hf_in + bf_sc[pl.ds(rf, V7X_SUBLANES), :]
            hb = ab_sc[pl.ds(rb, V7X_SUBLANES), :] * hb_in + bb_sc[pl.ds(rb, V7X_SUBLANES), :]
            hf_sc[pl.ds(rf, V7X_SUBLANES), :] = hf
            hb_sc[pl.ds(rb, V7X_SUBLANES), :] = hb
            out.append((bcast(hf[V7X_SUBLANES - 1:V7X_SUBLANES]), bcast(hb[0:1])))
        return tuple(out)

    final = lax.fori_loop(0, ng, step, init)
    for q, (hf_last, hb_first) in enumerate(final):
        st_ref[q, 0:1, :] = hf_last[0:1]
        st_ref[q, 1:2, :] = hb_first[0:1]

    gt = gate_ref[...].astype(F32)
    gelu = 0.5 * gt * (1.0 + jnp.tanh(math.sqrt(2.0 / math.pi) * (gt + 0.044715 * (gt * gt * gt))))
    y_ref[...] = ((hf_sc[...] + hb_sc[...]) * gelu).astype(y_ref.dtype)


def rglru(u, seq_len, conv_w, conv_b, wcat, ba, bi, lam, h0):
    tokens = u.shape[0]
    nseq = tokens // seq_len
    cb = LRU_CB
    rows = max(seq_len, LRU_ROWS)
    assert seq_len & (seq_len - 1) == 0 and rows % seq_len == 0
    per_blk = rows // seq_len
    gate_blk0 = 3 * CONV_W // cb
    xb_blk0 = (3 * CONV_W + LRU_W) // cb
    seq_scr = lambda: pltpu.VMEM((rows, cb), F32)
    return pl.pallas_call(
        functools.partial(_rglru_kernel, seq_len=seq_len),
        out_shape=(jax.ShapeDtypeStruct((tokens, LRU_W), BF16), jax.ShapeDtypeStruct((nseq, 2, LRU_W), F32)),
        grid=(tokens // rows, LRU_W // cb),
        in_specs=[pl.BlockSpec((rows, cb), lambda s, c: (s, gate_blk0 + c)),
                  pl.BlockSpec((rows, cb), lambda s, c: (s, xb_blk0 + c)),
                  pl.BlockSpec((4, cb), lambda s, c: (0, c)),
                  pl.BlockSpec((1, cb), lambda s, c: (0, c)),
                  pl.BlockSpec((cb // LRU_BW, LRU_BW, 4 * LRU_BW), lambda s, c: (c, 0, 0)),
                  pl.BlockSpec((2, cb), lambda s, c: (0, c)),
                  pl.BlockSpec((2, cb), lambda s, c: (0, c)),
                  pl.BlockSpec((2, cb), lambda s, c: (0, c)),
                  pl.BlockSpec((per_blk, 2, cb), lambda s, c: (s, 0, c))],
        out_specs=(pl.BlockSpec((rows, cb), lambda s, c: (s, c)),
                   pl.BlockSpec((per_blk, 2, cb), lambda s, c: (s, 0, c))),
        scratch_shapes=[seq_scr() for _ in range(6)],
        compiler_params=_cparams(("parallel", "parallel")),
        name="rglru",
    )(u, u, conv_w, conv_b.reshape(1, LRU_W), wcat, ba, bi, lam, h0)


def _out_res_kernel(p0_ref, p1_ref, p2_ref, w_ref, x_ref, gt_ref, o_ref, w_sc):
    s = pl.program_id(0)
    n_load, kb, _ = w_sc.shape

    @pl.when(s < n_load)
    def _():
        w_sc[s] = w_ref[...].astype(BF16)

    @pl.when(s >= n_load)
    def _():
        y = jnp.dot(p0_ref[...], w_sc[0], preferred_element_type=F32)
        y += jnp.dot(p1_ref[...], w_sc[1], preferred_element_type=F32)
        y += jnp.dot(p2_ref[...], w_sc[2], preferred_element_type=F32)
        o_ref[...] = x_ref[...] + gt_ref[...] * y


def out_res(parts, w_out, x, modtab, cond, gate_comp):
    tokens = x.shape[0]
    kb = 512
    n_load = len(parts)
    tile = _tile_of(n_load)
    blk = _block_of(n_load)
    lhs_specs = [pl.BlockSpec((TM, kb), (lambda s, cbk=cbk: (tile(s), cbk))) for _, cbk in parts]
    return pl.pallas_call(
        _out_res_kernel,
        out_shape=jax.ShapeDtypeStruct((tokens, D), F32),
        grid=(n_load + tokens // TM,),
        in_specs=lhs_specs + [pl.BlockSpec((kb, D), lambda s: (blk(s), 0)),
                              pl.BlockSpec((TM, D), lambda s: (tile(s), 0)),
                              _mod_spec(gate_comp, cond, TM, D, lambda s: 0, tile)],
        out_specs=pl.BlockSpec((TM, D), lambda s: (tile(s), 0)),
        scratch_shapes=[pltpu.VMEM((n_load, kb, D), BF16)],
        compiler_params=_cparams(("arbitrary",)),
        name="out_res",
    )(*[a for a, _ in parts], w_out, x, modtab)


def _ffn_kernel(x_ref, g_ref, sh_ref, sc_ref, gt_ref, wg_ref, wu_ref, wd_ref, o_ref, wg_sc, wu_sc, wd_sc):
    s = pl.program_id(0)
    n_load = wg_sc.shape[0]

    @pl.when(s < n_load)
    def _():
        wg_sc[s] = wg_ref[...].astype(BF16)
        wu_sc[s] = wu_ref[...].astype(BF16)
        wd_sc[s] = wd_ref[...].astype(BF16)

    @pl.when(s >= n_load)
    def _():
        x = x_ref[...]
        h = _norm_mod(x, g_ref[...], sh_ref[...], sc_ref[...]).astype(BF16)
        y = None
        for f in range(n_load):
            hg = jnp.dot(h, wg_sc[f], preferred_element_type=F32)
            hu = jnp.dot(h, wu_sc[f], preferred_element_type=F32)
            act = (_silu(hg) * hu).astype(BF16)
            yf = jnp.dot(act, wd_sc[f], preferred_element_type=F32)
            y = yf if y is None else y + yf
        o_ref[...] = x + gt_ref[...] * y


def ffn_res(x, g, modtab, cond, w_gate, w_up, w_down):
    tokens = x.shape[0]
    tf = TF_FFN
    n_load = D_FF // tf
    tile = _tile_of(n_load)
    blk = _block_of(n_load)
    zero = lambda s: 0
    return pl.pallas_call(
        _ffn_kernel,
        out_shape=jax.ShapeDtypeStruct((tokens, D), F32),
        grid=(n_load + tokens // TM,),
        in_specs=[pl.BlockSpec((TM, D), lambda s: (tile(s), 0)),
                  pl.BlockSpec((1, D), lambda s: (0, 0)),
                  _mod_spec(3, cond, TM, D, zero, tile),
                  _mod_spec(4, cond, TM, D, zero, tile),
                  _mod_spec(5, cond, TM, D, zero, tile),
                  pl.BlockSpec((D, tf), lambda s: (0, blk(s))),
                  pl.BlockSpec((D, tf), lambda s: (0, blk(s))),
                  pl.BlockSpec((tf, D), lambda s: (blk(s), 0))],
        out_specs=pl.BlockSpec((TM, D), lambda s: (tile(s), 0)),
        scratch_shapes=[pltpu.VMEM((n_load, D, tf), BF16), pltpu.VMEM((n_load, D, tf), BF16),
                        pltpu.VMEM((n_load, tf, D), BF16)],
        compiler_params=_cparams(("arbitrary",)),
        name="ffn_res",
    )(x, g.reshape(1, D), modtab, modtab, modtab, w_gate, w_up, w_down)


def _rms(x, g):
    return x * lax.rsqrt(jnp.mean(x * x, axis=-1, keepdims=True) + EPS) * g


def _in1_kernel(x_ref, g_ref, sh_ref, sc_ref, w_ref, qn_ref, kvn_ref, wq_ref, wkv_ref,
                qnope_ref, qpe_ref, ckv_ref, kr_ref, kv_ref, uh_ref, w_sc, wq_sc, wkv_sc):
    @pl.when(pl.program_id(0) == 0)
    def _():
        w_sc[...] = w_ref[...].astype(BF16)
        for h in range(MLA_HEADS):
            c0 = h * QK_DIM
            wq_sc[:, h * NOPE:(h + 1) * NOPE] = wq_ref[:, c0:c0 + NOPE].astype(BF16)
            r0 = MLA_HEADS * NOPE + h * ROPE
            wq_sc[:, r0:r0 + ROPE] = wq_ref[:, c0 + NOPE:c0 + QK_DIM].astype(BF16)
        wkv_sc[...] = wkv_ref[...].astype(BF16)

    h = _norm_mod(x_ref[...], g_ref[...], sh_ref[...], sc_ref[...]).astype(BF16)
    u = jnp.dot(h, w_sc[...], preferred_element_type=F32)
    o1, o2, o3 = Q_RANK, Q_RANK + KV_RANK, Q_RANK + KV_RANK + ROPE
    cq = _rms(u[:, :o1], qn_ref[...])
    q = jnp.dot(cq.astype(BF16), wq_sc[...], preferred_element_type=F32) * _SCALE
    qnope_ref[...] = q[:, :MLA_HEADS * NOPE].astype(qnope_ref.dtype)
    qpe_ref[...] = q[:, MLA_HEADS * NOPE:]
    ckv = _rms(u[:, o1:o2], kvn_ref[...])
    ckv_ref[...] = ckv
    kv_ref[...] = jnp.dot(ckv.astype(BF16), wkv_sc[...], preferred_element_type=F32).astype(kv_ref.dtype)
    kr_ref[...] = u[:, o2:o3]
    uh_ref[...] = u[:, o3:]


def in1_proj(x, g, modtab, cond, w_in, q_norm, kv_norm, w_q_up, w_kv_up):
    tokens = x.shape[0]
    tm = TM_IN1
    nkv = MLA_HEADS * (NOPE + VDIM)
    const = lambda i: (0, 0)
    zero = lambda i: 0
    once = pl.Buffered(1)
    outs = (jax.ShapeDtypeStruct((tokens, MLA_HEADS * NOPE), BF16),
            jax.ShapeDtypeStruct((tokens, MLA_HEADS * ROPE), F32),
            jax.ShapeDtypeStruct((tokens, KV_RANK), F32),
            jax.ShapeDtypeStruct((tokens, ROPE), F32),
            jax.ShapeDtypeStruct((tokens, nkv), BF16),
            jax.ShapeDtypeStruct((tokens, 3 * HY_W), F32))
    row = lambda w: pl.BlockSpec((tm, w), lambda i: (i, 0))
    return pl.pallas_call(
        _in1_kernel,
        out_shape=outs,
        grid=(tokens // tm,),
        in_specs=[row(D),
                  pl.BlockSpec((1, D), const),
                  _mod_spec(0, cond, tm, D, zero),
                  _mod_spec(1, cond, tm, D, zero),
                  pl.BlockSpec((D, IN1), const, pipeline_mode=once),
                  pl.BlockSpec((1, Q_RANK), const),
                  pl.BlockSpec((1, KV_RANK), const),
                  pl.BlockSpec((Q_RANK, MLA_HEADS * QK_DIM), const, pipeline_mode=once),
                  pl.BlockSpec((KV_RANK, nkv), const, pipeline_mode=once)],
        out_specs=tuple(row(o.shape[1]) for o in outs),
        scratch_shapes=[pltpu.VMEM((D, IN1), BF16), pltpu.VMEM((Q_RANK, MLA_HEADS * QK_DIM), BF16),
                        pltpu.VMEM((KV_RANK, nkv), BF16)],
        compiler_params=_cparams(("arbitrary",)),
        name="in1_proj",
    )(x, g.reshape(1, D), modtab, modtab, w_in, q_norm.reshape(1, Q_RANK), kv_norm.reshape(1, KV_RANK),
      w_q_up, w_kv_up)


def _mm_kernel(a_ref, w_ref, o_ref):
    o_ref[...] = jnp.dot(a_ref[...].astype(BF16), w_ref[...].astype(BF16),
                         preferred_element_type=F32).astype(o_ref.dtype)


def kv_up(ckv, w_kv_up):
    rows = ckv.shape[0]
    n = w_kv_up.shape[1]
    return pl.pallas_call(
        _mm_kernel,
        out_shape=jax.ShapeDtypeStruct((rows, n), BF16),
        grid=(rows // TM,),
        in_specs=[pl.BlockSpec((TM, KV_RANK), lambda i: (i, 0)), pl.BlockSpec((KV_RANK, n), lambda i: (0, 0))],
        out_specs=pl.BlockSpec((TM, n), lambda i: (i, 0)),
        compiler_params=_cparams(("parallel",)),
        name="kv_up",
    )(ckv, w_kv_up)


_NT = (((1,), (1,)), ((), ()))
_SCALE = 1.0 / math.sqrt(QK_DIM)


def _fill_rope_tables(cos_ref, sin_ref):
    n, width = cos_ref.shape
    n_grid_rows = n // GRID_W
    n_freq = ROPE // 4

    def trig(count):
        lane = lax.broadcasted_iota(jnp.int32, (count, width), 1)
        j = lane & (ROPE // 2 - 1)
        inv = jnp.exp((j & (n_freq - 1)).astype(F32) * (-math.log(ROPE_THETA) / n_freq))
        ang = lax.broadcasted_iota(jnp.int32, (count, width), 0).astype(F32) * inv
        return jnp.cos(ang), jnp.sin(ang), j < n_freq

    cos_c, sin_c, by_row = trig(GRID_W)
    cos_r, sin_r, _ = trig(n_grid_rows)
    for r in range(n_grid_rows):
        rows = slice(r * GRID_W, (r + 1) * GRID_W)
        cos_ref[rows, :] = jnp.where(by_row, jnp.broadcast_to(cos_r[r:r + 1], cos_c.shape), cos_c)
        sin_ref[rows, :] = jnp.where(by_row, jnp.broadcast_to(sin_r[r:r + 1], sin_c.shape), sin_c)


def _rope(x, cos, sin):
    width = x.shape[1]
    lane = lax.broadcasted_iota(jnp.int32, x.shape, 1)
    first_half = (lane & (ROPE - 1)) < ROPE // 2
    xr = jnp.where(first_half, -pltpu.roll(x, width - ROPE // 2, 1), pltpu.roll(x, ROPE // 2, 1))
    return x * cos + xr * sin


def _ones_column(n):
    lane = lax.broadcasted_iota(jnp.int32, (n, VDIM), 1)
    return jnp.where(lane == 0, 1.0, 0.0).astype(BF16)


def _head_attention(qcat, kcat, vaug):
    s = lax.dot_general(qcat, kcat, _NT, preferred_element_type=F32)
    p = jnp.exp(s - jnp.max(s, axis=-1, keepdims=True)).astype(BF16)
    oa = jnp.dot(p, vaug, preferred_element_type=F32)
    return oa[:, :VDIM] / oa[:, VDIM:VDIM + 1]


def _attn_ctx_kernel(qn_ref, qpe_ref, kv_ref, kr_ref, o_ref, *, seq_len):
    n = qn_ref.shape[0]
    n_seq = n // seq_len
    ones = _ones_column(n)
    kpe = kr_ref[...].astype(BF16)
    per_seq = lambda a: a.reshape(n_seq, seq_len, a.shape[-1])
    for h in range(MLA_HEADS):
        c0 = h * (NOPE + VDIM)
        qcat = per_seq(jnp.concatenate([qn_ref[:, h * NOPE:(h + 1) * NOPE],
                                        qpe_ref[:, h * ROPE:(h + 1) * ROPE].astype(BF16)], axis=1))
        kcat = per_seq(jnp.concatenate([kv_ref[:, c0:c0 + NOPE], kpe], axis=1))
        vaug = per_seq(jnp.concatenate([kv_ref[:, c0 + NOPE:c0 + NOPE + VDIM], ones], axis=1))
        s = jnp.einsum("bqd,bkd->bqk", qcat, kcat, preferred_element_type=F32)
        p = jnp.exp(s - jnp.max(s, axis=-1, keepdims=True)).astype(BF16)
        oa = jnp.einsum("bqk,bkd->bqd", p, vaug, preferred_element_type=F32)
        o = oa[:, :, :VDIM] / oa[:, :, VDIM:VDIM + 1]
        o_ref[:, h * VDIM:(h + 1) * VDIM] = o.reshape(n, VDIM).astype(o_ref.dtype)


def attn_ctx(qnope, qpe, kv, kr, seq_len):
    tokens = qnope.shape[0]
    rows = ATTN_CTX_SEQS * seq_len
    blk = lambda w: pl.BlockSpec((rows, w), lambda s: (s, 0))
    return pl.pallas_call(
        functools.partial(_attn_ctx_kernel, seq_len=seq_len),
        out_shape=jax.ShapeDtypeStruct((tokens, MLA_HEADS * VDIM), BF16),
        grid=(tokens // rows,),
        in_specs=[blk(MLA_HEADS * NOPE), blk(MLA_HEADS * ROPE), blk(MLA_HEADS * (NOPE + VDIM)), blk(ROPE)],
        out_specs=blk(MLA_HEADS * VDIM),
        compiler_params=_cparams(("parallel",)),
        name="attn_ctx",
    )(qnope, qpe, kv, kr)


def _attn_lat_kernel(qn_ref, qpe_ref, kvc_ref, krc_ref, kvl_ref, krl_ref, o_ref, kcat_sc, vaug_sc, cos_sc, sin_sc):
    tq = qn_ref.shape[0]
    n_ctx = krc_ref.shape[0]
    n_lat = krl_ref.shape[0]

    @pl.when(pl.program_id(1) == 0)
    def _():
        _fill_rope_tables(cos_sc, sin_sc)
        kr2 = jnp.concatenate([krl_ref[...], krl_ref[...]], axis=1)
        kpe_lat = _rope(kr2, cos_sc[...], sin_sc[...])[:, :ROPE].astype(BF16)
        kpe_ctx = krc_ref[...].astype(BF16)
        ones_c, ones_l = _ones_column(n_ctx), _ones_column(n_lat)
        for h in range(MLA_HEADS):
            c0 = h * (NOPE + VDIM)
            for r0, nr, kv_ref, kpe, ones in ((0, n_ctx, kvc_ref, kpe_ctx, ones_c), (n_ctx, n_lat, kvl_ref, kpe_lat, ones_l)):
                kcat_sc[h, r0:r0 + nr, 0:NOPE] = kv_ref[:, c0:c0 + NOPE]
                kcat_sc[h, r0:r0 + nr, NOPE:QK_DIM] = kpe
                vaug_sc[h, r0:r0 + nr, 0:VDIM] = kv_ref[:, c0 + NOPE:c0 + NOPE + VDIM]
                vaug_sc[h, r0:r0 + nr, VDIM:2 * VDIM] = ones

    q0 = pl.multiple_of(pl.program_id(1) * tq, tq)
    rep = lambda a: jnp.concatenate([a] * (MLA_HEADS // 2), axis=1)
    qp_all = _rope(qpe_ref[...], rep(cos_sc[pl.ds(q0, tq), :]), rep(sin_sc[pl.ds(q0, tq), :])).astype(BF16)
    for h in range(MLA_HEADS):
        qcat = jnp.concatenate([qn_ref[:, h * NOPE:(h + 1) * NOPE], qp_all[:, h * ROPE:(h + 1) * ROPE]], axis=1)
        o_ref[:, h * VDIM:(h + 1) * VDIM] = _head_attention(qcat, kcat_sc[h], vaug_sc[h]).astype(o_ref.dtype)


def attn_lat(qnope, qpe, kv_ctx, kr_ctx, kv_lat, kr_lat, seq_len, ctx_len):
    tokens = qnope.shape[0]
    nq = seq_len // TQ
    qblk = lambda w: pl.BlockSpec((TQ, w), lambda b, i: (b * nq + i, 0))
    seq = lambda n, w: pl.BlockSpec((n, w), lambda b, i: (b, 0))
    nkv = MLA_HEADS * (NOPE + VDIM)
    n_keys = ctx_len + seq_len
    return pl.pallas_call(
        _attn_lat_kernel,
        out_shape=jax.ShapeDtypeStruct((tokens, MLA_HEADS * VDIM), BF16),
        grid=(tokens // seq_len, nq),
        in_specs=[qblk(MLA_HEADS * NOPE), qblk(MLA_HEADS * ROPE), seq(ctx_len, nkv), seq(ctx_len, ROPE),
                  seq(seq_len, nkv), seq(seq_len, ROPE)],
        out_specs=qblk(MLA_HEADS * VDIM),
        scratch_shapes=[pltpu.VMEM((MLA_HEADS, n_keys, QK_DIM), BF16),
                        pltpu.VMEM((MLA_HEADS, n_keys, 2 * VDIM), BF16),
                        pltpu.VMEM((seq_len, 2 * ROPE), F32), pltpu.VMEM((seq_len, 2 * ROPE), F32)],
        compiler_params=_cparams(("parallel", "arbitrary")),
        name="attn_lat",
    )(qnope, qpe, kv_ctx, kr_ctx, kv_lat, kr_lat)


def _dft_kernel(o_ref):
    tr, n = o_ref.shape[1], o_ref.shape[2]
    nb = n // V7X_LANES
    f = pl.program_id(0) * tr + lax.broadcasted_iota(jnp.int32, (tr, V7X_LANES), 0)
    j = lax.broadcasted_iota(jnp.int32, (tr, V7X_LANES), 1)

    def cos_sin(m):
        ang = (m & (2 * n - 1)).astype(F32) * (math.pi / n)
        return jnp.cos(ang), jnp.sin(ang)

    cj, sj = cos_sin(f * j)
    cb, sb = cos_sin(f * (j * V7X_LANES))
    for b in range(nb):
        cbb, sbb = cb[:, b:b + 1], sb[:, b:b + 1]
        cols = slice(b * V7X_LANES, (b + 1) * V7X_LANES)
        o_ref[0, :, cols] = (cbb * cj - sbb * sj).astype(o_ref.dtype)
        o_ref[1, :, cols] = (sbb * cj + cbb * sj).astype(o_ref.dtype)


def dft_tables(n):
    tr = 128
    return pl.pallas_call(
        _dft_kernel,
        out_shape=jax.ShapeDtypeStruct((2, n, n), BF16),
        grid=(n // tr,),
        out_specs=pl.BlockSpec((2, tr, n), lambda i: (0, i, 0)),
        compiler_params=_cparams(("parallel",)),
        name="dft_tables",
    )()


def _split_dot(table, x):
    hi = x.astype(BF16)
    lo = (x - hi.astype(F32)).astype(BF16)
    return (jnp.dot(table, hi, preferred_element_type=F32) + jnp.dot(table, lo, preferred_element_type=F32))


def _hy_filter_kernel(cs_ref, w1_ref, b1_ref, w2_ref, b2_ref, w3_ref, kr_ref, ks_ref, kny_ref):
    n = cs_ref.shape[1]
    row = lax.broadcasted_iota(jnp.int32, (n, V7X_LANES), 0).astype(F32)
    lane = lax.broadcasted_iota(jnp.int32, (n, V7X_LANES), 1)
    t = row * (1.0 / (n - 1))
    w = (2.0 * math.pi) * row / n
    band = jnp.where(lane <= HY_BANDS, lane - 1, lane - 1 - HY_BANDS).astype(F32)
    freq = 1e-4 + band * ((HY_BANDS - 1 - 1e-4) / (HY_BANDS - 1))
    arg = jnp.where(lane <= HY_BANDS, freq * w + 0.5 * math.pi, -(freq * w))
    z = jnp.where(lane == 0, t, jnp.where(lane <= 2 * HY_BANDS, jnp.sin(arg), 0.0))
    hid = jnp.sin(_dot3(z, w1_ref[...]) + b1_ref[...])
    hid = jnp.sin(_dot3(hid, w2_ref[...]) + b2_ref[...])
    hf = _dot3(hid, w3_ref[...])

    rowc = lax.broadcasted_iota(jnp.int32, (n, HY_W), 0)
    chan = lax.broadcasted_iota(jnp.int32, (n, HY_W), 1).astype(F32)
    max_decay = math.log(HY_TARGET) / HY_FAST_DECAY
    min_decay = math.log(HY_TARGET) / HY_SLOW_DECAY
    deltas = min_decay + chan * ((max_decay - min_decay) / (HY_W - 1))
    decay = jnp.exp(-(rowc.astype(F32) * (1.0 / (n - 1))) * jnp.abs(deltas))
    h_fwd = hf[:, :HY_W] * decay
    h_bwd = jnp.where(rowc == 0, 0.0, hf[:, HY_W:] * decay)
    norm = jnp.sum(jnp.abs(h_fwd) + jnp.abs(h_bwd), axis=0, keepdims=True)
    even = (h_fwd + h_bwd) / norm
    odd = (h_fwd - h_bwd) / norm
    cf = jnp.where(rowc == 0, 1.0, 2.0) * (1.0 / (2 * n))
    kr_ref[...] = cf * _split_dot(cs_ref[0], even)
    ks_ref[...] = cf * _split_dot(cs_ref[1], odd)
    sgn = jnp.where((rowc & 1) == 1, -1.0, 1.0)
    kny_ref[...] = jnp.sum(sgn * even, axis=0, keepdims=True) * (1.0 / (2 * n))


def hy_filter(cs, w1p, b1p, w2p, b2p, w3p):
    n = cs.shape[1]
    full = lambda a: pl.BlockSpec(a.shape, lambda: (0,) * a.ndim)
    args = (cs, w1p, b1p, w2p, b2p, w3p)
    return pl.pallas_call(
        _hy_filter_kernel,
        out_shape=(jax.ShapeDtypeStruct((n, HY_W), F32), jax.ShapeDtypeStruct((n, HY_W), F32),
                   jax.ShapeDtypeStruct((1, HY_W), F32)),
        in_specs=[full(a) for a in args],
        out_specs=(pl.BlockSpec((n, HY_W), lambda: (0, 0)), pl.BlockSpec((n, HY_W), lambda: (0, 0)),
                   pl.BlockSpec((1, HY_W), lambda: (0, 0))),
        compiler_params=pltpu.CompilerParams(vmem_limit_bytes=V7X_VMEM_LIMIT_BYTES),
        name="hy_filter",
    )(*args)


def _hyena_kernel(u0_ref, u1_ref, u2_ref, sw_ref, sb_ref, cs_ref, kr_ref, ks_ref, kny_ref, bias_ref, o_ref,
                  *, seq_len):
    n, cb = u0_ref.shape
    n_seq = n // seq_len
    t = lax.broadcasted_iota(jnp.int32, (n, cb), 0) & (seq_len - 1)

    def short_conv(u_ref, k):
        u = u_ref[...]
        w = sw_ref[:, k * cb:(k + 1) * cb]
        return (sb_ref[:, k * cb:(k + 1) * cb] + w[0:1] * _shift_rows(u, 1, t, seq_len) + w[1:2] * u
                + w[2:3] * _shift_rows(u, -1, t, seq_len))

    x0 = short_conv(u0_ref, 0)
    z = short_conv(u1_ref, 1) * short_conv(u2_ref, 2)
    wide = lambda a: jnp.concatenate([a[q * seq_len:(q + 1) * seq_len] for q in range(n_seq)], axis=1)
    rep = lambda a: jnp.concatenate([a] * n_seq, axis=1)
    zw = wide(z)
    zb = zw.astype(BF16)
    c, s = cs_ref[0], cs_ref[1]
    ur = jnp.dot(c, zb, preferred_element_type=F32)
    us = jnp.dot(s, zb, preferred_element_type=F32)
    sgn = jnp.where((lax.broadcasted_iota(jnp.int32, zw.shape, 0) & 1) == 1, -1.0, 1.0)
    uny = jnp.sum(sgn * zw, axis=0, keepdims=True)
    kr, ks = rep(kr_ref[...]), rep(ks_ref[...])
    yr = (ur * kr - us * ks).astype(BF16)
    ys = (ur * ks + us * kr).astype(BF16)
    yw = jnp.dot(c, yr, preferred_element_type=F32) + jnp.dot(s, ys, preferred_element_type=F32)
    yw = yw + sgn * (uny * rep(kny_ref[...]))
    y = jnp.concatenate([yw[:, q * cb:(q + 1) * cb] for q in range(n_seq)], axis=0)
    o_ref[...] = (x0 * (y + bias_ref[...] * z)).astype(o_ref.dtype)


def hyena(uh, seq_len, short_w, short_b, cs, kr, ks, kny, bias):
    tokens = uh.shape[0]
    cb = HY_CB
    nc = HY_W // cb
    rows = max(seq_len, HY_ROWS)
    assert seq_len & (seq_len - 1) == 0 and rows % seq_len == 0
    ublk = lambda k: pl.BlockSpec((rows, cb), lambda s, c: (s, k * nc + c))
    chan = lambda r: pl.BlockSpec((r, cb), lambda s, c: (0, c))
    return pl.pallas_call(
        functools.partial(_hyena_kernel, seq_len=seq_len),
        out_shape=jax.ShapeDtypeStruct((tokens, HY_W), BF16),
        grid=(tokens // rows, nc),
        in_specs=[ublk(0), ublk(1), ublk(2),
                  pl.BlockSpec((None, 3, 3 * cb), lambda s, c: (c, 0, 0)),
                  pl.BlockSpec((None, 1, 3 * cb), lambda s, c: (c, 0, 0)),
                  pl.BlockSpec((2, seq_len, seq_len), lambda s, c: (0, 0, 0)),
                  chan(seq_len), chan(seq_len), chan(1), chan(1)],
        out_specs=pl.BlockSpec((rows, cb), lambda s, c: (s, c)),
        compiler_params=_cparams(("parallel", "parallel")),
        name="hyena",
    )(uh, uh, uh, short_w, short_b, cs, kr, ks, kny, bias)


META_E1, META_E2, META_R1, META_R2, META_G1, META_G2 = range(6)


def _route_kernel(x_ref, g_ref, sh_ref, sc_ref, wr_ref, br_ref, h_ref, meta_ref, meta_t_ref, cnt_ref, run_sc):
    tm = x_ref.shape[0]
    lane = lax.broadcasted_iota(jnp.int32, (tm, V7X_LANES), 1)

    @pl.when(pl.program_id(0) == 0)
    def _():
        run_sc[...] = jnp.zeros_like(run_sc)

    h = _norm_mod(x_ref[...], g_ref[...], sh_ref[...], sc_ref[...])
    h_ref[...] = h
    logits = _dot3(h, wr_ref[...]) + br_ref[...]
    valid = lane < N_EXPERTS
    lg = jnp.where(valid, logits, -jnp.inf)
    ex = jnp.exp(lg - jnp.max(lg, axis=-1, keepdims=True))
    p = ex / jnp.sum(ex, axis=-1, keepdims=True)
    p1 = jnp.max(p, axis=-1, keepdims=True)
    i1 = jnp.min(jnp.where((p == p1) & valid, lane, V7X_LANES), axis=-1, keepdims=True)
    rest = jnp.where((lane == i1) | (~valid), -1.0, p)
    p2 = jnp.max(rest, axis=-1, keepdims=True)
    i2 = jnp.min(jnp.where(rest == p2, lane, V7X_LANES), axis=-1, keepdims=True)
    m1 = lane == i1
    m2 = lane == i2
    chosen = jnp.where(m1 | m2, 1.0, 0.0)
    r = lax.broadcasted_iota(jnp.int32, (tm, tm), 0)
    c = lax.broadcasted_iota(jnp.int32, (tm, tm), 1)
    tri = jnp.where(c < r, 1.0, 0.0).astype(BF16)
    before = jnp.dot(tri, chosen.astype(BF16), preferred_element_type=F32) + run_sc[0:1, :]
    rank1 = jnp.sum(jnp.where(m1, before, 0.0), axis=-1, keepdims=True)
    rank2 = jnp.sum(jnp.where(m2, before, 0.0), axis=-1, keepdims=True)
    inv = 1.0 / (p1 + p2)
    vals = (i1.astype(F32), i2.astype(F32), rank1, rank2, p1 * inv, p2 * inv)
    meta = jnp.zeros((tm, V7X_LANES), F32)
    for k, v in enumerate(vals):
        meta = jnp.where(lane == k, v, meta)
    meta_ref[...] = meta
    meta_t_ref[...] = meta.T[:V7X_SUBLANES]
    run_sc[...] = run_sc[...] + jnp.sum(chosen, axis=0, keepdims=True)
    cnt_ref[...] = run_sc[...]


def moe_route(x, g, modtab, cond, wr_pad, br_pad):
    tokens = x.shape[0]
    tm = TM_ROUTE
    zero = lambda i: 0
    const = lambda i: (0, 0)
    return pl.pallas_call(
        _route_kernel,
        out_shape=(jax.ShapeDtypeStruct((tokens, D), F32),
                   jax.ShapeDtypeStruct((tokens, V7X_LANES), F32),
                   jax.ShapeDtypeStruct((V7X_SUBLANES, tokens), F32),
                   jax.ShapeDtypeStruct((V7X_SUBLANES, V7X_LANES), F32)),
        grid=(tokens // tm,),
        in_specs=[pl.BlockSpec((tm, D), lambda i: (i, 0)),
                  pl.BlockSpec((1, D), const),
                  _mod_spec(3, cond, tm, D, zero),
                  _mod_spec(4, cond, tm, D, zero),
                  pl.BlockSpec((D, V7X_LANES), const),
                  pl.BlockSpec((1, V7X_LANES), const)],
        out_specs=(pl.BlockSpec((tm, D), lambda i: (i, 0)),
                   pl.BlockSpec((tm, V7X_LANES), lambda i: (i, 0)),
                   pl.BlockSpec((V7X_SUBLANES, tm), lambda i: (0, i)),
                   pl.BlockSpec((V7X_SUBLANES, V7X_LANES), const)),
        scratch_shapes=[pltpu.VMEM((V7X_SUBLANES, V7X_LANES), F32)],
        compiler_params=_cparams(("arbitrary",)),
        name="moe_route",
    )(x, g.reshape(1, D), modtab, modtab, wr_pad, br_pad)


def _row_copy(src_ref, src_row, dst_ref, dst_row, sem):
    return pltpu.make_async_copy(src_ref.at[pl.ds(src_row, 1)], dst_ref.at[pl.ds(dst_row, 1)], sem)


def _dispatch_kernel(pos_ref, h_ref, hs_in_ref, hs_ref, sem):
    del hs_in_ref
    tm = h_ref.shape[0]
    n_tok = pos_ref.shape[0] // 2
    base = pl.program_id(0) * tm

    def issue(r, carry):
        _row_copy(h_ref, r, hs_ref, pos_ref[base + r], sem).start(priority=0)
        _row_copy(h_ref, r, hs_ref, pos_ref[n_tok + base + r], sem).start(priority=1)
        return carry

    lax.fori_loop(0, tm, issue, 0, unroll=8)
    for _ in range(2):
        pltpu.make_async_copy(h_ref, hs_ref.at[pl.ds(0, tm)], sem).wait()


def moe_dispatch(pos, h, hs):
    tokens = h.shape[0]
    tm = TM_ROUTE
    return pl.pallas_call(
        _dispatch_kernel,
        out_shape=jax.ShapeDtypeStruct(hs.shape, hs.dtype),
        grid_spec=pltpu.PrefetchScalarGridSpec(
            num_scalar_prefetch=1,
            grid=(tokens // tm,),
            in_specs=[pl.BlockSpec((tm, D), lambda i, pos: (i, 0)),
                      pl.BlockSpec(memory_space=pl.ANY)],
            out_specs=pl.BlockSpec(memory_space=pl.ANY),
            scratch_shapes=[pltpu.SemaphoreType.DMA(())]),
        input_output_aliases={2: 0},
        compiler_params=_cparams(("arbitrary",)),
        name="moe_dispatch",
    )(pos, h, hs)


def _experts_kernel(te_ref, sg_ref, su_ref, sd_ref, nu_ref, hs_ref, wg_ref, wu_ref, wd_ref, y_ref,
                    wg_sc, wu_sc, wd_sc):
    del sg_ref, su_ref, sd_ref
    j = pl.program_id(0)
    e = te_ref[j]
    e_prev = te_ref[jnp.maximum(j - 1, 0)]

    @pl.when((j == 0) | (e != e_prev))
    def _():
        wg_sc[...] = wg_ref[...].astype(BF16)
        wu_sc[...] = wu_ref[...].astype(BF16)
        wd_sc[...] = wd_ref[...].astype(BF16)

    @pl.when(j < nu_ref[0])
    def _():
        h = hs_ref[...].astype(BF16)
        y = None
        for c0 in range(0, D_FF_EXPERT, MOE_CHUNK):
            c1 = min(c0 + MOE_CHUNK, D_FF_EXPERT)
            hg = jnp.dot(h, wg_sc[:, c0:c1], preferred_element_type=F32)
            hu = jnp.dot(h, wu_sc[:, c0:c1], preferred_element_type=F32)
            act = (_silu(hg) * hu).astype(BF16)
            yc = jnp.dot(act, wd_sc[c0:c1, :], preferred_element_type=F32)
            y = yc if y is None else y + yc
        y_ref[...] = y

    @pl.when(j >= nu_ref[0])
    def _():
        y_ref[...] = jnp.zeros_like(y_ref)


def moe_experts(tile_expert, stages, n_used, hs, e_gate, e_up, e_down):
    rows = hs.shape[0]
    tmr = TM_EXPERT
    wspec = lambda shape, k: pl.BlockSpec((None,) + shape, lambda j, *pf: (pf[1 + k][j], 0, 0))
    return pl.pallas_call(
        _experts_kernel,
        out_shape=jax.ShapeDtypeStruct((rows, D), F32),
        grid_spec=pltpu.PrefetchScalarGridSpec(
            num_scalar_prefetch=5,
            grid=(rows // tmr,),
            in_specs=[pl.BlockSpec((tmr, D), lambda j, *pf: (j, 0)),
                      wspec((D, D_FF_EXPERT), 0), wspec((D, D_FF_EXPERT), 1), wspec((D_FF_EXPERT, D), 2)],
            out_specs=pl.BlockSpec((tmr, D), lambda j, *pf: (j, 0)),
            scratch_shapes=[pltpu.VMEM((D, D_FF_EXPERT), BF16), pltpu.VMEM((D, D_FF_EXPERT), BF16),
                            pltpu.VMEM((D_FF_EXPERT, D), BF16)]),
        compiler_params=_cparams(("arbitrary",)),
        name="moe_experts",
    )(tile_expert, *stages, n_used, hs, e_gate, e_up, e_down)


def _combine_kernel(pos_ref, x_ref, meta_ref, gt_ref, fg_ref, y_ref, o_ref, b1_sc, b2_sc, sem):
    tm = x_ref.shape[0]
    n_tok = pos_ref.shape[0] // 2
    i = pl.program_id(0)

    def gather(tile, slot):
        base = tile * tm

        def issue(r, carry):
            _row_copy(y_ref, pos_ref[base + r], b1_sc.at[slot], r, sem.at[slot]).start(priority=0)
            _row_copy(y_ref, pos_ref[n_tok + base + r], b2_sc.at[slot], r, sem.at[slot]).start(priority=1)
            return carry

        lax.fori_loop(0, tm, issue, 0, unroll=8)

    @pl.when(i == 0)
    def _():
        gather(0, 0)

    @pl.when(i + 1 < pl.num_programs(0))
    def _():
        gather(i + 1, (i + 1) % 2)

    slot = i % 2
    pltpu.make_async_copy(y_ref.at[pl.ds(0, tm)], b1_sc.at[slot], sem.at[slot]).wait()
    pltpu.make_async_copy(y_ref.at[pl.ds(0, tm)], b2_sc.at[slot], sem.at[slot]).wait()

    meta = meta_ref[...]
    lane = lax.broadcasted_iota(jnp.int32, meta.shape, 1)
    g1 = jnp.sum(jnp.where(lane == META_G1, meta, 0.0), axis=-1, keepdims=True)
    g2 = jnp.sum(jnp.where(lane == META_G2, meta, 0.0), axis=-1, keepdims=True)
    x = x_ref[...] + gt_ref[...] * (g1 * b1_sc[slot] + g2 * b2_sc[slot])
    o_ref[...] = _rms(x, fg_ref[...])


def moe_combine(pos, x, meta, modtab, cond, final_g, y):
    tokens = x.shape[0]
    tm = TM_COMBINE
    return pl.pallas_call(
        _combine_kernel,
        out_shape=jax.ShapeDtypeStruct((tokens, D), F32),
        grid_spec=pltpu.PrefetchScalarGridSpec(
            num_scalar_prefetch=1,
            grid=(tokens // tm,),
            in_specs=[pl.BlockSpec((tm, D), lambda i, pos: (i, 0)),
                      pl.BlockSpec((tm, V7X_LANES), lambda i, pos: (i, 0)),
                      _mod_spec(5, cond, tm, D, lambda i, pos: 0),
                      pl.BlockSpec((1, D), lambda i, pos: (0, 0)),
                      pl.BlockSpec(memory_space=pl.ANY)],
            out_specs=pl.BlockSpec((tm, D), lambda i, pos: (i, 0)),
            scratch_shapes=[pltpu.VMEM((2, tm, D), F32), pltpu.VMEM((2, tm, D), F32),
                            pltpu.SemaphoreType.DMA((2,))]),
        compiler_params=_cparams(("arbitrary",)),
        name="moe_combine",
    )(pos, x, meta, modtab, final_g.reshape(1, D), y)


def moe_plan(metas, counts):
    tmr = TM_EXPERT
    cnts = [c[0, :N_EXPERTS].astype(jnp.int32) for c in counts]
    total = functools.reduce(jnp.add, cnts)
    padded = ((total + tmr - 1) // tmr) * tmr
    ends = jnp.cumsum(padded)
    starts = ends - padded
    n_rows = sum(m.shape[1] for m in metas) * 2 + N_EXPERTS * tmr
    n_tiles = n_rows // tmr
    tile_start = jnp.arange(n_tiles, dtype=jnp.int32) * tmr
    tile_expert = jnp.minimum(jnp.sum(tile_start[:, None] >= ends[None, :], axis=1), N_EXPERTS - 1).astype(jnp.int32)
    n_used = (ends[-1] // tmr).astype(jnp.int32).reshape(1)
    eid = jnp.arange(N_EXPERTS, dtype=jnp.int32)
    later = jnp.where((eid[None, :] > eid[:, None]) & (padded[None, :] > 0), eid[None, :], N_EXPERTS)
    nxt = jnp.min(later, axis=1)
    next_used = jnp.where(nxt == N_EXPERTS, eid, nxt)
    pick = lambda table: jnp.sum(jnp.where(tile_expert[:, None] == eid[None, :], table[None, :], 0), axis=1)
    k_in_group = (tile_start - pick(starts)) // tmr
    tile_next = pick(next_used)
    stages = [jnp.where(k_in_group < k, tile_expert, tile_next).astype(jnp.int32) for k in (1, 2, 3)]
    pos = []
    base = jnp.zeros((N_EXPERTS,), jnp.int32)
    for m, c in zip(metas, cnts):
        first = starts + base
        sel = lambda field: m[field].astype(jnp.int32)
        lookup = lambda e: jnp.sum(jnp.where(e[:, None] == jnp.arange(N_EXPERTS)[None, :], first[None, :], 0), axis=1)
        p1 = lookup(sel(META_E1)) + sel(META_R1)
        p2 = lookup(sel(META_E2)) + sel(META_R2)
        pos.append(jnp.concatenate([p1, p2]).astype(jnp.int32))
        base = base + c
    return pos, tile_expert, stages, n_used, n_rows


def _pad_to(a, shape):
    return jnp.pad(a, [(0, t - s) for s, t in zip(a.shape, shape)])


def _regroup_chunks(a, cb):
    r = a.shape[0]
    return a.reshape(r, 3, HY_W // cb, cb).transpose(2, 0, 1, 3).reshape(HY_W // cb, r, 3 * cb)


def kernel(x_prompt, x_sample, state_l0_lru, cache_l1_ckv, cache_l1_krope, c, c_ctx, l0_norm1, l0_norm2, l0_w_mod, l0_b_mod, l0_w_in, l0_conv_a, l0_lru_conv_w, l0_lru_conv_b, l0_lru_wa, l0_lru_ba, l0_lru_wi, l0_lru_bi, l0_lru_lambda, l0_w_out, l0_ffn_gate, l0_ffn_up, l0_ffn_down, l1_norm1, l1_norm2, l1_w_mod, l1_b_mod, l1_w_in, l1_q_norm, l1_kv_norm, l1_w_q_up, l1_w_kv_up, l1_hy_short_w, l1_hy_short_b, l1_hy_f_w1, l1_hy_f_b1, l1_hy_f_w2, l1_hy_f_b2, l1_hy_f_w3, l1_hy_bias, l1_w_out, l1_router_w, l1_router_b, l1_exp_gate, l1_exp_up, l1_exp_down, final_norm):
    batch, seq, _ = x_prompt.shape
    dec_batch, dec_seq, _ = x_sample.shape
    past_len = cache_l1_ckv.shape[1]

    cond8 = jnp.concatenate([c_ctx[None, :], c, jnp.zeros((V7X_SUBLANES - 1 - dec_batch, D), F32)], axis=0)
    wcat = jnp.concatenate([l0_lru_wa[0], l0_lru_wi[0], l0_lru_wa[1], l0_lru_wi[1]], axis=-1)
    hid = V7X_LANES
    w1p = _pad_to(l1_hy_f_w1, (hid, hid))
    b1p = _pad_to(l1_hy_f_b1.reshape(1, -1), (1, hid))
    w2p = _pad_to(l1_hy_f_w2, (hid, hid))
    b2p = _pad_to(l1_hy_f_b2.reshape(1, -1), (1, hid))
    w3p = _pad_to(l1_hy_f_w3, (hid, 2 * HY_W))
    short_w = _regroup_chunks(l1_hy_short_w, HY_CB)
    short_b = _regroup_chunks(l1_hy_short_b.reshape(1, -1), HY_CB)
    hy_bias = l1_hy_bias.reshape(1, HY_W)
    wr_pad = _pad_to(l1_router_w, (D, V7X_LANES))
    br_pad = _pad_to(l1_router_b.reshape(1, -1), (1, V7X_LANES))

    mod0 = adaln_table(cond8, l0_w_mod, l0_b_mod)
    mod1 = adaln_table(cond8, l1_w_mod, l1_b_mod)

    kv_ctx = kv_up(cache_l1_ckv.reshape(dec_batch * past_len, KV_RANK), l1_w_kv_up)
    kr_ctx = cache_l1_krope.reshape(dec_batch * past_len, ROPE)

    def trunk(x, seq_len, cond, h0, latent):
        u = in0_proj(x, l0_norm1, mod0, cond, l0_w_in)
        ya = conv_a(u, seq_len, l0_conv_a)
        yb, lru_state = rglru(u, seq_len, l0_lru_conv_w, l0_lru_conv_b, wcat, l0_lru_ba, l0_lru_bi,
                              l0_lru_lambda, h0)
        x = out_res([(ya, 0), (yb, 0), (yb, 1)], l0_w_out, x, mod0, cond, 2)
        x = ffn_res(x, l0_norm2, mod0, cond, l0_ffn_gate, l0_ffn_up, l0_ffn_down)
        qnope, qpe, ckv, kr, kv, uh = in1_proj(x, l1_norm1, mod1, cond, l1_w_in, l1_q_norm, l1_kv_norm,
                                               l1_w_q_up, l1_w_kv_up)
        if latent:
            yc = attn_lat(qnope, qpe, kv_ctx, kr_ctx, kv, kr, seq_len, past_len)
        else:
            yc = attn_ctx(qnope, qpe, kv, kr, seq_len)
        cs = dft_tables(seq_len)
        k_r, k_s, k_ny = hy_filter(cs, w1p, b1p, w2p, b2p, w3p)
        yd = hyena(uh, seq_len, short_w, short_b, cs, k_r, k_s, k_ny, hy_bias)
        x = out_res([(yc, 0), (yc, 1), (yd, 0)], l1_w_out, x, mod1, cond, 2)
        return x, lru_state, ckv, kr

    conds = ((0, batch * seq), (1, dec_seq))
    zeros_state = jnp.zeros((batch, 2, LRU_W), F32)
    x_p, new_lru, new_ckv, new_kr = trunk(x_prompt.reshape(batch * seq, D), seq, conds[0], zeros_state, latent=False)
    x_s, _, _, _ = trunk(x_sample.reshape(dec_batch * dec_seq, D), dec_seq, conds[1], state_l0_lru, latent=True)

    xs = (x_p, x_s)
    routed = [moe_route(x, l1_norm2, mod1, cond, wr_pad, br_pad) for x, cond in zip(xs, conds)]
    pos, tile_expert, stages, n_used, n_rows = moe_plan([r[2] for r in routed], [r[3] for r in routed])
    hs = jnp.zeros((n_rows, D), F32)
    for p, r in zip(pos, routed):
        hs = moe_dispatch(p, r[0], hs)
    y_rows = moe_experts(tile_expert, stages, n_used, hs, l1_exp_gate, l1_exp_up, l1_exp_down)
    y_p, y_s = [moe_combine(p, x, r[1], mod1, cond, final_norm, y_rows)
                for p, x, r, cond in zip(pos, xs, routed, conds)]
    return (y_p.reshape(batch, seq, D), y_s.reshape(dec_batch, dec_seq, D), new_lru,
            new_ckv.reshape(batch, seq, KV_RANK), new_kr.reshape(batch, seq, ROPE))
```

```python
import functools
import math

import jax
import jax.numpy as jnp
from jax import lax
from jax.experimental import pallas as pl
from jax.experimental.pallas import tpu as pltpu

F32 = jnp.float32
BF16 = jnp.bfloat16
HIGHEST = lax.Precision.HIGHEST

D = 1024
GRID_W = 64
EPS = 1e-6
CONV_W = 512
LRU_W = 1024
LRU_BW = 128
LRU_C = 8.0
MLA_HEADS = 8
Q_RANK = 384
KV_RANK = 256
NOPE = 128
ROPE = 64
VDIM = 128
QK_DIM = NOPE + ROPE
ROPE_THETA = 10000.0
HY_W = 512
HY_BANDS = 16
HY_TARGET = 1e-2
HY_FAST_DECAY = 0.3
HY_SLOW_DECAY = 1.5
D_FF = 2816
N_EXPERTS = 8
D_FF_EXPERT = 1408
IN0 = 3 * CONV_W + 2 * LRU_W
IN1 = Q_RANK + KV_RANK + ROPE + 3 * HY_W

V7X_LANES = 128
V7X_SUBLANES = 8
V7X_VMEM_LIMIT_BYTES = 56 * 1024 * 1024

TM = 512
TN_IN0 = 512
TF_FFN = 256
MOE_CHUNK = 256
TM_ROUTE = 512
TM_EXPERT = 512
TM_COMBINE = 512
LRU_CB = 256
HY_CB = 256
TQ = 256
ATTN_CTX_SEQS = 4
CONV_A_ROWS = 1024
LRU_ROWS = 1024
HY_ROWS = 1024
TM_IN1 = 512


def _cparams(sem):
    return pltpu.CompilerParams(dimension_semantics=sem, vmem_limit_bytes=V7X_VMEM_LIMIT_BYTES)


def _sigmoid(x):
    return 0.5 * jnp.tanh(0.5 * x) + 0.5


def _silu(x):
    return x * _sigmoid(x)


def _norm_mod(x, g, shift, scale):
    ms = jnp.mean(x * x, axis=-1, keepdims=True)
    y = x * lax.rsqrt(ms + EPS) * g
    return y * (1.0 + scale) + shift


def _mod_spec(comp, cond, tm, width, col_fn, tile_fn=lambda *ids: ids[0]):
    row0, seg = cond
    assert seg % tm == 0
    return pl.BlockSpec((None, 1, width),
                        lambda *ids: (comp * 3 + row0 + (tile_fn(*ids) * tm) // seg, 0, col_fn(*ids)))


def _dot3(a, b):
    a_hi = a.astype(BF16)
    a_lo = (a - a_hi.astype(F32)).astype(BF16)
    b_hi = b.astype(BF16)
    b_lo = (b - b_hi.astype(F32)).astype(BF16)
    n = a.shape[0]
    y = jnp.dot(jnp.concatenate([a_hi, a_lo], axis=0), b_hi, preferred_element_type=F32)
    return y[:n] + y[n:] + jnp.dot(a_hi, b_lo, preferred_element_type=F32)


def _adaln_kernel(c_ref, w_ref, b_ref, o_ref):
    o_ref[...] = _dot3(_silu(c_ref[...]), w_ref[...]) + b_ref[...]


def adaln_table(cond8, w_mod, b_mod):
    tn = 1536
    m = pl.pallas_call(
        _adaln_kernel,
        out_shape=jax.ShapeDtypeStruct((V7X_SUBLANES, 6 * D), F32),
        grid=(6 * D // tn,),
        in_specs=[pl.BlockSpec((V7X_SUBLANES, D), lambda j: (0, 0)),
                  pl.BlockSpec((D, tn), lambda j: (0, j)),
                  pl.BlockSpec((1, tn), lambda j: (0, j))],
        out_specs=pl.BlockSpec((V7X_SUBLANES, tn), lambda j: (0, j)),
        compiler_params=_cparams(("arbitrary",)),
        name="adaln",
    )(cond8, w_mod, b_mod.reshape(1, 6 * D))
    return m[:3].reshape(3, 6, D).transpose(1, 0, 2).reshape(18, 1, D)


def _tile_of(n_load):
    return lambda s: jnp.maximum(s - n_load, 0)


def _block_of(n_load):
    return lambda s: jnp.minimum(s, n_load - 1)


def _in0_kernel(x_ref, g_ref, sh_ref, sc_ref, w_ref, o_ref, w_sc):
    s = pl.program_id(0)
    n_load, _, tn = w_sc.shape

    @pl.when(s < n_load)
    def _():
        w_sc[s] = w_ref[...].astype(BF16)

    @pl.when(s >= n_load)
    def _():
        h = _norm_mod(x_ref[...], g_ref[...], sh_ref[...], sc_ref[...]).astype(BF16)
        for j in range(n_load):
            o_ref[:, j * tn:(j + 1) * tn] = jnp.dot(h, w_sc[j], preferred_element_type=F32).astype(o_ref.dtype)


def in0_proj(x, g, modtab, cond, w_in):
    tn = TN_IN0
    tokens = x.shape[0]
    n = w_in.shape[1]
    n_load = n // tn
    tile = _tile_of(n_load)
    blk = _block_of(n_load)
    zero = lambda s: 0
    return pl.pallas_call(
        _in0_kernel,
        out_shape=jax.ShapeDtypeStruct((tokens, n), BF16),
        grid=(n_load + tokens // TM,),
        in_specs=[pl.BlockSpec((TM, D), lambda s: (tile(s), 0)),
                  pl.BlockSpec((1, D), lambda s: (0, 0)),
                  _mod_spec(0, cond, TM, D, zero, tile),
                  _mod_spec(1, cond, TM, D, zero, tile),
                  pl.BlockSpec((D, tn), lambda s: (0, blk(s)))],
        out_specs=pl.BlockSpec((TM, n), lambda s: (tile(s), 0)),
        scratch_shapes=[pltpu.VMEM((n_load, D, tn), BF16)],
        compiler_params=_cparams(("arbitrary",)),
        name="in0_proj",
    )(x, g.reshape(1, D), modtab, modtab, w_in)


def _shift_rows(v, d, t, seq_len=None):
    n = v.shape[0]
    seq_len = n if seq_len is None else seq_len
    if d > 0:
        return jnp.where(t < d, 0.0, pltpu.roll(v, d, 0))
    return jnp.where(t >= seq_len + d, 0.0, pltpu.roll(v, n + d, 0))


def _conv_a_kernel(b_ref, c_ref, x_ref, w_ref, o_ref, *, seq_len):
    v = c_ref[...].astype(F32) * x_ref[...].astype(F32)
    t = lax.broadcasted_iota(jnp.int32, v.shape, 0) & (seq_len - 1)
    w = w_ref[...]
    y = w[0:1] * _shift_rows(v, 1, t, seq_len) + w[1:2] * v + w[2:3] * _shift_rows(v, -1, t, seq_len)
    o_ref[...] = (b_ref[...].astype(F32) * y).astype(o_ref.dtype)


def conv_a(u, seq_len, conv_w):
    tokens = u.shape[0]
    rows = max(seq_len, CONV_A_ROWS)
    assert seq_len & (seq_len - 1) == 0 and rows % seq_len == 0
    return pl.pallas_call(
        functools.partial(_conv_a_kernel, seq_len=seq_len),
        out_shape=jax.ShapeDtypeStruct((tokens, CONV_W), BF16),
        grid=(tokens // rows,),
        in_specs=[pl.BlockSpec((rows, CONV_W), lambda s: (s, 0)),
                  pl.BlockSpec((rows, CONV_W), lambda s: (s, 1)),
                  pl.BlockSpec((rows, CONV_W), lambda s: (s, 2)),
                  pl.BlockSpec((3, CONV_W), lambda s: (0, 0))],
        out_specs=pl.BlockSpec((rows, CONV_W), lambda s: (s, 0)),
        compiler_params=_cparams(("parallel",)),
        name="conv_a",
    )(u, u, u, conv_w)


def _group_scan(a, b, reverse):
    n, c = a.shape
    a3 = a.reshape(n // V7X_SUBLANES, V7X_SUBLANES, c)
    b3 = b.reshape(n // V7X_SUBLANES, V7X_SUBLANES, c)
    t8 = lax.broadcasted_iota(jnp.int32, a3.shape, 1)
    for d in (1, 2, 4):
        if reverse:
            keep = t8 < V7X_SUBLANES - d
            shift = V7X_SUBLANES - d
        else:
            keep = t8 >= d
            shift = d
        a_sh = jnp.where(keep, pltpu.roll(a3, shift, 1), 1.0)
        b_sh = jnp.where(keep, pltpu.roll(b3, shift, 1), 0.0)
        b3 = a3 * b_sh + b3
        a3 = a3 * a_sh
    return a3.reshape(n, c), b3.reshape(n, c)


def _rglru_kernel(gate_ref, xb_ref, cw_ref, cb_ref, wcat_ref, ba_ref, bi_ref, lam_ref, h0_ref,
                  y_ref, st_ref, af_sc, bf_sc, ab_sc, bb_sc, hf_sc, hb_sc, *, seq_len):
    n, cb = xb_ref.shape
    n_seq = n // seq_len
    xb = xb_ref[...].astype(F32)
    t = lax.broadcasted_iota(jnp.int32, xb.shape, 0) & (seq_len - 1)
    cw = cw_ref[...]
    sh = lambda d: _shift_rows(xb, d, t, seq_len)
    xc = cb_ref[...] + cw[0:1] * sh(2) + cw[1:2] * sh(1) + cw[2:3] * xb + cw[3:4] * sh(-1)
    xcb = xc.astype(BF16)
    g = [jnp.dot(xcb[:, k * LRU_BW:(k + 1) * LRU_BW], wcat_ref[k].astype(BF16), preferred_element_type=F32)
         for k in range(cb // LRU_BW)]

    def direction(d):
        ga = jnp.concatenate([gk[:, (2 * d) * LRU_BW:(2 * d + 1) * LRU_BW] for gk in g], axis=1)
        gi = jnp.concatenate([gk[:, (2 * d + 1) * LRU_BW:(2 * d + 2) * LRU_BW] for gk in g], axis=1)
        r = _sigmoid(ga + ba_ref[d:d + 1, :])
        i = _sigmoid(gi + bi_ref[d:d + 1, :])
        log_a = (-LRU_C * jax.nn.softplus(-lam_ref[d:d + 1, :])) * r
        a = jnp.exp(log_a)
        m = 1.0 - a * a
        mult = m * lax.rsqrt(jnp.maximum(m, 1e-30))
        return a, mult * (i * xc)

    a_f, b_f = direction(0)
    a_f, b_f = _group_scan(a_f, b_f, reverse=False)
    af_sc[...] = a_f
    bf_sc[...] = b_f
    a_b, b_b = direction(1)
    a_b, b_b = _group_scan(a_b, b_b, reverse=True)
    ab_sc[...] = a_b
    bb_sc[...] = b_b

    ng = seq_len // V7X_SUBLANES
    bcast = lambda row: jnp.broadcast_to(row, (V7X_SUBLANES, cb))
    init = tuple((bcast(h0_ref[q, 0:1, :]), bcast(h0_ref[q, 1:2, :])) for q in range(n_seq))

    def step(k, carry):
        out = []
        for q, (hf_in, hb_in) in enumerate(carry):
            rf = pl.multiple_of(q * seq_len + k * V7X_SUBLANES, V7X_SUBLANES)
            rb = pl.multiple_of(q * seq_len + (ng - 1 - k) * V7X_SUBLANES, V7X_SUBLANES)
            hf = af_sc[pl.ds(rf, V7X_SUBLANES), :] * hf_in + bf_sc[pl.ds(rf, V7X_SUBLANES), :]
            hb = ab_sc[pl.ds(rb, V7X_SUBLANES), :] * hb_in + bb_sc[pl.ds(rb, V7X_SUBLANES), :]
            hf_sc[pl.ds(rf, V7X_SUBLANES), :] = hf
            hb_sc[pl.ds(rb, V7X_SUBLANES), :] = hb
            out.append((bcast(hf[V7X_SUBLANES - 1:V7X_SUBLANES]), bcast(hb[0:1])))
        return tuple(out)

    final = lax.fori_loop(0, ng, step, init)
    for q, (hf_last, hb_first) in enumerate(final):
        st_ref[q, 0:1, :] = hf_last[0:1]
        st_ref[q, 1:2, :] = hb_first[0:1]

    gt = gate_ref[...].astype(F32)
    gelu = 0.5 * gt * (1.0 + jnp.tanh(math.sqrt(2.0 / math.pi) * (gt + 0.044715 * (gt * gt * gt))))
    y_ref[...] = ((hf_sc[...] + hb_sc[...]) * gelu).astype(y_ref.dtype)


def rglru(u, seq_len, conv_w, conv_b, wcat, ba, bi, lam, h0):
    tokens = u.shape[0]
    nseq = tokens // seq_len
    cb = LRU_CB
    rows = max(seq_len, LRU_ROWS)
    assert seq_len & (seq_len - 1) == 0 and rows % seq_len == 0
    per_blk = rows // seq_len
    gate_blk0 = 3 * CONV_W // cb
    xb_blk0 = (3 * CONV_W + LRU_W) // cb
    seq_scr = lambda: pltpu.VMEM((rows, cb), F32)
    return pl.pallas_call(
        functools.partial(_rglru_kernel, seq_len=seq_len),
        out_shape=(jax.ShapeDtypeStruct((tokens, LRU_W), BF16), jax.ShapeDtypeStruct((nseq, 2, LRU_W), F32)),
        grid=(tokens // rows, LRU_W // cb),
        in_specs=[pl.BlockSpec((rows, cb), lambda s, c: (s, gate_blk0 + c)),
                  pl.BlockSpec((rows, cb), lambda s, c: (s, xb_blk0 + c)),
                  pl.BlockSpec((4, cb), lambda s, c: (0, c)),
                  pl.BlockSpec((1, cb), lambda s, c: (0, c)),
                  pl.BlockSpec((cb // LRU_BW, LRU_BW, 4 * LRU_BW), lambda s, c: (c, 0, 0)),
                  pl.BlockSpec((2, cb), lambda s, c: (0, c)),
                  pl.BlockSpec((2, cb), lambda s, c: (0, c)),
                  pl.BlockSpec((2, cb), lambda s, c: (0, c)),
                  pl.BlockSpec((per_blk, 2, cb), lambda s, c: (s, 0, c))],
        out_specs=(pl.BlockSpec((rows, cb), lambda s, c: (s, c)),
                   pl.BlockSpec((per_blk, 2, cb), lambda s, c: (s, 0, c))),
        scratch_shapes=[seq_scr() for _ in range(6)],
        compiler_params=_cparams(("parallel", "parallel")),
        name="rglru",
    )(u, u, conv_w, conv_b.reshape(1, LRU_W), wcat, ba, bi, lam, h0)


def _mix_ffn_kernel(p0_ref, p1_ref, p2_ref, wo_ref, x_ref, g1_ref, g_ref, sh_ref, sc_ref, g2_ref,
                    wg_ref, wu_ref, wd_ref, o_ref, wo_sc, wg_sc, wu_sc, wd_sc):
    s = pl.program_id(0)
    n_load = wg_sc.shape[0]
    n_out = wo_sc.shape[0]

    @pl.when(s < n_out)
    def _():
        wo_sc[s] = wo_ref[...].astype(BF16)

    @pl.when(s < n_load)
    def _():
        wg_sc[s] = wg_ref[...].astype(BF16)
        wu_sc[s] = wu_ref[...].astype(BF16)
        wd_sc[s] = wd_ref[...].astype(BF16)

    @pl.when(s >= n_load)
    def _():
        m = jnp.dot(p0_ref[...], wo_sc[0], preferred_element_type=F32)
        m += jnp.dot(p1_ref[...], wo_sc[1], preferred_element_type=F32)
        m += jnp.dot(p2_ref[...], wo_sc[2], preferred_element_type=F32)
        x = x_ref[...] + g1_ref[...] * m
        h = _norm_mod(x, g_ref[...], sh_ref[...], sc_ref[...]).astype(BF16)
        y = None
        for f in range(n_load):
            hg = jnp.dot(h, wg_sc[f], preferred_element_type=F32)
            hu = jnp.dot(h, wu_sc[f], preferred_element_type=F32)
            act = (_silu(hg) * hu).astype(BF16)
            yf = jnp.dot(act, wd_sc[f], preferred_element_type=F32)
            y = yf if y is None else y + yf
        o_ref[...] = x + g2_ref[...] * y


def mix_ffn(parts, w_out, x, g, modtab, cond, w_gate, w_up, w_down):
    tokens = x.shape[0]
    tf = TF_FFN
    kb = 512
    n_load = D_FF // tf
    n_out = len(parts)
    assert n_out <= n_load
    tile = _tile_of(n_load)
    blk = _block_of(n_load)
    oblk = _block_of(n_out)
    zero = lambda s: 0
    lhs_specs = [pl.BlockSpec((TM, kb), (lambda s, cbk=cbk: (tile(s), cbk))) for _, cbk in parts]
    return pl.pallas_call(
        _mix_ffn_kernel,
        out_shape=jax.ShapeDtypeStruct((tokens, D), F32),
        grid=(n_load + tokens // TM,),
        in_specs=lhs_specs + [
            pl.BlockSpec((kb, D), lambda s: (oblk(s), 0)),
            pl.BlockSpec((TM, D), lambda s: (tile(s), 0)),
            _mod_spec(2, cond, TM, D, zero, tile),
            pl.BlockSpec((1, D), lambda s: (0, 0)),
            _mod_spec(3, cond, TM, D, zero, tile),
            _mod_spec(4, cond, TM, D, zero, tile),
            _mod_spec(5, cond, TM, D, zero, tile),
            pl.BlockSpec((D, tf), lambda s: (0, blk(s))),
            pl.BlockSpec((D, tf), lambda s: (0, blk(s))),
            pl.BlockSpec((tf, D), lambda s: (blk(s), 0))],
        out_specs=pl.BlockSpec((TM, D), lambda s: (tile(s), 0)),
        scratch_shapes=[pltpu.VMEM((n_out, kb, D), BF16),
                        pltpu.VMEM((n_load, D, tf), BF16), pltpu.VMEM((n_load, D, tf), BF16),
                        pltpu.VMEM((n_load, tf, D), BF16)],
        compiler_params=_cparams(("arbitrary",)),
        name="mix_ffn",
    )(*[a for a, _ in parts], w_out, x, modtab, g.reshape(1, D), modtab, modtab, modtab, w_gate, w_up, w_down)


def _rms(x, g):
    return x * lax.rsqrt(jnp.mean(x * x, axis=-1, keepdims=True) + EPS) * g


def _in1_kernel(x_ref, g_ref, sh_ref, sc_ref, w_ref, qn_ref, kvn_ref, wq_ref, wkv_ref,
                qnope_ref, qpe_ref, ckv_ref, kr_ref, kv_ref, uh_ref, w_sc, wq_sc, wkv_sc):
    @pl.when(pl.program_id(0) == 0)
    def _():
        w_sc[...] = w_ref[...].astype(BF16)
        for h in range(MLA_HEADS):
            c0 = h * QK_DIM
            wq_sc[:, h * NOPE:(h + 1) * NOPE] = wq_ref[:, c0:c0 + NOPE].astype(BF16)
            r0 = MLA_HEADS * NOPE + h * ROPE
            wq_sc[:, r0:r0 + ROPE] = wq_ref[:, c0 + NOPE:c0 + QK_DIM].astype(BF16)
        wkv_sc[...] = wkv_ref[...].astype(BF16)

    h = _norm_mod(x_ref[...], g_ref[...], sh_ref[...], sc_ref[...]).astype(BF16)
    u = jnp.dot(h, w_sc[...], preferred_element_type=F32)
    o1, o2, o3 = Q_RANK, Q_RANK + KV_RANK, Q_RANK + KV_RANK + ROPE
    cq = _rms(u[:, :o1], qn_ref[...])
    q = jnp.dot(cq.astype(BF16), wq_sc[...], preferred_element_type=F32) * _SCALE
    qnope_ref[...] = q[:, :MLA_HEADS * NOPE].astype(qnope_ref.dtype)
    qpe_ref[...] = q[:, MLA_HEADS * NOPE:]
    ckv = _rms(u[:, o1:o2], kvn_ref[...])
    ckv_ref[...] = ckv
    kv_ref[...] = jnp.dot(ckv.astype(BF16), wkv_sc[...], preferred_element_type=F32).astype(kv_ref.dtype)
    kr_ref[...] = u[:, o2:o3]
    uh_ref[...] = u[:, o3:]


def in1_proj(x, g, modtab, cond, w_in, q_norm, kv_norm, w_q_up, w_kv_up):
    tokens = x.shape[0]
    tm = TM_IN1
    nkv = MLA_HEADS * (NOPE + VDIM)
    const = lambda i: (0, 0)
    zero = lambda i: 0
    once = pl.Buffered(1)
    outs = (jax.ShapeDtypeStruct((tokens, MLA_HEADS * NOPE), BF16),
            jax.ShapeDtypeStruct((tokens, MLA_HEADS * ROPE), F32),
            jax.ShapeDtypeStruct((tokens, KV_RANK), F32),
            jax.ShapeDtypeStruct((tokens, ROPE), F32),
            jax.ShapeDtypeStruct((tokens, nkv), BF16),
            jax.ShapeDtypeStruct((tokens, 3 * HY_W), F32))
    row = lambda w: pl.BlockSpec((tm, w), lambda i: (i, 0))
    return pl.pallas_call(
        _in1_kernel,
        out_shape=outs,
        grid=(tokens // tm,),
        in_specs=[row(D),
                  pl.BlockSpec((1, D), const),
                  _mod_spec(0, cond, tm, D, zero),
                  _mod_spec(1, cond, tm, D, zero),
                  pl.BlockSpec((D, IN1), const, pipeline_mode=once),
                  pl.BlockSpec((1, Q_RANK), const),
                  pl.BlockSpec((1, KV_RANK), const),
                  pl.BlockSpec((Q_RANK, MLA_HEADS * QK_DIM), const, pipeline_mode=once),
                  pl.BlockSpec((KV_RANK, nkv), const, pipeline_mode=once)],
        out_specs=tuple(row(o.shape[1]) for o in outs),
        scratch_shapes=[pltpu.VMEM((D, IN1), BF16), pltpu.VMEM((Q_RANK, MLA_HEADS * QK_DIM), BF16),
                        pltpu.VMEM((KV_RANK, nkv), BF16)],
        compiler_params=_cparams(("arbitrary",)),
        name="in1_proj",
    )(x, g.reshape(1, D), modtab, modtab, w_in, q_norm.reshape(1, Q_RANK), kv_norm.reshape(1, KV_RANK),
      w_q_up, w_kv_up)


def _mm_kernel(a_ref, w_ref, o_ref):
    o_ref[...] = jnp.dot(a_ref[...].astype(BF16), w_ref[...].astype(BF16),
                         preferred_element_type=F32).astype(o_ref.dtype)


def kv_up(ckv, w_kv_up):
    rows = ckv.shape[0]
    n = w_kv_up.shape[1]
    return pl.pallas_call(
        _mm_kernel,
        out_shape=jax.ShapeDtypeStruct((rows, n), BF16),
        grid=(rows // TM,),
        in_specs=[pl.BlockSpec((TM, KV_RANK), lambda i: (i, 0)), pl.BlockSpec((KV_RANK, n), lambda i: (0, 0))],
        out_specs=pl.BlockSpec((TM, n), lambda i: (i, 0)),
        compiler_params=_cparams(("parallel",)),
        name="kv_up",
    )(ckv, w_kv_up)


_NT = (((1,), (1,)), ((), ()))
_SCALE = 1.0 / math.sqrt(QK_DIM)


def _fill_rope_tables(cos_ref, sin_ref):
    n, width = cos_ref.shape
    n_grid_rows = n // GRID_W
    n_freq = ROPE // 4

    def trig(count):
        lane = lax.broadcasted_iota(jnp.int32, (count, width), 1)
        j = lane & (ROPE // 2 - 1)
        inv = jnp.exp((j & (n_freq - 1)).astype(F32) * (-math.log(ROPE_THETA) / n_freq))
        ang = lax.broadcasted_iota(jnp.int32, (count, width), 0).astype(F32) * inv
        return jnp.cos(ang), jnp.sin(ang), j < n_freq

    cos_c, sin_c, by_row = trig(GRID_W)
    cos_r, sin_r, _ = trig(n_grid_rows)
    for r in range(n_grid_rows):
        rows = slice(r * GRID_W, (r + 1) * GRID_W)
        cos_ref[rows, :] = jnp.where(by_row, jnp.broadcast_to(cos_r[r:r + 1], cos_c.shape), cos_c)
        sin_ref[rows, :] = jnp.where(by_row, jnp.broadcast_to(sin_r[r:r + 1], sin_c.shape), sin_c)


def _rope(x, cos, sin):
    width = x.shape[1]
    lane = lax.broadcasted_iota(jnp.int32, x.shape, 1)
    first_half = (lane & (ROPE - 1)) < ROPE // 2
    xr = jnp.where(first_half, -pltpu.roll(x, width - ROPE // 2, 1), pltpu.roll(x, ROPE // 2, 1))
    return x * cos + xr * sin


def _ones_column(n):
    lane = lax.broadcasted_iota(jnp.int32, (n, VDIM), 1)
    return jnp.where(lane == 0, 1.0, 0.0).astype(BF16)


def _head_attention(qcat, kcat, vaug):
    s = lax.dot_general(qcat, kcat, _NT, preferred_element_type=F32)
    p = jnp.exp(s - jnp.max(s, axis=-1, keepdims=True)).astype(BF16)
    oa = jnp.dot(p, vaug, preferred_element_type=F32)
    return oa[:, :VDIM] / oa[:, VDIM:VDIM + 1]


def _attn_ctx_kernel(qn_ref, qpe_ref, kv_ref, kr_ref, o_ref, *, seq_len):
    n = qn_ref.shape[0]
    n_seq = n // seq_len
    ones = _ones_column(n)
    kpe = kr_ref[...].astype(BF16)
    per_seq = lambda a: a.reshape(n_seq, seq_len, a.shape[-1])
    for h in range(MLA_HEADS):
        c0 = h * (NOPE + VDIM)
        qcat = per_seq(jnp.concatenate([qn_ref[:, h * NOPE:(h + 1) * NOPE],
                                        qpe_ref[:, h * ROPE:(h + 1) * ROPE].astype(BF16)], axis=1))
        kcat = per_seq(jnp.concatenate([kv_ref[:, c0:c0 + NOPE], kpe], axis=1))
        vaug = per_seq(jnp.concatenate([kv_ref[:, c0 + NOPE:c0 + NOPE + VDIM], ones], axis=1))
        s = jnp.einsum("bqd,bkd->bqk", qcat, kcat, preferred_element_type=F32)
        p = jnp.exp(s - jnp.max(s, axis=-1, keepdims=True)).astype(BF16)
        oa = jnp.einsum("bqk,bkd->bqd", p, vaug, preferred_element_type=F32)
        o = oa[:, :, :VDIM] / oa[:, :, VDIM:VDIM + 1]
        o_ref[:, h * VDIM:(h + 1) * VDIM] = o.reshape(n, VDIM).astype(o_ref.dtype)


def attn_ctx(qnope, qpe, kv, kr, seq_len):
    tokens = qnope.shape[0]
    rows = ATTN_CTX_SEQS * seq_len
    blk = lambda w: pl.BlockSpec((rows, w), lambda s: (s, 0))
    return pl.pallas_call(
        functools.partial(_attn_ctx_kernel, seq_len=seq_len),
        out_shape=jax.ShapeDtypeStruct((tokens, MLA_HEADS * VDIM), BF16),
        grid=(tokens // rows,),
        in_specs=[blk(MLA_HEADS * NOPE), blk(MLA_HEADS * ROPE), blk(MLA_HEADS * (NOPE + VDIM)), blk(ROPE)],
        out_specs=blk(MLA_HEADS * VDIM),
        compiler_params=_cparams(("parallel",)),
        name="attn_ctx",
    )(qnope, qpe, kv, kr)


def _attn_lat_kernel(qn_ref, qpe_ref, kvc_ref, krc_ref, kvl_ref, krl_ref, o_ref, kcat_sc, vaug_sc, cos_sc, sin_sc):
    tq = qn_ref.shape[0]
    n_ctx = krc_ref.shape[0]
    n_lat = krl_ref.shape[0]

    @pl.when(pl.program_id(1) == 0)
    def _():
        _fill_rope_tables(cos_sc, sin_sc)
        kr2 = jnp.concatenate([krl_ref[...], krl_ref[...]], axis=1)
        kpe_lat = _rope(kr2, cos_sc[...], sin_sc[...])[:, :ROPE].astype(BF16)
        kpe_ctx = krc_ref[...].astype(BF16)
        ones_c, ones_l = _ones_column(n_ctx), _ones_column(n_lat)
        for h in range(MLA_HEADS):
            c0 = h * (NOPE + VDIM)
            for r0, nr, kv_ref, kpe, ones in ((0, n_ctx, kvc_ref, kpe_ctx, ones_c), (n_ctx, n_lat, kvl_ref, kpe_lat, ones_l)):
                kcat_sc[h, r0:r0 + nr, 0:NOPE] = kv_ref[:, c0:c0 + NOPE]
                kcat_sc[h, r0:r0 + nr, NOPE:QK_DIM] = kpe
                vaug_sc[h, r0:r0 + nr, 0:VDIM] = kv_ref[:, c0 + NOPE:c0 + NOPE + VDIM]
                vaug_sc[h, r0:r0 + nr, VDIM:2 * VDIM] = ones

    q0 = pl.multiple_of(pl.program_id(1) * tq, tq)
    rep = lambda a: jnp.concatenate([a] * (MLA_HEADS // 2), axis=1)
    qp_all = _rope(qpe_ref[...], rep(cos_sc[pl.ds(q0, tq), :]), rep(sin_sc[pl.ds(q0, tq), :])).astype(BF16)
    for h in range(MLA_HEADS):
        qcat = jnp.concatenate([qn_ref[:, h * NOPE:(h + 1) * NOPE], qp_all[:, h * ROPE:(h + 1) * ROPE]], axis=1)
        o_ref[:, h * VDIM:(h + 1) * VDIM] = _head_attention(qcat, kcat_sc[h], vaug_sc[h]).astype(o_ref.dtype)


def attn_lat(qnope, qpe, kv_ctx, kr_ctx, kv_lat, kr_lat, seq_len, ctx_len):
    tokens = qnope.shape[0]
    nq = seq_len // TQ
    qblk = lambda w: pl.BlockSpec((TQ, w), lambda b, i: (b * nq + i, 0))
    seq = lambda n, w: pl.BlockSpec((n, w), lambda b, i: (b, 0))
    nkv = MLA_HEADS * (NOPE + VDIM)
    n_keys = ctx_len + seq_len
    return pl.pallas_call(
        _attn_lat_kernel,
        out_shape=jax.ShapeDtypeStruct((tokens, MLA_HEADS * VDIM), BF16),
        grid=(tokens // seq_len, nq),
        in_specs=[qblk(MLA_HEADS * NOPE), qblk(MLA_HEADS * ROPE), seq(ctx_len, nkv), seq(ctx_len, ROPE),
                  seq(seq_len, nkv), seq(seq_len, ROPE)],
        out_specs=qblk(MLA_HEADS * VDIM),
        scratch_shapes=[pltpu.VMEM((MLA_HEADS, n_keys, QK_DIM), BF16),
                        pltpu.VMEM((MLA_HEADS, n_keys, 2 * VDIM), BF16),
                        pltpu.VMEM((seq_len, 2 * ROPE), F32), pltpu.VMEM((seq_len, 2 * ROPE), F32)],
        compiler_params=_cparams(("parallel", "arbitrary")),
        name="attn_lat",
    )(qnope, qpe, kv_ctx, kr_ctx, kv_lat, kr_lat)


def _dft_kernel(o_ref):
    tr, n = o_ref.shape[1], o_ref.shape[2]
    nb = n // V7X_LANES
    f = pl.program_id(0) * tr + lax.broadcasted_iota(jnp.int32, (tr, V7X_LANES), 0)
    j = lax.broadcasted_iota(jnp.int32, (tr, V7X_LANES), 1)

    def cos_sin(m):
        ang = (m & (2 * n - 1)).astype(F32) * (math.pi / n)
        return jnp.cos(ang), jnp.sin(ang)

    cj, sj = cos_sin(f * j)
    cb, sb = cos_sin(f * (j * V7X_LANES))
    for b in range(nb):
        cbb, sbb = cb[:, b:b + 1], sb[:, b:b + 1]
        cols = slice(b * V7X_LANES, (b + 1) * V7X_LANES)
        o_ref[0, :, cols] = (cbb * cj - sbb * sj).astype(o_ref.dtype)
        o_ref[1, :, cols] = (sbb * cj + cbb * sj).astype(o_ref.dtype)


def dft_tables(n):
    tr = 128
    return pl.pallas_call(
        _dft_kernel,
        out_shape=jax.ShapeDtypeStruct((2, n, n), BF16),
        grid=(n // tr,),
        out_specs=pl.BlockSpec((2, tr, n), lambda i: (0, i, 0)),
        compiler_params=_cparams(("parallel",)),
        name="dft_tables",
    )()


def _split_dot(table, x):
    hi = x.astype(BF16)
    lo = (x - hi.astype(F32)).astype(BF16)
    return (jnp.dot(table, hi, preferred_element_type=F32) + jnp.dot(table, lo, preferred_element_type=F32))


def _hy_filter_kernel(cs_ref, w1_ref, b1_ref, w2_ref, b2_ref, w3_ref, kr_ref, ks_ref, kny_ref):
    n = cs_ref.shape[1]
    row = lax.broadcasted_iota(jnp.int32, (n, V7X_LANES), 0).astype(F32)
    lane = lax.broadcasted_iota(jnp.int32, (n, V7X_LANES), 1)
    t = row * (1.0 / (n - 1))
    w = (2.0 * math.pi) * row / n
    band = jnp.where(lane <= HY_BANDS, lane - 1, lane - 1 - HY_BANDS).astype(F32)
    freq = 1e-4 + band * ((HY_BANDS - 1 - 1e-4) / (HY_BANDS - 1))
    arg = jnp.where(lane <= HY_BANDS, freq * w + 0.5 * math.pi, -(freq * w))
    z = jnp.where(lane == 0, t, jnp.where(lane <= 2 * HY_BANDS, jnp.sin(arg), 0.0))
    hid = jnp.sin(_dot3(z, w1_ref[...]) + b1_ref[...])
    hid = jnp.sin(_dot3(hid, w2_ref[...]) + b2_ref[...])
    hf = _dot3(hid, w3_ref[...])

    rowc = lax.broadcasted_iota(jnp.int32, (n, HY_W), 0)
    chan = lax.broadcasted_iota(jnp.int32, (n, HY_W), 1).astype(F32)
    max_decay = math.log(HY_TARGET) / HY_FAST_DECAY
    min_decay = math.log(HY_TARGET) / HY_SLOW_DECAY
    deltas = min_decay + chan * ((max_decay - min_decay) / (HY_W - 1))
    decay = jnp.exp(-(rowc.astype(F32) * (1.0 / (n - 1))) * jnp.abs(deltas))
    h_fwd = hf[:, :HY_W] * decay
    h_bwd = jnp.where(rowc == 0, 0.0, hf[:, HY_W:] * decay)
    norm = jnp.sum(jnp.abs(h_fwd) + jnp.abs(h_bwd), axis=0, keepdims=True)
    even = (h_fwd + h_bwd) / norm
    odd = (h_fwd - h_bwd) / norm
    cf = jnp.where(rowc == 0, 1.0, 2.0) * (1.0 / (2 * n))
    kr_ref[...] = cf * _split_dot(cs_ref[0], even)
    ks_ref[...] = cf * _split_dot(cs_ref[1], odd)
    sgn = jnp.where((rowc & 1) == 1, -1.0, 1.0)
    kny_ref[...] = jnp.sum(sgn * even, axis=0, keepdims=True) * (1.0 / (2 * n))


def hy_filter(cs, w1p, b1p, w2p, b2p, w3p):
    n = cs.shape[1]
    full = lambda a: pl.BlockSpec(a.shape, lambda: (0,) * a.ndim)
    args = (cs, w1p, b1p, w2p, b2p, w3p)
    return pl.pallas_call(
        _hy_filter_kernel,
        out_shape=(jax.ShapeDtypeStruct((n, HY_W), F32), jax.ShapeDtypeStruct((n, HY_W), F32),
                   jax.ShapeDtypeStruct((1, HY_W), F32)),
        in_specs=[full(a) for a in args],
        out_specs=(pl.BlockSpec((n, HY_W), lambda: (0, 0)), pl.BlockSpec((n, HY_W), lambda: (0, 0)),
                   pl.BlockSpec((1, HY_W), lambda: (0, 0))),
        compiler_params=pltpu.CompilerParams(vmem_limit_bytes=V7X_VMEM_LIMIT_BYTES),
        name="hy_filter",
    )(*args)


def _hyena_kernel(u0_ref, u1_ref, u2_ref, sw_ref, sb_ref, cs_ref, kr_ref, ks_ref, kny_ref, bias_ref, o_ref,
                  *, seq_len):
    n, cb = u0_ref.shape
    n_seq = n // seq_len
    t = lax.broadcasted_iota(jnp.int32, (n, cb), 0) & (seq_len - 1)

    def short_conv(u_ref, k):
        u = u_ref[...]
        w = sw_ref[:, k * cb:(k + 1) * cb]
        return (sb_ref[:, k * cb:(k + 1) * cb] + w[0:1] * _shift_rows(u, 1, t, seq_len) + w[1:2] * u
                + w[2:3] * _shift_rows(u, -1, t, seq_len))

    x0 = short_conv(u0_ref, 0)
    z = short_conv(u1_ref, 1) * short_conv(u2_ref, 2)
    wide = lambda a: jnp.concatenate([a[q * seq_len:(q + 1) * seq_len] for q in range(n_seq)], axis=1)
    rep = lambda a: jnp.concatenate([a] * n_seq, axis=1)
    zw = wide(z)
    zb = zw.astype(BF16)
    c, s = cs_ref[0], cs_ref[1]
    ur = jnp.dot(c, zb, preferred_element_type=F32)
    us = jnp.dot(s, zb, preferred_element_type=F32)
    sgn = jnp.where((lax.broadcasted_iota(jnp.int32, zw.shape, 0) & 1) == 1, -1.0, 1.0)
    uny = jnp.sum(sgn * zw, axis=0, keepdims=True)
    kr, ks = rep(kr_ref[...]), rep(ks_ref[...])
    yr = (ur * kr - us * ks).astype(BF16)
    ys = (ur * ks + us * kr).astype(BF16)
    yw = jnp.dot(c, yr, preferred_element_type=F32) + jnp.dot(s, ys, preferred_element_type=F32)
    yw = yw + sgn * (uny * rep(kny_ref[...]))
    y = jnp.concatenate([yw[:, q * cb:(q + 1) * cb] for q in range(n_seq)], axis=0)
    o_ref[...] = (x0 * (y + bias_ref[...] * z)).astype(o_ref.dtype)


def hyena(uh, seq_len, short_w, short_b, cs, kr, ks, kny, bias):
    tokens = uh.shape[0]
    cb = HY_CB
    nc = HY_W // cb
    rows = max(seq_len, HY_ROWS)
    assert seq_len & (seq_len - 1) == 0 and rows % seq_len == 0
    ublk = lambda k: pl.BlockSpec((rows, cb), lambda s, c: (s, k * nc + c))
    chan = lambda r: pl.BlockSpec((r, cb), lambda s, c: (0, c))
    return pl.pallas_call(
        functools.partial(_hyena_kernel, seq_len=seq_len),
        out_shape=jax.ShapeDtypeStruct((tokens, HY_W), BF16),
        grid=(tokens // rows, nc),
        in_specs=[ublk(0), ublk(1), ublk(2),
                  pl.BlockSpec((None, 3, 3 * cb), lambda s, c: (c, 0, 0)),
                  pl.BlockSpec((None, 1, 3 * cb), lambda s, c: (c, 0, 0)),
                  pl.BlockSpec((2, seq_len, seq_len), lambda s, c: (0, 0, 0)),
                  chan(seq_len), chan(seq_len), chan(1), chan(1)],
        out_specs=pl.BlockSpec((rows, cb), lambda s, c: (s, c)),
        compiler_params=_cparams(("parallel", "parallel")),
        name="hyena",
    )(uh, uh, uh, short_w, short_b, cs, kr, ks, kny, bias)


META_E1, META_E2, META_R1, META_R2, META_G1, META_G2 = range(6)


def _route_kernel(p0_ref, p1_ref, p2_ref, wo_ref, x_ref, g1_ref, g_ref, sh_ref, sc_ref, wr_ref, br_ref,
                  x1_ref, h_ref, meta_ref, meta_t_ref, cnt_ref, run_sc, wo_sc):
    tm = x_ref.shape[0]
    lane = lax.broadcasted_iota(jnp.int32, (tm, V7X_LANES), 1)

    @pl.when(pl.program_id(0) == 0)
    def _():
        run_sc[...] = jnp.zeros_like(run_sc)
        wo_sc[...] = wo_ref[...].astype(BF16)

    kb = p0_ref.shape[1]
    m = jnp.dot(p0_ref[...], wo_sc[0:kb, :], preferred_element_type=F32)
    m += jnp.dot(p1_ref[...], wo_sc[kb:2 * kb, :], preferred_element_type=F32)
    m += jnp.dot(p2_ref[...], wo_sc[2 * kb:3 * kb, :], preferred_element_type=F32)
    x1 = x_ref[...] + g1_ref[...] * m
    x1_ref[...] = x1
    h = _norm_mod(x1, g_ref[...], sh_ref[...], sc_ref[...])
    h_ref[...] = h
    logits = _dot3(h, wr_ref[...]) + br_ref[...]
    valid = lane < N_EXPERTS
    lg = jnp.where(valid, logits, -jnp.inf)
    ex = jnp.exp(lg - jnp.max(lg, axis=-1, keepdims=True))
    p = ex / jnp.sum(ex, axis=-1, keepdims=True)
    p1 = jnp.max(p, axis=-1, keepdims=True)
    i1 = jnp.min(jnp.where((p == p1) & valid, lane, V7X_LANES), axis=-1, keepdims=True)
    rest = jnp.where((lane == i1) | (~valid), -1.0, p)
    p2 = jnp.max(rest, axis=-1, keepdims=True)
    i2 = jnp.min(jnp.where(rest == p2, lane, V7X_LANES), axis=-1, keepdims=True)
    m1 = lane == i1
    m2 = lane == i2
    chosen = jnp.where(m1 | m2, 1.0, 0.0)
    r = lax.broadcasted_iota(jnp.int32, (tm, tm), 0)
    c = lax.broadcasted_iota(jnp.int32, (tm, tm), 1)
    tri = jnp.where(c < r, 1.0, 0.0).astype(BF16)
    before = jnp.dot(tri, chosen.astype(BF16), preferred_element_type=F32) + run_sc[0:1, :]
    rank1 = jnp.sum(jnp.where(m1, before, 0.0), axis=-1, keepdims=True)
    rank2 = jnp.sum(jnp.where(m2, before, 0.0), axis=-1, keepdims=True)
    inv = 1.0 / (p1 + p2)
    vals = (i1.astype(F32), i2.astype(F32), rank1, rank2, p1 * inv, p2 * inv)
    meta = jnp.zeros((tm, V7X_LANES), F32)
    for k, v in enumerate(vals):
        meta = jnp.where(lane == k, v, meta)
    meta_ref[...] = meta
    meta_t_ref[...] = meta.T[:V7X_SUBLANES]
    run_sc[...] = run_sc[...] + jnp.sum(chosen, axis=0, keepdims=True)
    cnt_ref[...] = run_sc[...]


def mix_route(parts, w_out, x, g, modtab, cond, wr_pad, br_pad):
    tokens = x.shape[0]
    tm = TM_ROUTE
    kb = 512
    zero = lambda i: 0
    const = lambda i: (0, 0)
    rows = lambda w: pl.BlockSpec((tm, w), lambda i: (i, 0))
    lhs_specs = [pl.BlockSpec((tm, kb), (lambda i, cbk=cbk: (i, cbk))) for _, cbk in parts]
    return pl.pallas_call(
        _route_kernel,
        out_shape=(jax.ShapeDtypeStruct((tokens, D), F32),
                   jax.ShapeDtypeStruct((tokens, D), F32),
                   jax.ShapeDtypeStruct((tokens, V7X_LANES), F32),
                   jax.ShapeDtypeStruct((V7X_SUBLANES, tokens), F32),
                   jax.ShapeDtypeStruct((V7X_SUBLANES, V7X_LANES), F32)),
        grid=(tokens // tm,),
        in_specs=lhs_specs + [
            pl.BlockSpec((len(parts) * kb, D), const, pipeline_mode=pl.Buffered(1)),
            rows(D),
            _mod_spec(2, cond, tm, D, zero),
            pl.BlockSpec((1, D), const),
            _mod_spec(3, cond, tm, D, zero),
            _mod_spec(4, cond, tm, D, zero),
            pl.BlockSpec((D, V7X_LANES), const),
            pl.BlockSpec((1, V7X_LANES), const)],
        out_specs=(rows(D), rows(D), rows(V7X_LANES),
                   pl.BlockSpec((V7X_SUBLANES, tm), lambda i: (0, i)),
                   pl.BlockSpec((V7X_SUBLANES, V7X_LANES), const)),
        scratch_shapes=[pltpu.VMEM((V7X_SUBLANES, V7X_LANES), F32), pltpu.VMEM((len(parts) * kb, D), BF16)],
        compiler_params=_cparams(("arbitrary",)),
        name="mix_route",
    )(*[a for a, _ in parts], w_out, x, modtab, g.reshape(1, D), modtab, modtab, wr_pad, br_pad)


def _row_copy(src_ref, src_row, dst_ref, dst_row, sem):
    return pltpu.make_async_copy(src_ref.at[pl.ds(src_row, 1)], dst_ref.at[pl.ds(dst_row, 1)], sem)


def _dispatch_kernel(pos_ref, h_ref, hs_in_ref, hs_ref, sem):
    del hs_in_ref
    tm = h_ref.shape[0]
    n_tok = pos_ref.shape[0] // 2
    base = pl.program_id(0) * tm

    def issue(r, carry):
        _row_copy(h_ref, r, hs_ref, pos_ref[base + r], sem).start(priority=0)
        _row_copy(h_ref, r, hs_ref, pos_ref[n_tok + base + r], sem).start(priority=1)
        return carry

    lax.fori_loop(0, tm, issue, 0, unroll=8)
    for _ in range(2):
        pltpu.make_async_copy(h_ref, hs_ref.at[pl.ds(0, tm)], sem).wait()


def moe_dispatch(pos, h, hs):
    tokens = h.shape[0]
    tm = TM_ROUTE
    return pl.pallas_call(
        _dispatch_kernel,
        out_shape=jax.ShapeDtypeStruct(hs.shape, hs.dtype),
        grid_spec=pltpu.PrefetchScalarGridSpec(
            num_scalar_prefetch=1,
            grid=(tokens // tm,),
            in_specs=[pl.BlockSpec((tm, D), lambda i, pos: (i, 0)),
                      pl.BlockSpec(memory_space=pl.ANY)],
            out_specs=pl.BlockSpec(memory_space=pl.ANY),
            scratch_shapes=[pltpu.SemaphoreType.DMA(())]),
        input_output_aliases={2: 0},
        compiler_params=_cparams(("arbitrary",)),
        name="moe_dispatch",
    )(pos, h, hs)


def _experts_kernel(te_ref, sg_ref, su_ref, sd_ref, nu_ref, hs_ref, wg_ref, wu_ref, wd_ref, y_ref,
                    wg_sc, wu_sc, wd_sc):
    del sg_ref, su_ref, sd_ref
    j = pl.program_id(0)
    e = te_ref[j]
    e_prev = te_ref[jnp.maximum(j - 1, 0)]

    @pl.when((j == 0) | (e != e_prev))
    def _():
        wg_sc[...] = wg_ref[...].astype(BF16)
        wu_sc[...] = wu_ref[...].astype(BF16)
        wd_sc[...] = wd_ref[...].astype(BF16)

    @pl.when(j < nu_ref[0])
    def _():
        h = hs_ref[...].astype(BF16)
        y = None
        for c0 in range(0, D_FF_EXPERT, MOE_CHUNK):
            c1 = min(c0 + MOE_CHUNK, D_FF_EXPERT)
            hg = jnp.dot(h, wg_sc[:, c0:c1], preferred_element_type=F32)
            hu = jnp.dot(h, wu_sc[:, c0:c1], preferred_element_type=F32)
            act = (_silu(hg) * hu).astype(BF16)
            yc = jnp.dot(act, wd_sc[c0:c1, :], preferred_element_type=F32)
            y = yc if y is None else y + yc
        y_ref[...] = y

    @pl.when(j >= nu_ref[0])
    def _():
        y_ref[...] = jnp.zeros_like(y_ref)


def moe_experts(tile_expert, stages, n_used, hs, e_gate, e_up, e_down):
    rows = hs.shape[0]
    tmr = TM_EXPERT
    wspec = lambda shape, k: pl.BlockSpec((None,) + shape, lambda j, *pf: (pf[1 + k][j], 0, 0))
    return pl.pallas_call(
        _experts_kernel,
        out_shape=jax.ShapeDtypeStruct((rows, D), F32),
        grid_spec=pltpu.PrefetchScalarGridSpec(
            num_scalar_prefetch=5,
            grid=(rows // tmr,),
            in_specs=[pl.BlockSpec((tmr, D), lambda j, *pf: (j, 0)),
                      wspec((D, D_FF_EXPERT), 0), wspec((D, D_FF_EXPERT), 1), wspec((D_FF_EXPERT, D), 2)],
            out_specs=pl.BlockSpec((tmr, D), lambda j, *pf: (j, 0)),
            scratch_shapes=[pltpu.VMEM((D, D_FF_EXPERT), BF16), pltpu.VMEM((D, D_FF_EXPERT), BF16),
                            pltpu.VMEM((D_FF_EXPERT, D), BF16)]),
        compiler_params=_cparams(("arbitrary",)),
        name="moe_experts",
    )(tile_expert, *stages, n_used, hs, e_gate, e_up, e_down)


def _combine_kernel(pos_ref, x_ref, meta_ref, gt_ref, fg_ref, y_ref, o_ref, b1_sc, b2_sc, sem):
    tm = x_ref.shape[0]
    n_tok = pos_ref.shape[0] // 2
    i = pl.program_id(0)

    def gather(tile, slot):
        base = tile * tm

        def issue(r, carry):
            _row_copy(y_ref, pos_ref[base + r], b1_sc.at[slot], r, sem.at[slot]).start(priority=0)
            _row_copy(y_ref, pos_ref[n_tok + base + r], b2_sc.at[slot], r, sem.at[slot]).start(priority=1)
            return carry

        lax.fori_loop(0, tm, issue, 0, unroll=8)

    @pl.when(i == 0)
    def _():
        gather(0, 0)

    @pl.when(i + 1 < pl.num_programs(0))
    def _():
        gather(i + 1, (i + 1) % 2)

    slot = i % 2
    pltpu.make_async_copy(y_ref.at[pl.ds(0, tm)], b1_sc.at[slot], sem.at[slot]).wait()
    pltpu.make_async_copy(y_ref.at[pl.ds(0, tm)], b2_sc.at[slot], sem.at[slot]).wait()

    meta = meta_ref[...]
    lane = lax.broadcasted_iota(jnp.int32, meta.shape, 1)
    g1 = jnp.sum(jnp.where(lane == META_G1, meta, 0.0), axis=-1, keepdims=True)
    g2 = jnp.sum(jnp.where(lane == META_G2, meta, 0.0), axis=-1, keepdims=True)
    x = x_ref[...] + gt_ref[...] * (g1 * b1_sc[slot] + g2 * b2_sc[slot])
    o_ref[...] = _rms(x, fg_ref[...])


def moe_combine(pos, x, meta, modtab, cond, final_g, y):
    tokens = x.shape[0]
    tm = TM_COMBINE
    return pl.pallas_call(
        _combine_kernel,
        out_shape=jax.ShapeDtypeStruct((tokens, D), F32),
        grid_spec=pltpu.PrefetchScalarGridSpec(
            num_scalar_prefetch=1,
            grid=(tokens // tm,),
            in_specs=[pl.BlockSpec((tm, D), lambda i, pos: (i, 0)),
                      pl.BlockSpec((tm, V7X_LANES), lambda i, pos: (i, 0)),
                      _mod_spec(5, cond, tm, D, lambda i, pos: 0),
                      pl.BlockSpec((1, D), lambda i, pos: (0, 0)),
                      pl.BlockSpec(memory_space=pl.ANY)],
            out_specs=pl.BlockSpec((tm, D), lambda i, pos: (i, 0)),
            scratch_shapes=[pltpu.VMEM((2, tm, D), F32), pltpu.VMEM((2, tm, D), F32),
                            pltpu.SemaphoreType.DMA((2,))]),
        compiler_params=_cparams(("arbitrary",)),
        name="moe_combine",
    )(pos, x, meta, modtab, final_g.reshape(1, D), y)


def moe_plan(metas, counts):
    tmr = TM_EXPERT
    cnts = [c[0, :N_EXPERTS].astype(jnp.int32) for c in counts]
    total = functools.reduce(jnp.add, cnts)
    padded = ((total + tmr - 1) // tmr) * tmr
    ends = jnp.cumsum(padded)
    starts = ends - padded
    n_rows = sum(m.shape[1] for m in metas) * 2 + N_EXPERTS * tmr
    n_tiles = n_rows // tmr
    tile_start = jnp.arange(n_tiles, dtype=jnp.int32) * tmr
    tile_expert = jnp.minimum(jnp.sum(tile_start[:, None] >= ends[None, :], axis=1), N_EXPERTS - 1).astype(jnp.int32)
    n_used = (ends[-1] // tmr).astype(jnp.int32).reshape(1)
    eid = jnp.arange(N_EXPERTS, dtype=jnp.int32)
    later = jnp.where((eid[None, :] > eid[:, None]) & (padded[None, :] > 0), eid[None, :], N_EXPERTS)
    nxt = jnp.min(later, axis=1)
    next_used = jnp.where(nxt == N_EXPERTS, eid, nxt)
    pick = lambda table: jnp.sum(jnp.where(tile_expert[:, None] == eid[None, :], table[None, :], 0), axis=1)
    k_in_group = (tile_start - pick(starts)) // tmr
    tile_next = pick(next_used)
    stages = [jnp.where(k_in_group < k, tile_expert, tile_next).astype(jnp.int32) for k in (1, 2, 3)]
    pos = []
    base = jnp.zeros((N_EXPERTS,), jnp.int32)
    for m, c in zip(metas, cnts):
        first = starts + base
        sel = lambda field: m[field].astype(jnp.int32)
        lookup = lambda e: jnp.sum(jnp.where(e[:, None] == jnp.arange(N_EXPERTS)[None, :], first[None, :], 0), axis=1)
        p1 = lookup(sel(META_E1)) + sel(META_R1)
        p2 = lookup(sel(META_E2)) + sel(META_R2)
        pos.append(jnp.concatenate([p1, p2]).astype(jnp.int32))
        base = base + c
    return pos, tile_expert, stages, n_used, n_rows


def _pad_to(a, shape):
    return jnp.pad(a, [(0, t - s) for s, t in zip(a.shape, shape)])


def _regroup_chunks(a, cb):
    r = a.shape[0]
    return a.reshape(r, 3, HY_W // cb, cb).transpose(2, 0, 1, 3).reshape(HY_W // cb, r, 3 * cb)


def kernel(x_prompt, x_sample, state_l0_lru, cache_l1_ckv, cache_l1_krope, c, c_ctx, l0_norm1, l0_norm2, l0_w_mod, l0_b_mod, l0_w_in, l0_conv_a, l0_lru_conv_w, l0_lru_conv_b, l0_lru_wa, l0_lru_ba, l0_lru_wi, l0_lru_bi, l0_lru_lambda, l0_w_out, l0_ffn_gate, l0_ffn_up, l0_ffn_down, l1_norm1, l1_norm2, l1_w_mod, l1_b_mod, l1_w_in, l1_q_norm, l1_kv_norm, l1_w_q_up, l1_w_kv_up, l1_hy_short_w, l1_hy_short_b, l1_hy_f_w1, l1_hy_f_b1, l1_hy_f_w2, l1_hy_f_b2, l1_hy_f_w3, l1_hy_bias, l1_w_out, l1_router_w, l1_router_b, l1_exp_gate, l1_exp_up, l1_exp_down, final_norm):
    batch, seq, _ = x_prompt.shape
    dec_batch, dec_seq, _ = x_sample.shape
    past_len = cache_l1_ckv.shape[1]

    cond8 = jnp.concatenate([c_ctx[None, :], c, jnp.zeros((V7X_SUBLANES - 1 - dec_batch, D), F32)], axis=0)
    wcat = jnp.concatenate([l0_lru_wa[0], l0_lru_wi[0], l0_lru_wa[1], l0_lru_wi[1]], axis=-1)
    hid = V7X_LANES
    w1p = _pad_to(l1_hy_f_w1, (hid, hid))
    b1p = _pad_to(l1_hy_f_b1.reshape(1, -1), (1, hid))
    w2p = _pad_to(l1_hy_f_w2, (hid, hid))
    b2p = _pad_to(l1_hy_f_b2.reshape(1, -1), (1, hid))
    w3p = _pad_to(l1_hy_f_w3, (hid, 2 * HY_W))
    short_w = _regroup_chunks(l1_hy_short_w, HY_CB)
    short_b = _regroup_chunks(l1_hy_short_b.reshape(1, -1), HY_CB)
    hy_bias = l1_hy_bias.reshape(1, HY_W)
    wr_pad = _pad_to(l1_router_w, (D, V7X_LANES))
    br_pad = _pad_to(l1_router_b.reshape(1, -1), (1, V7X_LANES))

    mod0 = adaln_table(cond8, l0_w_mod, l0_b_mod)
    mod1 = adaln_table(cond8, l1_w_mod, l1_b_mod)

    kv_ctx = kv_up(cache_l1_ckv.reshape(dec_batch * past_len, KV_RANK), l1_w_kv_up)
    kr_ctx = cache_l1_krope.reshape(dec_batch * past_len, ROPE)

    def trunk(x, seq_len, cond, h0, latent):
        u = in0_proj(x, l0_norm1, mod0, cond, l0_w_in)
        ya = conv_a(u, seq_len, l0_conv_a)
        yb, lru_state = rglru(u, seq_len, l0_lru_conv_w, l0_lru_conv_b, wcat, l0_lru_ba, l0_lru_bi,
                              l0_lru_lambda, h0)
        x = mix_ffn([(ya, 0), (yb, 0), (yb, 1)], l0_w_out, x, l0_norm2, mod0, cond,
                    l0_ffn_gate, l0_ffn_up, l0_ffn_down)
        qnope, qpe, ckv, kr, kv, uh = in1_proj(x, l1_norm1, mod1, cond, l1_w_in, l1_q_norm, l1_kv_norm,
                                               l1_w_q_up, l1_w_kv_up)
        if latent:
            yc = attn_lat(qnope, qpe, kv_ctx, kr_ctx, kv, kr, seq_len, past_len)
        else:
            yc = attn_ctx(qnope, qpe, kv, kr, seq_len)
        cs = dft_tables(seq_len)
        k_r, k_s, k_ny = hy_filter(cs, w1p, b1p, w2p, b2p, w3p)
        yd = hyena(uh, seq_len, short_w, short_b, cs, k_r, k_s, k_ny, hy_bias)
        routed = mix_route([(yc, 0), (yc, 1), (yd, 0)], l1_w_out, x, l1_norm2, mod1, cond, wr_pad, br_pad)
        return routed, lru_state, ckv, kr

    conds = ((0, batch * seq), (1, dec_seq))
    zeros_state = jnp.zeros((batch, 2, LRU_W), F32)
    r_p, new_lru, new_ckv, new_kr = trunk(x_prompt.reshape(batch * seq, D), seq, conds[0], zeros_state, latent=False)
    r_s, _, _, _ = trunk(x_sample.reshape(dec_batch * dec_seq, D), dec_seq, conds[1], state_l0_lru, latent=True)

    routed = (r_p, r_s)
    pos, tile_expert, stages, n_used, n_rows = moe_plan([r[3] for r in routed], [r[4] for r in routed])
    hs = jnp.zeros((n_rows, D), F32)
    for p, r in zip(pos, routed):
        hs = moe_dispatch(p, r[1], hs)
    y_rows = moe_experts(tile_expert, stages, n_used, hs, l1_exp_gate, l1_exp_up, l1_exp_down)
    y_p, y_s = [moe_combine(p, r[0], r[2], mod1, cond, final_norm, y_rows)
                for p, r, cond in zip(pos, routed, conds)]
    return (y_p.reshape(batch, seq, D), y_s.reshape(dec_batch, dec_seq, D), new_lru,
            new_ckv.reshape(batch, seq, KV_RANK), new_kr.reshape(batch, seq, ROPE))
```

```python
import functools
import math

import jax
import jax.numpy as jnp
from jax import lax
from jax.experimental import pallas as pl
from jax.experimental.pallas import tpu as pltpu

F32 = jnp.float32
BF16 = jnp.bfloat16
HIGHEST = lax.Precision.HIGHEST

D = 1024
GRID_W = 64
EPS = 1e-6
CONV_W = 512
LRU_W = 1024
LRU_BW = 128
LRU_C = 8.0
MLA_HEADS = 8
Q_RANK = 384
KV_RANK = 256
NOPE = 128
ROPE = 64
VDIM = 128
QK_DIM = NOPE + ROPE
ROPE_THETA = 10000.0
HY_W = 512
HY_BANDS = 16
HY_TARGET = 1e-2
HY_FAST_DECAY = 0.3
HY_SLOW_DECAY = 1.5
D_FF = 2816
N_EXPERTS = 8
D_FF_EXPERT = 1408
IN0 = 3 * CONV_W + 2 * LRU_W
IN1 = Q_RANK + KV_RANK + ROPE + 3 * HY_W

V7X_LANES = 128
V7X_SUBLANES = 8
V7X_VMEM_LIMIT_BYTES = 56 * 1024 * 1024
V7X_VMEM_LIMIT_LARGE_BYTES = 60 * 1024 * 1024

TM = 512
TN_IN0 = 512
TF_FFN = 256
MOE_CHUNK = 256
TM_ROUTE = 512
TM_EXPERT = 512
TM_COMBINE = 512
LRU_CB = 256
HY_CB = 256
TQ = 256
ATTN_CTX_SEQS = 4
CONV_A_ROWS = 1024
LRU_ROWS = 1024
HY_ROWS = 1024
TM_IN1 = 512


def _cparams(sem, vmem_limit_bytes=V7X_VMEM_LIMIT_BYTES):
    return pltpu.CompilerParams(dimension_semantics=sem, vmem_limit_bytes=vmem_limit_bytes)


def _sigmoid(x):
    return 0.5 * jnp.tanh(0.5 * x) + 0.5


def _silu(x):
    return x * _sigmoid(x)


def _norm_mod(x, g, shift, scale):
    ms = jnp.mean(x * x, axis=-1, keepdims=True)
    y = x * lax.rsqrt(ms + EPS) * g
    return y * (1.0 + scale) + shift


def _mod_spec(comp, cond, tm, width, col_fn, tile_fn=lambda *ids: ids[0]):
    row0, seg = cond
    assert seg % tm == 0
    return pl.BlockSpec((None, 1, width),
                        lambda *ids: (comp * 3 + row0 + (tile_fn(*ids) * tm) // seg, 0, col_fn(*ids)))


def _dot3(a, b):
    a_hi = a.astype(BF16)
    a_lo = (a - a_hi.astype(F32)).astype(BF16)
    b_hi = b.astype(BF16)
    b_lo = (b - b_hi.astype(F32)).astype(BF16)
    n = a.shape[0]
    y = jnp.dot(jnp.concatenate([a_hi, a_lo], axis=0), b_hi, preferred_element_type=F32)
    return y[:n] + y[n:] + jnp.dot(a_hi, b_lo, preferred_element_type=F32)


def _adaln_kernel(c_ref, w_ref, b_ref, o_ref):
    o_ref[...] = _dot3(_silu(c_ref[...]), w_ref[...]) + b_ref[...]


def adaln_table(cond8, w_mod, b_mod):
    tn = 1536
    m = pl.pallas_call(
        _adaln_kernel,
        out_shape=jax.ShapeDtypeStruct((V7X_SUBLANES, 6 * D), F32),
        grid=(6 * D // tn,),
        in_specs=[pl.BlockSpec((V7X_SUBLANES, D), lambda j: (0, 0)),
                  pl.BlockSpec((D, tn), lambda j: (0, j)),
                  pl.BlockSpec((1, tn), lambda j: (0, j))],
        out_specs=pl.BlockSpec((V7X_SUBLANES, tn), lambda j: (0, j)),
        compiler_params=_cparams(("arbitrary",)),
        name="adaln",
    )(cond8, w_mod, b_mod.reshape(1, 6 * D))
    return m[:3].reshape(3, 6, D).transpose(1, 0, 2).reshape(18, 1, D)


def _tile_of(n_load):
    return lambda s: jnp.maximum(s - n_load, 0)


def _block_of(n_load):
    return lambda s: jnp.minimum(s, n_load - 1)


class _TwoSets:
    def __init__(self, n_load, tm, tokens, conds):
        self.n_load, self.tm, self.conds = n_load, tm, conds
        self.n_a, self.n_b = tokens[0] // tm, tokens[1] // tm
        self.steps = n_load + self.n_a + self.n_b

    def tile(self, s):
        return jnp.maximum(s - self.n_load, 0)

    def in_first(self, s):
        return s - self.n_load < self.n_a

    def idx_a(self, s):
        return jnp.minimum(self.tile(s), self.n_a - 1)

    def idx_b(self, s):
        return jnp.clip(self.tile(s) - self.n_a, 0, self.n_b - 1)

    def rows(self, width):
        return (pl.BlockSpec((self.tm, width), lambda s: (self.idx_a(s), 0)),
                pl.BlockSpec((self.tm, width), lambda s: (self.idx_b(s), 0)))

    def cols(self, width, col):
        return (pl.BlockSpec((self.tm, width), lambda s: (self.idx_a(s), col)),
                pl.BlockSpec((self.tm, width), lambda s: (self.idx_b(s), col)))

    def mod_spec(self, comp):
        (row_a, seg_a), (row_b, seg_b) = self.conds
        assert seg_a % self.tm == 0 and seg_b % self.tm == 0

        def row(s):
            return jnp.where(self.in_first(s), row_a + (self.idx_a(s) * self.tm) // seg_a,
                             row_b + (self.idx_b(s) * self.tm) // seg_b)

        return pl.BlockSpec((None, 1, D), lambda s: (comp * 3 + row(s), 0, 0))


def _in0_kernel(xa_ref, xb_ref, g_ref, sh_ref, sc_ref, w_ref, oa_ref, ob_ref, w_sc, *, n_a):
    s = pl.program_id(0)
    n_load = w_sc.shape[0]

    @pl.when(s < n_load)
    def _():
        w_sc[s] = w_ref[...].astype(BF16)

    @pl.when(s >= n_load)
    def _():
        first = s - n_load < n_a
        x = jnp.where(first, xa_ref[...], xb_ref[...])
        h = _norm_mod(x, g_ref[...], sh_ref[...], sc_ref[...]).astype(BF16)
        u = jnp.concatenate([jnp.dot(h, w_sc[j], preferred_element_type=F32).astype(BF16)
                             for j in range(n_load)], axis=1)

        @pl.when(first)
        def _():
            oa_ref[...] = u

        @pl.when(jnp.logical_not(first))
        def _():
            ob_ref[...] = u


def in0_proj(xs, g, modtab, conds, w_in):
    tn = TN_IN0
    n = w_in.shape[1]
    n_load = n // tn
    ts = _TwoSets(n_load, TM, [x.shape[0] for x in xs], conds)
    blk = _block_of(n_load)
    return pl.pallas_call(
        functools.partial(_in0_kernel, n_a=ts.n_a),
        out_shape=tuple(jax.ShapeDtypeStruct((x.shape[0], n), BF16) for x in xs),
        grid=(ts.steps,),
        in_specs=[*ts.rows(D),
                  pl.BlockSpec((1, D), lambda s: (0, 0)),
                  ts.mod_spec(0), ts.mod_spec(1),
                  pl.BlockSpec((D, tn), lambda s: (0, blk(s)))],
        out_specs=ts.rows(n),
        scratch_shapes=[pltpu.VMEM((n_load, D, tn), BF16)],
        compiler_params=_cparams(("arbitrary",)),
        name="in0_proj",
    )(*xs, g.reshape(1, D), modtab, modtab, w_in)


def _shift_rows(v, d, t, seq_len=None):
    n = v.shape[0]
    seq_len = n if seq_len is None else seq_len
    if d > 0:
        return jnp.where(t < d, 0.0, pltpu.roll(v, d, 0))
    return jnp.where(t >= seq_len + d, 0.0, pltpu.roll(v, n + d, 0))


def _conv_a_kernel(b_ref, c_ref, x_ref, w_ref, o_ref, *, seq_len):
    v = c_ref[...].astype(F32) * x_ref[...].astype(F32)
    t = lax.broadcasted_iota(jnp.int32, v.shape, 0) & (seq_len - 1)
    w = w_ref[...]
    y = w[0:1] * _shift_rows(v, 1, t, seq_len) + w[1:2] * v + w[2:3] * _shift_rows(v, -1, t, seq_len)
    o_ref[...] = (b_ref[...].astype(F32) * y).astype(o_ref.dtype)


def conv_a(u, seq_len, conv_w):
    tokens = u.shape[0]
    rows = max(seq_len, CONV_A_ROWS)
    assert seq_len & (seq_len - 1) == 0 and rows % seq_len == 0
    return pl.pallas_call(
        functools.partial(_conv_a_kernel, seq_len=seq_len),
        out_shape=jax.ShapeDtypeStruct((tokens, CONV_W), BF16),
        grid=(tokens // rows,),
        in_specs=[pl.BlockSpec((rows, CONV_W), lambda s: (s, 0)),
                  pl.BlockSpec((rows, CONV_W), lambda s: (s, 1)),
                  pl.BlockSpec((rows, CONV_W), lambda s: (s, 2)),
                  pl.BlockSpec((3, CONV_W), lambda s: (0, 0))],
        out_specs=pl.BlockSpec((rows, CONV_W), lambda s: (s, 0)),
        compiler_params=_cparams(("parallel",)),
        name="conv_a",
    )(u, u, u, conv_w)


def _group_scan(a, b, reverse):
    n, c = a.shape
    a3 = a.reshape(n // V7X_SUBLANES, V7X_SUBLANES, c)
    b3 = b.reshape(n // V7X_SUBLANES, V7X_SUBLANES, c)
    t8 = lax.broadcasted_iota(jnp.int32, a3.shape, 1)
    for d in (1, 2, 4):
        if reverse:
            keep = t8 < V7X_SUBLANES - d
            shift = V7X_SUBLANES - d
        else:
            keep = t8 >= d
            shift = d
        a_sh = jnp.where(keep, pltpu.roll(a3, shift, 1), 1.0)
        b_sh = jnp.where(keep, pltpu.roll(b3, shift, 1), 0.0)
        b3 = a3 * b_sh + b3
        a3 = a3 * a_sh
    return a3.reshape(n, c), b3.reshape(n, c)


def _rglru_kernel(gate_ref, xb_ref, cw_ref, cb_ref, wcat_ref, ba_ref, bi_ref, lam_ref, h0_ref,
                  y_ref, st_ref, af_sc, bf_sc, ab_sc, bb_sc, hf_sc, hb_sc, *, seq_len):
    n, cb = xb_ref.shape
    n_seq = n // seq_len
    xb = xb_ref[...].astype(F32)
    t = lax.broadcasted_iota(jnp.int32, xb.shape, 0) & (seq_len - 1)
    cw = cw_ref[...]
    sh = lambda d: _shift_rows(xb, d, t, seq_len)
    xc = cb_ref[...] + cw[0:1] * sh(2) + cw[1:2] * sh(1) + cw[2:3] * xb + cw[3:4] * sh(-1)
    xcb = xc.astype(BF16)
    g = [jnp.dot(xcb[:, k * LRU_BW:(k + 1) * LRU_BW], wcat_ref[k].astype(BF16), preferred_element_type=F32)
         for k in range(cb // LRU_BW)]

    def direction(d):
        ga = jnp.concatenate([gk[:, (2 * d) * LRU_BW:(2 * d + 1) * LRU_BW] for gk in g], axis=1)
        gi = jnp.concatenate([gk[:, (2 * d + 1) * LRU_BW:(2 * d + 2) * LRU_BW] for gk in g], axis=1)
        r = _sigmoid(ga + ba_ref[d:d + 1, :])
        i = _sigmoid(gi + bi_ref[d:d + 1, :])
        log_a = (-LRU_C * jax.nn.softplus(-lam_ref[d:d + 1, :])) * r
        a = jnp.exp(log_a)
        m = 1.0 - a * a
        mult = m * lax.rsqrt(jnp.maximum(m, 1e-30))
        return a, mult * (i * xc)

    a_f, b_f = direction(0)
    a_f, b_f = _group_scan(a_f, b_f, reverse=False)
    af_sc[...] = a_f
    bf_sc[...] = b_f
    a_b, b_b = direction(1)
    a_b, b_b = _group_scan(a_b, b_b, reverse=True)
    ab_sc[...] = a_b
    bb_sc[...] = b_b

    ng = seq_len // V7X_SUBLANES
    bcast = lambda row: jnp.broadcast_to(row, (V7X_SUBLANES, cb))
    init = tuple((bcast(h0_ref[q, 0:1, :]), bcast(h0_ref[q, 1:2, :])) for q in range(n_seq))

    def step(k, carry):
        out = []
        for q, (hf_in, hb_in) in enumerate(carry):
            rf = pl.multiple_of(q * seq_len + k * V7X_SUBLANES, V7X_SUBLANES)
            rb = pl.multiple_of(q * seq_len + (ng - 1 - k) * V7X_SUBLANES, V7X_SUBLANES)
            hf = af_sc[pl.ds(rf, V7X_SUBLANES), :] * hf_in + bf_sc[pl.ds(rf, V7X_SUBLANES), :]
            hb = ab_sc[pl.ds(rb, V7X_SUBLANES), :] * hb_in + bb_sc[pl.ds(rb, V7X_SUBLANES), :]
            hf_sc[pl.ds(rf, V7X_SUBLANES), :] = hf
            hb_sc[pl.ds(rb, V7X_SUBLANES), :] = hb
            out.append((bcast(hf[V7X_SUBLANES - 1:V7X_SUBLANES]), bcast(hb[0:1])))
        return tuple(out)

    final = lax.fori_loop(0, ng, step, init)
    for q, (hf_last, hb_first) in enumerate(final):
        st_ref[q, 0:1, :] = hf_last[0:1]
        st_ref[q, 1:2, :] = hb_first[0:1]

    gt = gate_ref[...].astype(F32)
    gelu = 0.5 * gt * (1.0 + jnp.tanh(math.sqrt(2.0 / math.pi) * (gt + 0.044715 * (gt * gt * gt))))
    y_ref[...] = ((hf_sc[...] + hb_sc[...]) * gelu).astype(y_ref.dtype)


def rglru(u, seq_len, conv_w, conv_b, wcat, ba, bi, lam, h0):
    tokens = u.shape[0]
    nseq = tokens // seq_len
    cb = LRU_CB
    rows = max(seq_len, LRU_ROWS)
    assert seq_len & (seq_len - 1) == 0 and rows % seq_len == 0
    per_blk = rows // seq_len
    gate_blk0 = 3 * CONV_W // cb
    xb_blk0 = (3 * CONV_W + LRU_W) // cb
    seq_scr = lambda: pltpu.VMEM((rows, cb), F32)
    return pl.pallas_call(
        functools.partial(_rglru_kernel, seq_len=seq_len),
        out_shape=(jax.ShapeDtypeStruct((tokens, LRU_W), BF16), jax.ShapeDtypeStruct((nseq, 2, LRU_W), F32)),
        grid=(tokens // rows, LRU_W // cb),
        in_specs=[pl.BlockSpec((rows, cb), lambda s, c: (s, gate_blk0 + c)),
                  pl.BlockSpec((rows, cb), lambda s, c: (s, xb_blk0 + c)),
                  pl.BlockSpec((4, cb), lambda s, c: (0, c)),
                  pl.BlockSpec((1, cb), lambda s, c: (0, c)),
                  pl.BlockSpec((cb // LRU_BW, LRU_BW, 4 * LRU_BW), lambda s, c: (c, 0, 0)),
                  pl.BlockSpec((2, cb), lambda s, c: (0, c)),
                  pl.BlockSpec((2, cb), lambda s, c: (0, c)),
                  pl.BlockSpec((2, cb), lambda s, c: (0, c)),
                  pl.BlockSpec((per_blk, 2, cb), lambda s, c: (s, 0, c))],
        out_specs=(pl.BlockSpec((rows, cb), lambda s, c: (s, c)),
                   pl.BlockSpec((per_blk, 2, cb), lambda s, c: (s, 0, c))),
        scratch_shapes=[seq_scr() for _ in range(6)],
        compiler_params=_cparams(("parallel", "parallel")),
        name="rglru",
    )(u, u, conv_w, conv_b.reshape(1, LRU_W), wcat, ba, bi, lam, h0)


def _mix_ffn_kernel(p0a_ref, p0b_ref, p1a_ref, p1b_ref, p2a_ref, p2b_ref, wo_ref, xa_ref, xb_ref,
                    g1_ref, g_ref, sh_ref, sc_ref, g2_ref, wg_ref, wu_ref, wd_ref, oa_ref, ob_ref,
                    wo_sc, wg_sc, wu_sc, wd_sc, *, n_a):
    s = pl.program_id(0)
    n_load = wg_sc.shape[0]
    n_out = wo_sc.shape[0]

    @pl.when(s < n_out)
    def _():
        wo_sc[s] = wo_ref[...].astype(BF16)

    @pl.when(s < n_load)
    def _():
        wg_sc[s] = wg_ref[...].astype(BF16)
        wu_sc[s] = wu_ref[...].astype(BF16)
        wd_sc[s] = wd_ref[...].astype(BF16)

    @pl.when(s >= n_load)
    def _():
        first = s - n_load < n_a
        pick = lambda a_ref, b_ref: jnp.where(first, a_ref[...], b_ref[...])
        m = jnp.dot(pick(p0a_ref, p0b_ref), wo_sc[0], preferred_element_type=F32)
        m += jnp.dot(pick(p1a_ref, p1b_ref), wo_sc[1], preferred_element_type=F32)
        m += jnp.dot(pick(p2a_ref, p2b_ref), wo_sc[2], preferred_element_type=F32)
        x = pick(xa_ref, xb_ref) + g1_ref[...] * m
        h = _norm_mod(x, g_ref[...], sh_ref[...], sc_ref[...]).astype(BF16)
        y = None
        for f in range(n_load):
            hg = jnp.dot(h, wg_sc[f], preferred_element_type=F32)
            hu = jnp.dot(h, wu_sc[f], preferred_element_type=F32)
            act = (_silu(hg) * hu).astype(BF16)
            yf = jnp.dot(act, wd_sc[f], preferred_element_type=F32)
            y = yf if y is None else y + yf
        out = x + g2_ref[...] * y

        @pl.when(first)
        def _():
            oa_ref[...] = out

        @pl.when(jnp.logical_not(first))
        def _():
            ob_ref[...] = out


def mix_ffn(parts, w_out, xs, g, modtab, conds, w_gate, w_up, w_down):
    tf = TF_FFN
    kb = 512
    n_load = D_FF // tf
    n_out = len(parts[0])
    assert n_out <= n_load
    ts = _TwoSets(n_load, TM, [x.shape[0] for x in xs], conds)
    blk = _block_of(n_load)
    oblk = _block_of(n_out)
    lhs_specs, lhs_args = [], []
    for (arr_a, col_a), (arr_b, col_b) in zip(*parts):
        assert col_a == col_b
        lhs_specs += ts.cols(kb, col_a)
        lhs_args += [arr_a, arr_b]
    return pl.pallas_call(
        functools.partial(_mix_ffn_kernel, n_a=ts.n_a),
        out_shape=tuple(jax.ShapeDtypeStruct(x.shape, F32) for x in xs),
        grid=(ts.steps,),
        in_specs=lhs_specs + [
            pl.BlockSpec((kb, D), lambda s: (oblk(s), 0)),
            *ts.rows(D),
            ts.mod_spec(2),
            pl.BlockSpec((1, D), lambda s: (0, 0)),
            ts.mod_spec(3), ts.mod_spec(4), ts.mod_spec(5),
            pl.BlockSpec((D, tf), lambda s: (0, blk(s))),
            pl.BlockSpec((D, tf), lambda s: (0, blk(s))),
            pl.BlockSpec((tf, D), lambda s: (blk(s), 0))],
        out_specs=ts.rows(D),
        scratch_shapes=[pltpu.VMEM((n_out, kb, D), BF16),
                        pltpu.VMEM((n_load, D, tf), BF16), pltpu.VMEM((n_load, D, tf), BF16),
                        pltpu.VMEM((n_load, tf, D), BF16)],
        compiler_params=_cparams(("arbitrary",), V7X_VMEM_LIMIT_LARGE_BYTES),
        name="mix_ffn",
    )(*lhs_args, w_out, *xs, modtab, g.reshape(1, D), modtab, modtab, modtab, w_gate, w_up, w_down)


def _rms(x, g):
    return x * lax.rsqrt(jnp.mean(x * x, axis=-1, keepdims=True) + EPS) * g


def _in1_kernel(x_ref, g_ref, sh_ref, sc_ref, w_ref, qn_ref, kvn_ref, wq_ref, wkv_ref,
                qnope_ref, qpe_ref, ckv_ref, kr_ref, kv_ref, uh_ref, w_sc, wq_sc, wkv_sc):
    @pl.when(pl.program_id(0) == 0)
    def _():
        w_sc[...] = w_ref[...].astype(BF16)
        for h in range(MLA_HEADS):
            c0 = h * QK_DIM
            wq_sc[:, h * NOPE:(h + 1) * NOPE] = wq_ref[:, c0:c0 + NOPE].astype(BF16)
            r0 = MLA_HEADS * NOPE + h * ROPE
            wq_sc[:, r0:r0 + ROPE] = wq_ref[:, c0 + NOPE:c0 + QK_DIM].astype(BF16)
        wkv_sc[...] = wkv_ref[...].astype(BF16)

    h = _norm_mod(x_ref[...], g_ref[...], sh_ref[...], sc_ref[...]).astype(BF16)
    u = lax.dot_general(h, w_sc[...], (((1,), (1,)), ((), ())), preferred_element_type=F32)
    o1, o2, o3 = Q_RANK, Q_RANK + KV_RANK, Q_RANK + KV_RANK + ROPE
    cq = _rms(u[:, :o1], qn_ref[...])
    q = jnp.dot(cq.astype(BF16), wq_sc[...], preferred_element_type=F32) * _SCALE
    qnope_ref[...] = q[:, :MLA_HEADS * NOPE].astype(qnope_ref.dtype)
    qpe_ref[...] = q[:, MLA_HEADS * NOPE:]
    ckv = _rms(u[:, o1:o2], kvn_ref[...])
    ckv_ref[...] = ckv
    kv_ref[...] = jnp.dot(ckv.astype(BF16), wkv_sc[...], preferred_element_type=F32).astype(kv_ref.dtype)
    kr_ref[...] = u[:, o2:o3]
    uh_ref[...] = u[:, o3:]


def in1_proj(x, g, modtab, cond, w_in, q_norm, kv_norm, w_q_up, w_kv_up):
    tokens = x.shape[0]
    tm = TM_IN1
    nkv = MLA_HEADS * (NOPE + VDIM)
    const = lambda i: (0, 0)
    zero = lambda i: 0
    once = pl.Buffered(1)
    outs = (jax.ShapeDtypeStruct((tokens, MLA_HEADS * NOPE), BF16),
            jax.ShapeDtypeStruct((tokens, MLA_HEADS * ROPE), F32),
            jax.ShapeDtypeStruct((tokens, KV_RANK), F32),
            jax.ShapeDtypeStruct((tokens, ROPE), F32),
            jax.ShapeDtypeStruct((tokens, nkv), BF16),
            jax.ShapeDtypeStruct((tokens, 3 * HY_W), F32))
    row = lambda w: pl.BlockSpec((tm, w), lambda i: (i, 0))
    return pl.pallas_call(
        _in1_kernel,
        out_shape=outs,
        grid=(tokens // tm,),
        in_specs=[row(D),
                  pl.BlockSpec((1, D), const),
                  _mod_spec(0, cond, tm, D, zero),
                  _mod_spec(1, cond, tm, D, zero),
                  pl.BlockSpec((IN1, D), const, pipeline_mode=once),
                  pl.BlockSpec((1, Q_RANK), const),
                  pl.BlockSpec((1, KV_RANK), const),
                  pl.BlockSpec((Q_RANK, MLA_HEADS * QK_DIM), const, pipeline_mode=once),
                  pl.BlockSpec((KV_RANK, nkv), const, pipeline_mode=once)],
        out_specs=tuple(row(o.shape[1]) for o in outs),
        scratch_shapes=[pltpu.VMEM((IN1, D), BF16), pltpu.VMEM((Q_RANK, MLA_HEADS * QK_DIM), BF16),
                        pltpu.VMEM((KV_RANK, nkv), BF16)],
        compiler_params=_cparams(("arbitrary",)),
        name="in1_proj",
    )(x, g.reshape(1, D), modtab, modtab, w_in.T, q_norm.reshape(1, Q_RANK), kv_norm.reshape(1, KV_RANK),
      w_q_up, w_kv_up)


def _mm_kernel(a_ref, w_ref, o_ref):
    o_ref[...] = jnp.dot(a_ref[...].astype(BF16), w_ref[...].astype(BF16),
                         preferred_element_type=F32).astype(o_ref.dtype)


def kv_up(ckv, w_kv_up):
    rows = ckv.shape[0]
    n = w_kv_up.shape[1]
    return pl.pallas_call(
        _mm_kernel,
        out_shape=jax.ShapeDtypeStruct((rows, n), BF16),
        grid=(rows // TM,),
        in_specs=[pl.BlockSpec((TM, KV_RANK), lambda i: (i, 0)), pl.BlockSpec((KV_RANK, n), lambda i: (0, 0))],
        out_specs=pl.BlockSpec((TM, n), lambda i: (i, 0)),
        compiler_params=_cparams(("parallel",)),
        name="kv_up",
    )(ckv, w_kv_up)


_NT = (((1,), (1,)), ((), ()))
_SCALE = 1.0 / math.sqrt(QK_DIM)


def _fill_rope_tables(cos_ref, sin_ref):
    n, width = cos_ref.shape
    n_grid_rows = n // GRID_W
    n_freq = ROPE // 4

    def trig(count):
        lane = lax.broadcasted_iota(jnp.int32, (count, width), 1)
        j = lane & (ROPE // 2 - 1)
        inv = jnp.exp((j & (n_freq - 1)).astype(F32) * (-math.log(ROPE_THETA) / n_freq))
        ang = lax.broadcasted_iota(jnp.int32, (count, width), 0).astype(F32) * inv
        return jnp.cos(ang), jnp.sin(ang), j < n_freq

    cos_c, sin_c, by_row = trig(GRID_W)
    cos_r, sin_r, _ = trig(n_grid_rows)
    for r in range(n_grid_rows):
        rows = slice(r * GRID_W, (r + 1) * GRID_W)
        cos_ref[rows, :] = jnp.where(by_row, jnp.broadcast_to(cos_r[r:r + 1], cos_c.shape), cos_c)
        sin_ref[rows, :] = jnp.where(by_row, jnp.broadcast_to(sin_r[r:r + 1], sin_c.shape), sin_c)


def _rope(x, cos, sin):
    width = x.shape[1]
    lane = lax.broadcasted_iota(jnp.int32, x.shape, 1)
    first_half = (lane & (ROPE - 1)) < ROPE // 2
    xr = jnp.where(first_half, -pltpu.roll(x, width - ROPE // 2, 1), pltpu.roll(x, ROPE // 2, 1))
    return x * cos + xr * sin


def _ones_column(n):
    lane = lax.broadcasted_iota(jnp.int32, (n, VDIM), 1)
    return jnp.where(lane == 0, 1.0, 0.0).astype(BF16)


def _head_attention(qcat, kcat, vaug):
    s = lax.dot_general(qcat, kcat, _NT, preferred_element_type=F32)
    p = jnp.exp(s - jnp.max(s, axis=-1, keepdims=True)).astype(BF16)
    oa = jnp.dot(p, vaug, preferred_element_type=F32)
    return oa[:, :VDIM] / oa[:, VDIM:VDIM + 1]


def _attn_ctx_kernel(qn_ref, qpe_ref, kv_ref, kr_ref, o_ref, *, seq_len):
    n = qn_ref.shape[0]
    n_seq = n // seq_len
    ones = _ones_column(n)
    kpe = kr_ref[...].astype(BF16)
    per_seq = lambda a: a.reshape(n_seq, seq_len, a.shape[-1])
    for h in range(MLA_HEADS):
        c0 = h * (NOPE + VDIM)
        qcat = per_seq(jnp.concatenate([qn_ref[:, h * NOPE:(h + 1) * NOPE],
                                        qpe_ref[:, h * ROPE:(h + 1) * ROPE].astype(BF16)], axis=1))
        kcat = per_seq(jnp.concatenate([kv_ref[:, c0:c0 + NOPE], kpe], axis=1))
        vaug = per_seq(jnp.concatenate([kv_ref[:, c0 + NOPE:c0 + NOPE + VDIM], ones], axis=1))
        s = jnp.einsum("bqd,bkd->bqk", qcat, kcat, preferred_element_type=F32)
        p = jnp.exp(s - jnp.max(s, axis=-1, keepdims=True)).astype(BF16)
        oa = jnp.einsum("bqk,bkd->bqd", p, vaug, preferred_element_type=F32)
        o = oa[:, :, :VDIM] / oa[:, :, VDIM:VDIM + 1]
        o_ref[:, h * VDIM:(h + 1) * VDIM] = o.reshape(n, VDIM).astype(o_ref.dtype)


def attn_ctx(qnope, qpe, kv, kr, seq_len):
    tokens = qnope.shape[0]
    rows = ATTN_CTX_SEQS * seq_len
    blk = lambda w: pl.BlockSpec((rows, w), lambda s: (s, 0))
    return pl.pallas_call(
        functools.partial(_attn_ctx_kernel, seq_len=seq_len),
        out_shape=jax.ShapeDtypeStruct((tokens, MLA_HEADS * VDIM), BF16),
        grid=(tokens // rows,),
        in_specs=[blk(MLA_HEADS * NOPE), blk(MLA_HEADS * ROPE), blk(MLA_HEADS * (NOPE + VDIM)), blk(ROPE)],
        out_specs=blk(MLA_HEADS * VDIM),
        compiler_params=_cparams(("parallel",)),
        name="attn_ctx",
    )(qnope, qpe, kv, kr)


def _attn_lat_kernel(qn_ref, qpe_ref, kvc_ref, krc_ref, kvl_ref, krl_ref, o_ref, kcat_sc, vaug_sc, cos_sc, sin_sc):
    tq = qn_ref.shape[0]
    n_ctx = krc_ref.shape[0]
    n_lat = krl_ref.shape[0]

    @pl.when(pl.program_id(1) == 0)
    def _():
        _fill_rope_tables(cos_sc, sin_sc)
        kr2 = jnp.concatenate([krl_ref[...], krl_ref[...]], axis=1)
        kpe_lat = _rope(kr2, cos_sc[...], sin_sc[...])[:, :ROPE].astype(BF16)
        kpe_ctx = krc_ref[...].astype(BF16)
        ones_c, ones_l = _ones_column(n_ctx), _ones_column(n_lat)
        for h in range(MLA_HEADS):
            c0 = h * (NOPE + VDIM)
            for r0, nr, kv_ref, kpe, ones in ((0, n_ctx, kvc_ref, kpe_ctx, ones_c), (n_ctx, n_lat, kvl_ref, kpe_lat, ones_l)):
                kcat_sc[h, r0:r0 + nr, 0:NOPE] = kv_ref[:, c0:c0 + NOPE]
                kcat_sc[h, r0:r0 + nr, NOPE:QK_DIM] = kpe
                vaug_sc[h, r0:r0 + nr, 0:VDIM] = kv_ref[:, c0 + NOPE:c0 + NOPE + VDIM]
                vaug_sc[h, r0:r0 + nr, VDIM:2 * VDIM] = ones

    q0 = pl.multiple_of(pl.program_id(1) * tq, tq)
    rep = lambda a: jnp.concatenate([a] * (MLA_HEADS // 2), axis=1)
    qp_all = _rope(qpe_ref[...], rep(cos_sc[pl.ds(q0, tq), :]), rep(sin_sc[pl.ds(q0, tq), :])).astype(BF16)
    for h in range(MLA_HEADS):
        qcat = jnp.concatenate([qn_ref[:, h * NOPE:(h + 1) * NOPE], qp_all[:, h * ROPE:(h + 1) * ROPE]], axis=1)
        o_ref[:, h * VDIM:(h + 1) * VDIM] = _head_attention(qcat, kcat_sc[h], vaug_sc[h]).astype(o_ref.dtype)


def attn_lat(qnope, qpe, kv_ctx, kr_ctx, kv_lat, kr_lat, seq_len, ctx_len):
    tokens = qnope.shape[0]
    nq = seq_len // TQ
    qblk = lambda w: pl.BlockSpec((TQ, w), lambda b, i: (b * nq + i, 0))
    seq = lambda n, w: pl.BlockSpec((n, w), lambda b, i: (b, 0))
    nkv = MLA_HEADS * (NOPE + VDIM)
    n_keys = ctx_len + seq_len
    return pl.pallas_call(
        _attn_lat_kernel,
        out_shape=jax.ShapeDtypeStruct((tokens, MLA_HEADS * VDIM), BF16),
        grid=(tokens // seq_len, nq),
        in_specs=[qblk(MLA_HEADS * NOPE), qblk(MLA_HEADS * ROPE), seq(ctx_len, nkv), seq(ctx_len, ROPE),
                  seq(seq_len, nkv), seq(seq_len, ROPE)],
        out_specs=qblk(MLA_HEADS * VDIM),
        scratch_shapes=[pltpu.VMEM((MLA_HEADS, n_keys, QK_DIM), BF16),
                        pltpu.VMEM((MLA_HEADS, n_keys, 2 * VDIM), BF16),
                        pltpu.VMEM((seq_len, 2 * ROPE), F32), pltpu.VMEM((seq_len, 2 * ROPE), F32)],
        compiler_params=_cparams(("parallel", "arbitrary")),
        name="attn_lat",
    )(qnope, qpe, kv_ctx, kr_ctx, kv_lat, kr_lat)


def _dft_kernel(o_ref):
    tr, n = o_ref.shape[1], o_ref.shape[2]
    nb = n // V7X_LANES
    f = pl.program_id(0) * tr + lax.broadcasted_iota(jnp.int32, (tr, V7X_LANES), 0)
    j = lax.broadcasted_iota(jnp.int32, (tr, V7X_LANES), 1)

    def cos_sin(m):
        ang = (m & (2 * n - 1)).astype(F32) * (math.pi / n)
        return jnp.cos(ang), jnp.sin(ang)

    cj, sj = cos_sin(f * j)
    cb, sb = cos_sin(f * (j * V7X_LANES))
    for b in range(nb):
        cbb, sbb = cb[:, b:b + 1], sb[:, b:b + 1]
        cols = slice(b * V7X_LANES, (b + 1) * V7X_LANES)
        o_ref[0, :, cols] = (cbb * cj - sbb * sj).astype(o_ref.dtype)
        o_ref[1, :, cols] = (sbb * cj + cbb * sj).astype(o_ref.dtype)


def dft_tables(n):
    tr = 128
    return pl.pallas_call(
        _dft_kernel,
        out_shape=jax.ShapeDtypeStruct((2, n, n), BF16),
        grid=(n // tr,),
        out_specs=pl.BlockSpec((2, tr, n), lambda i: (0, i, 0)),
        compiler_params=_cparams(("parallel",)),
        name="dft_tables",
    )()


def _split_dot(table, x):
    hi = x.astype(BF16)
    lo = (x - hi.astype(F32)).astype(BF16)
    return (jnp.dot(table, hi, preferred_element_type=F32) + jnp.dot(table, lo, preferred_element_type=F32))


def _hy_filter_kernel(cs_ref, w1_ref, b1_ref, w2_ref, b2_ref, w3_ref, kr_ref, ks_ref, kny_ref):
    n = cs_ref.shape[1]
    row = lax.broadcasted_iota(jnp.int32, (n, V7X_LANES), 0).astype(F32)
    lane = lax.broadcasted_iota(jnp.int32, (n, V7X_LANES), 1)
    t = row * (1.0 / (n - 1))
    w = (2.0 * math.pi) * row / n
    band = jnp.where(lane <= HY_BANDS, lane - 1, lane - 1 - HY_BANDS).astype(F32)
    freq = 1e-4 + band * ((HY_BANDS - 1 - 1e-4) / (HY_BANDS - 1))
    arg = jnp.where(lane <= HY_BANDS, freq * w + 0.5 * math.pi, -(freq * w))
    z = jnp.where(lane == 0, t, jnp.where(lane <= 2 * HY_BANDS, jnp.sin(arg), 0.0))
    hid = jnp.sin(_dot3(z, w1_ref[...]) + b1_ref[...])
    hid = jnp.sin(_dot3(hid, w2_ref[...]) + b2_ref[...])
    hf = _dot3(hid, w3_ref[...])

    rowc = lax.broadcasted_iota(jnp.int32, (n, HY_W), 0)
    chan = lax.broadcasted_iota(jnp.int32, (n, HY_W), 1).astype(F32)
    max_decay = math.log(HY_TARGET) / HY_FAST_DECAY
    min_decay = math.log(HY_TARGET) / HY_SLOW_DECAY
    deltas = min_decay + chan * ((max_decay - min_decay) / (HY_W - 1))
    decay = jnp.exp(-(rowc.astype(F32) * (1.0 / (n - 1))) * jnp.abs(deltas))
    h_fwd = hf[:, :HY_W] * decay
    h_bwd = jnp.where(rowc == 0, 0.0, hf[:, HY_W:] * decay)
    norm = jnp.sum(jnp.abs(h_fwd) + jnp.abs(h_bwd), axis=0, keepdims=True)
    even = (h_fwd + h_bwd) / norm
    odd = (h_fwd - h_bwd) / norm
    cf = jnp.where(rowc == 0, 1.0, 2.0) * (1.0 / (2 * n))
    kr_ref[...] = cf * _split_dot(cs_ref[0], even)
    ks_ref[...] = cf * _split_dot(cs_ref[1], odd)
    sgn = jnp.where((rowc & 1) == 1, -1.0, 1.0)
    kny_ref[...] = jnp.sum(sgn * even, axis=0, keepdims=True) * (1.0 / (2 * n))


def hy_filter(cs, w1p, b1p, w2p, b2p, w3p):
    n = cs.shape[1]
    full = lambda a: pl.BlockSpec(a.shape, lambda: (0,) * a.ndim)
    args = (cs, w1p, b1p, w2p, b2p, w3p)
    return pl.pallas_call(
        _hy_filter_kernel,
        out_shape=(jax.ShapeDtypeStruct((n, HY_W), F32), jax.ShapeDtypeStruct((n, HY_W), F32),
                   jax.ShapeDtypeStruct((1, HY_W), F32)),
        in_specs=[full(a) for a in args],
        out_specs=(pl.BlockSpec((n, HY_W), lambda: (0, 0)), pl.BlockSpec((n, HY_W), lambda: (0, 0)),
                   pl.BlockSpec((1, HY_W), lambda: (0, 0))),
        compiler_params=pltpu.CompilerParams(vmem_limit_bytes=V7X_VMEM_LIMIT_BYTES),
        name="hy_filter",
    )(*args)


def _hyena_kernel(u0_ref, u1_ref, u2_ref, sw_ref, sb_ref, cs_ref, kr_ref, ks_ref, kny_ref, bias_ref, o_ref,
                  *, seq_len):
    n, cb = u0_ref.shape
    n_seq = n // seq_len
    t = lax.broadcasted_iota(jnp.int32, (n, cb), 0) & (seq_len - 1)

    def short_conv(u_ref, k):
        u = u_ref[...]
        w = sw_ref[:, k * cb:(k + 1) * cb]
        return (sb_ref[:, k * cb:(k + 1) * cb] + w[0:1] * _shift_rows(u, 1, t, seq_len) + w[1:2] * u
                + w[2:3] * _shift_rows(u, -1, t, seq_len))

    x0 = short_conv(u0_ref, 0)
    z = short_conv(u1_ref, 1) * short_conv(u2_ref, 2)
    wide = lambda a: jnp.concatenate([a[q * seq_len:(q + 1) * seq_len] for q in range(n_seq)], axis=1)
    rep = lambda a: jnp.concatenate([a] * n_seq, axis=1)
    zw = wide(z)
    zb = zw.astype(BF16)
    c, s = cs_ref[0], cs_ref[1]
    ur = jnp.dot(c, zb, preferred_element_type=F32)
    us = jnp.dot(s, zb, preferred_element_type=F32)
    sgn = jnp.where((lax.broadcasted_iota(jnp.int32, zw.shape, 0) & 1) == 1, -1.0, 1.0)
    uny = jnp.sum(sgn * zw, axis=0, keepdims=True)
    kr, ks = rep(kr_ref[...]), rep(ks_ref[...])
    yr = (ur * kr - us * ks).astype(BF16)
    ys = (ur * ks + us * kr).astype(BF16)
    yw = jnp.dot(c, yr, preferred_element_type=F32) + jnp.dot(s, ys, preferred_element_type=F32)
    yw = yw + sgn * (uny * rep(kny_ref[...]))
    y = jnp.concatenate([yw[:, q * cb:(q + 1) * cb] for q in range(n_seq)], axis=0)
    o_ref[...] = (x0 * (y + bias_ref[...] * z)).astype(o_ref.dtype)


def hyena(uh, seq_len, short_w, short_b, cs, kr, ks, kny, bias):
    tokens = uh.shape[0]
    cb = HY_CB
    nc = HY_W // cb
    rows = max(seq_len, HY_ROWS)
    assert seq_len & (seq_len - 1) == 0 and rows % seq_len == 0
    ublk = lambda k: pl.BlockSpec((rows, cb), lambda s, c: (s, k * nc + c))
    chan = lambda r: pl.BlockSpec((r, cb), lambda s, c: (0, c))
    return pl.pallas_call(
        functools.partial(_hyena_kernel, seq_len=seq_len),
        out_shape=jax.ShapeDtypeStruct((tokens, HY_W), BF16),
        grid=(tokens // rows, nc),
        in_specs=[ublk(0), ublk(1), ublk(2),
                  pl.BlockSpec((None, 3, 3 * cb), lambda s, c: (c, 0, 0)),
                  pl.BlockSpec((None, 1, 3 * cb), lambda s, c: (c, 0, 0)),
                  pl.BlockSpec((2, seq_len, seq_len), lambda s, c: (0, 0, 0)),
                  chan(seq_len), chan(seq_len), chan(1), chan(1)],
        out_specs=pl.BlockSpec((rows, cb), lambda s, c: (s, c)),
        compiler_params=_cparams(("parallel", "parallel")),
        name="hyena",
    )(uh, uh, uh, short_w, short_b, cs, kr, ks, kny, bias)


META_E1, META_E2, META_R1, META_R2, META_G1, META_G2 = range(6)


def _route_kernel(p0_ref, p1_ref, p2_ref, wo_ref, x_ref, g1_ref, g_ref, sh_ref, sc_ref, wr_ref, br_ref,
                  x1_ref, h_ref, meta_ref, meta_t_ref, cnt_ref, run_sc, wo_sc):
    tm = x_ref.shape[0]
    lane = lax.broadcasted_iota(jnp.int32, (tm, V7X_LANES), 1)

    @pl.when(pl.program_id(0) == 0)
    def _():
        run_sc[...] = jnp.zeros_like(run_sc)
        wo_sc[...] = wo_ref[...].astype(BF16)

    kb = p0_ref.shape[1]
    m = jnp.dot(p0_ref[...], wo_sc[0:kb, :], preferred_element_type=F32)
    m += jnp.dot(p1_ref[...], wo_sc[kb:2 * kb, :], preferred_element_type=F32)
    m += jnp.dot(p2_ref[...], wo_sc[2 * kb:3 * kb, :], preferred_element_type=F32)
    x1 = x_ref[...] + g1_ref[...] * m
    x1_ref[...] = x1
    h = _norm_mod(x1, g_ref[...], sh_ref[...], sc_ref[...])
    h_ref[...] = h
    logits = _dot3(h, wr_ref[...]) + br_ref[...]
    valid = lane < N_EXPERTS
    lg = jnp.where(valid, logits, -jnp.inf)
    ex = jnp.exp(lg - jnp.max(lg, axis=-1, keepdims=True))
    p = ex / jnp.sum(ex, axis=-1, keepdims=True)
    p1 = jnp.max(p, axis=-1, keepdims=True)
    i1 = jnp.min(jnp.where((p == p1) & valid, lane, V7X_LANES), axis=-1, keepdims=True)
    rest = jnp.where((lane == i1) | (~valid), -1.0, p)
    p2 = jnp.max(rest, axis=-1, keepdims=True)
    i2 = jnp.min(jnp.where(rest == p2, lane, V7X_LANES), axis=-1, keepdims=True)
    m1 = lane == i1
    m2 = lane == i2
    chosen = jnp.where(m1 | m2, 1.0, 0.0)
    r = lax.broadcasted_iota(jnp.int32, (tm, tm), 0)
    c = lax.broadcasted_iota(jnp.int32, (tm, tm), 1)
    tri = jnp.where(c < r, 1.0, 0.0).astype(BF16)
    before = jnp.dot(tri, chosen.astype(BF16), preferred_element_type=F32) + run_sc[0:1, :]
    rank1 = jnp.sum(jnp.where(m1, before, 0.0), axis=-1, keepdims=True)
    rank2 = jnp.sum(jnp.where(m2, before, 0.0), axis=-1, keepdims=True)
    inv = 1.0 / (p1 + p2)
    vals = (i1.astype(F32), i2.astype(F32), rank1, rank2, p1 * inv, p2 * inv)
    meta = jnp.zeros((tm, V7X_LANES), F32)
    for k, v in enumerate(vals):
        meta = jnp.where(lane == k, v, meta)
    meta_ref[...] = meta
    meta_t_ref[...] = meta.T[:V7X_SUBLANES]
    run_sc[...] = run_sc[...] + jnp.sum(chosen, axis=0, keepdims=True)
    cnt_ref[...] = run_sc[...]


def mix_route(parts, w_out, x, g, modtab, cond, wr_pad, br_pad):
    tokens = x.shape[0]
    tm = TM_ROUTE
    kb = 512
    zero = lambda i: 0
    const = lambda i: (0, 0)
    rows = lambda w: pl.BlockSpec((tm, w), lambda i: (i, 0))
    lhs_specs = [pl.BlockSpec((tm, kb), (lambda i, cbk=cbk: (i, cbk))) for _, cbk in parts]
    return pl.pallas_call(
        _route_kernel,
        out_shape=(jax.ShapeDtypeStruct((tokens, D), F32),
                   jax.ShapeDtypeStruct((tokens, D), F32),
                   jax.ShapeDtypeStruct((tokens, V7X_LANES), F32),
                   jax.ShapeDtypeStruct((V7X_SUBLANES, tokens), F32),
                   jax.ShapeDtypeStruct((V7X_SUBLANES, V7X_LANES), F32)),
        grid=(tokens // tm,),
        in_specs=lhs_specs + [
            pl.BlockSpec((len(parts) * kb, D), const, pipeline_mode=pl.Buffered(1)),
            rows(D),
            _mod_spec(2, cond, tm, D, zero),
            pl.BlockSpec((1, D), const),
            _mod_spec(3, cond, tm, D, zero),
            _mod_spec(4, cond, tm, D, zero),
            pl.BlockSpec((D, V7X_LANES), const),
            pl.BlockSpec((1, V7X_LANES), const)],
        out_specs=(rows(D), rows(D), rows(V7X_LANES),
                   pl.BlockSpec((V7X_SUBLANES, tm), lambda i: (0, i)),
                   pl.BlockSpec((V7X_SUBLANES, V7X_LANES), const)),
        scratch_shapes=[pltpu.VMEM((V7X_SUBLANES, V7X_LANES), F32), pltpu.VMEM((len(parts) * kb, D), BF16)],
        compiler_params=_cparams(("arbitrary",)),
        name="mix_route",
    )(*[a for a, _ in parts], w_out, x, modtab, g.reshape(1, D), modtab, modtab, wr_pad, br_pad)


def _row_copy(src_ref, src_row, dst_ref, dst_row, sem):
    return pltpu.make_async_copy(src_ref.at[pl.ds(src_row, 1)], dst_ref.at[pl.ds(dst_row, 1)], sem)


def _dispatch_kernel(pos_ref, h_ref, hs_in_ref, hs_ref, sem):
    del hs_in_ref
    tm = h_ref.shape[0]
    n_tok = pos_ref.shape[0] // 2
    base = pl.program_id(0) * tm

    def issue(r, carry):
        _row_copy(h_ref, r, hs_ref, pos_ref[base + r], sem).start(priority=0)
        _row_copy(h_ref, r, hs_ref, pos_ref[n_tok + base + r], sem).start(priority=1)
        return carry

    lax.fori_loop(0, tm, issue, 0, unroll=8)
    for _ in range(2):
        pltpu.make_async_copy(h_ref, hs_ref.at[pl.ds(0, tm)], sem).wait()


def moe_dispatch(pos, h, hs):
    tokens = h.shape[0]
    tm = TM_ROUTE
    return pl.pallas_call(
        _dispatch_kernel,
        out_shape=jax.ShapeDtypeStruct(hs.shape, hs.dtype),
        grid_spec=pltpu.PrefetchScalarGridSpec(
            num_scalar_prefetch=1,
            grid=(tokens // tm,),
            in_specs=[pl.BlockSpec((tm, D), lambda i, pos: (i, 0)),
                      pl.BlockSpec(memory_space=pl.ANY)],
            out_specs=pl.BlockSpec(memory_space=pl.ANY),
            scratch_shapes=[pltpu.SemaphoreType.DMA(())]),
        input_output_aliases={2: 0},
        compiler_params=_cparams(("arbitrary",)),
        name="moe_dispatch",
    )(pos, h, hs)


def _experts_kernel(te_ref, sg_ref, su_ref, sd_ref, nu_ref, hs_ref, wg_ref, wu_ref, wd_ref, y_ref,
                    wg_sc, wu_sc, wd_sc):
    del sg_ref, su_ref, sd_ref
    j = pl.program_id(0)
    e = te_ref[j]
    e_prev = te_ref[jnp.maximum(j - 1, 0)]

    @pl.when((j == 0) | (e != e_prev))
    def _():
        wg_sc[...] = wg_ref[...].astype(BF16)
        wu_sc[...] = wu_ref[...].astype(BF16)
        wd_sc[...] = wd_ref[...].astype(BF16)

    @pl.when(j < nu_ref[0])
    def _():
        h = hs_ref[...].astype(BF16)
        y = None
        for c0 in range(0, D_FF_EXPERT, MOE_CHUNK):
            c1 = min(c0 + MOE_CHUNK, D_FF_EXPERT)
            hg = jnp.dot(h, wg_sc[:, c0:c1], preferred_element_type=F32)
            hu = jnp.dot(h, wu_sc[:, c0:c1], preferred_element_type=F32)
            act = (_silu(hg) * hu).astype(BF16)
            yc = jnp.dot(act, wd_sc[c0:c1, :], preferred_element_type=F32)
            y = yc if y is None else y + yc
        y_ref[...] = y

    @pl.when(j >= nu_ref[0])
    def _():
        y_ref[...] = jnp.zeros_like(y_ref)


def moe_experts(tile_expert, stages, n_used, hs, e_gate, e_up, e_down):
    rows = hs.shape[0]
    tmr = TM_EXPERT
    wspec = lambda shape, k: pl.BlockSpec((None,) + shape, lambda j, *pf: (pf[1 + k][j], 0, 0))
    return pl.pallas_call(
        _experts_kernel,
        out_shape=jax.ShapeDtypeStruct((rows, D), F32),
        grid_spec=pltpu.PrefetchScalarGridSpec(
            num_scalar_prefetch=5,
            grid=(rows // tmr,),
            in_specs=[pl.BlockSpec((tmr, D), lambda j, *pf: (j, 0)),
                      wspec((D, D_FF_EXPERT), 0), wspec((D, D_FF_EXPERT), 1), wspec((D_FF_EXPERT, D), 2)],
            out_specs=pl.BlockSpec((tmr, D), lambda j, *pf: (j, 0)),
            scratch_shapes=[pltpu.VMEM((D, D_FF_EXPERT), BF16), pltpu.VMEM((D, D_FF_EXPERT), BF16),
                            pltpu.VMEM((D_FF_EXPERT, D), BF16)]),
        compiler_params=_cparams(("arbitrary",)),
        name="moe_experts",
    )(tile_expert, *stages, n_used, hs, e_gate, e_up, e_down)


def _combine_kernel(pos_ref, x_ref, meta_ref, gt_ref, fg_ref, y_ref, o_ref, b1_sc, b2_sc, sem):
    tm = x_ref.shape[0]
    n_tok = pos_ref.shape[0] // 2
    i = pl.program_id(0)

    def gather(tile, slot):
        base = tile * tm

        def issue(r, carry):
            _row_copy(y_ref, pos_ref[base + r], b1_sc.at[slot], r, sem.at[slot]).start(priority=0)
            _row_copy(y_ref, pos_ref[n_tok + base + r], b2_sc.at[slot], r, sem.at[slot]).start(priority=1)
            return carry

        lax.fori_loop(0, tm, issue, 0, unroll=8)

    @pl.when(i == 0)
    def _():
        gather(0, 0)

    @pl.when(i + 1 < pl.num_programs(0))
    def _():
        gather(i + 1, (i + 1) % 2)

    slot = i % 2
    pltpu.make_async_copy(y_ref.at[pl.ds(0, tm)], b1_sc.at[slot], sem.at[slot]).wait()
    pltpu.make_async_copy(y_ref.at[pl.ds(0, tm)], b2_sc.at[slot], sem.at[slot]).wait()

    meta = meta_ref[...]
    lane = lax.broadcasted_iota(jnp.int32, meta.shape, 1)
    g1 = jnp.sum(jnp.where(lane == META_G1, meta, 0.0), axis=-1, keepdims=True)
    g2 = jnp.sum(jnp.where(lane == META_G2, meta, 0.0), axis=-1, keepdims=True)
    x = x_ref[...] + gt_ref[...] * (g1 * b1_sc[slot] + g2 * b2_sc[slot])
    o_ref[...] = _rms(x, fg_ref[...])


def moe_combine(pos, x, meta, modtab, cond, final_g, y):
    tokens = x.shape[0]
    tm = TM_COMBINE
    return pl.pallas_call(
        _combine_kernel,
        out_shape=jax.ShapeDtypeStruct((tokens, D), F32),
        grid_spec=pltpu.PrefetchScalarGridSpec(
            num_scalar_prefetch=1,
            grid=(tokens // tm,),
            in_specs=[pl.BlockSpec((tm, D), lambda i, pos: (i, 0)),
                      pl.BlockSpec((tm, V7X_LANES), lambda i, pos: (i, 0)),
                      _mod_spec(5, cond, tm, D, lambda i, pos: 0),
                      pl.BlockSpec((1, D), lambda i, pos: (0, 0)),
                      pl.BlockSpec(memory_space=pl.ANY)],
            out_specs=pl.BlockSpec((tm, D), lambda i, pos: (i, 0)),
            scratch_shapes=[pltpu.VMEM((2, tm, D), F32), pltpu.VMEM((2, tm, D), F32),
                            pltpu.SemaphoreType.DMA((2,))]),
        compiler_params=_cparams(("arbitrary",)),
        name="moe_combine",
    )(pos, x, meta, modtab, final_g.reshape(1, D), y)


def moe_plan(metas, counts):
    tmr = TM_EXPERT
    cnts = [c[0, :N_EXPERTS].astype(jnp.int32) for c in counts]
    total = functools.reduce(jnp.add, cnts)
    padded = ((total + tmr - 1) // tmr) * tmr
    ends = jnp.cumsum(padded)
    starts = ends - padded
    n_rows = sum(m.shape[1] for m in metas) * 2 + N_EXPERTS * tmr
    n_tiles = n_rows // tmr
    tile_start = jnp.arange(n_tiles, dtype=jnp.int32) * tmr
    tile_expert = jnp.minimum(jnp.sum(tile_start[:, None] >= ends[None, :], axis=1), N_EXPERTS - 1).astype(jnp.int32)
    n_used = (ends[-1] // tmr).astype(jnp.int32).reshape(1)
    eid = jnp.arange(N_EXPERTS, dtype=jnp.int32)
    later = jnp.where((eid[None, :] > eid[:, None]) & (padded[None, :] > 0), eid[None, :], N_EXPERTS)
    nxt = jnp.min(later, axis=1)
    next_used = jnp.where(nxt == N_EXPERTS, eid, nxt)
    pick = lambda table: jnp.sum(jnp.where(tile_expert[:, None] == eid[None, :], table[None, :], 0), axis=1)
    k_in_group = (tile_start - pick(starts)) // tmr
    tile_next = pick(next_used)
    stages = [jnp.where(k_in_group < k, tile_expert, tile_next).astype(jnp.int32) for k in (1, 2, 3)]
    pos = []
    base = jnp.zeros((N_EXPERTS,), jnp.int32)
    for m, c in zip(metas, cnts):
        first = starts + base
        sel = lambda field: m[field].astype(jnp.int32)
        lookup = lambda e: jnp.sum(jnp.where(e[:, None] == jnp.arange(N_EXPERTS)[None, :], first[None, :], 0), axis=1)
        p1 = lookup(sel(META_E1)) + sel(META_R1)
        p2 = lookup(sel(META_E2)) + sel(META_R2)
        pos.append(jnp.concatenate([p1, p2]).astype(jnp.int32))
        base = base + c
    return pos, tile_expert, stages, n_used, n_rows


def _pad_to(a, shape):
    return jnp.pad(a, [(0, t - s) for s, t in zip(a.shape, shape)])


def _regroup_chunks(a, cb):
    r = a.shape[0]
    return a.reshape(r, 3, HY_W // cb, cb).transpose(2, 0, 1, 3).reshape(HY_W // cb, r, 3 * cb)


def kernel(x_prompt, x_sample, state_l0_lru, cache_l1_ckv, cache_l1_krope, c, c_ctx, l0_norm1, l0_norm2, l0_w_mod, l0_b_mod, l0_w_in, l0_conv_a, l0_lru_conv_w, l0_lru_conv_b, l0_lru_wa, l0_lru_ba, l0_lru_wi, l0_lru_bi, l0_lru_lambda, l0_w_out, l0_ffn_gate, l0_ffn_up, l0_ffn_down, l1_norm1, l1_norm2, l1_w_mod, l1_b_mod, l1_w_in, l1_q_norm, l1_kv_norm, l1_w_q_up, l1_w_kv_up, l1_hy_short_w, l1_hy_short_b, l1_hy_f_w1, l1_hy_f_b1, l1_hy_f_w2, l1_hy_f_b2, l1_hy_f_w3, l1_hy_bias, l1_w_out, l1_router_w, l1_router_b, l1_exp_gate, l1_exp_up, l1_exp_down, final_norm):
    batch, seq, _ = x_prompt.shape
    dec_batch, dec_seq, _ = x_sample.shape
    past_len = cache_l1_ckv.shape[1]

    cond8 = jnp.concatenate([c_ctx[None, :], c, jnp.zeros((V7X_SUBLANES - 1 - dec_batch, D), F32)], axis=0)
    wcat = jnp.concatenate([l0_lru_wa[0], l0_lru_wi[0], l0_lru_wa[1], l0_lru_wi[1]], axis=-1)
    hid = V7X_LANES
    w1p = _pad_to(l1_hy_f_w1, (hid, hid))
    b1p = _pad_to(l1_hy_f_b1.reshape(1, -1), (1, hid))
    w2p = _pad_to(l1_hy_f_w2, (hid, hid))
    b2p = _pad_to(l1_hy_f_b2.reshape(1, -1), (1, hid))
    w3p = _pad_to(l1_hy_f_w3, (hid, 2 * HY_W))
    short_w = _regroup_chunks(l1_hy_short_w, HY_CB)
    short_b = _regroup_chunks(l1_hy_short_b.reshape(1, -1), HY_CB)
    hy_bias = l1_hy_bias.reshape(1, HY_W)
    wr_pad = _pad_to(l1_router_w, (D, V7X_LANES))
    br_pad = _pad_to(l1_router_b.reshape(1, -1), (1, V7X_LANES))

    mod0 = adaln_table(cond8, l0_w_mod, l0_b_mod)
    mod1 = adaln_table(cond8, l1_w_mod, l1_b_mod)

    kv_ctx = kv_up(cache_l1_ckv.reshape(dec_batch * past_len, KV_RANK), l1_w_kv_up)
    kr_ctx = cache_l1_krope.reshape(dec_batch * past_len, ROPE)

    conds = ((0, batch * seq), (1, dec_seq))
    seq_lens = (seq, dec_seq)
    xs = (x_prompt.reshape(batch * seq, D), x_sample.reshape(dec_batch * dec_seq, D))
    h0s = (jnp.zeros((batch, 2, LRU_W), F32), state_l0_lru)

    us = in0_proj(xs, l0_norm1, mod0, conds, l0_w_in)
    parts, lru_states = [], []
    for u, seq_len, h0 in zip(us, seq_lens, h0s):
        ya = conv_a(u, seq_len, l0_conv_a)
        yb, lru_state = rglru(u, seq_len, l0_lru_conv_w, l0_lru_conv_b, wcat, l0_lru_ba, l0_lru_bi,
                              l0_lru_lambda, h0)
        parts.append([(ya, 0), (yb, 0), (yb, 1)])
        lru_states.append(lru_state)
    xs = mix_ffn(parts, l0_w_out, xs, l0_norm2, mod0, conds, l0_ffn_gate, l0_ffn_up, l0_ffn_down)
    new_lru = lru_states[0]

    def layer1(x, seq_len, cond, latent):
        qnope, qpe, ckv, kr, kv, uh = in1_proj(x, l1_norm1, mod1, cond, l1_w_in, l1_q_norm, l1_kv_norm,
                                               l1_w_q_up, l1_w_kv_up)
        if latent:
            yc = attn_lat(qnope, qpe, kv_ctx, kr_ctx, kv, kr, seq_len, past_len)
        else:
            yc = attn_ctx(qnope, qpe, kv, kr, seq_len)
        cs = dft_tables(seq_len)
        k_r, k_s, k_ny = hy_filter(cs, w1p, b1p, w2p, b2p, w3p)
        yd = hyena(uh, seq_len, short_w, short_b, cs, k_r, k_s, k_ny, hy_bias)
        routed = mix_route([(yc, 0), (yc, 1), (yd, 0)], l1_w_out, x, l1_norm2, mod1, cond, wr_pad, br_pad)
        return routed, ckv, kr

    r_p, new_ckv, new_kr = layer1(xs[0], seq, conds[0], latent=False)
    r_s, _, _ = layer1(xs[1], dec_seq, conds[1], latent=True)

    routed = (r_p, r_s)
    pos, tile_expert, stages, n_used, n_rows = moe_plan([r[3] for r in routed], [r[4] for r in routed])
    hs = jnp.zeros((n_rows, D), F32)
    for p, r in zip(pos, routed):
        hs = moe_dispatch(p, r[1], hs)
    y_rows = moe_experts(tile_expert, stages, n_used, hs, l1_exp_gate, l1_exp_up, l1_exp_down)
    y_p, y_s = [moe_combine(p, r[0], r[2], mod1, cond, final_norm, y_rows)
                for p, r, cond in zip(pos, routed, conds)]
    return (y_p.reshape(batch, seq, D), y_s.reshape(dec_batch, dec_seq, D), new_lru,
            new_ckv.reshape(batch, seq, KV_RANK), new_kr.reshape(batch, seq, ROPE))
```

```python
import functools
import math

import jax
import jax.numpy as jnp
from jax import lax
from jax.experimental import pallas as pl
from jax.experimental.pallas import tpu as pltpu

F32 = jnp.float32
BF16 = jnp.bfloat16
HIGHEST = lax.Precision.HIGHEST

D = 1024
GRID_W = 64
EPS = 1e-6
CONV_W = 512
LRU_W = 1024
LRU_BW = 128
LRU_C = 8.0
MLA_HEADS = 8
Q_RANK = 384
KV_RANK = 256
NOPE = 128
ROPE = 64
VDIM = 128
QK_DIM = NOPE + ROPE
ROPE_THETA = 10000.0
HY_W = 512
HY_BANDS = 16
HY_TARGET = 1e-2
HY_FAST_DECAY = 0.3
HY_SLOW_DECAY = 1.5
D_FF = 2816
N_EXPERTS = 8
D_FF_EXPERT = 1408
IN0 = 3 * CONV_W + 2 * LRU_W
IN1 = Q_RANK + KV_RANK + ROPE + 3 * HY_W

V7X_LANES = 128
V7X_SUBLANES = 8
V7X_VMEM_LIMIT_BYTES = 56 * 1024 * 1024
V7X_VMEM_LIMIT_LARGE_BYTES = 60 * 1024 * 1024

TM = 512
TN_IN0 = 512
TF_FFN = 256
MOE_CHUNK = 256
TM_ROUTE = 512
TM_EXPERT = 512
TM_COMBINE = 512
LRU_CB = 256
HY_CB = 256
TQ = 256
ATTN_CTX_SEQS = 4
CONV_A_ROWS = 1024
LRU_ROWS = 1024
HY_ROWS = 1024
TM_IN1 = 512


def _cparams(sem, vmem_limit_bytes=V7X_VMEM_LIMIT_BYTES):
    return pltpu.CompilerParams(dimension_semantics=sem, vmem_limit_bytes=vmem_limit_bytes)


def _sigmoid(x):
    return 0.5 * jnp.tanh(0.5 * x) + 0.5


def _silu(x):
    return x * _sigmoid(x)


def _norm_mod(x, g, shift, scale):
    ms = jnp.mean(x * x, axis=-1, keepdims=True)
    y = x * lax.rsqrt(ms + EPS) * g
    return y * (1.0 + scale) + shift


def _mod_spec(comp, cond, tm, width, col_fn, tile_fn=lambda *ids: ids[0]):
    row0, seg = cond
    assert seg % tm == 0
    return pl.BlockSpec((None, 1, width),
                        lambda *ids: (comp * 3 + row0 + (tile_fn(*ids) * tm) // seg, 0, col_fn(*ids)))


def _dot3(a, b):
    a_hi = a.astype(BF16)
    a_lo = (a - a_hi.astype(F32)).astype(BF16)
    b_hi = b.astype(BF16)
    b_lo = (b - b_hi.astype(F32)).astype(BF16)
    n = a.shape[0]
    y = jnp.dot(jnp.concatenate([a_hi, a_lo], axis=0), b_hi, preferred_element_type=F32)
    return y[:n] + y[n:] + jnp.dot(a_hi, b_lo, preferred_element_type=F32)


def _adaln_kernel(c_ref, w0_ref, b0_ref, w1_ref, b1_ref, o_ref):
    layer0 = pl.program_id(0) == 0
    w = jnp.where(layer0, w0_ref[...], w1_ref[...])
    b = jnp.where(layer0, b0_ref[...], b1_ref[...])
    o_ref[...] = _dot3(_silu(c_ref[...]), w) + b


def adaln_tables(cond8, mods):
    tn = 1536
    nj = 6 * D // tn
    (w0, b0), (w1, b1) = mods
    at0 = lambda l, j: (0, jnp.where(l == 0, j, nj - 1))
    at1 = lambda l, j: (0, jnp.where(l == 1, j, 0))
    m = pl.pallas_call(
        _adaln_kernel,
        out_shape=jax.ShapeDtypeStruct((2, V7X_SUBLANES, 6 * D), F32),
        grid=(2, nj),
        in_specs=[pl.BlockSpec((V7X_SUBLANES, D), lambda l, j: (0, 0)),
                  pl.BlockSpec((D, tn), at0), pl.BlockSpec((1, tn), at0),
                  pl.BlockSpec((D, tn), at1), pl.BlockSpec((1, tn), at1)],
        out_specs=pl.BlockSpec((None, V7X_SUBLANES, tn), lambda l, j: (l, 0, j)),
        compiler_params=_cparams(("arbitrary", "arbitrary")),
        name="adaln",
    )(cond8, w0, b0.reshape(1, 6 * D), w1, b1.reshape(1, 6 * D))
    return [m[l, :3].reshape(3, 6, D).transpose(1, 0, 2).reshape(18, 1, D) for l in range(2)]


def _tile_of(n_load):
    return lambda s: jnp.maximum(s - n_load, 0)


def _block_of(n_load):
    return lambda s: jnp.minimum(s, n_load - 1)


class _TwoSets:
    def __init__(self, n_load, tm, tokens, conds):
        self.n_load, self.tm, self.conds = n_load, tm, conds
        self.n_a, self.n_b = tokens[0] // tm, tokens[1] // tm
        self.steps = n_load + self.n_a + self.n_b

    def tile(self, s):
        return jnp.maximum(s - self.n_load, 0)

    def in_first(self, s):
        return s - self.n_load < self.n_a

    def idx_a(self, s):
        return jnp.minimum(self.tile(s), self.n_a - 1)

    def idx_b(self, s):
        return jnp.clip(self.tile(s) - self.n_a, 0, self.n_b - 1)

    def rows(self, width):
        return (pl.BlockSpec((self.tm, width), lambda s: (self.idx_a(s), 0)),
                pl.BlockSpec((self.tm, width), lambda s: (self.idx_b(s), 0)))

    def cols(self, width, col):
        return (pl.BlockSpec((self.tm, width), lambda s: (self.idx_a(s), col)),
                pl.BlockSpec((self.tm, width), lambda s: (self.idx_b(s), col)))

    def mod_spec(self, comp):
        (row_a, seg_a), (row_b, seg_b) = self.conds
        assert seg_a % self.tm == 0 and seg_b % self.tm == 0

        def row(s):
            return jnp.where(self.in_first(s), row_a + (self.idx_a(s) * self.tm) // seg_a,
                             row_b + (self.idx_b(s) * self.tm) // seg_b)

        return pl.BlockSpec((None, 1, D), lambda s: (comp * 3 + row(s), 0, 0))


def _in0_kernel(xa_ref, xb_ref, g_ref, sh_ref, sc_ref, w_ref, oa_ref, ob_ref, w_sc, *, n_a):
    s = pl.program_id(0)
    n_load = w_sc.shape[0]

    @pl.when(s < n_load)
    def _():
        w_sc[s] = w_ref[...].astype(BF16)

    @pl.when(s >= n_load)
    def _():
        first = s - n_load < n_a
        x = jnp.where(first, xa_ref[...], xb_ref[...])
        h = _norm_mod(x, g_ref[...], sh_ref[...], sc_ref[...]).astype(BF16)
        u = jnp.concatenate([jnp.dot(h, w_sc[j], preferred_element_type=F32).astype(BF16)
                             for j in range(n_load)], axis=1)

        @pl.when(first)
        def _():
            oa_ref[...] = u

        @pl.when(jnp.logical_not(first))
        def _():
            ob_ref[...] = u


def in0_proj(xs, g, modtab, conds, w_in):
    tn = TN_IN0
    n = w_in.shape[1]
    n_load = n // tn
    ts = _TwoSets(n_load, TM, [x.shape[0] for x in xs], conds)
    blk = _block_of(n_load)
    return pl.pallas_call(
        functools.partial(_in0_kernel, n_a=ts.n_a),
        out_shape=tuple(jax.ShapeDtypeStruct((x.shape[0], n), BF16) for x in xs),
        grid=(ts.steps,),
        in_specs=[*ts.rows(D),
                  pl.BlockSpec((1, D), lambda s: (0, 0)),
                  ts.mod_spec(0), ts.mod_spec(1),
                  pl.BlockSpec((D, tn), lambda s: (0, blk(s)))],
        out_specs=ts.rows(n),
        scratch_shapes=[pltpu.VMEM((n_load, D, tn), BF16)],
        compiler_params=_cparams(("arbitrary",)),
        name="in0_proj",
    )(*xs, g.reshape(1, D), modtab, modtab, w_in)


def _shift_rows(v, d, t, seq_len=None):
    n = v.shape[0]
    seq_len = n if seq_len is None else seq_len
    if d > 0:
        return jnp.where(t < d, 0.0, pltpu.roll(v, d, 0))
    return jnp.where(t >= seq_len + d, 0.0, pltpu.roll(v, n + d, 0))


def _conv_a_kernel(b_ref, c_ref, x_ref, w_ref, o_ref, *, seq_len):
    v = c_ref[...].astype(F32) * x_ref[...].astype(F32)
    t = lax.broadcasted_iota(jnp.int32, v.shape, 0) & (seq_len - 1)
    w = w_ref[...]
    y = w[0:1] * _shift_rows(v, 1, t, seq_len) + w[1:2] * v + w[2:3] * _shift_rows(v, -1, t, seq_len)
    o_ref[...] = (b_ref[...].astype(F32) * y).astype(o_ref.dtype)


def conv_a(u, seq_len, conv_w):
    tokens = u.shape[0]
    rows = max(seq_len, CONV_A_ROWS)
    assert seq_len & (seq_len - 1) == 0 and rows % seq_len == 0
    return pl.pallas_call(
        functools.partial(_conv_a_kernel, seq_len=seq_len),
        out_shape=jax.ShapeDtypeStruct((tokens, CONV_W), BF16),
        grid=(tokens // rows,),
        in_specs=[pl.BlockSpec((rows, CONV_W), lambda s: (s, 0)),
                  pl.BlockSpec((rows, CONV_W), lambda s: (s, 1)),
                  pl.BlockSpec((rows, CONV_W), lambda s: (s, 2)),
                  pl.BlockSpec((3, CONV_W), lambda s: (0, 0))],
        out_specs=pl.BlockSpec((rows, CONV_W), lambda s: (s, 0)),
        compiler_params=_cparams(("parallel",)),
        name="conv_a",
    )(u, u, u, conv_w)


def _group_scan(a, b, reverse):
    n, c = a.shape
    a3 = a.reshape(n // V7X_SUBLANES, V7X_SUBLANES, c)
    b3 = b.reshape(n // V7X_SUBLANES, V7X_SUBLANES, c)
    t8 = lax.broadcasted_iota(jnp.int32, a3.shape, 1)
    for d in (1, 2, 4):
        if reverse:
            keep = t8 < V7X_SUBLANES - d
            shift = V7X_SUBLANES - d
        else:
            keep = t8 >= d
            shift = d
        a_sh = jnp.where(keep, pltpu.roll(a3, shift, 1), 1.0)
        b_sh = jnp.where(keep, pltpu.roll(b3, shift, 1), 0.0)
        b3 = a3 * b_sh + b3
        a3 = a3 * a_sh
    return a3.reshape(n, c), b3.reshape(n, c)


def _rglru_kernel(gate_ref, xb_ref, cw_ref, cb_ref, wcat_ref, ba_ref, bi_ref, lam_ref, h0_ref,
                  y_ref, st_ref, af_sc, bf_sc, ab_sc, bb_sc, hf_sc, hb_sc, *, seq_len):
    n, cb = xb_ref.shape
    n_seq = n // seq_len
    xb = xb_ref[...].astype(F32)
    t = lax.broadcasted_iota(jnp.int32, xb.shape, 0) & (seq_len - 1)
    cw = cw_ref[...]
    sh = lambda d: _shift_rows(xb, d, t, seq_len)
    xc = cb_ref[...] + cw[0:1] * sh(2) + cw[1:2] * sh(1) + cw[2:3] * xb + cw[3:4] * sh(-1)
    xcb = xc.astype(BF16)
    g = [jnp.dot(xcb[:, k * LRU_BW:(k + 1) * LRU_BW], wcat_ref[k].astype(BF16), preferred_element_type=F32)
         for k in range(cb // LRU_BW)]

    def direction(d):
        ga = jnp.concatenate([gk[:, (2 * d) * LRU_BW:(2 * d + 1) * LRU_BW] for gk in g], axis=1)
        gi = jnp.concatenate([gk[:, (2 * d + 1) * LRU_BW:(2 * d + 2) * LRU_BW] for gk in g], axis=1)
        r = _sigmoid(ga + ba_ref[d:d + 1, :])
        i = _sigmoid(gi + bi_ref[d:d + 1, :])
        log_a = (-LRU_C * jax.nn.softplus(-lam_ref[d:d + 1, :])) * r
        a = jnp.exp(log_a)
        m = 1.0 - a * a
        mult = m * lax.rsqrt(jnp.maximum(m, 1e-30))
        return a, mult * (i * xc)

    a_f, b_f = direction(0)
    a_f, b_f = _group_scan(a_f, b_f, reverse=False)
    af_sc[...] = a_f
    bf_sc[...] = b_f
    a_b, b_b = direction(1)
    a_b, b_b = _group_scan(a_b, b_b, reverse=True)
    ab_sc[...] = a_b
    bb_sc[...] = b_b

    ng = seq_len // V7X_SUBLANES
    bcast = lambda row: jnp.broadcast_to(row, (V7X_SUBLANES, cb))
    init = tuple((bcast(h0_ref[q, 0:1, :]), bcast(h0_ref[q, 1:2, :])) for q in range(n_seq))

    def step(k, carry):
        out = []
        for q, (hf_in, hb_in) in enumerate(carry):
            rf = pl.multiple_of(q * seq_len + k * V7X_SUBLANES, V7X_SUBLANES)
            rb = pl.multiple_of(q * seq_len + (ng - 1 - k) * V7X_SUBLANES, V7X_SUBLANES)
            hf = af_sc[pl.ds(rf, V7X_SUBLANES), :] * hf_in + bf_sc[pl.ds(rf, V7X_SUBLANES), :]
            hb = ab_sc[pl.ds(rb, V7X_SUBLANES), :] * hb_in + bb_sc[pl.ds(rb, V7X_SUBLANES), :]
            hf_sc[pl.ds(rf, V7X_SUBLANES), :] = hf
            hb_sc[pl.ds(rb, V7X_SUBLANES), :] = hb
            out.append((bcast(hf[V7X_SUBLANES - 1:V7X_SUBLANES]), bcast(hb[0:1])))
        return tuple(out)

    final = lax.fori_loop(0, ng, step, init)
    for q, (hf_last, hb_first) in enumerate(final):
        st_ref[q, 0:1, :] = hf_last[0:1]
        st_ref[q, 1:2, :] = hb_first[0:1]

    gt = gate_ref[...].astype(F32)
    gelu = 0.5 * gt * (1.0 + jnp.tanh(math.sqrt(2.0 / math.pi) * (gt + 0.044715 * (gt * gt * gt))))
    y_ref[...] = ((hf_sc[...] + hb_sc[...]) * gelu).astype(y_ref.dtype)


def rglru(u, seq_len, conv_w, conv_b, wcat, ba, bi, lam, h0):
    tokens = u.shape[0]
    nseq = tokens // seq_len
    cb = LRU_CB
    rows = max(seq_len, LRU_ROWS)
    assert seq_len & (seq_len - 1) == 0 and rows % seq_len == 0
    per_blk = rows // seq_len
    gate_blk0 = 3 * CONV_W // cb
    xb_blk0 = (3 * CONV_W + LRU_W) // cb
    seq_scr = lambda: pltpu.VMEM((rows, cb), F32)
    return pl.pallas_call(
        functools.partial(_rglru_kernel, seq_len=seq_len),
        out_shape=(jax.ShapeDtypeStruct((tokens, LRU_W), BF16), jax.ShapeDtypeStruct((nseq, 2, LRU_W), F32)),
        grid=(tokens // rows, LRU_W // cb),
        in_specs=[pl.BlockSpec((rows, cb), lambda s, c: (s, gate_blk0 + c)),
                  pl.BlockSpec((rows, cb), lambda s, c: (s, xb_blk0 + c)),
                  pl.BlockSpec((4, cb), lambda s, c: (0, c)),
                  pl.BlockSpec((1, cb), lambda s, c: (0, c)),
                  pl.BlockSpec((cb // LRU_BW, LRU_BW, 4 * LRU_BW), lambda s, c: (c, 0, 0)),
                  pl.BlockSpec((2, cb), lambda s, c: (0, c)),
                  pl.BlockSpec((2, cb), lambda s, c: (0, c)),
                  pl.BlockSpec((2, cb), lambda s, c: (0, c)),
                  pl.BlockSpec((per_blk, 2, cb), lambda s, c: (s, 0, c))],
        out_specs=(pl.BlockSpec((rows, cb), lambda s, c: (s, c)),
                   pl.BlockSpec((per_blk, 2, cb), lambda s, c: (s, 0, c))),
        scratch_shapes=[seq_scr() for _ in range(6)],
        compiler_params=_cparams(("parallel", "parallel")),
        name="rglru",
    )(u, u, conv_w, conv_b.reshape(1, LRU_W), wcat, ba, bi, lam, h0)


def _mix_ffn_kernel(p0a_ref, p0b_ref, p1a_ref, p1b_ref, p2a_ref, p2b_ref, wo_ref, xa_ref, xb_ref,
                    g1_ref, g_ref, sh_ref, sc_ref, g2_ref, wg_ref, wu_ref, wd_ref, oa_ref, ob_ref,
                    wo_sc, wg_sc, wu_sc, wd_sc, *, n_a):
    s = pl.program_id(0)
    n_load = wg_sc.shape[0]
    n_out = wo_sc.shape[0]

    @pl.when(s < n_out)
    def _():
        wo_sc[s] = wo_ref[...].astype(BF16)

    @pl.when(s < n_load)
    def _():
        wg_sc[s] = wg_ref[...].astype(BF16)
        wu_sc[s] = wu_ref[...].astype(BF16)
        wd_sc[s] = wd_ref[...].astype(BF16)

    @pl.when(s >= n_load)
    def _():
        first = s - n_load < n_a
        pick = lambda a_ref, b_ref: jnp.where(first, a_ref[...], b_ref[...])
        m = jnp.dot(pick(p0a_ref, p0b_ref), wo_sc[0], preferred_element_type=F32)
        m += jnp.dot(pick(p1a_ref, p1b_ref), wo_sc[1], preferred_element_type=F32)
        m += jnp.dot(pick(p2a_ref, p2b_ref), wo_sc[2], preferred_element_type=F32)
        x = pick(xa_ref, xb_ref) + g1_ref[...] * m
        h = _norm_mod(x, g_ref[...], sh_ref[...], sc_ref[...]).astype(BF16)
        y = None
        for f in range(n_load):
            hg = jnp.dot(h, wg_sc[f], preferred_element_type=F32)
            hu = jnp.dot(h, wu_sc[f], preferred_element_type=F32)
            act = (_silu(hg) * hu).astype(BF16)
            yf = jnp.dot(act, wd_sc[f], preferred_element_type=F32)
            y = yf if y is None else y + yf
        out = x + g2_ref[...] * y

        @pl.when(first)
        def _():
            oa_ref[...] = out

        @pl.when(jnp.logical_not(first))
        def _():
            ob_ref[...] = out


def mix_ffn(parts, w_out, xs, g, modtab, conds, w_gate, w_up, w_down):
    tf = TF_FFN
    kb = 512
    n_load = D_FF // tf
    n_out = len(parts[0])
    assert n_out <= n_load
    ts = _TwoSets(n_load, TM, [x.shape[0] for x in xs], conds)
    blk = _block_of(n_load)
    oblk = _block_of(n_out)
    lhs_specs, lhs_args = [], []
    for (arr_a, col_a), (arr_b, col_b) in zip(*parts):
        assert col_a == col_b
        lhs_specs += ts.cols(kb, col_a)
        lhs_args += [arr_a, arr_b]
    return pl.pallas_call(
        functools.partial(_mix_ffn_kernel, n_a=ts.n_a),
        out_shape=tuple(jax.ShapeDtypeStruct(x.shape, F32) for x in xs),
        grid=(ts.steps,),
        in_specs=lhs_specs + [
            pl.BlockSpec((kb, D), lambda s: (oblk(s), 0)),
            *ts.rows(D),
            ts.mod_spec(2),
            pl.BlockSpec((1, D), lambda s: (0, 0)),
            ts.mod_spec(3), ts.mod_spec(4), ts.mod_spec(5),
            pl.BlockSpec((D, tf), lambda s: (0, blk(s))),
            pl.BlockSpec((D, tf), lambda s: (0, blk(s))),
            pl.BlockSpec((tf, D), lambda s: (blk(s), 0))],
        out_specs=ts.rows(D),
        scratch_shapes=[pltpu.VMEM((n_out, kb, D), BF16),
                        pltpu.VMEM((n_load, D, tf), BF16), pltpu.VMEM((n_load, D, tf), BF16),
                        pltpu.VMEM((n_load, tf, D), BF16)],
        compiler_params=_cparams(("arbitrary",), V7X_VMEM_LIMIT_LARGE_BYTES),
        name="mix_ffn",
    )(*lhs_args, w_out, *xs, modtab, g.reshape(1, D), modtab, modtab, modtab, w_gate, w_up, w_down)


def _rms(x, g):
    return x * lax.rsqrt(jnp.mean(x * x, axis=-1, keepdims=True) + EPS) * g


def _in1_kernel(x_ref, g_ref, sh_ref, sc_ref, w_ref, qn_ref, kvn_ref, wq_ref, wkv_ref,
                qnope_ref, qpe_ref, ckv_ref, kr_ref, kv_ref, uh_ref, w_sc, wq_sc, wkv_sc):
    @pl.when(pl.program_id(0) == 0)
    def _():
        w_sc[...] = w_ref[...].astype(BF16)
        for h in range(MLA_HEADS):
            c0 = h * QK_DIM
            wq_sc[:, h * NOPE:(h + 1) * NOPE] = wq_ref[:, c0:c0 + NOPE].astype(BF16)
            r0 = MLA_HEADS * NOPE + h * ROPE
            wq_sc[:, r0:r0 + ROPE] = wq_ref[:, c0 + NOPE:c0 + QK_DIM].astype(BF16)
        wkv_sc[...] = wkv_ref[...].astype(BF16)

    h = _norm_mod(x_ref[...], g_ref[...], sh_ref[...], sc_ref[...]).astype(BF16)
    u = lax.dot_general(h, w_sc[...], (((1,), (1,)), ((), ())), preferred_element_type=F32)
    o1, o2, o3 = Q_RANK, Q_RANK + KV_RANK, Q_RANK + KV_RANK + ROPE
    cq = _rms(u[:, :o1], qn_ref[...])
    q = jnp.dot(cq.astype(BF16), wq_sc[...], preferred_element_type=F32) * _SCALE
    qnope_ref[...] = q[:, :MLA_HEADS * NOPE].astype(qnope_ref.dtype)
    qpe_ref[...] = q[:, MLA_HEADS * NOPE:]
    ckv = _rms(u[:, o1:o2], kvn_ref[...])
    ckv_ref[...] = ckv
    kv_ref[...] = jnp.dot(ckv.astype(BF16), wkv_sc[...], preferred_element_type=F32).astype(kv_ref.dtype)
    kr_ref[...] = u[:, o2:o3]
    uh_ref[...] = u[:, o3:]


def in1_proj(x, g, modtab, cond, w_in, q_norm, kv_norm, w_q_up, w_kv_up):
    tokens = x.shape[0]
    tm = TM_IN1
    nkv = MLA_HEADS * (NOPE + VDIM)
    const = lambda i: (0, 0)
    zero = lambda i: 0
    once = pl.Buffered(1)
    outs = (jax.ShapeDtypeStruct((tokens, MLA_HEADS * NOPE), BF16),
            jax.ShapeDtypeStruct((tokens, MLA_HEADS * ROPE), F32),
            jax.ShapeDtypeStruct((tokens, KV_RANK), F32),
            jax.ShapeDtypeStruct((tokens, ROPE), F32),
            jax.ShapeDtypeStruct((tokens, nkv), BF16),
            jax.ShapeDtypeStruct((tokens, 3 * HY_W), F32))
    row = lambda w: pl.BlockSpec((tm, w), lambda i: (i, 0))
    return pl.pallas_call(
        _in1_kernel,
        out_shape=outs,
        grid=(tokens // tm,),
        in_specs=[row(D),
                  pl.BlockSpec((1, D), const),
                  _mod_spec(0, cond, tm, D, zero),
                  _mod_spec(1, cond, tm, D, zero),
                  pl.BlockSpec((IN1, D), const, pipeline_mode=once),
                  pl.BlockSpec((1, Q_RANK), const),
                  pl.BlockSpec((1, KV_RANK), const),
                  pl.BlockSpec((Q_RANK, MLA_HEADS * QK_DIM), const, pipeline_mode=once),
                  pl.BlockSpec((KV_RANK, nkv), const, pipeline_mode=once)],
        out_specs=tuple(row(o.shape[1]) for o in outs),
        scratch_shapes=[pltpu.VMEM((IN1, D), BF16), pltpu.VMEM((Q_RANK, MLA_HEADS * QK_DIM), BF16),
                        pltpu.VMEM((KV_RANK, nkv), BF16)],
        compiler_params=_cparams(("arbitrary",)),
        name="in1_proj",
    )(x, g.reshape(1, D), modtab, modtab, w_in.T, q_norm.reshape(1, Q_RANK), kv_norm.reshape(1, KV_RANK),
      w_q_up, w_kv_up)


def _mm_kernel(a_ref, w_ref, o_ref):
    o_ref[...] = jnp.dot(a_ref[...].astype(BF16), w_ref[...].astype(BF16),
                         preferred_element_type=F32).astype(o_ref.dtype)


def kv_up(ckv, w_kv_up):
    rows = ckv.shape[0]
    n = w_kv_up.shape[1]
    return pl.pallas_call(
        _mm_kernel,
        out_shape=jax.ShapeDtypeStruct((rows, n), BF16),
        grid=(rows // TM,),
        in_specs=[pl.BlockSpec((TM, KV_RANK), lambda i: (i, 0)), pl.BlockSpec((KV_RANK, n), lambda i: (0, 0))],
        out_specs=pl.BlockSpec((TM, n), lambda i: (i, 0)),
        compiler_params=_cparams(("parallel",)),
        name="kv_up",
    )(ckv, w_kv_up)


_NT = (((1,), (1,)), ((), ()))
_SCALE = 1.0 / math.sqrt(QK_DIM)


def _fill_rope_tables(cos_ref, sin_ref):
    n, width = cos_ref.shape
    n_grid_rows = n // GRID_W
    n_freq = ROPE // 4

    def trig(count):
        lane = lax.broadcasted_iota(jnp.int32, (count, width), 1)
        j = lane & (ROPE // 2 - 1)
        inv = jnp.exp((j & (n_freq - 1)).astype(F32) * (-math.log(ROPE_THETA) / n_freq))
        ang = lax.broadcasted_iota(jnp.int32, (count, width), 0).astype(F32) * inv
        return jnp.cos(ang), jnp.sin(ang), j < n_freq

    cos_c, sin_c, by_row = trig(GRID_W)
    cos_r, sin_r, _ = trig(n_grid_rows)
    for r in range(n_grid_rows):
        rows = slice(r * GRID_W, (r + 1) * GRID_W)
        cos_ref[rows, :] = jnp.where(by_row, jnp.broadcast_to(cos_r[r:r + 1], cos_c.shape), cos_c)
        sin_ref[rows, :] = jnp.where(by_row, jnp.broadcast_to(sin_r[r:r + 1], sin_c.shape), sin_c)


def _rope(x, cos, sin):
    width = x.shape[1]
    lane = lax.broadcasted_iota(jnp.int32, x.shape, 1)
    first_half = (lane & (ROPE - 1)) < ROPE // 2
    xr = jnp.where(first_half, -pltpu.roll(x, width - ROPE // 2, 1), pltpu.roll(x, ROPE // 2, 1))
    return x * cos + xr * sin


def _ones_column(n):
    lane = lax.broadcasted_iota(jnp.int32, (n, VDIM), 1)
    return jnp.where(lane == 0, 1.0, 0.0).astype(BF16)


def _head_attention(qcat, kcat, vaug):
    s = lax.dot_general(qcat, kcat, _NT, preferred_element_type=F32)
    p = jnp.exp(s - jnp.max(s, axis=-1, keepdims=True)).astype(BF16)
    oa = jnp.dot(p, vaug, preferred_element_type=F32)
    return oa[:, :VDIM] / oa[:, VDIM:VDIM + 1]


def _attn_ctx_kernel(qn_ref, qpe_ref, kv_ref, kr_ref, o_ref, *, seq_len):
    n = qn_ref.shape[0]
    n_seq = n // seq_len
    ones = _ones_column(n)
    kpe = kr_ref[...].astype(BF16)
    per_seq = lambda a: a.reshape(n_seq, seq_len, a.shape[-1])
    for h in range(MLA_HEADS):
        c0 = h * (NOPE + VDIM)
        qcat = per_seq(jnp.concatenate([qn_ref[:, h * NOPE:(h + 1) * NOPE],
                                        qpe_ref[:, h * ROPE:(h + 1) * ROPE].astype(BF16)], axis=1))
        kcat = per_seq(jnp.concatenate([kv_ref[:, c0:c0 + NOPE], kpe], axis=1))
        vaug = per_seq(jnp.concatenate([kv_ref[:, c0 + NOPE:c0 + NOPE + VDIM], ones], axis=1))
        s = jnp.einsum("bqd,bkd->bqk", qcat, kcat, preferred_element_type=F32)
        p = jnp.exp(s - jnp.max(s, axis=-1, keepdims=True)).astype(BF16)
        oa = jnp.einsum("bqk,bkd->bqd", p, vaug, preferred_element_type=F32)
        o = oa[:, :, :VDIM] / oa[:, :, VDIM:VDIM + 1]
        o_ref[:, h * VDIM:(h + 1) * VDIM] = o.reshape(n, VDIM).astype(o_ref.dtype)


def attn_ctx(qnope, qpe, kv, kr, seq_len):
    tokens = qnope.shape[0]
    rows = ATTN_CTX_SEQS * seq_len
    blk = lambda w: pl.BlockSpec((rows, w), lambda s: (s, 0))
    return pl.pallas_call(
        functools.partial(_attn_ctx_kernel, seq_len=seq_len),
        out_shape=jax.ShapeDtypeStruct((tokens, MLA_HEADS * VDIM), BF16),
        grid=(tokens // rows,),
        in_specs=[blk(MLA_HEADS * NOPE), blk(MLA_HEADS * ROPE), blk(MLA_HEADS * (NOPE + VDIM)), blk(ROPE)],
        out_specs=blk(MLA_HEADS * VDIM),
        compiler_params=_cparams(("parallel",)),
        name="attn_ctx",
    )(qnope, qpe, kv, kr)


def _attn_lat_kernel(qn_ref, qpe_ref, kvc_ref, krc_ref, kvl_ref, krl_ref, o_ref, kcat_sc, vaug_sc, cos_sc, sin_sc):
    tq = qn_ref.shape[0]
    n_ctx = krc_ref.shape[0]
    n_lat = krl_ref.shape[0]

    @pl.when(pl.program_id(1) == 0)
    def _():
        _fill_rope_tables(cos_sc, sin_sc)
        kr2 = jnp.concatenate([krl_ref[...], krl_ref[...]], axis=1)
        kpe_lat = _rope(kr2, cos_sc[...], sin_sc[...])[:, :ROPE].astype(BF16)
        kpe_ctx = krc_ref[...].astype(BF16)
        ones_c, ones_l = _ones_column(n_ctx), _ones_column(n_lat)
        for h in range(MLA_HEADS):
            c0 = h * (NOPE + VDIM)
            for r0, nr, kv_ref, kpe, ones in ((0, n_ctx, kvc_ref, kpe_ctx, ones_c), (n_ctx, n_lat, kvl_ref, kpe_lat, ones_l)):
                kcat_sc[h, r0:r0 + nr, 0:NOPE] = kv_ref[:, c0:c0 + NOPE]
                kcat_sc[h, r0:r0 + nr, NOPE:QK_DIM] = kpe
                vaug_sc[h, r0:r0 + nr, 0:VDIM] = kv_ref[:, c0 + NOPE:c0 + NOPE + VDIM]
                vaug_sc[h, r0:r0 + nr, VDIM:2 * VDIM] = ones

    q0 = pl.multiple_of(pl.program_id(1) * tq, tq)
    rep = lambda a: jnp.concatenate([a] * (MLA_HEADS // 2), axis=1)
    qp_all = _rope(qpe_ref[...], rep(cos_sc[pl.ds(q0, tq), :]), rep(sin_sc[pl.ds(q0, tq), :])).astype(BF16)
    for h in range(MLA_HEADS):
        qcat = jnp.concatenate([qn_ref[:, h * NOPE:(h + 1) * NOPE], qp_all[:, h * ROPE:(h + 1) * ROPE]], axis=1)
        o_ref[:, h * VDIM:(h + 1) * VDIM] = _head_attention(qcat, kcat_sc[h], vaug_sc[h]).astype(o_ref.dtype)


def attn_lat(qnope, qpe, kv_ctx, kr_ctx, kv_lat, kr_lat, seq_len, ctx_len):
    tokens = qnope.shape[0]
    nq = seq_len // TQ
    qblk = lambda w: pl.BlockSpec((TQ, w), lambda b, i: (b * nq + i, 0))
    seq = lambda n, w: pl.BlockSpec((n, w), lambda b, i: (b, 0))
    nkv = MLA_HEADS * (NOPE + VDIM)
    n_keys = ctx_len + seq_len
    return pl.pallas_call(
        _attn_lat_kernel,
        out_shape=jax.ShapeDtypeStruct((tokens, MLA_HEADS * VDIM), BF16),
        grid=(tokens // seq_len, nq),
        in_specs=[qblk(MLA_HEADS * NOPE), qblk(MLA_HEADS * ROPE), seq(ctx_len, nkv), seq(ctx_len, ROPE),
                  seq(seq_len, nkv), seq(seq_len, ROPE)],
        out_specs=qblk(MLA_HEADS * VDIM),
        scratch_shapes=[pltpu.VMEM((MLA_HEADS, n_keys, QK_DIM), BF16),
                        pltpu.VMEM((MLA_HEADS, n_keys, 2 * VDIM), BF16),
                        pltpu.VMEM((seq_len, 2 * ROPE), F32), pltpu.VMEM((seq_len, 2 * ROPE), F32)],
        compiler_params=_cparams(("parallel", "arbitrary")),
        name="attn_lat",
    )(qnope, qpe, kv_ctx, kr_ctx, kv_lat, kr_lat)


def _dft_kernel(o_ref):
    tr, n = o_ref.shape[1], o_ref.shape[2]
    nb = n // V7X_LANES
    f = pl.program_id(0) * tr + lax.broadcasted_iota(jnp.int32, (tr, V7X_LANES), 0)
    j = lax.broadcasted_iota(jnp.int32, (tr, V7X_LANES), 1)

    def cos_sin(m):
        ang = (m & (2 * n - 1)).astype(F32) * (math.pi / n)
        return jnp.cos(ang), jnp.sin(ang)

    cj, sj = cos_sin(f * j)
    cb, sb = cos_sin(f * (j * V7X_LANES))
    for b in range(nb):
        cbb, sbb = cb[:, b:b + 1], sb[:, b:b + 1]
        cols = slice(b * V7X_LANES, (b + 1) * V7X_LANES)
        o_ref[0, :, cols] = (cbb * cj - sbb * sj).astype(o_ref.dtype)
        o_ref[1, :, cols] = (sbb * cj + cbb * sj).astype(o_ref.dtype)


def dft_tables(n):
    tr = 128
    return pl.pallas_call(
        _dft_kernel,
        out_shape=jax.ShapeDtypeStruct((2, n, n), BF16),
        grid=(n // tr,),
        out_specs=pl.BlockSpec((2, tr, n), lambda i: (0, i, 0)),
        compiler_params=_cparams(("parallel",)),
        name="dft_tables",
    )()


def _split_dot(table, x):
    hi = x.astype(BF16)
    lo = (x - hi.astype(F32)).astype(BF16)
    return (jnp.dot(table, hi, preferred_element_type=F32) + jnp.dot(table, lo, preferred_element_type=F32))


def _hy_filter_kernel(cs_ref, w1_ref, b1_ref, w2_ref, b2_ref, w3_ref, kr_ref, ks_ref, kny_ref):
    n = cs_ref.shape[1]
    row = lax.broadcasted_iota(jnp.int32, (n, V7X_LANES), 0).astype(F32)
    lane = lax.broadcasted_iota(jnp.int32, (n, V7X_LANES), 1)
    t = row * (1.0 / (n - 1))
    w = (2.0 * math.pi) * row / n
    band = jnp.where(lane <= HY_BANDS, lane - 1, lane - 1 - HY_BANDS).astype(F32)
    freq = 1e-4 + band * ((HY_BANDS - 1 - 1e-4) / (HY_BANDS - 1))
    arg = jnp.where(lane <= HY_BANDS, freq * w + 0.5 * math.pi, -(freq * w))
    z = jnp.where(lane == 0, t, jnp.where(lane <= 2 * HY_BANDS, jnp.sin(arg), 0.0))
    hid = jnp.sin(_dot3(z, w1_ref[...]) + b1_ref[...])
    hid = jnp.sin(_dot3(hid, w2_ref[...]) + b2_ref[...])
    hf = _dot3(hid, w3_ref[...])

    rowc = lax.broadcasted_iota(jnp.int32, (n, HY_W), 0)
    chan = lax.broadcasted_iota(jnp.int32, (n, HY_W), 1).astype(F32)
    max_decay = math.log(HY_TARGET) / HY_FAST_DECAY
    min_decay = math.log(HY_TARGET) / HY_SLOW_DECAY
    deltas = min_decay + chan * ((max_decay - min_decay) / (HY_W - 1))
    decay = jnp.exp(-(rowc.astype(F32) * (1.0 / (n - 1))) * jnp.abs(deltas))
    h_fwd = hf[:, :HY_W] * decay
    h_bwd = jnp.where(rowc == 0, 0.0, hf[:, HY_W:] * decay)
    norm = jnp.sum(jnp.abs(h_fwd) + jnp.abs(h_bwd), axis=0, keepdims=True)
    even = (h_fwd + h_bwd) / norm
    odd = (h_fwd - h_bwd) / norm
    cf = jnp.where(rowc == 0, 1.0, 2.0) * (1.0 / (2 * n))
    kr_ref[...] = cf * _split_dot(cs_ref[0], even)
    ks_ref[...] = cf * _split_dot(cs_ref[1], odd)
    sgn = jnp.where((rowc & 1) == 1, -1.0, 1.0)
    kny_ref[...] = jnp.sum(sgn * even, axis=0, keepdims=True) * (1.0 / (2 * n))


def hy_filter(cs, w1p, b1p, w2p, b2p, w3p):
    n = cs.shape[1]
    full = lambda a: pl.BlockSpec(a.shape, lambda: (0,) * a.ndim)
    args = (cs, w1p, b1p, w2p, b2p, w3p)
    return pl.pallas_call(
        _hy_filter_kernel,
        out_shape=(jax.ShapeDtypeStruct((n, HY_W), F32), jax.ShapeDtypeStruct((n, HY_W), F32),
                   jax.ShapeDtypeStruct((1, HY_W), F32)),
        in_specs=[full(a) for a in args],
        out_specs=(pl.BlockSpec((n, HY_W), lambda: (0, 0)), pl.BlockSpec((n, HY_W), lambda: (0, 0)),
                   pl.BlockSpec((1, HY_W), lambda: (0, 0))),
        compiler_params=pltpu.CompilerParams(vmem_limit_bytes=V7X_VMEM_LIMIT_BYTES),
        name="hy_filter",
    )(*args)


def _hyena_kernel(u0_ref, u1_ref, u2_ref, sw_ref, sb_ref, cs_ref, kr_ref, ks_ref, kny_ref, bias_ref, o_ref,
                  *, seq_len):
    n, cb = u0_ref.shape
    n_seq = n // seq_len
    t = lax.broadcasted_iota(jnp.int32, (n, cb), 0) & (seq_len - 1)

    def short_conv(u_ref, k):
        u = u_ref[...]
        w = sw_ref[:, k * cb:(k + 1) * cb]
        return (sb_ref[:, k * cb:(k + 1) * cb] + w[0:1] * _shift_rows(u, 1, t, seq_len) + w[1:2] * u
                + w[2:3] * _shift_rows(u, -1, t, seq_len))

    x0 = short_conv(u0_ref, 0)
    z = short_conv(u1_ref, 1) * short_conv(u2_ref, 2)
    wide = lambda a: jnp.concatenate([a[q * seq_len:(q + 1) * seq_len] for q in range(n_seq)], axis=1)
    rep = lambda a: jnp.concatenate([a] * n_seq, axis=1)
    zw = wide(z)
    zb = zw.astype(BF16)
    c, s = cs_ref[0], cs_ref[1]
    ur = jnp.dot(c, zb, preferred_element_type=F32)
    us = jnp.dot(s, zb, preferred_element_type=F32)
    sgn = jnp.where((lax.broadcasted_iota(jnp.int32, zw.shape, 0) & 1) == 1, -1.0, 1.0)
    uny = jnp.sum(sgn * zw, axis=0, keepdims=True)
    kr, ks = rep(kr_ref[...]), rep(ks_ref[...])
    yr = (ur * kr - us * ks).astype(BF16)
    ys = (ur * ks + us * kr).astype(BF16)
    yw = jnp.dot(c, yr, preferred_element_type=F32) + jnp.dot(s, ys, preferred_element_type=F32)
    yw = yw + sgn * (uny * rep(kny_ref[...]))
    y = jnp.concatenate([yw[:, q * cb:(q + 1) * cb] for q in range(n_seq)], axis=0)
    o_ref[...] = (x0 * (y + bias_ref[...] * z)).astype(o_ref.dtype)


def hyena(uh, seq_len, short_w, short_b, cs, kr, ks, kny, bias):
    tokens = uh.shape[0]
    cb = HY_CB
    nc = HY_W // cb
    rows = max(seq_len, HY_ROWS)
    assert seq_len & (seq_len - 1) == 0 and rows % seq_len == 0
    ublk = lambda k: pl.BlockSpec((rows, cb), lambda s, c: (s, k * nc + c))
    chan = lambda r: pl.BlockSpec((r, cb), lambda s, c: (0, c))
    return pl.pallas_call(
        functools.partial(_hyena_kernel, seq_len=seq_len),
        out_shape=jax.ShapeDtypeStruct((tokens, HY_W), BF16),
        grid=(tokens // rows, nc),
        in_specs=[ublk(0), ublk(1), ublk(2),
                  pl.BlockSpec((None, 3, 3 * cb), lambda s, c: (c, 0, 0)),
                  pl.BlockSpec((None, 1, 3 * cb), lambda s, c: (c, 0, 0)),
                  pl.BlockSpec((2, seq_len, seq_len), lambda s, c: (0, 0, 0)),
                  chan(seq_len), chan(seq_len), chan(1), chan(1)],
        out_specs=pl.BlockSpec((rows, cb), lambda s, c: (s, c)),
        compiler_params=_cparams(("parallel", "parallel")),
        name="hyena",
    )(uh, uh, uh, short_w, short_b, cs, kr, ks, kny, bias)


META_E1, META_E2, META_R1, META_R2, META_G1, META_G2 = range(6)


def _route_kernel(p0_ref, p1_ref, p2_ref, wo_ref, x_ref, g1_ref, g_ref, sh_ref, sc_ref, wr_ref, br_ref,
                  x1_ref, h_ref, meta_ref, meta_t_ref, cnt_ref, run_sc, wo_sc):
    tm = x_ref.shape[0]
    lane = lax.broadcasted_iota(jnp.int32, (tm, V7X_LANES), 1)

    @pl.when(pl.program_id(0) == 0)
    def _():
        run_sc[...] = jnp.zeros_like(run_sc)
        wo_sc[...] = wo_ref[...].astype(BF16)

    kb = p0_ref.shape[1]
    m = jnp.dot(p0_ref[...], wo_sc[0:kb, :], preferred_element_type=F32)
    m += jnp.dot(p1_ref[...], wo_sc[kb:2 * kb, :], preferred_element_type=F32)
    m += jnp.dot(p2_ref[...], wo_sc[2 * kb:3 * kb, :], preferred_element_type=F32)
    x1 = x_ref[...] + g1_ref[...] * m
    x1_ref[...] = x1
    h = _norm_mod(x1, g_ref[...], sh_ref[...], sc_ref[...])
    h_ref[...] = h
    logits = _dot3(h, wr_ref[...]) + br_ref[...]
    valid = lane < N_EXPERTS
    lg = jnp.where(valid, logits, -jnp.inf)
    ex = jnp.exp(lg - jnp.max(lg, axis=-1, keepdims=True))
    p = ex / jnp.sum(ex, axis=-1, keepdims=True)
    p1 = jnp.max(p, axis=-1, keepdims=True)
    i1 = jnp.min(jnp.where((p == p1) & valid, lane, V7X_LANES), axis=-1, keepdims=True)
    rest = jnp.where((lane == i1) | (~valid), -1.0, p)
    p2 = jnp.max(rest, axis=-1, keepdims=True)
    i2 = jnp.min(jnp.where(rest == p2, lane, V7X_LANES), axis=-1, keepdims=True)
    m1 = lane == i1
    m2 = lane == i2
    chosen = jnp.where(m1 | m2, 1.0, 0.0)
    r = lax.broadcasted_iota(jnp.int32, (tm, tm), 0)
    c = lax.broadcasted_iota(jnp.int32, (tm, tm), 1)
    tri = jnp.where(c < r, 1.0, 0.0).astype(BF16)
    before = jnp.dot(tri, chosen.astype(BF16), preferred_element_type=F32) + run_sc[0:1, :]
    rank1 = jnp.sum(jnp.where(m1, before, 0.0), axis=-1, keepdims=True)
    rank2 = jnp.sum(jnp.where(m2, before, 0.0), axis=-1, keepdims=True)
    inv = 1.0 / (p1 + p2)
    vals = (i1.astype(F32), i2.astype(F32), rank1, rank2, p1 * inv, p2 * inv)
    meta = jnp.zeros((tm, V7X_LANES), F32)
    for k, v in enumerate(vals):
        meta = jnp.where(lane == k, v, meta)
    meta_ref[...] = meta
    meta_t_ref[...] = meta.T[:V7X_SUBLANES]
    run_sc[...] = run_sc[...] + jnp.sum(chosen, axis=0, keepdims=True)
    cnt_ref[...] = run_sc[...]


def mix_route(parts, w_out, x, g, modtab, cond, wr_pad, br_pad):
    tokens = x.shape[0]
    tm = TM_ROUTE
    kb = 512
    zero = lambda i: 0
    const = lambda i: (0, 0)
    rows = lambda w: pl.BlockSpec((tm, w), lambda i: (i, 0))
    lhs_specs = [pl.BlockSpec((tm, kb), (lambda i, cbk=cbk: (i, cbk))) for _, cbk in parts]
    return pl.pallas_call(
        _route_kernel,
        out_shape=(jax.ShapeDtypeStruct((tokens, D), F32),
                   jax.ShapeDtypeStruct((tokens, D), F32),
                   jax.ShapeDtypeStruct((tokens, V7X_LANES), F32),
                   jax.ShapeDtypeStruct((V7X_SUBLANES, tokens), F32),
                   jax.ShapeDtypeStruct((V7X_SUBLANES, V7X_LANES), F32)),
        grid=(tokens // tm,),
        in_specs=lhs_specs + [
            pl.BlockSpec((len(parts) * kb, D), const, pipeline_mode=pl.Buffered(1)),
            rows(D),
            _mod_spec(2, cond, tm, D, zero),
            pl.BlockSpec((1, D), const),
            _mod_spec(3, cond, tm, D, zero),
            _mod_spec(4, cond, tm, D, zero),
            pl.BlockSpec((D, V7X_LANES), const),
            pl.BlockSpec((1, V7X_LANES), const)],
        out_specs=(rows(D), rows(D), rows(V7X_LANES),
                   pl.BlockSpec((V7X_SUBLANES, tm), lambda i: (0, i)),
                   pl.BlockSpec((V7X_SUBLANES, V7X_LANES), const)),
        scratch_shapes=[pltpu.VMEM((V7X_SUBLANES, V7X_LANES), F32), pltpu.VMEM((len(parts) * kb, D), BF16)],
        compiler_params=_cparams(("arbitrary",)),
        name="mix_route",
    )(*[a for a, _ in parts], w_out, x, modtab, g.reshape(1, D), modtab, modtab, wr_pad, br_pad)


def _row_copy(src_ref, src_row, dst_ref, dst_row, sem):
    return pltpu.make_async_copy(src_ref.at[pl.ds(src_row, 1)], dst_ref.at[pl.ds(dst_row, 1)], sem)


_PAD_BULK = (256, 128, 64, 32, 16, 8)


def _zero_fill(hs_ref, zero_sc, sem, pads_ref, n_tail_max, wait):
    tmr = zero_sc.shape[0]

    def copy(rows, dst):
        cp = pltpu.make_async_copy(zero_sc.at[pl.ds(0, rows)], hs_ref.at[pl.ds(dst, rows)], sem)
        cp.wait() if wait else cp.start()

    for e in range(N_EXPERTS):
        start, n = pads_ref[e], pads_ref[N_EXPERTS + e]
        head = jnp.minimum((-start) & (V7X_SUBLANES - 1), n)
        for r in range(V7X_SUBLANES - 1):
            @pl.when(r < head)
            def _():
                copy(1, start + r)
        body = start + head
        rem = n - head
        for k in _PAD_BULK:
            @pl.when((rem & k) != 0)
            def _():
                copy(k, pl.multiple_of(body + (rem & ~(2 * k - 1)), V7X_SUBLANES))
    tail_start, tail_tiles = pads_ref[2 * N_EXPERTS], pads_ref[2 * N_EXPERTS + 1]
    for t in range(n_tail_max):
        @pl.when(t < tail_tiles)
        def _():
            copy(tmr, pl.multiple_of(tail_start + t * tmr, tmr))


def _dispatch_kernel(pos_ref, pads_ref, ha_ref, hb_ref, hs_ref, zero_sc, sem, zsem, *, n_a, n_tail_max):
    tm = ha_ref.shape[0]
    n_tok = pos_ref.shape[0] // 2
    i = pl.program_id(0)
    base = i * tm

    @pl.when(i == 0)
    def _():
        zero_sc[...] = jnp.zeros_like(zero_sc)
        _zero_fill(hs_ref, zero_sc, zsem, pads_ref, n_tail_max, wait=False)

    def scatter(h_ref):
        def issue(r, carry):
            _row_copy(h_ref, r, hs_ref, pos_ref[base + r], sem).start(priority=0)
            _row_copy(h_ref, r, hs_ref, pos_ref[n_tok + base + r], sem).start(priority=1)
            return carry

        lax.fori_loop(0, tm, issue, 0, unroll=8)
        for _ in range(2):
            pltpu.make_async_copy(h_ref, hs_ref.at[pl.ds(0, tm)], sem).wait()

    @pl.when(i < n_a)
    def _():
        scatter(ha_ref)

    @pl.when(i >= n_a)
    def _():
        scatter(hb_ref)

    @pl.when(i == 0)
    def _():
        _zero_fill(hs_ref, zero_sc, zsem, pads_ref, n_tail_max, wait=True)


def moe_dispatch(pos, pads, hs_rows, h_a, h_b):
    tm = TM_ROUTE
    n_a, n_b = h_a.shape[0] // tm, h_b.shape[0] // tm
    n_tail_max = hs_rows // TM_EXPERT - (2 * (h_a.shape[0] + h_b.shape[0])) // TM_EXPERT
    return pl.pallas_call(
        functools.partial(_dispatch_kernel, n_a=n_a, n_tail_max=n_tail_max),
        out_shape=jax.ShapeDtypeStruct((hs_rows, D), F32),
        grid_spec=pltpu.PrefetchScalarGridSpec(
            num_scalar_prefetch=2,
            grid=(n_a + n_b,),
            in_specs=[pl.BlockSpec((tm, D), lambda i, *pf: (jnp.minimum(i, n_a - 1), 0)),
                      pl.BlockSpec((tm, D), lambda i, *pf: (jnp.clip(i - n_a, 0, n_b - 1), 0))],
            out_specs=pl.BlockSpec(memory_space=pl.ANY),
            scratch_shapes=[pltpu.VMEM((TM_EXPERT, D), F32), pltpu.SemaphoreType.DMA(()),
                            pltpu.SemaphoreType.DMA(())]),
        compiler_params=_cparams(("arbitrary",)),
        name="moe_dispatch",
    )(pos, pads, h_a, h_b)


def _experts_kernel(te_ref, sg_ref, su_ref, sd_ref, nv_ref, hs_ref, wg_ref, wu_ref, wd_ref, y_ref,
                    wg_sc, wu_sc, wd_sc):
    del sg_ref, su_ref, sd_ref
    j = pl.program_id(0)
    e = te_ref[j]
    e_prev = te_ref[jnp.maximum(j - 1, 0)]
    n_valid = nv_ref[j]
    half = y_ref.shape[0] // 2

    @pl.when((j == 0) | (e != e_prev))
    def _():
        wg_sc[...] = wg_ref[...].astype(BF16)
        wu_sc[...] = wu_ref[...].astype(BF16)
        wd_sc[...] = wd_ref[...].astype(BF16)

    def swiglu(rows):
        h = hs_ref[rows, :].astype(BF16)
        y = None
        for c0 in range(0, D_FF_EXPERT, MOE_CHUNK):
            c1 = min(c0 + MOE_CHUNK, D_FF_EXPERT)
            hg = jnp.dot(h, wg_sc[:, c0:c1], preferred_element_type=F32)
            hu = jnp.dot(h, wu_sc[:, c0:c1], preferred_element_type=F32)
            act = (_silu(hg) * hu).astype(BF16)
            yc = jnp.dot(act, wd_sc[c0:c1, :], preferred_element_type=F32)
            y = yc if y is None else y + yc
        y_ref[rows, :] = y

    @pl.when(n_valid > half)
    def _():
        swiglu(slice(None))

    @pl.when((n_valid > 0) & (n_valid <= half))
    def _():
        swiglu(slice(0, half))
        y_ref[half:, :] = jnp.zeros((half, D), F32)

    @pl.when(n_valid == 0)
    def _():
        y_ref[...] = jnp.zeros_like(y_ref)


def moe_experts(tile_expert, stages, tile_valid, hs, e_gate, e_up, e_down):
    rows = hs.shape[0]
    tmr = TM_EXPERT
    wspec = lambda shape, k: pl.BlockSpec((None,) + shape, lambda j, *pf: (pf[1 + k][j], 0, 0))
    return pl.pallas_call(
        _experts_kernel,
        out_shape=jax.ShapeDtypeStruct((rows, D), F32),
        grid_spec=pltpu.PrefetchScalarGridSpec(
            num_scalar_prefetch=5,
            grid=(rows // tmr,),
            in_specs=[pl.BlockSpec((tmr, D), lambda j, *pf: (j, 0)),
                      wspec((D, D_FF_EXPERT), 0), wspec((D, D_FF_EXPERT), 1), wspec((D_FF_EXPERT, D), 2)],
            out_specs=pl.BlockSpec((tmr, D), lambda j, *pf: (j, 0)),
            scratch_shapes=[pltpu.VMEM((D, D_FF_EXPERT), BF16), pltpu.VMEM((D, D_FF_EXPERT), BF16),
                            pltpu.VMEM((D_FF_EXPERT, D), BF16)]),
        compiler_params=_cparams(("arbitrary",)),
        name="moe_experts",
    )(tile_expert, *stages, tile_valid, hs, e_gate, e_up, e_down)


def _combine_kernel(pos_ref, x_ref, meta_ref, gt_ref, fg_ref, y_ref, o_ref, b1_sc, b2_sc, sem):
    tm = x_ref.shape[0]
    n_tok = pos_ref.shape[0] // 2
    i = pl.program_id(0)

    def gather(tile, slot):
        base = tile * tm

        def issue(r, carry):
            _row_copy(y_ref, pos_ref[base + r], b1_sc.at[slot], r, sem.at[slot]).start(priority=0)
            _row_copy(y_ref, pos_ref[n_tok + base + r], b2_sc.at[slot], r, sem.at[slot]).start(priority=1)
            return carry

        lax.fori_loop(0, tm, issue, 0, unroll=8)

    @pl.when(i == 0)
    def _():
        gather(0, 0)

    @pl.when(i + 1 < pl.num_programs(0))
    def _():
        gather(i + 1, (i + 1) % 2)

    slot = i % 2
    pltpu.make_async_copy(y_ref.at[pl.ds(0, tm)], b1_sc.at[slot], sem.at[slot]).wait()
    pltpu.make_async_copy(y_ref.at[pl.ds(0, tm)], b2_sc.at[slot], sem.at[slot]).wait()

    meta = meta_ref[...]
    lane = lax.broadcasted_iota(jnp.int32, meta.shape, 1)
    g1 = jnp.sum(jnp.where(lane == META_G1, meta, 0.0), axis=-1, keepdims=True)
    g2 = jnp.sum(jnp.where(lane == META_G2, meta, 0.0), axis=-1, keepdims=True)
    x = x_ref[...] + gt_ref[...] * (g1 * b1_sc[slot] + g2 * b2_sc[slot])
    o_ref[...] = _rms(x, fg_ref[...])


def moe_combine(pos, x, meta, modtab, cond, final_g, y):
    tokens = x.shape[0]
    tm = TM_COMBINE
    return pl.pallas_call(
        _combine_kernel,
        out_shape=jax.ShapeDtypeStruct((tokens, D), F32),
        grid_spec=pltpu.PrefetchScalarGridSpec(
            num_scalar_prefetch=1,
            grid=(tokens // tm,),
            in_specs=[pl.BlockSpec((tm, D), lambda i, pos: (i, 0)),
                      pl.BlockSpec((tm, V7X_LANES), lambda i, pos: (i, 0)),
                      _mod_spec(5, cond, tm, D, lambda i, pos: 0),
                      pl.BlockSpec((1, D), lambda i, pos: (0, 0)),
                      pl.BlockSpec(memory_space=pl.ANY)],
            out_specs=pl.BlockSpec((tm, D), lambda i, pos: (i, 0)),
            scratch_shapes=[pltpu.VMEM((2, tm, D), F32), pltpu.VMEM((2, tm, D), F32),
                            pltpu.SemaphoreType.DMA((2,))]),
        compiler_params=_cparams(("arbitrary",)),
        name="moe_combine",
    )(pos, x, meta, modtab, final_g.reshape(1, D), y)


def moe_plan(metas, counts):
    tmr = TM_EXPERT
    cnts = [c[0, :N_EXPERTS].astype(jnp.int32) for c in counts]
    total = functools.reduce(jnp.add, cnts)
    padded = ((total + tmr - 1) // tmr) * tmr
    ends = jnp.cumsum(padded)
    starts = ends - padded
    n_rows = sum(m.shape[1] for m in metas) * 2 + N_EXPERTS * tmr
    n_tiles = n_rows // tmr
    tile_start = jnp.arange(n_tiles, dtype=jnp.int32) * tmr
    tile_expert = jnp.minimum(jnp.sum(tile_start[:, None] >= ends[None, :], axis=1), N_EXPERTS - 1).astype(jnp.int32)
    group_of_tile = jnp.sum(tile_start[:, None] >= ends[None, :], axis=1)
    real_end = jnp.sum(jnp.where(group_of_tile[:, None] == jnp.arange(N_EXPERTS)[None, :],
                                 (starts + total)[None, :], 0), axis=1)
    tile_valid = jnp.clip(real_end - tile_start, 0, tmr).astype(jnp.int32)
    eid = jnp.arange(N_EXPERTS, dtype=jnp.int32)
    later = jnp.where((eid[None, :] > eid[:, None]) & (padded[None, :] > 0), eid[None, :], N_EXPERTS)
    nxt = jnp.min(later, axis=1)
    next_used = jnp.where(nxt == N_EXPERTS, eid, nxt)
    pick = lambda table: jnp.sum(jnp.where(tile_expert[:, None] == eid[None, :], table[None, :], 0), axis=1)
    k_in_group = (tile_start - pick(starts)) // tmr
    tile_next = pick(next_used)
    stages = [jnp.where(k_in_group < k, tile_expert, tile_next).astype(jnp.int32) for k in (1, 2, 3)]
    pos, p1s, p2s = [], [], []
    base = jnp.zeros((N_EXPERTS,), jnp.int32)
    for m, c in zip(metas, cnts):
        first = starts + base
        sel = lambda field: m[field].astype(jnp.int32)
        lookup = lambda e: jnp.sum(jnp.where(e[:, None] == jnp.arange(N_EXPERTS)[None, :], first[None, :], 0), axis=1)
        p1 = lookup(sel(META_E1)) + sel(META_R1)
        p2 = lookup(sel(META_E2)) + sel(META_R2)
        pos.append(jnp.concatenate([p1, p2]).astype(jnp.int32))
        p1s.append(p1)
        p2s.append(p2)
        base = base + c
    pos_all = jnp.concatenate(p1s + p2s).astype(jnp.int32)
    pads = jnp.concatenate([starts + total, padded - total,
                            jnp.stack([ends[-1], n_tiles - ends[-1] // tmr])]).astype(jnp.int32)
    return pos, pos_all, pads, tile_expert, stages, tile_valid, n_rows


def _pad_to(a, shape):
    return jnp.pad(a, [(0, t - s) for s, t in zip(a.shape, shape)])


def _regroup_chunks(a, cb):
    r = a.shape[0]
    return a.reshape(r, 3, HY_W // cb, cb).transpose(2, 0, 1, 3).reshape(HY_W // cb, r, 3 * cb)


def kernel(x_prompt, x_sample, state_l0_lru, cache_l1_ckv, cache_l1_krope, c, c_ctx, l0_norm1, l0_norm2, l0_w_mod, l0_b_mod, l0_w_in, l0_conv_a, l0_lru_conv_w, l0_lru_conv_b, l0_lru_wa, l0_lru_ba, l0_lru_wi, l0_lru_bi, l0_lru_lambda, l0_w_out, l0_ffn_gate, l0_ffn_up, l0_ffn_down, l1_norm1, l1_norm2, l1_w_mod, l1_b_mod, l1_w_in, l1_q_norm, l1_kv_norm, l1_w_q_up, l1_w_kv_up, l1_hy_short_w, l1_hy_short_b, l1_hy_f_w1, l1_hy_f_b1, l1_hy_f_w2, l1_hy_f_b2, l1_hy_f_w3, l1_hy_bias, l1_w_out, l1_router_w, l1_router_b, l1_exp_gate, l1_exp_up, l1_exp_down, final_norm):
    batch, seq, _ = x_prompt.shape
    dec_batch, dec_seq, _ = x_sample.shape
    past_len = cache_l1_ckv.shape[1]

    cond8 = jnp.concatenate([c_ctx[None, :], c, jnp.zeros((V7X_SUBLANES - 1 - dec_batch, D), F32)], axis=0)
    wcat = jnp.concatenate([l0_lru_wa[0], l0_lru_wi[0], l0_lru_wa[1], l0_lru_wi[1]], axis=-1)
    hid = V7X_LANES
    w1p = _pad_to(l1_hy_f_w1, (hid, hid))
    b1p = _pad_to(l1_hy_f_b1.reshape(1, -1), (1, hid))
    w2p = _pad_to(l1_hy_f_w2, (hid, hid))
    b2p = _pad_to(l1_hy_f_b2.reshape(1, -1), (1, hid))
    w3p = _pad_to(l1_hy_f_w3, (hid, 2 * HY_W))
    short_w = _regroup_chunks(l1_hy_short_w, HY_CB)
    short_b = _regroup_chunks(l1_hy_short_b.reshape(1, -1), HY_CB)
    hy_bias = l1_hy_bias.reshape(1, HY_W)
    wr_pad = _pad_to(l1_router_w, (D, V7X_LANES))
    br_pad = _pad_to(l1_router_b.reshape(1, -1), (1, V7X_LANES))

    mod0, mod1 = adaln_tables(cond8, ((l0_w_mod, l0_b_mod), (l1_w_mod, l1_b_mod)))

    kv_ctx = kv_up(cache_l1_ckv.reshape(dec_batch * past_len, KV_RANK), l1_w_kv_up)
    kr_ctx = cache_l1_krope.reshape(dec_batch * past_len, ROPE)

    conds = ((0, batch * seq), (1, dec_seq))
    seq_lens = (seq, dec_seq)
    xs = (x_prompt.reshape(batch * seq, D), x_sample.reshape(dec_batch * dec_seq, D))
    h0s = (jnp.zeros((batch, 2, LRU_W), F32), state_l0_lru)

    us = in0_proj(xs, l0_norm1, mod0, conds, l0_w_in)
    parts, lru_states = [], []
    for u, seq_len, h0 in zip(us, seq_lens, h0s):
        ya = conv_a(u, seq_len, l0_conv_a)
        yb, lru_state = rglru(u, seq_len, l0_lru_conv_w, l0_lru_conv_b, wcat, l0_lru_ba, l0_lru_bi,
                              l0_lru_lambda, h0)
        parts.append([(ya, 0), (yb, 0), (yb, 1)])
        lru_states.append(lru_state)
    xs = mix_ffn(parts, l0_w_out, xs, l0_norm2, mod0, conds, l0_ffn_gate, l0_ffn_up, l0_ffn_down)
    new_lru = lru_states[0]

    def layer1(x, seq_len, cond, latent):
        qnope, qpe, ckv, kr, kv, uh = in1_proj(x, l1_norm1, mod1, cond, l1_w_in, l1_q_norm, l1_kv_norm,
                                               l1_w_q_up, l1_w_kv_up)
        if latent:
            yc = attn_lat(qnope, qpe, kv_ctx, kr_ctx, kv, kr, seq_len, past_len)
        else:
            yc = attn_ctx(qnope, qpe, kv, kr, seq_len)
        cs = dft_tables(seq_len)
        k_r, k_s, k_ny = hy_filter(cs, w1p, b1p, w2p, b2p, w3p)
        yd = hyena(uh, seq_len, short_w, short_b, cs, k_r, k_s, k_ny, hy_bias)
        routed = mix_route([(yc, 0), (yc, 1), (yd, 0)], l1_w_out, x, l1_norm2, mod1, cond, wr_pad, br_pad)
        return routed, ckv, kr

    r_p, new_ckv, new_kr = layer1(xs[0], seq, conds[0], latent=False)
    r_s, _, _ = layer1(xs[1], dec_seq, conds[1], latent=True)

    routed = (r_p, r_s)
    pos, pos_all, pads, tile_expert, stages, tile_valid, n_rows = moe_plan([r[3] for r in routed],
                                                                          [r[4] for r in routed])
    hs = moe_dispatch(pos_all, pads, n_rows, r_p[1], r_s[1])
    y_rows = moe_experts(tile_expert, stages, tile_valid, hs, l1_exp_gate, l1_exp_up, l1_exp_down)
    y_p, y_s = [moe_combine(p, r[0], r[2], mod1, cond, final_norm, y_rows)
                for p, r, cond in zip(pos, routed, conds)]
    return (y_p.reshape(batch, seq, D), y_s.reshape(dec_batch, dec_seq, D), new_lru,
            new_ckv.reshape(batch, seq, KV_RANK), new_kr.reshape(batch, seq, ROPE))
```

```python
import functools
import math

import jax
import jax.numpy as jnp
from jax import lax
from jax.experimental import pallas as pl
from jax.experimental.pallas import tpu as pltpu

F32 = jnp.float32
BF16 = jnp.bfloat16
HIGHEST = lax.Precision.HIGHEST

D = 1024
GRID_W = 64
EPS = 1e-6
CONV_W = 512
LRU_W = 1024
LRU_BW = 128
LRU_C = 8.0
MLA_HEADS = 8
Q_RANK = 384
KV_RANK = 256
NOPE = 128
ROPE = 64
VDIM = 128
QK_DIM = NOPE + ROPE
ROPE_THETA = 10000.0
HY_W = 512
HY_BANDS = 16
HY_TARGET = 1e-2
HY_FAST_DECAY = 0.3
HY_SLOW_DECAY = 1.5
D_FF = 2816
N_EXPERTS = 8
D_FF_EXPERT = 1408
IN0 = 3 * CONV_W + 2 * LRU_W
IN1 = Q_RANK + KV_RANK + ROPE + 3 * HY_W

V7X_LANES = 128
V7X_SUBLANES = 8
V7X_VMEM_LIMIT_BYTES = 56 * 1024 * 1024
V7X_VMEM_LIMIT_LARGE_BYTES = 60 * 1024 * 1024

TM = 512
TN_IN0 = 512
TF_FFN = 256
MOE_CHUNK = 256
TM_ROUTE = 512
TM_EXPERT = 512
TM_COMBINE = 512
LRU_CB = 256
HY_CB = 256
TQ = 256
ATTN_CTX_SEQS = 4
CONV_A_ROWS = 1024
LRU_ROWS = 1024
HY_ROWS = 1024
TM_IN1 = 512


def _cparams(sem, vmem_limit_bytes=V7X_VMEM_LIMIT_BYTES):
    return pltpu.CompilerParams(dimension_semantics=sem, vmem_limit_bytes=vmem_limit_bytes)


def _sigmoid(x):
    return 0.5 * jnp.tanh(0.5 * x) + 0.5


def _silu(x):
    return x * _sigmoid(x)


def _norm_mod(x, g, shift, scale):
    ms = jnp.mean(x * x, axis=-1, keepdims=True)
    y = x * lax.rsqrt(ms + EPS) * g
    return y * (1.0 + scale) + shift


def _mod_spec(comp, cond, tm, width, col_fn, tile_fn=lambda *ids: ids[0]):
    row0, seg = cond
    assert seg % tm == 0
    return pl.BlockSpec((None, 1, width),
                        lambda *ids: (comp * 3 + row0 + (tile_fn(*ids) * tm) // seg, 0, col_fn(*ids)))


def _dot3(a, b):
    a_hi = a.astype(BF16)
    a_lo = (a - a_hi.astype(F32)).astype(BF16)
    b_hi = b.astype(BF16)
    b_lo = (b - b_hi.astype(F32)).astype(BF16)
    n = a.shape[0]
    y = jnp.dot(jnp.concatenate([a_hi, a_lo], axis=0), b_hi, preferred_element_type=F32)
    return y[:n] + y[n:] + jnp.dot(a_hi, b_lo, preferred_element_type=F32)


def _adaln_kernel(c_ref, w0_ref, b0_ref, w1_ref, b1_ref, o_ref):
    layer0 = pl.program_id(0) == 0
    w = jnp.where(layer0, w0_ref[...], w1_ref[...])
    b = jnp.where(layer0, b0_ref[...], b1_ref[...])
    o_ref[...] = _dot3(_silu(c_ref[...]), w) + b


def adaln_tables(cond8, mods):
    tn = 1536
    nj = 6 * D // tn
    (w0, b0), (w1, b1) = mods
    at0 = lambda l, j: (0, jnp.where(l == 0, j, nj - 1))
    at1 = lambda l, j: (0, jnp.where(l == 1, j, 0))
    m = pl.pallas_call(
        _adaln_kernel,
        out_shape=jax.ShapeDtypeStruct((2, V7X_SUBLANES, 6 * D), F32),
        grid=(2, nj),
        in_specs=[pl.BlockSpec((V7X_SUBLANES, D), lambda l, j: (0, 0)),
                  pl.BlockSpec((D, tn), at0), pl.BlockSpec((1, tn), at0),
                  pl.BlockSpec((D, tn), at1), pl.BlockSpec((1, tn), at1)],
        out_specs=pl.BlockSpec((None, V7X_SUBLANES, tn), lambda l, j: (l, 0, j)),
        compiler_params=_cparams(("arbitrary", "arbitrary")),
        name="adaln",
    )(cond8, w0, b0.reshape(1, 6 * D), w1, b1.reshape(1, 6 * D))
    return [m[l, :3].reshape(3, 6, D).transpose(1, 0, 2).reshape(18, 1, D) for l in range(2)]


def _tile_of(n_load):
    return lambda s: jnp.maximum(s - n_load, 0)


def _block_of(n_load):
    return lambda s: jnp.minimum(s, n_load - 1)


class _TwoSets:
    def __init__(self, n_load, tm, tokens, conds):
        self.n_load, self.tm, self.conds = n_load, tm, conds
        self.n_a, self.n_b = tokens[0] // tm, tokens[1] // tm
        self.steps = n_load + self.n_a + self.n_b

    def tile(self, s):
        return jnp.maximum(s - self.n_load, 0)

    def in_first(self, s):
        return s - self.n_load < self.n_a

    def idx_a(self, s):
        return jnp.minimum(self.tile(s), self.n_a - 1)

    def idx_b(self, s):
        return jnp.clip(self.tile(s) - self.n_a, 0, self.n_b - 1)

    def rows(self, width):
        return (pl.BlockSpec((self.tm, width), lambda s: (self.idx_a(s), 0)),
                pl.BlockSpec((self.tm, width), lambda s: (self.idx_b(s), 0)))

    def cols(self, width, col):
        return (pl.BlockSpec((self.tm, width), lambda s: (self.idx_a(s), col)),
                pl.BlockSpec((self.tm, width), lambda s: (self.idx_b(s), col)))

    def mod_spec(self, comp):
        (row_a, seg_a), (row_b, seg_b) = self.conds
        assert seg_a % self.tm == 0 and seg_b % self.tm == 0

        def row(s):
            return jnp.where(self.in_first(s), row_a + (self.idx_a(s) * self.tm) // seg_a,
                             row_b + (self.idx_b(s) * self.tm) // seg_b)

        return pl.BlockSpec((None, 1, D), lambda s: (comp * 3 + row(s), 0, 0))


def _in0_kernel(xa_ref, xb_ref, g_ref, sh_ref, sc_ref, w_ref, oa_ref, ob_ref, w_sc, *, n_a):
    s = pl.program_id(0)
    n_load = w_sc.shape[0]

    @pl.when(s < n_load)
    def _():
        w_sc[s] = w_ref[...].astype(BF16)

    @pl.when(s >= n_load)
    def _():
        first = s - n_load < n_a
        x = jnp.where(first, xa_ref[...], xb_ref[...])
        h = _norm_mod(x, g_ref[...], sh_ref[...], sc_ref[...]).astype(BF16)
        u = jnp.concatenate([jnp.dot(h, w_sc[j], preferred_element_type=F32).astype(BF16)
                             for j in range(n_load)], axis=1)

        @pl.when(first)
        def _():
            oa_ref[...] = u

        @pl.when(jnp.logical_not(first))
        def _():
            ob_ref[...] = u


def in0_proj(xs, g, modtab, conds, w_in):
    tn = TN_IN0
    n = w_in.shape[1]
    n_load = n // tn
    ts = _TwoSets(n_load, TM, [x.shape[0] for x in xs], conds)
    blk = _block_of(n_load)
    return pl.pallas_call(
        functools.partial(_in0_kernel, n_a=ts.n_a),
        out_shape=tuple(jax.ShapeDtypeStruct((x.shape[0], n), BF16) for x in xs),
        grid=(ts.steps,),
        in_specs=[*ts.rows(D),
                  pl.BlockSpec((1, D), lambda s: (0, 0)),
                  ts.mod_spec(0), ts.mod_spec(1),
                  pl.BlockSpec((D, tn), lambda s: (0, blk(s)))],
        out_specs=ts.rows(n),
        scratch_shapes=[pltpu.VMEM((n_load, D, tn), BF16)],
        compiler_params=_cparams(("arbitrary",)),
        name="in0_proj",
    )(*xs, g.reshape(1, D), modtab, modtab, w_in)


def _shift_rows(v, d, t, seq_len=None):
    n = v.shape[0]
    seq_len = n if seq_len is None else seq_len
    if d > 0:
        return jnp.where(t < d, 0.0, pltpu.roll(v, d, 0))
    return jnp.where(t >= seq_len + d, 0.0, pltpu.roll(v, n + d, 0))


def _conv_a_kernel(b_ref, c_ref, x_ref, w_ref, o_ref, *, seq_len):
    v = c_ref[...].astype(F32) * x_ref[...].astype(F32)
    t = lax.broadcasted_iota(jnp.int32, v.shape, 0) & (seq_len - 1)
    w = w_ref[...]
    y = w[0:1] * _shift_rows(v, 1, t, seq_len) + w[1:2] * v + w[2:3] * _shift_rows(v, -1, t, seq_len)
    o_ref[...] = (b_ref[...].astype(F32) * y).astype(o_ref.dtype)


def conv_a(u, seq_len, conv_w):
    tokens = u.shape[0]
    rows = max(seq_len, CONV_A_ROWS)
    assert seq_len & (seq_len - 1) == 0 and rows % seq_len == 0
    return pl.pallas_call(
        functools.partial(_conv_a_kernel, seq_len=seq_len),
        out_shape=jax.ShapeDtypeStruct((tokens, CONV_W), BF16),
        grid=(tokens // rows,),
        in_specs=[pl.BlockSpec((rows, CONV_W), lambda s: (s, 0)),
                  pl.BlockSpec((rows, CONV_W), lambda s: (s, 1)),
                  pl.BlockSpec((rows, CONV_W), lambda s: (s, 2)),
                  pl.BlockSpec((3, CONV_W), lambda s: (0, 0))],
        out_specs=pl.BlockSpec((rows, CONV_W), lambda s: (s, 0)),
        compiler_params=_cparams(("parallel",)),
        name="conv_a",
    )(u, u, u, conv_w)


def _group_scan(a_sc, b_sc, k, reverse):
    planes = a_sc.shape[1] // V7X_SUBLANES
    order = range(V7X_SUBLANES - 1, -1, -1) if reverse else range(V7X_SUBLANES)
    a_acc = b_acc = None
    for r in order:
        plane = (k, pl.ds(r, planes, stride=V7X_SUBLANES), slice(None))
        a_r, b_r = a_sc[plane], b_sc[plane]
        if a_acc is None:
            a_acc, b_acc = a_r, b_r
        else:
            b_acc = a_r * b_acc + b_r
            a_acc = a_r * a_acc
            a_sc[plane] = a_acc
            b_sc[plane] = b_acc


def _rglru_kernel(gate_ref, xb_ref, cw_ref, cb_ref, wcat_ref, ba_ref, bi_ref, lam_ref, h0_ref,
                  y_ref, st_ref, af_sc, bf_sc, ab_sc, bb_sc, hf_sc, hb_sc, *, seq_len):
    n, cb = xb_ref.shape
    n_seq = n // seq_len
    n_slab = cb // LRU_BW
    xb = xb_ref[...].astype(F32)
    t = lax.broadcasted_iota(jnp.int32, xb.shape, 0) & (seq_len - 1)
    cw = cw_ref[...]
    sh = lambda d: _shift_rows(xb, d, t, seq_len)
    xc = cb_ref[...] + cw[0:1] * sh(2) + cw[1:2] * sh(1) + cw[2:3] * xb + cw[3:4] * sh(-1)
    xcb = xc.astype(BF16)

    for k in range(n_slab):
        cols = slice(k * LRU_BW, (k + 1) * LRU_BW)
        gk = jnp.dot(xcb[:, cols], wcat_ref[k].astype(BF16), preferred_element_type=F32)
        for d, (a_sc, b_sc) in enumerate(((af_sc, bf_sc), (ab_sc, bb_sc))):
            ga = gk[:, (2 * d) * LRU_BW:(2 * d + 1) * LRU_BW]
            gi = gk[:, (2 * d + 1) * LRU_BW:(2 * d + 2) * LRU_BW]
            r = _sigmoid(ga + ba_ref[d:d + 1, cols])
            i = _sigmoid(gi + bi_ref[d:d + 1, cols])
            log_a = (-LRU_C * jax.nn.softplus(-lam_ref[d:d + 1, cols])) * r
            a = jnp.exp(log_a)
            m = 1.0 - a * a
            mult = m * lax.rsqrt(jnp.maximum(m, 1e-30))
            a_sc[k] = a
            b_sc[k] = mult * (i * xc[:, cols])
            _group_scan(a_sc, b_sc, k, reverse=(d == 1))

    ng = seq_len // V7X_SUBLANES
    bcast = lambda row: jnp.broadcast_to(row, (V7X_SUBLANES, LRU_BW))
    chains = [(q, k) for q in range(n_seq) for k in range(n_slab)]
    init = tuple((bcast(h0_ref[q, 0:1, k * LRU_BW:(k + 1) * LRU_BW]),
                  bcast(h0_ref[q, 1:2, k * LRU_BW:(k + 1) * LRU_BW])) for q, k in chains)

    def step(j, carry):
        out = []
        for (q, k), (hf_in, hb_in) in zip(chains, carry):
            rf = pl.ds(pl.multiple_of(q * seq_len + j * V7X_SUBLANES, V7X_SUBLANES), V7X_SUBLANES)
            rb = pl.ds(pl.multiple_of(q * seq_len + (ng - 1 - j) * V7X_SUBLANES, V7X_SUBLANES), V7X_SUBLANES)
            hf = af_sc[k, rf, :] * hf_in + bf_sc[k, rf, :]
            hb = ab_sc[k, rb, :] * hb_in + bb_sc[k, rb, :]
            hf_sc[k, rf, :] = hf
            hb_sc[k, rb, :] = hb
            out.append((bcast(hf[V7X_SUBLANES - 1:V7X_SUBLANES]), bcast(hb[0:1])))
        return tuple(out)

    final = lax.fori_loop(0, ng, step, init)
    for (q, k), (hf_last, hb_first) in zip(chains, final):
        st_ref[q, 0:1, k * LRU_BW:(k + 1) * LRU_BW] = hf_last[0:1]
        st_ref[q, 1:2, k * LRU_BW:(k + 1) * LRU_BW] = hb_first[0:1]

    gt = gate_ref[...].astype(F32)
    gelu = 0.5 * gt * (1.0 + jnp.tanh(math.sqrt(2.0 / math.pi) * (gt + 0.044715 * (gt * gt * gt))))
    h = jnp.concatenate([hf_sc[k] + hb_sc[k] for k in range(n_slab)], axis=1)
    y_ref[...] = (h * gelu).astype(y_ref.dtype)


def rglru(u, seq_len, conv_w, conv_b, wcat, ba, bi, lam, h0):
    tokens = u.shape[0]
    nseq = tokens // seq_len
    cb = LRU_CB
    rows = max(seq_len, LRU_ROWS)
    assert seq_len & (seq_len - 1) == 0 and rows % seq_len == 0
    per_blk = rows // seq_len
    gate_blk0 = 3 * CONV_W // cb
    xb_blk0 = (3 * CONV_W + LRU_W) // cb
    seq_scr = lambda: pltpu.VMEM((cb // LRU_BW, rows, LRU_BW), F32)
    return pl.pallas_call(
        functools.partial(_rglru_kernel, seq_len=seq_len),
        out_shape=(jax.ShapeDtypeStruct((tokens, LRU_W), BF16), jax.ShapeDtypeStruct((nseq, 2, LRU_W), F32)),
        grid=(tokens // rows, LRU_W // cb),
        in_specs=[pl.BlockSpec((rows, cb), lambda s, c: (s, gate_blk0 + c)),
                  pl.BlockSpec((rows, cb), lambda s, c: (s, xb_blk0 + c)),
                  pl.BlockSpec((4, cb), lambda s, c: (0, c)),
                  pl.BlockSpec((1, cb), lambda s, c: (0, c)),
                  pl.BlockSpec((cb // LRU_BW, LRU_BW, 4 * LRU_BW), lambda s, c: (c, 0, 0)),
                  pl.BlockSpec((2, cb), lambda s, c: (0, c)),
                  pl.BlockSpec((2, cb), lambda s, c: (0, c)),
                  pl.BlockSpec((2, cb), lambda s, c: (0, c)),
                  pl.BlockSpec((per_blk, 2, cb), lambda s, c: (s, 0, c))],
        out_specs=(pl.BlockSpec((rows, cb), lambda s, c: (s, c)),
                   pl.BlockSpec((per_blk, 2, cb), lambda s, c: (s, 0, c))),
        scratch_shapes=[seq_scr() for _ in range(6)],
        compiler_params=_cparams(("parallel", "parallel")),
        name="rglru",
    )(u, u, conv_w, conv_b.reshape(1, LRU_W), wcat, ba, bi, lam, h0)


def _mix_ffn_kernel(p0a_ref, p0b_ref, p1a_ref, p1b_ref, p2a_ref, p2b_ref, wo_ref, xa_ref, xb_ref,
                    g1_ref, g_ref, sh_ref, sc_ref, g2_ref, wg_ref, wu_ref, wd_ref, oa_ref, ob_ref,
                    wo_sc, wg_sc, wu_sc, wd_sc, *, n_a):
    s = pl.program_id(0)
    n_load = wg_sc.shape[0]
    n_out = wo_sc.shape[0]

    @pl.when(s < n_out)
    def _():
        wo_sc[s] = wo_ref[...].astype(BF16)

    @pl.when(s < n_load)
    def _():
        wg_sc[s] = wg_ref[...].astype(BF16)
        wu_sc[s] = wu_ref[...].astype(BF16)
        wd_sc[s] = wd_ref[...].astype(BF16)

    @pl.when(s >= n_load)
    def _():
        first = s - n_load < n_a
        pick = lambda a_ref, b_ref: jnp.where(first, a_ref[...], b_ref[...])
        m = jnp.dot(pick(p0a_ref, p0b_ref), wo_sc[0], preferred_element_type=F32)
        m += jnp.dot(pick(p1a_ref, p1b_ref), wo_sc[1], preferred_element_type=F32)
        m += jnp.dot(pick(p2a_ref, p2b_ref), wo_sc[2], preferred_element_type=F32)
        x = pick(xa_ref, xb_ref) + g1_ref[...] * m
        h = _norm_mod(x, g_ref[...], sh_ref[...], sc_ref[...]).astype(BF16)
        y = None
        for f in range(n_load):
            hg = jnp.dot(h, wg_sc[f], preferred_element_type=F32)
            hu = jnp.dot(h, wu_sc[f], preferred_element_type=F32)
            act = (_silu(hg) * hu).astype(BF16)
            yf = jnp.dot(act, wd_sc[f], preferred_element_type=F32)
            y = yf if y is None else y + yf
        out = x + g2_ref[...] * y

        @pl.when(first)
        def _():
            oa_ref[...] = out

        @pl.when(jnp.logical_not(first))
        def _():
            ob_ref[...] = out


def mix_ffn(parts, w_out, xs, g, modtab, conds, w_gate, w_up, w_down):
    tf = TF_FFN
    kb = 512
    n_load = D_FF // tf
    n_out = len(parts[0])
    assert n_out <= n_load
    ts = _TwoSets(n_load, TM, [x.shape[0] for x in xs], conds)
    blk = _block_of(n_load)
    oblk = _block_of(n_out)
    lhs_specs, lhs_args = [], []
    for (arr_a, col_a), (arr_b, col_b) in zip(*parts):
        assert col_a == col_b
        lhs_specs += ts.cols(kb, col_a)
        lhs_args += [arr_a, arr_b]
    return pl.pallas_call(
        functools.partial(_mix_ffn_kernel, n_a=ts.n_a),
        out_shape=tuple(jax.ShapeDtypeStruct(x.shape, F32) for x in xs),
        grid=(ts.steps,),
        in_specs=lhs_specs + [
            pl.BlockSpec((kb, D), lambda s: (oblk(s), 0)),
            *ts.rows(D),
            ts.mod_spec(2),
            pl.BlockSpec((1, D), lambda s: (0, 0)),
            ts.mod_spec(3), ts.mod_spec(4), ts.mod_spec(5),
            pl.BlockSpec((D, tf), lambda s: (0, blk(s))),
            pl.BlockSpec((D, tf), lambda s: (0, blk(s))),
            pl.BlockSpec((tf, D), lambda s: (blk(s), 0))],
        out_specs=ts.rows(D),
        scratch_shapes=[pltpu.VMEM((n_out, kb, D), BF16),
                        pltpu.VMEM((n_load, D, tf), BF16), pltpu.VMEM((n_load, D, tf), BF16),
                        pltpu.VMEM((n_load, tf, D), BF16)],
        compiler_params=_cparams(("arbitrary",), V7X_VMEM_LIMIT_LARGE_BYTES),
        name="mix_ffn",
    )(*lhs_args, w_out, *xs, modtab, g.reshape(1, D), modtab, modtab, modtab, w_gate, w_up, w_down)


def _rms(x, g):
    return x * lax.rsqrt(jnp.mean(x * x, axis=-1, keepdims=True) + EPS) * g


def _in1_kernel(x_ref, g_ref, sh_ref, sc_ref, w_ref, qn_ref, kvn_ref, wq_ref, wkv_ref,
                qnope_ref, qpe_ref, ckv_ref, kr_ref, kv_ref, uh_ref, w_sc, wq_sc, wkv_sc):
    @pl.when(pl.program_id(0) == 0)
    def _():
        w_sc[...] = w_ref[...].astype(BF16)
        for h in range(MLA_HEADS):
            c0 = h * QK_DIM
            wq_sc[:, h * NOPE:(h + 1) * NOPE] = wq_ref[:, c0:c0 + NOPE].astype(BF16)
            r0 = MLA_HEADS * NOPE + h * ROPE
            wq_sc[:, r0:r0 + ROPE] = wq_ref[:, c0 + NOPE:c0 + QK_DIM].astype(BF16)
        wkv_sc[...] = wkv_ref[...].astype(BF16)

    h = _norm_mod(x_ref[...], g_ref[...], sh_ref[...], sc_ref[...]).astype(BF16)
    u = lax.dot_general(h, w_sc[...], (((1,), (1,)), ((), ())), preferred_element_type=F32)
    o1, o2, o3 = Q_RANK, Q_RANK + KV_RANK, Q_RANK + KV_RANK + ROPE
    cq = _rms(u[:, :o1], qn_ref[...])
    q = jnp.dot(cq.astype(BF16), wq_sc[...], preferred_element_type=F32) * _SCALE
    qnope_ref[...] = q[:, :MLA_HEADS * NOPE].astype(qnope_ref.dtype)
    qpe_ref[...] = q[:, MLA_HEADS * NOPE:]
    ckv = _rms(u[:, o1:o2], kvn_ref[...])
    ckv_ref[...] = ckv
    kv_ref[...] = jnp.dot(ckv.astype(BF16), wkv_sc[...], preferred_element_type=F32).astype(kv_ref.dtype)
    kr_ref[...] = u[:, o2:o3]
    uh_ref[...] = u[:, o3:]


def in1_proj(x, g, modtab, cond, w_in, q_norm, kv_norm, w_q_up, w_kv_up):
    tokens = x.shape[0]
    tm = TM_IN1
    nkv = MLA_HEADS * (NOPE + VDIM)
    const = lambda i: (0, 0)
    zero = lambda i: 0
    once = pl.Buffered(1)
    outs = (jax.ShapeDtypeStruct((tokens, MLA_HEADS * NOPE), BF16),
            jax.ShapeDtypeStruct((tokens, MLA_HEADS * ROPE), F32),
            jax.ShapeDtypeStruct((tokens, KV_RANK), F32),
            jax.ShapeDtypeStruct((tokens, ROPE), F32),
            jax.ShapeDtypeStruct((tokens, nkv), BF16),
            jax.ShapeDtypeStruct((tokens, 3 * HY_W), F32))
    row = lambda w: pl.BlockSpec((tm, w), lambda i: (i, 0))
    return pl.pallas_call(
        _in1_kernel,
        out_shape=outs,
        grid=(tokens // tm,),
        in_specs=[row(D),
                  pl.BlockSpec((1, D), const),
                  _mod_spec(0, cond, tm, D, zero),
                  _mod_spec(1, cond, tm, D, zero),
                  pl.BlockSpec((IN1, D), const, pipeline_mode=once),
                  pl.BlockSpec((1, Q_RANK), const),
                  pl.BlockSpec((1, KV_RANK), const),
                  pl.BlockSpec((Q_RANK, MLA_HEADS * QK_DIM), const, pipeline_mode=once),
                  pl.BlockSpec((KV_RANK, nkv), const, pipeline_mode=once)],
        out_specs=tuple(row(o.shape[1]) for o in outs),
        scratch_shapes=[pltpu.VMEM((IN1, D), BF16), pltpu.VMEM((Q_RANK, MLA_HEADS * QK_DIM), BF16),
                        pltpu.VMEM((KV_RANK, nkv), BF16)],
        compiler_params=_cparams(("arbitrary",)),
        name="in1_proj",
    )(x, g.reshape(1, D), modtab, modtab, w_in.T, q_norm.reshape(1, Q_RANK), kv_norm.reshape(1, KV_RANK),
      w_q_up, w_kv_up)


def _mm_kernel(a_ref, w_ref, o_ref):
    o_ref[...] = jnp.dot(a_ref[...].astype(BF16), w_ref[...].astype(BF16),
                         preferred_element_type=F32).astype(o_ref.dtype)


def kv_up(ckv, w_kv_up):
    rows = ckv.shape[0]
    n = w_kv_up.shape[1]
    return pl.pallas_call(
        _mm_kernel,
        out_shape=jax.ShapeDtypeStruct((rows, n), BF16),
        grid=(rows // TM,),
        in_specs=[pl.BlockSpec((TM, KV_RANK), lambda i: (i, 0)), pl.BlockSpec((KV_RANK, n), lambda i: (0, 0))],
        out_specs=pl.BlockSpec((TM, n), lambda i: (i, 0)),
        compiler_params=_cparams(("parallel",)),
        name="kv_up",
    )(ckv, w_kv_up)


_NT = (((1,), (1,)), ((), ()))
_SCALE = 1.0 / math.sqrt(QK_DIM)


def _fill_rope_tables(cos_ref, sin_ref):
    n, width = cos_ref.shape
    n_grid_rows = n // GRID_W
    n_freq = ROPE // 4

    def trig(count):
        lane = lax.broadcasted_iota(jnp.int32, (count, width), 1)
        j = lane & (ROPE // 2 - 1)
        inv = jnp.exp((j & (n_freq - 1)).astype(F32) * (-math.log(ROPE_THETA) / n_freq))
        ang = lax.broadcasted_iota(jnp.int32, (count, width), 0).astype(F32) * inv
        return jnp.cos(ang), jnp.sin(ang), j < n_freq

    cos_c, sin_c, by_row = trig(GRID_W)
    cos_r, sin_r, _ = trig(n_grid_rows)
    for r in range(n_grid_rows):
        rows = slice(r * GRID_W, (r + 1) * GRID_W)
        cos_ref[rows, :] = jnp.where(by_row, jnp.broadcast_to(cos_r[r:r + 1], cos_c.shape), cos_c)
        sin_ref[rows, :] = jnp.where(by_row, jnp.broadcast_to(sin_r[r:r + 1], sin_c.shape), sin_c)


def _rope(x, cos, sin):
    width = x.shape[1]
    lane = lax.broadcasted_iota(jnp.int32, x.shape, 1)
    first_half = (lane & (ROPE - 1)) < ROPE // 2
    xr = jnp.where(first_half, -pltpu.roll(x, width - ROPE // 2, 1), pltpu.roll(x, ROPE // 2, 1))
    return x * cos + xr * sin


def _ones_column(n):
    lane = lax.broadcasted_iota(jnp.int32, (n, VDIM), 1)
    return jnp.where(lane == 0, 1.0, 0.0).astype(BF16)


def _head_attention(qcat, kcat, vaug):
    s = lax.dot_general(qcat, kcat, _NT, preferred_element_type=F32)
    p = jnp.exp(s - jnp.max(s, axis=-1, keepdims=True)).astype(BF16)
    oa = jnp.dot(p, vaug, preferred_element_type=F32)
    return oa[:, :VDIM] / oa[:, VDIM:VDIM + 1]


def _attn_ctx_kernel(qn_ref, qpe_ref, kv_ref, kr_ref, o_ref, *, seq_len):
    n = qn_ref.shape[0]
    n_seq = n // seq_len
    ones = _ones_column(n)
    kpe = kr_ref[...].astype(BF16)
    per_seq = lambda a: a.reshape(n_seq, seq_len, a.shape[-1])
    for h in range(MLA_HEADS):
        c0 = h * (NOPE + VDIM)
        qcat = per_seq(jnp.concatenate([qn_ref[:, h * NOPE:(h + 1) * NOPE],
                                        qpe_ref[:, h * ROPE:(h + 1) * ROPE].astype(BF16)], axis=1))
        kcat = per_seq(jnp.concatenate([kv_ref[:, c0:c0 + NOPE], kpe], axis=1))
        vaug = per_seq(jnp.concatenate([kv_ref[:, c0 + NOPE:c0 + NOPE + VDIM], ones], axis=1))
        s = jnp.einsum("bqd,bkd->bqk", qcat, kcat, preferred_element_type=F32)
        p = jnp.exp(s - jnp.max(s, axis=-1, keepdims=True)).astype(BF16)
        oa = jnp.einsum("bqk,bkd->bqd", p, vaug, preferred_element_type=F32)
        o = oa[:, :, :VDIM] / oa[:, :, VDIM:VDIM + 1]
        o_ref[:, h * VDIM:(h + 1) * VDIM] = o.reshape(n, VDIM).astype(o_ref.dtype)


def attn_ctx(qnope, qpe, kv, kr, seq_len):
    tokens = qnope.shape[0]
    rows = ATTN_CTX_SEQS * seq_len
    blk = lambda w: pl.BlockSpec((rows, w), lambda s: (s, 0))
    return pl.pallas_call(
        functools.partial(_attn_ctx_kernel, seq_len=seq_len),
        out_shape=jax.ShapeDtypeStruct((tokens, MLA_HEADS * VDIM), BF16),
        grid=(tokens // rows,),
        in_specs=[blk(MLA_HEADS * NOPE), blk(MLA_HEADS * ROPE), blk(MLA_HEADS * (NOPE + VDIM)), blk(ROPE)],
        out_specs=blk(MLA_HEADS * VDIM),
        compiler_params=_cparams(("parallel",)),
        name="attn_ctx",
    )(qnope, qpe, kv, kr)


def _attn_lat_kernel(qn_ref, qpe_ref, kvc_ref, krc_ref, kvl_ref, krl_ref, o_ref, kcat_sc, vaug_sc, cos_sc, sin_sc):
    tq = qn_ref.shape[0]
    n_ctx = krc_ref.shape[0]
    n_lat = krl_ref.shape[0]

    @pl.when(pl.program_id(1) == 0)
    def _():
        _fill_rope_tables(cos_sc, sin_sc)
        kr2 = jnp.concatenate([krl_ref[...], krl_ref[...]], axis=1)
        kpe_lat = _rope(kr2, cos_sc[...], sin_sc[...])[:, :ROPE].astype(BF16)
        kpe_ctx = krc_ref[...].astype(BF16)
        ones_c, ones_l = _ones_column(n_ctx), _ones_column(n_lat)
        for h in range(MLA_HEADS):
            c0 = h * (NOPE + VDIM)
            for r0, nr, kv_ref, kpe, ones in ((0, n_ctx, kvc_ref, kpe_ctx, ones_c), (n_ctx, n_lat, kvl_ref, kpe_lat, ones_l)):
                kcat_sc[h, r0:r0 + nr, 0:NOPE] = kv_ref[:, c0:c0 + NOPE]
                kcat_sc[h, r0:r0 + nr, NOPE:QK_DIM] = kpe
                vaug_sc[h, r0:r0 + nr, 0:VDIM] = kv_ref[:, c0 + NOPE:c0 + NOPE + VDIM]
                vaug_sc[h, r0:r0 + nr, VDIM:2 * VDIM] = ones

    q0 = pl.multiple_of(pl.program_id(1) * tq, tq)
    rep = lambda a: jnp.concatenate([a] * (MLA_HEADS // 2), axis=1)
    qp_all = _rope(qpe_ref[...], rep(cos_sc[pl.ds(q0, tq), :]), rep(sin_sc[pl.ds(q0, tq), :])).astype(BF16)
    for h in range(MLA_HEADS):
        qcat = jnp.concatenate([qn_ref[:, h * NOPE:(h + 1) * NOPE], qp_all[:, h * ROPE:(h + 1) * ROPE]], axis=1)
        o_ref[:, h * VDIM:(h + 1) * VDIM] = _head_attention(qcat, kcat_sc[h], vaug_sc[h]).astype(o_ref.dtype)


def attn_lat(qnope, qpe, kv_ctx, kr_ctx, kv_lat, kr_lat, seq_len, ctx_len):
    tokens = qnope.shape[0]
    nq = seq_len // TQ
    qblk = lambda w: pl.BlockSpec((TQ, w), lambda b, i: (b * nq + i, 0))
    seq = lambda n, w: pl.BlockSpec((n, w), lambda b, i: (b, 0))
    nkv = MLA_HEADS * (NOPE + VDIM)
    n_keys = ctx_len + seq_len
    return pl.pallas_call(
        _attn_lat_kernel,
        out_shape=jax.ShapeDtypeStruct((tokens, MLA_HEADS * VDIM), BF16),
        grid=(tokens // seq_len, nq),
        in_specs=[qblk(MLA_HEADS * NOPE), qblk(MLA_HEADS * ROPE), seq(ctx_len, nkv), seq(ctx_len, ROPE),
                  seq(seq_len, nkv), seq(seq_len, ROPE)],
        out_specs=qblk(MLA_HEADS * VDIM),
        scratch_shapes=[pltpu.VMEM((MLA_HEADS, n_keys, QK_DIM), BF16),
                        pltpu.VMEM((MLA_HEADS, n_keys, 2 * VDIM), BF16),
                        pltpu.VMEM((seq_len, 2 * ROPE), F32), pltpu.VMEM((seq_len, 2 * ROPE), F32)],
        compiler_params=_cparams(("parallel", "arbitrary")),
        name="attn_lat",
    )(qnope, qpe, kv_ctx, kr_ctx, kv_lat, kr_lat)


def _dft_kernel(o_ref):
    tr, n = o_ref.shape[1], o_ref.shape[2]
    nb = n // V7X_LANES
    f = pl.program_id(0) * tr + lax.broadcasted_iota(jnp.int32, (tr, V7X_LANES), 0)
    j = lax.broadcasted_iota(jnp.int32, (tr, V7X_LANES), 1)

    def cos_sin(m):
        ang = (m & (2 * n - 1)).astype(F32) * (math.pi / n)
        return jnp.cos(ang), jnp.sin(ang)

    cj, sj = cos_sin(f * j)
    cb, sb = cos_sin(f * (j * V7X_LANES))
    for b in range(nb):
        cbb, sbb = cb[:, b:b + 1], sb[:, b:b + 1]
        cols = slice(b * V7X_LANES, (b + 1) * V7X_LANES)
        o_ref[0, :, cols] = (cbb * cj - sbb * sj).astype(o_ref.dtype)
        o_ref[1, :, cols] = (sbb * cj + cbb * sj).astype(o_ref.dtype)


def dft_tables(n):
    tr = 128
    return pl.pallas_call(
        _dft_kernel,
        out_shape=jax.ShapeDtypeStruct((2, n, n), BF16),
        grid=(n // tr,),
        out_specs=pl.BlockSpec((2, tr, n), lambda i: (0, i, 0)),
        compiler_params=_cparams(("parallel",)),
        name="dft_tables",
    )()


def _split_dot(table, x):
    hi = x.astype(BF16)
    lo = (x - hi.astype(F32)).astype(BF16)
    return (jnp.dot(table, hi, preferred_element_type=F32) + jnp.dot(table, lo, preferred_element_type=F32))


def _hy_filter_kernel(cs_ref, w1_ref, b1_ref, w2_ref, b2_ref, w3_ref, kr_ref, ks_ref, kny_ref):
    n = cs_ref.shape[1]
    row = lax.broadcasted_iota(jnp.int32, (n, V7X_LANES), 0).astype(F32)
    lane = lax.broadcasted_iota(jnp.int32, (n, V7X_LANES), 1)
    t = row * (1.0 / (n - 1))
    w = (2.0 * math.pi) * row / n
    band = jnp.where(lane <= HY_BANDS, lane - 1, lane - 1 - HY_BANDS).astype(F32)
    freq = 1e-4 + band * ((HY_BANDS - 1 - 1e-4) / (HY_BANDS - 1))
    arg = jnp.where(lane <= HY_BANDS, freq * w + 0.5 * math.pi, -(freq * w))
    z = jnp.where(lane == 0, t, jnp.where(lane <= 2 * HY_BANDS, jnp.sin(arg), 0.0))
    hid = jnp.sin(_dot3(z, w1_ref[...]) + b1_ref[...])
    hid = jnp.sin(_dot3(hid, w2_ref[...]) + b2_ref[...])
    hf = _dot3(hid, w3_ref[...])

    rowc = lax.broadcasted_iota(jnp.int32, (n, HY_W), 0)
    chan = lax.broadcasted_iota(jnp.int32, (n, HY_W), 1).astype(F32)
    max_decay = math.log(HY_TARGET) / HY_FAST_DECAY
    min_decay = math.log(HY_TARGET) / HY_SLOW_DECAY
    deltas = min_decay + chan * ((max_decay - min_decay) / (HY_W - 1))
    decay = jnp.exp(-(rowc.astype(F32) * (1.0 / (n - 1))) * jnp.abs(deltas))
    h_fwd = hf[:, :HY_W] * decay
    h_bwd = jnp.where(rowc == 0, 0.0, hf[:, HY_W:] * decay)
    norm = jnp.sum(jnp.abs(h_fwd) + jnp.abs(h_bwd), axis=0, keepdims=True)
    even = (h_fwd + h_bwd) / norm
    odd = (h_fwd - h_bwd) / norm
    cf = jnp.where(rowc == 0, 1.0, 2.0) * (1.0 / (2 * n))
    kr_ref[...] = cf * _split_dot(cs_ref[0], even)
    ks_ref[...] = cf * _split_dot(cs_ref[1], odd)
    sgn = jnp.where((rowc & 1) == 1, -1.0, 1.0)
    kny_ref[...] = jnp.sum(sgn * even, axis=0, keepdims=True) * (1.0 / (2 * n))


def hy_filter(cs, w1p, b1p, w2p, b2p, w3p):
    n = cs.shape[1]
    full = lambda a: pl.BlockSpec(a.shape, lambda: (0,) * a.ndim)
    args = (cs, w1p, b1p, w2p, b2p, w3p)
    return pl.pallas_call(
        _hy_filter_kernel,
        out_shape=(jax.ShapeDtypeStruct((n, HY_W), F32), jax.ShapeDtypeStruct((n, HY_W), F32),
                   jax.ShapeDtypeStruct((1, HY_W), F32)),
        in_specs=[full(a) for a in args],
        out_specs=(pl.BlockSpec((n, HY_W), lambda: (0, 0)), pl.BlockSpec((n, HY_W), lambda: (0, 0)),
                   pl.BlockSpec((1, HY_W), lambda: (0, 0))),
        compiler_params=pltpu.CompilerParams(vmem_limit_bytes=V7X_VMEM_LIMIT_BYTES),
        name="hy_filter",
    )(*args)


def _hyena_kernel(u0_ref, u1_ref, u2_ref, sw_ref, sb_ref, cs_ref, kr_ref, ks_ref, kny_ref, bias_ref, o_ref,
                  *, seq_len):
    n, cb = u0_ref.shape
    n_seq = n // seq_len
    t = lax.broadcasted_iota(jnp.int32, (n, cb), 0) & (seq_len - 1)

    def short_conv(u_ref, k):
        u = u_ref[...]
        w = sw_ref[:, k * cb:(k + 1) * cb]
        return (sb_ref[:, k * cb:(k + 1) * cb] + w[0:1] * _shift_rows(u, 1, t, seq_len) + w[1:2] * u
                + w[2:3] * _shift_rows(u, -1, t, seq_len))

    x0 = short_conv(u0_ref, 0)
    z = short_conv(u1_ref, 1) * short_conv(u2_ref, 2)
    wide = lambda a: jnp.concatenate([a[q * seq_len:(q + 1) * seq_len] for q in range(n_seq)], axis=1)
    rep = lambda a: jnp.concatenate([a] * n_seq, axis=1)
    zw = wide(z)
    zb = zw.astype(BF16)
    c, s = cs_ref[0], cs_ref[1]
    ur = jnp.dot(c, zb, preferred_element_type=F32)
    us = jnp.dot(s, zb, preferred_element_type=F32)
    sgn = jnp.where((lax.broadcasted_iota(jnp.int32, zw.shape, 0) & 1) == 1, -1.0, 1.0)
    uny = jnp.sum(sgn * zw, axis=0, keepdims=True)
    kr, ks = rep(kr_ref[...]), rep(ks_ref[...])
    yr = (ur * kr - us * ks).astype(BF16)
    ys = (ur * ks + us * kr).astype(BF16)
    yw = jnp.dot(c, yr, preferred_element_type=F32) + jnp.dot(s, ys, preferred_element_type=F32)
    yw = yw + sgn * (uny * rep(kny_ref[...]))
    y = jnp.concatenate([yw[:, q * cb:(q + 1) * cb] for q in range(n_seq)], axis=0)
    o_ref[...] = (x0 * (y + bias_ref[...] * z)).astype(o_ref.dtype)


def hyena(uh, seq_len, short_w, short_b, cs, kr, ks, kny, bias):
    tokens = uh.shape[0]
    cb = HY_CB
    nc = HY_W // cb
    rows = max(seq_len, HY_ROWS)
    assert seq_len & (seq_len - 1) == 0 and rows % seq_len == 0
    ublk = lambda k: pl.BlockSpec((rows, cb), lambda s, c: (s, k * nc + c))
    chan = lambda r: pl.BlockSpec((r, cb), lambda s, c: (0, c))
    return pl.pallas_call(
        functools.partial(_hyena_kernel, seq_len=seq_len),
        out_shape=jax.ShapeDtypeStruct((tokens, HY_W), BF16),
        grid=(tokens // rows, nc),
        in_specs=[ublk(0), ublk(1), ublk(2),
                  pl.BlockSpec((None, 3, 3 * cb), lambda s, c: (c, 0, 0)),
                  pl.BlockSpec((None, 1, 3 * cb), lambda s, c: (c, 0, 0)),
                  pl.BlockSpec((2, seq_len, seq_len), lambda s, c: (0, 0, 0)),
                  chan(seq_len), chan(seq_len), chan(1), chan(1)],
        out_specs=pl.BlockSpec((rows, cb), lambda s, c: (s, c)),
        compiler_params=_cparams(("parallel", "parallel")),
        name="hyena",
    )(uh, uh, uh, short_w, short_b, cs, kr, ks, kny, bias)


META_E1, META_E2, META_R1, META_R2, META_G1, META_G2 = range(6)


def _route_kernel(p0_ref, p1_ref, p2_ref, wo_ref, x_ref, g1_ref, g_ref, sh_ref, sc_ref, wr_ref, br_ref,
                  x1_ref, h_ref, meta_ref, meta_t_ref, cnt_ref, run_sc, wo_sc):
    tm = x_ref.shape[0]
    lane = lax.broadcasted_iota(jnp.int32, (tm, V7X_LANES), 1)

    @pl.when(pl.program_id(0) == 0)
    def _():
        run_sc[...] = jnp.zeros_like(run_sc)
        wo_sc[...] = wo_ref[...].astype(BF16)

    kb = p0_ref.shape[1]
    m = jnp.dot(p0_ref[...], wo_sc[0:kb, :], preferred_element_type=F32)
    m += jnp.dot(p1_ref[...], wo_sc[kb:2 * kb, :], preferred_element_type=F32)
    m += jnp.dot(p2_ref[...], wo_sc[2 * kb:3 * kb, :], preferred_element_type=F32)
    x1 = x_ref[...] + g1_ref[...] * m
    x1_ref[...] = x1
    h = _norm_mod(x1, g_ref[...], sh_ref[...], sc_ref[...])
    h_ref[...] = h
    logits = _dot3(h, wr_ref[...]) + br_ref[...]
    valid = lane < N_EXPERTS
    lg = jnp.where(valid, logits, -jnp.inf)
    ex = jnp.exp(lg - jnp.max(lg, axis=-1, keepdims=True))
    p = ex / jnp.sum(ex, axis=-1, keepdims=True)
    p1 = jnp.max(p, axis=-1, keepdims=True)
    i1 = jnp.min(jnp.where((p == p1) & valid, lane, V7X_LANES), axis=-1, keepdims=True)
    rest = jnp.where((lane == i1) | (~valid), -1.0, p)
    p2 = jnp.max(rest, axis=-1, keepdims=True)
    i2 = jnp.min(jnp.where(rest == p2, lane, V7X_LANES), axis=-1, keepdims=True)
    m1 = lane == i1
    m2 = lane == i2
    chosen = jnp.where(m1 | m2, 1.0, 0.0)
    r = lax.broadcasted_iota(jnp.int32, (tm, tm), 0)
    c = lax.broadcasted_iota(jnp.int32, (tm, tm), 1)
    tri = jnp.where(c < r, 1.0, 0.0).astype(BF16)
    before = jnp.dot(tri, chosen.astype(BF16), preferred_element_type=F32) + run_sc[0:1, :]
    rank1 = jnp.sum(jnp.where(m1, before, 0.0), axis=-1, keepdims=True)
    rank2 = jnp.sum(jnp.where(m2, before, 0.0), axis=-1, keepdims=True)
    inv = 1.0 / (p1 + p2)
    vals = (i1.astype(F32), i2.astype(F32), rank1, rank2, p1 * inv, p2 * inv)
    meta = jnp.zeros((tm, V7X_LANES), F32)
    for k, v in enumerate(vals):
        meta = jnp.where(lane == k, v, meta)
    meta_ref[...] = meta
    meta_t_ref[...] = meta.T[:V7X_SUBLANES]
    run_sc[...] = run_sc[...] + jnp.sum(chosen, axis=0, keepdims=True)
    cnt_ref[...] = run_sc[...]


def mix_route(parts, w_out, x, g, modtab, cond, wr_pad, br_pad):
    tokens = x.shape[0]
    tm = TM_ROUTE
    kb = 512
    zero = lambda i: 0
    const = lambda i: (0, 0)
    rows = lambda w: pl.BlockSpec((tm, w), lambda i: (i, 0))
    lhs_specs = [pl.BlockSpec((tm, kb), (lambda i, cbk=cbk: (i, cbk))) for _, cbk in parts]
    return pl.pallas_call(
        _route_kernel,
        out_shape=(jax.ShapeDtypeStruct((tokens, D), F32),
                   jax.ShapeDtypeStruct((tokens, D), F32),
                   jax.ShapeDtypeStruct((tokens, V7X_LANES), F32),
                   jax.ShapeDtypeStruct((V7X_SUBLANES, tokens), F32),
                   jax.ShapeDtypeStruct((V7X_SUBLANES, V7X_LANES), F32)),
        grid=(tokens // tm,),
        in_specs=lhs_specs + [
            pl.BlockSpec((len(parts) * kb, D), const, pipeline_mode=pl.Buffered(1)),
            rows(D),
            _mod_spec(2, cond, tm, D, zero),
            pl.BlockSpec((1, D), const),
            _mod_spec(3, cond, tm, D, zero),
            _mod_spec(4, cond, tm, D, zero),
            pl.BlockSpec((D, V7X_LANES), const),
            pl.BlockSpec((1, V7X_LANES), const)],
        out_specs=(rows(D), rows(D), rows(V7X_LANES),
                   pl.BlockSpec((V7X_SUBLANES, tm), lambda i: (0, i)),
                   pl.BlockSpec((V7X_SUBLANES, V7X_LANES), const)),
        scratch_shapes=[pltpu.VMEM((V7X_SUBLANES, V7X_LANES), F32), pltpu.VMEM((len(parts) * kb, D), BF16)],
        compiler_params=_cparams(("arbitrary",)),
        name="mix_route",
    )(*[a for a, _ in parts], w_out, x, modtab, g.reshape(1, D), modtab, modtab, wr_pad, br_pad)


def _row_copy(src_ref, src_row, dst_ref, dst_row, sem):
    return pltpu.make_async_copy(src_ref.at[pl.ds(src_row, 1)], dst_ref.at[pl.ds(dst_row, 1)], sem)


_PAD_BULK = (256, 128, 64, 32, 16, 8)


def _zero_fill(hs_ref, zero_sc, sem, pads_ref, n_tail_max, wait):
    tmr = zero_sc.shape[0]

    def copy(rows, dst):
        cp = pltpu.make_async_copy(zero_sc.at[pl.ds(0, rows)], hs_ref.at[pl.ds(dst, rows)], sem)
        cp.wait() if wait else cp.start()

    for e in range(N_EXPERTS):
        start, n = pads_ref[e], pads_ref[N_EXPERTS + e]
        head = jnp.minimum((-start) & (V7X_SUBLANES - 1), n)
        for r in range(V7X_SUBLANES - 1):
            @pl.when(r < head)
            def _():
                copy(1, start + r)
        body = start + head
        rem = n - head
        for k in _PAD_BULK:
            @pl.when((rem & k) != 0)
            def _():
                copy(k, pl.multiple_of(body + (rem & ~(2 * k - 1)), V7X_SUBLANES))
    tail_start, tail_tiles = pads_ref[2 * N_EXPERTS], pads_ref[2 * N_EXPERTS + 1]
    for t in range(n_tail_max):
        @pl.when(t < tail_tiles)
        def _():
            copy(tmr, pl.multiple_of(tail_start + t * tmr, tmr))


def _dispatch_kernel(pos_ref, pads_ref, ha_ref, hb_ref, hs_ref, zero_sc, sem, zsem, *, n_a, n_tail_max):
    tm = ha_ref.shape[0]
    n_tok = pos_ref.shape[0] // 2
    i = pl.program_id(0)
    base = i * tm

    @pl.when(i == 0)
    def _():
        zero_sc[...] = jnp.zeros_like(zero_sc)
        _zero_fill(hs_ref, zero_sc, zsem, pads_ref, n_tail_max, wait=False)

    def scatter(h_ref):
        def issue(r, carry):
            _row_copy(h_ref, r, hs_ref, pos_ref[base + r], sem).start(priority=0)
            _row_copy(h_ref, r, hs_ref, pos_ref[n_tok + base + r], sem).start(priority=1)
            return carry

        lax.fori_loop(0, tm, issue, 0, unroll=8)
        for _ in range(2):
            pltpu.make_async_copy(h_ref, hs_ref.at[pl.ds(0, tm)], sem).wait()

    @pl.when(i < n_a)
    def _():
        scatter(ha_ref)

    @pl.when(i >= n_a)
    def _():
        scatter(hb_ref)

    @pl.when(i == 0)
    def _():
        _zero_fill(hs_ref, zero_sc, zsem, pads_ref, n_tail_max, wait=True)


def moe_dispatch(pos, pads, hs_rows, h_a, h_b):
    tm = TM_ROUTE
    n_a, n_b = h_a.shape[0] // tm, h_b.shape[0] // tm
    n_tail_max = hs_rows // TM_EXPERT - (2 * (h_a.shape[0] + h_b.shape[0])) // TM_EXPERT
    return pl.pallas_call(
        functools.partial(_dispatch_kernel, n_a=n_a, n_tail_max=n_tail_max),
        out_shape=jax.ShapeDtypeStruct((hs_rows, D), F32),
        grid_spec=pltpu.PrefetchScalarGridSpec(
            num_scalar_prefetch=2,
            grid=(n_a + n_b,),
            in_specs=[pl.BlockSpec((tm, D), lambda i, *pf: (jnp.minimum(i, n_a - 1), 0)),
                      pl.BlockSpec((tm, D), lambda i, *pf: (jnp.clip(i - n_a, 0, n_b - 1), 0))],
            out_specs=pl.BlockSpec(memory_space=pl.ANY),
            scratch_shapes=[pltpu.VMEM((TM_EXPERT, D), F32), pltpu.SemaphoreType.DMA(()),
                            pltpu.SemaphoreType.DMA(())]),
        compiler_params=_cparams(("arbitrary",)),
        name="moe_dispatch",
    )(pos, pads, h_a, h_b)


def _experts_kernel(te_ref, sg_ref, su_ref, sd_ref, nv_ref, hs_ref, wg_ref, wu_ref, wd_ref, y_ref,
                    wg_sc, wu_sc, wd_sc):
    del sg_ref, su_ref, sd_ref
    j = pl.program_id(0)
    e = te_ref[j]
    e_prev = te_ref[jnp.maximum(j - 1, 0)]
    n_valid = nv_ref[j]
    half = y_ref.shape[0] // 2

    @pl.when((j == 0) | (e != e_prev))
    def _():
        wg_sc[...] = wg_ref[...].astype(BF16)
        wu_sc[...] = wu_ref[...].astype(BF16)
        wd_sc[...] = wd_ref[...].astype(BF16)

    def swiglu(rows):
        h = hs_ref[rows, :].astype(BF16)
        y = None
        for c0 in range(0, D_FF_EXPERT, MOE_CHUNK):
            c1 = min(c0 + MOE_CHUNK, D_FF_EXPERT)
            hg = jnp.dot(h, wg_sc[:, c0:c1], preferred_element_type=F32)
            hu = jnp.dot(h, wu_sc[:, c0:c1], preferred_element_type=F32)
            act = (_silu(hg) * hu).astype(BF16)
            yc = jnp.dot(act, wd_sc[c0:c1, :], preferred_element_type=F32)
            y = yc if y is None else y + yc
        y_ref[rows, :] = y

    @pl.when(n_valid > half)
    def _():
        swiglu(slice(None))

    @pl.when((n_valid > 0) & (n_valid <= half))
    def _():
        swiglu(slice(0, half))
        y_ref[half:, :] = jnp.zeros((half, D), F32)

    @pl.when(n_valid == 0)
    def _():
        y_ref[...] = jnp.zeros_like(y_ref)


def moe_experts(tile_expert, stages, tile_valid, hs, e_gate, e_up, e_down):
    rows = hs.shape[0]
    tmr = TM_EXPERT
    wspec = lambda shape, k: pl.BlockSpec((None,) + shape, lambda j, *pf: (pf[1 + k][j], 0, 0))
    return pl.pallas_call(
        _experts_kernel,
        out_shape=jax.ShapeDtypeStruct((rows, D), F32),
        grid_spec=pltpu.PrefetchScalarGridSpec(
            num_scalar_prefetch=5,
            grid=(rows // tmr,),
            in_specs=[pl.BlockSpec((tmr, D), lambda j, *pf: (j, 0)),
                      wspec((D, D_FF_EXPERT), 0), wspec((D, D_FF_EXPERT), 1), wspec((D_FF_EXPERT, D), 2)],
            out_specs=pl.BlockSpec((tmr, D), lambda j, *pf: (j, 0)),
            scratch_shapes=[pltpu.VMEM((D, D_FF_EXPERT), BF16), pltpu.VMEM((D, D_FF_EXPERT), BF16),
                            pltpu.VMEM((D_FF_EXPERT, D), BF16)]),
        compiler_params=_cparams(("arbitrary",)),
        name="moe_experts",
    )(tile_expert, *stages, tile_valid, hs, e_gate, e_up, e_down)


def _combine_kernel(pos_ref, x_ref, meta_ref, gt_ref, fg_ref, y_ref, o_ref, b1_sc, b2_sc, sem):
    tm = x_ref.shape[0]
    n_tok = pos_ref.shape[0] // 2
    i = pl.program_id(0)

    def gather(tile, slot):
        base = tile * tm

        def issue(r, carry):
            _row_copy(y_ref, pos_ref[base + r], b1_sc.at[slot], r, sem.at[slot]).start(priority=0)
            _row_copy(y_ref, pos_ref[n_tok + base + r], b2_sc.at[slot], r, sem.at[slot]).start(priority=1)
            return carry

        lax.fori_loop(0, tm, issue, 0, unroll=8)

    @pl.when(i == 0)
    def _():
        gather(0, 0)

    @pl.when(i + 1 < pl.num_programs(0))
    def _():
        gather(i + 1, (i + 1) % 2)

    slot = i % 2
    pltpu.make_async_copy(y_ref.at[pl.ds(0, tm)], b1_sc.at[slot], sem.at[slot]).wait()
    pltpu.make_async_copy(y_ref.at[pl.ds(0, tm)], b2_sc.at[slot], sem.at[slot]).wait()

    meta = meta_ref[...]
    lane = lax.broadcasted_iota(jnp.int32, meta.shape, 1)
    g1 = jnp.sum(jnp.where(lane == META_G1, meta, 0.0), axis=-1, keepdims=True)
    g2 = jnp.sum(jnp.where(lane == META_G2, meta, 0.0), axis=-1, keepdims=True)
    x = x_ref[...] + gt_ref[...] * (g1 * b1_sc[slot] + g2 * b2_sc[slot])
    o_ref[...] = _rms(x, fg_ref[...])


def moe_combine(pos, x, meta, modtab, cond, final_g, y):
    tokens = x.shape[0]
    tm = TM_COMBINE
    return pl.pallas_call(
        _combine_kernel,
        out_shape=jax.ShapeDtypeStruct((tokens, D), F32),
        grid_spec=pltpu.PrefetchScalarGridSpec(
            num_scalar_prefetch=1,
            grid=(tokens // tm,),
            in_specs=[pl.BlockSpec((tm, D), lambda i, pos: (i, 0)),
                      pl.BlockSpec((tm, V7X_LANES), lambda i, pos: (i, 0)),
                      _mod_spec(5, cond, tm, D, lambda i, pos: 0),
                      pl.BlockSpec((1, D), lambda i, pos: (0, 0)),
                      pl.BlockSpec(memory_space=pl.ANY)],
            out_specs=pl.BlockSpec((tm, D), lambda i, pos: (i, 0)),
            scratch_shapes=[pltpu.VMEM((2, tm, D), F32), pltpu.VMEM((2, tm, D), F32),
                            pltpu.SemaphoreType.DMA((2,))]),
        compiler_params=_cparams(("arbitrary",)),
        name="moe_combine",
    )(pos, x, meta, modtab, final_g.reshape(1, D), y)


def moe_plan(metas, counts):
    tmr = TM_EXPERT
    cnts = [c[0, :N_EXPERTS].astype(jnp.int32) for c in counts]
    total = functools.reduce(jnp.add, cnts)
    padded = ((total + tmr - 1) // tmr) * tmr
    ends = jnp.cumsum(padded)
    starts = ends - padded
    n_rows = sum(m.shape[1] for m in metas) * 2 + N_EXPERTS * tmr
    n_tiles = n_rows // tmr
    tile_start = jnp.arange(n_tiles, dtype=jnp.int32) * tmr
    tile_expert = jnp.minimum(jnp.sum(tile_start[:, None] >= ends[None, :], axis=1), N_EXPERTS - 1).astype(jnp.int32)
    group_of_tile = jnp.sum(tile_start[:, None] >= ends[None, :], axis=1)
    real_end = jnp.sum(jnp.where(group_of_tile[:, None] == jnp.arange(N_EXPERTS)[None, :],
                                 (starts + total)[None, :], 0), axis=1)
    tile_valid = jnp.clip(real_end - tile_start, 0, tmr).astype(jnp.int32)
    eid = jnp.arange(N_EXPERTS, dtype=jnp.int32)
    later = jnp.where((eid[None, :] > eid[:, None]) & (padded[None, :] > 0), eid[None, :], N_EXPERTS)
    nxt = jnp.min(later, axis=1)
    next_used = jnp.where(nxt == N_EXPERTS, eid, nxt)
    pick = lambda table: jnp.sum(jnp.where(tile_expert[:, None] == eid[None, :], table[None, :], 0), axis=1)
    k_in_group = (tile_start - pick(starts)) // tmr
    tile_next = pick(next_used)
    stages = [jnp.where(k_in_group < k, tile_expert, tile_next).astype(jnp.int32) for k in (1, 2, 3)]
    pos, p1s, p2s = [], [], []
    base = jnp.zeros((N_EXPERTS,), jnp.int32)
    for m, c in zip(metas, cnts):
        first = starts + base
        sel = lambda field: m[field].astype(jnp.int32)
        lookup = lambda e: jnp.sum(jnp.where(e[:, None] == jnp.arange(N_EXPERTS)[None, :], first[None, :], 0), axis=1)
        p1 = lookup(sel(META_E1)) + sel(META_R1)
        p2 = lookup(sel(META_E2)) + sel(META_R2)
        pos.append(jnp.concatenate([p1, p2]).astype(jnp.int32))
        p1s.append(p1)
        p2s.append(p2)
        base = base + c
    pos_all = jnp.concatenate(p1s + p2s).astype(jnp.int32)
    pads = jnp.concatenate([starts + total, padded - total,
                            jnp.stack([ends[-1], n_tiles - ends[-1] // tmr])]).astype(jnp.int32)
    return pos, pos_all, pads, tile_expert, stages, tile_valid, n_rows


def _pad_to(a, shape):
    return jnp.pad(a, [(0, t - s) for s, t in zip(a.shape, shape)])


def _regroup_chunks(a, cb):
    r = a.shape[0]
    return a.reshape(r, 3, HY_W // cb, cb).transpose(2, 0, 1, 3).reshape(HY_W // cb, r, 3 * cb)


def kernel(x_prompt, x_sample, state_l0_lru, cache_l1_ckv, cache_l1_krope, c, c_ctx, l0_norm1, l0_norm2, l0_w_mod, l0_b_mod, l0_w_in, l0_conv_a, l0_lru_conv_w, l0_lru_conv_b, l0_lru_wa, l0_lru_ba, l0_lru_wi, l0_lru_bi, l0_lru_lambda, l0_w_out, l0_ffn_gate, l0_ffn_up, l0_ffn_down, l1_norm1, l1_norm2, l1_w_mod, l1_b_mod, l1_w_in, l1_q_norm, l1_kv_norm, l1_w_q_up, l1_w_kv_up, l1_hy_short_w, l1_hy_short_b, l1_hy_f_w1, l1_hy_f_b1, l1_hy_f_w2, l1_hy_f_b2, l1_hy_f_w3, l1_hy_bias, l1_w_out, l1_router_w, l1_router_b, l1_exp_gate, l1_exp_up, l1_exp_down, final_norm):
    batch, seq, _ = x_prompt.shape
    dec_batch, dec_seq, _ = x_sample.shape
    past_len = cache_l1_ckv.shape[1]

    cond8 = jnp.concatenate([c_ctx[None, :], c, jnp.zeros((V7X_SUBLANES - 1 - dec_batch, D), F32)], axis=0)
    wcat = jnp.concatenate([l0_lru_wa[0], l0_lru_wi[0], l0_lru_wa[1], l0_lru_wi[1]], axis=-1)
    hid = V7X_LANES
    w1p = _pad_to(l1_hy_f_w1, (hid, hid))
    b1p = _pad_to(l1_hy_f_b1.reshape(1, -1), (1, hid))
    w2p = _pad_to(l1_hy_f_w2, (hid, hid))
    b2p = _pad_to(l1_hy_f_b2.reshape(1, -1), (1, hid))
    w3p = _pad_to(l1_hy_f_w3, (hid, 2 * HY_W))
    short_w = _regroup_chunks(l1_hy_short_w, HY_CB)
    short_b = _regroup_chunks(l1_hy_short_b.reshape(1, -1), HY_CB)
    hy_bias = l1_hy_bias.reshape(1, HY_W)
    wr_pad = _pad_to(l1_router_w, (D, V7X_LANES))
    br_pad = _pad_to(l1_router_b.reshape(1, -1), (1, V7X_LANES))

    mod0, mod1 = adaln_tables(cond8, ((l0_w_mod, l0_b_mod), (l1_w_mod, l1_b_mod)))

    kv_ctx = kv_up(cache_l1_ckv.reshape(dec_batch * past_len, KV_RANK), l1_w_kv_up)
    kr_ctx = cache_l1_krope.reshape(dec_batch * past_len, ROPE)

    conds = ((0, batch * seq), (1, dec_seq))
    seq_lens = (seq, dec_seq)
    xs = (x_prompt.reshape(batch * seq, D), x_sample.reshape(dec_batch * dec_seq, D))
    h0s = (jnp.zeros((batch, 2, LRU_W), F32), state_l0_lru)

    us = in0_proj(xs, l0_norm1, mod0, conds, l0_w_in)
    parts, lru_states = [], []
    for u, seq_len, h0 in zip(us, seq_lens, h0s):
        ya = conv_a(u, seq_len, l0_conv_a)
        yb, lru_state = rglru(u, seq_len, l0_lru_conv_w, l0_lru_conv_b, wcat, l0_lru_ba, l0_lru_bi,
                              l0_lru_lambda, h0)
        parts.append([(ya, 0), (yb, 0), (yb, 1)])
        lru_states.append(lru_state)
    xs = mix_ffn(parts, l0_w_out, xs, l0_norm2, mod0, conds, l0_ffn_gate, l0_ffn_up, l0_ffn_down)
    new_lru = lru_states[0]

    def layer1(x, seq_len, cond, latent):
        qnope, qpe, ckv, kr, kv, uh = in1_proj(x, l1_norm1, mod1, cond, l1_w_in, l1_q_norm, l1_kv_norm,
                                               l1_w_q_up, l1_w_kv_up)
        if latent:
            yc = attn_lat(qnope, qpe, kv_ctx, kr_ctx, kv, kr, seq_len, past_len)
        else:
            yc = attn_ctx(qnope, qpe, kv, kr, seq_len)
        cs = dft_tables(seq_len)
        k_r, k_s, k_ny = hy_filter(cs, w1p, b1p, w2p, b2p, w3p)
        yd = hyena(uh, seq_len, short_w, short_b, cs, k_r, k_s, k_ny, hy_bias)
        routed = mix_route([(yc, 0), (yc, 1), (yd, 0)], l1_w_out, x, l1_norm2, mod1, cond, wr_pad, br_pad)
        return routed, ckv, kr

    r_p, new_ckv, new_kr = layer1(xs[0], seq, conds[0], latent=False)
    r_s, _, _ = layer1(xs[1], dec_seq, conds[1], latent=True)

    routed = (r_p, r_s)
    pos, pos_all, pads, tile_expert, stages, tile_valid, n_rows = moe_plan([r[3] for r in routed],
                                                                          [r[4] for r in routed])
    hs = moe_dispatch(pos_all, pads, n_rows, r_p[1], r_s[1])
    y_rows = moe_experts(tile_expert, stages, tile_valid, hs, l1_exp_gate, l1_exp_up, l1_exp_down)
    y_p, y_s = [moe_combine(p, r[0], r[2], mod1, cond, final_norm, y_rows)
                for p, r, cond in zip(pos, routed, conds)]
    return (y_p.reshape(batch, seq, D), y_s.reshape(dec_batch, dec_seq, D), new_lru,
            new_ckv.reshape(batch, seq, KV_RANK), new_kr.reshape(batch, seq, ROPE))
```

```python
import functools
import math

import jax
import jax.numpy as jnp
from jax import lax
from jax.experimental import pallas as pl
from jax.experimental.pallas import tpu as pltpu

F32 = jnp.float32
BF16 = jnp.bfloat16
HIGHEST = lax.Precision.HIGHEST

D = 1024
GRID_W = 64
EPS = 1e-6
CONV_W = 512
LRU_W = 1024
LRU_BW = 128
LRU_C = 8.0
MLA_HEADS = 8
Q_RANK = 384
KV_RANK = 256
NOPE = 128
ROPE = 64
VDIM = 128
QK_DIM = NOPE + ROPE
ROPE_THETA = 10000.0
HY_W = 512
HY_BANDS = 16
HY_TARGET = 1e-2
HY_FAST_DECAY = 0.3
HY_SLOW_DECAY = 1.5
D_FF = 2816
N_EXPERTS = 8
D_FF_EXPERT = 1408
IN0 = 3 * CONV_W + 2 * LRU_W
IN1 = Q_RANK + KV_RANK + ROPE + 3 * HY_W

V7X_LANES = 128
V7X_SUBLANES = 8
V7X_VMEM_LIMIT_BYTES = 56 * 1024 * 1024
V7X_VMEM_LIMIT_LARGE_BYTES = 60 * 1024 * 1024

TM = 512
TN_IN0 = 512
TF_FFN = 256
MOE_CHUNK = 256
TM_ROUTE = 512
TM_EXPERT = 512
TM_COMBINE = 512
LRU_CB = 256
HY_CB = 256
TQ = 256
ATTN_CTX_SEQS = 4
CONV_A_ROWS = 1024
LRU_ROWS = 1024
HY_ROWS = 1024
TM_IN1 = 512


def _cparams(sem, vmem_limit_bytes=V7X_VMEM_LIMIT_BYTES):
    return pltpu.CompilerParams(dimension_semantics=sem, vmem_limit_bytes=vmem_limit_bytes)


def _sigmoid(x):
    return 0.5 * jnp.tanh(0.5 * x) + 0.5


def _silu(x):
    return x * _sigmoid(x)


def _norm_mod(x, g, shift, scale):
    ms = jnp.mean(x * x, axis=-1, keepdims=True)
    y = x * lax.rsqrt(ms + EPS) * g
    return y * (1.0 + scale) + shift


def _mod_spec(comp, cond, tm, width, col_fn, tile_fn=lambda *ids: ids[0]):
    row0, seg = cond
    assert seg % tm == 0
    return pl.BlockSpec((None, 1, width),
                        lambda *ids: (comp * 3 + row0 + (tile_fn(*ids) * tm) // seg, 0, col_fn(*ids)))


def _dot3(a, b):
    a_hi = a.astype(BF16)
    a_lo = (a - a_hi.astype(F32)).astype(BF16)
    b_hi = b.astype(BF16)
    b_lo = (b - b_hi.astype(F32)).astype(BF16)
    n = a.shape[0]
    y = jnp.dot(jnp.concatenate([a_hi, a_lo], axis=0), b_hi, preferred_element_type=F32)
    return y[:n] + y[n:] + jnp.dot(a_hi, b_lo, preferred_element_type=F32)


def _adaln_kernel(c_ref, w0_ref, b0_ref, w1_ref, b1_ref, o_ref):
    a = _silu(c_ref[...])
    for layer, (w_ref, b_ref) in enumerate(((w0_ref, b0_ref), (w1_ref, b1_ref))):
        @pl.when(pl.program_id(0) == layer)
        def _():
            o_ref[...] = _dot3(a, w_ref[...]) + b_ref[...]


def adaln_tables(cond8, mods):
    tn = 1536
    nj = 6 * D // tn
    (w0, b0), (w1, b1) = mods
    at0 = lambda l, j: (0, jnp.where(l == 0, j, nj - 1))
    at1 = lambda l, j: (0, jnp.where(l == 1, j, 0))
    m = pl.pallas_call(
        _adaln_kernel,
        out_shape=jax.ShapeDtypeStruct((2, V7X_SUBLANES, 6 * D), F32),
        grid=(2, nj),
        in_specs=[pl.BlockSpec((V7X_SUBLANES, D), lambda l, j: (0, 0)),
                  pl.BlockSpec((D, tn), at0), pl.BlockSpec((1, tn), at0),
                  pl.BlockSpec((D, tn), at1), pl.BlockSpec((1, tn), at1)],
        out_specs=pl.BlockSpec((None, V7X_SUBLANES, tn), lambda l, j: (l, 0, j)),
        compiler_params=_cparams(("arbitrary", "arbitrary")),
        name="adaln",
    )(cond8, w0, b0.reshape(1, 6 * D), w1, b1.reshape(1, 6 * D))
    return [m[l, :3].reshape(3, 6, D).transpose(1, 0, 2).reshape(18, 1, D) for l in range(2)]


def _tile_of(n_load):
    return lambda s: jnp.maximum(s - n_load, 0)


def _block_of(n_load):
    return lambda s: jnp.minimum(s, n_load - 1)


class _TwoSets:
    def __init__(self, n_load, tm, tokens, conds):
        self.n_load, self.tm, self.conds = n_load, tm, conds
        self.n_a, self.n_b = tokens[0] // tm, tokens[1] // tm
        self.steps = n_load + self.n_a + self.n_b

    def tile(self, s):
        return jnp.maximum(s - self.n_load, 0)

    def in_first(self, s):
        return s - self.n_load < self.n_a

    def idx_a(self, s):
        return jnp.minimum(self.tile(s), self.n_a - 1)

    def idx_b(self, s):
        return jnp.clip(self.tile(s) - self.n_a, 0, self.n_b - 1)

    def rows(self, width):
        return (pl.BlockSpec((self.tm, width), lambda s: (self.idx_a(s), 0)),
                pl.BlockSpec((self.tm, width), lambda s: (self.idx_b(s), 0)))

    def cols(self, width, col):
        return (pl.BlockSpec((self.tm, width), lambda s: (self.idx_a(s), col)),
                pl.BlockSpec((self.tm, width), lambda s: (self.idx_b(s), col)))

    def mod_spec(self, comp):
        (row_a, seg_a), (row_b, seg_b) = self.conds
        assert seg_a % self.tm == 0 and seg_b % self.tm == 0

        def row(s):
            return jnp.where(self.in_first(s), row_a + (self.idx_a(s) * self.tm) // seg_a,
                             row_b + (self.idx_b(s) * self.tm) // seg_b)

        return pl.BlockSpec((None, 1, D), lambda s: (comp * 3 + row(s), 0, 0))


def _in0_kernel(xa_ref, xb_ref, g_ref, sh_ref, sc_ref, w_ref, oa_ref, ob_ref, w_sc, *, n_a):
    s = pl.program_id(0)
    n_load = w_sc.shape[0]

    @pl.when(s < n_load)
    def _():
        w_sc[s] = w_ref[...].astype(BF16)

    @pl.when(s >= n_load)
    def _():
        first = s - n_load < n_a
        x = jnp.where(first, xa_ref[...], xb_ref[...])
        h = _norm_mod(x, g_ref[...], sh_ref[...], sc_ref[...]).astype(BF16)
        u = jnp.concatenate([jnp.dot(h, w_sc[j], preferred_element_type=F32).astype(BF16)
                             for j in range(n_load)], axis=1)

        @pl.when(first)
        def _():
            oa_ref[...] = u

        @pl.when(jnp.logical_not(first))
        def _():
            ob_ref[...] = u


def in0_proj(xs, g, modtab, conds, w_in):
    tn = TN_IN0
    n = w_in.shape[1]
    n_load = n // tn
    ts = _TwoSets(n_load, TM, [x.shape[0] for x in xs], conds)
    blk = _block_of(n_load)
    return pl.pallas_call(
        functools.partial(_in0_kernel, n_a=ts.n_a),
        out_shape=tuple(jax.ShapeDtypeStruct((x.shape[0], n), BF16) for x in xs),
        grid=(ts.steps,),
        in_specs=[*ts.rows(D),
                  pl.BlockSpec((1, D), lambda s: (0, 0)),
                  ts.mod_spec(0), ts.mod_spec(1),
                  pl.BlockSpec((D, tn), lambda s: (0, blk(s)))],
        out_specs=ts.rows(n),
        scratch_shapes=[pltpu.VMEM((n_load, D, tn), BF16)],
        compiler_params=_cparams(("arbitrary",)),
        name="in0_proj",
    )(*xs, g.reshape(1, D), modtab, modtab, w_in)


def _shift_rows(v, d, t, seq_len=None):
    n = v.shape[0]
    seq_len = n if seq_len is None else seq_len
    if d > 0:
        return jnp.where(t < d, 0.0, pltpu.roll(v, d, 0))
    return jnp.where(t >= seq_len + d, 0.0, pltpu.roll(v, n + d, 0))


def _conv_a_kernel(b_ref, c_ref, x_ref, w_ref, o_ref, *, seq_len):
    v = c_ref[...].astype(F32) * x_ref[...].astype(F32)
    t = lax.broadcasted_iota(jnp.int32, v.shape, 0) & (seq_len - 1)
    w = w_ref[...]
    y = w[0:1] * _shift_rows(v, 1, t, seq_len) + w[1:2] * v + w[2:3] * _shift_rows(v, -1, t, seq_len)
    o_ref[...] = (b_ref[...].astype(F32) * y).astype(o_ref.dtype)


def conv_a(u, seq_len, conv_w):
    tokens = u.shape[0]
    rows = max(seq_len, CONV_A_ROWS)
    assert seq_len & (seq_len - 1) == 0 and rows % seq_len == 0
    return pl.pallas_call(
        functools.partial(_conv_a_kernel, seq_len=seq_len),
        out_shape=jax.ShapeDtypeStruct((tokens, CONV_W), BF16),
        grid=(tokens // rows,),
        in_specs=[pl.BlockSpec((rows, CONV_W), lambda s: (s, 0)),
                  pl.BlockSpec((rows, CONV_W), lambda s: (s, 1)),
                  pl.BlockSpec((rows, CONV_W), lambda s: (s, 2)),
                  pl.BlockSpec((3, CONV_W), lambda s: (0, 0))],
        out_specs=pl.BlockSpec((rows, CONV_W), lambda s: (s, 0)),
        compiler_params=_cparams(("parallel",)),
        name="conv_a",
    )(u, u, u, conv_w)


def _group_scan(a_sc, b_sc, k, reverse):
    planes = a_sc.shape[1] // V7X_SUBLANES
    order = range(V7X_SUBLANES - 1, -1, -1) if reverse else range(V7X_SUBLANES)
    a_acc = b_acc = None
    for r in order:
        plane = (k, pl.ds(r, planes, stride=V7X_SUBLANES), slice(None))
        a_r, b_r = a_sc[plane], b_sc[plane]
        if a_acc is None:
            a_acc, b_acc = a_r, b_r
        else:
            b_acc = a_r * b_acc + b_r
            a_acc = a_r * a_acc
            a_sc[plane] = a_acc
            b_sc[plane] = b_acc


def _rglru_kernel(gate_ref, xb_ref, cw_ref, cb_ref, wcat_ref, ba_ref, bi_ref, lam_ref, h0_ref,
                  y_ref, st_ref, af_sc, bf_sc, ab_sc, bb_sc, hf_sc, hb_sc, *, seq_len):
    n, cb = xb_ref.shape
    n_seq = n // seq_len
    n_slab = cb // LRU_BW
    xb = xb_ref[...].astype(F32)
    t = lax.broadcasted_iota(jnp.int32, xb.shape, 0) & (seq_len - 1)
    cw = cw_ref[...]
    sh = lambda d: _shift_rows(xb, d, t, seq_len)
    xc = cb_ref[...] + cw[0:1] * sh(2) + cw[1:2] * sh(1) + cw[2:3] * xb + cw[3:4] * sh(-1)
    xcb = xc.astype(BF16)

    for k in range(n_slab):
        cols = slice(k * LRU_BW, (k + 1) * LRU_BW)
        gk = jnp.dot(xcb[:, cols], wcat_ref[k].astype(BF16), preferred_element_type=F32)
        for d, (a_sc, b_sc) in enumerate(((af_sc, bf_sc), (ab_sc, bb_sc))):
            ga = gk[:, (2 * d) * LRU_BW:(2 * d + 1) * LRU_BW]
            gi = gk[:, (2 * d + 1) * LRU_BW:(2 * d + 2) * LRU_BW]
            r = _sigmoid(ga + ba_ref[d:d + 1, cols])
            i = _sigmoid(gi + bi_ref[d:d + 1, cols])
            log_a = (-LRU_C * jax.nn.softplus(-lam_ref[d:d + 1, cols])) * r
            a = jnp.exp(log_a)
            m = 1.0 - a * a
            mult = m * lax.rsqrt(jnp.maximum(m, 1e-30))
            a_sc[k] = a
            b_sc[k] = mult * (i * xc[:, cols])
            _group_scan(a_sc, b_sc, k, reverse=(d == 1))

    ng = seq_len // V7X_SUBLANES
    bcast = lambda row: jnp.broadcast_to(row, (V7X_SUBLANES, LRU_BW))
    chains = [(q, k) for q in range(n_seq) for k in range(n_slab)]
    init = tuple((bcast(h0_ref[q, 0:1, k * LRU_BW:(k + 1) * LRU_BW]),
                  bcast(h0_ref[q, 1:2, k * LRU_BW:(k + 1) * LRU_BW])) for q, k in chains)

    def step(j, carry):
        out = []
        for (q, k), (hf_in, hb_in) in zip(chains, carry):
            rf = pl.ds(pl.multiple_of(q * seq_len + j * V7X_SUBLANES, V7X_SUBLANES), V7X_SUBLANES)
            rb = pl.ds(pl.multiple_of(q * seq_len + (ng - 1 - j) * V7X_SUBLANES, V7X_SUBLANES), V7X_SUBLANES)
            hf = af_sc[k, rf, :] * hf_in + bf_sc[k, rf, :]
            hb = ab_sc[k, rb, :] * hb_in + bb_sc[k, rb, :]
            hf_sc[k, rf, :] = hf
            hb_sc[k, rb, :] = hb
            out.append((bcast(hf[V7X_SUBLANES - 1:V7X_SUBLANES]), bcast(hb[0:1])))
        return tuple(out)

    final = lax.fori_loop(0, ng, step, init)
    for (q, k), (hf_last, hb_first) in zip(chains, final):
        st_ref[q, 0:1, k * LRU_BW:(k + 1) * LRU_BW] = hf_last[0:1]
        st_ref[q, 1:2, k * LRU_BW:(k + 1) * LRU_BW] = hb_first[0:1]

    gt = gate_ref[...].astype(F32)
    gelu = 0.5 * gt * (1.0 + jnp.tanh(math.sqrt(2.0 / math.pi) * (gt + 0.044715 * (gt * gt * gt))))
    h = jnp.concatenate([hf_sc[k] + hb_sc[k] for k in range(n_slab)], axis=1)
    y_ref[...] = (h * gelu).astype(y_ref.dtype)


def rglru(u, seq_len, conv_w, conv_b, wcat, ba, bi, lam, h0):
    tokens = u.shape[0]
    nseq = tokens // seq_len
    cb = LRU_CB
    rows = max(seq_len, LRU_ROWS)
    assert seq_len & (seq_len - 1) == 0 and rows % seq_len == 0
    per_blk = rows // seq_len
    gate_blk0 = 3 * CONV_W // cb
    xb_blk0 = (3 * CONV_W + LRU_W) // cb
    seq_scr = lambda: pltpu.VMEM((cb // LRU_BW, rows, LRU_BW), F32)
    return pl.pallas_call(
        functools.partial(_rglru_kernel, seq_len=seq_len),
        out_shape=(jax.ShapeDtypeStruct((tokens, LRU_W), BF16), jax.ShapeDtypeStruct((nseq, 2, LRU_W), F32)),
        grid=(tokens // rows, LRU_W // cb),
        in_specs=[pl.BlockSpec((rows, cb), lambda s, c: (s, gate_blk0 + c)),
                  pl.BlockSpec((rows, cb), lambda s, c: (s, xb_blk0 + c)),
                  pl.BlockSpec((4, cb), lambda s, c: (0, c)),
                  pl.BlockSpec((1, cb), lambda s, c: (0, c)),
                  pl.BlockSpec((cb // LRU_BW, LRU_BW, 4 * LRU_BW), lambda s, c: (c, 0, 0)),
                  pl.BlockSpec((2, cb), lambda s, c: (0, c)),
                  pl.BlockSpec((2, cb), lambda s, c: (0, c)),
                  pl.BlockSpec((2, cb), lambda s, c: (0, c)),
                  pl.BlockSpec((per_blk, 2, cb), lambda s, c: (s, 0, c))],
        out_specs=(pl.BlockSpec((rows, cb), lambda s, c: (s, c)),
                   pl.BlockSpec((per_blk, 2, cb), lambda s, c: (s, 0, c))),
        scratch_shapes=[seq_scr() for _ in range(6)],
        compiler_params=_cparams(("parallel", "parallel")),
        name="rglru",
    )(u, u, conv_w, conv_b.reshape(1, LRU_W), wcat, ba, bi, lam, h0)


def _mix_ffn_kernel(p0a_ref, p0b_ref, p1a_ref, p1b_ref, p2a_ref, p2b_ref, wo_ref, xa_ref, xb_ref,
                    g1_ref, g_ref, sh_ref, sc_ref, g2_ref, wg_ref, wu_ref, wd_ref, oa_ref, ob_ref,
                    wo_sc, wg_sc, wu_sc, wd_sc, *, n_a):
    s = pl.program_id(0)
    n_load = wg_sc.shape[0]
    n_out = wo_sc.shape[0]

    @pl.when(s < n_out)
    def _():
        wo_sc[s] = wo_ref[...].astype(BF16)

    @pl.when(s < n_load)
    def _():
        wg_sc[s] = wg_ref[...].astype(BF16)
        wu_sc[s] = wu_ref[...].astype(BF16)
        wd_sc[s] = wd_ref[...].astype(BF16)

    @pl.when(s >= n_load)
    def _():
        first = s - n_load < n_a
        pick = lambda a_ref, b_ref: jnp.where(first, a_ref[...], b_ref[...])
        m = jnp.dot(pick(p0a_ref, p0b_ref), wo_sc[0], preferred_element_type=F32)
        m += jnp.dot(pick(p1a_ref, p1b_ref), wo_sc[1], preferred_element_type=F32)
        m += jnp.dot(pick(p2a_ref, p2b_ref), wo_sc[2], preferred_element_type=F32)
        x = pick(xa_ref, xb_ref) + g1_ref[...] * m
        h = _norm_mod(x, g_ref[...], sh_ref[...], sc_ref[...]).astype(BF16)
        y = None
        for f in range(n_load):
            hg = jnp.dot(h, wg_sc[f], preferred_element_type=F32)
            hu = jnp.dot(h, wu_sc[f], preferred_element_type=F32)
            act = (_silu(hg) * hu).astype(BF16)
            yf = jnp.dot(act, wd_sc[f], preferred_element_type=F32)
            y = yf if y is None else y + yf
        out = x + g2_ref[...] * y

        @pl.when(first)
        def _():
            oa_ref[...] = out

        @pl.when(jnp.logical_not(first))
        def _():
            ob_ref[...] = out


def mix_ffn(parts, w_out, xs, g, modtab, conds, w_gate, w_up, w_down):
    tf = TF_FFN
    kb = 512
    n_load = D_FF // tf
    n_out = len(parts[0])
    assert n_out <= n_load
    ts = _TwoSets(n_load, TM, [x.shape[0] for x in xs], conds)
    blk = _block_of(n_load)
    oblk = _block_of(n_out)
    lhs_specs, lhs_args = [], []
    for (arr_a, col_a), (arr_b, col_b) in zip(*parts):
        assert col_a == col_b
        lhs_specs += ts.cols(kb, col_a)
        lhs_args += [arr_a, arr_b]
    return pl.pallas_call(
        functools.partial(_mix_ffn_kernel, n_a=ts.n_a),
        out_shape=tuple(jax.ShapeDtypeStruct(x.shape, F32) for x in xs),
        grid=(ts.steps,),
        in_specs=lhs_specs + [
            pl.BlockSpec((kb, D), lambda s: (oblk(s), 0)),
            *ts.rows(D),
            ts.mod_spec(2),
            pl.BlockSpec((1, D), lambda s: (0, 0)),
            ts.mod_spec(3), ts.mod_spec(4), ts.mod_spec(5),
            pl.BlockSpec((D, tf), lambda s: (0, blk(s))),
            pl.BlockSpec((D, tf), lambda s: (0, blk(s))),
            pl.BlockSpec((tf, D), lambda s: (blk(s), 0))],
        out_specs=ts.rows(D),
        scratch_shapes=[pltpu.VMEM((n_out, kb, D), BF16),
                        pltpu.VMEM((n_load, D, tf), BF16), pltpu.VMEM((n_load, D, tf), BF16),
                        pltpu.VMEM((n_load, tf, D), BF16)],
        compiler_params=_cparams(("arbitrary",), V7X_VMEM_LIMIT_LARGE_BYTES),
        name="mix_ffn",
    )(*lhs_args, w_out, *xs, modtab, g.reshape(1, D), modtab, modtab, modtab, w_gate, w_up, w_down)


def _rms(x, g):
    return x * lax.rsqrt(jnp.mean(x * x, axis=-1, keepdims=True) + EPS) * g


def _in1_kernel(x_ref, g_ref, sh_ref, sc_ref, w_ref, qn_ref, kvn_ref, wq_ref, wkv_ref,
                qnope_ref, qpe_ref, ckv_ref, kr_ref, kv_ref, uh_ref, w_sc, wq_sc, wkv_sc):
    @pl.when(pl.program_id(0) == 0)
    def _():
        w_sc[...] = w_ref[...].astype(BF16)
        for h in range(MLA_HEADS):
            c0 = h * QK_DIM
            wq_sc[:, h * NOPE:(h + 1) * NOPE] = wq_ref[:, c0:c0 + NOPE].astype(BF16)
            r0 = MLA_HEADS * NOPE + h * ROPE
            wq_sc[:, r0:r0 + ROPE] = wq_ref[:, c0 + NOPE:c0 + QK_DIM].astype(BF16)
        wkv_sc[...] = wkv_ref[...].astype(BF16)

    h = _norm_mod(x_ref[...], g_ref[...], sh_ref[...], sc_ref[...]).astype(BF16)
    u = lax.dot_general(h, w_sc[...], (((1,), (1,)), ((), ())), preferred_element_type=F32)
    o1, o2, o3 = Q_RANK, Q_RANK + KV_RANK, Q_RANK + KV_RANK + ROPE
    cq = _rms(u[:, :o1], qn_ref[...])
    q = jnp.dot(cq.astype(BF16), wq_sc[...], preferred_element_type=F32) * _SCALE
    qnope_ref[...] = q[:, :MLA_HEADS * NOPE].astype(qnope_ref.dtype)
    qpe_ref[...] = q[:, MLA_HEADS * NOPE:]
    ckv = _rms(u[:, o1:o2], kvn_ref[...])
    ckv_ref[...] = ckv
    kv_ref[...] = jnp.dot(ckv.astype(BF16), wkv_sc[...], preferred_element_type=F32).astype(kv_ref.dtype)
    kr_ref[...] = u[:, o2:o3]
    uh_ref[...] = u[:, o3:]


def in1_proj(x, g, modtab, cond, w_in, q_norm, kv_norm, w_q_up, w_kv_up):
    tokens = x.shape[0]
    tm = TM_IN1
    nkv = MLA_HEADS * (NOPE + VDIM)
    const = lambda i: (0, 0)
    zero = lambda i: 0
    once = pl.Buffered(1)
    outs = (jax.ShapeDtypeStruct((tokens, MLA_HEADS * NOPE), BF16),
            jax.ShapeDtypeStruct((tokens, MLA_HEADS * ROPE), F32),
            jax.ShapeDtypeStruct((tokens, KV_RANK), F32),
            jax.ShapeDtypeStruct((tokens, ROPE), F32),
            jax.ShapeDtypeStruct((tokens, nkv), BF16),
            jax.ShapeDtypeStruct((tokens, 3 * HY_W), F32))
    row = lambda w: pl.BlockSpec((tm, w), lambda i: (i, 0))
    return pl.pallas_call(
        _in1_kernel,
        out_shape=outs,
        grid=(tokens // tm,),
        in_specs=[row(D),
                  pl.BlockSpec((1, D), const),
                  _mod_spec(0, cond, tm, D, zero),
                  _mod_spec(1, cond, tm, D, zero),
                  pl.BlockSpec((IN1, D), const, pipeline_mode=once),
                  pl.BlockSpec((1, Q_RANK), const),
                  pl.BlockSpec((1, KV_RANK), const),
                  pl.BlockSpec((Q_RANK, MLA_HEADS * QK_DIM), const, pipeline_mode=once),
                  pl.BlockSpec((KV_RANK, nkv), const, pipeline_mode=once)],
        out_specs=tuple(row(o.shape[1]) for o in outs),
        scratch_shapes=[pltpu.VMEM((IN1, D), BF16), pltpu.VMEM((Q_RANK, MLA_HEADS * QK_DIM), BF16),
                        pltpu.VMEM((KV_RANK, nkv), BF16)],
        compiler_params=_cparams(("arbitrary",)),
        name="in1_proj",
    )(x, g.reshape(1, D), modtab, modtab, w_in.T, q_norm.reshape(1, Q_RANK), kv_norm.reshape(1, KV_RANK),
      w_q_up, w_kv_up)


def _mm_kernel(a_ref, w_ref, o_ref):
    o_ref[...] = jnp.dot(a_ref[...].astype(BF16), w_ref[...].astype(BF16),
                         preferred_element_type=F32).astype(o_ref.dtype)


def kv_up(ckv, w_kv_up):
    rows = ckv.shape[0]
    n = w_kv_up.shape[1]
    return pl.pallas_call(
        _mm_kernel,
        out_shape=jax.ShapeDtypeStruct((rows, n), BF16),
        grid=(rows // TM,),
        in_specs=[pl.BlockSpec((TM, KV_RANK), lambda i: (i, 0)), pl.BlockSpec((KV_RANK, n), lambda i: (0, 0))],
        out_specs=pl.BlockSpec((TM, n), lambda i: (i, 0)),
        compiler_params=_cparams(("parallel",)),
        name="kv_up",
    )(ckv, w_kv_up)


_NT = (((1,), (1,)), ((), ()))
_SCALE = 1.0 / math.sqrt(QK_DIM)


def _fill_rope_tables(cos_ref, sin_ref):
    n, width = cos_ref.shape
    n_grid_rows = n // GRID_W
    n_freq = ROPE // 4

    def trig(count):
        lane = lax.broadcasted_iota(jnp.int32, (count, width), 1)
        j = lane & (ROPE // 2 - 1)
        inv = jnp.exp((j & (n_freq - 1)).astype(F32) * (-math.log(ROPE_THETA) / n_freq))
        ang = lax.broadcasted_iota(jnp.int32, (count, width), 0).astype(F32) * inv
        return jnp.cos(ang), jnp.sin(ang), j < n_freq

    cos_c, sin_c, by_row = trig(GRID_W)
    cos_r, sin_r, _ = trig(n_grid_rows)
    for r in range(n_grid_rows):
        rows = slice(r * GRID_W, (r + 1) * GRID_W)
        cos_ref[rows, :] = jnp.where(by_row, jnp.broadcast_to(cos_r[r:r + 1], cos_c.shape), cos_c)
        sin_ref[rows, :] = jnp.where(by_row, jnp.broadcast_to(sin_r[r:r + 1], sin_c.shape), sin_c)


def _rope(x, cos, sin):
    width = x.shape[1]
    lane = lax.broadcasted_iota(jnp.int32, x.shape, 1)
    first_half = (lane & (ROPE - 1)) < ROPE // 2
    xr = jnp.where(first_half, -pltpu.roll(x, width - ROPE // 2, 1), pltpu.roll(x, ROPE // 2, 1))
    return x * cos + xr * sin


def _ones_column(n):
    lane = lax.broadcasted_iota(jnp.int32, (n, VDIM), 1)
    return jnp.where(lane == 0, 1.0, 0.0).astype(BF16)


def _head_attention(qcat, kcat, vaug):
    s = lax.dot_general(qcat, kcat, _NT, preferred_element_type=F32)
    p = jnp.exp(s - jnp.max(s, axis=-1, keepdims=True)).astype(BF16)
    oa = jnp.dot(p, vaug, preferred_element_type=F32)
    return oa[:, :VDIM] / oa[:, VDIM:VDIM + 1]


def _attn_ctx_kernel(qn_ref, qpe_ref, kv_ref, kr_ref, o_ref, *, seq_len):
    n = qn_ref.shape[0]
    n_seq = n // seq_len
    ones = _ones_column(n)
    kpe = kr_ref[...].astype(BF16)
    per_seq = lambda a: a.reshape(n_seq, seq_len, a.shape[-1])
    for h in range(MLA_HEADS):
        c0 = h * (NOPE + VDIM)
        qcat = per_seq(jnp.concatenate([qn_ref[:, h * NOPE:(h + 1) * NOPE],
                                        qpe_ref[:, h * ROPE:(h + 1) * ROPE].astype(BF16)], axis=1))
        kcat = per_seq(jnp.concatenate([kv_ref[:, c0:c0 + NOPE], kpe], axis=1))
        vaug = per_seq(jnp.concatenate([kv_ref[:, c0 + NOPE:c0 + NOPE + VDIM], ones], axis=1))
        s = jnp.einsum("bqd,bkd->bqk", qcat, kcat, preferred_element_type=F32)
        p = jnp.exp(s - jnp.max(s, axis=-1, keepdims=True)).astype(BF16)
        oa = jnp.einsum("bqk,bkd->bqd", p, vaug, preferred_element_type=F32)
        o = oa[:, :, :VDIM] / oa[:, :, VDIM:VDIM + 1]
        o_ref[:, h * VDIM:(h + 1) * VDIM] = o.reshape(n, VDIM).astype(o_ref.dtype)


def attn_ctx(qnope, qpe, kv, kr, seq_len):
    tokens = qnope.shape[0]
    rows = ATTN_CTX_SEQS * seq_len
    blk = lambda w: pl.BlockSpec((rows, w), lambda s: (s, 0))
    return pl.pallas_call(
        functools.partial(_attn_ctx_kernel, seq_len=seq_len),
        out_shape=jax.ShapeDtypeStruct((tokens, MLA_HEADS * VDIM), BF16),
        grid=(tokens // rows,),
        in_specs=[blk(MLA_HEADS * NOPE), blk(MLA_HEADS * ROPE), blk(MLA_HEADS * (NOPE + VDIM)), blk(ROPE)],
        out_specs=blk(MLA_HEADS * VDIM),
        compiler_params=_cparams(("parallel",)),
        name="attn_ctx",
    )(qnope, qpe, kv, kr)


def _attn_lat_kernel(qn_ref, qpe_ref, kvc_ref, krc_ref, kvl_ref, krl_ref, o_ref, kcat_sc, vaug_sc, cos_sc, sin_sc):
    tq = qn_ref.shape[0]
    n_ctx = krc_ref.shape[0]
    n_lat = krl_ref.shape[0]

    @pl.when(pl.program_id(1) == 0)
    def _():
        _fill_rope_tables(cos_sc, sin_sc)
        kr2 = jnp.concatenate([krl_ref[...], krl_ref[...]], axis=1)
        kpe_lat = _rope(kr2, cos_sc[...], sin_sc[...])[:, :ROPE].astype(BF16)
        kpe_ctx = krc_ref[...].astype(BF16)
        ones_c, ones_l = _ones_column(n_ctx), _ones_column(n_lat)
        for h in range(MLA_HEADS):
            c0 = h * (NOPE + VDIM)
            for r0, nr, kv_ref, kpe, ones in ((0, n_ctx, kvc_ref, kpe_ctx, ones_c), (n_ctx, n_lat, kvl_ref, kpe_lat, ones_l)):
                kcat_sc[h, r0:r0 + nr, 0:NOPE] = kv_ref[:, c0:c0 + NOPE]
                kcat_sc[h, r0:r0 + nr, NOPE:QK_DIM] = kpe
                vaug_sc[h, r0:r0 + nr, 0:VDIM] = kv_ref[:, c0 + NOPE:c0 + NOPE + VDIM]
                vaug_sc[h, r0:r0 + nr, VDIM:2 * VDIM] = ones

    q0 = pl.multiple_of(pl.program_id(1) * tq, tq)
    rep = lambda a: jnp.concatenate([a] * (MLA_HEADS // 2), axis=1)
    qp_all = _rope(qpe_ref[...], rep(cos_sc[pl.ds(q0, tq), :]), rep(sin_sc[pl.ds(q0, tq), :])).astype(BF16)
    for h in range(MLA_HEADS):
        qcat = jnp.concatenate([qn_ref[:, h * NOPE:(h + 1) * NOPE], qp_all[:, h * ROPE:(h + 1) * ROPE]], axis=1)
        o_ref[:, h * VDIM:(h + 1) * VDIM] = _head_attention(qcat, kcat_sc[h], vaug_sc[h]).astype(o_ref.dtype)


def attn_lat(qnope, qpe, kv_ctx, kr_ctx, kv_lat, kr_lat, seq_len, ctx_len):
    tokens = qnope.shape[0]
    nq = seq_len // TQ
    qblk = lambda w: pl.BlockSpec((TQ, w), lambda b, i: (b * nq + i, 0))
    seq = lambda n, w: pl.BlockSpec((n, w), lambda b, i: (b, 0))
    nkv = MLA_HEADS * (NOPE + VDIM)
    n_keys = ctx_len + seq_len
    return pl.pallas_call(
        _attn_lat_kernel,
        out_shape=jax.ShapeDtypeStruct((tokens, MLA_HEADS * VDIM), BF16),
        grid=(tokens // seq_len, nq),
        in_specs=[qblk(MLA_HEADS * NOPE), qblk(MLA_HEADS * ROPE), seq(ctx_len, nkv), seq(ctx_len, ROPE),
                  seq(seq_len, nkv), seq(seq_len, ROPE)],
        out_specs=qblk(MLA_HEADS * VDIM),
        scratch_shapes=[pltpu.VMEM((MLA_HEADS, n_keys, QK_DIM), BF16),
                        pltpu.VMEM((MLA_HEADS, n_keys, 2 * VDIM), BF16),
                        pltpu.VMEM((seq_len, 2 * ROPE), F32), pltpu.VMEM((seq_len, 2 * ROPE), F32)],
        compiler_params=_cparams(("parallel", "arbitrary")),
        name="attn_lat",
    )(qnope, qpe, kv_ctx, kr_ctx, kv_lat, kr_lat)


def _dft_kernel(o_ref):
    tr, n = o_ref.shape[1], o_ref.shape[2]
    nb = n // V7X_LANES
    f = pl.program_id(0) * tr + lax.broadcasted_iota(jnp.int32, (tr, V7X_LANES), 0)
    j = lax.broadcasted_iota(jnp.int32, (tr, V7X_LANES), 1)

    def cos_sin(m):
        ang = (m & (2 * n - 1)).astype(F32) * (math.pi / n)
        return jnp.cos(ang), jnp.sin(ang)

    cj, sj = cos_sin(f * j)
    cb, sb = cos_sin(f * (j * V7X_LANES))
    for b in range(nb):
        cbb, sbb = cb[:, b:b + 1], sb[:, b:b + 1]
        cols = slice(b * V7X_LANES, (b + 1) * V7X_LANES)
        o_ref[0, :, cols] = (cbb * cj - sbb * sj).astype(o_ref.dtype)
        o_ref[1, :, cols] = (sbb * cj + cbb * sj).astype(o_ref.dtype)


def dft_tables(n):
    tr = 128
    return pl.pallas_call(
        _dft_kernel,
        out_shape=jax.ShapeDtypeStruct((2, n, n), BF16),
        grid=(n // tr,),
        out_specs=pl.BlockSpec((2, tr, n), lambda i: (0, i, 0)),
        compiler_params=_cparams(("parallel",)),
        name="dft_tables",
    )()


def _split_dot(table, x):
    hi = x.astype(BF16)
    lo = (x - hi.astype(F32)).astype(BF16)
    return (jnp.dot(table, hi, preferred_element_type=F32) + jnp.dot(table, lo, preferred_element_type=F32))


def _hy_filter_kernel(cs_ref, w1_ref, b1_ref, w2_ref, b2_ref, w3_ref, kr_ref, ks_ref, kny_ref):
    n = cs_ref.shape[1]
    row = lax.broadcasted_iota(jnp.int32, (n, V7X_LANES), 0).astype(F32)
    lane = lax.broadcasted_iota(jnp.int32, (n, V7X_LANES), 1)
    t = row * (1.0 / (n - 1))
    w = (2.0 * math.pi) * row / n
    band = jnp.where(lane <= HY_BANDS, lane - 1, lane - 1 - HY_BANDS).astype(F32)
    freq = 1e-4 + band * ((HY_BANDS - 1 - 1e-4) / (HY_BANDS - 1))
    arg = jnp.where(lane <= HY_BANDS, freq * w + 0.5 * math.pi, -(freq * w))
    z = jnp.where(lane == 0, t, jnp.where(lane <= 2 * HY_BANDS, jnp.sin(arg), 0.0))
    hid = jnp.sin(_dot3(z, w1_ref[...]) + b1_ref[...])
    hid = jnp.sin(_dot3(hid, w2_ref[...]) + b2_ref[...])
    hf = _dot3(hid, w3_ref[...])

    rowc = lax.broadcasted_iota(jnp.int32, (n, HY_W), 0)
    chan = lax.broadcasted_iota(jnp.int32, (n, HY_W), 1).astype(F32)
    max_decay = math.log(HY_TARGET) / HY_FAST_DECAY
    min_decay = math.log(HY_TARGET) / HY_SLOW_DECAY
    deltas = min_decay + chan * ((max_decay - min_decay) / (HY_W - 1))
    decay = jnp.exp(-(rowc.astype(F32) * (1.0 / (n - 1))) * jnp.abs(deltas))
    h_fwd = hf[:, :HY_W] * decay
    h_bwd = jnp.where(rowc == 0, 0.0, hf[:, HY_W:] * decay)
    norm = jnp.sum(jnp.abs(h_fwd) + jnp.abs(h_bwd), axis=0, keepdims=True)
    even = (h_fwd + h_bwd) / norm
    odd = (h_fwd - h_bwd) / norm
    cf = jnp.where(rowc == 0, 1.0, 2.0) * (1.0 / (2 * n))
    kr_ref[...] = cf * _split_dot(cs_ref[0], even)
    ks_ref[...] = cf * _split_dot(cs_ref[1], odd)
    sgn = jnp.where((rowc & 1) == 1, -1.0, 1.0)
    kny_ref[...] = jnp.sum(sgn * even, axis=0, keepdims=True) * (1.0 / (2 * n))


def hy_filter(cs, w1p, b1p, w2p, b2p, w3p):
    n = cs.shape[1]
    full = lambda a: pl.BlockSpec(a.shape, lambda: (0,) * a.ndim)
    args = (cs, w1p, b1p, w2p, b2p, w3p)
    return pl.pallas_call(
        _hy_filter_kernel,
        out_shape=(jax.ShapeDtypeStruct((n, HY_W), F32), jax.ShapeDtypeStruct((n, HY_W), F32),
                   jax.ShapeDtypeStruct((1, HY_W), F32)),
        in_specs=[full(a) for a in args],
        out_specs=(pl.BlockSpec((n, HY_W), lambda: (0, 0)), pl.BlockSpec((n, HY_W), lambda: (0, 0)),
                   pl.BlockSpec((1, HY_W), lambda: (0, 0))),
        compiler_params=pltpu.CompilerParams(vmem_limit_bytes=V7X_VMEM_LIMIT_BYTES),
        name="hy_filter",
    )(*args)


def _hyena_kernel(u0_ref, u1_ref, u2_ref, sw_ref, sb_ref, cs_ref, kr_ref, ks_ref, kny_ref, bias_ref, o_ref,
                  *, seq_len):
    n, cb = u0_ref.shape
    n_seq = n // seq_len
    t = lax.broadcasted_iota(jnp.int32, (n, cb), 0) & (seq_len - 1)

    def short_conv(u_ref, k):
        u = u_ref[...]
        w = sw_ref[:, k * cb:(k + 1) * cb]
        return (sb_ref[:, k * cb:(k + 1) * cb] + w[0:1] * _shift_rows(u, 1, t, seq_len) + w[1:2] * u
                + w[2:3] * _shift_rows(u, -1, t, seq_len))

    x0 = short_conv(u0_ref, 0)
    z = short_conv(u1_ref, 1) * short_conv(u2_ref, 2)
    wide = lambda a: jnp.concatenate([a[q * seq_len:(q + 1) * seq_len] for q in range(n_seq)], axis=1)
    rep = lambda a: jnp.concatenate([a] * n_seq, axis=1)
    zw = wide(z)
    zb = zw.astype(BF16)
    c, s = cs_ref[0], cs_ref[1]
    ur = jnp.dot(c, zb, preferred_element_type=F32)
    us = jnp.dot(s, zb, preferred_element_type=F32)
    sgn = jnp.where((lax.broadcasted_iota(jnp.int32, zw.shape, 0) & 1) == 1, -1.0, 1.0)
    uny = jnp.sum(sgn * zw, axis=0, keepdims=True)
    kr, ks = rep(kr_ref[...]), rep(ks_ref[...])
    yr = (ur * kr - us * ks).astype(BF16)
    ys = (ur * ks + us * kr).astype(BF16)
    yw = jnp.dot(c, yr, preferred_element_type=F32) + jnp.dot(s, ys, preferred_element_type=F32)
    yw = yw + sgn * (uny * rep(kny_ref[...]))
    y = jnp.concatenate([yw[:, q * cb:(q + 1) * cb] for q in range(n_seq)], axis=0)
    o_ref[...] = (x0 * (y + bias_ref[...] * z)).astype(o_ref.dtype)


def hyena(uh, seq_len, short_w, short_b, cs, kr, ks, kny, bias):
    tokens = uh.shape[0]
    cb = HY_CB
    nc = HY_W // cb
    rows = max(seq_len, HY_ROWS)
    assert seq_len & (seq_len - 1) == 0 and rows % seq_len == 0
    ublk = lambda k: pl.BlockSpec((rows, cb), lambda s, c: (s, k * nc + c))
    chan = lambda r: pl.BlockSpec((r, cb), lambda s, c: (0, c))
    return pl.pallas_call(
        functools.partial(_hyena_kernel, seq_len=seq_len),
        out_shape=jax.ShapeDtypeStruct((tokens, HY_W), BF16),
        grid=(tokens // rows, nc),
        in_specs=[ublk(0), ublk(1), ublk(2),
                  pl.BlockSpec((None, 3, 3 * cb), lambda s, c: (c, 0, 0)),
                  pl.BlockSpec((None, 1, 3 * cb), lambda s, c: (c, 0, 0)),
                  pl.BlockSpec((2, seq_len, seq_len), lambda s, c: (0, 0, 0)),
                  chan(seq_len), chan(seq_len), chan(1), chan(1)],
        out_specs=pl.BlockSpec((rows, cb), lambda s, c: (s, c)),
        compiler_params=_cparams(("parallel", "parallel")),
        name="hyena",
    )(uh, uh, uh, short_w, short_b, cs, kr, ks, kny, bias)


META_E1, META_E2, META_R1, META_R2, META_G1, META_G2 = range(6)


def _route_kernel(p0_ref, p1_ref, p2_ref, wo_ref, x_ref, g1_ref, g_ref, sh_ref, sc_ref, wr_ref, br_ref,
                  x1_ref, h_ref, meta_ref, meta_t_ref, cnt_ref, run_sc, wo_sc):
    tm = x_ref.shape[0]
    lane = lax.broadcasted_iota(jnp.int32, (tm, V7X_LANES), 1)

    @pl.when(pl.program_id(0) == 0)
    def _():
        run_sc[...] = jnp.zeros_like(run_sc)
        wo_sc[...] = wo_ref[...].astype(BF16)

    kb = p0_ref.shape[1]
    m = jnp.dot(p0_ref[...], wo_sc[0:kb, :], preferred_element_type=F32)
    m += jnp.dot(p1_ref[...], wo_sc[kb:2 * kb, :], preferred_element_type=F32)
    m += jnp.dot(p2_ref[...], wo_sc[2 * kb:3 * kb, :], preferred_element_type=F32)
    x1 = x_ref[...] + g1_ref[...] * m
    x1_ref[...] = x1
    h = _norm_mod(x1, g_ref[...], sh_ref[...], sc_ref[...])
    h_ref[...] = h
    logits = _dot3(h, wr_ref[...]) + br_ref[...]
    lg = jnp.where(lane < N_EXPERTS, logits, -jnp.inf)
    l1 = jnp.max(lg, axis=-1, keepdims=True)
    i1 = jnp.min(jnp.where(lg == l1, lane, V7X_LANES), axis=-1, keepdims=True)
    rest = jnp.where(lane == i1, -jnp.inf, lg)
    l2 = jnp.max(rest, axis=-1, keepdims=True)
    i2 = jnp.min(jnp.where(rest == l2, lane, V7X_LANES), axis=-1, keepdims=True)
    gap = jnp.exp(l2 - l1)
    gate1 = 1.0 / (1.0 + gap)
    gate2 = gap * gate1
    m1 = lane == i1
    m2 = lane == i2
    chosen = jnp.where(m1 | m2, 1.0, 0.0)
    r = lax.broadcasted_iota(jnp.int32, (tm, tm), 0)
    c = lax.broadcasted_iota(jnp.int32, (tm, tm), 1)
    tri = jnp.where(c < r, 1.0, 0.0).astype(BF16)
    before = jnp.dot(tri, chosen.astype(BF16), preferred_element_type=F32) + run_sc[0:1, :]
    rank1 = jnp.sum(jnp.where(m1, before, 0.0), axis=-1, keepdims=True)
    rank2 = jnp.sum(jnp.where(m2, before, 0.0), axis=-1, keepdims=True)
    vals = (i1.astype(F32), i2.astype(F32), rank1, rank2, gate1, gate2)
    meta = jnp.zeros((tm, V7X_LANES), F32)
    for k, v in enumerate(vals):
        meta = jnp.where(lane == k, v, meta)
    meta_ref[...] = meta
    meta_t_ref[...] = meta.T[:V7X_SUBLANES]
    run_sc[...] = run_sc[...] + jnp.sum(chosen, axis=0, keepdims=True)
    cnt_ref[...] = run_sc[...]


def mix_route(parts, w_out, x, g, modtab, cond, wr_pad, br_pad):
    tokens = x.shape[0]
    tm = TM_ROUTE
    kb = 512
    zero = lambda i: 0
    const = lambda i: (0, 0)
    rows = lambda w: pl.BlockSpec((tm, w), lambda i: (i, 0))
    lhs_specs = [pl.BlockSpec((tm, kb), (lambda i, cbk=cbk: (i, cbk))) for _, cbk in parts]
    return pl.pallas_call(
        _route_kernel,
        out_shape=(jax.ShapeDtypeStruct((tokens, D), F32),
                   jax.ShapeDtypeStruct((tokens, D), F32),
                   jax.ShapeDtypeStruct((tokens, V7X_LANES), F32),
                   jax.ShapeDtypeStruct((V7X_SUBLANES, tokens), F32),
                   jax.ShapeDtypeStruct((V7X_SUBLANES, V7X_LANES), F32)),
        grid=(tokens // tm,),
        in_specs=lhs_specs + [
            pl.BlockSpec((len(parts) * kb, D), const, pipeline_mode=pl.Buffered(1)),
            rows(D),
            _mod_spec(2, cond, tm, D, zero),
            pl.BlockSpec((1, D), const),
            _mod_spec(3, cond, tm, D, zero),
            _mod_spec(4, cond, tm, D, zero),
            pl.BlockSpec((D, V7X_LANES), const),
            pl.BlockSpec((1, V7X_LANES), const)],
        out_specs=(rows(D), rows(D), rows(V7X_LANES),
                   pl.BlockSpec((V7X_SUBLANES, tm), lambda i: (0, i)),
                   pl.BlockSpec((V7X_SUBLANES, V7X_LANES), const)),
        scratch_shapes=[pltpu.VMEM((V7X_SUBLANES, V7X_LANES), F32), pltpu.VMEM((len(parts) * kb, D), BF16)],
        compiler_params=_cparams(("arbitrary",)),
        name="mix_route",
    )(*[a for a, _ in parts], w_out, x, modtab, g.reshape(1, D), modtab, modtab, wr_pad, br_pad)


def _row_copy(src_ref, src_row, dst_ref, dst_row, sem):
    return pltpu.make_async_copy(src_ref.at[pl.ds(src_row, 1)], dst_ref.at[pl.ds(dst_row, 1)], sem)


_PAD_BULK = (256, 128, 64, 32, 16, 8)


def _zero_fill(hs_ref, zero_sc, sem, pads_ref, n_tail_max, wait):
    tmr = zero_sc.shape[0]

    def copy(rows, dst):
        cp = pltpu.make_async_copy(zero_sc.at[pl.ds(0, rows)], hs_ref.at[pl.ds(dst, rows)], sem)
        cp.wait() if wait else cp.start()

    for e in range(N_EXPERTS):
        start, n = pads_ref[e], pads_ref[N_EXPERTS + e]
        head = jnp.minimum((-start) & (V7X_SUBLANES - 1), n)
        for r in range(V7X_SUBLANES - 1):
            @pl.when(r < head)
            def _():
                copy(1, start + r)
        body = start + head
        rem = n - head
        for k in _PAD_BULK:
            @pl.when((rem & k) != 0)
            def _():
                copy(k, pl.multiple_of(body + (rem & ~(2 * k - 1)), V7X_SUBLANES))
    tail_start, tail_tiles = pads_ref[2 * N_EXPERTS], pads_ref[2 * N_EXPERTS + 1]
    for t in range(n_tail_max):
        @pl.when(t < tail_tiles)
        def _():
            copy(tmr, pl.multiple_of(tail_start + t * tmr, tmr))


def _dispatch_kernel(pos_ref, pads_ref, ha_ref, hb_ref, hs_ref, zero_sc, sem, zsem, *, n_a, n_tail_max):
    tm = ha_ref.shape[0]
    n_tok = pos_ref.shape[0] // 2
    i = pl.program_id(0)
    base = i * tm

    @pl.when(i == 0)
    def _():
        zero_sc[...] = jnp.zeros_like(zero_sc)
        _zero_fill(hs_ref, zero_sc, zsem, pads_ref, n_tail_max, wait=False)

    def scatter(h_ref):
        def issue(r, carry):
            _row_copy(h_ref, r, hs_ref, pos_ref[base + r], sem).start(priority=0)
            _row_copy(h_ref, r, hs_ref, pos_ref[n_tok + base + r], sem).start(priority=1)
            return carry

        lax.fori_loop(0, tm, issue, 0, unroll=8)
        for _ in range(2):
            pltpu.make_async_copy(h_ref, hs_ref.at[pl.ds(0, tm)], sem).wait()

    @pl.when(i < n_a)
    def _():
        scatter(ha_ref)

    @pl.when(i >= n_a)
    def _():
        scatter(hb_ref)

    @pl.when(i == 0)
    def _():
        _zero_fill(hs_ref, zero_sc, zsem, pads_ref, n_tail_max, wait=True)


def moe_dispatch(pos, pads, hs_rows, h_a, h_b):
    tm = TM_ROUTE
    n_a, n_b = h_a.shape[0] // tm, h_b.shape[0] // tm
    n_tail_max = hs_rows // TM_EXPERT - (2 * (h_a.shape[0] + h_b.shape[0])) // TM_EXPERT
    return pl.pallas_call(
        functools.partial(_dispatch_kernel, n_a=n_a, n_tail_max=n_tail_max),
        out_shape=jax.ShapeDtypeStruct((hs_rows, D), F32),
        grid_spec=pltpu.PrefetchScalarGridSpec(
            num_scalar_prefetch=2,
            grid=(n_a + n_b,),
            in_specs=[pl.BlockSpec((tm, D), lambda i, *pf: (jnp.minimum(i, n_a - 1), 0)),
                      pl.BlockSpec((tm, D), lambda i, *pf: (jnp.clip(i - n_a, 0, n_b - 1), 0))],
            out_specs=pl.BlockSpec(memory_space=pl.ANY),
            scratch_shapes=[pltpu.VMEM((TM_EXPERT, D), F32), pltpu.SemaphoreType.DMA(()),
                            pltpu.SemaphoreType.DMA(())]),
        compiler_params=_cparams(("arbitrary",)),
        name="moe_dispatch",
    )(pos, pads, h_a, h_b)


def _experts_kernel(te_ref, sg_ref, su_ref, sd_ref, nv_ref, hs_ref, wg_ref, wu_ref, wd_ref, y_ref,
                    wg_sc, wu_sc, wd_sc):
    del sg_ref, su_ref, sd_ref
    j = pl.program_id(0)
    e = te_ref[j]
    e_prev = te_ref[jnp.maximum(j - 1, 0)]
    n_valid = nv_ref[j]
    half = y_ref.shape[0] // 2

    @pl.when((j == 0) | (e != e_prev))
    def _():
        wg_sc[...] = wg_ref[...].astype(BF16)
        wu_sc[...] = wu_ref[...].astype(BF16)
        wd_sc[...] = wd_ref[...].astype(BF16)

    def swiglu(rows):
        h = hs_ref[rows, :].astype(BF16)
        y = None
        for c0 in range(0, D_FF_EXPERT, MOE_CHUNK):
            c1 = min(c0 + MOE_CHUNK, D_FF_EXPERT)
            hg = jnp.dot(h, wg_sc[:, c0:c1], preferred_element_type=F32)
            hu = jnp.dot(h, wu_sc[:, c0:c1], preferred_element_type=F32)
            act = (_silu(hg) * hu).astype(BF16)
            yc = jnp.dot(act, wd_sc[c0:c1, :], preferred_element_type=F32)
            y = yc if y is None else y + yc
        y_ref[rows, :] = y

    @pl.when(n_valid > half)
    def _():
        swiglu(slice(None))

    @pl.when((n_valid > 0) & (n_valid <= half))
    def _():
        swiglu(slice(0, half))
        y_ref[half:, :] = jnp.zeros((half, D), F32)

    @pl.when(n_valid == 0)
    def _():
        y_ref[...] = jnp.zeros_like(y_ref)


def moe_experts(tile_expert, stages, tile_valid, hs, e_gate, e_up, e_down):
    rows = hs.shape[0]
    tmr = TM_EXPERT
    wspec = lambda shape, k: pl.BlockSpec((None,) + shape, lambda j, *pf: (pf[1 + k][j], 0, 0))
    return pl.pallas_call(
        _experts_kernel,
        out_shape=jax.ShapeDtypeStruct((rows, D), F32),
        grid_spec=pltpu.PrefetchScalarGridSpec(
            num_scalar_prefetch=5,
            grid=(rows // tmr,),
            in_specs=[pl.BlockSpec((tmr, D), lambda j, *pf: (j, 0)),
                      wspec((D, D_FF_EXPERT), 0), wspec((D, D_FF_EXPERT), 1), wspec((D_FF_EXPERT, D), 2)],
            out_specs=pl.BlockSpec((tmr, D), lambda j, *pf: (j, 0)),
            scratch_shapes=[pltpu.VMEM((D, D_FF_EXPERT), BF16), pltpu.VMEM((D, D_FF_EXPERT), BF16),
                            pltpu.VMEM((D_FF_EXPERT, D), BF16)]),
        compiler_params=_cparams(("arbitrary",)),
        name="moe_experts",
    )(tile_expert, *stages, tile_valid, hs, e_gate, e_up, e_down)


def _combine_kernel(pos_ref, x_ref, meta_ref, gt_ref, fg_ref, y_ref, o_ref, b1_sc, b2_sc, sem):
    tm = x_ref.shape[0]
    n_tok = pos_ref.shape[0] // 2
    i = pl.program_id(0)

    def gather(tile, slot):
        base = tile * tm

        def issue(r, carry):
            _row_copy(y_ref, pos_ref[base + r], b1_sc.at[slot], r, sem.at[slot]).start(priority=0)
            _row_copy(y_ref, pos_ref[n_tok + base + r], b2_sc.at[slot], r, sem.at[slot]).start(priority=1)
            return carry

        lax.fori_loop(0, tm, issue, 0, unroll=8)

    @pl.when(i == 0)
    def _():
        gather(0, 0)

    @pl.when(i + 1 < pl.num_programs(0))
    def _():
        gather(i + 1, (i + 1) % 2)

    slot = i % 2
    pltpu.make_async_copy(y_ref.at[pl.ds(0, tm)], b1_sc.at[slot], sem.at[slot]).wait()
    pltpu.make_async_copy(y_ref.at[pl.ds(0, tm)], b2_sc.at[slot], sem.at[slot]).wait()

    meta = meta_ref[...]
    lane = lax.broadcasted_iota(jnp.int32, meta.shape, 1)
    g1 = jnp.sum(jnp.where(lane == META_G1, meta, 0.0), axis=-1, keepdims=True)
    g2 = jnp.sum(jnp.where(lane == META_G2, meta, 0.0), axis=-1, keepdims=True)
    x = x_ref[...] + gt_ref[...] * (g1 * b1_sc[slot] + g2 * b2_sc[slot])
    o_ref[...] = _rms(x, fg_ref[...])


def moe_combine(pos, x, meta, modtab, cond, final_g, y):
    tokens = x.shape[0]
    tm = TM_COMBINE
    return pl.pallas_call(
        _combine_kernel,
        out_shape=jax.ShapeDtypeStruct((tokens, D), F32),
        grid_spec=pltpu.PrefetchScalarGridSpec(
            num_scalar_prefetch=1,
            grid=(tokens // tm,),
            in_specs=[pl.BlockSpec((tm, D), lambda i, pos: (i, 0)),
                      pl.BlockSpec((tm, V7X_LANES), lambda i, pos: (i, 0)),
                      _mod_spec(5, cond, tm, D, lambda i, pos: 0),
                      pl.BlockSpec((1, D), lambda i, pos: (0, 0)),
                      pl.BlockSpec(memory_space=pl.ANY)],
            out_specs=pl.BlockSpec((tm, D), lambda i, pos: (i, 0)),
            scratch_shapes=[pltpu.VMEM((2, tm, D), F32), pltpu.VMEM((2, tm, D), F32),
                            pltpu.SemaphoreType.DMA((2,))]),
        compiler_params=_cparams(("arbitrary",)),
        name="moe_combine",
    )(pos, x, meta, modtab, final_g.reshape(1, D), y)


def moe_plan(metas, counts):
    tmr = TM_EXPERT
    cnts = [c[0, :N_EXPERTS].astype(jnp.int32) for c in counts]
    total = functools.reduce(jnp.add, cnts)
    padded = ((total + tmr - 1) // tmr) * tmr
    ends = jnp.cumsum(padded)
    starts = ends - padded
    n_rows = sum(m.shape[1] for m in metas) * 2 + N_EXPERTS * tmr
    n_tiles = n_rows // tmr
    tile_start = jnp.arange(n_tiles, dtype=jnp.int32) * tmr
    tile_expert = jnp.minimum(jnp.sum(tile_start[:, None] >= ends[None, :], axis=1), N_EXPERTS - 1).astype(jnp.int32)
    group_of_tile = jnp.sum(tile_start[:, None] >= ends[None, :], axis=1)
    real_end = jnp.sum(jnp.where(group_of_tile[:, None] == jnp.arange(N_EXPERTS)[None, :],
                                 (starts + total)[None, :], 0), axis=1)
    tile_valid = jnp.clip(real_end - tile_start, 0, tmr).astype(jnp.int32)
    eid = jnp.arange(N_EXPERTS, dtype=jnp.int32)
    later = jnp.where((eid[None, :] > eid[:, None]) & (padded[None, :] > 0), eid[None, :], N_EXPERTS)
    nxt = jnp.min(later, axis=1)
    next_used = jnp.where(nxt == N_EXPERTS, eid, nxt)
    pick = lambda table: jnp.sum(jnp.where(tile_expert[:, None] == eid[None, :], table[None, :], 0), axis=1)
    k_in_group = (tile_start - pick(starts)) // tmr
    tile_next = pick(next_used)
    stages = [jnp.where(k_in_group < k, tile_expert, tile_next).astype(jnp.int32) for k in (1, 2, 3)]
    pos, p1s, p2s = [], [], []
    base = jnp.zeros((N_EXPERTS,), jnp.int32)
    for m, c in zip(metas, cnts):
        first = starts + base
        sel = lambda field: m[field].astype(jnp.int32)
        lookup = lambda e: jnp.sum(jnp.where(e[:, None] == jnp.arange(N_EXPERTS)[None, :], first[None, :], 0), axis=1)
        p1 = lookup(sel(META_E1)) + sel(META_R1)
        p2 = lookup(sel(META_E2)) + sel(META_R2)
        pos.append(jnp.concatenate([p1, p2]).astype(jnp.int32))
        p1s.append(p1)
        p2s.append(p2)
        base = base + c
    pos_all = jnp.concatenate(p1s + p2s).astype(jnp.int32)
    pads = jnp.concatenate([starts + total, padded - total,
                            jnp.stack([ends[-1], n_tiles - ends[-1] // tmr])]).astype(jnp.int32)
    return pos, pos_all, pads, tile_expert, stages, tile_valid, n_rows


def _pad_to(a, shape):
    return jnp.pad(a, [(0, t - s) for s, t in zip(a.shape, shape)])


def _regroup_chunks(a, cb):
    r = a.shape[0]
    return a.reshape(r, 3, HY_W // cb, cb).transpose(2, 0, 1, 3).reshape(HY_W // cb, r, 3 * cb)


def kernel(x_prompt, x_sample, state_l0_lru, cache_l1_ckv, cache_l1_krope, c, c_ctx, l0_norm1, l0_norm2, l0_w_mod, l0_b_mod, l0_w_in, l0_conv_a, l0_lru_conv_w, l0_lru_conv_b, l0_lru_wa, l0_lru_ba, l0_lru_wi, l0_lru_bi, l0_lru_lambda, l0_w_out, l0_ffn_gate, l0_ffn_up, l0_ffn_down, l1_norm1, l1_norm2, l1_w_mod, l1_b_mod, l1_w_in, l1_q_norm, l1_kv_norm, l1_w_q_up, l1_w_kv_up, l1_hy_short_w, l1_hy_short_b, l1_hy_f_w1, l1_hy_f_b1, l1_hy_f_w2, l1_hy_f_b2, l1_hy_f_w3, l1_hy_bias, l1_w_out, l1_router_w, l1_router_b, l1_exp_gate, l1_exp_up, l1_exp_down, final_norm):
    batch, seq, _ = x_prompt.shape
    dec_batch, dec_seq, _ = x_sample.shape
    past_len = cache_l1_ckv.shape[1]

    cond8 = jnp.concatenate([c_ctx[None, :], c, jnp.zeros((V7X_SUBLANES - 1 - dec_batch, D), F32)], axis=0)
    wcat = jnp.concatenate([l0_lru_wa[0], l0_lru_wi[0], l0_lru_wa[1], l0_lru_wi[1]], axis=-1)
    hid = V7X_LANES
    w1p = _pad_to(l1_hy_f_w1, (hid, hid))
    b1p = _pad_to(l1_hy_f_b1.reshape(1, -1), (1, hid))
    w2p = _pad_to(l1_hy_f_w2, (hid, hid))
    b2p = _pad_to(l1_hy_f_b2.reshape(1, -1), (1, hid))
    w3p = _pad_to(l1_hy_f_w3, (hid, 2 * HY_W))
    short_w = _regroup_chunks(l1_hy_short_w, HY_CB)
    short_b = _regroup_chunks(l1_hy_short_b.reshape(1, -1), HY_CB)
    hy_bias = l1_hy_bias.reshape(1, HY_W)
    wr_pad = _pad_to(l1_router_w, (D, V7X_LANES))
    br_pad = _pad_to(l1_router_b.reshape(1, -1), (1, V7X_LANES))

    mod0, mod1 = adaln_tables(cond8, ((l0_w_mod, l0_b_mod), (l1_w_mod, l1_b_mod)))

    kv_ctx = kv_up(cache_l1_ckv.reshape(dec_batch * past_len, KV_RANK), l1_w_kv_up)
    kr_ctx = cache_l1_krope.reshape(dec_batch * past_len, ROPE)

    conds = ((0, batch * seq), (1, dec_seq))
    seq_lens = (seq, dec_seq)
    xs = (x_prompt.reshape(batch * seq, D), x_sample.reshape(dec_batch * dec_seq, D))
    h0s = (jnp.zeros((batch, 2, LRU_W), F32), state_l0_lru)

    us = in0_proj(xs, l0_norm1, mod0, conds, l0_w_in)
    parts, lru_states = [], []
    for u, seq_len, h0 in zip(us, seq_lens, h0s):
        ya = conv_a(u, seq_len, l0_conv_a)
        yb, lru_state = rglru(u, seq_len, l0_lru_conv_w, l0_lru_conv_b, wcat, l0_lru_ba, l0_lru_bi,
                              l0_lru_lambda, h0)
        parts.append([(ya, 0), (yb, 0), (yb, 1)])
        lru_states.append(lru_state)
    xs = mix_ffn(parts, l0_w_out, xs, l0_norm2, mod0, conds, l0_ffn_gate, l0_ffn_up, l0_ffn_down)
    new_lru = lru_states[0]

    def layer1(x, seq_len, cond, latent):
        qnope, qpe, ckv, kr, kv, uh = in1_proj(x, l1_norm1, mod1, cond, l1_w_in, l1_q_norm, l1_kv_norm,
                                               l1_w_q_up, l1_w_kv_up)
        if latent:
            yc = attn_lat(qnope, qpe, kv_ctx, kr_ctx, kv, kr, seq_len, past_len)
        else:
            yc = attn_ctx(qnope, qpe, kv, kr, seq_len)
        cs = dft_tables(seq_len)
        k_r, k_s, k_ny = hy_filter(cs, w1p, b1p, w2p, b2p, w3p)
        yd = hyena(uh, seq_len, short_w, short_b, cs, k_r, k_s, k_ny, hy_bias)
        routed = mix_route([(yc, 0), (yc, 1), (yd, 0)], l1_w_out, x, l1_norm2, mod1, cond, wr_pad, br_pad)
        return routed, ckv, kr

    r_p, new_ckv, new_kr = layer1(xs[0], seq, conds[0], latent=False)
    r_s, _, _ = layer1(xs[1], dec_seq, conds[1], latent=True)

    routed = (r_p, r_s)
    pos, pos_all, pads, tile_expert, stages, tile_valid, n_rows = moe_plan([r[3] for r in routed],
                                                                          [r[4] for r in routed])
    hs = moe_dispatch(pos_all, pads, n_rows, r_p[1], r_s[1])
    y_rows = moe_experts(tile_expert, stages, tile_valid, hs, l1_exp_gate, l1_exp_up, l1_exp_down)
    y_p, y_s = [moe_combine(p, r[0], r[2], mod1, cond, final_norm, y_rows)
                for p, r, cond in zip(pos, routed, conds)]
    return (y_p.reshape(batch, seq, D), y_s.reshape(dec_batch, dec_seq, D), new_lru,
            new_ckv.reshape(batch, seq, KV_RANK), new_kr.reshape(batch, seq, ROPE))
```

```python
import functools
import math

import jax
import jax.numpy as jnp
from jax import lax
from jax.experimental import pallas as pl
from jax.experimental.pallas import tpu as pltpu

F32 = jnp.float32
BF16 = jnp.bfloat16
HIGHEST = lax.Precision.HIGHEST

D = 1024
GRID_W = 64
EPS = 1e-6
CONV_W = 512
LRU_W = 1024
LRU_BW = 128
LRU_C = 8.0
MLA_HEADS = 8
Q_RANK = 384
KV_RANK = 256
NOPE = 128
ROPE = 64
VDIM = 128
QK_DIM = NOPE + ROPE
ROPE_THETA = 10000.0
HY_W = 512
HY_BANDS = 16
HY_TARGET = 1e-2
HY_FAST_DECAY = 0.3
HY_SLOW_DECAY = 1.5
D_FF = 2816
N_EXPERTS = 8
D_FF_EXPERT = 1408
IN0 = 3 * CONV_W + 2 * LRU_W
IN1 = Q_RANK + KV_RANK + ROPE + 3 * HY_W

V7X_LANES = 128
V7X_SUBLANES = 8
V7X_VMEM_LIMIT_BYTES = 56 * 1024 * 1024
V7X_VMEM_LIMIT_LARGE_BYTES = 60 * 1024 * 1024

TM = 512
TN_IN0 = 512
TF_FFN = 256
MOE_CHUNK = 256
TM_ROUTE = 512
TM_EXPERT = 512
TM_COMBINE = 512
LRU_CB = 256
HY_CB = 256
TQ = 256
ATTN_CTX_SEQS = 4
CONV_A_ROWS = 1024
LRU_ROWS = 1024
HY_ROWS = 1024
TM_IN1 = 512


def _cparams(sem, vmem_limit_bytes=V7X_VMEM_LIMIT_BYTES):
    return pltpu.CompilerParams(dimension_semantics=sem, vmem_limit_bytes=vmem_limit_bytes)


def _sigmoid(x):
    return 0.5 * jnp.tanh(0.5 * x) + 0.5


def _silu(x):
    return x * _sigmoid(x)


def _norm_mod(x, g, shift, scale):
    ms = jnp.mean(x * x, axis=-1, keepdims=True)
    y = x * lax.rsqrt(ms + EPS) * g
    return y * (1.0 + scale) + shift


def _mod_spec(comp, cond, tm, width, col_fn, tile_fn=lambda *ids: ids[0]):
    row0, seg = cond
    assert seg % tm == 0
    return pl.BlockSpec((None, 1, width),
                        lambda *ids: (comp * 3 + row0 + (tile_fn(*ids) * tm) // seg, 0, col_fn(*ids)))


def _dot3(a, b):
    a_hi = a.astype(BF16)
    a_lo = (a - a_hi.astype(F32)).astype(BF16)
    b_hi = b.astype(BF16)
    b_lo = (b - b_hi.astype(F32)).astype(BF16)
    n = a.shape[0]
    y = jnp.dot(jnp.concatenate([a_hi, a_lo], axis=0), b_hi, preferred_element_type=F32)
    return y[:n] + y[n:] + jnp.dot(a_hi, b_lo, preferred_element_type=F32)


def _adaln_kernel(c_ref, w0_ref, b0_ref, w1_ref, b1_ref, o_ref):
    a = _silu(c_ref[...])
    for layer, (w_ref, b_ref) in enumerate(((w0_ref, b0_ref), (w1_ref, b1_ref))):
        @pl.when(pl.program_id(0) == layer)
        def _():
            o_ref[...] = _dot3(a, w_ref[...]) + b_ref[...]


def adaln_tables(cond8, mods):
    tn = 1536
    nj = 6 * D // tn
    (w0, b0), (w1, b1) = mods
    at0 = lambda l, j: (0, jnp.where(l == 0, j, nj - 1))
    at1 = lambda l, j: (0, jnp.where(l == 1, j, 0))
    m = pl.pallas_call(
        _adaln_kernel,
        out_shape=jax.ShapeDtypeStruct((2, V7X_SUBLANES, 6 * D), F32),
        grid=(2, nj),
        in_specs=[pl.BlockSpec((V7X_SUBLANES, D), lambda l, j: (0, 0)),
                  pl.BlockSpec((D, tn), at0), pl.BlockSpec((1, tn), at0),
                  pl.BlockSpec((D, tn), at1), pl.BlockSpec((1, tn), at1)],
        out_specs=pl.BlockSpec((None, V7X_SUBLANES, tn), lambda l, j: (l, 0, j)),
        compiler_params=_cparams(("arbitrary", "arbitrary")),
        name="adaln",
    )(cond8, w0, b0.reshape(1, 6 * D), w1, b1.reshape(1, 6 * D))
    return [m[l, :3].reshape(3, 6, D).transpose(1, 0, 2).reshape(18, 1, D) for l in range(2)]


def _tile_of(n_load):
    return lambda s: jnp.maximum(s - n_load, 0)


def _block_of(n_load):
    return lambda s: jnp.minimum(s, n_load - 1)


class _TwoSets:
    def __init__(self, n_load, tm, tokens, conds):
        self.n_load, self.tm, self.conds = n_load, tm, conds
        self.n_a, self.n_b = tokens[0] // tm, tokens[1] // tm
        self.steps = n_load + self.n_a + self.n_b

    def tile(self, s):
        return jnp.maximum(s - self.n_load, 0)

    def in_first(self, s):
        return s - self.n_load < self.n_a

    def idx_a(self, s):
        return jnp.minimum(self.tile(s), self.n_a - 1)

    def idx_b(self, s):
        return jnp.clip(self.tile(s) - self.n_a, 0, self.n_b - 1)

    def rows(self, width):
        return (pl.BlockSpec((self.tm, width), lambda s: (self.idx_a(s), 0)),
                pl.BlockSpec((self.tm, width), lambda s: (self.idx_b(s), 0)))

    def cols(self, width, col):
        return (pl.BlockSpec((self.tm, width), lambda s: (self.idx_a(s), col)),
                pl.BlockSpec((self.tm, width), lambda s: (self.idx_b(s), col)))

    def mod_spec(self, comp):
        (row_a, seg_a), (row_b, seg_b) = self.conds
        assert seg_a % self.tm == 0 and seg_b % self.tm == 0

        def row(s):
            return jnp.where(self.in_first(s), row_a + (self.idx_a(s) * self.tm) // seg_a,
                             row_b + (self.idx_b(s) * self.tm) // seg_b)

        return pl.BlockSpec((None, 1, D), lambda s: (comp * 3 + row(s), 0, 0))


def _in0_kernel(xa_ref, xb_ref, g_ref, sh_ref, sc_ref, w_ref, oa_ref, ob_ref, w_sc, *, n_a):
    s = pl.program_id(0)
    n_load = w_sc.shape[0]

    @pl.when(s < n_load)
    def _():
        w_sc[s] = w_ref[...].astype(BF16)

    @pl.when(s >= n_load)
    def _():
        first = s - n_load < n_a
        x = jnp.where(first, xa_ref[...], xb_ref[...])
        h = _norm_mod(x, g_ref[...], sh_ref[...], sc_ref[...]).astype(BF16)
        u = jnp.concatenate([jnp.dot(h, w_sc[j], preferred_element_type=F32).astype(BF16)
                             for j in range(n_load)], axis=1)

        @pl.when(first)
        def _():
            oa_ref[...] = u

        @pl.when(jnp.logical_not(first))
        def _():
            ob_ref[...] = u


def in0_proj(xs, g, modtab, conds, w_in):
    tn = TN_IN0
    n = w_in.shape[1]
    n_load = n // tn
    ts = _TwoSets(n_load, TM, [x.shape[0] for x in xs], conds)
    blk = _block_of(n_load)
    return pl.pallas_call(
        functools.partial(_in0_kernel, n_a=ts.n_a),
        out_shape=tuple(jax.ShapeDtypeStruct((x.shape[0], n), BF16) for x in xs),
        grid=(ts.steps,),
        in_specs=[*ts.rows(D),
                  pl.BlockSpec((1, D), lambda s: (0, 0)),
                  ts.mod_spec(0), ts.mod_spec(1),
                  pl.BlockSpec((D, tn), lambda s: (0, blk(s)))],
        out_specs=ts.rows(n),
        scratch_shapes=[pltpu.VMEM((n_load, D, tn), BF16)],
        compiler_params=_cparams(("arbitrary",)),
        name="in0_proj",
    )(*xs, g.reshape(1, D), modtab, modtab, w_in)


def _shift_rows(v, d, t, seq_len=None):
    n = v.shape[0]
    seq_len = n if seq_len is None else seq_len
    if d > 0:
        return jnp.where(t < d, 0.0, pltpu.roll(v, d, 0))
    return jnp.where(t >= seq_len + d, 0.0, pltpu.roll(v, n + d, 0))


def _conv_a_kernel(b_ref, c_ref, x_ref, w_ref, o_ref, *, seq_len):
    v = c_ref[...].astype(F32) * x_ref[...].astype(F32)
    t = lax.broadcasted_iota(jnp.int32, v.shape, 0) & (seq_len - 1)
    w = w_ref[...]
    y = w[0:1] * _shift_rows(v, 1, t, seq_len) + w[1:2] * v + w[2:3] * _shift_rows(v, -1, t, seq_len)
    o_ref[...] = (b_ref[...].astype(F32) * y).astype(o_ref.dtype)


def conv_a(u, seq_len, conv_w):
    tokens = u.shape[0]
    rows = max(seq_len, CONV_A_ROWS)
    assert seq_len & (seq_len - 1) == 0 and rows % seq_len == 0
    return pl.pallas_call(
        functools.partial(_conv_a_kernel, seq_len=seq_len),
        out_shape=jax.ShapeDtypeStruct((tokens, CONV_W), BF16),
        grid=(tokens // rows,),
        in_specs=[pl.BlockSpec((rows, CONV_W), lambda s: (s, 0)),
                  pl.BlockSpec((rows, CONV_W), lambda s: (s, 1)),
                  pl.BlockSpec((rows, CONV_W), lambda s: (s, 2)),
                  pl.BlockSpec((3, CONV_W), lambda s: (0, 0))],
        out_specs=pl.BlockSpec((rows, CONV_W), lambda s: (s, 0)),
        compiler_params=_cparams(("parallel",)),
        name="conv_a",
    )(u, u, u, conv_w)


def _group_scan(a_sc, b_sc, k, reverse):
    planes = a_sc.shape[1] // V7X_SUBLANES
    order = range(V7X_SUBLANES - 1, -1, -1) if reverse else range(V7X_SUBLANES)
    a_acc = b_acc = None
    for r in order:
        plane = (k, pl.ds(r, planes, stride=V7X_SUBLANES), slice(None))
        a_r, b_r = a_sc[plane], b_sc[plane]
        if a_acc is None:
            a_acc, b_acc = a_r, b_r
        else:
            b_acc = a_r * b_acc + b_r
            a_acc = a_r * a_acc
            a_sc[plane] = a_acc
            b_sc[plane] = b_acc


def _rglru_kernel(gate_ref, xb_ref, cw_ref, cb_ref, wcat_ref, ba_ref, bi_ref, lam_ref, h0_ref,
                  y_ref, st_ref, af_sc, bf_sc, ab_sc, bb_sc, hf_sc, hb_sc, *, seq_len):
    n, cb = xb_ref.shape
    n_seq = n // seq_len
    n_slab = cb // LRU_BW
    xb = xb_ref[...].astype(F32)
    t = lax.broadcasted_iota(jnp.int32, xb.shape, 0) & (seq_len - 1)
    cw = cw_ref[...]
    sh = lambda d: _shift_rows(xb, d, t, seq_len)
    xc = cb_ref[...] + cw[0:1] * sh(2) + cw[1:2] * sh(1) + cw[2:3] * xb + cw[3:4] * sh(-1)
    xcb = xc.astype(BF16)

    for k in range(n_slab):
        cols = slice(k * LRU_BW, (k + 1) * LRU_BW)
        gk = jnp.dot(xcb[:, cols], wcat_ref[k].astype(BF16), preferred_element_type=F32)
        for d, (a_sc, b_sc) in enumerate(((af_sc, bf_sc), (ab_sc, bb_sc))):
            ga = gk[:, (2 * d) * LRU_BW:(2 * d + 1) * LRU_BW]
            gi = gk[:, (2 * d + 1) * LRU_BW:(2 * d + 2) * LRU_BW]
            r = _sigmoid(ga + ba_ref[d:d + 1, cols])
            i = _sigmoid(gi + bi_ref[d:d + 1, cols])
            log_a = (-LRU_C * jax.nn.softplus(-lam_ref[d:d + 1, cols])) * r
            a = jnp.exp(log_a)
            m = 1.0 - a * a
            mult = m * lax.rsqrt(jnp.maximum(m, 1e-30))
            a_sc[k] = a
            b_sc[k] = mult * (i * xc[:, cols])
            _group_scan(a_sc, b_sc, k, reverse=(d == 1))

    ng = seq_len // V7X_SUBLANES
    bcast = lambda row: jnp.broadcast_to(row, (V7X_SUBLANES, LRU_BW))
    chains = [(q, k) for q in range(n_seq) for k in range(n_slab)]
    init = tuple((bcast(h0_ref[q, 0:1, k * LRU_BW:(k + 1) * LRU_BW]),
                  bcast(h0_ref[q, 1:2, k * LRU_BW:(k + 1) * LRU_BW])) for q, k in chains)

    def step(j, carry):
        out = []
        for (q, k), (hf_in, hb_in) in zip(chains, carry):
            rf = pl.ds(pl.multiple_of(q * seq_len + j * V7X_SUBLANES, V7X_SUBLANES), V7X_SUBLANES)
            rb = pl.ds(pl.multiple_of(q * seq_len + (ng - 1 - j) * V7X_SUBLANES, V7X_SUBLANES), V7X_SUBLANES)
            hf = af_sc[k, rf, :] * hf_in + bf_sc[k, rf, :]
            hb = ab_sc[k, rb, :] * hb_in + bb_sc[k, rb, :]
            hf_sc[k, rf, :] = hf
            hb_sc[k, rb, :] = hb
            out.append((bcast(hf[V7X_SUBLANES - 1:V7X_SUBLANES]), bcast(hb[0:1])))
        return tuple(out)

    final = lax.fori_loop(0, ng, step, init)
    for (q, k), (hf_last, hb_first) in zip(chains, final):
        st_ref[q, 0:1, k * LRU_BW:(k + 1) * LRU_BW] = hf_last[0:1]
        st_ref[q, 1:2, k * LRU_BW:(k + 1) * LRU_BW] = hb_first[0:1]

    gt = gate_ref[...].astype(F32)
    gelu = 0.5 * gt * (1.0 + jnp.tanh(math.sqrt(2.0 / math.pi) * (gt + 0.044715 * (gt * gt * gt))))
    h = jnp.concatenate([hf_sc[k] + hb_sc[k] for k in range(n_slab)], axis=1)
    y_ref[...] = (h * gelu).astype(y_ref.dtype)


def rglru(u, seq_len, conv_w, conv_b, wcat, ba, bi, lam, h0):
    tokens = u.shape[0]
    nseq = tokens // seq_len
    cb = LRU_CB
    rows = max(seq_len, LRU_ROWS)
    assert seq_len & (seq_len - 1) == 0 and rows % seq_len == 0
    per_blk = rows // seq_len
    gate_blk0 = 3 * CONV_W // cb
    xb_blk0 = (3 * CONV_W + LRU_W) // cb
    seq_scr = lambda: pltpu.VMEM((cb // LRU_BW, rows, LRU_BW), F32)
    return pl.pallas_call(
        functools.partial(_rglru_kernel, seq_len=seq_len),
        out_shape=(jax.ShapeDtypeStruct((tokens, LRU_W), BF16), jax.ShapeDtypeStruct((nseq, 2, LRU_W), F32)),
        grid=(tokens // rows, LRU_W // cb),
        in_specs=[pl.BlockSpec((rows, cb), lambda s, c: (s, gate_blk0 + c)),
                  pl.BlockSpec((rows, cb), lambda s, c: (s, xb_blk0 + c)),
                  pl.BlockSpec((4, cb), lambda s, c: (0, c)),
                  pl.BlockSpec((1, cb), lambda s, c: (0, c)),
                  pl.BlockSpec((cb // LRU_BW, LRU_BW, 4 * LRU_BW), lambda s, c: (c, 0, 0)),
                  pl.BlockSpec((2, cb), lambda s, c: (0, c)),
                  pl.BlockSpec((2, cb), lambda s, c: (0, c)),
                  pl.BlockSpec((2, cb), lambda s, c: (0, c)),
                  pl.BlockSpec((per_blk, 2, cb), lambda s, c: (s, 0, c))],
        out_specs=(pl.BlockSpec((rows, cb), lambda s, c: (s, c)),
                   pl.BlockSpec((per_blk, 2, cb), lambda s, c: (s, 0, c))),
        scratch_shapes=[seq_scr() for _ in range(6)],
        compiler_params=_cparams(("parallel", "parallel")),
        name="rglru",
    )(u, u, conv_w, conv_b.reshape(1, LRU_W), wcat, ba, bi, lam, h0)


def _mix_ffn_kernel(p0a_ref, p0b_ref, p1a_ref, p1b_ref, p2a_ref, p2b_ref, wo_ref, xa_ref, xb_ref,
                    g1_ref, g_ref, sh_ref, sc_ref, g2_ref, wg_ref, wu_ref, wd_ref, oa_ref, ob_ref,
                    wo_sc, wg_sc, wu_sc, wd_sc, *, n_a):
    s = pl.program_id(0)
    n_load = wg_sc.shape[0]
    n_out = wo_sc.shape[0]

    @pl.when(s < n_out)
    def _():
        wo_sc[s] = wo_ref[...].astype(BF16)

    @pl.when(s < n_load)
    def _():
        wg_sc[s] = wg_ref[...].astype(BF16)
        wu_sc[s] = wu_ref[...].astype(BF16)
        wd_sc[s] = wd_ref[...].astype(BF16)

    @pl.when(s >= n_load)
    def _():
        first = s - n_load < n_a
        pick = lambda a_ref, b_ref: jnp.where(first, a_ref[...], b_ref[...])
        m = jnp.dot(pick(p0a_ref, p0b_ref), wo_sc[0], preferred_element_type=F32)
        m += jnp.dot(pick(p1a_ref, p1b_ref), wo_sc[1], preferred_element_type=F32)
        m += jnp.dot(pick(p2a_ref, p2b_ref), wo_sc[2], preferred_element_type=F32)
        x = pick(xa_ref, xb_ref) + g1_ref[...] * m
        h = _norm_mod(x, g_ref[...], sh_ref[...], sc_ref[...]).astype(BF16)
        y = None
        for f in range(n_load):
            hg = jnp.dot(h, wg_sc[f], preferred_element_type=F32)
            hu = jnp.dot(h, wu_sc[f], preferred_element_type=F32)
            act = (_silu(hg) * hu).astype(BF16)
            yf = jnp.dot(act, wd_sc[f], preferred_element_type=F32)
            y = yf if y is None else y + yf
        out = x + g2_ref[...] * y

        @pl.when(first)
        def _():
            oa_ref[...] = out

        @pl.when(jnp.logical_not(first))
        def _():
            ob_ref[...] = out


def mix_ffn(parts, w_out, xs, g, modtab, conds, w_gate, w_up, w_down):
    tf = TF_FFN
    kb = 512
    n_load = D_FF // tf
    n_out = len(parts[0])
    assert n_out <= n_load
    ts = _TwoSets(n_load, TM, [x.shape[0] for x in xs], conds)
    blk = _block_of(n_load)
    oblk = _block_of(n_out)
    lhs_specs, lhs_args = [], []
    for (arr_a, col_a), (arr_b, col_b) in zip(*parts):
        assert col_a == col_b
        lhs_specs += ts.cols(kb, col_a)
        lhs_args += [arr_a, arr_b]
    return pl.pallas_call(
        functools.partial(_mix_ffn_kernel, n_a=ts.n_a),
        out_shape=tuple(jax.ShapeDtypeStruct(x.shape, F32) for x in xs),
        grid=(ts.steps,),
        in_specs=lhs_specs + [
            pl.BlockSpec((kb, D), lambda s: (oblk(s), 0)),
            *ts.rows(D),
            ts.mod_spec(2),
            pl.BlockSpec((1, D), lambda s: (0, 0)),
            ts.mod_spec(3), ts.mod_spec(4), ts.mod_spec(5),
            pl.BlockSpec((D, tf), lambda s: (0, blk(s))),
            pl.BlockSpec((D, tf), lambda s: (0, blk(s))),
            pl.BlockSpec((tf, D), lambda s: (blk(s), 0))],
        out_specs=ts.rows(D),
        scratch_shapes=[pltpu.VMEM((n_out, kb, D), BF16),
                        pltpu.VMEM((n_load, D, tf), BF16), pltpu.VMEM((n_load, D, tf), BF16),
                        pltpu.VMEM((n_load, tf, D), BF16)],
        compiler_params=_cparams(("arbitrary",), V7X_VMEM_LIMIT_LARGE_BYTES),
        name="mix_ffn",
    )(*lhs_args, w_out, *xs, modtab, g.reshape(1, D), modtab, modtab, modtab, w_gate, w_up, w_down)


def _rms(x, g):
    return x * lax.rsqrt(jnp.mean(x * x, axis=-1, keepdims=True) + EPS) * g


def _in1_kernel(x_ref, g_ref, sh_ref, sc_ref, w_ref, qn_ref, kvn_ref, wq_ref, wkv_ref,
                qnope_ref, qpe_ref, ckv_ref, kr_ref, kv_ref, uh_ref, w_sc, wq_sc, wkv_sc):
    @pl.when(pl.program_id(0) == 0)
    def _():
        w_sc[...] = w_ref[...].astype(BF16)
        for h in range(MLA_HEADS):
            c0 = h * QK_DIM
            wq_sc[:, h * NOPE:(h + 1) * NOPE] = wq_ref[:, c0:c0 + NOPE].astype(BF16)
            r0 = MLA_HEADS * NOPE + h * ROPE
            wq_sc[:, r0:r0 + ROPE] = wq_ref[:, c0 + NOPE:c0 + QK_DIM].astype(BF16)
        wkv_sc[...] = wkv_ref[...].astype(BF16)

    h = _norm_mod(x_ref[...], g_ref[...], sh_ref[...], sc_ref[...]).astype(BF16)
    u = lax.dot_general(h, w_sc[...], (((1,), (1,)), ((), ())), preferred_element_type=F32)
    o1, o2, o3 = Q_RANK, Q_RANK + KV_RANK, Q_RANK + KV_RANK + ROPE
    cq = _rms(u[:, :o1], qn_ref[...])
    q = jnp.dot(cq.astype(BF16), wq_sc[...], preferred_element_type=F32) * _SCALE
    qnope_ref[...] = q[:, :MLA_HEADS * NOPE].astype(qnope_ref.dtype)
    qpe_ref[...] = q[:, MLA_HEADS * NOPE:]
    ckv = _rms(u[:, o1:o2], kvn_ref[...])
    ckv_ref[...] = ckv
    kv_ref[...] = jnp.dot(ckv.astype(BF16), wkv_sc[...], preferred_element_type=F32).astype(kv_ref.dtype)
    kr_ref[...] = u[:, o2:o3]
    uh_ref[...] = u[:, o3:]


def in1_proj(x, g, modtab, cond, w_in, q_norm, kv_norm, w_q_up, w_kv_up):
    tokens = x.shape[0]
    tm = TM_IN1
    nkv = MLA_HEADS * (NOPE + VDIM)
    const = lambda i: (0, 0)
    zero = lambda i: 0
    once = pl.Buffered(1)
    outs = (jax.ShapeDtypeStruct((tokens, MLA_HEADS * NOPE), BF16),
            jax.ShapeDtypeStruct((tokens, MLA_HEADS * ROPE), F32),
            jax.ShapeDtypeStruct((tokens, KV_RANK), F32),
            jax.ShapeDtypeStruct((tokens, ROPE), F32),
            jax.ShapeDtypeStruct((tokens, nkv), BF16),
            jax.ShapeDtypeStruct((tokens, 3 * HY_W), F32))
    row = lambda w: pl.BlockSpec((tm, w), lambda i: (i, 0))
    return pl.pallas_call(
        _in1_kernel,
        out_shape=outs,
        grid=(tokens // tm,),
        in_specs=[row(D),
                  pl.BlockSpec((1, D), const),
                  _mod_spec(0, cond, tm, D, zero),
                  _mod_spec(1, cond, tm, D, zero),
                  pl.BlockSpec((IN1, D), const, pipeline_mode=once),
                  pl.BlockSpec((1, Q_RANK), const),
                  pl.BlockSpec((1, KV_RANK), const),
                  pl.BlockSpec((Q_RANK, MLA_HEADS * QK_DIM), const, pipeline_mode=once),
                  pl.BlockSpec((KV_RANK, nkv), const, pipeline_mode=once)],
        out_specs=tuple(row(o.shape[1]) for o in outs),
        scratch_shapes=[pltpu.VMEM((IN1, D), BF16), pltpu.VMEM((Q_RANK, MLA_HEADS * QK_DIM), BF16),
                        pltpu.VMEM((KV_RANK, nkv), BF16)],
        compiler_params=_cparams(("arbitrary",)),
        name="in1_proj",
    )(x, g.reshape(1, D), modtab, modtab, w_in.T, q_norm.reshape(1, Q_RANK), kv_norm.reshape(1, KV_RANK),
      w_q_up, w_kv_up)


def _mm_kernel(a_ref, w_ref, o_ref):
    o_ref[...] = jnp.dot(a_ref[...].astype(BF16), w_ref[...].astype(BF16),
                         preferred_element_type=F32).astype(o_ref.dtype)


def kv_up(ckv, w_kv_up):
    rows = ckv.shape[0]
    n = w_kv_up.shape[1]
    return pl.pallas_call(
        _mm_kernel,
        out_shape=jax.ShapeDtypeStruct((rows, n), BF16),
        grid=(rows // TM,),
        in_specs=[pl.BlockSpec((TM, KV_RANK), lambda i: (i, 0)), pl.BlockSpec((KV_RANK, n), lambda i: (0, 0))],
        out_specs=pl.BlockSpec((TM, n), lambda i: (i, 0)),
        compiler_params=_cparams(("parallel",)),
        name="kv_up",
    )(ckv, w_kv_up)


_NT = (((1,), (1,)), ((), ()))
_SCALE = 1.0 / math.sqrt(QK_DIM)


def _fill_rope_tables(cos_ref, sin_ref):
    n, width = cos_ref.shape
    n_grid_rows = n // GRID_W
    n_freq = ROPE // 4

    def trig(count):
        lane = lax.broadcasted_iota(jnp.int32, (count, width), 1)
        j = lane & (ROPE // 2 - 1)
        inv = jnp.exp((j & (n_freq - 1)).astype(F32) * (-math.log(ROPE_THETA) / n_freq))
        ang = lax.broadcasted_iota(jnp.int32, (count, width), 0).astype(F32) * inv
        return jnp.cos(ang), jnp.sin(ang), j < n_freq

    cos_c, sin_c, by_row = trig(GRID_W)
    cos_r, sin_r, _ = trig(n_grid_rows)
    for r in range(n_grid_rows):
        rows = slice(r * GRID_W, (r + 1) * GRID_W)
        cos_ref[rows, :] = jnp.where(by_row, jnp.broadcast_to(cos_r[r:r + 1], cos_c.shape), cos_c)
        sin_ref[rows, :] = jnp.where(by_row, jnp.broadcast_to(sin_r[r:r + 1], sin_c.shape), sin_c)


def _rope(x, cos, sin):
    width = x.shape[1]
    lane = lax.broadcasted_iota(jnp.int32, x.shape, 1)
    first_half = (lane & (ROPE - 1)) < ROPE // 2
    xr = jnp.where(first_half, -pltpu.roll(x, width - ROPE // 2, 1), pltpu.roll(x, ROPE // 2, 1))
    return x * cos + xr * sin


def _ones_column(n):
    lane = lax.broadcasted_iota(jnp.int32, (n, VDIM), 1)
    return jnp.where(lane == 0, 1.0, 0.0).astype(BF16)


def _head_attention(qcat, kcat, vaug):
    s = lax.dot_general(qcat, kcat, _NT, preferred_element_type=F32)
    p = jnp.exp(s - jnp.max(s, axis=-1, keepdims=True)).astype(BF16)
    oa = jnp.dot(p, vaug, preferred_element_type=F32)
    return oa[:, :VDIM] / oa[:, VDIM:VDIM + 1]


def _attn_ctx_kernel(qn_ref, qpe_ref, kv_ref, kr_ref, o_ref, *, seq_len):
    n = qn_ref.shape[0]
    n_seq = n // seq_len
    ones = _ones_column(n)
    kpe = kr_ref[...].astype(BF16)
    per_seq = lambda a: a.reshape(n_seq, seq_len, a.shape[-1])
    for h in range(MLA_HEADS):
        c0 = h * (NOPE + VDIM)
        qcat = per_seq(jnp.concatenate([qn_ref[:, h * NOPE:(h + 1) * NOPE],
                                        qpe_ref[:, h * ROPE:(h + 1) * ROPE].astype(BF16)], axis=1))
        kcat = per_seq(jnp.concatenate([kv_ref[:, c0:c0 + NOPE], kpe], axis=1))
        vaug = per_seq(jnp.concatenate([kv_ref[:, c0 + NOPE:c0 + NOPE + VDIM], ones], axis=1))
        s = jnp.einsum("bqd,bkd->bqk", qcat, kcat, preferred_element_type=F32)
        p = jnp.exp(s - jnp.max(s, axis=-1, keepdims=True)).astype(BF16)
        oa = jnp.einsum("bqk,bkd->bqd", p, vaug, preferred_element_type=F32)
        o = oa[:, :, :VDIM] / oa[:, :, VDIM:VDIM + 1]
        o_ref[:, h * VDIM:(h + 1) * VDIM] = o.reshape(n, VDIM).astype(o_ref.dtype)


def attn_ctx(qnope, qpe, kv, kr, seq_len):
    tokens = qnope.shape[0]
    rows = ATTN_CTX_SEQS * seq_len
    blk = lambda w: pl.BlockSpec((rows, w), lambda s: (s, 0))
    return pl.pallas_call(
        functools.partial(_attn_ctx_kernel, seq_len=seq_len),
        out_shape=jax.ShapeDtypeStruct((tokens, MLA_HEADS * VDIM), BF16),
        grid=(tokens // rows,),
        in_specs=[blk(MLA_HEADS * NOPE), blk(MLA_HEADS * ROPE), blk(MLA_HEADS * (NOPE + VDIM)), blk(ROPE)],
        out_specs=blk(MLA_HEADS * VDIM),
        compiler_params=_cparams(("parallel",)),
        name="attn_ctx",
    )(qnope, qpe, kv, kr)


def _attn_lat_kernel(qn_ref, qpe_ref, kvc_ref, krc_ref, kvl_ref, krl_ref, o_ref, kcat_sc, vaug_sc, cos_sc, sin_sc):
    tq = qn_ref.shape[0]
    n_ctx = krc_ref.shape[0]
    n_lat = krl_ref.shape[0]

    @pl.when(pl.program_id(1) == 0)
    def _():
        _fill_rope_tables(cos_sc, sin_sc)
        kr2 = jnp.concatenate([krl_ref[...], krl_ref[...]], axis=1)
        kpe_lat = _rope(kr2, cos_sc[...], sin_sc[...])[:, :ROPE].astype(BF16)
        kpe_ctx = krc_ref[...].astype(BF16)
        ones_c, ones_l = _ones_column(n_ctx), _ones_column(n_lat)
        for h in range(MLA_HEADS):
            c0 = h * (NOPE + VDIM)
            for r0, nr, kv_ref, kpe, ones in ((0, n_ctx, kvc_ref, kpe_ctx, ones_c), (n_ctx, n_lat, kvl_ref, kpe_lat, ones_l)):
                kcat_sc[h, r0:r0 + nr, 0:NOPE] = kv_ref[:, c0:c0 + NOPE]
                kcat_sc[h, r0:r0 + nr, NOPE:QK_DIM] = kpe
                vaug_sc[h, r0:r0 + nr, 0:VDIM] = kv_ref[:, c0 + NOPE:c0 + NOPE + VDIM]
                vaug_sc[h, r0:r0 + nr, VDIM:2 * VDIM] = ones

    q0 = pl.multiple_of(pl.program_id(1) * tq, tq)
    rep = lambda a: jnp.concatenate([a] * (MLA_HEADS // 2), axis=1)
    qp_all = _rope(qpe_ref[...], rep(cos_sc[pl.ds(q0, tq), :]), rep(sin_sc[pl.ds(q0, tq), :])).astype(BF16)
    for h in range(MLA_HEADS):
        qcat = jnp.concatenate([qn_ref[:, h * NOPE:(h + 1) * NOPE], qp_all[:, h * ROPE:(h + 1) * ROPE]], axis=1)
        o_ref[:, h * VDIM:(h + 1) * VDIM] = _head_attention(qcat, kcat_sc[h], vaug_sc[h]).astype(o_ref.dtype)


def attn_lat(qnope, qpe, kv_ctx, kr_ctx, kv_lat, kr_lat, seq_len, ctx_len):
    tokens = qnope.shape[0]
    nq = seq_len // TQ
    qblk = lambda w: pl.BlockSpec((TQ, w), lambda b, i: (b * nq + i, 0))
    seq = lambda n, w: pl.BlockSpec((n, w), lambda b, i: (b, 0))
    nkv = MLA_HEADS * (NOPE + VDIM)
    n_keys = ctx_len + seq_len
    return pl.pallas_call(
        _attn_lat_kernel,
        out_shape=jax.ShapeDtypeStruct((tokens, MLA_HEADS * VDIM), BF16),
        grid=(tokens // seq_len, nq),
        in_specs=[qblk(MLA_HEADS * NOPE), qblk(MLA_HEADS * ROPE), seq(ctx_len, nkv), seq(ctx_len, ROPE),
                  seq(seq_len, nkv), seq(seq_len, ROPE)],
        out_specs=qblk(MLA_HEADS * VDIM),
        scratch_shapes=[pltpu.VMEM((MLA_HEADS, n_keys, QK_DIM), BF16),
                        pltpu.VMEM((MLA_HEADS, n_keys, 2 * VDIM), BF16),
                        pltpu.VMEM((seq_len, 2 * ROPE), F32), pltpu.VMEM((seq_len, 2 * ROPE), F32)],
        compiler_params=_cparams(("parallel", "arbitrary")),
        name="attn_lat",
    )(qnope, qpe, kv_ctx, kr_ctx, kv_lat, kr_lat)


def _dft_kernel(o_ref):
    tr, n = o_ref.shape[1], o_ref.shape[2]
    nb = n // V7X_LANES
    f = pl.program_id(0) * tr + lax.broadcasted_iota(jnp.int32, (tr, V7X_LANES), 0)
    j = lax.broadcasted_iota(jnp.int32, (tr, V7X_LANES), 1)

    def cos_sin(m):
        ang = (m & (2 * n - 1)).astype(F32) * (math.pi / n)
        return jnp.cos(ang), jnp.sin(ang)

    cj, sj = cos_sin(f * j)
    cb, sb = cos_sin(f * (j * V7X_LANES))
    for b in range(nb):
        cbb, sbb = cb[:, b:b + 1], sb[:, b:b + 1]
        cols = slice(b * V7X_LANES, (b + 1) * V7X_LANES)
        o_ref[0, :, cols] = (cbb * cj - sbb * sj).astype(o_ref.dtype)
        o_ref[1, :, cols] = (sbb * cj + cbb * sj).astype(o_ref.dtype)


def dft_tables(n):
    tr = 128
    return pl.pallas_call(
        _dft_kernel,
        out_shape=jax.ShapeDtypeStruct((2, n, n), BF16),
        grid=(n // tr,),
        out_specs=pl.BlockSpec((2, tr, n), lambda i: (0, i, 0)),
        compiler_params=_cparams(("parallel",)),
        name="dft_tables",
    )()


def _split_dot(table, x):
    hi = x.astype(BF16)
    lo = (x - hi.astype(F32)).astype(BF16)
    return (jnp.dot(table, hi, preferred_element_type=F32) + jnp.dot(table, lo, preferred_element_type=F32))


def _hy_filter_kernel(cs_ref, w1_ref, b1_ref, w2_ref, b2_ref, w3_ref, kr_ref, ks_ref, kny_ref):
    n = cs_ref.shape[1]
    row = lax.broadcasted_iota(jnp.int32, (n, V7X_LANES), 0).astype(F32)
    lane = lax.broadcasted_iota(jnp.int32, (n, V7X_LANES), 1)
    t = row * (1.0 / (n - 1))
    w = (2.0 * math.pi) * row / n
    band = jnp.where(lane <= HY_BANDS, lane - 1, lane - 1 - HY_BANDS).astype(F32)
    freq = 1e-4 + band * ((HY_BANDS - 1 - 1e-4) / (HY_BANDS - 1))
    arg = jnp.where(lane <= HY_BANDS, freq * w + 0.5 * math.pi, -(freq * w))
    z = jnp.where(lane == 0, t, jnp.where(lane <= 2 * HY_BANDS, jnp.sin(arg), 0.0))
    hid = jnp.sin(_dot3(z, w1_ref[...]) + b1_ref[...])
    hid = jnp.sin(_dot3(hid, w2_ref[...]) + b2_ref[...])
    hf = _dot3(hid, w3_ref[...])

    rowc = lax.broadcasted_iota(jnp.int32, (n, HY_W), 0)
    chan = lax.broadcasted_iota(jnp.int32, (n, HY_W), 1).astype(F32)
    max_decay = math.log(HY_TARGET) / HY_FAST_DECAY
    min_decay = math.log(HY_TARGET) / HY_SLOW_DECAY
    deltas = min_decay + chan * ((max_decay - min_decay) / (HY_W - 1))
    decay = jnp.exp(-(rowc.astype(F32) * (1.0 / (n - 1))) * jnp.abs(deltas))
    h_fwd = hf[:, :HY_W] * decay
    h_bwd = jnp.where(rowc == 0, 0.0, hf[:, HY_W:] * decay)
    norm = jnp.sum(jnp.abs(h_fwd) + jnp.abs(h_bwd), axis=0, keepdims=True)
    even = (h_fwd + h_bwd) / norm
    odd = (h_fwd - h_bwd) / norm
    cf = jnp.where(rowc == 0, 1.0, 2.0) * (1.0 / (2 * n))
    kr_ref[...] = cf * _split_dot(cs_ref[0], even)
    ks_ref[...] = cf * _split_dot(cs_ref[1], odd)
    sgn = jnp.where((rowc & 1) == 1, -1.0, 1.0)
    kny_ref[...] = jnp.sum(sgn * even, axis=0, keepdims=True) * (1.0 / (2 * n))


def hy_filter(cs, w1p, b1p, w2p, b2p, w3p):
    n = cs.shape[1]
    full = lambda a: pl.BlockSpec(a.shape, lambda: (0,) * a.ndim)
    args = (cs, w1p, b1p, w2p, b2p, w3p)
    return pl.pallas_call(
        _hy_filter_kernel,
        out_shape=(jax.ShapeDtypeStruct((n, HY_W), F32), jax.ShapeDtypeStruct((n, HY_W), F32),
                   jax.ShapeDtypeStruct((1, HY_W), F32)),
        in_specs=[full(a) for a in args],
        out_specs=(pl.BlockSpec((n, HY_W), lambda: (0, 0)), pl.BlockSpec((n, HY_W), lambda: (0, 0)),
                   pl.BlockSpec((1, HY_W), lambda: (0, 0))),
        compiler_params=pltpu.CompilerParams(vmem_limit_bytes=V7X_VMEM_LIMIT_BYTES),
        name="hy_filter",
    )(*args)


def _hyena_kernel(u0_ref, u1_ref, u2_ref, sw_ref, sb_ref, cs_ref, kr_ref, ks_ref, kny_ref, bias_ref, o_ref,
                  *, seq_len):
    n, cb = u0_ref.shape
    n_seq = n // seq_len
    t = lax.broadcasted_iota(jnp.int32, (n, cb), 0) & (seq_len - 1)

    def short_conv(u_ref, k):
        u = u_ref[...]
        w = sw_ref[:, k * cb:(k + 1) * cb]
        return (sb_ref[:, k * cb:(k + 1) * cb] + w[0:1] * _shift_rows(u, 1, t, seq_len) + w[1:2] * u
                + w[2:3] * _shift_rows(u, -1, t, seq_len))

    x0 = short_conv(u0_ref, 0)
    z = short_conv(u1_ref, 1) * short_conv(u2_ref, 2)
    wide = lambda a: jnp.concatenate([a[q * seq_len:(q + 1) * seq_len] for q in range(n_seq)], axis=1)
    rep = lambda a: jnp.concatenate([a] * n_seq, axis=1)
    zw = wide(z)
    zb = zw.astype(BF16)
    c, s = cs_ref[0], cs_ref[1]
    ur = jnp.dot(c, zb, preferred_element_type=F32)
    us = jnp.dot(s, zb, preferred_element_type=F32)
    sgn = jnp.where((lax.broadcasted_iota(jnp.int32, zw.shape, 0) & 1) == 1, -1.0, 1.0)
    uny = jnp.sum(sgn * zw, axis=0, keepdims=True)
    kr, ks = rep(kr_ref[...]), rep(ks_ref[...])
    yr = (ur * kr - us * ks).astype(BF16)
    ys = (ur * ks + us * kr).astype(BF16)
    yw = jnp.dot(c, yr, preferred_element_type=F32) + jnp.dot(s, ys, preferred_element_type=F32)
    yw = yw + sgn * (uny * rep(kny_ref[...]))
    y = jnp.concatenate([yw[:, q * cb:(q + 1) * cb] for q in range(n_seq)], axis=0)
    o_ref[...] = (x0 * (y + bias_ref[...] * z)).astype(o_ref.dtype)


def hyena(uh, seq_len, short_w, short_b, cs, kr, ks, kny, bias):
    tokens = uh.shape[0]
    cb = HY_CB
    nc = HY_W // cb
    rows = max(seq_len, HY_ROWS)
    assert seq_len & (seq_len - 1) == 0 and rows % seq_len == 0
    ublk = lambda k: pl.BlockSpec((rows, cb), lambda s, c: (s, k * nc + c))
    chan = lambda r: pl.BlockSpec((r, cb), lambda s, c: (0, c))
    return pl.pallas_call(
        functools.partial(_hyena_kernel, seq_len=seq_len),
        out_shape=jax.ShapeDtypeStruct((tokens, HY_W), BF16),
        grid=(tokens // rows, nc),
        in_specs=[ublk(0), ublk(1), ublk(2),
                  pl.BlockSpec((None, 3, 3 * cb), lambda s, c: (c, 0, 0)),
                  pl.BlockSpec((None, 1, 3 * cb), lambda s, c: (c, 0, 0)),
                  pl.BlockSpec((2, seq_len, seq_len), lambda s, c: (0, 0, 0)),
                  chan(seq_len), chan(seq_len), chan(1), chan(1)],
        out_specs=pl.BlockSpec((rows, cb), lambda s, c: (s, c)),
        compiler_params=_cparams(("parallel", "parallel")),
        name="hyena",
    )(uh, uh, uh, short_w, short_b, cs, kr, ks, kny, bias)


META_E1, META_E2, META_R1, META_R2, META_G1, META_G2 = range(6)


def _route_kernel(p0_ref, p1_ref, p2_ref, wo_ref, x_ref, g1_ref, g_ref, sh_ref, sc_ref, wr_ref, br_ref,
                  x1_ref, h_ref, meta_ref, meta_t_ref, cnt_ref, run_sc, wo_sc):
    tm = x_ref.shape[0]
    lane = lax.broadcasted_iota(jnp.int32, (tm, V7X_LANES), 1)

    @pl.when(pl.program_id(0) == 0)
    def _():
        run_sc[...] = jnp.zeros_like(run_sc)
        wo_sc[...] = wo_ref[...].astype(BF16)

    kb = p0_ref.shape[1]
    m = jnp.dot(p0_ref[...], wo_sc[0:kb, :], preferred_element_type=F32)
    m += jnp.dot(p1_ref[...], wo_sc[kb:2 * kb, :], preferred_element_type=F32)
    m += jnp.dot(p2_ref[...], wo_sc[2 * kb:3 * kb, :], preferred_element_type=F32)
    x1 = x_ref[...] + g1_ref[...] * m
    x1_ref[...] = x1
    h = _norm_mod(x1, g_ref[...], sh_ref[...], sc_ref[...])
    h_ref[...] = h
    logits = _dot3(h, wr_ref[...]) + br_ref[...]
    lg = jnp.where(lane < N_EXPERTS, logits, -jnp.inf)
    l1 = jnp.max(lg, axis=-1, keepdims=True)
    i1 = jnp.min(jnp.where(lg == l1, lane, V7X_LANES), axis=-1, keepdims=True)
    rest = jnp.where(lane == i1, -jnp.inf, lg)
    l2 = jnp.max(rest, axis=-1, keepdims=True)
    i2 = jnp.min(jnp.where(rest == l2, lane, V7X_LANES), axis=-1, keepdims=True)
    gap = jnp.exp(l2 - l1)
    gate1 = 1.0 / (1.0 + gap)
    gate2 = gap * gate1
    m1 = lane == i1
    m2 = lane == i2
    chosen = jnp.where(m1 | m2, 1.0, 0.0)
    r = lax.broadcasted_iota(jnp.int32, (tm, tm), 0)
    c = lax.broadcasted_iota(jnp.int32, (tm, tm), 1)
    tri = jnp.where(c < r, 1.0, 0.0).astype(BF16)
    before = jnp.dot(tri, chosen.astype(BF16), preferred_element_type=F32) + run_sc[0:1, :]
    rank1 = jnp.sum(jnp.where(m1, before, 0.0), axis=-1, keepdims=True)
    rank2 = jnp.sum(jnp.where(m2, before, 0.0), axis=-1, keepdims=True)
    vals = (i1.astype(F32), i2.astype(F32), rank1, rank2, gate1, gate2)
    meta = jnp.zeros((tm, V7X_LANES), F32)
    for k, v in enumerate(vals):
        meta = jnp.where(lane == k, v, meta)
    meta_ref[...] = meta
    meta_t_ref[...] = meta.T[:V7X_SUBLANES]
    run_sc[...] = run_sc[...] + jnp.sum(chosen, axis=0, keepdims=True)
    cnt_ref[...] = run_sc[...]


def mix_route(parts, w_out, x, g, modtab, cond, wr_pad, br_pad):
    tokens = x.shape[0]
    tm = TM_ROUTE
    kb = 512
    zero = lambda i: 0
    const = lambda i: (0, 0)
    rows = lambda w: pl.BlockSpec((tm, w), lambda i: (i, 0))
    lhs_specs = [pl.BlockSpec((tm, kb), (lambda i, cbk=cbk: (i, cbk))) for _, cbk in parts]
    return pl.pallas_call(
        _route_kernel,
        out_shape=(jax.ShapeDtypeStruct((tokens, D), F32),
                   jax.ShapeDtypeStruct((tokens, D), F32),
                   jax.ShapeDtypeStruct((tokens, V7X_LANES), F32),
                   jax.ShapeDtypeStruct((V7X_SUBLANES, tokens), F32),
                   jax.ShapeDtypeStruct((V7X_SUBLANES, V7X_LANES), F32)),
        grid=(tokens // tm,),
        in_specs=lhs_specs + [
            pl.BlockSpec((len(parts) * kb, D), const, pipeline_mode=pl.Buffered(1)),
            rows(D),
            _mod_spec(2, cond, tm, D, zero),
            pl.BlockSpec((1, D), const),
            _mod_spec(3, cond, tm, D, zero),
            _mod_spec(4, cond, tm, D, zero),
            pl.BlockSpec((D, V7X_LANES), const),
            pl.BlockSpec((1, V7X_LANES), const)],
        out_specs=(rows(D), rows(D), rows(V7X_LANES),
                   pl.BlockSpec((V7X_SUBLANES, tm), lambda i: (0, i)),
                   pl.BlockSpec((V7X_SUBLANES, V7X_LANES), const)),
        scratch_shapes=[pltpu.VMEM((V7X_SUBLANES, V7X_LANES), F32), pltpu.VMEM((len(parts) * kb, D), BF16)],
        compiler_params=_cparams(("arbitrary",)),
        name="mix_route",
    )(*[a for a, _ in parts], w_out, x, modtab, g.reshape(1, D), modtab, modtab, wr_pad, br_pad)


def _row_copy(src_ref, src_row, dst_ref, dst_row, sem):
    return pltpu.make_async_copy(src_ref.at[pl.ds(src_row, 1)], dst_ref.at[pl.ds(dst_row, 1)], sem)


_PAD_BULK = (256, 128, 64, 32, 16, 8)


def _zero_fill(hs_ref, zero_sc, sem, pads_ref, n_tail_max, wait):
    tmr = zero_sc.shape[0]

    def copy(rows, dst):
        cp = pltpu.make_async_copy(zero_sc.at[pl.ds(0, rows)], hs_ref.at[pl.ds(dst, rows)], sem)
        cp.wait() if wait else cp.start()

    for e in range(N_EXPERTS):
        start, n = pads_ref[e], pads_ref[N_EXPERTS + e]
        head = jnp.minimum((-start) & (V7X_SUBLANES - 1), n)
        for r in range(V7X_SUBLANES - 1):
            @pl.when(r < head)
            def _():
                copy(1, start + r)
        body = start + head
        rem = n - head
        for k in _PAD_BULK:
            @pl.when((rem & k) != 0)
            def _():
                copy(k, pl.multiple_of(body + (rem & ~(2 * k - 1)), V7X_SUBLANES))
    tail_start, tail_tiles = pads_ref[2 * N_EXPERTS], pads_ref[2 * N_EXPERTS + 1]
    for t in range(n_tail_max):
        @pl.when(t < tail_tiles)
        def _():
            copy(tmr, pl.multiple_of(tail_start + t * tmr, tmr))


def _dispatch_kernel(pos_ref, pads_ref, ha_ref, hb_ref, hs_ref, zero_sc, sem, zsem, *, n_tail_max):
    n_tok = pos_ref.shape[0] // 2
    tm = zero_sc.shape[0]
    zero_sc[...] = jnp.zeros_like(zero_sc)
    _zero_fill(hs_ref, zero_sc, zsem, pads_ref, n_tail_max, wait=False)

    def drain():
        for _ in range(2):
            pltpu.make_async_copy(ha_ref.at[pl.ds(0, tm)], hs_ref.at[pl.ds(0, tm)], sem).wait()

    first, pending = 0, False
    for h_ref in (ha_ref, hb_ref):
        for c0 in range(0, h_ref.shape[0], tm):
            def issue(r, carry, h_ref=h_ref, tok0=first + c0, row0=c0):
                _row_copy(h_ref, row0 + r, hs_ref, pos_ref[tok0 + r], sem).start(priority=0)
                _row_copy(h_ref, row0 + r, hs_ref, pos_ref[n_tok + tok0 + r], sem).start(priority=1)
                return carry

            lax.fori_loop(0, tm, issue, 0, unroll=8)
            if pending:
                drain()
            pending = True
        first += h_ref.shape[0]
    drain()
    _zero_fill(hs_ref, zero_sc, zsem, pads_ref, n_tail_max, wait=True)


def moe_dispatch(pos, pads, hs_rows, h_a, h_b):
    n_tok = h_a.shape[0] + h_b.shape[0]
    assert h_a.shape[0] % TM_EXPERT == 0 and h_b.shape[0] % TM_EXPERT == 0
    n_tail_max = hs_rows // TM_EXPERT - (2 * n_tok) // TM_EXPERT
    return pl.pallas_call(
        functools.partial(_dispatch_kernel, n_tail_max=n_tail_max),
        out_shape=jax.ShapeDtypeStruct((hs_rows, D), F32),
        grid_spec=pltpu.PrefetchScalarGridSpec(
            num_scalar_prefetch=2,
            grid=(1,),
            in_specs=[pl.BlockSpec(memory_space=pl.ANY), pl.BlockSpec(memory_space=pl.ANY)],
            out_specs=pl.BlockSpec(memory_space=pl.ANY),
            scratch_shapes=[pltpu.VMEM((TM_EXPERT, D), F32), pltpu.SemaphoreType.DMA(()),
                            pltpu.SemaphoreType.DMA(())]),
        compiler_params=_cparams(("arbitrary",)),
        name="moe_dispatch",
    )(pos, pads, h_a, h_b)


def _experts_kernel(te_ref, sg_ref, su_ref, sd_ref, nv_ref, hs_ref, wg_ref, wu_ref, wd_ref, y_ref,
                    wg_sc, wu_sc, wd_sc):
    del sg_ref, su_ref, sd_ref
    j = pl.program_id(0)
    e = te_ref[j]
    e_prev = te_ref[jnp.maximum(j - 1, 0)]
    n_valid = nv_ref[j]
    half = y_ref.shape[0] // 2

    @pl.when((j == 0) | (e != e_prev))
    def _():
        wg_sc[...] = wg_ref[...].astype(BF16)
        wu_sc[...] = wu_ref[...].astype(BF16)
        wd_sc[...] = wd_ref[...].astype(BF16)

    def swiglu(rows):
        h = hs_ref[rows, :].astype(BF16)
        y = None
        for c0 in range(0, D_FF_EXPERT, MOE_CHUNK):
            c1 = min(c0 + MOE_CHUNK, D_FF_EXPERT)
            hg = jnp.dot(h, wg_sc[:, c0:c1], preferred_element_type=F32)
            hu = jnp.dot(h, wu_sc[:, c0:c1], preferred_element_type=F32)
            act = (_silu(hg) * hu).astype(BF16)
            yc = jnp.dot(act, wd_sc[c0:c1, :], preferred_element_type=F32)
            y = yc if y is None else y + yc
        y_ref[rows, :] = y

    @pl.when(n_valid > half)
    def _():
        swiglu(slice(None))

    @pl.when((n_valid > 0) & (n_valid <= half))
    def _():
        swiglu(slice(0, half))
        y_ref[half:, :] = jnp.zeros((half, D), F32)

    @pl.when(n_valid == 0)
    def _():
        y_ref[...] = jnp.zeros_like(y_ref)


def moe_experts(tile_expert, stages, tile_valid, hs, e_gate, e_up, e_down):
    rows = hs.shape[0]
    tmr = TM_EXPERT
    wspec = lambda shape, k: pl.BlockSpec((None,) + shape, lambda j, *pf: (pf[1 + k][j], 0, 0))
    return pl.pallas_call(
        _experts_kernel,
        out_shape=jax.ShapeDtypeStruct((rows, D), F32),
        grid_spec=pltpu.PrefetchScalarGridSpec(
            num_scalar_prefetch=5,
            grid=(rows // tmr,),
            in_specs=[pl.BlockSpec((tmr, D), lambda j, *pf: (j, 0)),
                      wspec((D, D_FF_EXPERT), 0), wspec((D, D_FF_EXPERT), 1), wspec((D_FF_EXPERT, D), 2)],
            out_specs=pl.BlockSpec((tmr, D), lambda j, *pf: (j, 0)),
            scratch_shapes=[pltpu.VMEM((D, D_FF_EXPERT), BF16), pltpu.VMEM((D, D_FF_EXPERT), BF16),
                            pltpu.VMEM((D_FF_EXPERT, D), BF16)]),
        compiler_params=_cparams(("arbitrary",)),
        name="moe_experts",
    )(tile_expert, *stages, tile_valid, hs, e_gate, e_up, e_down)


def _combine_kernel(pos_ref, x_ref, meta_ref, gt_ref, fg_ref, y_ref, o_ref, b1_sc, b2_sc, sem):
    tm = x_ref.shape[0]
    n_tok = pos_ref.shape[0] // 2
    i = pl.program_id(0)

    def gather(tile, slot):
        base = tile * tm

        def issue(r, carry):
            _row_copy(y_ref, pos_ref[base + r], b1_sc.at[slot], r, sem.at[slot]).start(priority=0)
            _row_copy(y_ref, pos_ref[n_tok + base + r], b2_sc.at[slot], r, sem.at[slot]).start(priority=1)
            return carry

        lax.fori_loop(0, tm, issue, 0, unroll=8)

    @pl.when(i == 0)
    def _():
        gather(0, 0)

    @pl.when(i + 1 < pl.num_programs(0))
    def _():
        gather(i + 1, (i + 1) % 2)

    slot = i % 2
    pltpu.make_async_copy(y_ref.at[pl.ds(0, tm)], b1_sc.at[slot], sem.at[slot]).wait()
    pltpu.make_async_copy(y_ref.at[pl.ds(0, tm)], b2_sc.at[slot], sem.at[slot]).wait()

    meta = meta_ref[...]
    lane = lax.broadcasted_iota(jnp.int32, meta.shape, 1)
    g1 = jnp.sum(jnp.where(lane == META_G1, meta, 0.0), axis=-1, keepdims=True)
    g2 = jnp.sum(jnp.where(lane == META_G2, meta, 0.0), axis=-1, keepdims=True)
    x = x_ref[...] + gt_ref[...] * (g1 * b1_sc[slot] + g2 * b2_sc[slot])
    o_ref[...] = _rms(x, fg_ref[...])


def moe_combine(pos, x, meta, modtab, cond, final_g, y):
    tokens = x.shape[0]
    tm = TM_COMBINE
    return pl.pallas_call(
        _combine_kernel,
        out_shape=jax.ShapeDtypeStruct((tokens, D), F32),
        grid_spec=pltpu.PrefetchScalarGridSpec(
            num_scalar_prefetch=1,
            grid=(tokens // tm,),
            in_specs=[pl.BlockSpec((tm, D), lambda i, pos: (i, 0)),
                      pl.BlockSpec((tm, V7X_LANES), lambda i, pos: (i, 0)),
                      _mod_spec(5, cond, tm, D, lambda i, pos: 0),
                      pl.BlockSpec((1, D), lambda i, pos: (0, 0)),
                      pl.BlockSpec(memory_space=pl.ANY)],
            out_specs=pl.BlockSpec((tm, D), lambda i, pos: (i, 0)),
            scratch_shapes=[pltpu.VMEM((2, tm, D), F32), pltpu.VMEM((2, tm, D), F32),
                            pltpu.SemaphoreType.DMA((2,))]),
        compiler_params=_cparams(("arbitrary",)),
        name="moe_combine",
    )(pos, x, meta, modtab, final_g.reshape(1, D), y)


def moe_plan(metas, counts):
    tmr = TM_EXPERT
    cnts = [c[0, :N_EXPERTS].astype(jnp.int32) for c in counts]
    total = functools.reduce(jnp.add, cnts)
    padded = ((total + tmr - 1) // tmr) * tmr
    ends = jnp.cumsum(padded)
    starts = ends - padded
    n_rows = sum(m.shape[1] for m in metas) * 2 + N_EXPERTS * tmr
    n_tiles = n_rows // tmr
    tile_start = jnp.arange(n_tiles, dtype=jnp.int32) * tmr
    tile_expert = jnp.minimum(jnp.sum(tile_start[:, None] >= ends[None, :], axis=1), N_EXPERTS - 1).astype(jnp.int32)
    group_of_tile = jnp.sum(tile_start[:, None] >= ends[None, :], axis=1)
    real_end = jnp.sum(jnp.where(group_of_tile[:, None] == jnp.arange(N_EXPERTS)[None, :],
                                 (starts + total)[None, :], 0), axis=1)
    tile_valid = jnp.clip(real_end - tile_start, 0, tmr).astype(jnp.int32)
    eid = jnp.arange(N_EXPERTS, dtype=jnp.int32)
    later = jnp.where((eid[None, :] > eid[:, None]) & (padded[None, :] > 0), eid[None, :], N_EXPERTS)
    nxt = jnp.min(later, axis=1)
    next_used = jnp.where(nxt == N_EXPERTS, eid, nxt)
    pick = lambda table: jnp.sum(jnp.where(tile_expert[:, None] == eid[None, :], table[None, :], 0), axis=1)
    k_in_group = (tile_start - pick(starts)) // tmr
    tile_next = pick(next_used)
    stages = [jnp.where(k_in_group < k, tile_expert, tile_next).astype(jnp.int32) for k in (1, 2, 3)]
    pos, p1s, p2s = [], [], []
    base = jnp.zeros((N_EXPERTS,), jnp.int32)
    for m, c in zip(metas, cnts):
        first = starts + base
        sel = lambda field: m[field].astype(jnp.int32)
        lookup = lambda e: jnp.sum(jnp.where(e[:, None] == jnp.arange(N_EXPERTS)[None, :], first[None, :], 0), axis=1)
        p1 = lookup(sel(META_E1)) + sel(META_R1)
        p2 = lookup(sel(META_E2)) + sel(META_R2)
        pos.append(jnp.concatenate([p1, p2]).astype(jnp.int32))
        p1s.append(p1)
        p2s.append(p2)
        base = base + c
    pos_all = jnp.concatenate(p1s + p2s).astype(jnp.int32)
    pads = jnp.concatenate([starts + total, padded - total,
                            jnp.stack([ends[-1], n_tiles - ends[-1] // tmr])]).astype(jnp.int32)
    return pos, pos_all, pads, tile_expert, stages, tile_valid, n_rows


def _pad_to(a, shape):
    return jnp.pad(a, [(0, t - s) for s, t in zip(a.shape, shape)])


def _regroup_chunks(a, cb):
    r = a.shape[0]
    return a.reshape(r, 3, HY_W // cb, cb).transpose(2, 0, 1, 3).reshape(HY_W // cb, r, 3 * cb)


def kernel(x_prompt, x_sample, state_l0_lru, cache_l1_ckv, cache_l1_krope, c, c_ctx, l0_norm1, l0_norm2, l0_w_mod, l0_b_mod, l0_w_in, l0_conv_a, l0_lru_conv_w, l0_lru_conv_b, l0_lru_wa, l0_lru_ba, l0_lru_wi, l0_lru_bi, l0_lru_lambda, l0_w_out, l0_ffn_gate, l0_ffn_up, l0_ffn_down, l1_norm1, l1_norm2, l1_w_mod, l1_b_mod, l1_w_in, l1_q_norm, l1_kv_norm, l1_w_q_up, l1_w_kv_up, l1_hy_short_w, l1_hy_short_b, l1_hy_f_w1, l1_hy_f_b1, l1_hy_f_w2, l1_hy_f_b2, l1_hy_f_w3, l1_hy_bias, l1_w_out, l1_router_w, l1_router_b, l1_exp_gate, l1_exp_up, l1_exp_down, final_norm):
    batch, seq, _ = x_prompt.shape
    dec_batch, dec_seq, _ = x_sample.shape
    past_len = cache_l1_ckv.shape[1]

    cond8 = jnp.concatenate([c_ctx[None, :], c, jnp.zeros((V7X_SUBLANES - 1 - dec_batch, D), F32)], axis=0)
    wcat = jnp.concatenate([l0_lru_wa[0], l0_lru_wi[0], l0_lru_wa[1], l0_lru_wi[1]], axis=-1)
    hid = V7X_LANES
    w1p = _pad_to(l1_hy_f_w1, (hid, hid))
    b1p = _pad_to(l1_hy_f_b1.reshape(1, -1), (1, hid))
    w2p = _pad_to(l1_hy_f_w2, (hid, hid))
    b2p = _pad_to(l1_hy_f_b2.reshape(1, -1), (1, hid))
    w3p = _pad_to(l1_hy_f_w3, (hid, 2 * HY_W))
    short_w = _regroup_chunks(l1_hy_short_w, HY_CB)
    short_b = _regroup_chunks(l1_hy_short_b.reshape(1, -1), HY_CB)
    hy_bias = l1_hy_bias.reshape(1, HY_W)
    wr_pad = _pad_to(l1_router_w, (D, V7X_LANES))
    br_pad = _pad_to(l1_router_b.reshape(1, -1), (1, V7X_LANES))

    mod0, mod1 = adaln_tables(cond8, ((l0_w_mod, l0_b_mod), (l1_w_mod, l1_b_mod)))

    kv_ctx = kv_up(cache_l1_ckv.reshape(dec_batch * past_len, KV_RANK), l1_w_kv_up)
    kr_ctx = cache_l1_krope.reshape(dec_batch * past_len, ROPE)

    conds = ((0, batch * seq), (1, dec_seq))
    seq_lens = (seq, dec_seq)
    xs = (x_prompt.reshape(batch * seq, D), x_sample.reshape(dec_batch * dec_seq, D))
    h0s = (jnp.zeros((batch, 2, LRU_W), F32), state_l0_lru)

    us = in0_proj(xs, l0_norm1, mod0, conds, l0_w_in)
    parts, lru_states = [], []
    for u, seq_len, h0 in zip(us, seq_lens, h0s):
        ya = conv_a(u, seq_len, l0_conv_a)
        yb, lru_state = rglru(u, seq_len, l0_lru_conv_w, l0_lru_conv_b, wcat, l0_lru_ba, l0_lru_bi,
                              l0_lru_lambda, h0)
        parts.append([(ya, 0), (yb, 0), (yb, 1)])
        lru_states.append(lru_state)
    xs = mix_ffn(parts, l0_w_out, xs, l0_norm2, mod0, conds, l0_ffn_gate, l0_ffn_up, l0_ffn_down)
    new_lru = lru_states[0]

    def layer1(x, seq_len, cond, latent):
        qnope, qpe, ckv, kr, kv, uh = in1_proj(x, l1_norm1, mod1, cond, l1_w_in, l1_q_norm, l1_kv_norm,
                                               l1_w_q_up, l1_w_kv_up)
        if latent:
            yc = attn_lat(qnope, qpe, kv_ctx, kr_ctx, kv, kr, seq_len, past_len)
        else:
            yc = attn_ctx(qnope, qpe, kv, kr, seq_len)
        cs = dft_tables(seq_len)
        k_r, k_s, k_ny = hy_filter(cs, w1p, b1p, w2p, b2p, w3p)
        yd = hyena(uh, seq_len, short_w, short_b, cs, k_r, k_s, k_ny, hy_bias)
        routed = mix_route([(yc, 0), (yc, 1), (yd, 0)], l1_w_out, x, l1_norm2, mod1, cond, wr_pad, br_pad)
        return routed, ckv, kr

    r_p, new_ckv, new_kr = layer1(xs[0], seq, conds[0], latent=False)
    r_s, _, _ = layer1(xs[1], dec_seq, conds[1], latent=True)

    routed = (r_p, r_s)
    pos, pos_all, pads, tile_expert, stages, tile_valid, n_rows = moe_plan([r[3] for r in routed],
                                                                          [r[4] for r in routed])
    hs = moe_dispatch(pos_all, pads, n_rows, r_p[1], r_s[1])
    y_rows = moe_experts(tile_expert, stages, tile_valid, hs, l1_exp_gate, l1_exp_up, l1_exp_down)
    y_p, y_s = [moe_combine(p, r[0], r[2], mod1, cond, final_norm, y_rows)
                for p, r, cond in zip(pos, routed, conds)]
    return (y_p.reshape(batch, seq, D), y_s.reshape(dec_batch, dec_seq, D), new_lru,
            new_ckv.reshape(batch, seq, KV_RANK), new_kr.reshape(batch, seq, ROPE))
```

```python
import functools
import math

import jax
import jax.numpy as jnp
from jax import lax
from jax.experimental import pallas as pl
from jax.experimental.pallas import tpu as pltpu

F32 = jnp.float32
BF16 = jnp.bfloat16

D = 1024
GRID_W = 64
EPS = 1e-6
CONV_W = 512
LRU_W = 1024
LRU_BW = 128
LRU_C = 8.0
MLA_HEADS = 8
Q_RANK = 384
KV_RANK = 256
NOPE = 128
ROPE = 64
VDIM = 128
QK_DIM = NOPE + ROPE
ROPE_THETA = 10000.0
HY_W = 512
HY_BANDS = 16
HY_TARGET = 1e-2
HY_FAST_DECAY = 0.3
HY_SLOW_DECAY = 1.5
D_FF = 2816
N_EXPERTS = 8
D_FF_EXPERT = 1408
IN0 = 3 * CONV_W + 2 * LRU_W
IN1 = Q_RANK + KV_RANK + ROPE + 3 * HY_W

V7X_LANES = 128
V7X_SUBLANES = 8
V7X_VMEM_LIMIT_BYTES = 56 * 1024 * 1024
V7X_VMEM_LIMIT_LARGE_BYTES = 60 * 1024 * 1024

TM = 512
TN_IN0 = 512
TF_FFN = 256
MIX_SLAB = 512
MOE_CHUNK = 256
TM_ROUTE = 512
TM_EXPERT = 512
TM_COMBINE = 512
LRU_CB = 512
HY_CB = 512
TQ = 256
ATTN_CTX_SEQS = 4
CONV_A_ROWS = 1024
LRU_ROWS = 1024
HY_ROWS = 1024
TM_IN1 = 512


def _cparams(sem, vmem_limit_bytes=V7X_VMEM_LIMIT_BYTES):
    return pltpu.CompilerParams(dimension_semantics=sem, vmem_limit_bytes=vmem_limit_bytes)


def _sigmoid(x):
    return 0.5 * jnp.tanh(0.5 * x) + 0.5


def _silu(x):
    return x * _sigmoid(x)


def _norm_mod(x, g, shift, scale):
    ms = jnp.mean(x * x, axis=-1, keepdims=True)
    y = x * lax.rsqrt(ms + EPS) * g
    return y * (1.0 + scale) + shift


def _mod_spec(comp, cond, tm, width, col_fn, tile_fn=lambda *ids: ids[0]):
    row0, seg = cond
    assert seg % tm == 0
    return pl.BlockSpec((None, 1, width),
                        lambda *ids: (comp * 3 + row0 + (tile_fn(*ids) * tm) // seg, 0, col_fn(*ids)))


def _dot3(a, b):
    a_hi = a.astype(BF16)
    a_lo = (a - a_hi.astype(F32)).astype(BF16)
    b_hi = b.astype(BF16)
    b_lo = (b - b_hi.astype(F32)).astype(BF16)
    n = a.shape[0]
    y = jnp.dot(jnp.concatenate([a_hi, a_lo], axis=0), b_hi, preferred_element_type=F32)
    return y[:n] + y[n:] + jnp.dot(a_hi, b_lo, preferred_element_type=F32)


def _adaln_kernel(c_ref, w0_ref, b0_ref, w1_ref, b1_ref, o_ref):
    a = _silu(c_ref[...])
    for layer, (w_ref, b_ref) in enumerate(((w0_ref, b0_ref), (w1_ref, b1_ref))):
        @pl.when(pl.program_id(0) == layer)
        def _():
            o_ref[...] = _dot3(a, w_ref[...]) + b_ref[...]


def adaln_tables(cond8, mods):
    tn = 1536
    nj = 6 * D // tn
    (w0, b0), (w1, b1) = mods
    at0 = lambda l, j: (0, jnp.where(l == 0, j, nj - 1))
    at1 = lambda l, j: (0, jnp.where(l == 1, j, 0))
    m = pl.pallas_call(
        _adaln_kernel,
        out_shape=jax.ShapeDtypeStruct((2, V7X_SUBLANES, 6 * D), F32),
        grid=(2, nj),
        in_specs=[pl.BlockSpec((V7X_SUBLANES, D), lambda l, j: (0, 0)),
                  pl.BlockSpec((D, tn), at0), pl.BlockSpec((1, tn), at0),
                  pl.BlockSpec((D, tn), at1), pl.BlockSpec((1, tn), at1)],
        out_specs=pl.BlockSpec((None, V7X_SUBLANES, tn), lambda l, j: (l, 0, j)),
        compiler_params=_cparams(("arbitrary", "arbitrary")),
        name="adaln",
    )(cond8, w0, b0.reshape(1, 6 * D), w1, b1.reshape(1, 6 * D))
    return [m[l, :3].reshape(3, 6, D).transpose(1, 0, 2).reshape(18, 1, D) for l in range(2)]


def _tile_of(n_load):
    return lambda s: jnp.maximum(s - n_load, 0)


def _block_of(n_load):
    return lambda s: jnp.minimum(s, n_load - 1)


class _TwoSets:
    def __init__(self, n_load, tm, tokens, conds):
        self.n_load, self.tm, self.conds = n_load, tm, conds
        self.n_a, self.n_b = tokens[0] // tm, tokens[1] // tm
        self.steps = n_load + self.n_a + self.n_b

    def tile(self, s):
        return jnp.maximum(s - self.n_load, 0)

    def in_first(self, s):
        return s - self.n_load < self.n_a

    def idx_a(self, s):
        return jnp.minimum(self.tile(s), self.n_a - 1)

    def idx_b(self, s):
        return jnp.clip(self.tile(s) - self.n_a, 0, self.n_b - 1)

    def rows(self, width):
        return (pl.BlockSpec((self.tm, width), lambda s: (self.idx_a(s), 0)),
                pl.BlockSpec((self.tm, width), lambda s: (self.idx_b(s), 0)))

    def cols(self, width, col):
        return (pl.BlockSpec((self.tm, width), lambda s: (self.idx_a(s), col)),
                pl.BlockSpec((self.tm, width), lambda s: (self.idx_b(s), col)))

    def mod_spec(self, comp):
        (row_a, seg_a), (row_b, seg_b) = self.conds
        assert seg_a % self.tm == 0 and seg_b % self.tm == 0

        def row(s):
            return jnp.where(self.in_first(s), row_a + (self.idx_a(s) * self.tm) // seg_a,
                             row_b + (self.idx_b(s) * self.tm) // seg_b)

        return pl.BlockSpec((None, 1, D), lambda s: (comp * 3 + row(s), 0, 0))


def _in0_kernel(xa_ref, xb_ref, g_ref, sh_ref, sc_ref, w_ref, oa_ref, ob_ref, w_sc, *, n_a):
    s = pl.program_id(0)
    n_load = w_sc.shape[0]

    @pl.when(s < n_load)
    def _():
        w_sc[s] = w_ref[...].astype(BF16)

    @pl.when(s >= n_load)
    def _():
        first = s - n_load < n_a
        x = jnp.where(first, xa_ref[...], xb_ref[...])
        h = _norm_mod(x, g_ref[...], sh_ref[...], sc_ref[...]).astype(BF16)
        u = jnp.concatenate([jnp.dot(h, w_sc[j], preferred_element_type=F32).astype(BF16)
                             for j in range(n_load)], axis=1)

        @pl.when(first)
        def _():
            oa_ref[...] = u

        @pl.when(jnp.logical_not(first))
        def _():
            ob_ref[...] = u


def in0_proj(xs, g, modtab, conds, w_in):
    tn = TN_IN0
    n = w_in.shape[1]
    n_load = n // tn
    ts = _TwoSets(n_load, TM, [x.shape[0] for x in xs], conds)
    blk = _block_of(n_load)
    return pl.pallas_call(
        functools.partial(_in0_kernel, n_a=ts.n_a),
        out_shape=tuple(jax.ShapeDtypeStruct((x.shape[0], n), BF16) for x in xs),
        grid=(ts.steps,),
        in_specs=[*ts.rows(D),
                  pl.BlockSpec((1, D), lambda s: (0, 0)),
                  ts.mod_spec(0), ts.mod_spec(1),
                  pl.BlockSpec((D, tn), lambda s: (0, blk(s)))],
        out_specs=ts.rows(n),
        scratch_shapes=[pltpu.VMEM((n_load, D, tn), BF16)],
        compiler_params=_cparams(("arbitrary",)),
        name="in0_proj",
    )(*xs, g.reshape(1, D), modtab, modtab, w_in)


def _shift_rows(v, d, t, seq_len=None):
    n = v.shape[0]
    seq_len = n if seq_len is None else seq_len
    if d > 0:
        return jnp.where(t < d, 0.0, pltpu.roll(v, d, 0))
    return jnp.where(t >= seq_len + d, 0.0, pltpu.roll(v, n + d, 0))


def _conv_a_kernel(b_ref, c_ref, x_ref, w_ref, o_ref, *, seq_len):
    v = c_ref[...].astype(F32) * x_ref[...].astype(F32)
    t = lax.broadcasted_iota(jnp.int32, v.shape, 0) & (seq_len - 1)
    w = w_ref[...]
    y = w[0:1] * _shift_rows(v, 1, t, seq_len) + w[1:2] * v + w[2:3] * _shift_rows(v, -1, t, seq_len)
    o_ref[...] = (b_ref[...].astype(F32) * y).astype(o_ref.dtype)


def conv_a(u, seq_len, conv_w):
    tokens = u.shape[0]
    rows = max(seq_len, CONV_A_ROWS)
    assert seq_len & (seq_len - 1) == 0 and rows % seq_len == 0
    return pl.pallas_call(
        functools.partial(_conv_a_kernel, seq_len=seq_len),
        out_shape=jax.ShapeDtypeStruct((tokens, CONV_W), BF16),
        grid=(tokens // rows,),
        in_specs=[pl.BlockSpec((rows, CONV_W), lambda s: (s, 0)),
                  pl.BlockSpec((rows, CONV_W), lambda s: (s, 1)),
                  pl.BlockSpec((rows, CONV_W), lambda s: (s, 2)),
                  pl.BlockSpec((3, CONV_W), lambda s: (0, 0))],
        out_specs=pl.BlockSpec((rows, CONV_W), lambda s: (s, 0)),
        compiler_params=_cparams(("parallel",)),
        name="conv_a",
    )(u, u, u, conv_w)


def _group_scan(a_sc, b_sc, k, reverse):
    planes = a_sc.shape[1] // V7X_SUBLANES
    order = range(V7X_SUBLANES - 1, -1, -1) if reverse else range(V7X_SUBLANES)
    a_acc = b_acc = None
    for r in order:
        plane = (k, pl.ds(r, planes, stride=V7X_SUBLANES), slice(None))
        a_r, b_r = a_sc[plane], b_sc[plane]
        if a_acc is None:
            a_acc, b_acc = a_r, b_r
        else:
            b_acc = a_r * b_acc + b_r
            a_acc = a_r * a_acc
            a_sc[plane] = a_acc
            b_sc[plane] = b_acc


def _rglru_kernel(gate_ref, xb_ref, cw_ref, cb_ref, wcat_ref, ba_ref, bi_ref, lam_ref, h0_ref,
                  y_ref, st_ref, af_sc, bf_sc, ab_sc, bb_sc, hf_sc, hb_sc, *, seq_len):
    n, cb = xb_ref.shape
    n_seq = n // seq_len
    n_slab = cb // LRU_BW
    xb = xb_ref[...].astype(F32)
    t = lax.broadcasted_iota(jnp.int32, xb.shape, 0) & (seq_len - 1)
    cw = cw_ref[...]
    sh = lambda d: _shift_rows(xb, d, t, seq_len)
    xc = cb_ref[...] + cw[0:1] * sh(2) + cw[1:2] * sh(1) + cw[2:3] * xb + cw[3:4] * sh(-1)
    xcb = xc.astype(BF16)

    for k in range(n_slab):
        cols = slice(k * LRU_BW, (k + 1) * LRU_BW)
        gk = jnp.dot(xcb[:, cols], wcat_ref[k].astype(BF16), preferred_element_type=F32)
        for d, (a_sc, b_sc) in enumerate(((af_sc, bf_sc), (ab_sc, bb_sc))):
            ga = gk[:, (2 * d) * LRU_BW:(2 * d + 1) * LRU_BW]
            gi = gk[:, (2 * d + 1) * LRU_BW:(2 * d + 2) * LRU_BW]
            r = _sigmoid(ga + ba_ref[d:d + 1, cols])
            i = _sigmoid(gi + bi_ref[d:d + 1, cols])
            log_a = (-LRU_C * jax.nn.softplus(-lam_ref[d:d + 1, cols])) * r
            a = jnp.exp(log_a)
            m = 1.0 - a * a
            mult = m * lax.rsqrt(jnp.maximum(m, 1e-30))
            a_sc[k] = a
            b_sc[k] = mult * (i * xc[:, cols])
            _group_scan(a_sc, b_sc, k, reverse=(d == 1))

    ng = seq_len // V7X_SUBLANES
    bcast = lambda row: jnp.broadcast_to(row, (V7X_SUBLANES, LRU_BW))
    chains = [(q, k) for q in range(n_seq) for k in range(n_slab)]
    init = tuple((bcast(h0_ref[q, 0:1, k * LRU_BW:(k + 1) * LRU_BW]),
                  bcast(h0_ref[q, 1:2, k * LRU_BW:(k + 1) * LRU_BW])) for q, k in chains)

    def step(j, carry):
        out = []
        for (q, k), (hf_in, hb_in) in zip(chains, carry):
            rf = pl.ds(pl.multiple_of(q * seq_len + j * V7X_SUBLANES, V7X_SUBLANES), V7X_SUBLANES)
            rb = pl.ds(pl.multiple_of(q * seq_len + (ng - 1 - j) * V7X_SUBLANES, V7X_SUBLANES), V7X_SUBLANES)
            hf = af_sc[k, rf, :] * hf_in + bf_sc[k, rf, :]
            hb = ab_sc[k, rb, :] * hb_in + bb_sc[k, rb, :]
            hf_sc[k, rf, :] = hf
            hb_sc[k, rb, :] = hb
            out.append((bcast(hf[V7X_SUBLANES - 1:V7X_SUBLANES]), bcast(hb[0:1])))
        return tuple(out)

    final = lax.fori_loop(0, ng, step, init)
    for (q, k), (hf_last, hb_first) in zip(chains, final):
        st_ref[q, 0:1, k * LRU_BW:(k + 1) * LRU_BW] = hf_last[0:1]
        st_ref[q, 1:2, k * LRU_BW:(k + 1) * LRU_BW] = hb_first[0:1]

    gt = gate_ref[...].astype(F32)
    gelu = 0.5 * gt * (1.0 + jnp.tanh(math.sqrt(2.0 / math.pi) * (gt + 0.044715 * (gt * gt * gt))))
    h = jnp.concatenate([hf_sc[k] + hb_sc[k] for k in range(n_slab)], axis=1)
    y_ref[...] = (h * gelu).astype(y_ref.dtype)


def rglru(u, seq_len, conv_w, conv_b, wcat, ba, bi, lam, h0):
    tokens = u.shape[0]
    nseq = tokens // seq_len
    cb = LRU_CB
    rows = max(seq_len, LRU_ROWS)
    assert seq_len & (seq_len - 1) == 0 and rows % seq_len == 0
    per_blk = rows // seq_len
    gate_blk0 = 3 * CONV_W // cb
    xb_blk0 = (3 * CONV_W + LRU_W) // cb
    seq_scr = lambda: pltpu.VMEM((cb // LRU_BW, rows, LRU_BW), F32)
    return pl.pallas_call(
        functools.partial(_rglru_kernel, seq_len=seq_len),
        out_shape=(jax.ShapeDtypeStruct((tokens, LRU_W), BF16), jax.ShapeDtypeStruct((nseq, 2, LRU_W), F32)),
        grid=(tokens // rows, LRU_W // cb),
        in_specs=[pl.BlockSpec((rows, cb), lambda s, c: (s, gate_blk0 + c)),
                  pl.BlockSpec((rows, cb), lambda s, c: (s, xb_blk0 + c)),
                  pl.BlockSpec((4, cb), lambda s, c: (0, c)),
                  pl.BlockSpec((1, cb), lambda s, c: (0, c)),
                  pl.BlockSpec((cb // LRU_BW, LRU_BW, 4 * LRU_BW), lambda s, c: (c, 0, 0)),
                  pl.BlockSpec((2, cb), lambda s, c: (0, c)),
                  pl.BlockSpec((2, cb), lambda s, c: (0, c)),
                  pl.BlockSpec((2, cb), lambda s, c: (0, c)),
                  pl.BlockSpec((per_blk, 2, cb), lambda s, c: (s, 0, c))],
        out_specs=(pl.BlockSpec((rows, cb), lambda s, c: (s, c)),
                   pl.BlockSpec((per_blk, 2, cb), lambda s, c: (s, 0, c))),
        scratch_shapes=[seq_scr() for _ in range(6)],
        compiler_params=_cparams(("parallel", "parallel")),
        name="rglru",
    )(u, u, conv_w, conv_b.reshape(1, LRU_W), wcat, ba, bi, lam, h0)


def _mix_ffn_kernel(p0a_ref, p0b_ref, p1a_ref, p1b_ref, p2a_ref, p2b_ref, wo_ref, xa_ref, xb_ref,
                    g1_ref, g_ref, sh_ref, sc_ref, g2_ref, wg_ref, wu_ref, wd_ref, oa_ref, ob_ref,
                    wo_sc, wg_sc, wu_sc, wd_sc, *, n_a):
    s = pl.program_id(0)
    n_load = wg_sc.shape[0]
    n_out = wo_sc.shape[0]

    @pl.when(s < n_out)
    def _():
        wo_sc[s] = wo_ref[...].astype(BF16)

    @pl.when(s < n_load)
    def _():
        wg_sc[s] = wg_ref[...].astype(BF16)
        wu_sc[s] = wu_ref[...].astype(BF16)
        wd_sc[s] = wd_ref[...].astype(BF16)

    @pl.when(s >= n_load)
    def _():
        first = s - n_load < n_a
        pick = lambda a_ref, b_ref: jnp.where(first, a_ref[...], b_ref[...])
        m = jnp.dot(pick(p0a_ref, p0b_ref), wo_sc[0], preferred_element_type=F32)
        m += jnp.dot(pick(p1a_ref, p1b_ref), wo_sc[1], preferred_element_type=F32)
        m += jnp.dot(pick(p2a_ref, p2b_ref), wo_sc[2], preferred_element_type=F32)
        x = pick(xa_ref, xb_ref) + g1_ref[...] * m
        h = _norm_mod(x, g_ref[...], sh_ref[...], sc_ref[...]).astype(BF16)
        y = None
        for f in range(n_load):
            hg = jnp.dot(h, wg_sc[f], preferred_element_type=F32)
            hu = jnp.dot(h, wu_sc[f], preferred_element_type=F32)
            act = (_silu(hg) * hu).astype(BF16)
            yf = jnp.dot(act, wd_sc[f], preferred_element_type=F32)
            y = yf if y is None else y + yf
        out = x + g2_ref[...] * y

        @pl.when(first)
        def _():
            oa_ref[...] = out

        @pl.when(jnp.logical_not(first))
        def _():
            ob_ref[...] = out


def mix_ffn(parts, w_out, xs, g, modtab, conds, w_gate, w_up, w_down):
    tf = TF_FFN
    kb = MIX_SLAB
    n_load = D_FF // tf
    n_out = len(parts[0])
    assert n_out <= n_load
    ts = _TwoSets(n_load, TM, [x.shape[0] for x in xs], conds)
    blk = _block_of(n_load)
    oblk = _block_of(n_out)
    lhs_specs, lhs_args = [], []
    for (arr_a, col_a), (arr_b, col_b) in zip(*parts):
        assert col_a == col_b
        lhs_specs += ts.cols(kb, col_a)
        lhs_args += [arr_a, arr_b]
    return pl.pallas_call(
        functools.partial(_mix_ffn_kernel, n_a=ts.n_a),
        out_shape=tuple(jax.ShapeDtypeStruct(x.shape, F32) for x in xs),
        grid=(ts.steps,),
        in_specs=lhs_specs + [
            pl.BlockSpec((kb, D), lambda s: (oblk(s), 0)),
            *ts.rows(D),
            ts.mod_spec(2),
            pl.BlockSpec((1, D), lambda s: (0, 0)),
            ts.mod_spec(3), ts.mod_spec(4), ts.mod_spec(5),
            pl.BlockSpec((D, tf), lambda s: (0, blk(s))),
            pl.BlockSpec((D, tf), lambda s: (0, blk(s))),
            pl.BlockSpec((tf, D), lambda s: (blk(s), 0))],
        out_specs=ts.rows(D),
        scratch_shapes=[pltpu.VMEM((n_out, kb, D), BF16),
                        pltpu.VMEM((n_load, D, tf), BF16), pltpu.VMEM((n_load, D, tf), BF16),
                        pltpu.VMEM((n_load, tf, D), BF16)],
        compiler_params=_cparams(("arbitrary",), V7X_VMEM_LIMIT_LARGE_BYTES),
        name="mix_ffn",
    )(*lhs_args, w_out, *xs, modtab, g.reshape(1, D), modtab, modtab, modtab, w_gate, w_up, w_down)


def _rms(x, g):
    return x * lax.rsqrt(jnp.mean(x * x, axis=-1, keepdims=True) + EPS) * g


def _in1_kernel(x_ref, g_ref, sh_ref, sc_ref, w_ref, qn_ref, kvn_ref, wq_ref, wkv_ref,
                qnope_ref, qpe_ref, ckv_ref, kr_ref, kv_ref, uh_ref, w_sc, wq_sc, wkv_sc):
    @pl.when(pl.program_id(0) == 0)
    def _():
        w_sc[...] = w_ref[...].astype(BF16)
        for h in range(MLA_HEADS):
            c0 = h * QK_DIM
            wq_sc[:, h * NOPE:(h + 1) * NOPE] = wq_ref[:, c0:c0 + NOPE].astype(BF16)
            r0 = MLA_HEADS * NOPE + h * ROPE
            wq_sc[:, r0:r0 + ROPE] = wq_ref[:, c0 + NOPE:c0 + QK_DIM].astype(BF16)
        wkv_sc[...] = wkv_ref[...].astype(BF16)

    h = _norm_mod(x_ref[...], g_ref[...], sh_ref[...], sc_ref[...]).astype(BF16)
    u = lax.dot_general(h, w_sc[...], (((1,), (1,)), ((), ())), preferred_element_type=F32)
    o1, o2, o3 = Q_RANK, Q_RANK + KV_RANK, Q_RANK + KV_RANK + ROPE
    cq = _rms(u[:, :o1], qn_ref[...])
    q = jnp.dot(cq.astype(BF16), wq_sc[...], preferred_element_type=F32) * _SCALE
    qnope_ref[...] = q[:, :MLA_HEADS * NOPE].astype(qnope_ref.dtype)
    qpe_ref[...] = q[:, MLA_HEADS * NOPE:]
    ckv = _rms(u[:, o1:o2], kvn_ref[...])
    ckv_ref[...] = ckv
    kv_ref[...] = jnp.dot(ckv.astype(BF16), wkv_sc[...], preferred_element_type=F32).astype(kv_ref.dtype)
    kr_ref[...] = u[:, o2:o3]
    uh_ref[...] = u[:, o3:]


def in1_proj(x, g, modtab, cond, w_in, q_norm, kv_norm, w_q_up, w_kv_up):
    tokens = x.shape[0]
    tm = TM_IN1
    nkv = MLA_HEADS * (NOPE + VDIM)
    const = lambda i: (0, 0)
    zero = lambda i: 0
    once = pl.Buffered(1)
    outs = (jax.ShapeDtypeStruct((tokens, MLA_HEADS * NOPE), BF16),
            jax.ShapeDtypeStruct((tokens, MLA_HEADS * ROPE), F32),
            jax.ShapeDtypeStruct((tokens, KV_RANK), F32),
            jax.ShapeDtypeStruct((tokens, ROPE), F32),
            jax.ShapeDtypeStruct((tokens, nkv), BF16),
            jax.ShapeDtypeStruct((tokens, 3 * HY_W), F32))
    row = lambda w: pl.BlockSpec((tm, w), lambda i: (i, 0))
    return pl.pallas_call(
        _in1_kernel,
        out_shape=outs,
        grid=(tokens // tm,),
        in_specs=[row(D),
                  pl.BlockSpec((1, D), const),
                  _mod_spec(0, cond, tm, D, zero),
                  _mod_spec(1, cond, tm, D, zero),
                  pl.BlockSpec((IN1, D), const, pipeline_mode=once),
                  pl.BlockSpec((1, Q_RANK), const),
                  pl.BlockSpec((1, KV_RANK), const),
                  pl.BlockSpec((Q_RANK, MLA_HEADS * QK_DIM), const, pipeline_mode=once),
                  pl.BlockSpec((KV_RANK, nkv), const, pipeline_mode=once)],
        out_specs=tuple(row(o.shape[1]) for o in outs),
        scratch_shapes=[pltpu.VMEM((IN1, D), BF16), pltpu.VMEM((Q_RANK, MLA_HEADS * QK_DIM), BF16),
                        pltpu.VMEM((KV_RANK, nkv), BF16)],
        compiler_params=_cparams(("arbitrary",)),
        name="in1_proj",
    )(x, g.reshape(1, D), modtab, modtab, w_in.T, q_norm.reshape(1, Q_RANK), kv_norm.reshape(1, KV_RANK),
      w_q_up, w_kv_up)


def _mm_kernel(a_ref, w_ref, o_ref):
    o_ref[...] = jnp.dot(a_ref[...].astype(BF16), w_ref[...].astype(BF16),
                         preferred_element_type=F32).astype(o_ref.dtype)


def kv_up(ckv, w_kv_up):
    rows = ckv.shape[0]
    n = w_kv_up.shape[1]
    return pl.pallas_call(
        _mm_kernel,
        out_shape=jax.ShapeDtypeStruct((rows, n), BF16),
        grid=(rows // TM,),
        in_specs=[pl.BlockSpec((TM, KV_RANK), lambda i: (i, 0)), pl.BlockSpec((KV_RANK, n), lambda i: (0, 0))],
        out_specs=pl.BlockSpec((TM, n), lambda i: (i, 0)),
        compiler_params=_cparams(("parallel",)),
        name="kv_up",
    )(ckv, w_kv_up)


_NT = (((1,), (1,)), ((), ()))
_SCALE = 1.0 / math.sqrt(QK_DIM)


def _fill_rope_tables(cos_ref, sin_ref):
    n, width = cos_ref.shape
    n_grid_rows = n // GRID_W
    n_freq = ROPE // 4

    def trig(count):
        lane = lax.broadcasted_iota(jnp.int32, (count, width), 1)
        j = lane & (ROPE // 2 - 1)
        inv = jnp.exp((j & (n_freq - 1)).astype(F32) * (-math.log(ROPE_THETA) / n_freq))
        ang = lax.broadcasted_iota(jnp.int32, (count, width), 0).astype(F32) * inv
        return jnp.cos(ang), jnp.sin(ang), j < n_freq

    cos_c, sin_c, by_row = trig(GRID_W)
    cos_r, sin_r, _ = trig(n_grid_rows)
    for r in range(n_grid_rows):
        rows = slice(r * GRID_W, (r + 1) * GRID_W)
        cos_ref[rows, :] = jnp.where(by_row, jnp.broadcast_to(cos_r[r:r + 1], cos_c.shape), cos_c)
        sin_ref[rows, :] = jnp.where(by_row, jnp.broadcast_to(sin_r[r:r + 1], sin_c.shape), sin_c)


def _rope(x, cos, sin):
    width = x.shape[1]
    lane = lax.broadcasted_iota(jnp.int32, x.shape, 1)
    first_half = (lane & (ROPE - 1)) < ROPE // 2
    xr = jnp.where(first_half, -pltpu.roll(x, width - ROPE // 2, 1), pltpu.roll(x, ROPE // 2, 1))
    return x * cos + xr * sin


def _ones_column(n):
    lane = lax.broadcasted_iota(jnp.int32, (n, VDIM), 1)
    return jnp.where(lane == 0, 1.0, 0.0).astype(BF16)


def _head_attention(qcat, kcat, vaug):
    s = lax.dot_general(qcat, kcat, _NT, preferred_element_type=F32)
    p = jnp.exp(s - jnp.max(s, axis=-1, keepdims=True)).astype(BF16)
    oa = jnp.dot(p, vaug, preferred_element_type=F32)
    return oa[:, :VDIM] / oa[:, VDIM:VDIM + 1]


def _attn_ctx_kernel(qn_ref, qpe_ref, kv_ref, kr_ref, o_ref, *, seq_len):
    n = qn_ref.shape[0]
    n_seq = n // seq_len
    ones = _ones_column(n)
    kpe = kr_ref[...].astype(BF16)
    per_seq = lambda a: a.reshape(n_seq, seq_len, a.shape[-1])
    for h in range(MLA_HEADS):
        c0 = h * (NOPE + VDIM)
        qcat = per_seq(jnp.concatenate([qn_ref[:, h * NOPE:(h + 1) * NOPE],
                                        qpe_ref[:, h * ROPE:(h + 1) * ROPE].astype(BF16)], axis=1))
        kcat = per_seq(jnp.concatenate([kv_ref[:, c0:c0 + NOPE], kpe], axis=1))
        vaug = per_seq(jnp.concatenate([kv_ref[:, c0 + NOPE:c0 + NOPE + VDIM], ones], axis=1))
        s = jnp.einsum("bqd,bkd->bqk", qcat, kcat, preferred_element_type=F32)
        p = jnp.exp(s - jnp.max(s, axis=-1, keepdims=True)).astype(BF16)
        oa = jnp.einsum("bqk,bkd->bqd", p, vaug, preferred_element_type=F32)
        o = oa[:, :, :VDIM] / oa[:, :, VDIM:VDIM + 1]
        o_ref[:, h * VDIM:(h + 1) * VDIM] = o.reshape(n, VDIM).astype(o_ref.dtype)


def attn_ctx(qnope, qpe, kv, kr, seq_len):
    tokens = qnope.shape[0]
    rows = ATTN_CTX_SEQS * seq_len
    blk = lambda w: pl.BlockSpec((rows, w), lambda s: (s, 0))
    return pl.pallas_call(
        functools.partial(_attn_ctx_kernel, seq_len=seq_len),
        out_shape=jax.ShapeDtypeStruct((tokens, MLA_HEADS * VDIM), BF16),
        grid=(tokens // rows,),
        in_specs=[blk(MLA_HEADS * NOPE), blk(MLA_HEADS * ROPE), blk(MLA_HEADS * (NOPE + VDIM)), blk(ROPE)],
        out_specs=blk(MLA_HEADS * VDIM),
        compiler_params=_cparams(("parallel",)),
        name="attn_ctx",
    )(qnope, qpe, kv, kr)


def _attn_lat_kernel(qn_ref, qpe_ref, kvc_ref, krc_ref, kvl_ref, krl_ref, o_ref, kcat_sc, vaug_sc, cos_sc, sin_sc):
    tq = qn_ref.shape[0]
    n_ctx = krc_ref.shape[0]
    n_lat = krl_ref.shape[0]

    @pl.when(pl.program_id(1) == 0)
    def _():
        _fill_rope_tables(cos_sc, sin_sc)
        kr2 = jnp.concatenate([krl_ref[...], krl_ref[...]], axis=1)
        kpe_lat = _rope(kr2, cos_sc[...], sin_sc[...])[:, :ROPE].astype(BF16)
        kpe_ctx = krc_ref[...].astype(BF16)
        ones_c, ones_l = _ones_column(n_ctx), _ones_column(n_lat)
        for h in range(MLA_HEADS):
            c0 = h * (NOPE + VDIM)
            for r0, nr, kv_ref, kpe, ones in ((0, n_ctx, kvc_ref, kpe_ctx, ones_c), (n_ctx, n_lat, kvl_ref, kpe_lat, ones_l)):
                kcat_sc[h, r0:r0 + nr, 0:NOPE] = kv_ref[:, c0:c0 + NOPE]
                kcat_sc[h, r0:r0 + nr, NOPE:QK_DIM] = kpe
                vaug_sc[h, r0:r0 + nr, 0:VDIM] = kv_ref[:, c0 + NOPE:c0 + NOPE + VDIM]
                vaug_sc[h, r0:r0 + nr, VDIM:2 * VDIM] = ones

    q0 = pl.multiple_of(pl.program_id(1) * tq, tq)
    rep = lambda a: jnp.concatenate([a] * (MLA_HEADS // 2), axis=1)
    qp_all = _rope(qpe_ref[...], rep(cos_sc[pl.ds(q0, tq), :]), rep(sin_sc[pl.ds(q0, tq), :])).astype(BF16)
    for h in range(MLA_HEADS):
        qcat = jnp.concatenate([qn_ref[:, h * NOPE:(h + 1) * NOPE], qp_all[:, h * ROPE:(h + 1) * ROPE]], axis=1)
        o_ref[:, h * VDIM:(h + 1) * VDIM] = _head_attention(qcat, kcat_sc[h], vaug_sc[h]).astype(o_ref.dtype)


def attn_lat(qnope, qpe, kv_ctx, kr_ctx, kv_lat, kr_lat, seq_len, ctx_len):
    tokens = qnope.shape[0]
    nq = seq_len // TQ
    qblk = lambda w: pl.BlockSpec((TQ, w), lambda b, i: (b * nq + i, 0))
    seq = lambda n, w: pl.BlockSpec((n, w), lambda b, i: (b, 0))
    nkv = MLA_HEADS * (NOPE + VDIM)
    n_keys = ctx_len + seq_len
    return pl.pallas_call(
        _attn_lat_kernel,
        out_shape=jax.ShapeDtypeStruct((tokens, MLA_HEADS * VDIM), BF16),
        grid=(tokens // seq_len, nq),
        in_specs=[qblk(MLA_HEADS * NOPE), qblk(MLA_HEADS * ROPE), seq(ctx_len, nkv), seq(ctx_len, ROPE),
                  seq(seq_len, nkv), seq(seq_len, ROPE)],
        out_specs=qblk(MLA_HEADS * VDIM),
        scratch_shapes=[pltpu.VMEM((MLA_HEADS, n_keys, QK_DIM), BF16),
                        pltpu.VMEM((MLA_HEADS, n_keys, 2 * VDIM), BF16),
                        pltpu.VMEM((seq_len, 2 * ROPE), F32), pltpu.VMEM((seq_len, 2 * ROPE), F32)],
        compiler_params=_cparams(("parallel", "arbitrary")),
        name="attn_lat",
    )(qnope, qpe, kv_ctx, kr_ctx, kv_lat, kr_lat)


def _dft_kernel(o_ref):
    tr, n = o_ref.shape[1], o_ref.shape[2]
    nb = n // V7X_LANES
    f = pl.program_id(0) * tr + lax.broadcasted_iota(jnp.int32, (tr, V7X_LANES), 0)
    j = lax.broadcasted_iota(jnp.int32, (tr, V7X_LANES), 1)

    def cos_sin(m):
        ang = (m & (2 * n - 1)).astype(F32) * (math.pi / n)
        return jnp.cos(ang), jnp.sin(ang)

    cj, sj = cos_sin(f * j)
    cb, sb = cos_sin(f * (j * V7X_LANES))
    for b in range(nb):
        cbb, sbb = cb[:, b:b + 1], sb[:, b:b + 1]
        cols = slice(b * V7X_LANES, (b + 1) * V7X_LANES)
        o_ref[0, :, cols] = (cbb * cj - sbb * sj).astype(o_ref.dtype)
        o_ref[1, :, cols] = (sbb * cj + cbb * sj).astype(o_ref.dtype)


def dft_tables(n):
    tr = 128
    return pl.pallas_call(
        _dft_kernel,
        out_shape=jax.ShapeDtypeStruct((2, n, n), BF16),
        grid=(n // tr,),
        out_specs=pl.BlockSpec((2, tr, n), lambda i: (0, i, 0)),
        compiler_params=_cparams(("parallel",)),
        name="dft_tables",
    )()


def _split_dot(table, x):
    hi = x.astype(BF16)
    lo = (x - hi.astype(F32)).astype(BF16)
    return (jnp.dot(table, hi, preferred_element_type=F32) + jnp.dot(table, lo, preferred_element_type=F32))


def _hy_filter_kernel(cs_ref, w1_ref, b1_ref, w2_ref, b2_ref, w3_ref, kr_ref, ks_ref, kny_ref):
    n = cs_ref.shape[1]
    row = lax.broadcasted_iota(jnp.int32, (n, V7X_LANES), 0).astype(F32)
    lane = lax.broadcasted_iota(jnp.int32, (n, V7X_LANES), 1)
    t = row * (1.0 / (n - 1))
    w = (2.0 * math.pi) * row / n
    band = jnp.where(lane <= HY_BANDS, lane - 1, lane - 1 - HY_BANDS).astype(F32)
    freq = 1e-4 + band * ((HY_BANDS - 1 - 1e-4) / (HY_BANDS - 1))
    arg = jnp.where(lane <= HY_BANDS, freq * w + 0.5 * math.pi, -(freq * w))
    z = jnp.where(lane == 0, t, jnp.where(lane <= 2 * HY_BANDS, jnp.sin(arg), 0.0))
    hid = jnp.sin(_dot3(z, w1_ref[...]) + b1_ref[...])
    hid = jnp.sin(_dot3(hid, w2_ref[...]) + b2_ref[...])
    hf = _dot3(hid, w3_ref[...])

    rowc = lax.broadcasted_iota(jnp.int32, (n, HY_W), 0)
    chan = lax.broadcasted_iota(jnp.int32, (n, HY_W), 1).astype(F32)
    max_decay = math.log(HY_TARGET) / HY_FAST_DECAY
    min_decay = math.log(HY_TARGET) / HY_SLOW_DECAY
    deltas = min_decay + chan * ((max_decay - min_decay) / (HY_W - 1))
    decay = jnp.exp(-(rowc.astype(F32) * (1.0 / (n - 1))) * jnp.abs(deltas))
    h_fwd = hf[:, :HY_W] * decay
    h_bwd = jnp.where(rowc == 0, 0.0, hf[:, HY_W:] * decay)
    norm = jnp.sum(jnp.abs(h_fwd) + jnp.abs(h_bwd), axis=0, keepdims=True)
    even = (h_fwd + h_bwd) / norm
    odd = (h_fwd - h_bwd) / norm
    cf = jnp.where(rowc == 0, 1.0, 2.0) * (1.0 / (2 * n))
    kr_ref[...] = cf * _split_dot(cs_ref[0], even)
    ks_ref[...] = cf * _split_dot(cs_ref[1], odd)
    sgn = jnp.where((rowc & 1) == 1, -1.0, 1.0)
    kny_ref[...] = jnp.sum(sgn * even, axis=0, keepdims=True) * (1.0 / (2 * n))


def hy_filter(cs, w1p, b1p, w2p, b2p, w3p):
    n = cs.shape[1]
    full = lambda a: pl.BlockSpec(a.shape, lambda: (0,) * a.ndim)
    args = (cs, w1p, b1p, w2p, b2p, w3p)
    return pl.pallas_call(
        _hy_filter_kernel,
        out_shape=(jax.ShapeDtypeStruct((n, HY_W), F32), jax.ShapeDtypeStruct((n, HY_W), F32),
                   jax.ShapeDtypeStruct((1, HY_W), F32)),
        in_specs=[full(a) for a in args],
        out_specs=(pl.BlockSpec((n, HY_W), lambda: (0, 0)), pl.BlockSpec((n, HY_W), lambda: (0, 0)),
                   pl.BlockSpec((1, HY_W), lambda: (0, 0))),
        compiler_params=pltpu.CompilerParams(vmem_limit_bytes=V7X_VMEM_LIMIT_BYTES),
        name="hy_filter",
    )(*args)


def _hyena_kernel(u0_ref, u1_ref, u2_ref, sw_ref, sb_ref, cs_ref, kr_ref, ks_ref, kny_ref, bias_ref, o_ref,
                  *, seq_len):
    n, cb = u0_ref.shape
    n_seq = n // seq_len
    t = lax.broadcasted_iota(jnp.int32, (n, cb), 0) & (seq_len - 1)

    def short_conv(u_ref, k):
        u = u_ref[...]
        w = sw_ref[:, k * cb:(k + 1) * cb]
        return (sb_ref[:, k * cb:(k + 1) * cb] + w[0:1] * _shift_rows(u, 1, t, seq_len) + w[1:2] * u
                + w[2:3] * _shift_rows(u, -1, t, seq_len))

    x0 = short_conv(u0_ref, 0)
    z = short_conv(u1_ref, 1) * short_conv(u2_ref, 2)
    wide = lambda a: jnp.concatenate([a[q * seq_len:(q + 1) * seq_len] for q in range(n_seq)], axis=1)
    rep = lambda a: jnp.concatenate([a] * n_seq, axis=1)
    zw = wide(z)
    zb = zw.astype(BF16)
    c, s = cs_ref[0], cs_ref[1]
    ur = jnp.dot(c, zb, preferred_element_type=F32)
    us = jnp.dot(s, zb, preferred_element_type=F32)
    sgn = jnp.where((lax.broadcasted_iota(jnp.int32, zw.shape, 0) & 1) == 1, -1.0, 1.0)
    uny = jnp.sum(sgn * zw, axis=0, keepdims=True)
    kr, ks = rep(kr_ref[...]), rep(ks_ref[...])
    yr = (ur * kr - us * ks).astype(BF16)
    ys = (ur * ks + us * kr).astype(BF16)
    yw = jnp.dot(c, yr, preferred_element_type=F32) + jnp.dot(s, ys, preferred_element_type=F32)
    yw = yw + sgn * (uny * rep(kny_ref[...]))
    y = jnp.concatenate([yw[:, q * cb:(q + 1) * cb] for q in range(n_seq)], axis=0)
    o_ref[...] = (x0 * (y + bias_ref[...] * z)).astype(o_ref.dtype)


def hyena(uh, seq_len, short_w, short_b, cs, kr, ks, kny, bias):
    tokens = uh.shape[0]
    cb = HY_CB
    nc = HY_W // cb
    rows = max(seq_len, HY_ROWS)
    assert seq_len & (seq_len - 1) == 0 and rows % seq_len == 0
    ublk = lambda k: pl.BlockSpec((rows, cb), lambda s, c: (s, k * nc + c))
    chan = lambda r: pl.BlockSpec((r, cb), lambda s, c: (0, c))
    return pl.pallas_call(
        functools.partial(_hyena_kernel, seq_len=seq_len),
        out_shape=jax.ShapeDtypeStruct((tokens, HY_W), BF16),
        grid=(tokens // rows, nc),
        in_specs=[ublk(0), ublk(1), ublk(2),
                  pl.BlockSpec((None, 3, 3 * cb), lambda s, c: (c, 0, 0)),
                  pl.BlockSpec((None, 1, 3 * cb), lambda s, c: (c, 0, 0)),
                  pl.BlockSpec((2, seq_len, seq_len), lambda s, c: (0, 0, 0)),
                  chan(seq_len), chan(seq_len), chan(1), chan(1)],
        out_specs=pl.BlockSpec((rows, cb), lambda s, c: (s, c)),
        compiler_params=_cparams(("parallel", "parallel")),
        name="hyena",
    )(uh, uh, uh, short_w, short_b, cs, kr, ks, kny, bias)


META_E1, META_E2, META_R1, META_R2, META_G1, META_G2 = range(6)


def _route_kernel(p0_ref, p1_ref, p2_ref, wo_ref, x_ref, g1_ref, g_ref, sh_ref, sc_ref, wr_ref, br_ref,
                  x1_ref, h_ref, meta_ref, meta_t_ref, cnt_ref, run_sc, wo_sc):
    tm = x_ref.shape[0]
    lane = lax.broadcasted_iota(jnp.int32, (tm, V7X_LANES), 1)

    @pl.when(pl.program_id(0) == 0)
    def _():
        run_sc[...] = jnp.zeros_like(run_sc)
        wo_sc[...] = wo_ref[...].astype(BF16)

    kb = p0_ref.shape[1]
    m = jnp.dot(p0_ref[...], wo_sc[0:kb, :], preferred_element_type=F32)
    m += jnp.dot(p1_ref[...], wo_sc[kb:2 * kb, :], preferred_element_type=F32)
    m += jnp.dot(p2_ref[...], wo_sc[2 * kb:3 * kb, :], preferred_element_type=F32)
    x1 = x_ref[...] + g1_ref[...] * m
    x1_ref[...] = x1
    h = _norm_mod(x1, g_ref[...], sh_ref[...], sc_ref[...])
    h_ref[...] = h
    logits = _dot3(h, wr_ref[...]) + br_ref[...]
    lg = jnp.where(lane < N_EXPERTS, logits, -jnp.inf)
    l1 = jnp.max(lg, axis=-1, keepdims=True)
    i1 = jnp.min(jnp.where(lg == l1, lane, V7X_LANES), axis=-1, keepdims=True)
    rest = jnp.where(lane == i1, -jnp.inf, lg)
    l2 = jnp.max(rest, axis=-1, keepdims=True)
    i2 = jnp.min(jnp.where(rest == l2, lane, V7X_LANES), axis=-1, keepdims=True)
    gap = jnp.exp(l2 - l1)
    gate1 = 1.0 / (1.0 + gap)
    gate2 = gap * gate1
    m1 = lane == i1
    m2 = lane == i2
    chosen = jnp.where(m1 | m2, 1.0, 0.0)
    r = lax.broadcasted_iota(jnp.int32, (tm, tm), 0)
    c = lax.broadcasted_iota(jnp.int32, (tm, tm), 1)
    tri = jnp.where(c < r, 1.0, 0.0).astype(BF16)
    before = jnp.dot(tri, chosen.astype(BF16), preferred_element_type=F32) + run_sc[0:1, :]
    rank1 = jnp.sum(jnp.where(m1, before, 0.0), axis=-1, keepdims=True)
    rank2 = jnp.sum(jnp.where(m2, before, 0.0), axis=-1, keepdims=True)
    vals = (i1.astype(F32), i2.astype(F32), rank1, rank2, gate1, gate2)
    meta = jnp.zeros((tm, V7X_LANES), F32)
    for k, v in enumerate(vals):
        meta = jnp.where(lane == k, v, meta)
    meta_ref[...] = meta
    meta_t_ref[...] = meta.T[:V7X_SUBLANES]
    run_sc[...] = run_sc[...] + jnp.sum(chosen, axis=0, keepdims=True)
    cnt_ref[...] = run_sc[...]


def mix_route(parts, w_out, x, g, modtab, cond, wr_pad, br_pad):
    tokens = x.shape[0]
    tm = TM_ROUTE
    kb = MIX_SLAB
    zero = lambda i: 0
    const = lambda i: (0, 0)
    rows = lambda w: pl.BlockSpec((tm, w), lambda i: (i, 0))
    lhs_specs = [pl.BlockSpec((tm, kb), (lambda i, cbk=cbk: (i, cbk))) for _, cbk in parts]
    return pl.pallas_call(
        _route_kernel,
        out_shape=(jax.ShapeDtypeStruct((tokens, D), F32),
                   jax.ShapeDtypeStruct((tokens, D), F32),
                   jax.ShapeDtypeStruct((tokens, V7X_LANES), F32),
                   jax.ShapeDtypeStruct((V7X_SUBLANES, tokens), F32),
                   jax.ShapeDtypeStruct((V7X_SUBLANES, V7X_LANES), F32)),
        grid=(tokens // tm,),
        in_specs=lhs_specs + [
            pl.BlockSpec((len(parts) * kb, D), const, pipeline_mode=pl.Buffered(1)),
            rows(D),
            _mod_spec(2, cond, tm, D, zero),
            pl.BlockSpec((1, D), const),
            _mod_spec(3, cond, tm, D, zero),
            _mod_spec(4, cond, tm, D, zero),
            pl.BlockSpec((D, V7X_LANES), const),
            pl.BlockSpec((1, V7X_LANES), const)],
        out_specs=(rows(D), rows(D), rows(V7X_LANES),
                   pl.BlockSpec((V7X_SUBLANES, tm), lambda i: (0, i)),
                   pl.BlockSpec((V7X_SUBLANES, V7X_LANES), const)),
        scratch_shapes=[pltpu.VMEM((V7X_SUBLANES, V7X_LANES), F32), pltpu.VMEM((len(parts) * kb, D), BF16)],
        compiler_params=_cparams(("arbitrary",)),
        name="mix_route",
    )(*[a for a, _ in parts], w_out, x, modtab, g.reshape(1, D), modtab, modtab, wr_pad, br_pad)


def _row_copy(src_ref, src_row, dst_ref, dst_row, sem):
    return pltpu.make_async_copy(src_ref.at[pl.ds(src_row, 1)], dst_ref.at[pl.ds(dst_row, 1)], sem)


_PAD_BULK = (256, 128, 64, 32, 16, 8)


def _zero_fill(hs_ref, zero_sc, sem, pads_ref, n_tail_max, wait):
    tmr = zero_sc.shape[0]

    def copy(rows, dst):
        cp = pltpu.make_async_copy(zero_sc.at[pl.ds(0, rows)], hs_ref.at[pl.ds(dst, rows)], sem)
        cp.wait() if wait else cp.start()

    for e in range(N_EXPERTS):
        start, n = pads_ref[e], pads_ref[N_EXPERTS + e]
        head = jnp.minimum((-start) & (V7X_SUBLANES - 1), n)
        for r in range(V7X_SUBLANES - 1):
            @pl.when(r < head)
            def _():
                copy(1, start + r)
        body = start + head
        rem = n - head
        for k in _PAD_BULK:
            @pl.when((rem & k) != 0)
            def _():
                copy(k, pl.multiple_of(body + (rem & ~(2 * k - 1)), V7X_SUBLANES))
    tail_start, tail_tiles = pads_ref[2 * N_EXPERTS], pads_ref[2 * N_EXPERTS + 1]
    for t in range(n_tail_max):
        @pl.when(t < tail_tiles)
        def _():
            copy(tmr, pl.multiple_of(tail_start + t * tmr, tmr))


def _dispatch_kernel(pos_ref, pads_ref, ha_ref, hb_ref, hs_ref, zero_sc, sem, zsem, *, n_a, n_tail_max):
    tm = ha_ref.shape[0]
    n_tok = pos_ref.shape[0] // 2
    i = pl.program_id(0)
    base = i * tm

    @pl.when(i == 0)
    def _():
        zero_sc[...] = jnp.zeros_like(zero_sc)
        _zero_fill(hs_ref, zero_sc, zsem, pads_ref, n_tail_max, wait=False)

    def scatter(h_ref):
        def issue(r, carry):
            _row_copy(h_ref, r, hs_ref, pos_ref[base + r], sem).start(priority=0)
            _row_copy(h_ref, r, hs_ref, pos_ref[n_tok + base + r], sem).start(priority=1)
            return carry

        lax.fori_loop(0, tm, issue, 0, unroll=8)
        for _ in range(2):
            pltpu.make_async_copy(h_ref, hs_ref.at[pl.ds(0, tm)], sem).wait()

    @pl.when(i < n_a)
    def _():
        scatter(ha_ref)

    @pl.when(i >= n_a)
    def _():
        scatter(hb_ref)

    @pl.when(i == 0)
    def _():
        _zero_fill(hs_ref, zero_sc, zsem, pads_ref, n_tail_max, wait=True)


def moe_dispatch(pos, pads, hs_rows, h_a, h_b):
    tm = TM_ROUTE
    n_a, n_b = h_a.shape[0] // tm, h_b.shape[0] // tm
    n_tail_max = hs_rows // TM_EXPERT - (2 * (h_a.shape[0] + h_b.shape[0])) // TM_EXPERT
    return pl.pallas_call(
        functools.partial(_dispatch_kernel, n_a=n_a, n_tail_max=n_tail_max),
        out_shape=jax.ShapeDtypeStruct((hs_rows, D), F32),
        grid_spec=pltpu.PrefetchScalarGridSpec(
            num_scalar_prefetch=2,
            grid=(n_a + n_b,),
            in_specs=[pl.BlockSpec((tm, D), lambda i, *pf: (jnp.minimum(i, n_a - 1), 0)),
                      pl.BlockSpec((tm, D), lambda i, *pf: (jnp.clip(i - n_a, 0, n_b - 1), 0))],
            out_specs=pl.BlockSpec(memory_space=pl.ANY),
            scratch_shapes=[pltpu.VMEM((TM_EXPERT, D), F32), pltpu.SemaphoreType.DMA(()),
                            pltpu.SemaphoreType.DMA(())]),
        compiler_params=_cparams(("arbitrary",)),
        name="moe_dispatch",
    )(pos, pads, h_a, h_b)


def _experts_kernel(te_ref, sg_ref, su_ref, sd_ref, nv_ref, hs_ref, wg_ref, wu_ref, wd_ref, y_ref,
                    wg_sc, wu_sc, wd_sc):
    del sg_ref, su_ref, sd_ref
    j = pl.program_id(0)
    e = te_ref[j]
    e_prev = te_ref[jnp.maximum(j - 1, 0)]
    n_valid = nv_ref[j]
    half = y_ref.shape[0] // 2

    @pl.when((j == 0) | (e != e_prev))
    def _():
        wg_sc[...] = wg_ref[...].astype(BF16)
        wu_sc[...] = wu_ref[...].astype(BF16)
        wd_sc[...] = wd_ref[...].astype(BF16)

    def swiglu(rows):
        h = hs_ref[rows, :].astype(BF16)
        y = None
        for c0 in range(0, D_FF_EXPERT, MOE_CHUNK):
            c1 = min(c0 + MOE_CHUNK, D_FF_EXPERT)
            hg = jnp.dot(h, wg_sc[:, c0:c1], preferred_element_type=F32)
            hu = jnp.dot(h, wu_sc[:, c0:c1], preferred_element_type=F32)
            act = (_silu(hg) * hu).astype(BF16)
            yc = jnp.dot(act, wd_sc[c0:c1, :], preferred_element_type=F32)
            y = yc if y is None else y + yc
        y_ref[rows, :] = y

    @pl.when(n_valid > half)
    def _():
        swiglu(slice(None))

    @pl.when((n_valid > 0) & (n_valid <= half))
    def _():
        swiglu(slice(0, half))
        y_ref[half:, :] = jnp.zeros((half, D), F32)

    @pl.when(n_valid == 0)
    def _():
        y_ref[...] = jnp.zeros_like(y_ref)


def moe_experts(tile_expert, stages, tile_valid, hs, e_gate, e_up, e_down):
    rows = hs.shape[0]
    tmr = TM_EXPERT
    wspec = lambda shape, k: pl.BlockSpec((None,) + shape, lambda j, *pf: (pf[1 + k][j], 0, 0))
    return pl.pallas_call(
        _experts_kernel,
        out_shape=jax.ShapeDtypeStruct((rows, D), F32),
        grid_spec=pltpu.PrefetchScalarGridSpec(
            num_scalar_prefetch=5,
            grid=(rows // tmr,),
            in_specs=[pl.BlockSpec((tmr, D), lambda j, *pf: (j, 0)),
                      wspec((D, D_FF_EXPERT), 0), wspec((D, D_FF_EXPERT), 1), wspec((D_FF_EXPERT, D), 2)],
            out_specs=pl.BlockSpec((tmr, D), lambda j, *pf: (j, 0)),
            scratch_shapes=[pltpu.VMEM((D, D_FF_EXPERT), BF16), pltpu.VMEM((D, D_FF_EXPERT), BF16),
                            pltpu.VMEM((D_FF_EXPERT, D), BF16)]),
        compiler_params=_cparams(("arbitrary",)),
        name="moe_experts",
    )(tile_expert, *stages, tile_valid, hs, e_gate, e_up, e_down)


def _combine_kernel(pos_ref, x_ref, meta_ref, gt_ref, fg_ref, y_ref, o_ref, b1_sc, b2_sc, sem):
    tm = x_ref.shape[0]
    n_tok = pos_ref.shape[0] // 2
    i = pl.program_id(0)

    def gather(tile, slot):
        base = tile * tm

        def issue(r, carry):
            _row_copy(y_ref, pos_ref[base + r], b1_sc.at[slot], r, sem.at[slot]).start(priority=0)
            _row_copy(y_ref, pos_ref[n_tok + base + r], b2_sc.at[slot], r, sem.at[slot]).start(priority=1)
            return carry

        lax.fori_loop(0, tm, issue, 0, unroll=8)

    @pl.when(i == 0)
    def _():
        gather(0, 0)

    @pl.when(i + 1 < pl.num_programs(0))
    def _():
        gather(i + 1, (i + 1) % 2)

    slot = i % 2
    pltpu.make_async_copy(y_ref.at[pl.ds(0, tm)], b1_sc.at[slot], sem.at[slot]).wait()
    pltpu.make_async_copy(y_ref.at[pl.ds(0, tm)], b2_sc.at[slot], sem.at[slot]).wait()

    meta = meta_ref[...]
    lane = lax.broadcasted_iota(jnp.int32, meta.shape, 1)
    g1 = jnp.sum(jnp.where(lane == META_G1, meta, 0.0), axis=-1, keepdims=True)
    g2 = jnp.sum(jnp.where(lane == META_G2, meta, 0.0), axis=-1, keepdims=True)
    x = x_ref[...] + gt_ref[...] * (g1 * b1_sc[slot] + g2 * b2_sc[slot])
    o_ref[...] = _rms(x, fg_ref[...])


def moe_combine(pos, x, meta, modtab, cond, final_g, y):
    tokens = x.shape[0]
    tm = TM_COMBINE
    return pl.pallas_call(
        _combine_kernel,
        out_shape=jax.ShapeDtypeStruct((tokens, D), F32),
        grid_spec=pltpu.PrefetchScalarGridSpec(
            num_scalar_prefetch=1,
            grid=(tokens // tm,),
            in_specs=[pl.BlockSpec((tm, D), lambda i, pos: (i, 0)),
                      pl.BlockSpec((tm, V7X_LANES), lambda i, pos: (i, 0)),
                      _mod_spec(5, cond, tm, D, lambda i, pos: 0),
                      pl.BlockSpec((1, D), lambda i, pos: (0, 0)),
                      pl.BlockSpec(memory_space=pl.ANY)],
            out_specs=pl.BlockSpec((tm, D), lambda i, pos: (i, 0)),
            scratch_shapes=[pltpu.VMEM((2, tm, D), F32), pltpu.VMEM((2, tm, D), F32),
                            pltpu.SemaphoreType.DMA((2,))]),
        compiler_params=_cparams(("arbitrary",)),
        name="moe_combine",
    )(pos, x, meta, modtab, final_g.reshape(1, D), y)


def moe_plan(metas, counts):
    tmr = TM_EXPERT
    cnts = [c[0, :N_EXPERTS].astype(jnp.int32) for c in counts]
    total = functools.reduce(jnp.add, cnts)
    padded = ((total + tmr - 1) // tmr) * tmr
    ends = jnp.cumsum(padded)
    starts = ends - padded
    n_rows = sum(m.shape[1] for m in metas) * 2 + N_EXPERTS * tmr
    n_tiles = n_rows // tmr
    tile_start = jnp.arange(n_tiles, dtype=jnp.int32) * tmr
    tile_expert = jnp.minimum(jnp.sum(tile_start[:, None] >= ends[None, :], axis=1), N_EXPERTS - 1).astype(jnp.int32)
    group_of_tile = jnp.sum(tile_start[:, None] >= ends[None, :], axis=1)
    real_end = jnp.sum(jnp.where(group_of_tile[:, None] == jnp.arange(N_EXPERTS)[None, :],
                                 (starts + total)[None, :], 0), axis=1)
    tile_valid = jnp.clip(real_end - tile_start, 0, tmr).astype(jnp.int32)
    eid = jnp.arange(N_EXPERTS, dtype=jnp.int32)
    later = jnp.where((eid[None, :] > eid[:, None]) & (padded[None, :] > 0), eid[None, :], N_EXPERTS)
    nxt = jnp.min(later, axis=1)
    next_used = jnp.where(nxt == N_EXPERTS, eid, nxt)
    pick = lambda table: jnp.sum(jnp.where(tile_expert[:, None] == eid[None, :], table[None, :], 0), axis=1)
    k_in_group = (tile_start - pick(starts)) // tmr
    tile_next = pick(next_used)
    stages = [jnp.where(k_in_group < k, tile_expert, tile_next).astype(jnp.int32) for k in (1, 2, 3)]
    pos, p1s, p2s = [], [], []
    base = jnp.zeros((N_EXPERTS,), jnp.int32)
    for m, c in zip(metas, cnts):
        first = starts + base
        sel = lambda field: m[field].astype(jnp.int32)
        lookup = lambda e: jnp.sum(jnp.where(e[:, None] == jnp.arange(N_EXPERTS)[None, :], first[None, :], 0), axis=1)
        p1 = lookup(sel(META_E1)) + sel(META_R1)
        p2 = lookup(sel(META_E2)) + sel(META_R2)
        pos.append(jnp.concatenate([p1, p2]).astype(jnp.int32))
        p1s.append(p1)
        p2s.append(p2)
        base = base + c
    pos_all = jnp.concatenate(p1s + p2s).astype(jnp.int32)
    pads = jnp.concatenate([starts + total, padded - total,
                            jnp.stack([ends[-1], n_tiles - ends[-1] // tmr])]).astype(jnp.int32)
    return pos, pos_all, pads, tile_expert, stages, tile_valid, n_rows


def _pad_to(a, shape):
    return jnp.pad(a, [(0, t - s) for s, t in zip(a.shape, shape)])


def _regroup_chunks(a, cb):
    r = a.shape[0]
    return a.reshape(r, 3, HY_W // cb, cb).transpose(2, 0, 1, 3).reshape(HY_W // cb, r, 3 * cb)


def kernel(x_prompt, x_sample, state_l0_lru, cache_l1_ckv, cache_l1_krope, c, c_ctx, l0_norm1, l0_norm2, l0_w_mod, l0_b_mod, l0_w_in, l0_conv_a, l0_lru_conv_w, l0_lru_conv_b, l0_lru_wa, l0_lru_ba, l0_lru_wi, l0_lru_bi, l0_lru_lambda, l0_w_out, l0_ffn_gate, l0_ffn_up, l0_ffn_down, l1_norm1, l1_norm2, l1_w_mod, l1_b_mod, l1_w_in, l1_q_norm, l1_kv_norm, l1_w_q_up, l1_w_kv_up, l1_hy_short_w, l1_hy_short_b, l1_hy_f_w1, l1_hy_f_b1, l1_hy_f_w2, l1_hy_f_b2, l1_hy_f_w3, l1_hy_bias, l1_w_out, l1_router_w, l1_router_b, l1_exp_gate, l1_exp_up, l1_exp_down, final_norm):
    batch, seq, _ = x_prompt.shape
    dec_batch, dec_seq, _ = x_sample.shape
    past_len = cache_l1_ckv.shape[1]

    cond8 = jnp.concatenate([c_ctx[None, :], c, jnp.zeros((V7X_SUBLANES - 1 - dec_batch, D), F32)], axis=0)
    wcat = jnp.concatenate([l0_lru_wa[0], l0_lru_wi[0], l0_lru_wa[1], l0_lru_wi[1]], axis=-1)
    hid = V7X_LANES
    w1p = _pad_to(l1_hy_f_w1, (hid, hid))
    b1p = _pad_to(l1_hy_f_b1.reshape(1, -1), (1, hid))
    w2p = _pad_to(l1_hy_f_w2, (hid, hid))
    b2p = _pad_to(l1_hy_f_b2.reshape(1, -1), (1, hid))
    w3p = _pad_to(l1_hy_f_w3, (hid, 2 * HY_W))
    short_w = _regroup_chunks(l1_hy_short_w, HY_CB)
    short_b = _regroup_chunks(l1_hy_short_b.reshape(1, -1), HY_CB)
    hy_bias = l1_hy_bias.reshape(1, HY_W)
    wr_pad = _pad_to(l1_router_w, (D, V7X_LANES))
    br_pad = _pad_to(l1_router_b.reshape(1, -1), (1, V7X_LANES))

    mod0, mod1 = adaln_tables(cond8, ((l0_w_mod, l0_b_mod), (l1_w_mod, l1_b_mod)))

    kv_ctx = kv_up(cache_l1_ckv.reshape(dec_batch * past_len, KV_RANK), l1_w_kv_up)
    kr_ctx = cache_l1_krope.reshape(dec_batch * past_len, ROPE)

    conds = ((0, batch * seq), (1, dec_seq))
    seq_lens = (seq, dec_seq)
    xs = (x_prompt.reshape(batch * seq, D), x_sample.reshape(dec_batch * dec_seq, D))
    h0s = (jnp.zeros((batch, 2, LRU_W), F32), state_l0_lru)

    us = in0_proj(xs, l0_norm1, mod0, conds, l0_w_in)
    parts, lru_states = [], []
    for u, seq_len, h0 in zip(us, seq_lens, h0s):
        ya = conv_a(u, seq_len, l0_conv_a)
        yb, lru_state = rglru(u, seq_len, l0_lru_conv_w, l0_lru_conv_b, wcat, l0_lru_ba, l0_lru_bi,
                              l0_lru_lambda, h0)
        parts.append([(ya, 0), (yb, 0), (yb, 1)])
        lru_states.append(lru_state)
    xs = mix_ffn(parts, l0_w_out, xs, l0_norm2, mod0, conds, l0_ffn_gate, l0_ffn_up, l0_ffn_down)
    new_lru = lru_states[0]

    def layer1(x, seq_len, cond, latent):
        qnope, qpe, ckv, kr, kv, uh = in1_proj(x, l1_norm1, mod1, cond, l1_w_in, l1_q_norm, l1_kv_norm,
                                               l1_w_q_up, l1_w_kv_up)
        if latent:
            yc = attn_lat(qnope, qpe, kv_ctx, kr_ctx, kv, kr, seq_len, past_len)
        else:
            yc = attn_ctx(qnope, qpe, kv, kr, seq_len)
        cs = dft_tables(seq_len)
        k_r, k_s, k_ny = hy_filter(cs, w1p, b1p, w2p, b2p, w3p)
        yd = hyena(uh, seq_len, short_w, short_b, cs, k_r, k_s, k_ny, hy_bias)
        routed = mix_route([(yc, 0), (yc, 1), (yd, 0)], l1_w_out, x, l1_norm2, mod1, cond, wr_pad, br_pad)
        return routed, ckv, kr

    r_p, new_ckv, new_kr = layer1(xs[0], seq, conds[0], latent=False)
    r_s, _, _ = layer1(xs[1], dec_seq, conds[1], latent=True)

    routed = (r_p, r_s)
    pos, pos_all, pads, tile_expert, stages, tile_valid, n_rows = moe_plan([r[3] for r in routed],
                                                                          [r[4] for r in routed])
    hs = moe_dispatch(pos_all, pads, n_rows, r_p[1], r_s[1])
    y_rows = moe_experts(tile_expert, stages, tile_valid, hs, l1_exp_gate, l1_exp_up, l1_exp_down)
    y_p, y_s = [moe_combine(p, r[0], r[2], mod1, cond, final_norm, y_rows)
                for p, r, cond in zip(pos, routed, conds)]
    return (y_p.reshape(batch, seq, D), y_s.reshape(dec_batch, dec_seq, D), new_lru,
            new_ckv.reshape(batch, seq, KV_RANK), new_kr.reshape(batch, seq, ROPE))
```

```python
import functools
import math

import jax
import jax.numpy as jnp
from jax import lax
from jax.experimental import pallas as pl
from jax.experimental.pallas import tpu as pltpu

F32 = jnp.float32
BF16 = jnp.bfloat16

D = 1024
GRID_W = 64
EPS = 1e-6
CONV_W = 512
LRU_W = 1024
LRU_BW = 128
LRU_C = 8.0
MLA_HEADS = 8
Q_RANK = 384
KV_RANK = 256
NOPE = 128
ROPE = 64
VDIM = 128
QK_DIM = NOPE + ROPE
ROPE_THETA = 10000.0
HY_W = 512
HY_BANDS = 16
HY_TARGET = 1e-2
HY_FAST_DECAY = 0.3
HY_SLOW_DECAY = 1.5
D_FF = 2816
N_EXPERTS = 8
D_FF_EXPERT = 1408
IN0 = 3 * CONV_W + 2 * LRU_W
IN1 = Q_RANK + KV_RANK + ROPE + 3 * HY_W

V7X_LANES = 128
V7X_SUBLANES = 8
V7X_VMEM_LIMIT_BYTES = 56 * 1024 * 1024
V7X_VMEM_LIMIT_LARGE_BYTES = 60 * 1024 * 1024

TM = 512
TN_IN0 = 512
TF_FFN = 256
MIX_SLAB = 512
MOE_CHUNK = 256
TM_ROUTE = 512
TM_EXPERT = 512
TM_COMBINE = 512
LRU_CB = 512
HY_CB = 512
TQ = 256
ATTN_CTX_SEQS = 4
CONV_A_ROWS = 1024
LRU_ROWS = 1024
HY_ROWS = 1024
TM_IN1 = 512


def _cparams(sem, vmem_limit_bytes=V7X_VMEM_LIMIT_BYTES):
    return pltpu.CompilerParams(dimension_semantics=sem, vmem_limit_bytes=vmem_limit_bytes)


def _sigmoid(x):
    return 0.5 * jnp.tanh(0.5 * x) + 0.5


def _silu(x):
    return x * _sigmoid(x)


def _norm_mod(x, g, shift, scale):
    ms = jnp.mean(x * x, axis=-1, keepdims=True)
    y = x * lax.rsqrt(ms + EPS) * g
    return y * (1.0 + scale) + shift


def _mod_spec(comp, cond, tm, width, col_fn, tile_fn=lambda *ids: ids[0]):
    row0, seg = cond
    assert seg % tm == 0
    return pl.BlockSpec((None, 1, width),
                        lambda *ids: (comp * 3 + row0 + (tile_fn(*ids) * tm) // seg, 0, col_fn(*ids)))


def _dot3(a, b):
    a_hi = a.astype(BF16)
    a_lo = (a - a_hi.astype(F32)).astype(BF16)
    b_hi = b.astype(BF16)
    b_lo = (b - b_hi.astype(F32)).astype(BF16)
    n = a.shape[0]
    y = jnp.dot(jnp.concatenate([a_hi, a_lo], axis=0), b_hi, preferred_element_type=F32)
    return y[:n] + y[n:] + jnp.dot(a_hi, b_lo, preferred_element_type=F32)


def _adaln_kernel(c_ref, w0_ref, b0_ref, w1_ref, b1_ref, o_ref):
    a = _silu(c_ref[...])
    for layer, (w_ref, b_ref) in enumerate(((w0_ref, b0_ref), (w1_ref, b1_ref))):
        @pl.when(pl.program_id(0) == layer)
        def _():
            o_ref[...] = _dot3(a, w_ref[...]) + b_ref[...]


def adaln_tables(cond8, mods):
    tn = 1536
    nj = 6 * D // tn
    (w0, b0), (w1, b1) = mods
    at0 = lambda l, j: (0, jnp.where(l == 0, j, nj - 1))
    at1 = lambda l, j: (0, jnp.where(l == 1, j, 0))
    m = pl.pallas_call(
        _adaln_kernel,
        out_shape=jax.ShapeDtypeStruct((2, V7X_SUBLANES, 6 * D), F32),
        grid=(2, nj),
        in_specs=[pl.BlockSpec((V7X_SUBLANES, D), lambda l, j: (0, 0)),
                  pl.BlockSpec((D, tn), at0), pl.BlockSpec((1, tn), at0),
                  pl.BlockSpec((D, tn), at1), pl.BlockSpec((1, tn), at1)],
        out_specs=pl.BlockSpec((None, V7X_SUBLANES, tn), lambda l, j: (l, 0, j)),
        compiler_params=_cparams(("arbitrary", "arbitrary")),
        name="adaln",
    )(cond8, w0, b0.reshape(1, 6 * D), w1, b1.reshape(1, 6 * D))
    return [m[l, :3].reshape(3, 6, D).transpose(1, 0, 2).reshape(18, 1, D) for l in range(2)]


def _tile_of(n_load):
    return lambda s: jnp.maximum(s - n_load, 0)


def _block_of(n_load):
    return lambda s: jnp.minimum(s, n_load - 1)


class _TwoSets:
    def __init__(self, n_load, tm, tokens, conds):
        self.n_load, self.tm, self.conds = n_load, tm, conds
        self.n_a, self.n_b = tokens[0] // tm, tokens[1] // tm
        self.steps = n_load + self.n_a + self.n_b

    def tile(self, s):
        return jnp.maximum(s - self.n_load, 0)

    def in_first(self, s):
        return s - self.n_load < self.n_a

    def idx_a(self, s):
        return jnp.minimum(self.tile(s), self.n_a - 1)

    def idx_b(self, s):
        return jnp.clip(self.tile(s) - self.n_a, 0, self.n_b - 1)

    def rows(self, width):
        return (pl.BlockSpec((self.tm, width), lambda s: (self.idx_a(s), 0)),
                pl.BlockSpec((self.tm, width), lambda s: (self.idx_b(s), 0)))

    def cols(self, width, col):
        return (pl.BlockSpec((self.tm, width), lambda s: (self.idx_a(s), col)),
                pl.BlockSpec((self.tm, width), lambda s: (self.idx_b(s), col)))

    def mod_spec(self, comp):
        (row_a, seg_a), (row_b, seg_b) = self.conds
        assert seg_a % self.tm == 0 and seg_b % self.tm == 0

        def row(s):
            return jnp.where(self.in_first(s), row_a + (self.idx_a(s) * self.tm) // seg_a,
                             row_b + (self.idx_b(s) * self.tm) // seg_b)

        return pl.BlockSpec((None, 1, D), lambda s: (comp * 3 + row(s), 0, 0))


def _in0_kernel(xa_ref, xb_ref, g_ref, sh_ref, sc_ref, w_ref, oa_ref, ob_ref, w_sc, *, n_a):
    s = pl.program_id(0)
    n_load = w_sc.shape[0]

    @pl.when(s < n_load)
    def _():
        w_sc[s] = w_ref[...].astype(BF16)

    @pl.when(s >= n_load)
    def _():
        first = s - n_load < n_a
        x = jnp.where(first, xa_ref[...], xb_ref[...])
        h = _norm_mod(x, g_ref[...], sh_ref[...], sc_ref[...]).astype(BF16)
        u = jnp.concatenate([jnp.dot(h, w_sc[j], preferred_element_type=F32).astype(BF16)
                             for j in range(n_load)], axis=1)

        @pl.when(first)
        def _():
            oa_ref[...] = u

        @pl.when(jnp.logical_not(first))
        def _():
            ob_ref[...] = u


def in0_proj(xs, g, modtab, conds, w_in):
    tn = TN_IN0
    n = w_in.shape[1]
    n_load = n // tn
    ts = _TwoSets(n_load, TM, [x.shape[0] for x in xs], conds)
    blk = _block_of(n_load)
    return pl.pallas_call(
        functools.partial(_in0_kernel, n_a=ts.n_a),
        out_shape=tuple(jax.ShapeDtypeStruct((x.shape[0], n), BF16) for x in xs),
        grid=(ts.steps,),
        in_specs=[*ts.rows(D),
                  pl.BlockSpec((1, D), lambda s: (0, 0)),
                  ts.mod_spec(0), ts.mod_spec(1),
                  pl.BlockSpec((D, tn), lambda s: (0, blk(s)))],
        out_specs=ts.rows(n),
        scratch_shapes=[pltpu.VMEM((n_load, D, tn), BF16)],
        compiler_params=_cparams(("arbitrary",)),
        name="in0_proj",
    )(*xs, g.reshape(1, D), modtab, modtab, w_in)


def _shift_rows(v, d, t, seq_len=None):
    n = v.shape[0]
    seq_len = n if seq_len is None else seq_len
    if d > 0:
        return jnp.where(t < d, 0.0, pltpu.roll(v, d, 0))
    return jnp.where(t >= seq_len + d, 0.0, pltpu.roll(v, n + d, 0))


def _conv_a_kernel(b_ref, c_ref, x_ref, w_ref, o_ref, *, seq_len):
    v = c_ref[...].astype(F32) * x_ref[...].astype(F32)
    t = lax.broadcasted_iota(jnp.int32, v.shape, 0) & (seq_len - 1)
    w = w_ref[...]
    y = w[0:1] * _shift_rows(v, 1, t, seq_len) + w[1:2] * v + w[2:3] * _shift_rows(v, -1, t, seq_len)
    o_ref[...] = (b_ref[...].astype(F32) * y).astype(o_ref.dtype)


def conv_a(u, seq_len, conv_w):
    tokens = u.shape[0]
    rows = max(seq_len, CONV_A_ROWS)
    assert seq_len & (seq_len - 1) == 0 and rows % seq_len == 0
    return pl.pallas_call(
        functools.partial(_conv_a_kernel, seq_len=seq_len),
        out_shape=jax.ShapeDtypeStruct((tokens, CONV_W), BF16),
        grid=(tokens // rows,),
        in_specs=[pl.BlockSpec((rows, CONV_W), lambda s: (s, 0)),
                  pl.BlockSpec((rows, CONV_W), lambda s: (s, 1)),
                  pl.BlockSpec((rows, CONV_W), lambda s: (s, 2)),
                  pl.BlockSpec((3, CONV_W), lambda s: (0, 0))],
        out_specs=pl.BlockSpec((rows, CONV_W), lambda s: (s, 0)),
        compiler_params=_cparams(("parallel",)),
        name="conv_a",
    )(u, u, u, conv_w)


def _group_scan(a_sc, b_sc, k, reverse):
    planes = a_sc.shape[1] // V7X_SUBLANES
    order = range(V7X_SUBLANES - 1, -1, -1) if reverse else range(V7X_SUBLANES)
    a_acc = b_acc = None
    for r in order:
        plane = (k, pl.ds(r, planes, stride=V7X_SUBLANES), slice(None))
        a_r, b_r = a_sc[plane], b_sc[plane]
        if a_acc is None:
            a_acc, b_acc = a_r, b_r
        else:
            b_acc = a_r * b_acc + b_r
            a_acc = a_r * a_acc
            a_sc[plane] = a_acc
            b_sc[plane] = b_acc


def _rglru_kernel(gate_ref, xb_ref, cw_ref, cb_ref, wcat_ref, ba_ref, bi_ref, lam_ref, h0_ref,
                  y_ref, st_ref, af_sc, bf_sc, ab_sc, bb_sc, hf_sc, hb_sc, *, seq_len):
    n, cb = xb_ref.shape
    n_seq = n // seq_len
    n_slab = cb // LRU_BW
    xb = xb_ref[...].astype(F32)
    t = lax.broadcasted_iota(jnp.int32, xb.shape, 0) & (seq_len - 1)
    cw = cw_ref[...]
    sh = lambda d: _shift_rows(xb, d, t, seq_len)
    xc = cb_ref[...] + cw[0:1] * sh(2) + cw[1:2] * sh(1) + cw[2:3] * xb + cw[3:4] * sh(-1)
    xcb = xc.astype(BF16)

    for k in range(n_slab):
        cols = slice(k * LRU_BW, (k + 1) * LRU_BW)
        gk = jnp.dot(xcb[:, cols], wcat_ref[k].astype(BF16), preferred_element_type=F32)
        for d, (a_sc, b_sc) in enumerate(((af_sc, bf_sc), (ab_sc, bb_sc))):
            ga = gk[:, (2 * d) * LRU_BW:(2 * d + 1) * LRU_BW]
            gi = gk[:, (2 * d + 1) * LRU_BW:(2 * d + 2) * LRU_BW]
            r = _sigmoid(ga + ba_ref[d:d + 1, cols])
            i = _sigmoid(gi + bi_ref[d:d + 1, cols])
            log_a = (-LRU_C * jax.nn.softplus(-lam_ref[d:d + 1, cols])) * r
            a = jnp.exp(log_a)
            m = 1.0 - a * a
            mult = m * lax.rsqrt(jnp.maximum(m, 1e-30))
            a_sc[k] = a
            b_sc[k] = mult * (i * xc[:, cols])
            _group_scan(a_sc, b_sc, k, reverse=(d == 1))

    ng = seq_len // V7X_SUBLANES
    bcast = lambda row: jnp.broadcast_to(row, (V7X_SUBLANES, LRU_BW))
    chains = [(q, k) for q in range(n_seq) for k in range(n_slab)]
    init = tuple((bcast(h0_ref[q, 0:1, k * LRU_BW:(k + 1) * LRU_BW]),
                  bcast(h0_ref[q, 1:2, k * LRU_BW:(k + 1) * LRU_BW])) for q, k in chains)

    def step(j, carry):
        out = []
        for (q, k), (hf_in, hb_in) in zip(chains, carry):
            rf = pl.ds(pl.multiple_of(q * seq_len + j * V7X_SUBLANES, V7X_SUBLANES), V7X_SUBLANES)
            rb = pl.ds(pl.multiple_of(q * seq_len + (ng - 1 - j) * V7X_SUBLANES, V7X_SUBLANES), V7X_SUBLANES)
            hf = af_sc[k, rf, :] * hf_in + bf_sc[k, rf, :]
            hb = ab_sc[k, rb, :] * hb_in + bb_sc[k, rb, :]
            hf_sc[k, rf, :] = hf
            hb_sc[k, rb, :] = hb
            out.append((bcast(hf[V7X_SUBLANES - 1:V7X_SUBLANES]), bcast(hb[0:1])))
        return tuple(out)

    final = lax.fori_loop(0, ng, step, init)
    for (q, k), (hf_last, hb_first) in zip(chains, final):
        st_ref[q, 0:1, k * LRU_BW:(k + 1) * LRU_BW] = hf_last[0:1]
        st_ref[q, 1:2, k * LRU_BW:(k + 1) * LRU_BW] = hb_first[0:1]

    gt = gate_ref[...].astype(F32)
    gelu = 0.5 * gt * (1.0 + jnp.tanh(math.sqrt(2.0 / math.pi) * (gt + 0.044715 * (gt * gt * gt))))
    h = jnp.concatenate([hf_sc[k] + hb_sc[k] for k in range(n_slab)], axis=1)
    y_ref[...] = (h * gelu).astype(y_ref.dtype)


def rglru(u, seq_len, conv_w, conv_b, wcat, ba, bi, lam, h0):
    tokens = u.shape[0]
    nseq = tokens // seq_len
    cb = LRU_CB
    rows = max(seq_len, LRU_ROWS)
    assert seq_len & (seq_len - 1) == 0 and rows % seq_len == 0
    per_blk = rows // seq_len
    gate_blk0 = 3 * CONV_W // cb
    xb_blk0 = (3 * CONV_W + LRU_W) // cb
    seq_scr = lambda: pltpu.VMEM((cb // LRU_BW, rows, LRU_BW), F32)
    return pl.pallas_call(
        functools.partial(_rglru_kernel, seq_len=seq_len),
        out_shape=(jax.ShapeDtypeStruct((tokens, LRU_W), BF16), jax.ShapeDtypeStruct((nseq, 2, LRU_W), F32)),
        grid=(tokens // rows, LRU_W // cb),
        in_specs=[pl.BlockSpec((rows, cb), lambda s, c: (s, gate_blk0 + c)),
                  pl.BlockSpec((rows, cb), lambda s, c: (s, xb_blk0 + c)),
                  pl.BlockSpec((4, cb), lambda s, c: (0, c)),
                  pl.BlockSpec((1, cb), lambda s, c: (0, c)),
                  pl.BlockSpec((cb // LRU_BW, LRU_BW, 4 * LRU_BW), lambda s, c: (c, 0, 0)),
                  pl.BlockSpec((2, cb), lambda s, c: (0, c)),
                  pl.BlockSpec((2, cb), lambda s, c: (0, c)),
                  pl.BlockSpec((2, cb), lambda s, c: (0, c)),
                  pl.BlockSpec((per_blk, 2, cb), lambda s, c: (s, 0, c))],
        out_specs=(pl.BlockSpec((rows, cb), lambda s, c: (s, c)),
                   pl.BlockSpec((per_blk, 2, cb), lambda s, c: (s, 0, c))),
        scratch_shapes=[seq_scr() for _ in range(6)],
        compiler_params=_cparams(("parallel", "parallel")),
        name="rglru",
    )(u, u, conv_w, conv_b.reshape(1, LRU_W), wcat, ba, bi, lam, h0)


def _mix_ffn_kernel(p0a_ref, p0b_ref, p1a_ref, p1b_ref, p2a_ref, p2b_ref, wo_ref, xa_ref, xb_ref,
                    g1_ref, g_ref, sh_ref, sc_ref, g2_ref, wg_ref, wu_ref, wd_ref, oa_ref, ob_ref,
                    wo_sc, wg_sc, wu_sc, wd_sc, *, n_a):
    s = pl.program_id(0)
    n_load = wg_sc.shape[0]
    n_out = wo_sc.shape[0]

    @pl.when(s < n_out)
    def _():
        wo_sc[s] = wo_ref[...].astype(BF16)

    @pl.when(s < n_load)
    def _():
        wg_sc[s] = wg_ref[...].astype(BF16)
        wu_sc[s] = wu_ref[...].astype(BF16)
        wd_sc[s] = wd_ref[...].astype(BF16)

    @pl.when(s >= n_load)
    def _():
        first = s - n_load < n_a
        pick = lambda a_ref, b_ref: jnp.where(first, a_ref[...], b_ref[...])
        m = jnp.dot(pick(p0a_ref, p0b_ref), wo_sc[0], preferred_element_type=F32)
        m += jnp.dot(pick(p1a_ref, p1b_ref), wo_sc[1], preferred_element_type=F32)
        m += jnp.dot(pick(p2a_ref, p2b_ref), wo_sc[2], preferred_element_type=F32)
        x = pick(xa_ref, xb_ref) + g1_ref[...] * m
        h = _norm_mod(x, g_ref[...], sh_ref[...], sc_ref[...]).astype(BF16)
        y = None
        for f in range(n_load):
            hg = jnp.dot(h, wg_sc[f], preferred_element_type=F32)
            hu = jnp.dot(h, wu_sc[f], preferred_element_type=F32)
            act = (_silu(hg) * hu).astype(BF16)
            yf = jnp.dot(act, wd_sc[f], preferred_element_type=F32)
            y = yf if y is None else y + yf
        out = x + g2_ref[...] * y

        @pl.when(first)
        def _():
            oa_ref[...] = out

        @pl.when(jnp.logical_not(first))
        def _():
            ob_ref[...] = out


def mix_ffn(parts, w_out, xs, g, modtab, conds, w_gate, w_up, w_down):
    tf = TF_FFN
    kb = MIX_SLAB
    n_load = D_FF // tf
    n_out = len(parts[0])
    assert n_out <= n_load
    ts = _TwoSets(n_load, TM, [x.shape[0] for x in xs], conds)
    blk = _block_of(n_load)
    oblk = _block_of(n_out)
    lhs_specs, lhs_args = [], []
    for (arr_a, col_a), (arr_b, col_b) in zip(*parts):
        assert col_a == col_b
        lhs_specs += ts.cols(kb, col_a)
        lhs_args += [arr_a, arr_b]
    return pl.pallas_call(
        functools.partial(_mix_ffn_kernel, n_a=ts.n_a),
        out_shape=tuple(jax.ShapeDtypeStruct(x.shape, F32) for x in xs),
        grid=(ts.steps,),
        in_specs=lhs_specs + [
            pl.BlockSpec((kb, D), lambda s: (oblk(s), 0)),
            *ts.rows(D),
            ts.mod_spec(2),
            pl.BlockSpec((1, D), lambda s: (0, 0)),
            ts.mod_spec(3), ts.mod_spec(4), ts.mod_spec(5),
            pl.BlockSpec((D, tf), lambda s: (0, blk(s))),
            pl.BlockSpec((D, tf), lambda s: (0, blk(s))),
            pl.BlockSpec((tf, D), lambda s: (blk(s), 0))],
        out_specs=ts.rows(D),
        scratch_shapes=[pltpu.VMEM((n_out, kb, D), BF16),
                        pltpu.VMEM((n_load, D, tf), BF16), pltpu.VMEM((n_load, D, tf), BF16),
                        pltpu.VMEM((n_load, tf, D), BF16)],
        compiler_params=_cparams(("arbitrary",), V7X_VMEM_LIMIT_LARGE_BYTES),
        name="mix_ffn",
    )(*lhs_args, w_out, *xs, modtab, g.reshape(1, D), modtab, modtab, modtab, w_gate, w_up, w_down)


def _rms(x, g):
    return x * lax.rsqrt(jnp.mean(x * x, axis=-1, keepdims=True) + EPS) * g


def _in1_kernel(x_ref, g_ref, sh_ref, sc_ref, w_ref, qn_ref, kvn_ref, wq_ref, wkv_ref,
                qnope_ref, qpe_ref, ckv_ref, kr_ref, kv_ref, uh_ref, w_sc, wq_sc, wkv_sc):
    @pl.when(pl.program_id(0) == 0)
    def _():
        w_sc[...] = w_ref[...].astype(BF16)
        for h in range(MLA_HEADS):
            c0 = h * QK_DIM
            wq_sc[:, h * NOPE:(h + 1) * NOPE] = wq_ref[:, c0:c0 + NOPE].astype(BF16)
            r0 = MLA_HEADS * NOPE + h * ROPE
            wq_sc[:, r0:r0 + ROPE] = wq_ref[:, c0 + NOPE:c0 + QK_DIM].astype(BF16)
        wkv_sc[...] = wkv_ref[...].astype(BF16)

    h = _norm_mod(x_ref[...], g_ref[...], sh_ref[...], sc_ref[...]).astype(BF16)
    u = lax.dot_general(h, w_sc[...], (((1,), (1,)), ((), ())), preferred_element_type=F32)
    o1, o2, o3 = Q_RANK, Q_RANK + KV_RANK, Q_RANK + KV_RANK + ROPE
    cq = _rms(u[:, :o1], qn_ref[...])
    q = jnp.dot(cq.astype(BF16), wq_sc[...], preferred_element_type=F32) * _SCALE
    qnope_ref[...] = q[:, :MLA_HEADS * NOPE].astype(qnope_ref.dtype)
    qpe_ref[...] = q[:, MLA_HEADS * NOPE:]
    ckv = _rms(u[:, o1:o2], kvn_ref[...])
    ckv_ref[...] = ckv
    kv_ref[...] = jnp.dot(ckv.astype(BF16), wkv_sc[...], preferred_element_type=F32).astype(kv_ref.dtype)
    kr_ref[...] = u[:, o2:o3]
    uh_ref[...] = u[:, o3:]


def in1_proj(x, g, modtab, cond, w_in, q_norm, kv_norm, w_q_up, w_kv_up):
    tokens = x.shape[0]
    tm = TM_IN1
    nkv = MLA_HEADS * (NOPE + VDIM)
    const = lambda i: (0, 0)
    zero = lambda i: 0
    once = pl.Buffered(1)
    outs = (jax.ShapeDtypeStruct((tokens, MLA_HEADS * NOPE), BF16),
            jax.ShapeDtypeStruct((tokens, MLA_HEADS * ROPE), F32),
            jax.ShapeDtypeStruct((tokens, KV_RANK), F32),
            jax.ShapeDtypeStruct((tokens, ROPE), F32),
            jax.ShapeDtypeStruct((tokens, nkv), BF16),
            jax.ShapeDtypeStruct((tokens, 3 * HY_W), F32))
    row = lambda w: pl.BlockSpec((tm, w), lambda i: (i, 0))
    return pl.pallas_call(
        _in1_kernel,
        out_shape=outs,
        grid=(tokens // tm,),
        in_specs=[row(D),
                  pl.BlockSpec((1, D), const),
                  _mod_spec(0, cond, tm, D, zero),
                  _mod_spec(1, cond, tm, D, zero),
                  pl.BlockSpec((IN1, D), const, pipeline_mode=once),
                  pl.BlockSpec((1, Q_RANK), const),
                  pl.BlockSpec((1, KV_RANK), const),
                  pl.BlockSpec((Q_RANK, MLA_HEADS * QK_DIM), const, pipeline_mode=once),
                  pl.BlockSpec((KV_RANK, nkv), const, pipeline_mode=once)],
        out_specs=tuple(row(o.shape[1]) for o in outs),
        scratch_shapes=[pltpu.VMEM((IN1, D), BF16), pltpu.VMEM((Q_RANK, MLA_HEADS * QK_DIM), BF16),
                        pltpu.VMEM((KV_RANK, nkv), BF16)],
        compiler_params=_cparams(("arbitrary",)),
        name="in1_proj",
    )(x, g.reshape(1, D), modtab, modtab, w_in.T, q_norm.reshape(1, Q_RANK), kv_norm.reshape(1, KV_RANK),
      w_q_up, w_kv_up)


def _mm_kernel(a_ref, w_ref, o_ref):
    o_ref[...] = jnp.dot(a_ref[...].astype(BF16), w_ref[...].astype(BF16),
                         preferred_element_type=F32).astype(o_ref.dtype)


def kv_up(ckv, w_kv_up):
    rows = ckv.shape[0]
    n = w_kv_up.shape[1]
    return pl.pallas_call(
        _mm_kernel,
        out_shape=jax.ShapeDtypeStruct((rows, n), BF16),
        grid=(rows // TM,),
        in_specs=[pl.BlockSpec((TM, KV_RANK), lambda i: (i, 0)), pl.BlockSpec((KV_RANK, n), lambda i: (0, 0))],
        out_specs=pl.BlockSpec((TM, n), lambda i: (i, 0)),
        compiler_params=_cparams(("parallel",)),
        name="kv_up",
    )(ckv, w_kv_up)


_NT = (((1,), (1,)), ((), ()))
_SCALE = 1.0 / math.sqrt(QK_DIM)


def _fill_rope_tables(cos_ref, sin_ref):
    n, width = cos_ref.shape
    n_grid_rows = n // GRID_W
    n_freq = ROPE // 4

    def trig(count):
        lane = lax.broadcasted_iota(jnp.int32, (count, width), 1)
        j = lane & (ROPE // 2 - 1)
        inv = jnp.exp((j & (n_freq - 1)).astype(F32) * (-math.log(ROPE_THETA) / n_freq))
        ang = lax.broadcasted_iota(jnp.int32, (count, width), 0).astype(F32) * inv
        return jnp.cos(ang), jnp.sin(ang), j < n_freq

    cos_c, sin_c, by_row = trig(GRID_W)
    cos_r, sin_r, _ = trig(n_grid_rows)
    for r in range(n_grid_rows):
        rows = slice(r * GRID_W, (r + 1) * GRID_W)
        cos_ref[rows, :] = jnp.where(by_row, jnp.broadcast_to(cos_r[r:r + 1], cos_c.shape), cos_c)
        sin_ref[rows, :] = jnp.where(by_row, jnp.broadcast_to(sin_r[r:r + 1], sin_c.shape), sin_c)


def _rope(x, cos, sin):
    width = x.shape[1]
    lane = lax.broadcasted_iota(jnp.int32, x.shape, 1)
    first_half = (lane & (ROPE - 1)) < ROPE // 2
    xr = jnp.where(first_half, -pltpu.roll(x, width - ROPE // 2, 1), pltpu.roll(x, ROPE // 2, 1))
    return x * cos + xr * sin


def _ones_column(n):
    lane = lax.broadcasted_iota(jnp.int32, (n, VDIM), 1)
    return jnp.where(lane == 0, 1.0, 0.0).astype(BF16)


def _head_attention(qcat, kcat, vaug):
    s = lax.dot_general(qcat, kcat, _NT, preferred_element_type=F32)
    p = jnp.exp(s - jnp.max(s, axis=-1, keepdims=True)).astype(BF16)
    oa = jnp.dot(p, vaug, preferred_element_type=F32)
    return oa[:, :VDIM] / oa[:, VDIM:VDIM + 1]


def _attn_ctx_kernel(qn_ref, qpe_ref, kv_ref, kr_ref, o_ref, *, seq_len):
    n = qn_ref.shape[0]
    n_seq = n // seq_len
    ones = _ones_column(n)
    kpe = kr_ref[...].astype(BF16)
    per_seq = lambda a: a.reshape(n_seq, seq_len, a.shape[-1])
    for h in range(MLA_HEADS):
        c0 = h * (NOPE + VDIM)
        qcat = per_seq(jnp.concatenate([qn_ref[:, h * NOPE:(h + 1) * NOPE],
                                        qpe_ref[:, h * ROPE:(h + 1) * ROPE].astype(BF16)], axis=1))
        kcat = per_seq(jnp.concatenate([kv_ref[:, c0:c0 + NOPE], kpe], axis=1))
        vaug = per_seq(jnp.concatenate([kv_ref[:, c0 + NOPE:c0 + NOPE + VDIM], ones], axis=1))
        s = jnp.einsum("bqd,bkd->bqk", qcat, kcat, preferred_element_type=F32)
        p = jnp.exp(s - jnp.max(s, axis=-1, keepdims=True)).astype(BF16)
        oa = jnp.einsum("bqk,bkd->bqd", p, vaug, preferred_element_type=F32)
        o = oa[:, :, :VDIM] / oa[:, :, VDIM:VDIM + 1]
        o_ref[:, h * VDIM:(h + 1) * VDIM] = o.reshape(n, VDIM).astype(o_ref.dtype)


def attn_ctx(qnope, qpe, kv, kr, seq_len):
    tokens = qnope.shape[0]
    rows = ATTN_CTX_SEQS * seq_len
    blk = lambda w: pl.BlockSpec((rows, w), lambda s: (s, 0))
    return pl.pallas_call(
        functools.partial(_attn_ctx_kernel, seq_len=seq_len),
        out_shape=jax.ShapeDtypeStruct((tokens, MLA_HEADS * VDIM), BF16),
        grid=(tokens // rows,),
        in_specs=[blk(MLA_HEADS * NOPE), blk(MLA_HEADS * ROPE), blk(MLA_HEADS * (NOPE + VDIM)), blk(ROPE)],
        out_specs=blk(MLA_HEADS * VDIM),
        compiler_params=_cparams(("parallel",)),
        name="attn_ctx",
    )(qnope, qpe, kv, kr)


def _attn_lat_kernel(qn_ref, qpe_ref, kvc_ref, krc_ref, kvl_ref, krl_ref, o_ref, kcat_sc, vaug_sc, cos_sc, sin_sc):
    tq = qn_ref.shape[0]
    n_ctx = krc_ref.shape[0]
    n_lat = krl_ref.shape[0]

    @pl.when(pl.program_id(1) == 0)
    def _():
        _fill_rope_tables(cos_sc, sin_sc)
        kr2 = jnp.concatenate([krl_ref[...], krl_ref[...]], axis=1)
        kpe_lat = _rope(kr2, cos_sc[...], sin_sc[...])[:, :ROPE].astype(BF16)
        kpe_ctx = krc_ref[...].astype(BF16)
        ones_c, ones_l = _ones_column(n_ctx), _ones_column(n_lat)
        for h in range(MLA_HEADS):
            c0 = h * (NOPE + VDIM)
            for r0, nr, kv_ref, kpe, ones in ((0, n_ctx, kvc_ref, kpe_ctx, ones_c), (n_ctx, n_lat, kvl_ref, kpe_lat, ones_l)):
                kcat_sc[h, r0:r0 + nr, 0:NOPE] = kv_ref[:, c0:c0 + NOPE]
                kcat_sc[h, r0:r0 + nr, NOPE:QK_DIM] = kpe
                vaug_sc[h, r0:r0 + nr, 0:VDIM] = kv_ref[:, c0 + NOPE:c0 + NOPE + VDIM]
                vaug_sc[h, r0:r0 + nr, VDIM:2 * VDIM] = ones

    q0 = pl.multiple_of(pl.program_id(1) * tq, tq)
    rep = lambda a: jnp.concatenate([a] * (MLA_HEADS // 2), axis=1)
    qp_all = _rope(qpe_ref[...], rep(cos_sc[pl.ds(q0, tq), :]), rep(sin_sc[pl.ds(q0, tq), :])).astype(BF16)
    for h in range(MLA_HEADS):
        qcat = jnp.concatenate([qn_ref[:, h * NOPE:(h + 1) * NOPE], qp_all[:, h * ROPE:(h + 1) * ROPE]], axis=1)
        o_ref[:, h * VDIM:(h + 1) * VDIM] = _head_attention(qcat, kcat_sc[h], vaug_sc[h]).astype(o_ref.dtype)


def attn_lat(qnope, qpe, kv_ctx, kr_ctx, kv_lat, kr_lat, seq_len, ctx_len):
    tokens = qnope.shape[0]
    nq = seq_len // TQ
    qblk = lambda w: pl.BlockSpec((TQ, w), lambda b, i: (b * nq + i, 0))
    seq = lambda n, w: pl.BlockSpec((n, w), lambda b, i: (b, 0))
    nkv = MLA_HEADS * (NOPE + VDIM)
    n_keys = ctx_len + seq_len
    return pl.pallas_call(
        _attn_lat_kernel,
        out_shape=jax.ShapeDtypeStruct((tokens, MLA_HEADS * VDIM), BF16),
        grid=(tokens // seq_len, nq),
        in_specs=[qblk(MLA_HEADS * NOPE), qblk(MLA_HEADS * ROPE), seq(ctx_len, nkv), seq(ctx_len, ROPE),
                  seq(seq_len, nkv), seq(seq_len, ROPE)],
        out_specs=qblk(MLA_HEADS * VDIM),
        scratch_shapes=[pltpu.VMEM((MLA_HEADS, n_keys, QK_DIM), BF16),
                        pltpu.VMEM((MLA_HEADS, n_keys, 2 * VDIM), BF16),
                        pltpu.VMEM((seq_len, 2 * ROPE), F32), pltpu.VMEM((seq_len, 2 * ROPE), F32)],
        compiler_params=_cparams(("parallel", "arbitrary")),
        name="attn_lat",
    )(qnope, qpe, kv_ctx, kr_ctx, kv_lat, kr_lat)


def _dft_kernel(o_ref):
    tr, n = o_ref.shape[1], o_ref.shape[2]
    nb = n // V7X_LANES
    f = pl.program_id(0) * tr + lax.broadcasted_iota(jnp.int32, (tr, V7X_LANES), 0)
    j = lax.broadcasted_iota(jnp.int32, (tr, V7X_LANES), 1)

    def cos_sin(m):
        ang = (m & (2 * n - 1)).astype(F32) * (math.pi / n)
        return jnp.cos(ang), jnp.sin(ang)

    cj, sj = cos_sin(f * j)
    cb, sb = cos_sin(f * (j * V7X_LANES))
    for b in range(nb):
        cbb, sbb = cb[:, b:b + 1], sb[:, b:b + 1]
        cols = slice(b * V7X_LANES, (b + 1) * V7X_LANES)
        o_ref[0, :, cols] = (cbb * cj - sbb * sj).astype(o_ref.dtype)
        o_ref[1, :, cols] = (sbb * cj + cbb * sj).astype(o_ref.dtype)


def dft_tables(n):
    tr = 128
    return pl.pallas_call(
        _dft_kernel,
        out_shape=jax.ShapeDtypeStruct((2, n, n), BF16),
        grid=(n // tr,),
        out_specs=pl.BlockSpec((2, tr, n), lambda i: (0, i, 0)),
        compiler_params=_cparams(("parallel",)),
        name="dft_tables",
    )()


def _split_dot(table, x):
    hi = x.astype(BF16)
    lo = (x - hi.astype(F32)).astype(BF16)
    return (jnp.dot(table, hi, preferred_element_type=F32) + jnp.dot(table, lo, preferred_element_type=F32))


def _hy_filter_kernel(cs_ref, pack_ref, w2_ref, w3_ref, kr_ref, ks_ref, kny_ref):
    n = cs_ref.shape[1]
    row = lax.broadcasted_iota(jnp.int32, (n, V7X_LANES), 0).astype(F32)
    lane = lax.broadcasted_iota(jnp.int32, (n, V7X_LANES), 1)
    t = row * (1.0 / (n - 1))
    w = (2.0 * math.pi) * row / n
    band = jnp.where(lane <= HY_BANDS, lane - 1, lane - 1 - HY_BANDS).astype(F32)
    freq = 1e-4 + band * ((HY_BANDS - 1 - 1e-4) / (HY_BANDS - 1))
    arg = jnp.where(lane <= HY_BANDS, freq * w + 0.5 * math.pi, -(freq * w))
    z = jnp.where(lane == 0, t, jnp.where(lane <= 2 * HY_BANDS, jnp.sin(arg), 0.0))
    hid = jnp.sin(_dot3(z, pack_ref[0:V7X_LANES, :]) + pack_ref[V7X_LANES:V7X_LANES + 1, :])
    hid = jnp.sin(_dot3(hid, w2_ref[...]) + pack_ref[V7X_LANES + 1:V7X_LANES + 2, :])
    hf = _dot3(hid, w3_ref[...])

    rowc = lax.broadcasted_iota(jnp.int32, (n, HY_W), 0)
    chan = lax.broadcasted_iota(jnp.int32, (n, HY_W), 1).astype(F32)
    max_decay = math.log(HY_TARGET) / HY_FAST_DECAY
    min_decay = math.log(HY_TARGET) / HY_SLOW_DECAY
    deltas = min_decay + chan * ((max_decay - min_decay) / (HY_W - 1))
    decay = jnp.exp(-(rowc.astype(F32) * (1.0 / (n - 1))) * jnp.abs(deltas))
    h_fwd = hf[:, :HY_W] * decay
    h_bwd = jnp.where(rowc == 0, 0.0, hf[:, HY_W:] * decay)
    norm = jnp.sum(jnp.abs(h_fwd) + jnp.abs(h_bwd), axis=0, keepdims=True)
    even = (h_fwd + h_bwd) / norm
    odd = (h_fwd - h_bwd) / norm
    cf = jnp.where(rowc == 0, 1.0, 2.0) * (1.0 / (2 * n))
    kr_ref[...] = cf * _split_dot(cs_ref[0], even)
    ks_ref[...] = cf * _split_dot(cs_ref[1], odd)
    sgn = jnp.where((rowc & 1) == 1, -1.0, 1.0)
    kny_ref[...] = jnp.sum(sgn * even, axis=0, keepdims=True) * (1.0 / (2 * n))


def hy_filter(cs, pack, w2, w3):
    n = cs.shape[1]
    full = lambda a: pl.BlockSpec(a.shape, lambda: (0,) * a.ndim)
    args = (cs, pack, w2, w3)
    return pl.pallas_call(
        _hy_filter_kernel,
        out_shape=(jax.ShapeDtypeStruct((n, HY_W), F32), jax.ShapeDtypeStruct((n, HY_W), F32),
                   jax.ShapeDtypeStruct((1, HY_W), F32)),
        in_specs=[full(a) for a in args],
        out_specs=(pl.BlockSpec((n, HY_W), lambda: (0, 0)), pl.BlockSpec((n, HY_W), lambda: (0, 0)),
                   pl.BlockSpec((1, HY_W), lambda: (0, 0))),
        compiler_params=pltpu.CompilerParams(vmem_limit_bytes=V7X_VMEM_LIMIT_BYTES),
        name="hy_filter",
    )(*args)


def _hyena_kernel(u0_ref, u1_ref, u2_ref, sw_ref, sb_ref, cs_ref, kr_ref, ks_ref, kny_ref, bias_ref, o_ref,
                  *, seq_len):
    n, cb = u0_ref.shape
    n_seq = n // seq_len
    t = lax.broadcasted_iota(jnp.int32, (n, cb), 0) & (seq_len - 1)

    def short_conv(u_ref, k):
        u = u_ref[...]
        w = sw_ref[:, k * cb:(k + 1) * cb]
        return (sb_ref[:, k * cb:(k + 1) * cb] + w[0:1] * _shift_rows(u, 1, t, seq_len) + w[1:2] * u
                + w[2:3] * _shift_rows(u, -1, t, seq_len))

    x0 = short_conv(u0_ref, 0)
    z = short_conv(u1_ref, 1) * short_conv(u2_ref, 2)
    wide = lambda a: jnp.concatenate([a[q * seq_len:(q + 1) * seq_len] for q in range(n_seq)], axis=1)
    rep = lambda a: jnp.concatenate([a] * n_seq, axis=1)
    zw = wide(z)
    zb = zw.astype(BF16)
    c, s = cs_ref[0], cs_ref[1]
    ur = jnp.dot(c, zb, preferred_element_type=F32)
    us = jnp.dot(s, zb, preferred_element_type=F32)
    sgn = jnp.where((lax.broadcasted_iota(jnp.int32, zw.shape, 0) & 1) == 1, -1.0, 1.0)
    uny = jnp.sum(sgn * zw, axis=0, keepdims=True)
    kr, ks = rep(kr_ref[...]), rep(ks_ref[...])
    yr = (ur * kr - us * ks).astype(BF16)
    ys = (ur * ks + us * kr).astype(BF16)
    yw = jnp.dot(c, yr, preferred_element_type=F32) + jnp.dot(s, ys, preferred_element_type=F32)
    yw = yw + sgn * (uny * rep(kny_ref[...]))
    y = jnp.concatenate([yw[:, q * cb:(q + 1) * cb] for q in range(n_seq)], axis=0)
    o_ref[...] = (x0 * (y + bias_ref[...] * z)).astype(o_ref.dtype)


def hyena(uh, seq_len, short_w, short_b, cs, kr, ks, kny, bias):
    tokens = uh.shape[0]
    cb = HY_CB
    nc = HY_W // cb
    rows = max(seq_len, HY_ROWS)
    assert seq_len & (seq_len - 1) == 0 and rows % seq_len == 0
    ublk = lambda k: pl.BlockSpec((rows, cb), lambda s, c: (s, k * nc + c))
    chan = lambda r: pl.BlockSpec((r, cb), lambda s, c: (0, c))
    return pl.pallas_call(
        functools.partial(_hyena_kernel, seq_len=seq_len),
        out_shape=jax.ShapeDtypeStruct((tokens, HY_W), BF16),
        grid=(tokens // rows, nc),
        in_specs=[ublk(0), ublk(1), ublk(2),
                  pl.BlockSpec((None, 3, 3 * cb), lambda s, c: (c, 0, 0)),
                  pl.BlockSpec((None, 1, 3 * cb), lambda s, c: (c, 0, 0)),
                  pl.BlockSpec((2, seq_len, seq_len), lambda s, c: (0, 0, 0)),
                  chan(seq_len), chan(seq_len), chan(1), chan(1)],
        out_specs=pl.BlockSpec((rows, cb), lambda s, c: (s, c)),
        compiler_params=_cparams(("parallel", "parallel")),
        name="hyena",
    )(uh, uh, uh, short_w, short_b, cs, kr, ks, kny, bias)


META_E1, META_E2, META_R1, META_R2, META_G1, META_G2 = range(6)


def _route_kernel(p0_ref, p1_ref, p2_ref, wo_ref, x_ref, g1_ref, g_ref, sh_ref, sc_ref, wr_ref,
                  x1_ref, h_ref, meta_ref, meta_t_ref, cnt_ref, run_sc, wo_sc):
    tm = x_ref.shape[0]
    lane = lax.broadcasted_iota(jnp.int32, (tm, V7X_LANES), 1)

    @pl.when(pl.program_id(0) == 0)
    def _():
        run_sc[...] = jnp.zeros_like(run_sc)
        wo_sc[...] = wo_ref[...].astype(BF16)

    kb = p0_ref.shape[1]
    m = jnp.dot(p0_ref[...], wo_sc[0:kb, :], preferred_element_type=F32)
    m += jnp.dot(p1_ref[...], wo_sc[kb:2 * kb, :], preferred_element_type=F32)
    m += jnp.dot(p2_ref[...], wo_sc[2 * kb:3 * kb, :], preferred_element_type=F32)
    x1 = x_ref[...] + g1_ref[...] * m
    x1_ref[...] = x1
    h = _norm_mod(x1, g_ref[...], sh_ref[...], sc_ref[...])
    h_ref[...] = h
    logits = _dot3(h, wr_ref[0:D, :]) + wr_ref[D:D + 1, :]
    lg = jnp.where(lane < N_EXPERTS, logits, -jnp.inf)
    l1 = jnp.max(lg, axis=-1, keepdims=True)
    i1 = jnp.min(jnp.where(lg == l1, lane, V7X_LANES), axis=-1, keepdims=True)
    rest = jnp.where(lane == i1, -jnp.inf, lg)
    l2 = jnp.max(rest, axis=-1, keepdims=True)
    i2 = jnp.min(jnp.where(rest == l2, lane, V7X_LANES), axis=-1, keepdims=True)
    gap = jnp.exp(l2 - l1)
    gate1 = 1.0 / (1.0 + gap)
    gate2 = gap * gate1
    m1 = lane == i1
    m2 = lane == i2
    chosen = jnp.where(m1 | m2, 1.0, 0.0)
    r = lax.broadcasted_iota(jnp.int32, (tm, tm), 0)
    c = lax.broadcasted_iota(jnp.int32, (tm, tm), 1)
    tri = jnp.where(c < r, 1.0, 0.0).astype(BF16)
    before = jnp.dot(tri, chosen.astype(BF16), preferred_element_type=F32) + run_sc[0:1, :]
    rank1 = jnp.sum(jnp.where(m1, before, 0.0), axis=-1, keepdims=True)
    rank2 = jnp.sum(jnp.where(m2, before, 0.0), axis=-1, keepdims=True)
    vals = (i1.astype(F32), i2.astype(F32), rank1, rank2, gate1, gate2)
    meta = jnp.zeros((tm, V7X_LANES), F32)
    for k, v in enumerate(vals):
        meta = jnp.where(lane == k, v, meta)
    meta_ref[...] = meta
    meta_t_ref[...] = meta.T[:V7X_SUBLANES]
    run_sc[...] = run_sc[...] + jnp.sum(chosen, axis=0, keepdims=True)
    cnt_ref[...] = run_sc[...]


def mix_route(parts, w_out, x, g, modtab, cond, router_pack):
    tokens = x.shape[0]
    tm = TM_ROUTE
    kb = MIX_SLAB
    zero = lambda i: 0
    const = lambda i: (0, 0)
    rows = lambda w: pl.BlockSpec((tm, w), lambda i: (i, 0))
    lhs_specs = [pl.BlockSpec((tm, kb), (lambda i, cbk=cbk: (i, cbk))) for _, cbk in parts]
    return pl.pallas_call(
        _route_kernel,
        out_shape=(jax.ShapeDtypeStruct((tokens, D), F32),
                   jax.ShapeDtypeStruct((tokens, D), F32),
                   jax.ShapeDtypeStruct((tokens, V7X_LANES), F32),
                   jax.ShapeDtypeStruct((V7X_SUBLANES, tokens), F32),
                   jax.ShapeDtypeStruct((V7X_SUBLANES, V7X_LANES), F32)),
        grid=(tokens // tm,),
        in_specs=lhs_specs + [
            pl.BlockSpec((len(parts) * kb, D), const, pipeline_mode=pl.Buffered(1)),
            rows(D),
            _mod_spec(2, cond, tm, D, zero),
            pl.BlockSpec((1, D), const),
            _mod_spec(3, cond, tm, D, zero),
            _mod_spec(4, cond, tm, D, zero),
            pl.BlockSpec((D + V7X_SUBLANES, V7X_LANES), const)],
        out_specs=(rows(D), rows(D), rows(V7X_LANES),
                   pl.BlockSpec((V7X_SUBLANES, tm), lambda i: (0, i)),
                   pl.BlockSpec((V7X_SUBLANES, V7X_LANES), const)),
        scratch_shapes=[pltpu.VMEM((V7X_SUBLANES, V7X_LANES), F32), pltpu.VMEM((len(parts) * kb, D), BF16)],
        compiler_params=_cparams(("arbitrary",)),
        name="mix_route",
    )(*[a for a, _ in parts], w_out, x, modtab, g.reshape(1, D), modtab, modtab, router_pack)


def _row_copy(src_ref, src_row, dst_ref, dst_row, sem):
    return pltpu.make_async_copy(src_ref.at[pl.ds(src_row, 1)], dst_ref.at[pl.ds(dst_row, 1)], sem)


_PAD_BULK = (256, 128, 64, 32, 16, 8)


def _zero_fill(hs_ref, zero_sc, sem, pads_ref, n_tail_max, wait):
    tmr = zero_sc.shape[0]

    def copy(rows, dst):
        cp = pltpu.make_async_copy(zero_sc.at[pl.ds(0, rows)], hs_ref.at[pl.ds(dst, rows)], sem)
        cp.wait() if wait else cp.start()

    for e in range(N_EXPERTS):
        start, n = pads_ref[e], pads_ref[N_EXPERTS + e]
        head = jnp.minimum((-start) & (V7X_SUBLANES - 1), n)
        for r in range(V7X_SUBLANES - 1):
            @pl.when(r < head)
            def _():
                copy(1, start + r)
        body = start + head
        rem = n - head
        for k in _PAD_BULK:
            @pl.when((rem & k) != 0)
            def _():
                copy(k, pl.multiple_of(body + (rem & ~(2 * k - 1)), V7X_SUBLANES))
    tail_start, tail_tiles = pads_ref[2 * N_EXPERTS], pads_ref[2 * N_EXPERTS + 1]
    for t in range(n_tail_max):
        @pl.when(t < tail_tiles)
        def _():
            copy(tmr, pl.multiple_of(tail_start + t * tmr, tmr))


def _dispatch_kernel(pos_ref, pads_ref, ha_ref, hb_ref, hs_ref, zero_sc, sem, zsem, *, n_a, n_tail_max):
    tm = ha_ref.shape[0]
    n_tok = pos_ref.shape[0] // 2
    i = pl.program_id(0)
    base = i * tm

    @pl.when(i == 0)
    def _():
        zero_sc[...] = jnp.zeros_like(zero_sc)
        _zero_fill(hs_ref, zero_sc, zsem, pads_ref, n_tail_max, wait=False)

    def scatter(h_ref):
        def issue(r, carry):
            _row_copy(h_ref, r, hs_ref, pos_ref[base + r], sem).start(priority=0)
            _row_copy(h_ref, r, hs_ref, pos_ref[n_tok + base + r], sem).start(priority=1)
            return carry

        lax.fori_loop(0, tm, issue, 0, unroll=8)
        for _ in range(2):
            pltpu.make_async_copy(h_ref, hs_ref.at[pl.ds(0, tm)], sem).wait()

    @pl.when(i < n_a)
    def _():
        scatter(ha_ref)

    @pl.when(i >= n_a)
    def _():
        scatter(hb_ref)

    @pl.when(i == 0)
    def _():
        _zero_fill(hs_ref, zero_sc, zsem, pads_ref, n_tail_max, wait=True)


def moe_dispatch(pos, pads, hs_rows, h_a, h_b):
    tm = TM_ROUTE
    n_a, n_b = h_a.shape[0] // tm, h_b.shape[0] // tm
    n_tail_max = hs_rows // TM_EXPERT - (2 * (h_a.shape[0] + h_b.shape[0])) // TM_EXPERT
    return pl.pallas_call(
        functools.partial(_dispatch_kernel, n_a=n_a, n_tail_max=n_tail_max),
        out_shape=jax.ShapeDtypeStruct((hs_rows, D), F32),
        grid_spec=pltpu.PrefetchScalarGridSpec(
            num_scalar_prefetch=2,
            grid=(n_a + n_b,),
            in_specs=[pl.BlockSpec((tm, D), lambda i, *pf: (jnp.minimum(i, n_a - 1), 0)),
                      pl.BlockSpec((tm, D), lambda i, *pf: (jnp.clip(i - n_a, 0, n_b - 1), 0))],
            out_specs=pl.BlockSpec(memory_space=pl.ANY),
            scratch_shapes=[pltpu.VMEM((TM_EXPERT, D), F32), pltpu.SemaphoreType.DMA(()),
                            pltpu.SemaphoreType.DMA(())]),
        compiler_params=_cparams(("arbitrary",)),
        name="moe_dispatch",
    )(pos, pads, h_a, h_b)


def _experts_kernel(te_ref, sg_ref, su_ref, sd_ref, nv_ref, hs_ref, wg_ref, wu_ref, wd_ref, y_ref,
                    wg_sc, wu_sc, wd_sc):
    del sg_ref, su_ref, sd_ref
    j = pl.program_id(0)
    e = te_ref[j]
    e_prev = te_ref[jnp.maximum(j - 1, 0)]
    n_valid = nv_ref[j]
    half = y_ref.shape[0] // 2

    @pl.when((j == 0) | (e != e_prev))
    def _():
        wg_sc[...] = wg_ref[...].astype(BF16)
        wu_sc[...] = wu_ref[...].astype(BF16)
        wd_sc[...] = wd_ref[...].astype(BF16)

    def swiglu(rows):
        h = hs_ref[rows, :].astype(BF16)
        y = None
        for c0 in range(0, D_FF_EXPERT, MOE_CHUNK):
            c1 = min(c0 + MOE_CHUNK, D_FF_EXPERT)
            hg = jnp.dot(h, wg_sc[:, c0:c1], preferred_element_type=F32)
            hu = jnp.dot(h, wu_sc[:, c0:c1], preferred_element_type=F32)
            act = (_silu(hg) * hu).astype(BF16)
            yc = jnp.dot(act, wd_sc[c0:c1, :], preferred_element_type=F32)
            y = yc if y is None else y + yc
        y_ref[rows, :] = y

    @pl.when(n_valid > half)
    def _():
        swiglu(slice(None))

    @pl.when((n_valid > 0) & (n_valid <= half))
    def _():
        swiglu(slice(0, half))
        y_ref[half:, :] = jnp.zeros((half, D), F32)

    @pl.when(n_valid == 0)
    def _():
        y_ref[...] = jnp.zeros_like(y_ref)


def moe_experts(tile_expert, stages, tile_valid, hs, e_gate, e_up, e_down):
    rows = hs.shape[0]
    tmr = TM_EXPERT
    wspec = lambda shape, k: pl.BlockSpec((None,) + shape, lambda j, *pf: (pf[1 + k][j], 0, 0))
    return pl.pallas_call(
        _experts_kernel,
        out_shape=jax.ShapeDtypeStruct((rows, D), F32),
        grid_spec=pltpu.PrefetchScalarGridSpec(
            num_scalar_prefetch=5,
            grid=(rows // tmr,),
            in_specs=[pl.BlockSpec((tmr, D), lambda j, *pf: (j, 0)),
                      wspec((D, D_FF_EXPERT), 0), wspec((D, D_FF_EXPERT), 1), wspec((D_FF_EXPERT, D), 2)],
            out_specs=pl.BlockSpec((tmr, D), lambda j, *pf: (j, 0)),
            scratch_shapes=[pltpu.VMEM((D, D_FF_EXPERT), BF16), pltpu.VMEM((D, D_FF_EXPERT), BF16),
                            pltpu.VMEM((D_FF_EXPERT, D), BF16)]),
        compiler_params=_cparams(("arbitrary",)),
        name="moe_experts",
    )(tile_expert, *stages, tile_valid, hs, e_gate, e_up, e_down)


def _combine_kernel(pos_ref, x_ref, meta_ref, gt_ref, fg_ref, y_ref, o_ref, b1_sc, b2_sc, sem):
    tm = x_ref.shape[0]
    n_tok = pos_ref.shape[0] // 2
    i = pl.program_id(0)

    def gather(tile, slot):
        base = tile * tm

        def issue(r, carry):
            _row_copy(y_ref, pos_ref[base + r], b1_sc.at[slot], r, sem.at[slot]).start(priority=0)
            _row_copy(y_ref, pos_ref[n_tok + base + r], b2_sc.at[slot], r, sem.at[slot]).start(priority=1)
            return carry

        lax.fori_loop(0, tm, issue, 0, unroll=8)

    @pl.when(i == 0)
    def _():
        gather(0, 0)

    @pl.when(i + 1 < pl.num_programs(0))
    def _():
        gather(i + 1, (i + 1) % 2)

    slot = i % 2
    pltpu.make_async_copy(y_ref.at[pl.ds(0, tm)], b1_sc.at[slot], sem.at[slot]).wait()
    pltpu.make_async_copy(y_ref.at[pl.ds(0, tm)], b2_sc.at[slot], sem.at[slot]).wait()

    meta = meta_ref[...]
    lane = lax.broadcasted_iota(jnp.int32, meta.shape, 1)
    g1 = jnp.sum(jnp.where(lane == META_G1, meta, 0.0), axis=-1, keepdims=True)
    g2 = jnp.sum(jnp.where(lane == META_G2, meta, 0.0), axis=-1, keepdims=True)
    x = x_ref[...] + gt_ref[...] * (g1 * b1_sc[slot] + g2 * b2_sc[slot])
    o_ref[...] = _rms(x, fg_ref[...])


def moe_combine(pos, x, meta, modtab, cond, final_g, y):
    tokens = x.shape[0]
    tm = TM_COMBINE
    return pl.pallas_call(
        _combine_kernel,
        out_shape=jax.ShapeDtypeStruct((tokens, D), F32),
        grid_spec=pltpu.PrefetchScalarGridSpec(
            num_scalar_prefetch=1,
            grid=(tokens // tm,),
            in_specs=[pl.BlockSpec((tm, D), lambda i, pos: (i, 0)),
                      pl.BlockSpec((tm, V7X_LANES), lambda i, pos: (i, 0)),
                      _mod_spec(5, cond, tm, D, lambda i, pos: 0),
                      pl.BlockSpec((1, D), lambda i, pos: (0, 0)),
                      pl.BlockSpec(memory_space=pl.ANY)],
            out_specs=pl.BlockSpec((tm, D), lambda i, pos: (i, 0)),
            scratch_shapes=[pltpu.VMEM((2, tm, D), F32), pltpu.VMEM((2, tm, D), F32),
                            pltpu.SemaphoreType.DMA((2,))]),
        compiler_params=_cparams(("arbitrary",)),
        name="moe_combine",
    )(pos, x, meta, modtab, final_g.reshape(1, D), y)


def moe_plan(metas, counts):
    tmr = TM_EXPERT
    cnts = [c[0, :N_EXPERTS].astype(jnp.int32) for c in counts]
    total = functools.reduce(jnp.add, cnts)
    padded = ((total + tmr - 1) // tmr) * tmr
    ends = jnp.cumsum(padded)
    starts = ends - padded
    n_rows = sum(m.shape[1] for m in metas) * 2 + N_EXPERTS * tmr
    n_tiles = n_rows // tmr
    tile_start = jnp.arange(n_tiles, dtype=jnp.int32) * tmr
    tile_expert = jnp.minimum(jnp.sum(tile_start[:, None] >= ends[None, :], axis=1), N_EXPERTS - 1).astype(jnp.int32)
    group_of_tile = jnp.sum(tile_start[:, None] >= ends[None, :], axis=1)
    real_end = jnp.sum(jnp.where(group_of_tile[:, None] == jnp.arange(N_EXPERTS)[None, :],
                                 (starts + total)[None, :], 0), axis=1)
    tile_valid = jnp.clip(real_end - tile_start, 0, tmr).astype(jnp.int32)
    eid = jnp.arange(N_EXPERTS, dtype=jnp.int32)
    later = jnp.where((eid[None, :] > eid[:, None]) & (padded[None, :] > 0), eid[None, :], N_EXPERTS)
    nxt = jnp.min(later, axis=1)
    next_used = jnp.where(nxt == N_EXPERTS, eid, nxt)
    pick = lambda table: jnp.sum(jnp.where(tile_expert[:, None] == eid[None, :], table[None, :], 0), axis=1)
    k_in_group = (tile_start - pick(starts)) // tmr
    tile_next = pick(next_used)
    stages = [jnp.where(k_in_group < k, tile_expert, tile_next).astype(jnp.int32) for k in (1, 2, 3)]
    pos, p1s, p2s = [], [], []
    base = jnp.zeros((N_EXPERTS,), jnp.int32)
    for m, c in zip(metas, cnts):
        first = starts + base
        sel = lambda field: m[field].astype(jnp.int32)
        lookup = lambda e: jnp.sum(jnp.where(e[:, None] == jnp.arange(N_EXPERTS)[None, :], first[None, :], 0), axis=1)
        p1 = lookup(sel(META_E1)) + sel(META_R1)
        p2 = lookup(sel(META_E2)) + sel(META_R2)
        pos.append(jnp.concatenate([p1, p2]).astype(jnp.int32))
        p1s.append(p1)
        p2s.append(p2)
        base = base + c
    pos_all = jnp.concatenate(p1s + p2s).astype(jnp.int32)
    pads = jnp.concatenate([starts + total, padded - total,
                            jnp.stack([ends[-1], n_tiles - ends[-1] // tmr])]).astype(jnp.int32)
    return pos, pos_all, pads, tile_expert, stages, tile_valid, n_rows


def _pad_to(a, shape):
    return jnp.pad(a, [(0, t - s) for s, t in zip(a.shape, shape)])


def _regroup_chunks(a, cb):
    r = a.shape[0]
    return a.reshape(r, 3, HY_W // cb, cb).transpose(2, 0, 1, 3).reshape(HY_W // cb, r, 3 * cb)


def kernel(x_prompt, x_sample, state_l0_lru, cache_l1_ckv, cache_l1_krope, c, c_ctx, l0_norm1, l0_norm2, l0_w_mod, l0_b_mod, l0_w_in, l0_conv_a, l0_lru_conv_w, l0_lru_conv_b, l0_lru_wa, l0_lru_ba, l0_lru_wi, l0_lru_bi, l0_lru_lambda, l0_w_out, l0_ffn_gate, l0_ffn_up, l0_ffn_down, l1_norm1, l1_norm2, l1_w_mod, l1_b_mod, l1_w_in, l1_q_norm, l1_kv_norm, l1_w_q_up, l1_w_kv_up, l1_hy_short_w, l1_hy_short_b, l1_hy_f_w1, l1_hy_f_b1, l1_hy_f_w2, l1_hy_f_b2, l1_hy_f_w3, l1_hy_bias, l1_w_out, l1_router_w, l1_router_b, l1_exp_gate, l1_exp_up, l1_exp_down, final_norm):
    batch, seq, _ = x_prompt.shape
    dec_batch, dec_seq, _ = x_sample.shape
    past_len = cache_l1_ckv.shape[1]

    cond8 = jnp.concatenate([c_ctx[None, :], c, jnp.zeros((V7X_SUBLANES - 1 - dec_batch, D), F32)], axis=0)
    wcat = jnp.concatenate([l0_lru_wa[0], l0_lru_wi[0], l0_lru_wa[1], l0_lru_wi[1]], axis=-1)
    hid = l1_hy_f_w2.shape[0]
    filter_pack = jnp.concatenate([_pad_to(l1_hy_f_w1, (V7X_LANES, hid)), l1_hy_f_b1[None, :], l1_hy_f_b2[None, :],
                                   jnp.zeros((V7X_SUBLANES - 2, hid), F32)], axis=0)
    short_w = _regroup_chunks(l1_hy_short_w, HY_CB)
    short_b = _regroup_chunks(l1_hy_short_b.reshape(1, -1), HY_CB)
    hy_bias = l1_hy_bias.reshape(1, HY_W)
    router_pack = _pad_to(jnp.concatenate([l1_router_w, l1_router_b[None, :]], axis=0),
                          (D + V7X_SUBLANES, V7X_LANES))

    mod0, mod1 = adaln_tables(cond8, ((l0_w_mod, l0_b_mod), (l1_w_mod, l1_b_mod)))

    kv_ctx = kv_up(cache_l1_ckv.reshape(dec_batch * past_len, KV_RANK), l1_w_kv_up)
    kr_ctx = cache_l1_krope.reshape(dec_batch * past_len, ROPE)

    conds = ((0, batch * seq), (1, dec_seq))
    seq_lens = (seq, dec_seq)
    xs = (x_prompt.reshape(batch * seq, D), x_sample.reshape(dec_batch * dec_seq, D))
    h0s = (jnp.zeros((batch, 2, LRU_W), F32), state_l0_lru)

    us = in0_proj(xs, l0_norm1, mod0, conds, l0_w_in)
    parts, lru_states = [], []
    for u, seq_len, h0 in zip(us, seq_lens, h0s):
        ya = conv_a(u, seq_len, l0_conv_a)
        yb, lru_state = rglru(u, seq_len, l0_lru_conv_w, l0_lru_conv_b, wcat, l0_lru_ba, l0_lru_bi,
                              l0_lru_lambda, h0)
        parts.append([(ya, 0), (yb, 0), (yb, 1)])
        lru_states.append(lru_state)
    xs = mix_ffn(parts, l0_w_out, xs, l0_norm2, mod0, conds, l0_ffn_gate, l0_ffn_up, l0_ffn_down)
    new_lru = lru_states[0]

    def layer1(x, seq_len, cond, latent):
        qnope, qpe, ckv, kr, kv, uh = in1_proj(x, l1_norm1, mod1, cond, l1_w_in, l1_q_norm, l1_kv_norm,
                                               l1_w_q_up, l1_w_kv_up)
        if latent:
            yc = attn_lat(qnope, qpe, kv_ctx, kr_ctx, kv, kr, seq_len, past_len)
        else:
            yc = attn_ctx(qnope, qpe, kv, kr, seq_len)
        cs = dft_tables(seq_len)
        k_r, k_s, k_ny = hy_filter(cs, filter_pack, l1_hy_f_w2, l1_hy_f_w3)
        yd = hyena(uh, seq_len, short_w, short_b, cs, k_r, k_s, k_ny, hy_bias)
        routed = mix_route([(yc, 0), (yc, 1), (yd, 0)], l1_w_out, x, l1_norm2, mod1, cond, router_pack)
        return routed, ckv, kr

    r_p, new_ckv, new_kr = layer1(xs[0], seq, conds[0], latent=False)
    r_s, _, _ = layer1(xs[1], dec_seq, conds[1], latent=True)

    routed = (r_p, r_s)
    pos, pos_all, pads, tile_expert, stages, tile_valid, n_rows = moe_plan([r[3] for r in routed],
                                                                          [r[4] for r in routed])
    hs = moe_dispatch(pos_all, pads, n_rows, r_p[1], r_s[1])
    y_rows = moe_experts(tile_expert, stages, tile_valid, hs, l1_exp_gate, l1_exp_up, l1_exp_down)
    y_p, y_s = [moe_combine(p, r[0], r[2], mod1, cond, final_norm, y_rows)
                for p, r, cond in zip(pos, routed, conds)]
    return (y_p.reshape(batch, seq, D), y_s.reshape(dec_batch, dec_seq, D), new_lru,
            new_ckv.reshape(batch, seq, KV_RANK), new_kr.reshape(batch, seq, ROPE))
```

```python
import functools
import math

import jax
import jax.numpy as jnp
from jax import lax
from jax.experimental import pallas as pl
from jax.experimental.pallas import tpu as pltpu

F32 = jnp.float32
BF16 = jnp.bfloat16

D = 1024
GRID_W = 64
EPS = 1e-6
CONV_W = 512
LRU_W = 1024
LRU_BW = 128
LRU_C = 8.0
MLA_HEADS = 8
Q_RANK = 384
KV_RANK = 256
NOPE = 128
ROPE = 64
VDIM = 128
QK_DIM = NOPE + ROPE
ROPE_THETA = 10000.0
HY_W = 512
HY_BANDS = 16
HY_TARGET = 1e-2
HY_FAST_DECAY = 0.3
HY_SLOW_DECAY = 1.5
D_FF = 2816
N_EXPERTS = 8
D_FF_EXPERT = 1408
IN0 = 3 * CONV_W + 2 * LRU_W
IN1 = Q_RANK + KV_RANK + ROPE + 3 * HY_W

V7X_LANES = 128
V7X_SUBLANES = 8
V7X_VMEM_LIMIT_BYTES = 56 * 1024 * 1024
V7X_VMEM_LIMIT_LARGE_BYTES = 60 * 1024 * 1024

TM = 512
TN_IN0 = 512
TF_FFN = 256
MIX_SLAB = 512
MOE_CHUNK = 256
TM_ROUTE = 512
TM_EXPERT = 512
TM_COMBINE = 512
LRU_CB = 512
HY_CB = 512
TQ = 256
ATTN_CTX_SEQS = 4
CONV_A_ROWS = 1024
LRU_ROWS = 1024
HY_ROWS = 1024
TM_IN1 = 512


def _cparams(sem, vmem_limit_bytes=V7X_VMEM_LIMIT_BYTES):
    return pltpu.CompilerParams(dimension_semantics=sem, vmem_limit_bytes=vmem_limit_bytes)


def _sigmoid(x):
    return 0.5 * jnp.tanh(0.5 * x) + 0.5


def _silu(x):
    return x * _sigmoid(x)


def _norm_mod(x, g, shift, scale):
    ms = jnp.mean(x * x, axis=-1, keepdims=True)
    y = x * lax.rsqrt(ms + EPS) * g
    return y * (1.0 + scale) + shift


def _mod_spec(comp, cond, tm, width, col_fn, tile_fn=lambda *ids: ids[0]):
    row0, seg = cond
    assert seg % tm == 0
    return pl.BlockSpec((None, 1, width),
                        lambda *ids: (comp * 3 + row0 + (tile_fn(*ids) * tm) // seg, 0, col_fn(*ids)))


def _dot3(a, b):
    a_hi = a.astype(BF16)
    a_lo = (a - a_hi.astype(F32)).astype(BF16)
    b_hi = b.astype(BF16)
    b_lo = (b - b_hi.astype(F32)).astype(BF16)
    n = a.shape[0]
    y = jnp.dot(jnp.concatenate([a_hi, a_lo], axis=0), b_hi, preferred_element_type=F32)
    return y[:n] + y[n:] + jnp.dot(a_hi, b_lo, preferred_element_type=F32)


def _adaln_kernel(c_ref, w0_ref, b0_ref, w1_ref, b1_ref, o_ref):
    a = _silu(c_ref[...])
    for layer, (w_ref, b_ref) in enumerate(((w0_ref, b0_ref), (w1_ref, b1_ref))):
        @pl.when(pl.program_id(0) == layer)
        def _():
            o_ref[...] = _dot3(a, w_ref[...]) + b_ref[...]


def adaln_tables(cond8, mods):
    tn = 1536
    nj = 6 * D // tn
    (w0, b0), (w1, b1) = mods
    at0 = lambda l, j: (0, jnp.where(l == 0, j, nj - 1))
    at1 = lambda l, j: (0, jnp.where(l == 1, j, 0))
    m = pl.pallas_call(
        _adaln_kernel,
        out_shape=jax.ShapeDtypeStruct((2, V7X_SUBLANES, 6 * D), F32),
        grid=(2, nj),
        in_specs=[pl.BlockSpec((V7X_SUBLANES, D), lambda l, j: (0, 0)),
                  pl.BlockSpec((D, tn), at0), pl.BlockSpec((1, tn), at0),
                  pl.BlockSpec((D, tn), at1), pl.BlockSpec((1, tn), at1)],
        out_specs=pl.BlockSpec((None, V7X_SUBLANES, tn), lambda l, j: (l, 0, j)),
        compiler_params=_cparams(("arbitrary", "arbitrary")),
        name="adaln",
    )(cond8, w0, b0.reshape(1, 6 * D), w1, b1.reshape(1, 6 * D))
    return [m[l, :3].reshape(3, 6, D).transpose(1, 0, 2).reshape(18, 1, D) for l in range(2)]


def _tile_of(n_load):
    return lambda s: jnp.maximum(s - n_load, 0)


def _block_of(n_load):
    return lambda s: jnp.minimum(s, n_load - 1)


class _TwoSets:
    def __init__(self, n_load, tm, tokens, conds):
        self.n_load, self.tm, self.conds = n_load, tm, conds
        self.n_a, self.n_b = tokens[0] // tm, tokens[1] // tm
        self.steps = n_load + self.n_a + self.n_b

    def tile(self, s):
        return jnp.maximum(s - self.n_load, 0)

    def in_first(self, s):
        return s - self.n_load < self.n_a

    def idx_a(self, s):
        return jnp.minimum(self.tile(s), self.n_a - 1)

    def idx_b(self, s):
        return jnp.clip(self.tile(s) - self.n_a, 0, self.n_b - 1)

    def rows(self, width):
        return (pl.BlockSpec((self.tm, width), lambda s: (self.idx_a(s), 0)),
                pl.BlockSpec((self.tm, width), lambda s: (self.idx_b(s), 0)))

    def cols(self, width, col):
        return (pl.BlockSpec((self.tm, width), lambda s: (self.idx_a(s), col)),
                pl.BlockSpec((self.tm, width), lambda s: (self.idx_b(s), col)))

    def mod_spec(self, comp):
        (row_a, seg_a), (row_b, seg_b) = self.conds
        assert seg_a % self.tm == 0 and seg_b % self.tm == 0

        def row(s):
            return jnp.where(self.in_first(s), row_a + (self.idx_a(s) * self.tm) // seg_a,
                             row_b + (self.idx_b(s) * self.tm) // seg_b)

        return pl.BlockSpec((None, 1, D), lambda s: (comp * 3 + row(s), 0, 0))


def _in0_kernel(xa_ref, xb_ref, g_ref, sh_ref, sc_ref, w_ref, oa_ref, ob_ref, w_sc, *, n_a):
    s = pl.program_id(0)
    n_load = w_sc.shape[0]

    @pl.when(s < n_load)
    def _():
        w_sc[s] = w_ref[...].astype(BF16)

    def tile(x_ref, o_ref):
        tn = w_sc.shape[2]
        h = _norm_mod(x_ref[...], g_ref[...], sh_ref[...], sc_ref[...]).astype(BF16)
        for j in range(n_load):
            o_ref[:, j * tn:(j + 1) * tn] = jnp.dot(h, w_sc[j], preferred_element_type=F32).astype(BF16)

    @pl.when(jnp.logical_and(s >= n_load, s - n_load < n_a))
    def _():
        tile(xa_ref, oa_ref)

    @pl.when(s - n_load >= n_a)
    def _():
        tile(xb_ref, ob_ref)


def in0_proj(xs, g, modtab, conds, w_in):
    tn = TN_IN0
    n = w_in.shape[1]
    n_load = n // tn
    ts = _TwoSets(n_load, TM, [x.shape[0] for x in xs], conds)
    blk = _block_of(n_load)
    return pl.pallas_call(
        functools.partial(_in0_kernel, n_a=ts.n_a),
        out_shape=tuple(jax.ShapeDtypeStruct((x.shape[0], n), BF16) for x in xs),
        grid=(ts.steps,),
        in_specs=[*ts.rows(D),
                  pl.BlockSpec((1, D), lambda s: (0, 0)),
                  ts.mod_spec(0), ts.mod_spec(1),
                  pl.BlockSpec((D, tn), lambda s: (0, blk(s)))],
        out_specs=ts.rows(n),
        scratch_shapes=[pltpu.VMEM((n_load, D, tn), BF16)],
        compiler_params=_cparams(("arbitrary",)),
        name="in0_proj",
    )(*xs, g.reshape(1, D), modtab, modtab, w_in)


def _shift_rows(v, d, t, seq_len=None):
    n = v.shape[0]
    seq_len = n if seq_len is None else seq_len
    if d > 0:
        return jnp.where(t < d, 0.0, pltpu.roll(v, d, 0))
    return jnp.where(t >= seq_len + d, 0.0, pltpu.roll(v, n + d, 0))


def _conv_a_kernel(b_ref, c_ref, x_ref, w_ref, o_ref, *, seq_len):
    v = c_ref[...].astype(F32) * x_ref[...].astype(F32)
    t = lax.broadcasted_iota(jnp.int32, v.shape, 0) & (seq_len - 1)
    w = w_ref[...]
    y = w[0:1] * _shift_rows(v, 1, t, seq_len) + w[1:2] * v + w[2:3] * _shift_rows(v, -1, t, seq_len)
    o_ref[...] = (b_ref[...].astype(F32) * y).astype(o_ref.dtype)


def conv_a(u, seq_len, conv_w):
    tokens = u.shape[0]
    rows = max(seq_len, CONV_A_ROWS)
    assert seq_len & (seq_len - 1) == 0 and rows % seq_len == 0
    return pl.pallas_call(
        functools.partial(_conv_a_kernel, seq_len=seq_len),
        out_shape=jax.ShapeDtypeStruct((tokens, CONV_W), BF16),
        grid=(tokens // rows,),
        in_specs=[pl.BlockSpec((rows, CONV_W), lambda s: (s, 0)),
                  pl.BlockSpec((rows, CONV_W), lambda s: (s, 1)),
                  pl.BlockSpec((rows, CONV_W), lambda s: (s, 2)),
                  pl.BlockSpec((3, CONV_W), lambda s: (0, 0))],
        out_specs=pl.BlockSpec((rows, CONV_W), lambda s: (s, 0)),
        compiler_params=_cparams(("parallel",)),
        name="conv_a",
    )(u, u, u, conv_w)


def _group_scan(a_sc, b_sc, k, reverse):
    planes = a_sc.shape[1] // V7X_SUBLANES
    order = range(V7X_SUBLANES - 1, -1, -1) if reverse else range(V7X_SUBLANES)
    a_acc = b_acc = None
    for r in order:
        plane = (k, pl.ds(r, planes, stride=V7X_SUBLANES), slice(None))
        a_r, b_r = a_sc[plane], b_sc[plane]
        if a_acc is None:
            a_acc, b_acc = a_r, b_r
        else:
            b_acc = a_r * b_acc + b_r
            a_acc = a_r * a_acc
            a_sc[plane] = a_acc
            b_sc[plane] = b_acc


def _rglru_kernel(gate_ref, xb_ref, cw_ref, cb_ref, wcat_ref, ba_ref, bi_ref, lam_ref, h0_ref,
                  y_ref, st_ref, af_sc, bf_sc, ab_sc, bb_sc, hf_sc, hb_sc, *, seq_len):
    n, cb = xb_ref.shape
    n_seq = n // seq_len
    n_slab = cb // LRU_BW
    xb = xb_ref[...].astype(F32)
    t = lax.broadcasted_iota(jnp.int32, xb.shape, 0) & (seq_len - 1)
    cw = cw_ref[...]
    sh = lambda d: _shift_rows(xb, d, t, seq_len)
    xc = cb_ref[...] + cw[0:1] * sh(2) + cw[1:2] * sh(1) + cw[2:3] * xb + cw[3:4] * sh(-1)
    xcb = xc.astype(BF16)

    for k in range(n_slab):
        cols = slice(k * LRU_BW, (k + 1) * LRU_BW)
        gk = jnp.dot(xcb[:, cols], wcat_ref[k].astype(BF16), preferred_element_type=F32)
        for d, (a_sc, b_sc) in enumerate(((af_sc, bf_sc), (ab_sc, bb_sc))):
            ga = gk[:, (2 * d) * LRU_BW:(2 * d + 1) * LRU_BW]
            gi = gk[:, (2 * d + 1) * LRU_BW:(2 * d + 2) * LRU_BW]
            r = _sigmoid(ga + ba_ref[d:d + 1, cols])
            i = _sigmoid(gi + bi_ref[d:d + 1, cols])
            log_a = (-LRU_C * jax.nn.softplus(-lam_ref[d:d + 1, cols])) * r
            a = jnp.exp(log_a)
            m = 1.0 - a * a
            mult = m * lax.rsqrt(jnp.maximum(m, 1e-30))
            a_sc[k] = a
            b_sc[k] = mult * (i * xc[:, cols])
            _group_scan(a_sc, b_sc, k, reverse=(d == 1))

    ng = seq_len // V7X_SUBLANES
    bcast = lambda row: jnp.broadcast_to(row, (V7X_SUBLANES, LRU_BW))
    chains = [(q, k) for q in range(n_seq) for k in range(n_slab)]
    init = tuple((bcast(h0_ref[q, 0:1, k * LRU_BW:(k + 1) * LRU_BW]),
                  bcast(h0_ref[q, 1:2, k * LRU_BW:(k + 1) * LRU_BW])) for q, k in chains)

    def step(j, carry):
        out = []
        for (q, k), (hf_in, hb_in) in zip(chains, carry):
            rf = pl.ds(pl.multiple_of(q * seq_len + j * V7X_SUBLANES, V7X_SUBLANES), V7X_SUBLANES)
            rb = pl.ds(pl.multiple_of(q * seq_len + (ng - 1 - j) * V7X_SUBLANES, V7X_SUBLANES), V7X_SUBLANES)
            hf = af_sc[k, rf, :] * hf_in + bf_sc[k, rf, :]
            hb = ab_sc[k, rb, :] * hb_in + bb_sc[k, rb, :]
            hf_sc[k, rf, :] = hf
            hb_sc[k, rb, :] = hb
            out.append((bcast(hf[V7X_SUBLANES - 1:V7X_SUBLANES]), bcast(hb[0:1])))
        return tuple(out)

    final = lax.fori_loop(0, ng, step, init)
    for (q, k), (hf_last, hb_first) in zip(chains, final):
        st_ref[q, 0:1, k * LRU_BW:(k + 1) * LRU_BW] = hf_last[0:1]
        st_ref[q, 1:2, k * LRU_BW:(k + 1) * LRU_BW] = hb_first[0:1]

    gt = gate_ref[...].astype(F32)
    gelu = 0.5 * gt * (1.0 + jnp.tanh(math.sqrt(2.0 / math.pi) * (gt + 0.044715 * (gt * gt * gt))))
    h = jnp.concatenate([hf_sc[k] + hb_sc[k] for k in range(n_slab)], axis=1)
    y_ref[...] = (h * gelu).astype(y_ref.dtype)


def rglru(u, seq_len, conv_w, conv_b, wcat, ba, bi, lam, h0):
    tokens = u.shape[0]
    nseq = tokens // seq_len
    cb = LRU_CB
    rows = max(seq_len, LRU_ROWS)
    assert seq_len & (seq_len - 1) == 0 and rows % seq_len == 0
    per_blk = rows // seq_len
    gate_blk0 = 3 * CONV_W // cb
    xb_blk0 = (3 * CONV_W + LRU_W) // cb
    seq_scr = lambda: pltpu.VMEM((cb // LRU_BW, rows, LRU_BW), F32)
    return pl.pallas_call(
        functools.partial(_rglru_kernel, seq_len=seq_len),
        out_shape=(jax.ShapeDtypeStruct((tokens, LRU_W), BF16), jax.ShapeDtypeStruct((nseq, 2, LRU_W), F32)),
        grid=(tokens // rows, LRU_W // cb),
        in_specs=[pl.BlockSpec((rows, cb), lambda s, c: (s, gate_blk0 + c)),
                  pl.BlockSpec((rows, cb), lambda s, c: (s, xb_blk0 + c)),
                  pl.BlockSpec((4, cb), lambda s, c: (0, c)),
                  pl.BlockSpec((1, cb), lambda s, c: (0, c)),
                  pl.BlockSpec((cb // LRU_BW, LRU_BW, 4 * LRU_BW), lambda s, c: (c, 0, 0)),
                  pl.BlockSpec((2, cb), lambda s, c: (0, c)),
                  pl.BlockSpec((2, cb), lambda s, c: (0, c)),
                  pl.BlockSpec((2, cb), lambda s, c: (0, c)),
                  pl.BlockSpec((per_blk, 2, cb), lambda s, c: (s, 0, c))],
        out_specs=(pl.BlockSpec((rows, cb), lambda s, c: (s, c)),
                   pl.BlockSpec((per_blk, 2, cb), lambda s, c: (s, 0, c))),
        scratch_shapes=[seq_scr() for _ in range(6)],
        compiler_params=_cparams(("parallel", "parallel")),
        name="rglru",
    )(u, u, conv_w, conv_b.reshape(1, LRU_W), wcat, ba, bi, lam, h0)


def _mix_ffn_kernel(p0a_ref, p0b_ref, p1a_ref, p1b_ref, p2a_ref, p2b_ref, wo_ref, xa_ref, xb_ref,
                    g1_ref, g_ref, sh_ref, sc_ref, g2_ref, wg_ref, wu_ref, wd_ref, oa_ref, ob_ref,
                    wo_sc, wg_sc, wu_sc, wd_sc, *, n_a):
    s = pl.program_id(0)
    n_load = wg_sc.shape[0]
    n_out = wo_sc.shape[0]

    @pl.when(s < n_out)
    def _():
        wo_sc[s] = wo_ref[...].astype(BF16)

    @pl.when(s < n_load)
    def _():
        wg_sc[s] = wg_ref[...].astype(BF16)
        wu_sc[s] = wu_ref[...].astype(BF16)
        wd_sc[s] = wd_ref[...].astype(BF16)

    @pl.when(s >= n_load)
    def _():
        first = s - n_load < n_a
        pick = lambda a_ref, b_ref: jnp.where(first, a_ref[...], b_ref[...])
        m = jnp.dot(pick(p0a_ref, p0b_ref), wo_sc[0], preferred_element_type=F32)
        m += jnp.dot(pick(p1a_ref, p1b_ref), wo_sc[1], preferred_element_type=F32)
        m += jnp.dot(pick(p2a_ref, p2b_ref), wo_sc[2], preferred_element_type=F32)
        x = pick(xa_ref, xb_ref) + g1_ref[...] * m
        h = _norm_mod(x, g_ref[...], sh_ref[...], sc_ref[...]).astype(BF16)
        y = None
        for f in range(n_load):
            hg = jnp.dot(h, wg_sc[f], preferred_element_type=F32)
            hu = jnp.dot(h, wu_sc[f], preferred_element_type=F32)
            act = (_silu(hg) * hu).astype(BF16)
            yf = jnp.dot(act, wd_sc[f], preferred_element_type=F32)
            y = yf if y is None else y + yf
        out = x + g2_ref[...] * y

        @pl.when(first)
        def _():
            oa_ref[...] = out

        @pl.when(jnp.logical_not(first))
        def _():
            ob_ref[...] = out


def mix_ffn(parts, w_out, xs, g, modtab, conds, w_gate, w_up, w_down):
    tf = TF_FFN
    kb = MIX_SLAB
    n_load = D_FF // tf
    n_out = len(parts[0])
    assert n_out <= n_load
    ts = _TwoSets(n_load, TM, [x.shape[0] for x in xs], conds)
    blk = _block_of(n_load)
    oblk = _block_of(n_out)
    lhs_specs, lhs_args = [], []
    for (arr_a, col_a), (arr_b, col_b) in zip(*parts):
        assert col_a == col_b
        lhs_specs += ts.cols(kb, col_a)
        lhs_args += [arr_a, arr_b]
    return pl.pallas_call(
        functools.partial(_mix_ffn_kernel, n_a=ts.n_a),
        out_shape=tuple(jax.ShapeDtypeStruct(x.shape, F32) for x in xs),
        grid=(ts.steps,),
        in_specs=lhs_specs + [
            pl.BlockSpec((kb, D), lambda s: (oblk(s), 0)),
            *ts.rows(D),
            ts.mod_spec(2),
            pl.BlockSpec((1, D), lambda s: (0, 0)),
            ts.mod_spec(3), ts.mod_spec(4), ts.mod_spec(5),
            pl.BlockSpec((D, tf), lambda s: (0, blk(s))),
            pl.BlockSpec((D, tf), lambda s: (0, blk(s))),
            pl.BlockSpec((tf, D), lambda s: (blk(s), 0))],
        out_specs=ts.rows(D),
        scratch_shapes=[pltpu.VMEM((n_out, kb, D), BF16),
                        pltpu.VMEM((n_load, D, tf), BF16), pltpu.VMEM((n_load, D, tf), BF16),
                        pltpu.VMEM((n_load, tf, D), BF16)],
        compiler_params=_cparams(("arbitrary",), V7X_VMEM_LIMIT_LARGE_BYTES),
        name="mix_ffn",
    )(*lhs_args, w_out, *xs, modtab, g.reshape(1, D), modtab, modtab, modtab, w_gate, w_up, w_down)


def _rms(x, g):
    return x * lax.rsqrt(jnp.mean(x * x, axis=-1, keepdims=True) + EPS) * g


def _in1_kernel(x_ref, g_ref, sh_ref, sc_ref, w_ref, qn_ref, kvn_ref, wq_ref, wkv_ref,
                qnope_ref, qpe_ref, ckv_ref, kr_ref, kv_ref, uh_ref, w_sc, wq_sc, wkv_sc):
    @pl.when(pl.program_id(0) == 0)
    def _():
        w_sc[...] = w_ref[...].astype(BF16)
        for h in range(MLA_HEADS):
            c0 = h * QK_DIM
            wq_sc[:, h * NOPE:(h + 1) * NOPE] = wq_ref[:, c0:c0 + NOPE].astype(BF16)
            r0 = MLA_HEADS * NOPE + h * ROPE
            wq_sc[:, r0:r0 + ROPE] = wq_ref[:, c0 + NOPE:c0 + QK_DIM].astype(BF16)
        wkv_sc[...] = wkv_ref[...].astype(BF16)

    h = _norm_mod(x_ref[...], g_ref[...], sh_ref[...], sc_ref[...]).astype(BF16)
    u = lax.dot_general(h, w_sc[...], (((1,), (1,)), ((), ())), preferred_element_type=F32)
    o1, o2, o3 = Q_RANK, Q_RANK + KV_RANK, Q_RANK + KV_RANK + ROPE
    cq = _rms(u[:, :o1], qn_ref[...])
    q = jnp.dot(cq.astype(BF16), wq_sc[...], preferred_element_type=F32) * _SCALE
    qnope_ref[...] = q[:, :MLA_HEADS * NOPE].astype(qnope_ref.dtype)
    qpe_ref[...] = q[:, MLA_HEADS * NOPE:]
    ckv = _rms(u[:, o1:o2], kvn_ref[...])
    ckv_ref[...] = ckv
    kv_ref[...] = jnp.dot(ckv.astype(BF16), wkv_sc[...], preferred_element_type=F32).astype(kv_ref.dtype)
    kr_ref[...] = u[:, o2:o3]
    uh_ref[...] = u[:, o3:]


def in1_proj(x, g, modtab, cond, w_in, q_norm, kv_norm, w_q_up, w_kv_up):
    tokens = x.shape[0]
    tm = TM_IN1
    nkv = MLA_HEADS * (NOPE + VDIM)
    const = lambda i: (0, 0)
    zero = lambda i: 0
    once = pl.Buffered(1)
    outs = (jax.ShapeDtypeStruct((tokens, MLA_HEADS * NOPE), BF16),
            jax.ShapeDtypeStruct((tokens, MLA_HEADS * ROPE), F32),
            jax.ShapeDtypeStruct((tokens, KV_RANK), F32),
            jax.ShapeDtypeStruct((tokens, ROPE), F32),
            jax.ShapeDtypeStruct((tokens, nkv), BF16),
            jax.ShapeDtypeStruct((tokens, 3 * HY_W), F32))
    row = lambda w: pl.BlockSpec((tm, w), lambda i: (i, 0))
    return pl.pallas_call(
        _in1_kernel,
        out_shape=outs,
        grid=(tokens // tm,),
        in_specs=[row(D),
                  pl.BlockSpec((1, D), const),
                  _mod_spec(0, cond, tm, D, zero),
                  _mod_spec(1, cond, tm, D, zero),
                  pl.BlockSpec((IN1, D), const, pipeline_mode=once),
                  pl.BlockSpec((1, Q_RANK), const),
                  pl.BlockSpec((1, KV_RANK), const),
                  pl.BlockSpec((Q_RANK, MLA_HEADS * QK_DIM), const, pipeline_mode=once),
                  pl.BlockSpec((KV_RANK, nkv), const, pipeline_mode=once)],
        out_specs=tuple(row(o.shape[1]) for o in outs),
        scratch_shapes=[pltpu.VMEM((IN1, D), BF16), pltpu.VMEM((Q_RANK, MLA_HEADS * QK_DIM), BF16),
                        pltpu.VMEM((KV_RANK, nkv), BF16)],
        compiler_params=_cparams(("arbitrary",)),
        name="in1_proj",
    )(x, g.reshape(1, D), modtab, modtab, w_in.T, q_norm.reshape(1, Q_RANK), kv_norm.reshape(1, KV_RANK),
      w_q_up, w_kv_up)


def _mm_kernel(a_ref, w_ref, o_ref):
    o_ref[...] = jnp.dot(a_ref[...].astype(BF16), w_ref[...].astype(BF16),
                         preferred_element_type=F32).astype(o_ref.dtype)


def kv_up(ckv, w_kv_up):
    rows = ckv.shape[0]
    n = w_kv_up.shape[1]
    return pl.pallas_call(
        _mm_kernel,
        out_shape=jax.ShapeDtypeStruct((rows, n), BF16),
        grid=(rows // TM,),
        in_specs=[pl.BlockSpec((TM, KV_RANK), lambda i: (i, 0)), pl.BlockSpec((KV_RANK, n), lambda i: (0, 0))],
        out_specs=pl.BlockSpec((TM, n), lambda i: (i, 0)),
        compiler_params=_cparams(("parallel",)),
        name="kv_up",
    )(ckv, w_kv_up)


_NT = (((1,), (1,)), ((), ()))
_SCALE = 1.0 / math.sqrt(QK_DIM)


def _fill_rope_tables(cos_ref, sin_ref):
    n, width = cos_ref.shape
    n_grid_rows = n // GRID_W
    n_freq = ROPE // 4

    def trig(count):
        lane = lax.broadcasted_iota(jnp.int32, (count, width), 1)
        j = lane & (ROPE // 2 - 1)
        inv = jnp.exp((j & (n_freq - 1)).astype(F32) * (-math.log(ROPE_THETA) / n_freq))
        ang = lax.broadcasted_iota(jnp.int32, (count, width), 0).astype(F32) * inv
        return jnp.cos(ang), jnp.sin(ang), j < n_freq

    cos_c, sin_c, by_row = trig(GRID_W)
    cos_r, sin_r, _ = trig(n_grid_rows)
    for r in range(n_grid_rows):
        rows = slice(r * GRID_W, (r + 1) * GRID_W)
        cos_ref[rows, :] = jnp.where(by_row, jnp.broadcast_to(cos_r[r:r + 1], cos_c.shape), cos_c)
        sin_ref[rows, :] = jnp.where(by_row, jnp.broadcast_to(sin_r[r:r + 1], sin_c.shape), sin_c)


def _rope(x, cos, sin):
    width = x.shape[1]
    lane = lax.broadcasted_iota(jnp.int32, x.shape, 1)
    first_half = (lane & (ROPE - 1)) < ROPE // 2
    xr = jnp.where(first_half, -pltpu.roll(x, width - ROPE // 2, 1), pltpu.roll(x, ROPE // 2, 1))
    return x * cos + xr * sin


def _ones_column(n):
    lane = lax.broadcasted_iota(jnp.int32, (n, VDIM), 1)
    return jnp.where(lane == 0, 1.0, 0.0).astype(BF16)


def _head_attention(qcat, kcat, vaug):
    s = lax.dot_general(qcat, kcat, _NT, preferred_element_type=F32)
    p = jnp.exp(s - jnp.max(s, axis=-1, keepdims=True)).astype(BF16)
    oa = jnp.dot(p, vaug, preferred_element_type=F32)
    return oa[:, :VDIM] / oa[:, VDIM:VDIM + 1]


def _attn_ctx_kernel(qn_ref, qpe_ref, kv_ref, kr_ref, o_ref, *, seq_len):
    n = qn_ref.shape[0]
    n_seq = n // seq_len
    ones = _ones_column(n)
    kpe = kr_ref[...].astype(BF16)
    per_seq = lambda a: a.reshape(n_seq, seq_len, a.shape[-1])
    for h in range(MLA_HEADS):
        c0 = h * (NOPE + VDIM)
        qcat = per_seq(jnp.concatenate([qn_ref[:, h * NOPE:(h + 1) * NOPE],
                                        qpe_ref[:, h * ROPE:(h + 1) * ROPE].astype(BF16)], axis=1))
        kcat = per_seq(jnp.concatenate([kv_ref[:, c0:c0 + NOPE], kpe], axis=1))
        vaug = per_seq(jnp.concatenate([kv_ref[:, c0 + NOPE:c0 + NOPE + VDIM], ones], axis=1))
        s = jnp.einsum("bqd,bkd->bqk", qcat, kcat, preferred_element_type=F32)
        p = jnp.exp(s - jnp.max(s, axis=-1, keepdims=True)).astype(BF16)
        oa = jnp.einsum("bqk,bkd->bqd", p, vaug, preferred_element_type=F32)
        o = oa[:, :, :VDIM] / oa[:, :, VDIM:VDIM + 1]
        o_ref[:, h * VDIM:(h + 1) * VDIM] = o.reshape(n, VDIM).astype(o_ref.dtype)


def attn_ctx(qnope, qpe, kv, kr, seq_len):
    tokens = qnope.shape[0]
    rows = ATTN_CTX_SEQS * seq_len
    blk = lambda w: pl.BlockSpec((rows, w), lambda s: (s, 0))
    return pl.pallas_call(
        functools.partial(_attn_ctx_kernel, seq_len=seq_len),
        out_shape=jax.ShapeDtypeStruct((tokens, MLA_HEADS * VDIM), BF16),
        grid=(tokens // rows,),
        in_specs=[blk(MLA_HEADS * NOPE), blk(MLA_HEADS * ROPE), blk(MLA_HEADS * (NOPE + VDIM)), blk(ROPE)],
        out_specs=blk(MLA_HEADS * VDIM),
        compiler_params=_cparams(("parallel",)),
        name="attn_ctx",
    )(qnope, qpe, kv, kr)


def _attn_lat_kernel(qn_ref, qpe_ref, kvc_ref, krc_ref, kvl_ref, krl_ref, o_ref, kcat_sc, vaug_sc, cos_sc, sin_sc):
    tq = qn_ref.shape[0]
    n_ctx = krc_ref.shape[0]
    n_lat = krl_ref.shape[0]

    @pl.when(pl.program_id(1) == 0)
    def _():
        _fill_rope_tables(cos_sc, sin_sc)
        kr2 = jnp.concatenate([krl_ref[...], krl_ref[...]], axis=1)
        kpe_lat = _rope(kr2, cos_sc[...], sin_sc[...])[:, :ROPE].astype(BF16)
        kpe_ctx = krc_ref[...].astype(BF16)
        ones_c, ones_l = _ones_column(n_ctx), _ones_column(n_lat)
        for h in range(MLA_HEADS):
            c0 = h * (NOPE + VDIM)
            for r0, nr, kv_ref, kpe, ones in ((0, n_ctx, kvc_ref, kpe_ctx, ones_c), (n_ctx, n_lat, kvl_ref, kpe_lat, ones_l)):
                kcat_sc[h, r0:r0 + nr, 0:NOPE] = kv_ref[:, c0:c0 + NOPE]
                kcat_sc[h, r0:r0 + nr, NOPE:QK_DIM] = kpe
                vaug_sc[h, r0:r0 + nr, 0:VDIM] = kv_ref[:, c0 + NOPE:c0 + NOPE + VDIM]
                vaug_sc[h, r0:r0 + nr, VDIM:2 * VDIM] = ones

    q0 = pl.multiple_of(pl.program_id(1) * tq, tq)
    rep = lambda a: jnp.concatenate([a] * (MLA_HEADS // 2), axis=1)
    qp_all = _rope(qpe_ref[...], rep(cos_sc[pl.ds(q0, tq), :]), rep(sin_sc[pl.ds(q0, tq), :])).astype(BF16)
    for h in range(MLA_HEADS):
        qcat = jnp.concatenate([qn_ref[:, h * NOPE:(h + 1) * NOPE], qp_all[:, h * ROPE:(h + 1) * ROPE]], axis=1)
        o_ref[:, h * VDIM:(h + 1) * VDIM] = _head_attention(qcat, kcat_sc[h], vaug_sc[h]).astype(o_ref.dtype)


def attn_lat(qnope, qpe, kv_ctx, kr_ctx, kv_lat, kr_lat, seq_len, ctx_len):
    tokens = qnope.shape[0]
    nq = seq_len // TQ
    qblk = lambda w: pl.BlockSpec((TQ, w), lambda b, i: (b * nq + i, 0))
    seq = lambda n, w: pl.BlockSpec((n, w), lambda b, i: (b, 0))
    nkv = MLA_HEADS * (NOPE + VDIM)
    n_keys = ctx_len + seq_len
    return pl.pallas_call(
        _attn_lat_kernel,
        out_shape=jax.ShapeDtypeStruct((tokens, MLA_HEADS * VDIM), BF16),
        grid=(tokens // seq_len, nq),
        in_specs=[qblk(MLA_HEADS * NOPE), qblk(MLA_HEADS * ROPE), seq(ctx_len, nkv), seq(ctx_len, ROPE),
                  seq(seq_len, nkv), seq(seq_len, ROPE)],
        out_specs=qblk(MLA_HEADS * VDIM),
        scratch_shapes=[pltpu.VMEM((MLA_HEADS, n_keys, QK_DIM), BF16),
                        pltpu.VMEM((MLA_HEADS, n_keys, 2 * VDIM), BF16),
                        pltpu.VMEM((seq_len, 2 * ROPE), F32), pltpu.VMEM((seq_len, 2 * ROPE), F32)],
        compiler_params=_cparams(("parallel", "arbitrary")),
        name="attn_lat",
    )(qnope, qpe, kv_ctx, kr_ctx, kv_lat, kr_lat)


def _dft_kernel(o_ref):
    tr, n = o_ref.shape[1], o_ref.shape[2]
    nb = n // V7X_LANES
    f = pl.program_id(0) * tr + lax.broadcasted_iota(jnp.int32, (tr, V7X_LANES), 0)
    j = lax.broadcasted_iota(jnp.int32, (tr, V7X_LANES), 1)

    def cos_sin(m):
        ang = (m & (2 * n - 1)).astype(F32) * (math.pi / n)
        return jnp.cos(ang), jnp.sin(ang)

    cj, sj = cos_sin(f * j)
    cb, sb = cos_sin(f * (j * V7X_LANES))
    for b in range(nb):
        cbb, sbb = cb[:, b:b + 1], sb[:, b:b + 1]
        cols = slice(b * V7X_LANES, (b + 1) * V7X_LANES)
        o_ref[0, :, cols] = (cbb * cj - sbb * sj).astype(o_ref.dtype)
        o_ref[1, :, cols] = (sbb * cj + cbb * sj).astype(o_ref.dtype)


def dft_tables(n):
    tr = 128
    return pl.pallas_call(
        _dft_kernel,
        out_shape=jax.ShapeDtypeStruct((2, n, n), BF16),
        grid=(n // tr,),
        out_specs=pl.BlockSpec((2, tr, n), lambda i: (0, i, 0)),
        compiler_params=_cparams(("parallel",)),
        name="dft_tables",
    )()


def _split_dot(table, x):
    hi = x.astype(BF16)
    lo = (x - hi.astype(F32)).astype(BF16)
    return (jnp.dot(table, hi, preferred_element_type=F32) + jnp.dot(table, lo, preferred_element_type=F32))


def _hy_filter_kernel(cs_ref, pack_ref, w2_ref, w3_ref, kr_ref, ks_ref, kny_ref):
    n = cs_ref.shape[1]
    row = lax.broadcasted_iota(jnp.int32, (n, V7X_LANES), 0).astype(F32)
    lane = lax.broadcasted_iota(jnp.int32, (n, V7X_LANES), 1)
    t = row * (1.0 / (n - 1))
    w = (2.0 * math.pi) * row / n
    band = jnp.where(lane <= HY_BANDS, lane - 1, lane - 1 - HY_BANDS).astype(F32)
    freq = 1e-4 + band * ((HY_BANDS - 1 - 1e-4) / (HY_BANDS - 1))
    arg = jnp.where(lane <= HY_BANDS, freq * w + 0.5 * math.pi, -(freq * w))
    z = jnp.where(lane == 0, t, jnp.where(lane <= 2 * HY_BANDS, jnp.sin(arg), 0.0))
    hid = jnp.sin(_dot3(z, pack_ref[0:V7X_LANES, :]) + pack_ref[V7X_LANES:V7X_LANES + 1, :])
    hid = jnp.sin(_dot3(hid, w2_ref[...]) + pack_ref[V7X_LANES + 1:V7X_LANES + 2, :])
    hf = _dot3(hid, w3_ref[...])

    rowc = lax.broadcasted_iota(jnp.int32, (n, HY_W), 0)
    chan = lax.broadcasted_iota(jnp.int32, (n, HY_W), 1).astype(F32)
    max_decay = math.log(HY_TARGET) / HY_FAST_DECAY
    min_decay = math.log(HY_TARGET) / HY_SLOW_DECAY
    deltas = min_decay + chan * ((max_decay - min_decay) / (HY_W - 1))
    decay = jnp.exp(-(rowc.astype(F32) * (1.0 / (n - 1))) * jnp.abs(deltas))
    h_fwd = hf[:, :HY_W] * decay
    h_bwd = jnp.where(rowc == 0, 0.0, hf[:, HY_W:] * decay)
    norm = jnp.sum(jnp.abs(h_fwd) + jnp.abs(h_bwd), axis=0, keepdims=True)
    even = (h_fwd + h_bwd) / norm
    odd = (h_fwd - h_bwd) / norm
    cf = jnp.where(rowc == 0, 1.0, 2.0) * (1.0 / (2 * n))
    kr_ref[...] = cf * _split_dot(cs_ref[0], even)
    ks_ref[...] = cf * _split_dot(cs_ref[1], odd)
    sgn = jnp.where((rowc & 1) == 1, -1.0, 1.0)
    kny_ref[...] = jnp.sum(sgn * even, axis=0, keepdims=True) * (1.0 / (2 * n))


def hy_filter(cs, pack, w2, w3):
    n = cs.shape[1]
    full = lambda a: pl.BlockSpec(a.shape, lambda: (0,) * a.ndim)
    args = (cs, pack, w2, w3)
    return pl.pallas_call(
        _hy_filter_kernel,
        out_shape=(jax.ShapeDtypeStruct((n, HY_W), F32), jax.ShapeDtypeStruct((n, HY_W), F32),
                   jax.ShapeDtypeStruct((1, HY_W), F32)),
        in_specs=[full(a) for a in args],
        out_specs=(pl.BlockSpec((n, HY_W), lambda: (0, 0)), pl.BlockSpec((n, HY_W), lambda: (0, 0)),
                   pl.BlockSpec((1, HY_W), lambda: (0, 0))),
        compiler_params=pltpu.CompilerParams(vmem_limit_bytes=V7X_VMEM_LIMIT_BYTES),
        name="hy_filter",
    )(*args)


def _hyena_kernel(u0_ref, u1_ref, u2_ref, sw_ref, sb_ref, cs_ref, kr_ref, ks_ref, kny_ref, bias_ref, o_ref,
                  *, seq_len):
    n, cb = u0_ref.shape
    n_seq = n // seq_len
    t = lax.broadcasted_iota(jnp.int32, (n, cb), 0) & (seq_len - 1)

    def short_conv(u_ref, k):
        u = u_ref[...]
        w = sw_ref[:, k * cb:(k + 1) * cb]
        return (sb_ref[:, k * cb:(k + 1) * cb] + w[0:1] * _shift_rows(u, 1, t, seq_len) + w[1:2] * u
                + w[2:3] * _shift_rows(u, -1, t, seq_len))

    x0 = short_conv(u0_ref, 0)
    z = short_conv(u1_ref, 1) * short_conv(u2_ref, 2)
    wide = lambda a: jnp.concatenate([a[q * seq_len:(q + 1) * seq_len] for q in range(n_seq)], axis=1)
    rep = lambda a: jnp.concatenate([a] * n_seq, axis=1)
    zw = wide(z)
    zb = zw.astype(BF16)
    c, s = cs_ref[0], cs_ref[1]
    ur = jnp.dot(c, zb, preferred_element_type=F32)
    us = jnp.dot(s, zb, preferred_element_type=F32)
    sgn = jnp.where((lax.broadcasted_iota(jnp.int32, zw.shape, 0) & 1) == 1, -1.0, 1.0)
    uny = jnp.sum(sgn * zw, axis=0, keepdims=True)
    kr, ks = rep(kr_ref[...]), rep(ks_ref[...])
    yr = (ur * kr - us * ks).astype(BF16)
    ys = (ur * ks + us * kr).astype(BF16)
    yw = jnp.dot(c, yr, preferred_element_type=F32) + jnp.dot(s, ys, preferred_element_type=F32)
    yw = yw + sgn * (uny * rep(kny_ref[...]))
    y = jnp.concatenate([yw[:, q * cb:(q + 1) * cb] for q in range(n_seq)], axis=0)
    o_ref[...] = (x0 * (y + bias_ref[...] * z)).astype(o_ref.dtype)


def hyena(uh, seq_len, short_w, short_b, cs, kr, ks, kny, bias):
    tokens = uh.shape[0]
    cb = HY_CB
    nc = HY_W // cb
    rows = max(seq_len, HY_ROWS)
    assert seq_len & (seq_len - 1) == 0 and rows % seq_len == 0
    ublk = lambda k: pl.BlockSpec((rows, cb), lambda s, c: (s, k * nc + c))
    chan = lambda r: pl.BlockSpec((r, cb), lambda s, c: (0, c))
    return pl.pallas_call(
        functools.partial(_hyena_kernel, seq_len=seq_len),
        out_shape=jax.ShapeDtypeStruct((tokens, HY_W), BF16),
        grid=(tokens // rows, nc),
        in_specs=[ublk(0), ublk(1), ublk(2),
                  pl.BlockSpec((None, 3, 3 * cb), lambda s, c: (c, 0, 0)),
                  pl.BlockSpec((None, 1, 3 * cb), lambda s, c: (c, 0, 0)),
                  pl.BlockSpec((2, seq_len, seq_len), lambda s, c: (0, 0, 0)),
                  chan(seq_len), chan(seq_len), chan(1), chan(1)],
        out_specs=pl.BlockSpec((rows, cb), lambda s, c: (s, c)),
        compiler_params=_cparams(("parallel", "parallel")),
        name="hyena",
    )(uh, uh, uh, short_w, short_b, cs, kr, ks, kny, bias)


META_E1, META_E2, META_R1, META_R2, META_G1, META_G2 = range(6)


def _route_kernel(p0_ref, p1_ref, p2_ref, wo_ref, x_ref, g1_ref, g_ref, sh_ref, sc_ref, wr_ref,
                  x1_ref, h_ref, meta_ref, meta_t_ref, cnt_ref, run_sc, wo_sc):
    tm = x_ref.shape[0]
    lane = lax.broadcasted_iota(jnp.int32, (tm, V7X_LANES), 1)

    @pl.when(pl.program_id(0) == 0)
    def _():
        run_sc[...] = jnp.zeros_like(run_sc)
        wo_sc[...] = wo_ref[...].astype(BF16)

    kb = p0_ref.shape[1]
    m = jnp.dot(p0_ref[...], wo_sc[0:kb, :], preferred_element_type=F32)
    m += jnp.dot(p1_ref[...], wo_sc[kb:2 * kb, :], preferred_element_type=F32)
    m += jnp.dot(p2_ref[...], wo_sc[2 * kb:3 * kb, :], preferred_element_type=F32)
    x1 = x_ref[...] + g1_ref[...] * m
    x1_ref[...] = x1
    h = _norm_mod(x1, g_ref[...], sh_ref[...], sc_ref[...])
    h_ref[...] = h
    logits = _dot3(h, wr_ref[0:D, :]) + wr_ref[D:D + 1, :]
    lg = jnp.where(lane < N_EXPERTS, logits, -jnp.inf)
    l1 = jnp.max(lg, axis=-1, keepdims=True)
    i1 = jnp.min(jnp.where(lg == l1, lane, V7X_LANES), axis=-1, keepdims=True)
    rest = jnp.where(lane == i1, -jnp.inf, lg)
    l2 = jnp.max(rest, axis=-1, keepdims=True)
    i2 = jnp.min(jnp.where(rest == l2, lane, V7X_LANES), axis=-1, keepdims=True)
    gap = jnp.exp(l2 - l1)
    gate1 = 1.0 / (1.0 + gap)
    gate2 = gap * gate1
    m1 = lane == i1
    m2 = lane == i2
    chosen = jnp.where(m1 | m2, 1.0, 0.0)
    r = lax.broadcasted_iota(jnp.int32, (tm, tm), 0)
    c = lax.broadcasted_iota(jnp.int32, (tm, tm), 1)
    tri = jnp.where(c < r, 1.0, 0.0).astype(BF16)
    before = jnp.dot(tri, chosen.astype(BF16), preferred_element_type=F32) + run_sc[0:1, :]
    rank1 = jnp.sum(jnp.where(m1, before, 0.0), axis=-1, keepdims=True)
    rank2 = jnp.sum(jnp.where(m2, before, 0.0), axis=-1, keepdims=True)
    vals = (i1.astype(F32), i2.astype(F32), rank1, rank2, gate1, gate2)
    meta = jnp.zeros((tm, V7X_LANES), F32)
    for k, v in enumerate(vals):
        meta = jnp.where(lane == k, v, meta)
    meta_ref[...] = meta
    meta_t_ref[...] = meta.T[:V7X_SUBLANES]
    run_sc[...] = run_sc[...] + jnp.sum(chosen, axis=0, keepdims=True)
    cnt_ref[...] = run_sc[...]


def mix_route(parts, w_out, x, g, modtab, cond, router_pack):
    tokens = x.shape[0]
    tm = TM_ROUTE
    kb = MIX_SLAB
    zero = lambda i: 0
    const = lambda i: (0, 0)
    rows = lambda w: pl.BlockSpec((tm, w), lambda i: (i, 0))
    lhs_specs = [pl.BlockSpec((tm, kb), (lambda i, cbk=cbk: (i, cbk))) for _, cbk in parts]
    return pl.pallas_call(
        _route_kernel,
        out_shape=(jax.ShapeDtypeStruct((tokens, D), F32),
                   jax.ShapeDtypeStruct((tokens, D), F32),
                   jax.ShapeDtypeStruct((tokens, V7X_LANES), F32),
                   jax.ShapeDtypeStruct((V7X_SUBLANES, tokens), F32),
                   jax.ShapeDtypeStruct((V7X_SUBLANES, V7X_LANES), F32)),
        grid=(tokens // tm,),
        in_specs=lhs_specs + [
            pl.BlockSpec((len(parts) * kb, D), const, pipeline_mode=pl.Buffered(1)),
            rows(D),
            _mod_spec(2, cond, tm, D, zero),
            pl.BlockSpec((1, D), const),
            _mod_spec(3, cond, tm, D, zero),
            _mod_spec(4, cond, tm, D, zero),
            pl.BlockSpec((D + V7X_SUBLANES, V7X_LANES), const)],
        out_specs=(rows(D), rows(D), rows(V7X_LANES),
                   pl.BlockSpec((V7X_SUBLANES, tm), lambda i: (0, i)),
                   pl.BlockSpec((V7X_SUBLANES, V7X_LANES), const)),
        scratch_shapes=[pltpu.VMEM((V7X_SUBLANES, V7X_LANES), F32), pltpu.VMEM((len(parts) * kb, D), BF16)],
        compiler_params=_cparams(("arbitrary",)),
        name="mix_route",
    )(*[a for a, _ in parts], w_out, x, modtab, g.reshape(1, D), modtab, modtab, router_pack)


def _row_copy(src_ref, src_row, dst_ref, dst_row, sem):
    return pltpu.make_async_copy(src_ref.at[pl.ds(src_row, 1)], dst_ref.at[pl.ds(dst_row, 1)], sem)


_PAD_BULK = (256, 128, 64, 32, 16, 8)


def _zero_fill(hs_ref, zero_sc, sem, pads_ref, n_tail_max, wait):
    tmr = zero_sc.shape[0]

    def copy(rows, dst):
        cp = pltpu.make_async_copy(zero_sc.at[pl.ds(0, rows)], hs_ref.at[pl.ds(dst, rows)], sem)
        cp.wait() if wait else cp.start()

    for e in range(N_EXPERTS):
        start, n = pads_ref[e], pads_ref[N_EXPERTS + e]
        head = jnp.minimum((-start) & (V7X_SUBLANES - 1), n)
        for r in range(V7X_SUBLANES - 1):
            @pl.when(r < head)
            def _():
                copy(1, start + r)
        body = start + head
        rem = n - head
        for k in _PAD_BULK:
            @pl.when((rem & k) != 0)
            def _():
                copy(k, pl.multiple_of(body + (rem & ~(2 * k - 1)), V7X_SUBLANES))
    tail_start, tail_tiles = pads_ref[2 * N_EXPERTS], pads_ref[2 * N_EXPERTS + 1]
    for t in range(n_tail_max):
        @pl.when(t < tail_tiles)
        def _():
            copy(tmr, pl.multiple_of(tail_start + t * tmr, tmr))


def _dispatch_kernel(pos_ref, pads_ref, ha_ref, hb_ref, hs_ref, zero_sc, sem, zsem, *, n_a, n_tail_max):
    tm = ha_ref.shape[0]
    n_tok = pos_ref.shape[0] // 2
    i = pl.program_id(0)
    base = i * tm

    @pl.when(i == 0)
    def _():
        zero_sc[...] = jnp.zeros_like(zero_sc)
        _zero_fill(hs_ref, zero_sc, zsem, pads_ref, n_tail_max, wait=False)

    def scatter(h_ref):
        def issue(r, carry):
            _row_copy(h_ref, r, hs_ref, pos_ref[base + r], sem).start(priority=0)
            _row_copy(h_ref, r, hs_ref, pos_ref[n_tok + base + r], sem).start(priority=1)
            return carry

        lax.fori_loop(0, tm, issue, 0, unroll=8)
        for _ in range(2):
            pltpu.make_async_copy(h_ref, hs_ref.at[pl.ds(0, tm)], sem).wait()

    @pl.when(i < n_a)
    def _():
        scatter(ha_ref)

    @pl.when(i >= n_a)
    def _():
        scatter(hb_ref)

    @pl.when(i == 0)
    def _():
        _zero_fill(hs_ref, zero_sc, zsem, pads_ref, n_tail_max, wait=True)


def moe_dispatch(pos, pads, hs_rows, h_a, h_b):
    tm = TM_ROUTE
    n_a, n_b = h_a.shape[0] // tm, h_b.shape[0] // tm
    n_tail_max = hs_rows // TM_EXPERT - (2 * (h_a.shape[0] + h_b.shape[0])) // TM_EXPERT
    return pl.pallas_call(
        functools.partial(_dispatch_kernel, n_a=n_a, n_tail_max=n_tail_max),
        out_shape=jax.ShapeDtypeStruct((hs_rows, D), F32),
        grid_spec=pltpu.PrefetchScalarGridSpec(
            num_scalar_prefetch=2,
            grid=(n_a + n_b,),
            in_specs=[pl.BlockSpec((tm, D), lambda i, *pf: (jnp.minimum(i, n_a - 1), 0)),
                      pl.BlockSpec((tm, D), lambda i, *pf: (jnp.clip(i - n_a, 0, n_b - 1), 0))],
            out_specs=pl.BlockSpec(memory_space=pl.ANY),
            scratch_shapes=[pltpu.VMEM((TM_EXPERT, D), F32), pltpu.SemaphoreType.DMA(()),
                            pltpu.SemaphoreType.DMA(())]),
        compiler_params=_cparams(("arbitrary",)),
        name="moe_dispatch",
    )(pos, pads, h_a, h_b)


def _experts_kernel(te_ref, sg_ref, su_ref, sd_ref, nv_ref, hs_ref, wg_ref, wu_ref, wd_ref, y_ref,
                    wg_sc, wu_sc, wd_sc):
    del sg_ref, su_ref, sd_ref
    j = pl.program_id(0)
    e = te_ref[j]
    e_prev = te_ref[jnp.maximum(j - 1, 0)]
    n_valid = nv_ref[j]
    half = y_ref.shape[0] // 2

    @pl.when((j == 0) | (e != e_prev))
    def _():
        wg_sc[...] = wg_ref[...].astype(BF16)
        wu_sc[...] = wu_ref[...].astype(BF16)
        wd_sc[...] = wd_ref[...].astype(BF16)

    def swiglu(rows):
        h = hs_ref[rows, :].astype(BF16)
        y = None
        for c0 in range(0, D_FF_EXPERT, MOE_CHUNK):
            c1 = min(c0 + MOE_CHUNK, D_FF_EXPERT)
            hg = jnp.dot(h, wg_sc[:, c0:c1], preferred_element_type=F32)
            hu = jnp.dot(h, wu_sc[:, c0:c1], preferred_element_type=F32)
            act = (_silu(hg) * hu).astype(BF16)
            yc = jnp.dot(act, wd_sc[c0:c1, :], preferred_element_type=F32)
            y = yc if y is None else y + yc
        y_ref[rows, :] = y

    @pl.when(n_valid > half)
    def _():
        swiglu(slice(None))

    @pl.when((n_valid > 0) & (n_valid <= half))
    def _():
        swiglu(slice(0, half))
        y_ref[half:, :] = jnp.zeros((half, D), F32)

    @pl.when(n_valid == 0)
    def _():
        y_ref[...] = jnp.zeros_like(y_ref)


def moe_experts(tile_expert, stages, tile_valid, hs, e_gate, e_up, e_down):
    rows = hs.shape[0]
    tmr = TM_EXPERT
    wspec = lambda shape, k: pl.BlockSpec((None,) + shape, lambda j, *pf: (pf[1 + k][j], 0, 0))
    return pl.pallas_call(
        _experts_kernel,
        out_shape=jax.ShapeDtypeStruct((rows, D), F32),
        grid_spec=pltpu.PrefetchScalarGridSpec(
            num_scalar_prefetch=5,
            grid=(rows // tmr,),
            in_specs=[pl.BlockSpec((tmr, D), lambda j, *pf: (j, 0)),
                      wspec((D, D_FF_EXPERT), 0), wspec((D, D_FF_EXPERT), 1), wspec((D_FF_EXPERT, D), 2)],
            out_specs=pl.BlockSpec((tmr, D), lambda j, *pf: (j, 0)),
            scratch_shapes=[pltpu.VMEM((D, D_FF_EXPERT), BF16), pltpu.VMEM((D, D_FF_EXPERT), BF16),
                            pltpu.VMEM((D_FF_EXPERT, D), BF16)]),
        compiler_params=_cparams(("arbitrary",)),
        name="moe_experts",
    )(tile_expert, *stages, tile_valid, hs, e_gate, e_up, e_down)


def _combine_kernel(pos_ref, x_ref, meta_ref, gt_ref, fg_ref, y_ref, o_ref, b1_sc, b2_sc, sem):
    tm = x_ref.shape[0]
    n_tok = pos_ref.shape[0] // 2
    i = pl.program_id(0)

    def gather(tile, slot):
        base = tile * tm

        def issue(r, carry):
            _row_copy(y_ref, pos_ref[base + r], b1_sc.at[slot], r, sem.at[slot]).start(priority=0)
            _row_copy(y_ref, pos_ref[n_tok + base + r], b2_sc.at[slot], r, sem.at[slot]).start(priority=1)
            return carry

        lax.fori_loop(0, tm, issue, 0, unroll=8)

    @pl.when(i == 0)
    def _():
        gather(0, 0)

    @pl.when(i + 1 < pl.num_programs(0))
    def _():
        gather(i + 1, (i + 1) % 2)

    slot = i % 2
    pltpu.make_async_copy(y_ref.at[pl.ds(0, tm)], b1_sc.at[slot], sem.at[slot]).wait()
    pltpu.make_async_copy(y_ref.at[pl.ds(0, tm)], b2_sc.at[slot], sem.at[slot]).wait()

    meta = meta_ref[...]
    lane = lax.broadcasted_iota(jnp.int32, meta.shape, 1)
    g1 = jnp.sum(jnp.where(lane == META_G1, meta, 0.0), axis=-1, keepdims=True)
    g2 = jnp.sum(jnp.where(lane == META_G2, meta, 0.0), axis=-1, keepdims=True)
    x = x_ref[...] + gt_ref[...] * (g1 * b1_sc[slot] + g2 * b2_sc[slot])
    o_ref[...] = _rms(x, fg_ref[...])


def moe_combine(pos, x, meta, modtab, cond, final_g, y):
    tokens = x.shape[0]
    tm = TM_COMBINE
    return pl.pallas_call(
        _combine_kernel,
        out_shape=jax.ShapeDtypeStruct((tokens, D), F32),
        grid_spec=pltpu.PrefetchScalarGridSpec(
            num_scalar_prefetch=1,
            grid=(tokens // tm,),
            in_specs=[pl.BlockSpec((tm, D), lambda i, pos: (i, 0)),
                      pl.BlockSpec((tm, V7X_LANES), lambda i, pos: (i, 0)),
                      _mod_spec(5, cond, tm, D, lambda i, pos: 0),
                      pl.BlockSpec((1, D), lambda i, pos: (0, 0)),
                      pl.BlockSpec(memory_space=pl.ANY)],
            out_specs=pl.BlockSpec((tm, D), lambda i, pos: (i, 0)),
            scratch_shapes=[pltpu.VMEM((2, tm, D), F32), pltpu.VMEM((2, tm, D), F32),
                            pltpu.SemaphoreType.DMA((2,))]),
        compiler_params=_cparams(("arbitrary",)),
        name="moe_combine",
    )(pos, x, meta, modtab, final_g.reshape(1, D), y)


def moe_plan(metas, counts):
    tmr = TM_EXPERT
    cnts = [c[0, :N_EXPERTS].astype(jnp.int32) for c in counts]
    total = functools.reduce(jnp.add, cnts)
    padded = ((total + tmr - 1) // tmr) * tmr
    ends = jnp.cumsum(padded)
    starts = ends - padded
    n_rows = sum(m.shape[1] for m in metas) * 2 + N_EXPERTS * tmr
    n_tiles = n_rows // tmr
    tile_start = jnp.arange(n_tiles, dtype=jnp.int32) * tmr
    tile_expert = jnp.minimum(jnp.sum(tile_start[:, None] >= ends[None, :], axis=1), N_EXPERTS - 1).astype(jnp.int32)
    group_of_tile = jnp.sum(tile_start[:, None] >= ends[None, :], axis=1)
    real_end = jnp.sum(jnp.where(group_of_tile[:, None] == jnp.arange(N_EXPERTS)[None, :],
                                 (starts + total)[None, :], 0), axis=1)
    tile_valid = jnp.clip(real_end - tile_start, 0, tmr).astype(jnp.int32)
    eid = jnp.arange(N_EXPERTS, dtype=jnp.int32)
    later = jnp.where((eid[None, :] > eid[:, None]) & (padded[None, :] > 0), eid[None, :], N_EXPERTS)
    nxt = jnp.min(later, axis=1)
    next_used = jnp.where(nxt == N_EXPERTS, eid, nxt)
    pick = lambda table: jnp.sum(jnp.where(tile_expert[:, None] == eid[None, :], table[None, :], 0), axis=1)
    k_in_group = (tile_start - pick(starts)) // tmr
    tile_next = pick(next_used)
    stages = [jnp.where(k_in_group < k, tile_expert, tile_next).astype(jnp.int32) for k in (1, 2, 3)]
    pos, p1s, p2s = [], [], []
    base = jnp.zeros((N_EXPERTS,), jnp.int32)
    for m, c in zip(metas, cnts):
        first = starts + base
        sel = lambda field: m[field].astype(jnp.int32)
        lookup = lambda e: jnp.sum(jnp.where(e[:, None] == jnp.arange(N_EXPERTS)[None, :], first[None, :], 0), axis=1)
        p1 = lookup(sel(META_E1)) + sel(META_R1)
        p2 = lookup(sel(META_E2)) + sel(META_R2)
        pos.append(jnp.concatenate([p1, p2]).astype(jnp.int32))
        p1s.append(p1)
        p2s.append(p2)
        base = base + c
    pos_all = jnp.concatenate(p1s + p2s).astype(jnp.int32)
    pads = jnp.concatenate([starts + total, padded - total,
                            jnp.stack([ends[-1], n_tiles - ends[-1] // tmr])]).astype(jnp.int32)
    return pos, pos_all, pads, tile_expert, stages, tile_valid, n_rows


def _pad_to(a, shape):
    return jnp.pad(a, [(0, t - s) for s, t in zip(a.shape, shape)])


def _regroup_chunks(a, cb):
    r = a.shape[0]
    return a.reshape(r, 3, HY_W // cb, cb).transpose(2, 0, 1, 3).reshape(HY_W // cb, r, 3 * cb)


def kernel(x_prompt, x_sample, state_l0_lru, cache_l1_ckv, cache_l1_krope, c, c_ctx, l0_norm1, l0_norm2, l0_w_mod, l0_b_mod, l0_w_in, l0_conv_a, l0_lru_conv_w, l0_lru_conv_b, l0_lru_wa, l0_lru_ba, l0_lru_wi, l0_lru_bi, l0_lru_lambda, l0_w_out, l0_ffn_gate, l0_ffn_up, l0_ffn_down, l1_norm1, l1_norm2, l1_w_mod, l1_b_mod, l1_w_in, l1_q_norm, l1_kv_norm, l1_w_q_up, l1_w_kv_up, l1_hy_short_w, l1_hy_short_b, l1_hy_f_w1, l1_hy_f_b1, l1_hy_f_w2, l1_hy_f_b2, l1_hy_f_w3, l1_hy_bias, l1_w_out, l1_router_w, l1_router_b, l1_exp_gate, l1_exp_up, l1_exp_down, final_norm):
    batch, seq, _ = x_prompt.shape
    dec_batch, dec_seq, _ = x_sample.shape
    past_len = cache_l1_ckv.shape[1]

    cond8 = jnp.concatenate([c_ctx[None, :], c, jnp.zeros((V7X_SUBLANES - 1 - dec_batch, D), F32)], axis=0)
    wcat = jnp.concatenate([l0_lru_wa[0], l0_lru_wi[0], l0_lru_wa[1], l0_lru_wi[1]], axis=-1)
    hid = l1_hy_f_w2.shape[0]
    filter_pack = jnp.concatenate([_pad_to(l1_hy_f_w1, (V7X_LANES, hid)), l1_hy_f_b1[None, :], l1_hy_f_b2[None, :],
                                   jnp.zeros((V7X_SUBLANES - 2, hid), F32)], axis=0)
    short_w = _regroup_chunks(l1_hy_short_w, HY_CB)
    short_b = _regroup_chunks(l1_hy_short_b.reshape(1, -1), HY_CB)
    hy_bias = l1_hy_bias.reshape(1, HY_W)
    router_pack = _pad_to(jnp.concatenate([l1_router_w, l1_router_b[None, :]], axis=0),
                          (D + V7X_SUBLANES, V7X_LANES))

    mod0, mod1 = adaln_tables(cond8, ((l0_w_mod, l0_b_mod), (l1_w_mod, l1_b_mod)))

    kv_ctx = kv_up(cache_l1_ckv.reshape(dec_batch * past_len, KV_RANK), l1_w_kv_up)
    kr_ctx = cache_l1_krope.reshape(dec_batch * past_len, ROPE)

    conds = ((0, batch * seq), (1, dec_seq))
    seq_lens = (seq, dec_seq)
    xs = (x_prompt.reshape(batch * seq, D), x_sample.reshape(dec_batch * dec_seq, D))
    h0s = (jnp.zeros((batch, 2, LRU_W), F32), state_l0_lru)

    us = in0_proj(xs, l0_norm1, mod0, conds, l0_w_in)
    parts, lru_states = [], []
    for u, seq_len, h0 in zip(us, seq_lens, h0s):
        ya = conv_a(u, seq_len, l0_conv_a)
        yb, lru_state = rglru(u, seq_len, l0_lru_conv_w, l0_lru_conv_b, wcat, l0_lru_ba, l0_lru_bi,
                              l0_lru_lambda, h0)
        parts.append([(ya, 0), (yb, 0), (yb, 1)])
        lru_states.append(lru_state)
    xs = mix_ffn(parts, l0_w_out, xs, l0_norm2, mod0, conds, l0_ffn_gate, l0_ffn_up, l0_ffn_down)
    new_lru = lru_states[0]

    def layer1(x, seq_len, cond, latent):
        qnope, qpe, ckv, kr, kv, uh = in1_proj(x, l1_norm1, mod1, cond, l1_w_in, l1_q_norm, l1_kv_norm,
                                               l1_w_q_up, l1_w_kv_up)
        if latent:
            yc = attn_lat(qnope, qpe, kv_ctx, kr_ctx, kv, kr, seq_len, past_len)
        else:
            yc = attn_ctx(qnope, qpe, kv, kr, seq_len)
        cs = dft_tables(seq_len)
        k_r, k_s, k_ny = hy_filter(cs, filter_pack, l1_hy_f_w2, l1_hy_f_w3)
        yd = hyena(uh, seq_len, short_w, short_b, cs, k_r, k_s, k_ny, hy_bias)
        routed = mix_route([(yc, 0), (yc, 1), (yd, 0)], l1_w_out, x, l1_norm2, mod1, cond, router_pack)
        return routed, ckv, kr

    r_p, new_ckv, new_kr = layer1(xs[0], seq, conds[0], latent=False)
    r_s, _, _ = layer1(xs[1], dec_seq, conds[1], latent=True)

    routed = (r_p, r_s)
    pos, pos_all, pads, tile_expert, stages, tile_valid, n_rows = moe_plan([r[3] for r in routed],
                                                                          [r[4] for r in routed])
    hs = moe_dispatch(pos_all, pads, n_rows, r_p[1], r_s[1])
    y_rows = moe_experts(tile_expert, stages, tile_valid, hs, l1_exp_gate, l1_exp_up, l1_exp_down)
    y_p, y_s = [moe_combine(p, r[0], r[2], mod1, cond, final_norm, y_rows)
                for p, r, cond in zip(pos, routed, conds)]
    return (y_p.reshape(batch, seq, D), y_s.reshape(dec_batch, dec_seq, D), new_lru,
            new_ckv.reshape(batch, seq, KV_RANK), new_kr.reshape(batch, seq, ROPE))
```

```python
import functools
import math

import jax
import jax.numpy as jnp
from jax import lax
from jax.experimental import pallas as pl
from jax.experimental.pallas import tpu as pltpu

F32 = jnp.float32
BF16 = jnp.bfloat16

D = 1024
GRID_W = 64
EPS = 1e-6
CONV_W = 512
LRU_W = 1024
LRU_BW = 128
LRU_C = 8.0
MLA_HEADS = 8
Q_RANK = 384
KV_RANK = 256
NOPE = 128
ROPE = 64
VDIM = 128
QK_DIM = NOPE + ROPE
ROPE_THETA = 10000.0
HY_W = 512
HY_BANDS = 16
HY_TARGET = 1e-2
HY_FAST_DECAY = 0.3
HY_SLOW_DECAY = 1.5
D_FF = 2816
N_EXPERTS = 8
D_FF_EXPERT = 1408
IN0 = 3 * CONV_W + 2 * LRU_W
IN1 = Q_RANK + KV_RANK + ROPE + 3 * HY_W

V7X_LANES = 128
V7X_SUBLANES = 8
V7X_VMEM_LIMIT_BYTES = 56 * 1024 * 1024
V7X_VMEM_LIMIT_LARGE_BYTES = 60 * 1024 * 1024

TM = 512
TN_IN0 = 512
TK_IN0 = 256
TF_FFN = 256
MIX_SLAB = 512
MOE_CHUNK = 256
TM_ROUTE = 512
TM_EXPERT = 512
TM_COMBINE = 512
LRU_CB = 512
HY_CB = 512
TQ = 256
ATTN_CTX_SEQS = 4
CONV_A_ROWS = 1024
LRU_ROWS = 1024
HY_ROWS = 1024
TM_IN1 = 512


def _cparams(sem, vmem_limit_bytes=V7X_VMEM_LIMIT_BYTES):
    return pltpu.CompilerParams(dimension_semantics=sem, vmem_limit_bytes=vmem_limit_bytes)


def _sigmoid(x):
    return 0.5 * jnp.tanh(0.5 * x) + 0.5


def _silu(x):
    return x * _sigmoid(x)


def _norm_mod(x, g, shift, scale):
    ms = jnp.mean(x * x, axis=-1, keepdims=True)
    y = x * lax.rsqrt(ms + EPS) * g
    return y * (1.0 + scale) + shift


def _mod_spec(comp, cond, tm, width, col_fn, tile_fn=lambda *ids: ids[0]):
    row0, seg = cond
    assert seg % tm == 0
    return pl.BlockSpec((None, 1, width),
                        lambda *ids: (comp * 3 + row0 + (tile_fn(*ids) * tm) // seg, 0, col_fn(*ids)))


def _dot3(a, b):
    a_hi = a.astype(BF16)
    a_lo = (a - a_hi.astype(F32)).astype(BF16)
    b_hi = b.astype(BF16)
    b_lo = (b - b_hi.astype(F32)).astype(BF16)
    n = a.shape[0]
    y = jnp.dot(jnp.concatenate([a_hi, a_lo], axis=0), b_hi, preferred_element_type=F32)
    return y[:n] + y[n:] + jnp.dot(a_hi, b_lo, preferred_element_type=F32)


def _adaln_kernel(c_ref, w0_ref, b0_ref, w1_ref, b1_ref, o_ref):
    a = _silu(c_ref[...])
    for layer, (w_ref, b_ref) in enumerate(((w0_ref, b0_ref), (w1_ref, b1_ref))):
        @pl.when(pl.program_id(0) == layer)
        def _():
            o_ref[...] = _dot3(a, w_ref[...]) + b_ref[...]


def adaln_tables(cond8, mods):
    tn = 1536
    nj = 6 * D // tn
    (w0, b0), (w1, b1) = mods
    at0 = lambda l, j: (0, jnp.where(l == 0, j, nj - 1))
    at1 = lambda l, j: (0, jnp.where(l == 1, j, 0))
    m = pl.pallas_call(
        _adaln_kernel,
        out_shape=jax.ShapeDtypeStruct((2, V7X_SUBLANES, 6 * D), F32),
        grid=(2, nj),
        in_specs=[pl.BlockSpec((V7X_SUBLANES, D), lambda l, j: (0, 0)),
                  pl.BlockSpec((D, tn), at0), pl.BlockSpec((1, tn), at0),
                  pl.BlockSpec((D, tn), at1), pl.BlockSpec((1, tn), at1)],
        out_specs=pl.BlockSpec((None, V7X_SUBLANES, tn), lambda l, j: (l, 0, j)),
        compiler_params=_cparams(("arbitrary", "arbitrary")),
        name="adaln",
    )(cond8, w0, b0.reshape(1, 6 * D), w1, b1.reshape(1, 6 * D))
    return [m[l, :3].reshape(3, 6, D).transpose(1, 0, 2).reshape(18, 1, D) for l in range(2)]


def _tile_of(n_load):
    return lambda s: jnp.maximum(s - n_load, 0)


def _block_of(n_load):
    return lambda s: jnp.minimum(s, n_load - 1)


class _TwoSets:
    def __init__(self, n_load, tm, tokens, conds):
        self.n_load, self.tm, self.conds = n_load, tm, conds
        self.n_a, self.n_b = tokens[0] // tm, tokens[1] // tm
        self.steps = n_load + self.n_a + self.n_b

    def tile(self, s):
        return jnp.maximum(s - self.n_load, 0)

    def in_first(self, s):
        return s - self.n_load < self.n_a

    def idx_a(self, s):
        return jnp.minimum(self.tile(s), self.n_a - 1)

    def idx_b(self, s):
        return jnp.clip(self.tile(s) - self.n_a, 0, self.n_b - 1)

    def rows(self, width):
        return (pl.BlockSpec((self.tm, width), lambda s: (self.idx_a(s), 0)),
                pl.BlockSpec((self.tm, width), lambda s: (self.idx_b(s), 0)))

    def cols(self, width, col):
        return (pl.BlockSpec((self.tm, width), lambda s: (self.idx_a(s), col)),
                pl.BlockSpec((self.tm, width), lambda s: (self.idx_b(s), col)))

    def mod_spec(self, comp):
        (row_a, seg_a), (row_b, seg_b) = self.conds
        assert seg_a % self.tm == 0 and seg_b % self.tm == 0

        def row(s):
            return jnp.where(self.in_first(s), row_a + (self.idx_a(s) * self.tm) // seg_a,
                             row_b + (self.idx_b(s) * self.tm) // seg_b)

        return pl.BlockSpec((None, 1, D), lambda s: (comp * 3 + row(s), 0, 0))


def _in0_kernel(xa_ref, xb_ref, g_ref, sh_ref, sc_ref, w_ref, oa_ref, ob_ref, w_sc, *, n_a, tn):
    s = pl.program_id(0)
    tk = w_ref.shape[0]
    n_load = w_sc.shape[0] // tk

    @pl.when(s < n_load)
    def _():
        w_sc[pl.ds(pl.multiple_of(s * tk, tk), tk), :] = w_ref[...].astype(BF16)

    def tile(x_ref, o_ref):
        h = _norm_mod(x_ref[...], g_ref[...], sh_ref[...], sc_ref[...]).astype(BF16)
        for j in range(w_sc.shape[1] // tn):
            o_ref[:, j * tn:(j + 1) * tn] = jnp.dot(h, w_sc[:, j * tn:(j + 1) * tn],
                                                    preferred_element_type=F32).astype(BF16)

    @pl.when(jnp.logical_and(s >= n_load, s - n_load < n_a))
    def _():
        tile(xa_ref, oa_ref)

    @pl.when(s - n_load >= n_a)
    def _():
        tile(xb_ref, ob_ref)


def in0_proj(xs, g, modtab, conds, w_in):
    n = w_in.shape[1]
    n_load = D // TK_IN0
    ts = _TwoSets(n_load, TM, [x.shape[0] for x in xs], conds)
    blk = _block_of(n_load)
    return pl.pallas_call(
        functools.partial(_in0_kernel, n_a=ts.n_a, tn=TN_IN0),
        out_shape=tuple(jax.ShapeDtypeStruct((x.shape[0], n), BF16) for x in xs),
        grid=(ts.steps,),
        in_specs=[*ts.rows(D),
                  pl.BlockSpec((1, D), lambda s: (0, 0)),
                  ts.mod_spec(0), ts.mod_spec(1),
                  pl.BlockSpec((TK_IN0, n), lambda s: (blk(s), 0))],
        out_specs=ts.rows(n),
        scratch_shapes=[pltpu.VMEM((D, n), BF16)],
        compiler_params=_cparams(("arbitrary",)),
        name="in0_proj",
    )(*xs, g.reshape(1, D), modtab, modtab, w_in)


def _shift_rows(v, d, t, seq_len=None):
    n = v.shape[0]
    seq_len = n if seq_len is None else seq_len
    if d > 0:
        return jnp.where(t < d, 0.0, pltpu.roll(v, d, 0))
    return jnp.where(t >= seq_len + d, 0.0, pltpu.roll(v, n + d, 0))


def _conv_a_kernel(b_ref, c_ref, x_ref, w_ref, o_ref, *, seq_len):
    v = c_ref[...].astype(F32) * x_ref[...].astype(F32)
    t = lax.broadcasted_iota(jnp.int32, v.shape, 0) & (seq_len - 1)
    w = w_ref[...]
    y = w[0:1] * _shift_rows(v, 1, t, seq_len) + w[1:2] * v + w[2:3] * _shift_rows(v, -1, t, seq_len)
    o_ref[...] = (b_ref[...].astype(F32) * y).astype(o_ref.dtype)


def conv_a(u, seq_len, conv_w):
    tokens = u.shape[0]
    rows = max(seq_len, CONV_A_ROWS)
    assert seq_len & (seq_len - 1) == 0 and rows % seq_len == 0
    return pl.pallas_call(
        functools.partial(_conv_a_kernel, seq_len=seq_len),
        out_shape=jax.ShapeDtypeStruct((tokens, CONV_W), BF16),
        grid=(tokens // rows,),
        in_specs=[pl.BlockSpec((rows, CONV_W), lambda s: (s, 0)),
                  pl.BlockSpec((rows, CONV_W), lambda s: (s, 1)),
                  pl.BlockSpec((rows, CONV_W), lambda s: (s, 2)),
                  pl.BlockSpec((3, CONV_W), lambda s: (0, 0))],
        out_specs=pl.BlockSpec((rows, CONV_W), lambda s: (s, 0)),
        compiler_params=_cparams(("parallel",)),
        name="conv_a",
    )(u, u, u, conv_w)


def _group_scan(a_sc, b_sc, k, reverse):
    planes = a_sc.shape[1] // V7X_SUBLANES
    order = range(V7X_SUBLANES - 1, -1, -1) if reverse else range(V7X_SUBLANES)
    a_acc = b_acc = None
    for r in order:
        plane = (k, pl.ds(r, planes, stride=V7X_SUBLANES), slice(None))
        a_r, b_r = a_sc[plane], b_sc[plane]
        if a_acc is None:
            a_acc, b_acc = a_r, b_r
        else:
            b_acc = a_r * b_acc + b_r
            a_acc = a_r * a_acc
            a_sc[plane] = a_acc
            b_sc[plane] = b_acc


def _rglru_kernel(gate_ref, xb_ref, cw_ref, cb_ref, wcat_ref, ba_ref, bi_ref, lam_ref, h0_ref,
                  y_ref, st_ref, af_sc, bf_sc, ab_sc, bb_sc, hf_sc, hb_sc, *, seq_len):
    n, cb = xb_ref.shape
    n_seq = n // seq_len
    n_slab = cb // LRU_BW
    xb = xb_ref[...].astype(F32)
    t = lax.broadcasted_iota(jnp.int32, xb.shape, 0) & (seq_len - 1)
    cw = cw_ref[...]
    sh = lambda d: _shift_rows(xb, d, t, seq_len)
    xc = cb_ref[...] + cw[0:1] * sh(2) + cw[1:2] * sh(1) + cw[2:3] * xb + cw[3:4] * sh(-1)
    xcb = xc.astype(BF16)

    for k in range(n_slab):
        cols = slice(k * LRU_BW, (k + 1) * LRU_BW)
        gk = jnp.dot(xcb[:, cols], wcat_ref[k].astype(BF16), preferred_element_type=F32)
        for d, (a_sc, b_sc) in enumerate(((af_sc, bf_sc), (ab_sc, bb_sc))):
            ga = gk[:, (2 * d) * LRU_BW:(2 * d + 1) * LRU_BW]
            gi = gk[:, (2 * d + 1) * LRU_BW:(2 * d + 2) * LRU_BW]
            r = _sigmoid(ga + ba_ref[d:d + 1, cols])
            i = _sigmoid(gi + bi_ref[d:d + 1, cols])
            log_a = (-LRU_C * jax.nn.softplus(-lam_ref[d:d + 1, cols])) * r
            a = jnp.exp(log_a)
            m = 1.0 - a * a
            mult = m * lax.rsqrt(jnp.maximum(m, 1e-30))
            a_sc[k] = a
            b_sc[k] = mult * (i * xc[:, cols])
            _group_scan(a_sc, b_sc, k, reverse=(d == 1))

    ng = seq_len // V7X_SUBLANES
    bcast = lambda row: jnp.broadcast_to(row, (V7X_SUBLANES, LRU_BW))
    chains = [(q, k) for q in range(n_seq) for k in range(n_slab)]
    init = tuple((bcast(h0_ref[q, 0:1, k * LRU_BW:(k + 1) * LRU_BW]),
                  bcast(h0_ref[q, 1:2, k * LRU_BW:(k + 1) * LRU_BW])) for q, k in chains)

    def step(j, carry):
        out = []
        for (q, k), (hf_in, hb_in) in zip(chains, carry):
            rf = pl.ds(pl.multiple_of(q * seq_len + j * V7X_SUBLANES, V7X_SUBLANES), V7X_SUBLANES)
            rb = pl.ds(pl.multiple_of(q * seq_len + (ng - 1 - j) * V7X_SUBLANES, V7X_SUBLANES), V7X_SUBLANES)
            hf = af_sc[k, rf, :] * hf_in + bf_sc[k, rf, :]
            hb = ab_sc[k, rb, :] * hb_in + bb_sc[k, rb, :]
            hf_sc[k, rf, :] = hf
            hb_sc[k, rb, :] = hb
            out.append((bcast(hf[V7X_SUBLANES - 1:V7X_SUBLANES]), bcast(hb[0:1])))
        return tuple(out)

    final = lax.fori_loop(0, ng, step, init)
    for (q, k), (hf_last, hb_first) in zip(chains, final):
        st_ref[q, 0:1, k * LRU_BW:(k + 1) * LRU_BW] = hf_last[0:1]
        st_ref[q, 1:2, k * LRU_BW:(k + 1) * LRU_BW] = hb_first[0:1]

    gt = gate_ref[...].astype(F32)
    gelu = 0.5 * gt * (1.0 + jnp.tanh(math.sqrt(2.0 / math.pi) * (gt + 0.044715 * (gt * gt * gt))))
    h = jnp.concatenate([hf_sc[k] + hb_sc[k] for k in range(n_slab)], axis=1)
    y_ref[...] = (h * gelu).astype(y_ref.dtype)


def rglru(u, seq_len, conv_w, conv_b, wcat, ba, bi, lam, h0):
    tokens = u.shape[0]
    nseq = tokens // seq_len
    cb = LRU_CB
    rows = max(seq_len, LRU_ROWS)
    assert seq_len & (seq_len - 1) == 0 and rows % seq_len == 0
    per_blk = rows // seq_len
    gate_blk0 = 3 * CONV_W // cb
    xb_blk0 = (3 * CONV_W + LRU_W) // cb
    seq_scr = lambda: pltpu.VMEM((cb // LRU_BW, rows, LRU_BW), F32)
    return pl.pallas_call(
        functools.partial(_rglru_kernel, seq_len=seq_len),
        out_shape=(jax.ShapeDtypeStruct((tokens, LRU_W), BF16), jax.ShapeDtypeStruct((nseq, 2, LRU_W), F32)),
        grid=(tokens // rows, LRU_W // cb),
        in_specs=[pl.BlockSpec((rows, cb), lambda s, c: (s, gate_blk0 + c)),
                  pl.BlockSpec((rows, cb), lambda s, c: (s, xb_blk0 + c)),
                  pl.BlockSpec((4, cb), lambda s, c: (0, c)),
                  pl.BlockSpec((1, cb), lambda s, c: (0, c)),
                  pl.BlockSpec((cb // LRU_BW, LRU_BW, 4 * LRU_BW), lambda s, c: (c, 0, 0)),
                  pl.BlockSpec((2, cb), lambda s, c: (0, c)),
                  pl.BlockSpec((2, cb), lambda s, c: (0, c)),
                  pl.BlockSpec((2, cb), lambda s, c: (0, c)),
                  pl.BlockSpec((per_blk, 2, cb), lambda s, c: (s, 0, c))],
        out_specs=(pl.BlockSpec((rows, cb), lambda s, c: (s, c)),
                   pl.BlockSpec((per_blk, 2, cb), lambda s, c: (s, 0, c))),
        scratch_shapes=[seq_scr() for _ in range(6)],
        compiler_params=_cparams(("parallel", "parallel")),
        name="rglru",
    )(u, u, conv_w, conv_b.reshape(1, LRU_W), wcat, ba, bi, lam, h0)


def _mix_ffn_kernel(p0a_ref, p0b_ref, p1a_ref, p1b_ref, p2a_ref, p2b_ref, wo_ref, xa_ref, xb_ref,
                    g1_ref, g_ref, sh_ref, sc_ref, g2_ref, wg_ref, wu_ref, wd_ref, oa_ref, ob_ref,
                    wo_sc, wg_sc, wu_sc, wd_sc, *, n_a):
    s = pl.program_id(0)
    n_load = wg_sc.shape[0]
    n_out = wo_sc.shape[0]

    @pl.when(s < n_out)
    def _():
        wo_sc[s] = wo_ref[...].astype(BF16)

    @pl.when(s < n_load)
    def _():
        wg_sc[s] = wg_ref[...].astype(BF16)
        wu_sc[s] = wu_ref[...].astype(BF16)
        wd_sc[s] = wd_ref[...].astype(BF16)

    @pl.when(s >= n_load)
    def _():
        first = s - n_load < n_a
        pick = lambda a_ref, b_ref: jnp.where(first, a_ref[...], b_ref[...])
        m = jnp.dot(pick(p0a_ref, p0b_ref), wo_sc[0], preferred_element_type=F32)
        m += jnp.dot(pick(p1a_ref, p1b_ref), wo_sc[1], preferred_element_type=F32)
        m += jnp.dot(pick(p2a_ref, p2b_ref), wo_sc[2], preferred_element_type=F32)
        x = pick(xa_ref, xb_ref) + g1_ref[...] * m
        h = _norm_mod(x, g_ref[...], sh_ref[...], sc_ref[...]).astype(BF16)
        y = None
        for f in range(n_load):
            hg = jnp.dot(h, wg_sc[f], preferred_element_type=F32)
            hu = jnp.dot(h, wu_sc[f], preferred_element_type=F32)
            act = (_silu(hg) * hu).astype(BF16)
            yf = jnp.dot(act, wd_sc[f], preferred_element_type=F32)
            y = yf if y is None else y + yf
        out = x + g2_ref[...] * y

        @pl.when(first)
        def _():
            oa_ref[...] = out

        @pl.when(jnp.logical_not(first))
        def _():
            ob_ref[...] = out


def mix_ffn(parts, w_out, xs, g, modtab, conds, w_gate, w_up, w_down):
    tf = TF_FFN
    kb = MIX_SLAB
    n_load = D_FF // tf
    n_out = len(parts[0])
    assert n_out <= n_load
    ts = _TwoSets(n_load, TM, [x.shape[0] for x in xs], conds)
    blk = _block_of(n_load)
    oblk = _block_of(n_out)
    lhs_specs, lhs_args = [], []
    for (arr_a, col_a), (arr_b, col_b) in zip(*parts):
        assert col_a == col_b
        lhs_specs += ts.cols(kb, col_a)
        lhs_args += [arr_a, arr_b]
    return pl.pallas_call(
        functools.partial(_mix_ffn_kernel, n_a=ts.n_a),
        out_shape=tuple(jax.ShapeDtypeStruct(x.shape, F32) for x in xs),
        grid=(ts.steps,),
        in_specs=lhs_specs + [
            pl.BlockSpec((kb, D), lambda s: (oblk(s), 0)),
            *ts.rows(D),
            ts.mod_spec(2),
            pl.BlockSpec((1, D), lambda s: (0, 0)),
            ts.mod_spec(3), ts.mod_spec(4), ts.mod_spec(5),
            pl.BlockSpec((D, tf), lambda s: (0, blk(s))),
            pl.BlockSpec((D, tf), lambda s: (0, blk(s))),
            pl.BlockSpec((tf, D), lambda s: (blk(s), 0))],
        out_specs=ts.rows(D),
        scratch_shapes=[pltpu.VMEM((n_out, kb, D), BF16),
                        pltpu.VMEM((n_load, D, tf), BF16), pltpu.VMEM((n_load, D, tf), BF16),
                        pltpu.VMEM((n_load, tf, D), BF16)],
        compiler_params=_cparams(("arbitrary",), V7X_VMEM_LIMIT_LARGE_BYTES),
        name="mix_ffn",
    )(*lhs_args, w_out, *xs, modtab, g.reshape(1, D), modtab, modtab, modtab, w_gate, w_up, w_down)


def _rms(x, g):
    return x * lax.rsqrt(jnp.mean(x * x, axis=-1, keepdims=True) + EPS) * g


def _in1_kernel(x_ref, g_ref, sh_ref, sc_ref, w_ref, qn_ref, kvn_ref, wq_ref, wkv_ref,
                qnope_ref, qpe_ref, ckv_ref, kr_ref, kv_ref, uh_ref, w_sc, wq_sc, wkv_sc):
    @pl.when(pl.program_id(0) == 0)
    def _():
        w_sc[...] = w_ref[...].astype(BF16)
        for h in range(MLA_HEADS):
            c0 = h * QK_DIM
            wq_sc[:, h * NOPE:(h + 1) * NOPE] = wq_ref[:, c0:c0 + NOPE].astype(BF16)
            r0 = MLA_HEADS * NOPE + h * ROPE
            wq_sc[:, r0:r0 + ROPE] = wq_ref[:, c0 + NOPE:c0 + QK_DIM].astype(BF16)
        wkv_sc[...] = wkv_ref[...].astype(BF16)

    h = _norm_mod(x_ref[...], g_ref[...], sh_ref[...], sc_ref[...]).astype(BF16)
    u = lax.dot_general(h, w_sc[...], (((1,), (1,)), ((), ())), preferred_element_type=F32)
    o1, o2, o3 = Q_RANK, Q_RANK + KV_RANK, Q_RANK + KV_RANK + ROPE
    cq = _rms(u[:, :o1], qn_ref[...])
    q = jnp.dot(cq.astype(BF16), wq_sc[...], preferred_element_type=F32) * _SCALE
    qnope_ref[...] = q[:, :MLA_HEADS * NOPE].astype(qnope_ref.dtype)
    qpe_ref[...] = q[:, MLA_HEADS * NOPE:]
    ckv = _rms(u[:, o1:o2], kvn_ref[...])
    ckv_ref[...] = ckv
    kv_ref[...] = jnp.dot(ckv.astype(BF16), wkv_sc[...], preferred_element_type=F32).astype(kv_ref.dtype)
    kr_ref[...] = u[:, o2:o3]
    uh_ref[...] = u[:, o3:]


def in1_proj(x, g, modtab, cond, w_in, q_norm, kv_norm, w_q_up, w_kv_up):
    tokens = x.shape[0]
    tm = TM_IN1
    nkv = MLA_HEADS * (NOPE + VDIM)
    const = lambda i: (0, 0)
    zero = lambda i: 0
    once = pl.Buffered(1)
    outs = (jax.ShapeDtypeStruct((tokens, MLA_HEADS * NOPE), BF16),
            jax.ShapeDtypeStruct((tokens, MLA_HEADS * ROPE), F32),
            jax.ShapeDtypeStruct((tokens, KV_RANK), F32),
            jax.ShapeDtypeStruct((tokens, ROPE), F32),
            jax.ShapeDtypeStruct((tokens, nkv), BF16),
            jax.ShapeDtypeStruct((tokens, 3 * HY_W), F32))
    row = lambda w: pl.BlockSpec((tm, w), lambda i: (i, 0))
    return pl.pallas_call(
        _in1_kernel,
        out_shape=outs,
        grid=(tokens // tm,),
        in_specs=[row(D),
                  pl.BlockSpec((1, D), const),
                  _mod_spec(0, cond, tm, D, zero),
                  _mod_spec(1, cond, tm, D, zero),
                  pl.BlockSpec((IN1, D), const, pipeline_mode=once),
                  pl.BlockSpec((1, Q_RANK), const),
                  pl.BlockSpec((1, KV_RANK), const),
                  pl.BlockSpec((Q_RANK, MLA_HEADS * QK_DIM), const, pipeline_mode=once),
                  pl.BlockSpec((KV_RANK, nkv), const, pipeline_mode=once)],
        out_specs=tuple(row(o.shape[1]) for o in outs),
        scratch_shapes=[pltpu.VMEM((IN1, D), BF16), pltpu.VMEM((Q_RANK, MLA_HEADS * QK_DIM), BF16),
                        pltpu.VMEM((KV_RANK, nkv), BF16)],
        compiler_params=_cparams(("arbitrary",)),
        name="in1_proj",
    )(x, g.reshape(1, D), modtab, modtab, w_in.T, q_norm.reshape(1, Q_RANK), kv_norm.reshape(1, KV_RANK),
      w_q_up, w_kv_up)


def _mm_kernel(a_ref, w_ref, o_ref):
    o_ref[...] = jnp.dot(a_ref[...].astype(BF16), w_ref[...].astype(BF16),
                         preferred_element_type=F32).astype(o_ref.dtype)


def kv_up(ckv, w_kv_up):
    rows = ckv.shape[0]
    n = w_kv_up.shape[1]
    return pl.pallas_call(
        _mm_kernel,
        out_shape=jax.ShapeDtypeStruct((rows, n), BF16),
        grid=(rows // TM,),
        in_specs=[pl.BlockSpec((TM, KV_RANK), lambda i: (i, 0)), pl.BlockSpec((KV_RANK, n), lambda i: (0, 0))],
        out_specs=pl.BlockSpec((TM, n), lambda i: (i, 0)),
        compiler_params=_cparams(("parallel",)),
        name="kv_up",
    )(ckv, w_kv_up)


_NT = (((1,), (1,)), ((), ()))
_SCALE = 1.0 / math.sqrt(QK_DIM)


def _fill_rope_tables(cos_ref, sin_ref):
    n, width = cos_ref.shape
    n_grid_rows = n // GRID_W
    n_freq = ROPE // 4

    def trig(count):
        lane = lax.broadcasted_iota(jnp.int32, (count, width), 1)
        j = lane & (ROPE // 2 - 1)
        inv = jnp.exp((j & (n_freq - 1)).astype(F32) * (-math.log(ROPE_THETA) / n_freq))
        ang = lax.broadcasted_iota(jnp.int32, (count, width), 0).astype(F32) * inv
        return jnp.cos(ang), jnp.sin(ang), j < n_freq

    cos_c, sin_c, by_row = trig(GRID_W)
    cos_r, sin_r, _ = trig(n_grid_rows)
    for r in range(n_grid_rows):
        rows = slice(r * GRID_W, (r + 1) * GRID_W)
        cos_ref[rows, :] = jnp.where(by_row, jnp.broadcast_to(cos_r[r:r + 1], cos_c.shape), cos_c)
        sin_ref[rows, :] = jnp.where(by_row, jnp.broadcast_to(sin_r[r:r + 1], sin_c.shape), sin_c)


def _rope(x, cos, sin):
    width = x.shape[1]
    lane = lax.broadcasted_iota(jnp.int32, x.shape, 1)
    first_half = (lane & (ROPE - 1)) < ROPE // 2
    xr = jnp.where(first_half, -pltpu.roll(x, width - ROPE // 2, 1), pltpu.roll(x, ROPE // 2, 1))
    return x * cos + xr * sin


def _ones_column(n):
    lane = lax.broadcasted_iota(jnp.int32, (n, VDIM), 1)
    return jnp.where(lane == 0, 1.0, 0.0).astype(BF16)


def _head_attention(qcat, kcat, vaug):
    s = lax.dot_general(qcat, kcat, _NT, preferred_element_type=F32)
    p = jnp.exp(s - jnp.max(s, axis=-1, keepdims=True)).astype(BF16)
    oa = jnp.dot(p, vaug, preferred_element_type=F32)
    return oa[:, :VDIM] / oa[:, VDIM:VDIM + 1]


def _attn_ctx_kernel(qn_ref, qpe_ref, kv_ref, kr_ref, o_ref, *, seq_len):
    n = qn_ref.shape[0]
    n_seq = n // seq_len
    ones = _ones_column(n)
    kpe = kr_ref[...].astype(BF16)
    per_seq = lambda a: a.reshape(n_seq, seq_len, a.shape[-1])
    for h in range(MLA_HEADS):
        c0 = h * (NOPE + VDIM)
        qcat = per_seq(jnp.concatenate([qn_ref[:, h * NOPE:(h + 1) * NOPE],
                                        qpe_ref[:, h * ROPE:(h + 1) * ROPE].astype(BF16)], axis=1))
        kcat = per_seq(jnp.concatenate([kv_ref[:, c0:c0 + NOPE], kpe], axis=1))
        vaug = per_seq(jnp.concatenate([kv_ref[:, c0 + NOPE:c0 + NOPE + VDIM], ones], axis=1))
        s = jnp.einsum("bqd,bkd->bqk", qcat, kcat, preferred_element_type=F32)
        p = jnp.exp(s - jnp.max(s, axis=-1, keepdims=True)).astype(BF16)
        oa = jnp.einsum("bqk,bkd->bqd", p, vaug, preferred_element_type=F32)
        o = oa[:, :, :VDIM] / oa[:, :, VDIM:VDIM + 1]
        o_ref[:, h * VDIM:(h + 1) * VDIM] = o.reshape(n, VDIM).astype(o_ref.dtype)


def attn_ctx(qnope, qpe, kv, kr, seq_len):
    tokens = qnope.shape[0]
    rows = ATTN_CTX_SEQS * seq_len
    blk = lambda w: pl.BlockSpec((rows, w), lambda s: (s, 0))
    return pl.pallas_call(
        functools.partial(_attn_ctx_kernel, seq_len=seq_len),
        out_shape=jax.ShapeDtypeStruct((tokens, MLA_HEADS * VDIM), BF16),
        grid=(tokens // rows,),
        in_specs=[blk(MLA_HEADS * NOPE), blk(MLA_HEADS * ROPE), blk(MLA_HEADS * (NOPE + VDIM)), blk(ROPE)],
        out_specs=blk(MLA_HEADS * VDIM),
        compiler_params=_cparams(("parallel",)),
        name="attn_ctx",
    )(qnope, qpe, kv, kr)


def _attn_lat_kernel(qn_ref, qpe_ref, kvc_ref, krc_ref, kvl_ref, krl_ref, o_ref, kcat_sc, vaug_sc, cos_sc, sin_sc):
    tq = qn_ref.shape[0]
    n_ctx = krc_ref.shape[0]
    n_lat = krl_ref.shape[0]

    @pl.when(pl.program_id(1) == 0)
    def _():
        _fill_rope_tables(cos_sc, sin_sc)
        kr2 = jnp.concatenate([krl_ref[...], krl_ref[...]], axis=1)
        kpe_lat = _rope(kr2, cos_sc[...], sin_sc[...])[:, :ROPE].astype(BF16)
        kpe_ctx = krc_ref[...].astype(BF16)
        ones_c, ones_l = _ones_column(n_ctx), _ones_column(n_lat)
        for h in range(MLA_HEADS):
            c0 = h * (NOPE + VDIM)
            for r0, nr, kv_ref, kpe, ones in ((0, n_ctx, kvc_ref, kpe_ctx, ones_c), (n_ctx, n_lat, kvl_ref, kpe_lat, ones_l)):
                kcat_sc[h, r0:r0 + nr, 0:NOPE] = kv_ref[:, c0:c0 + NOPE]
                kcat_sc[h, r0:r0 + nr, NOPE:QK_DIM] = kpe
                vaug_sc[h, r0:r0 + nr, 0:VDIM] = kv_ref[:, c0 + NOPE:c0 + NOPE + VDIM]
                vaug_sc[h, r0:r0 + nr, VDIM:2 * VDIM] = ones

    q0 = pl.multiple_of(pl.program_id(1) * tq, tq)
    rep = lambda a: jnp.concatenate([a] * (MLA_HEADS // 2), axis=1)
    qp_all = _rope(qpe_ref[...], rep(cos_sc[pl.ds(q0, tq), :]), rep(sin_sc[pl.ds(q0, tq), :])).astype(BF16)
    for h in range(MLA_HEADS):
        qcat = jnp.concatenate([qn_ref[:, h * NOPE:(h + 1) * NOPE], qp_all[:, h * ROPE:(h + 1) * ROPE]], axis=1)
        o_ref[:, h * VDIM:(h + 1) * VDIM] = _head_attention(qcat, kcat_sc[h], vaug_sc[h]).astype(o_ref.dtype)


def attn_lat(qnope, qpe, kv_ctx, kr_ctx, kv_lat, kr_lat, seq_len, ctx_len):
    tokens = qnope.shape[0]
    nq = seq_len // TQ
    qblk = lambda w: pl.BlockSpec((TQ, w), lambda b, i: (b * nq + i, 0))
    seq = lambda n, w: pl.BlockSpec((n, w), lambda b, i: (b, 0))
    nkv = MLA_HEADS * (NOPE + VDIM)
    n_keys = ctx_len + seq_len
    return pl.pallas_call(
        _attn_lat_kernel,
        out_shape=jax.ShapeDtypeStruct((tokens, MLA_HEADS * VDIM), BF16),
        grid=(tokens // seq_len, nq),
        in_specs=[qblk(MLA_HEADS * NOPE), qblk(MLA_HEADS * ROPE), seq(ctx_len, nkv), seq(ctx_len, ROPE),
                  seq(seq_len, nkv), seq(seq_len, ROPE)],
        out_specs=qblk(MLA_HEADS * VDIM),
        scratch_shapes=[pltpu.VMEM((MLA_HEADS, n_keys, QK_DIM), BF16),
                        pltpu.VMEM((MLA_HEADS, n_keys, 2 * VDIM), BF16),
                        pltpu.VMEM((seq_len, 2 * ROPE), F32), pltpu.VMEM((seq_len, 2 * ROPE), F32)],
        compiler_params=_cparams(("parallel", "arbitrary")),
        name="attn_lat",
    )(qnope, qpe, kv_ctx, kr_ctx, kv_lat, kr_lat)


def _dft_kernel(o_ref):
    tr, n = o_ref.shape[1], o_ref.shape[2]
    nb = n // V7X_LANES
    f = pl.program_id(0) * tr + lax.broadcasted_iota(jnp.int32, (tr, V7X_LANES), 0)
    j = lax.broadcasted_iota(jnp.int32, (tr, V7X_LANES), 1)

    def cos_sin(m):
        ang = (m & (2 * n - 1)).astype(F32) * (math.pi / n)
        return jnp.cos(ang), jnp.sin(ang)

    cj, sj = cos_sin(f * j)
    cb, sb = cos_sin(f * (j * V7X_LANES))
    for b in range(nb):
        cbb, sbb = cb[:, b:b + 1], sb[:, b:b + 1]
        cols = slice(b * V7X_LANES, (b + 1) * V7X_LANES)
        o_ref[0, :, cols] = (cbb * cj - sbb * sj).astype(o_ref.dtype)
        o_ref[1, :, cols] = (sbb * cj + cbb * sj).astype(o_ref.dtype)


def dft_tables(n):
    tr = 128
    return pl.pallas_call(
        _dft_kernel,
        out_shape=jax.ShapeDtypeStruct((2, n, n), BF16),
        grid=(n // tr,),
        out_specs=pl.BlockSpec((2, tr, n), lambda i: (0, i, 0)),
        compiler_params=_cparams(("parallel",)),
        name="dft_tables",
    )()


def _split_dot(table, x):
    hi = x.astype(BF16)
    lo = (x - hi.astype(F32)).astype(BF16)
    return (jnp.dot(table, hi, preferred_element_type=F32) + jnp.dot(table, lo, preferred_element_type=F32))


def _hy_filter_kernel(cs_ref, pack_ref, w2_ref, w3_ref, kr_ref, ks_ref, kny_ref):
    n = cs_ref.shape[1]
    row = lax.broadcasted_iota(jnp.int32, (n, V7X_LANES), 0).astype(F32)
    lane = lax.broadcasted_iota(jnp.int32, (n, V7X_LANES), 1)
    t = row * (1.0 / (n - 1))
    w = (2.0 * math.pi) * row / n
    band = jnp.where(lane <= HY_BANDS, lane - 1, lane - 1 - HY_BANDS).astype(F32)
    freq = 1e-4 + band * ((HY_BANDS - 1 - 1e-4) / (HY_BANDS - 1))
    arg = jnp.where(lane <= HY_BANDS, freq * w + 0.5 * math.pi, -(freq * w))
    z = jnp.where(lane == 0, t, jnp.where(lane <= 2 * HY_BANDS, jnp.sin(arg), 0.0))
    hid = jnp.sin(_dot3(z, pack_ref[0:V7X_LANES, :]) + pack_ref[V7X_LANES:V7X_LANES + 1, :])
    hid = jnp.sin(_dot3(hid, w2_ref[...]) + pack_ref[V7X_LANES + 1:V7X_LANES + 2, :])
    hf = _dot3(hid, w3_ref[...])

    rowc = lax.broadcasted_iota(jnp.int32, (n, HY_W), 0)
    chan = lax.broadcasted_iota(jnp.int32, (n, HY_W), 1).astype(F32)
    max_decay = math.log(HY_TARGET) / HY_FAST_DECAY
    min_decay = math.log(HY_TARGET) / HY_SLOW_DECAY
    deltas = min_decay + chan * ((max_decay - min_decay) / (HY_W - 1))
    decay = jnp.exp(-(rowc.astype(F32) * (1.0 / (n - 1))) * jnp.abs(deltas))
    h_fwd = hf[:, :HY_W] * decay
    h_bwd = jnp.where(rowc == 0, 0.0, hf[:, HY_W:] * decay)
    norm = jnp.sum(jnp.abs(h_fwd) + jnp.abs(h_bwd), axis=0, keepdims=True)
    even = (h_fwd + h_bwd) / norm
    odd = (h_fwd - h_bwd) / norm
    cf = jnp.where(rowc == 0, 1.0, 2.0) * (1.0 / (2 * n))
    kr_ref[...] = cf * _split_dot(cs_ref[0], even)
    ks_ref[...] = cf * _split_dot(cs_ref[1], odd)
    sgn = jnp.where((rowc & 1) == 1, -1.0, 1.0)
    kny_ref[...] = jnp.sum(sgn * even, axis=0, keepdims=True) * (1.0 / (2 * n))


def hy_filter(cs, pack, w2, w3):
    n = cs.shape[1]
    full = lambda a: pl.BlockSpec(a.shape, lambda: (0,) * a.ndim)
    args = (cs, pack, w2, w3)
    return pl.pallas_call(
        _hy_filter_kernel,
        out_shape=(jax.ShapeDtypeStruct((n, HY_W), F32), jax.ShapeDtypeStruct((n, HY_W), F32),
                   jax.ShapeDtypeStruct((1, HY_W), F32)),
        in_specs=[full(a) for a in args],
        out_specs=(pl.BlockSpec((n, HY_W), lambda: (0, 0)), pl.BlockSpec((n, HY_W), lambda: (0, 0)),
                   pl.BlockSpec((1, HY_W), lambda: (0, 0))),
        compiler_params=pltpu.CompilerParams(vmem_limit_bytes=V7X_VMEM_LIMIT_BYTES),
        name="hy_filter",
    )(*args)


def _hyena_kernel(u0_ref, u1_ref, u2_ref, sw_ref, sb_ref, cs_ref, kr_ref, ks_ref, kny_ref, bias_ref, o_ref,
                  *, seq_len):
    n, cb = u0_ref.shape
    n_seq = n // seq_len
    t = lax.broadcasted_iota(jnp.int32, (n, cb), 0) & (seq_len - 1)

    def short_conv(u_ref, k):
        u = u_ref[...]
        w = sw_ref[:, k * cb:(k + 1) * cb]
        return (sb_ref[:, k * cb:(k + 1) * cb] + w[0:1] * _shift_rows(u, 1, t, seq_len) + w[1:2] * u
                + w[2:3] * _shift_rows(u, -1, t, seq_len))

    x0 = short_conv(u0_ref, 0)
    z = short_conv(u1_ref, 1) * short_conv(u2_ref, 2)
    wide = lambda a: jnp.concatenate([a[q * seq_len:(q + 1) * seq_len] for q in range(n_seq)], axis=1)
    rep = lambda a: jnp.concatenate([a] * n_seq, axis=1)
    zw = wide(z)
    zb = zw.astype(BF16)
    c, s = cs_ref[0], cs_ref[1]
    ur = jnp.dot(c, zb, preferred_element_type=F32)
    us = jnp.dot(s, zb, preferred_element_type=F32)
    sgn = jnp.where((lax.broadcasted_iota(jnp.int32, zw.shape, 0) & 1) == 1, -1.0, 1.0)
    uny = jnp.sum(sgn * zw, axis=0, keepdims=True)
    kr, ks = rep(kr_ref[...]), rep(ks_ref[...])
    yr = (ur * kr - us * ks).astype(BF16)
    ys = (ur * ks + us * kr).astype(BF16)
    yw = jnp.dot(c, yr, preferred_element_type=F32) + jnp.dot(s, ys, preferred_element_type=F32)
    yw = yw + sgn * (uny * rep(kny_ref[...]))
    y = jnp.concatenate([yw[:, q * cb:(q + 1) * cb] for q in range(n_seq)], axis=0)
    o_ref[...] = (x0 * (y + bias_ref[...] * z)).astype(o_ref.dtype)


def hyena(uh, seq_len, short_w, short_b, cs, kr, ks, kny, bias):
    tokens = uh.shape[0]
    cb = HY_CB
    nc = HY_W // cb
    rows = max(seq_len, HY_ROWS)
    assert seq_len & (seq_len - 1) == 0 and rows % seq_len == 0
    ublk = lambda k: pl.BlockSpec((rows, cb), lambda s, c: (s, k * nc + c))
    chan = lambda r: pl.BlockSpec((r, cb), lambda s, c: (0, c))
    return pl.pallas_call(
        functools.partial(_hyena_kernel, seq_len=seq_len),
        out_shape=jax.ShapeDtypeStruct((tokens, HY_W), BF16),
        grid=(tokens // rows, nc),
        in_specs=[ublk(0), ublk(1), ublk(2),
                  pl.BlockSpec((None, 3, 3 * cb), lambda s, c: (c, 0, 0)),
                  pl.BlockSpec((None, 1, 3 * cb), lambda s, c: (c, 0, 0)),
                  pl.BlockSpec((2, seq_len, seq_len), lambda s, c: (0, 0, 0)),
                  chan(seq_len), chan(seq_len), chan(1), chan(1)],
        out_specs=pl.BlockSpec((rows, cb), lambda s, c: (s, c)),
        compiler_params=_cparams(("parallel", "parallel")),
        name="hyena",
    )(uh, uh, uh, short_w, short_b, cs, kr, ks, kny, bias)


META_E1, META_E2, META_R1, META_R2, META_G1, META_G2 = range(6)


def _route_kernel(p0_ref, p1_ref, p2_ref, wo_ref, x_ref, g1_ref, g_ref, sh_ref, sc_ref, wr_ref,
                  x1_ref, h_ref, meta_ref, meta_t_ref, cnt_ref, run_sc, wo_sc):
    tm = x_ref.shape[0]
    lane = lax.broadcasted_iota(jnp.int32, (tm, V7X_LANES), 1)

    @pl.when(pl.program_id(0) == 0)
    def _():
        run_sc[...] = jnp.zeros_like(run_sc)
        wo_sc[...] = wo_ref[...].astype(BF16)

    kb = p0_ref.shape[1]
    m = jnp.dot(p0_ref[...], wo_sc[0:kb, :], preferred_element_type=F32)
    m += jnp.dot(p1_ref[...], wo_sc[kb:2 * kb, :], preferred_element_type=F32)
    m += jnp.dot(p2_ref[...], wo_sc[2 * kb:3 * kb, :], preferred_element_type=F32)
    x1 = x_ref[...] + g1_ref[...] * m
    x1_ref[...] = x1
    h = _norm_mod(x1, g_ref[...], sh_ref[...], sc_ref[...])
    h_ref[...] = h
    logits = _dot3(h, wr_ref[0:D, :]) + wr_ref[D:D + 1, :]
    lg = jnp.where(lane < N_EXPERTS, logits, -jnp.inf)
    l1 = jnp.max(lg, axis=-1, keepdims=True)
    i1 = jnp.min(jnp.where(lg == l1, lane, V7X_LANES), axis=-1, keepdims=True)
    rest = jnp.where(lane == i1, -jnp.inf, lg)
    l2 = jnp.max(rest, axis=-1, keepdims=True)
    i2 = jnp.min(jnp.where(rest == l2, lane, V7X_LANES), axis=-1, keepdims=True)
    gap = jnp.exp(l2 - l1)
    gate1 = 1.0 / (1.0 + gap)
    gate2 = gap * gate1
    m1 = lane == i1
    m2 = lane == i2
    chosen = jnp.where(m1 | m2, 1.0, 0.0)
    r = lax.broadcasted_iota(jnp.int32, (tm, tm), 0)
    c = lax.broadcasted_iota(jnp.int32, (tm, tm), 1)
    tri = jnp.where(c < r, 1.0, 0.0).astype(BF16)
    before = jnp.dot(tri, chosen.astype(BF16), preferred_element_type=F32) + run_sc[0:1, :]
    rank1 = jnp.sum(jnp.where(m1, before, 0.0), axis=-1, keepdims=True)
    rank2 = jnp.sum(jnp.where(m2, before, 0.0), axis=-1, keepdims=True)
    vals = (i1.astype(F32), i2.astype(F32), rank1, rank2, gate1, gate2)
    meta = jnp.zeros((tm, V7X_LANES), F32)
    for k, v in enumerate(vals):
        meta = jnp.where(lane == k, v, meta)
    meta_ref[...] = meta
    meta_t_ref[...] = meta.T[:V7X_SUBLANES]
    run_sc[...] = run_sc[...] + jnp.sum(chosen, axis=0, keepdims=True)
    cnt_ref[...] = run_sc[...]


def mix_route(parts, w_out, x, g, modtab, cond, router_pack):
    tokens = x.shape[0]
    tm = TM_ROUTE
    kb = MIX_SLAB
    zero = lambda i: 0
    const = lambda i: (0, 0)
    rows = lambda w: pl.BlockSpec((tm, w), lambda i: (i, 0))
    lhs_specs = [pl.BlockSpec((tm, kb), (lambda i, cbk=cbk: (i, cbk))) for _, cbk in parts]
    return pl.pallas_call(
        _route_kernel,
        out_shape=(jax.ShapeDtypeStruct((tokens, D), F32),
                   jax.ShapeDtypeStruct((tokens, D), F32),
                   jax.ShapeDtypeStruct((tokens, V7X_LANES), F32),
                   jax.ShapeDtypeStruct((V7X_SUBLANES, tokens), F32),
                   jax.ShapeDtypeStruct((V7X_SUBLANES, V7X_LANES), F32)),
        grid=(tokens // tm,),
        in_specs=lhs_specs + [
            pl.BlockSpec((len(parts) * kb, D), const, pipeline_mode=pl.Buffered(1)),
            rows(D),
            _mod_spec(2, cond, tm, D, zero),
            pl.BlockSpec((1, D), const),
            _mod_spec(3, cond, tm, D, zero),
            _mod_spec(4, cond, tm, D, zero),
            pl.BlockSpec((D + V7X_SUBLANES, V7X_LANES), const)],
        out_specs=(rows(D), rows(D), rows(V7X_LANES),
                   pl.BlockSpec((V7X_SUBLANES, tm), lambda i: (0, i)),
                   pl.BlockSpec((V7X_SUBLANES, V7X_LANES), const)),
        scratch_shapes=[pltpu.VMEM((V7X_SUBLANES, V7X_LANES), F32), pltpu.VMEM((len(parts) * kb, D), BF16)],
        compiler_params=_cparams(("arbitrary",)),
        name="mix_route",
    )(*[a for a, _ in parts], w_out, x, modtab, g.reshape(1, D), modtab, modtab, router_pack)


def _row_copy(src_ref, src_row, dst_ref, dst_row, sem):
    return pltpu.make_async_copy(src_ref.at[pl.ds(src_row, 1)], dst_ref.at[pl.ds(dst_row, 1)], sem)


_PAD_BULK = (256, 128, 64, 32, 16, 8)


def _zero_fill(hs_ref, zero_sc, sem, pads_ref, n_tail_max, wait):
    tmr = zero_sc.shape[0]

    def copy(rows, dst):
        cp = pltpu.make_async_copy(zero_sc.at[pl.ds(0, rows)], hs_ref.at[pl.ds(dst, rows)], sem)
        cp.wait() if wait else cp.start()

    for e in range(N_EXPERTS):
        start, n = pads_ref[e], pads_ref[N_EXPERTS + e]
        head = jnp.minimum((-start) & (V7X_SUBLANES - 1), n)
        for r in range(V7X_SUBLANES - 1):
            @pl.when(r < head)
            def _():
                copy(1, start + r)
        body = start + head
        rem = n - head
        for k in _PAD_BULK:
            @pl.when((rem & k) != 0)
            def _():
                copy(k, pl.multiple_of(body + (rem & ~(2 * k - 1)), V7X_SUBLANES))
    tail_start, tail_tiles = pads_ref[2 * N_EXPERTS], pads_ref[2 * N_EXPERTS + 1]
    for t in range(n_tail_max):
        @pl.when(t < tail_tiles)
        def _():
            copy(tmr, pl.multiple_of(tail_start + t * tmr, tmr))


def _dispatch_kernel(pos_ref, pads_ref, ha_ref, hb_ref, hs_ref, zero_sc, sem, zsem, *, n_a, n_tail_max):
    tm = ha_ref.shape[0]
    n_tok = pos_ref.shape[0] // 2
    i = pl.program_id(0)
    base = i * tm

    @pl.when(i == 0)
    def _():
        zero_sc[...] = jnp.zeros_like(zero_sc)
        _zero_fill(hs_ref, zero_sc, zsem, pads_ref, n_tail_max, wait=False)

    def scatter(h_ref):
        def issue(r, carry):
            _row_copy(h_ref, r, hs_ref, pos_ref[base + r], sem).start(priority=0)
            _row_copy(h_ref, r, hs_ref, pos_ref[n_tok + base + r], sem).start(priority=1)
            return carry

        lax.fori_loop(0, tm, issue, 0, unroll=8)
        for _ in range(2):
            pltpu.make_async_copy(h_ref, hs_ref.at[pl.ds(0, tm)], sem).wait()

    @pl.when(i < n_a)
    def _():
        scatter(ha_ref)

    @pl.when(i >= n_a)
    def _():
        scatter(hb_ref)

    @pl.when(i == 0)
    def _():
        _zero_fill(hs_ref, zero_sc, zsem, pads_ref, n_tail_max, wait=True)


def moe_dispatch(pos, pads, hs_rows, h_a, h_b):
    tm = TM_ROUTE
    n_a, n_b = h_a.shape[0] // tm, h_b.shape[0] // tm
    n_tail_max = hs_rows // TM_EXPERT - (2 * (h_a.shape[0] + h_b.shape[0])) // TM_EXPERT
    return pl.pallas_call(
        functools.partial(_dispatch_kernel, n_a=n_a, n_tail_max=n_tail_max),
        out_shape=jax.ShapeDtypeStruct((hs_rows, D), F32),
        grid_spec=pltpu.PrefetchScalarGridSpec(
            num_scalar_prefetch=2,
            grid=(n_a + n_b,),
            in_specs=[pl.BlockSpec((tm, D), lambda i, *pf: (jnp.minimum(i, n_a - 1), 0)),
                      pl.BlockSpec((tm, D), lambda i, *pf: (jnp.clip(i - n_a, 0, n_b - 1), 0))],
            out_specs=pl.BlockSpec(memory_space=pl.ANY),
            scratch_shapes=[pltpu.VMEM((TM_EXPERT, D), F32), pltpu.SemaphoreType.DMA(()),
                            pltpu.SemaphoreType.DMA(())]),
        compiler_params=_cparams(("arbitrary",)),
        name="moe_dispatch",
    )(pos, pads, h_a, h_b)


def _experts_kernel(te_ref, sg_ref, su_ref, sd_ref, nv_ref, hs_ref, wg_ref, wu_ref, wd_ref, y_ref,
                    wg_sc, wu_sc, wd_sc):
    del sg_ref, su_ref, sd_ref
    j = pl.program_id(0)
    e = te_ref[j]
    e_prev = te_ref[jnp.maximum(j - 1, 0)]
    n_valid = nv_ref[j]
    half = y_ref.shape[0] // 2

    @pl.when((j == 0) | (e != e_prev))
    def _():
        wg_sc[...] = wg_ref[...].astype(BF16)
        wu_sc[...] = wu_ref[...].astype(BF16)
        wd_sc[...] = wd_ref[...].astype(BF16)

    def swiglu(rows):
        h = hs_ref[rows, :].astype(BF16)
        y = None
        for c0 in range(0, D_FF_EXPERT, MOE_CHUNK):
            c1 = min(c0 + MOE_CHUNK, D_FF_EXPERT)
            hg = jnp.dot(h, wg_sc[:, c0:c1], preferred_element_type=F32)
            hu = jnp.dot(h, wu_sc[:, c0:c1], preferred_element_type=F32)
            act = (_silu(hg) * hu).astype(BF16)
            yc = jnp.dot(act, wd_sc[c0:c1, :], preferred_element_type=F32)
            y = yc if y is None else y + yc
        y_ref[rows, :] = y

    @pl.when(n_valid > half)
    def _():
        swiglu(slice(None))

    @pl.when((n_valid > 0) & (n_valid <= half))
    def _():
        swiglu(slice(0, half))
        y_ref[half:, :] = jnp.zeros((half, D), F32)

    @pl.when(n_valid == 0)
    def _():
        y_ref[...] = jnp.zeros_like(y_ref)


def moe_experts(tile_expert, stages, tile_valid, hs, e_gate, e_up, e_down):
    rows = hs.shape[0]
    tmr = TM_EXPERT
    wspec = lambda shape, k: pl.BlockSpec((None,) + shape, lambda j, *pf: (pf[1 + k][j], 0, 0))
    return pl.pallas_call(
        _experts_kernel,
        out_shape=jax.ShapeDtypeStruct((rows, D), F32),
        grid_spec=pltpu.PrefetchScalarGridSpec(
            num_scalar_prefetch=5,
            grid=(rows // tmr,),
            in_specs=[pl.BlockSpec((tmr, D), lambda j, *pf: (j, 0)),
                      wspec((D, D_FF_EXPERT), 0), wspec((D, D_FF_EXPERT), 1), wspec((D_FF_EXPERT, D), 2)],
            out_specs=pl.BlockSpec((tmr, D), lambda j, *pf: (j, 0)),
            scratch_shapes=[pltpu.VMEM((D, D_FF_EXPERT), BF16), pltpu.VMEM((D, D_FF_EXPERT), BF16),
                            pltpu.VMEM((D_FF_EXPERT, D), BF16)]),
        compiler_params=_cparams(("arbitrary",)),
        name="moe_experts",
    )(tile_expert, *stages, tile_valid, hs, e_gate, e_up, e_down)


def _combine_kernel(pos_ref, x_ref, meta_ref, gt_ref, fg_ref, y_ref, o_ref, b1_sc, b2_sc, sem):
    tm = x_ref.shape[0]
    n_tok = pos_ref.shape[0] // 2
    i = pl.program_id(0)

    def gather(tile, slot):
        base = tile * tm

        def issue(r, carry):
            _row_copy(y_ref, pos_ref[base + r], b1_sc.at[slot], r, sem.at[slot]).start(priority=0)
            _row_copy(y_ref, pos_ref[n_tok + base + r], b2_sc.at[slot], r, sem.at[slot]).start(priority=1)
            return carry

        lax.fori_loop(0, tm, issue, 0, unroll=8)

    @pl.when(i == 0)
    def _():
        gather(0, 0)

    @pl.when(i + 1 < pl.num_programs(0))
    def _():
        gather(i + 1, (i + 1) % 2)

    slot = i % 2
    pltpu.make_async_copy(y_ref.at[pl.ds(0, tm)], b1_sc.at[slot], sem.at[slot]).wait()
    pltpu.make_async_copy(y_ref.at[pl.ds(0, tm)], b2_sc.at[slot], sem.at[slot]).wait()

    meta = meta_ref[...]
    lane = lax.broadcasted_iota(jnp.int32, meta.shape, 1)
    g1 = jnp.sum(jnp.where(lane == META_G1, meta, 0.0), axis=-1, keepdims=True)
    g2 = jnp.sum(jnp.where(lane == META_G2, meta, 0.0), axis=-1, keepdims=True)
    x = x_ref[...] + gt_ref[...] * (g1 * b1_sc[slot] + g2 * b2_sc[slot])
    o_ref[...] = _rms(x, fg_ref[...])


def moe_combine(pos, x, meta, modtab, cond, final_g, y):
    tokens = x.shape[0]
    tm = TM_COMBINE
    return pl.pallas_call(
        _combine_kernel,
        out_shape=jax.ShapeDtypeStruct((tokens, D), F32),
        grid_spec=pltpu.PrefetchScalarGridSpec(
            num_scalar_prefetch=1,
            grid=(tokens // tm,),
            in_specs=[pl.BlockSpec((tm, D), lambda i, pos: (i, 0)),
                      pl.BlockSpec((tm, V7X_LANES), lambda i, pos: (i, 0)),
                      _mod_spec(5, cond, tm, D, lambda i, pos: 0),
                      pl.BlockSpec((1, D), lambda i, pos: (0, 0)),
                      pl.BlockSpec(memory_space=pl.ANY)],
            out_specs=pl.BlockSpec((tm, D), lambda i, pos: (i, 0)),
            scratch_shapes=[pltpu.VMEM((2, tm, D), F32), pltpu.VMEM((2, tm, D), F32),
                            pltpu.SemaphoreType.DMA((2,))]),
        compiler_params=_cparams(("arbitrary",)),
        name="moe_combine",
    )(pos, x, meta, modtab, final_g.reshape(1, D), y)


def moe_plan(metas, counts):
    tmr = TM_EXPERT
    cnts = [c[0, :N_EXPERTS].astype(jnp.int32) for c in counts]
    total = functools.reduce(jnp.add, cnts)
    padded = ((total + tmr - 1) // tmr) * tmr
    ends = jnp.cumsum(padded)
    starts = ends - padded
    n_rows = sum(m.shape[1] for m in metas) * 2 + N_EXPERTS * tmr
    n_tiles = n_rows // tmr
    tile_start = jnp.arange(n_tiles, dtype=jnp.int32) * tmr
    tile_expert = jnp.minimum(jnp.sum(tile_start[:, None] >= ends[None, :], axis=1), N_EXPERTS - 1).astype(jnp.int32)
    group_of_tile = jnp.sum(tile_start[:, None] >= ends[None, :], axis=1)
    real_end = jnp.sum(jnp.where(group_of_tile[:, None] == jnp.arange(N_EXPERTS)[None, :],
                                 (starts + total)[None, :], 0), axis=1)
    tile_valid = jnp.clip(real_end - tile_start, 0, tmr).astype(jnp.int32)
    eid = jnp.arange(N_EXPERTS, dtype=jnp.int32)
    later = jnp.where((eid[None, :] > eid[:, None]) & (padded[None, :] > 0), eid[None, :], N_EXPERTS)
    nxt = jnp.min(later, axis=1)
    next_used = jnp.where(nxt == N_EXPERTS, eid, nxt)
    pick = lambda table: jnp.sum(jnp.where(tile_expert[:, None] == eid[None, :], table[None, :], 0), axis=1)
    k_in_group = (tile_start - pick(starts)) // tmr
    tile_next = pick(next_used)
    stages = [jnp.where(k_in_group < k, tile_expert, tile_next).astype(jnp.int32) for k in (1, 2, 3)]
    pos, p1s, p2s = [], [], []
    base = jnp.zeros((N_EXPERTS,), jnp.int32)
    for m, c in zip(metas, cnts):
        first = starts + base
        sel = lambda field: m[field].astype(jnp.int32)
        lookup = lambda e: jnp.sum(jnp.where(e[:, None] == jnp.arange(N_EXPERTS)[None, :], first[None, :], 0), axis=1)
        p1 = lookup(sel(META_E1)) + sel(META_R1)
        p2 = lookup(sel(META_E2)) + sel(META_R2)
        pos.append(jnp.concatenate([p1, p2]).astype(jnp.int32))
        p1s.append(p1)
        p2s.append(p2)
        base = base + c
    pos_all = jnp.concatenate(p1s + p2s).astype(jnp.int32)
    pads = jnp.concatenate([starts + total, padded - total,
                            jnp.stack([ends[-1], n_tiles - ends[-1] // tmr])]).astype(jnp.int32)
    return pos, pos_all, pads, tile_expert, stages, tile_valid, n_rows


def _pad_to(a, shape):
    return jnp.pad(a, [(0, t - s) for s, t in zip(a.shape, shape)])


def _regroup_chunks(a, cb):
    r = a.shape[0]
    return a.reshape(r, 3, HY_W // cb, cb).transpose(2, 0, 1, 3).reshape(HY_W // cb, r, 3 * cb)


def kernel(x_prompt, x_sample, state_l0_lru, cache_l1_ckv, cache_l1_krope, c, c_ctx, l0_norm1, l0_norm2, l0_w_mod, l0_b_mod, l0_w_in, l0_conv_a, l0_lru_conv_w, l0_lru_conv_b, l0_lru_wa, l0_lru_ba, l0_lru_wi, l0_lru_bi, l0_lru_lambda, l0_w_out, l0_ffn_gate, l0_ffn_up, l0_ffn_down, l1_norm1, l1_norm2, l1_w_mod, l1_b_mod, l1_w_in, l1_q_norm, l1_kv_norm, l1_w_q_up, l1_w_kv_up, l1_hy_short_w, l1_hy_short_b, l1_hy_f_w1, l1_hy_f_b1, l1_hy_f_w2, l1_hy_f_b2, l1_hy_f_w3, l1_hy_bias, l1_w_out, l1_router_w, l1_router_b, l1_exp_gate, l1_exp_up, l1_exp_down, final_norm):
    batch, seq, _ = x_prompt.shape
    dec_batch, dec_seq, _ = x_sample.shape
    past_len = cache_l1_ckv.shape[1]

    cond8 = jnp.concatenate([c_ctx[None, :], c, jnp.zeros((V7X_SUBLANES - 1 - dec_batch, D), F32)], axis=0)
    wcat = jnp.concatenate([l0_lru_wa[0], l0_lru_wi[0], l0_lru_wa[1], l0_lru_wi[1]], axis=-1)
    hid = l1_hy_f_w2.shape[0]
    filter_pack = jnp.concatenate([_pad_to(l1_hy_f_w1, (V7X_LANES, hid)), l1_hy_f_b1[None, :], l1_hy_f_b2[None, :],
                                   jnp.zeros((V7X_SUBLANES - 2, hid), F32)], axis=0)
    short_w = _regroup_chunks(l1_hy_short_w, HY_CB)
    short_b = _regroup_chunks(l1_hy_short_b.reshape(1, -1), HY_CB)
    hy_bias = l1_hy_bias.reshape(1, HY_W)
    router_pack = _pad_to(jnp.concatenate([l1_router_w, l1_router_b[None, :]], axis=0),
                          (D + V7X_SUBLANES, V7X_LANES))

    mod0, mod1 = adaln_tables(cond8, ((l0_w_mod, l0_b_mod), (l1_w_mod, l1_b_mod)))

    kv_ctx = kv_up(cache_l1_ckv.reshape(dec_batch * past_len, KV_RANK), l1_w_kv_up)
    kr_ctx = cache_l1_krope.reshape(dec_batch * past_len, ROPE)

    conds = ((0, batch * seq), (1, dec_seq))
    seq_lens = (seq, dec_seq)
    xs = (x_prompt.reshape(batch * seq, D), x_sample.reshape(dec_batch * dec_seq, D))
    h0s = (jnp.zeros((batch, 2, LRU_W), F32), state_l0_lru)

    us = in0_proj(xs, l0_norm1, mod0, conds, l0_w_in)
    parts, lru_states = [], []
    for u, seq_len, h0 in zip(us, seq_lens, h0s):
        ya = conv_a(u, seq_len, l0_conv_a)
        yb, lru_state = rglru(u, seq_len, l0_lru_conv_w, l0_lru_conv_b, wcat, l0_lru_ba, l0_lru_bi,
                              l0_lru_lambda, h0)
        parts.append([(ya, 0), (yb, 0), (yb, 1)])
        lru_states.append(lru_state)
    xs = mix_ffn(parts, l0_w_out, xs, l0_norm2, mod0, conds, l0_ffn_gate, l0_ffn_up, l0_ffn_down)
    new_lru = lru_states[0]

    def layer1(x, seq_len, cond, latent):
        qnope, qpe, ckv, kr, kv, uh = in1_proj(x, l1_norm1, mod1, cond, l1_w_in, l1_q_norm, l1_kv_norm,
                                               l1_w_q_up, l1_w_kv_up)
        if latent:
            yc = attn_lat(qnope, qpe, kv_ctx, kr_ctx, kv, kr, seq_len, past_len)
        else:
            yc = attn_ctx(qnope, qpe, kv, kr, seq_len)
        cs = dft_tables(seq_len)
        k_r, k_s, k_ny = hy_filter(cs, filter_pack, l1_hy_f_w2, l1_hy_f_w3)
        yd = hyena(uh, seq_len, short_w, short_b, cs, k_r, k_s, k_ny, hy_bias)
        routed = mix_route([(yc, 0), (yc, 1), (yd, 0)], l1_w_out, x, l1_norm2, mod1, cond, router_pack)
        return routed, ckv, kr

    r_p, new_ckv, new_kr = layer1(xs[0], seq, conds[0], latent=False)
    r_s, _, _ = layer1(xs[1], dec_seq, conds[1], latent=True)

    routed = (r_p, r_s)
    pos, pos_all, pads, tile_expert, stages, tile_valid, n_rows = moe_plan([r[3] for r in routed],
                                                                          [r[4] for r in routed])
    hs = moe_dispatch(pos_all, pads, n_rows, r_p[1], r_s[1])
    y_rows = moe_experts(tile_expert, stages, tile_valid, hs, l1_exp_gate, l1_exp_up, l1_exp_down)
    y_p, y_s = [moe_combine(p, r[0], r[2], mod1, cond, final_norm, y_rows)
                for p, r, cond in zip(pos, routed, conds)]
    return (y_p.reshape(batch, seq, D), y_s.reshape(dec_batch, dec_seq, D), new_lru,
            new_ckv.reshape(batch, seq, KV_RANK), new_kr.reshape(batch, seq, ROPE))
```

```python
import functools
import math

import jax
import jax.numpy as jnp
from jax import lax
from jax.experimental import pallas as pl
from jax.experimental.pallas import tpu as pltpu

F32 = jnp.float32
BF16 = jnp.bfloat16

D = 1024
GRID_W = 64
EPS = 1e-6
CONV_W = 512
LRU_W = 1024
LRU_BW = 128
LRU_C = 8.0
MLA_HEADS = 8
Q_RANK = 384
KV_RANK = 256
NOPE = 128
ROPE = 64
VDIM = 128
QK_DIM = NOPE + ROPE
ROPE_THETA = 10000.0
HY_W = 512
HY_BANDS = 16
HY_TARGET = 1e-2
HY_FAST_DECAY = 0.3
HY_SLOW_DECAY = 1.5
D_FF = 2816
N_EXPERTS = 8
D_FF_EXPERT = 1408
IN0 = 3 * CONV_W + 2 * LRU_W
IN1 = Q_RANK + KV_RANK + ROPE + 3 * HY_W

V7X_LANES = 128
V7X_SUBLANES = 8
V7X_VMEM_LIMIT_BYTES = 56 * 1024 * 1024
V7X_VMEM_LIMIT_LARGE_BYTES = 60 * 1024 * 1024

TM = 512
TN_IN0 = 512
TK_IN0 = 256
TF_FFN = 256
MIX_SLAB = 512
MOE_CHUNK = 256
TM_ROUTE = 512
TM_EXPERT = 512
TM_COMBINE = 512
LRU_CB = 512
HY_CB = 512
TQ = 256
ATTN_CTX_SEQS = 4
CONV_A_ROWS = 1024
LRU_ROWS = 1024
HY_ROWS = 1024
TM_IN1 = 512


def _cparams(sem, vmem_limit_bytes=V7X_VMEM_LIMIT_BYTES):
    return pltpu.CompilerParams(dimension_semantics=sem, vmem_limit_bytes=vmem_limit_bytes)


def _sigmoid(x):
    return 0.5 * jnp.tanh(0.5 * x) + 0.5


def _silu(x):
    return x * _sigmoid(x)


def _norm_mod(x, g, shift, scale):
    ms = jnp.mean(x * x, axis=-1, keepdims=True)
    y = x * lax.rsqrt(ms + EPS) * g
    return y * (1.0 + scale) + shift


def _mod_spec(comp, cond, tm, width, col_fn, tile_fn=lambda *ids: ids[0]):
    row0, seg = cond
    assert seg % tm == 0
    return pl.BlockSpec((None, 1, width),
                        lambda *ids: (comp * 3 + row0 + (tile_fn(*ids) * tm) // seg, 0, col_fn(*ids)))


def _dot3(a, b):
    a_hi = a.astype(BF16)
    a_lo = (a - a_hi.astype(F32)).astype(BF16)
    b_hi = b.astype(BF16)
    b_lo = (b - b_hi.astype(F32)).astype(BF16)
    n = a.shape[0]
    y = jnp.dot(jnp.concatenate([a_hi, a_lo], axis=0), b_hi, preferred_element_type=F32)
    return y[:n] + y[n:] + jnp.dot(a_hi, b_lo, preferred_element_type=F32)


def _adaln_kernel(c_ref, w0_ref, b0_ref, w1_ref, b1_ref, o_ref):
    a = _silu(c_ref[...])
    for layer, (w_ref, b_ref) in enumerate(((w0_ref, b0_ref), (w1_ref, b1_ref))):
        @pl.when(pl.program_id(0) == layer)
        def _():
            o_ref[...] = _dot3(a, w_ref[...]) + b_ref[...]


def adaln_tables(cond8, mods):
    tn = 1536
    nj = 6 * D // tn
    (w0, b0), (w1, b1) = mods
    at0 = lambda l, j: (0, jnp.where(l == 0, j, nj - 1))
    at1 = lambda l, j: (0, jnp.where(l == 1, j, 0))
    m = pl.pallas_call(
        _adaln_kernel,
        out_shape=jax.ShapeDtypeStruct((2, V7X_SUBLANES, 6 * D), F32),
        grid=(2, nj),
        in_specs=[pl.BlockSpec((V7X_SUBLANES, D), lambda l, j: (0, 0)),
                  pl.BlockSpec((D, tn), at0), pl.BlockSpec((1, tn), at0),
                  pl.BlockSpec((D, tn), at1), pl.BlockSpec((1, tn), at1)],
        out_specs=pl.BlockSpec((None, V7X_SUBLANES, tn), lambda l, j: (l, 0, j)),
        compiler_params=_cparams(("arbitrary", "arbitrary")),
        name="adaln",
    )(cond8, w0, b0.reshape(1, 6 * D), w1, b1.reshape(1, 6 * D))
    return [m[l, :3].reshape(3, 6, D).transpose(1, 0, 2).reshape(18, 1, D) for l in range(2)]


def _tile_of(n_load):
    return lambda s: jnp.maximum(s - n_load, 0)


def _block_of(n_load):
    return lambda s: jnp.minimum(s, n_load - 1)


class _TwoSets:
    def __init__(self, n_load, tm, tokens, conds):
        self.n_load, self.tm, self.conds = n_load, tm, conds
        self.n_a, self.n_b = tokens[0] // tm, tokens[1] // tm
        self.steps = n_load + self.n_a + self.n_b

    def tile(self, s):
        return jnp.maximum(s - self.n_load, 0)

    def in_first(self, s):
        return s - self.n_load < self.n_a

    def idx_a(self, s):
        return jnp.minimum(self.tile(s), self.n_a - 1)

    def idx_b(self, s):
        return jnp.clip(self.tile(s) - self.n_a, 0, self.n_b - 1)

    def rows(self, width):
        return (pl.BlockSpec((self.tm, width), lambda s: (self.idx_a(s), 0)),
                pl.BlockSpec((self.tm, width), lambda s: (self.idx_b(s), 0)))

    def cols(self, width, col):
        return (pl.BlockSpec((self.tm, width), lambda s: (self.idx_a(s), col)),
                pl.BlockSpec((self.tm, width), lambda s: (self.idx_b(s), col)))

    def mod_spec(self, comp):
        (row_a, seg_a), (row_b, seg_b) = self.conds
        assert seg_a % self.tm == 0 and seg_b % self.tm == 0

        def row(s):
            return jnp.where(self.in_first(s), row_a + (self.idx_a(s) * self.tm) // seg_a,
                             row_b + (self.idx_b(s) * self.tm) // seg_b)

        return pl.BlockSpec((None, 1, D), lambda s: (comp * 3 + row(s), 0, 0))


def _in0_kernel(xa_ref, xb_ref, g_ref, sh_ref, sc_ref, w_ref, oa_ref, ob_ref, w_sc, *, n_a, tn):
    s = pl.program_id(0)
    tk = w_ref.shape[0]
    n_load = w_sc.shape[0] // tk

    @pl.when(s < n_load)
    def _():
        w_sc[pl.ds(pl.multiple_of(s * tk, tk), tk), :] = w_ref[...].astype(BF16)

    def tile(x_ref, o_ref):
        h = _norm_mod(x_ref[...], g_ref[...], sh_ref[...], sc_ref[...]).astype(BF16)
        for j in range(w_sc.shape[1] // tn):
            o_ref[:, j * tn:(j + 1) * tn] = jnp.dot(h, w_sc[:, j * tn:(j + 1) * tn],
                                                    preferred_element_type=F32).astype(BF16)

    @pl.when(jnp.logical_and(s >= n_load, s - n_load < n_a))
    def _():
        tile(xa_ref, oa_ref)

    @pl.when(s - n_load >= n_a)
    def _():
        tile(xb_ref, ob_ref)


def in0_proj(xs, g, modtab, conds, w_in):
    n = w_in.shape[1]
    n_load = D // TK_IN0
    ts = _TwoSets(n_load, TM, [x.shape[0] for x in xs], conds)
    blk = _block_of(n_load)
    return pl.pallas_call(
        functools.partial(_in0_kernel, n_a=ts.n_a, tn=TN_IN0),
        out_shape=tuple(jax.ShapeDtypeStruct((x.shape[0], n), BF16) for x in xs),
        grid=(ts.steps,),
        in_specs=[*ts.rows(D),
                  pl.BlockSpec((1, D), lambda s: (0, 0)),
                  ts.mod_spec(0), ts.mod_spec(1),
                  pl.BlockSpec((TK_IN0, n), lambda s: (blk(s), 0))],
        out_specs=ts.rows(n),
        scratch_shapes=[pltpu.VMEM((D, n), BF16)],
        compiler_params=_cparams(("arbitrary",)),
        name="in0_proj",
    )(*xs, g.reshape(1, D), modtab, modtab, w_in)


def _shift_rows(v, d, t, seq_len=None):
    n = v.shape[0]
    seq_len = n if seq_len is None else seq_len
    if d > 0:
        return jnp.where(t < d, 0.0, pltpu.roll(v, d, 0))
    return jnp.where(t >= seq_len + d, 0.0, pltpu.roll(v, n + d, 0))


def _conv_a_kernel(b_ref, c_ref, x_ref, w_ref, o_ref, *, seq_len):
    v = c_ref[...].astype(F32) * x_ref[...].astype(F32)
    t = lax.broadcasted_iota(jnp.int32, v.shape, 0) & (seq_len - 1)
    w = w_ref[...]
    y = w[0:1] * _shift_rows(v, 1, t, seq_len) + w[1:2] * v + w[2:3] * _shift_rows(v, -1, t, seq_len)
    o_ref[...] = (b_ref[...].astype(F32) * y).astype(o_ref.dtype)


def conv_a(u, seq_len, conv_w):
    tokens = u.shape[0]
    rows = max(seq_len, CONV_A_ROWS)
    assert seq_len & (seq_len - 1) == 0 and rows % seq_len == 0
    return pl.pallas_call(
        functools.partial(_conv_a_kernel, seq_len=seq_len),
        out_shape=jax.ShapeDtypeStruct((tokens, CONV_W), BF16),
        grid=(tokens // rows,),
        in_specs=[pl.BlockSpec((rows, CONV_W), lambda s: (s, 0)),
                  pl.BlockSpec((rows, CONV_W), lambda s: (s, 1)),
                  pl.BlockSpec((rows, CONV_W), lambda s: (s, 2)),
                  pl.BlockSpec((3, CONV_W), lambda s: (0, 0))],
        out_specs=pl.BlockSpec((rows, CONV_W), lambda s: (s, 0)),
        compiler_params=_cparams(("parallel",)),
        name="conv_a",
    )(u, u, u, conv_w)


def _group_scan(a_sc, b_sc, k, reverse):
    planes = a_sc.shape[1] // V7X_SUBLANES
    order = range(V7X_SUBLANES - 1, -1, -1) if reverse else range(V7X_SUBLANES)
    a_acc = b_acc = None
    for r in order:
        plane = (k, pl.ds(r, planes, stride=V7X_SUBLANES), slice(None))
        a_r, b_r = a_sc[plane], b_sc[plane]
        if a_acc is None:
            a_acc, b_acc = a_r, b_r
        else:
            b_acc = a_r * b_acc + b_r
            a_acc = a_r * a_acc
            a_sc[plane] = a_acc
            b_sc[plane] = b_acc


def _rglru_kernel(gate_ref, xb_ref, cw_ref, cb_ref, wcat_ref, ba_ref, bi_ref, lam_ref, h0_ref,
                  y_ref, st_ref, af_sc, bf_sc, ab_sc, bb_sc, hf_sc, hb_sc, *, seq_len):
    n, cb = xb_ref.shape
    n_seq = n // seq_len
    n_slab = cb // LRU_BW
    xb = xb_ref[...].astype(F32)
    t = lax.broadcasted_iota(jnp.int32, xb.shape, 0) & (seq_len - 1)
    cw = cw_ref[...]
    sh = lambda d: _shift_rows(xb, d, t, seq_len)
    xc = cb_ref[...] + cw[0:1] * sh(2) + cw[1:2] * sh(1) + cw[2:3] * xb + cw[3:4] * sh(-1)
    xcb = xc.astype(BF16)

    for k in range(n_slab):
        cols = slice(k * LRU_BW, (k + 1) * LRU_BW)
        gk = jnp.dot(xcb[:, cols], wcat_ref[k].astype(BF16), preferred_element_type=F32)
        for d, (a_sc, b_sc) in enumerate(((af_sc, bf_sc), (ab_sc, bb_sc))):
            ga = gk[:, (2 * d) * LRU_BW:(2 * d + 1) * LRU_BW]
            gi = gk[:, (2 * d + 1) * LRU_BW:(2 * d + 2) * LRU_BW]
            r = _sigmoid(ga + ba_ref[d:d + 1, cols])
            i = _sigmoid(gi + bi_ref[d:d + 1, cols])
            log_a = (-LRU_C * jax.nn.softplus(-lam_ref[d:d + 1, cols])) * r
            a = jnp.exp(log_a)
            m = 1.0 - a * a
            mult = m * lax.rsqrt(jnp.maximum(m, 1e-30))
            a_sc[k] = a
            b_sc[k] = mult * (i * xc[:, cols])
            _group_scan(a_sc, b_sc, k, reverse=(d == 1))

    ng = seq_len // V7X_SUBLANES
    bcast = lambda row: jnp.broadcast_to(row, (V7X_SUBLANES, LRU_BW))
    chains = [(q, k) for q in range(n_seq) for k in range(n_slab)]
    init = tuple((bcast(h0_ref[q, 0:1, k * LRU_BW:(k + 1) * LRU_BW]),
                  bcast(h0_ref[q, 1:2, k * LRU_BW:(k + 1) * LRU_BW])) for q, k in chains)

    def step(j, carry):
        out = []
        for (q, k), (hf_in, hb_in) in zip(chains, carry):
            rf = pl.ds(pl.multiple_of(q * seq_len + j * V7X_SUBLANES, V7X_SUBLANES), V7X_SUBLANES)
            rb = pl.ds(pl.multiple_of(q * seq_len + (ng - 1 - j) * V7X_SUBLANES, V7X_SUBLANES), V7X_SUBLANES)
            hf = af_sc[k, rf, :] * hf_in + bf_sc[k, rf, :]
            hb = ab_sc[k, rb, :] * hb_in + bb_sc[k, rb, :]
            hf_sc[k, rf, :] = hf
            hb_sc[k, rb, :] = hb
            out.append((bcast(hf[V7X_SUBLANES - 1:V7X_SUBLANES]), bcast(hb[0:1])))
        return tuple(out)

    final = lax.fori_loop(0, ng, step, init)
    for (q, k), (hf_last, hb_first) in zip(chains, final):
        st_ref[q, 0:1, k * LRU_BW:(k + 1) * LRU_BW] = hf_last[0:1]
        st_ref[q, 1:2, k * LRU_BW:(k + 1) * LRU_BW] = hb_first[0:1]

    gt = gate_ref[...].astype(F32)
    gelu = 0.5 * gt * (1.0 + jnp.tanh(math.sqrt(2.0 / math.pi) * (gt + 0.044715 * (gt * gt * gt))))
    h = jnp.concatenate([hf_sc[k] + hb_sc[k] for k in range(n_slab)], axis=1)
    y_ref[...] = (h * gelu).astype(y_ref.dtype)


def rglru(u, seq_len, conv_w, conv_b, wcat, ba, bi, lam, h0):
    tokens = u.shape[0]
    nseq = tokens // seq_len
    cb = LRU_CB
    rows = max(seq_len, LRU_ROWS)
    assert seq_len & (seq_len - 1) == 0 and rows % seq_len == 0
    per_blk = rows // seq_len
    gate_blk0 = 3 * CONV_W // cb
    xb_blk0 = (3 * CONV_W + LRU_W) // cb
    seq_scr = lambda: pltpu.VMEM((cb // LRU_BW, rows, LRU_BW), F32)
    return pl.pallas_call(
        functools.partial(_rglru_kernel, seq_len=seq_len),
        out_shape=(jax.ShapeDtypeStruct((tokens, LRU_W), BF16), jax.ShapeDtypeStruct((nseq, 2, LRU_W), F32)),
        grid=(tokens // rows, LRU_W // cb),
        in_specs=[pl.BlockSpec((rows, cb), lambda s, c: (s, gate_blk0 + c)),
                  pl.BlockSpec((rows, cb), lambda s, c: (s, xb_blk0 + c)),
                  pl.BlockSpec((4, cb), lambda s, c: (0, c)),
                  pl.BlockSpec((1, cb), lambda s, c: (0, c)),
                  pl.BlockSpec((cb // LRU_BW, LRU_BW, 4 * LRU_BW), lambda s, c: (c, 0, 0)),
                  pl.BlockSpec((2, cb), lambda s, c: (0, c)),
                  pl.BlockSpec((2, cb), lambda s, c: (0, c)),
                  pl.BlockSpec((2, cb), lambda s, c: (0, c)),
                  pl.BlockSpec((per_blk, 2, cb), lambda s, c: (s, 0, c))],
        out_specs=(pl.BlockSpec((rows, cb), lambda s, c: (s, c)),
                   pl.BlockSpec((per_blk, 2, cb), lambda s, c: (s, 0, c))),
        scratch_shapes=[seq_scr() for _ in range(6)],
        compiler_params=_cparams(("parallel", "parallel")),
        name="rglru",
    )(u, u, conv_w, conv_b.reshape(1, LRU_W), wcat, ba, bi, lam, h0)


def _mix_ffn_kernel(p0a_ref, p0b_ref, p1a_ref, p1b_ref, p2a_ref, p2b_ref, wo_ref, xa_ref, xb_ref,
                    g1_ref, g_ref, sh_ref, sc_ref, g2_ref, wg_ref, wu_ref, wd_ref, oa_ref, ob_ref,
                    wo_sc, wg_sc, wu_sc, wd_sc, *, n_a):
    s = pl.program_id(0)
    n_load = wg_sc.shape[0]
    n_out = wo_sc.shape[0]

    @pl.when(s < n_out)
    def _():
        wo_sc[s] = wo_ref[...].astype(BF16)

    @pl.when(s < n_load)
    def _():
        wg_sc[s] = wg_ref[...].astype(BF16)
        wu_sc[s] = wu_ref[...].astype(BF16)
        wd_sc[s] = wd_ref[...].astype(BF16)

    def tile(p0_ref, p1_ref, p2_ref, x_ref, o_ref):
        m = jnp.dot(p0_ref[...], wo_sc[0], preferred_element_type=F32)
        m += jnp.dot(p1_ref[...], wo_sc[1], preferred_element_type=F32)
        m += jnp.dot(p2_ref[...], wo_sc[2], preferred_element_type=F32)
        x = x_ref[...] + g1_ref[...] * m
        h = _norm_mod(x, g_ref[...], sh_ref[...], sc_ref[...]).astype(BF16)
        y = None
        for f in range(n_load):
            hg = jnp.dot(h, wg_sc[f], preferred_element_type=F32)
            hu = jnp.dot(h, wu_sc[f], preferred_element_type=F32)
            act = (_silu(hg) * hu).astype(BF16)
            yf = jnp.dot(act, wd_sc[f], preferred_element_type=F32)
            y = yf if y is None else y + yf
        o_ref[...] = x + g2_ref[...] * y

    @pl.when(jnp.logical_and(s >= n_load, s - n_load < n_a))
    def _():
        tile(p0a_ref, p1a_ref, p2a_ref, xa_ref, oa_ref)

    @pl.when(s - n_load >= n_a)
    def _():
        tile(p0b_ref, p1b_ref, p2b_ref, xb_ref, ob_ref)


def mix_ffn(parts, w_out, xs, g, modtab, conds, w_gate, w_up, w_down):
    tf = TF_FFN
    kb = MIX_SLAB
    n_load = D_FF // tf
    n_out = len(parts[0])
    assert n_out <= n_load
    ts = _TwoSets(n_load, TM, [x.shape[0] for x in xs], conds)
    blk = _block_of(n_load)
    oblk = _block_of(n_out)
    lhs_specs, lhs_args = [], []
    for (arr_a, col_a), (arr_b, col_b) in zip(*parts):
        assert col_a == col_b
        lhs_specs += ts.cols(kb, col_a)
        lhs_args += [arr_a, arr_b]
    return pl.pallas_call(
        functools.partial(_mix_ffn_kernel, n_a=ts.n_a),
        out_shape=tuple(jax.ShapeDtypeStruct(x.shape, F32) for x in xs),
        grid=(ts.steps,),
        in_specs=lhs_specs + [
            pl.BlockSpec((kb, D), lambda s: (oblk(s), 0)),
            *ts.rows(D),
            ts.mod_spec(2),
            pl.BlockSpec((1, D), lambda s: (0, 0)),
            ts.mod_spec(3), ts.mod_spec(4), ts.mod_spec(5),
            pl.BlockSpec((D, tf), lambda s: (0, blk(s))),
            pl.BlockSpec((D, tf), lambda s: (0, blk(s))),
            pl.BlockSpec((tf, D), lambda s: (blk(s), 0))],
        out_specs=ts.rows(D),
        scratch_shapes=[pltpu.VMEM((n_out, kb, D), BF16),
                        pltpu.VMEM((n_load, D, tf), BF16), pltpu.VMEM((n_load, D, tf), BF16),
                        pltpu.VMEM((n_load, tf, D), BF16)],
        compiler_params=_cparams(("arbitrary",), V7X_VMEM_LIMIT_LARGE_BYTES),
        name="mix_ffn",
    )(*lhs_args, w_out, *xs, modtab, g.reshape(1, D), modtab, modtab, modtab, w_gate, w_up, w_down)


def _rms(x, g):
    return x * lax.rsqrt(jnp.mean(x * x, axis=-1, keepdims=True) + EPS) * g


def _in1_kernel(x_ref, g_ref, sh_ref, sc_ref, w_ref, qn_ref, kvn_ref, wq_ref, wkv_ref,
                qnope_ref, qpe_ref, ckv_ref, kr_ref, kv_ref, uh_ref, w_sc, wq_sc, wkv_sc):
    @pl.when(pl.program_id(0) == 0)
    def _():
        w_sc[...] = w_ref[...].astype(BF16)
        for h in range(MLA_HEADS):
            c0 = h * QK_DIM
            wq_sc[:, h * NOPE:(h + 1) * NOPE] = wq_ref[:, c0:c0 + NOPE].astype(BF16)
            r0 = MLA_HEADS * NOPE + h * ROPE
            wq_sc[:, r0:r0 + ROPE] = wq_ref[:, c0 + NOPE:c0 + QK_DIM].astype(BF16)
        wkv_sc[...] = wkv_ref[...].astype(BF16)

    h = _norm_mod(x_ref[...], g_ref[...], sh_ref[...], sc_ref[...]).astype(BF16)
    u = lax.dot_general(h, w_sc[...], (((1,), (1,)), ((), ())), preferred_element_type=F32)
    o1, o2, o3 = Q_RANK, Q_RANK + KV_RANK, Q_RANK + KV_RANK + ROPE
    cq = _rms(u[:, :o1], qn_ref[...])
    q = jnp.dot(cq.astype(BF16), wq_sc[...], preferred_element_type=F32) * _SCALE
    qnope_ref[...] = q[:, :MLA_HEADS * NOPE].astype(qnope_ref.dtype)
    qpe_ref[...] = q[:, MLA_HEADS * NOPE:]
    ckv = _rms(u[:, o1:o2], kvn_ref[...])
    ckv_ref[...] = ckv
    kv_ref[...] = jnp.dot(ckv.astype(BF16), wkv_sc[...], preferred_element_type=F32).astype(kv_ref.dtype)
    kr_ref[...] = u[:, o2:o3]
    uh_ref[...] = u[:, o3:]


def in1_proj(x, g, modtab, cond, w_in, q_norm, kv_norm, w_q_up, w_kv_up):
    tokens = x.shape[0]
    tm = TM_IN1
    nkv = MLA_HEADS * (NOPE + VDIM)
    const = lambda i: (0, 0)
    zero = lambda i: 0
    once = pl.Buffered(1)
    outs = (jax.ShapeDtypeStruct((tokens, MLA_HEADS * NOPE), BF16),
            jax.ShapeDtypeStruct((tokens, MLA_HEADS * ROPE), F32),
            jax.ShapeDtypeStruct((tokens, KV_RANK), F32),
            jax.ShapeDtypeStruct((tokens, ROPE), F32),
            jax.ShapeDtypeStruct((tokens, nkv), BF16),
            jax.ShapeDtypeStruct((tokens, 3 * HY_W), F32))
    row = lambda w: pl.BlockSpec((tm, w), lambda i: (i, 0))
    return pl.pallas_call(
        _in1_kernel,
        out_shape=outs,
        grid=(tokens // tm,),
        in_specs=[row(D),
                  pl.BlockSpec((1, D), const),
                  _mod_spec(0, cond, tm, D, zero),
                  _mod_spec(1, cond, tm, D, zero),
                  pl.BlockSpec((IN1, D), const, pipeline_mode=once),
                  pl.BlockSpec((1, Q_RANK), const),
                  pl.BlockSpec((1, KV_RANK), const),
                  pl.BlockSpec((Q_RANK, MLA_HEADS * QK_DIM), const, pipeline_mode=once),
                  pl.BlockSpec((KV_RANK, nkv), const, pipeline_mode=once)],
        out_specs=tuple(row(o.shape[1]) for o in outs),
        scratch_shapes=[pltpu.VMEM((IN1, D), BF16), pltpu.VMEM((Q_RANK, MLA_HEADS * QK_DIM), BF16),
                        pltpu.VMEM((KV_RANK, nkv), BF16)],
        compiler_params=_cparams(("arbitrary",)),
        name="in1_proj",
    )(x, g.reshape(1, D), modtab, modtab, w_in.T, q_norm.reshape(1, Q_RANK), kv_norm.reshape(1, KV_RANK),
      w_q_up, w_kv_up)


def _mm_kernel(a_ref, w_ref, o_ref):
    o_ref[...] = jnp.dot(a_ref[...].astype(BF16), w_ref[...].astype(BF16),
                         preferred_element_type=F32).astype(o_ref.dtype)


def kv_up(ckv, w_kv_up):
    rows = ckv.shape[0]
    n = w_kv_up.shape[1]
    return pl.pallas_call(
        _mm_kernel,
        out_shape=jax.ShapeDtypeStruct((rows, n), BF16),
        grid=(rows // TM,),
        in_specs=[pl.BlockSpec((TM, KV_RANK), lambda i: (i, 0)), pl.BlockSpec((KV_RANK, n), lambda i: (0, 0))],
        out_specs=pl.BlockSpec((TM, n), lambda i: (i, 0)),
        compiler_params=_cparams(("parallel",)),
        name="kv_up",
    )(ckv, w_kv_up)


_NT = (((1,), (1,)), ((), ()))
_SCALE = 1.0 / math.sqrt(QK_DIM)


def _fill_rope_tables(cos_ref, sin_ref):
    n, width = cos_ref.shape
    n_grid_rows = n // GRID_W
    n_freq = ROPE // 4

    def trig(count):
        lane = lax.broadcasted_iota(jnp.int32, (count, width), 1)
        j = lane & (ROPE // 2 - 1)
        inv = jnp.exp((j & (n_freq - 1)).astype(F32) * (-math.log(ROPE_THETA) / n_freq))
        ang = lax.broadcasted_iota(jnp.int32, (count, width), 0).astype(F32) * inv
        return jnp.cos(ang), jnp.sin(ang), j < n_freq

    cos_c, sin_c, by_row = trig(GRID_W)
    cos_r, sin_r, _ = trig(n_grid_rows)
    for r in range(n_grid_rows):
        rows = slice(r * GRID_W, (r + 1) * GRID_W)
        cos_ref[rows, :] = jnp.where(by_row, jnp.broadcast_to(cos_r[r:r + 1], cos_c.shape), cos_c)
        sin_ref[rows, :] = jnp.where(by_row, jnp.broadcast_to(sin_r[r:r + 1], sin_c.shape), sin_c)


def _rope(x, cos, sin):
    width = x.shape[1]
    lane = lax.broadcasted_iota(jnp.int32, x.shape, 1)
    first_half = (lane & (ROPE - 1)) < ROPE // 2
    xr = jnp.where(first_half, -pltpu.roll(x, width - ROPE // 2, 1), pltpu.roll(x, ROPE // 2, 1))
    return x * cos + xr * sin


def _ones_column(n):
    lane = lax.broadcasted_iota(jnp.int32, (n, VDIM), 1)
    return jnp.where(lane == 0, 1.0, 0.0).astype(BF16)


def _head_attention(qcat, kcat, vaug):
    s = lax.dot_general(qcat, kcat, _NT, preferred_element_type=F32)
    p = jnp.exp(s - jnp.max(s, axis=-1, keepdims=True)).astype(BF16)
    oa = jnp.dot(p, vaug, preferred_element_type=F32)
    return oa[:, :VDIM] / oa[:, VDIM:VDIM + 1]


def _attn_ctx_kernel(qn_ref, qpe_ref, kv_ref, kr_ref, o_ref, *, seq_len):
    n = qn_ref.shape[0]
    n_seq = n // seq_len
    ones = _ones_column(n)
    kpe = kr_ref[...].astype(BF16)
    per_seq = lambda a: a.reshape(n_seq, seq_len, a.shape[-1])
    for h in range(MLA_HEADS):
        c0 = h * (NOPE + VDIM)
        qcat = per_seq(jnp.concatenate([qn_ref[:, h * NOPE:(h + 1) * NOPE],
                                        qpe_ref[:, h * ROPE:(h + 1) * ROPE].astype(BF16)], axis=1))
        kcat = per_seq(jnp.concatenate([kv_ref[:, c0:c0 + NOPE], kpe], axis=1))
        vaug = per_seq(jnp.concatenate([kv_ref[:, c0 + NOPE:c0 + NOPE + VDIM], ones], axis=1))
        s = jnp.einsum("bqd,bkd->bqk", qcat, kcat, preferred_element_type=F32)
        p = jnp.exp(s - jnp.max(s, axis=-1, keepdims=True)).astype(BF16)
        oa = jnp.einsum("bqk,bkd->bqd", p, vaug, preferred_element_type=F32)
        o = oa[:, :, :VDIM] / oa[:, :, VDIM:VDIM + 1]
        o_ref[:, h * VDIM:(h + 1) * VDIM] = o.reshape(n, VDIM).astype(o_ref.dtype)


def attn_ctx(qnope, qpe, kv, kr, seq_len):
    tokens = qnope.shape[0]
    rows = ATTN_CTX_SEQS * seq_len
    blk = lambda w: pl.BlockSpec((rows, w), lambda s: (s, 0))
    return pl.pallas_call(
        functools.partial(_attn_ctx_kernel, seq_len=seq_len),
        out_shape=jax.ShapeDtypeStruct((tokens, MLA_HEADS * VDIM), BF16),
        grid=(tokens // rows,),
        in_specs=[blk(MLA_HEADS * NOPE), blk(MLA_HEADS * ROPE), blk(MLA_HEADS * (NOPE + VDIM)), blk(ROPE)],
        out_specs=blk(MLA_HEADS * VDIM),
        compiler_params=_cparams(("parallel",)),
        name="attn_ctx",
    )(qnope, qpe, kv, kr)


def _attn_lat_kernel(qn_ref, qpe_ref, kvc_ref, krc_ref, kvl_ref, krl_ref, o_ref, kcat_sc, vaug_sc, cos_sc, sin_sc):
    tq = qn_ref.shape[0]
    n_ctx = krc_ref.shape[0]
    n_lat = krl_ref.shape[0]

    @pl.when(pl.program_id(1) == 0)
    def _():
        _fill_rope_tables(cos_sc, sin_sc)
        kr2 = jnp.concatenate([krl_ref[...], krl_ref[...]], axis=1)
        kpe_lat = _rope(kr2, cos_sc[...], sin_sc[...])[:, :ROPE].astype(BF16)
        kpe_ctx = krc_ref[...].astype(BF16)
        ones_c, ones_l = _ones_column(n_ctx), _ones_column(n_lat)
        for h in range(MLA_HEADS):
            c0 = h * (NOPE + VDIM)
            for r0, nr, kv_ref, kpe, ones in ((0, n_ctx, kvc_ref, kpe_ctx, ones_c), (n_ctx, n_lat, kvl_ref, kpe_lat, ones_l)):
                kcat_sc[h, r0:r0 + nr, 0:NOPE] = kv_ref[:, c0:c0 + NOPE]
                kcat_sc[h, r0:r0 + nr, NOPE:QK_DIM] = kpe
                vaug_sc[h, r0:r0 + nr, 0:VDIM] = kv_ref[:, c0 + NOPE:c0 + NOPE + VDIM]
                vaug_sc[h, r0:r0 + nr, VDIM:2 * VDIM] = ones

    q0 = pl.multiple_of(pl.program_id(1) * tq, tq)
    rep = lambda a: jnp.concatenate([a] * (MLA_HEADS // 2), axis=1)
    qp_all = _rope(qpe_ref[...], rep(cos_sc[pl.ds(q0, tq), :]), rep(sin_sc[pl.ds(q0, tq), :])).astype(BF16)
    for h in range(MLA_HEADS):
        qcat = jnp.concatenate([qn_ref[:, h * NOPE:(h + 1) * NOPE], qp_all[:, h * ROPE:(h + 1) * ROPE]], axis=1)
        o_ref[:, h * VDIM:(h + 1) * VDIM] = _head_attention(qcat, kcat_sc[h], vaug_sc[h]).astype(o_ref.dtype)


def attn_lat(qnope, qpe, kv_ctx, kr_ctx, kv_lat, kr_lat, seq_len, ctx_len):
    tokens = qnope.shape[0]
    nq = seq_len // TQ
    qblk = lambda w: pl.BlockSpec((TQ, w), lambda b, i: (b * nq + i, 0))
    seq = lambda n, w: pl.BlockSpec((n, w), lambda b, i: (b, 0))
    nkv = MLA_HEADS * (NOPE + VDIM)
    n_keys = ctx_len + seq_len
    return pl.pallas_call(
        _attn_lat_kernel,
        out_shape=jax.ShapeDtypeStruct((tokens, MLA_HEADS * VDIM), BF16),
        grid=(tokens // seq_len, nq),
        in_specs=[qblk(MLA_HEADS * NOPE), qblk(MLA_HEADS * ROPE), seq(ctx_len, nkv), seq(ctx_len, ROPE),
                  seq(seq_len, nkv), seq(seq_len, ROPE)],
        out_specs=qblk(MLA_HEADS * VDIM),
        scratch_shapes=[pltpu.VMEM((MLA_HEADS, n_keys, QK_DIM), BF16),
                        pltpu.VMEM((MLA_HEADS, n_keys, 2 * VDIM), BF16),
                        pltpu.VMEM((seq_len, 2 * ROPE), F32), pltpu.VMEM((seq_len, 2 * ROPE), F32)],
        compiler_params=_cparams(("parallel", "arbitrary")),
        name="attn_lat",
    )(qnope, qpe, kv_ctx, kr_ctx, kv_lat, kr_lat)


def _dft_kernel(o_ref):
    tr, n = o_ref.shape[1], o_ref.shape[2]
    nb = n // V7X_LANES
    f = pl.program_id(0) * tr + lax.broadcasted_iota(jnp.int32, (tr, V7X_LANES), 0)
    j = lax.broadcasted_iota(jnp.int32, (tr, V7X_LANES), 1)

    def cos_sin(m):
        ang = (m & (2 * n - 1)).astype(F32) * (math.pi / n)
        return jnp.cos(ang), jnp.sin(ang)

    cj, sj = cos_sin(f * j)
    cb, sb = cos_sin(f * (j * V7X_LANES))
    for b in range(nb):
        cbb, sbb = cb[:, b:b + 1], sb[:, b:b + 1]
        cols = slice(b * V7X_LANES, (b + 1) * V7X_LANES)
        o_ref[0, :, cols] = (cbb * cj - sbb * sj).astype(o_ref.dtype)
        o_ref[1, :, cols] = (sbb * cj + cbb * sj).astype(o_ref.dtype)


def dft_tables(n):
    tr = 128
    return pl.pallas_call(
        _dft_kernel,
        out_shape=jax.ShapeDtypeStruct((2, n, n), BF16),
        grid=(n // tr,),
        out_specs=pl.BlockSpec((2, tr, n), lambda i: (0, i, 0)),
        compiler_params=_cparams(("parallel",)),
        name="dft_tables",
    )()


def _split_dot(table, x):
    hi = x.astype(BF16)
    lo = (x - hi.astype(F32)).astype(BF16)
    return (jnp.dot(table, hi, preferred_element_type=F32) + jnp.dot(table, lo, preferred_element_type=F32))


def _hy_filter_kernel(cs_ref, pack_ref, w2_ref, w3_ref, kr_ref, ks_ref, kny_ref):
    n = cs_ref.shape[1]
    row = lax.broadcasted_iota(jnp.int32, (n, V7X_LANES), 0).astype(F32)
    lane = lax.broadcasted_iota(jnp.int32, (n, V7X_LANES), 1)
    t = row * (1.0 / (n - 1))
    w = (2.0 * math.pi) * row / n
    band = jnp.where(lane <= HY_BANDS, lane - 1, lane - 1 - HY_BANDS).astype(F32)
    freq = 1e-4 + band * ((HY_BANDS - 1 - 1e-4) / (HY_BANDS - 1))
    arg = jnp.where(lane <= HY_BANDS, freq * w + 0.5 * math.pi, -(freq * w))
    z = jnp.where(lane == 0, t, jnp.where(lane <= 2 * HY_BANDS, jnp.sin(arg), 0.0))
    hid = jnp.sin(_dot3(z, pack_ref[0:V7X_LANES, :]) + pack_ref[V7X_LANES:V7X_LANES + 1, :])
    hid = jnp.sin(_dot3(hid, w2_ref[...]) + pack_ref[V7X_LANES + 1:V7X_LANES + 2, :])
    hf = _dot3(hid, w3_ref[...])

    rowc = lax.broadcasted_iota(jnp.int32, (n, HY_W), 0)
    chan = lax.broadcasted_iota(jnp.int32, (n, HY_W), 1).astype(F32)
    max_decay = math.log(HY_TARGET) / HY_FAST_DECAY
    min_decay = math.log(HY_TARGET) / HY_SLOW_DECAY
    deltas = min_decay + chan * ((max_decay - min_decay) / (HY_W - 1))
    decay = jnp.exp(-(rowc.astype(F32) * (1.0 / (n - 1))) * jnp.abs(deltas))
    h_fwd = hf[:, :HY_W] * decay
    h_bwd = jnp.where(rowc == 0, 0.0, hf[:, HY_W:] * decay)
    norm = jnp.sum(jnp.abs(h_fwd) + jnp.abs(h_bwd), axis=0, keepdims=True)
    even = (h_fwd + h_bwd) / norm
    odd = (h_fwd - h_bwd) / norm
    cf = jnp.where(rowc == 0, 1.0, 2.0) * (1.0 / (2 * n))
    kr_ref[...] = cf * _split_dot(cs_ref[0], even)
    ks_ref[...] = cf * _split_dot(cs_ref[1], odd)
    sgn = jnp.where((rowc & 1) == 1, -1.0, 1.0)
    kny_ref[...] = jnp.sum(sgn * even, axis=0, keepdims=True) * (1.0 / (2 * n))


def hy_filter(cs, pack, w2, w3):
    n = cs.shape[1]
    full = lambda a: pl.BlockSpec(a.shape, lambda: (0,) * a.ndim)
    args = (cs, pack, w2, w3)
    return pl.pallas_call(
        _hy_filter_kernel,
        out_shape=(jax.ShapeDtypeStruct((n, HY_W), F32), jax.ShapeDtypeStruct((n, HY_W), F32),
                   jax.ShapeDtypeStruct((1, HY_W), F32)),
        in_specs=[full(a) for a in args],
        out_specs=(pl.BlockSpec((n, HY_W), lambda: (0, 0)), pl.BlockSpec((n, HY_W), lambda: (0, 0)),
                   pl.BlockSpec((1, HY_W), lambda: (0, 0))),
        compiler_params=pltpu.CompilerParams(vmem_limit_bytes=V7X_VMEM_LIMIT_BYTES),
        name="hy_filter",
    )(*args)


def _hyena_kernel(u0_ref, u1_ref, u2_ref, sw_ref, sb_ref, cs_ref, kr_ref, ks_ref, kny_ref, bias_ref, o_ref,
                  *, seq_len):
    n, cb = u0_ref.shape
    n_seq = n // seq_len
    t = lax.broadcasted_iota(jnp.int32, (n, cb), 0) & (seq_len - 1)

    def short_conv(u_ref, k):
        u = u_ref[...]
        w = sw_ref[:, k * cb:(k + 1) * cb]
        return (sb_ref[:, k * cb:(k + 1) * cb] + w[0:1] * _shift_rows(u, 1, t, seq_len) + w[1:2] * u
                + w[2:3] * _shift_rows(u, -1, t, seq_len))

    x0 = short_conv(u0_ref, 0)
    z = short_conv(u1_ref, 1) * short_conv(u2_ref, 2)
    wide = lambda a: jnp.concatenate([a[q * seq_len:(q + 1) * seq_len] for q in range(n_seq)], axis=1)
    rep = lambda a: jnp.concatenate([a] * n_seq, axis=1)
    zw = wide(z)
    zb = zw.astype(BF16)
    c, s = cs_ref[0], cs_ref[1]
    ur = jnp.dot(c, zb, preferred_element_type=F32)
    us = jnp.dot(s, zb, preferred_element_type=F32)
    sgn = jnp.where((lax.broadcasted_iota(jnp.int32, zw.shape, 0) & 1) == 1, -1.0, 1.0)
    uny = jnp.sum(sgn * zw, axis=0, keepdims=True)
    kr, ks = rep(kr_ref[...]), rep(ks_ref[...])
    yr = (ur * kr - us * ks).astype(BF16)
    ys = (ur * ks + us * kr).astype(BF16)
    yw = jnp.dot(c, yr, preferred_element_type=F32) + jnp.dot(s, ys, preferred_element_type=F32)
    yw = yw + sgn * (uny * rep(kny_ref[...]))
    y = jnp.concatenate([yw[:, q * cb:(q + 1) * cb] for q in range(n_seq)], axis=0)
    o_ref[...] = (x0 * (y + bias_ref[...] * z)).astype(o_ref.dtype)


def hyena(uh, seq_len, short_w, short_b, cs, kr, ks, kny, bias):
    tokens = uh.shape[0]
    cb = HY_CB
    nc = HY_W // cb
    rows = max(seq_len, HY_ROWS)
    assert seq_len & (seq_len - 1) == 0 and rows % seq_len == 0
    ublk = lambda k: pl.BlockSpec((rows, cb), lambda s, c: (s, k * nc + c))
    chan = lambda r: pl.BlockSpec((r, cb), lambda s, c: (0, c))
    return pl.pallas_call(
        functools.partial(_hyena_kernel, seq_len=seq_len),
        out_shape=jax.ShapeDtypeStruct((tokens, HY_W), BF16),
        grid=(tokens // rows, nc),
        in_specs=[ublk(0), ublk(1), ublk(2),
                  pl.BlockSpec((None, 3, 3 * cb), lambda s, c: (c, 0, 0)),
                  pl.BlockSpec((None, 1, 3 * cb), lambda s, c: (c, 0, 0)),
                  pl.BlockSpec((2, seq_len, seq_len), lambda s, c: (0, 0, 0)),
                  chan(seq_len), chan(seq_len), chan(1), chan(1)],
        out_specs=pl.BlockSpec((rows, cb), lambda s, c: (s, c)),
        compiler_params=_cparams(("parallel", "parallel")),
        name="hyena",
    )(uh, uh, uh, short_w, short_b, cs, kr, ks, kny, bias)


META_E1, META_E2, META_R1, META_R2, META_G1, META_G2 = range(6)


def _route_kernel(p0_ref, p1_ref, p2_ref, wo_ref, x_ref, g1_ref, g_ref, sh_ref, sc_ref, wr_ref,
                  x1_ref, h_ref, meta_ref, meta_t_ref, cnt_ref, run_sc, wo_sc):
    tm = x_ref.shape[0]
    lane = lax.broadcasted_iota(jnp.int32, (tm, V7X_LANES), 1)

    @pl.when(pl.program_id(0) == 0)
    def _():
        run_sc[...] = jnp.zeros_like(run_sc)
        wo_sc[...] = wo_ref[...].astype(BF16)

    kb = p0_ref.shape[1]
    m = jnp.dot(p0_ref[...], wo_sc[0:kb, :], preferred_element_type=F32)
    m += jnp.dot(p1_ref[...], wo_sc[kb:2 * kb, :], preferred_element_type=F32)
    m += jnp.dot(p2_ref[...], wo_sc[2 * kb:3 * kb, :], preferred_element_type=F32)
    x1 = x_ref[...] + g1_ref[...] * m
    x1_ref[...] = x1
    h = _norm_mod(x1, g_ref[...], sh_ref[...], sc_ref[...])
    h_ref[...] = h
    logits = _dot3(h, wr_ref[0:D, :]) + wr_ref[D:D + 1, :]
    lg = jnp.where(lane < N_EXPERTS, logits, -jnp.inf)
    l1 = jnp.max(lg, axis=-1, keepdims=True)
    i1 = jnp.min(jnp.where(lg == l1, lane, V7X_LANES), axis=-1, keepdims=True)
    rest = jnp.where(lane == i1, -jnp.inf, lg)
    l2 = jnp.max(rest, axis=-1, keepdims=True)
    i2 = jnp.min(jnp.where(rest == l2, lane, V7X_LANES), axis=-1, keepdims=True)
    gap = jnp.exp(l2 - l1)
    gate1 = 1.0 / (1.0 + gap)
    gate2 = gap * gate1
    m1 = lane == i1
    m2 = lane == i2
    chosen = jnp.where(m1 | m2, 1.0, 0.0)
    r = lax.broadcasted_iota(jnp.int32, (tm, tm), 0)
    c = lax.broadcasted_iota(jnp.int32, (tm, tm), 1)
    tri = jnp.where(c < r, 1.0, 0.0).astype(BF16)
    before = jnp.dot(tri, chosen.astype(BF16), preferred_element_type=F32) + run_sc[0:1, :]
    rank1 = jnp.sum(jnp.where(m1, before, 0.0), axis=-1, keepdims=True)
    rank2 = jnp.sum(jnp.where(m2, before, 0.0), axis=-1, keepdims=True)
    vals = (i1.astype(F32), i2.astype(F32), rank1, rank2, gate1, gate2)
    meta = jnp.zeros((tm, V7X_LANES), F32)
    for k, v in enumerate(vals):
        meta = jnp.where(lane == k, v, meta)
    meta_ref[...] = meta
    meta_t_ref[...] = meta.T[:V7X_SUBLANES]
    run_sc[...] = run_sc[...] + jnp.sum(chosen, axis=0, keepdims=True)
    cnt_ref[...] = run_sc[...]


def mix_route(parts, w_out, x, g, modtab, cond, router_pack):
    tokens = x.shape[0]
    tm = TM_ROUTE
    kb = MIX_SLAB
    zero = lambda i: 0
    const = lambda i: (0, 0)
    rows = lambda w: pl.BlockSpec((tm, w), lambda i: (i, 0))
    lhs_specs = [pl.BlockSpec((tm, kb), (lambda i, cbk=cbk: (i, cbk))) for _, cbk in parts]
    return pl.pallas_call(
        _route_kernel,
        out_shape=(jax.ShapeDtypeStruct((tokens, D), F32),
                   jax.ShapeDtypeStruct((tokens, D), F32),
                   jax.ShapeDtypeStruct((tokens, V7X_LANES), F32),
                   jax.ShapeDtypeStruct((V7X_SUBLANES, tokens), F32),
                   jax.ShapeDtypeStruct((V7X_SUBLANES, V7X_LANES), F32)),
        grid=(tokens // tm,),
        in_specs=lhs_specs + [
            pl.BlockSpec((len(parts) * kb, D), const, pipeline_mode=pl.Buffered(1)),
            rows(D),
            _mod_spec(2, cond, tm, D, zero),
            pl.BlockSpec((1, D), const),
            _mod_spec(3, cond, tm, D, zero),
            _mod_spec(4, cond, tm, D, zero),
            pl.BlockSpec((D + V7X_SUBLANES, V7X_LANES), const)],
        out_specs=(rows(D), rows(D), rows(V7X_LANES),
                   pl.BlockSpec((V7X_SUBLANES, tm), lambda i: (0, i)),
                   pl.BlockSpec((V7X_SUBLANES, V7X_LANES), const)),
        scratch_shapes=[pltpu.VMEM((V7X_SUBLANES, V7X_LANES), F32), pltpu.VMEM((len(parts) * kb, D), BF16)],
        compiler_params=_cparams(("arbitrary",)),
        name="mix_route",
    )(*[a for a, _ in parts], w_out, x, modtab, g.reshape(1, D), modtab, modtab, router_pack)


def _row_copy(src_ref, src_row, dst_ref, dst_row, sem):
    return pltpu.make_async_copy(src_ref.at[pl.ds(src_row, 1)], dst_ref.at[pl.ds(dst_row, 1)], sem)


_PAD_BULK = (256, 128, 64, 32, 16, 8)


def _zero_fill(hs_ref, zero_sc, sem, pads_ref, n_tail_max, wait):
    tmr = zero_sc.shape[0]

    def copy(rows, dst):
        cp = pltpu.make_async_copy(zero_sc.at[pl.ds(0, rows)], hs_ref.at[pl.ds(dst, rows)], sem)
        cp.wait() if wait else cp.start()

    for e in range(N_EXPERTS):
        start, n = pads_ref[e], pads_ref[N_EXPERTS + e]
        head = jnp.minimum((-start) & (V7X_SUBLANES - 1), n)
        for r in range(V7X_SUBLANES - 1):
            @pl.when(r < head)
            def _():
                copy(1, start + r)
        body = start + head
        rem = n - head
        for k in _PAD_BULK:
            @pl.when((rem & k) != 0)
            def _():
                copy(k, pl.multiple_of(body + (rem & ~(2 * k - 1)), V7X_SUBLANES))
    tail_start, tail_tiles = pads_ref[2 * N_EXPERTS], pads_ref[2 * N_EXPERTS + 1]
    for t in range(n_tail_max):
        @pl.when(t < tail_tiles)
        def _():
            copy(tmr, pl.multiple_of(tail_start + t * tmr, tmr))


def _dispatch_kernel(pos_ref, pads_ref, ha_ref, hb_ref, hs_ref, zero_sc, sem, zsem, *, n_a, n_tail_max):
    tm = ha_ref.shape[0]
    n_tok = pos_ref.shape[0] // 2
    i = pl.program_id(0)
    base = i * tm

    @pl.when(i == 0)
    def _():
        zero_sc[...] = jnp.zeros_like(zero_sc)
        _zero_fill(hs_ref, zero_sc, zsem, pads_ref, n_tail_max, wait=False)

    def scatter(h_ref):
        def issue(r, carry):
            _row_copy(h_ref, r, hs_ref, pos_ref[base + r], sem).start(priority=0)
            _row_copy(h_ref, r, hs_ref, pos_ref[n_tok + base + r], sem).start(priority=1)
            return carry

        lax.fori_loop(0, tm, issue, 0, unroll=8)
        for _ in range(2):
            pltpu.make_async_copy(h_ref, hs_ref.at[pl.ds(0, tm)], sem).wait()

    @pl.when(i < n_a)
    def _():
        scatter(ha_ref)

    @pl.when(i >= n_a)
    def _():
        scatter(hb_ref)

    @pl.when(i == 0)
    def _():
        _zero_fill(hs_ref, zero_sc, zsem, pads_ref, n_tail_max, wait=True)


def moe_dispatch(pos, pads, hs_rows, h_a, h_b):
    tm = TM_ROUTE
    n_a, n_b = h_a.shape[0] // tm, h_b.shape[0] // tm
    n_tail_max = hs_rows // TM_EXPERT - (2 * (h_a.shape[0] + h_b.shape[0])) // TM_EXPERT
    return pl.pallas_call(
        functools.partial(_dispatch_kernel, n_a=n_a, n_tail_max=n_tail_max),
        out_shape=jax.ShapeDtypeStruct((hs_rows, D), F32),
        grid_spec=pltpu.PrefetchScalarGridSpec(
            num_scalar_prefetch=2,
            grid=(n_a + n_b,),
            in_specs=[pl.BlockSpec((tm, D), lambda i, *pf: (jnp.minimum(i, n_a - 1), 0)),
                      pl.BlockSpec((tm, D), lambda i, *pf: (jnp.clip(i - n_a, 0, n_b - 1), 0))],
            out_specs=pl.BlockSpec(memory_space=pl.ANY),
            scratch_shapes=[pltpu.VMEM((TM_EXPERT, D), F32), pltpu.SemaphoreType.DMA(()),
                            pltpu.SemaphoreType.DMA(())]),
        compiler_params=_cparams(("arbitrary",)),
        name="moe_dispatch",
    )(pos, pads, h_a, h_b)


def _experts_kernel(te_ref, sg_ref, su_ref, sd_ref, nv_ref, hs_ref, wg_ref, wu_ref, wd_ref, y_ref,
                    wg_sc, wu_sc, wd_sc):
    del sg_ref, su_ref, sd_ref
    j = pl.program_id(0)
    e = te_ref[j]
    e_prev = te_ref[jnp.maximum(j - 1, 0)]
    n_valid = nv_ref[j]
    half = y_ref.shape[0] // 2

    @pl.when((j == 0) | (e != e_prev))
    def _():
        wg_sc[...] = wg_ref[...].astype(BF16)
        wu_sc[...] = wu_ref[...].astype(BF16)
        wd_sc[...] = wd_ref[...].astype(BF16)

    def swiglu(rows):
        h = hs_ref[rows, :].astype(BF16)
        y = None
        for c0 in range(0, D_FF_EXPERT, MOE_CHUNK):
            c1 = min(c0 + MOE_CHUNK, D_FF_EXPERT)
            hg = jnp.dot(h, wg_sc[:, c0:c1], preferred_element_type=F32)
            hu = jnp.dot(h, wu_sc[:, c0:c1], preferred_element_type=F32)
            act = (_silu(hg) * hu).astype(BF16)
            yc = jnp.dot(act, wd_sc[c0:c1, :], preferred_element_type=F32)
            y = yc if y is None else y + yc
        y_ref[rows, :] = y

    @pl.when(n_valid > half)
    def _():
        swiglu(slice(None))

    @pl.when((n_valid > 0) & (n_valid <= half))
    def _():
        swiglu(slice(0, half))
        y_ref[half:, :] = jnp.zeros((half, D), F32)

    @pl.when(n_valid == 0)
    def _():
        y_ref[...] = jnp.zeros_like(y_ref)


def moe_experts(tile_expert, stages, tile_valid, hs, e_gate, e_up, e_down):
    rows = hs.shape[0]
    tmr = TM_EXPERT
    wspec = lambda shape, k: pl.BlockSpec((None,) + shape, lambda j, *pf: (pf[1 + k][j], 0, 0))
    return pl.pallas_call(
        _experts_kernel,
        out_shape=jax.ShapeDtypeStruct((rows, D), F32),
        grid_spec=pltpu.PrefetchScalarGridSpec(
            num_scalar_prefetch=5,
            grid=(rows // tmr,),
            in_specs=[pl.BlockSpec((tmr, D), lambda j, *pf: (j, 0)),
                      wspec((D, D_FF_EXPERT), 0), wspec((D, D_FF_EXPERT), 1), wspec((D_FF_EXPERT, D), 2)],
            out_specs=pl.BlockSpec((tmr, D), lambda j, *pf: (j, 0)),
            scratch_shapes=[pltpu.VMEM((D, D_FF_EXPERT), BF16), pltpu.VMEM((D, D_FF_EXPERT), BF16),
                            pltpu.VMEM((D_FF_EXPERT, D), BF16)]),
        compiler_params=_cparams(("arbitrary",)),
        name="moe_experts",
    )(tile_expert, *stages, tile_valid, hs, e_gate, e_up, e_down)


def _combine_kernel(pos_ref, x_ref, meta_ref, gt_ref, fg_ref, y_ref, o_ref, b1_sc, b2_sc, sem):
    tm = x_ref.shape[0]
    n_tok = pos_ref.shape[0] // 2
    i = pl.program_id(0)

    def gather(tile, slot):
        base = tile * tm

        def issue(r, carry):
            _row_copy(y_ref, pos_ref[base + r], b1_sc.at[slot], r, sem.at[slot]).start(priority=0)
            _row_copy(y_ref, pos_ref[n_tok + base + r], b2_sc.at[slot], r, sem.at[slot]).start(priority=1)
            return carry

        lax.fori_loop(0, tm, issue, 0, unroll=8)

    @pl.when(i == 0)
    def _():
        gather(0, 0)

    @pl.when(i + 1 < pl.num_programs(0))
    def _():
        gather(i + 1, (i + 1) % 2)

    slot = i % 2
    pltpu.make_async_copy(y_ref.at[pl.ds(0, tm)], b1_sc.at[slot], sem.at[slot]).wait()
    pltpu.make_async_copy(y_ref.at[pl.ds(0, tm)], b2_sc.at[slot], sem.at[slot]).wait()

    meta = meta_ref[...]
    lane = lax.broadcasted_iota(jnp.int32, meta.shape, 1)
    g1 = jnp.sum(jnp.where(lane == META_G1, meta, 0.0), axis=-1, keepdims=True)
    g2 = jnp.sum(jnp.where(lane == META_G2, meta, 0.0), axis=-1, keepdims=True)
    x = x_ref[...] + gt_ref[...] * (g1 * b1_sc[slot] + g2 * b2_sc[slot])
    o_ref[...] = _rms(x, fg_ref[...])


def moe_combine(pos, x, meta, modtab, cond, final_g, y):
    tokens = x.shape[0]
    tm = TM_COMBINE
    return pl.pallas_call(
        _combine_kernel,
        out_shape=jax.ShapeDtypeStruct((tokens, D), F32),
        grid_spec=pltpu.PrefetchScalarGridSpec(
            num_scalar_prefetch=1,
            grid=(tokens // tm,),
            in_specs=[pl.BlockSpec((tm, D), lambda i, pos: (i, 0)),
                      pl.BlockSpec((tm, V7X_LANES), lambda i, pos: (i, 0)),
                      _mod_spec(5, cond, tm, D, lambda i, pos: 0),
                      pl.BlockSpec((1, D), lambda i, pos: (0, 0)),
                      pl.BlockSpec(memory_space=pl.ANY)],
            out_specs=pl.BlockSpec((tm, D), lambda i, pos: (i, 0)),
            scratch_shapes=[pltpu.VMEM((2, tm, D), F32), pltpu.VMEM((2, tm, D), F32),
                            pltpu.SemaphoreType.DMA((2,))]),
        compiler_params=_cparams(("arbitrary",)),
        name="moe_combine",
    )(pos, x, meta, modtab, final_g.reshape(1, D), y)


def moe_plan(metas, counts):
    tmr = TM_EXPERT
    cnts = [c[0, :N_EXPERTS].astype(jnp.int32) for c in counts]
    total = functools.reduce(jnp.add, cnts)
    padded = ((total + tmr - 1) // tmr) * tmr
    ends = jnp.cumsum(padded)
    starts = ends - padded
    n_rows = sum(m.shape[1] for m in metas) * 2 + N_EXPERTS * tmr
    n_tiles = n_rows // tmr
    tile_start = jnp.arange(n_tiles, dtype=jnp.int32) * tmr
    tile_expert = jnp.minimum(jnp.sum(tile_start[:, None] >= ends[None, :], axis=1), N_EXPERTS - 1).astype(jnp.int32)
    group_of_tile = jnp.sum(tile_start[:, None] >= ends[None, :], axis=1)
    real_end = jnp.sum(jnp.where(group_of_tile[:, None] == jnp.arange(N_EXPERTS)[None, :],
                                 (starts + total)[None, :], 0), axis=1)
    tile_valid = jnp.clip(real_end - tile_start, 0, tmr).astype(jnp.int32)
    eid = jnp.arange(N_EXPERTS, dtype=jnp.int32)
    later = jnp.where((eid[None, :] > eid[:, None]) & (padded[None, :] > 0), eid[None, :], N_EXPERTS)
    nxt = jnp.min(later, axis=1)
    next_used = jnp.where(nxt == N_EXPERTS, eid, nxt)
    pick = lambda table: jnp.sum(jnp.where(tile_expert[:, None] == eid[None, :], table[None, :], 0), axis=1)
    k_in_group = (tile_start - pick(starts)) // tmr
    tile_next = pick(next_used)
    stages = [jnp.where(k_in_group < k, tile_expert, tile_next).astype(jnp.int32) for k in (1, 2, 3)]
    pos, p1s, p2s = [], [], []
    base = jnp.zeros((N_EXPERTS,), jnp.int32)
    for m, c in zip(metas, cnts):
        first = starts + base
        sel = lambda field: m[field].astype(jnp.int32)
        lookup = lambda e: jnp.sum(jnp.where(e[:, None] == jnp.arange(N_EXPERTS)[None, :], first[None, :], 0), axis=1)
        p1 = lookup(sel(META_E1)) + sel(META_R1)
        p2 = lookup(sel(META_E2)) + sel(META_R2)
        pos.append(jnp.concatenate([p1, p2]).astype(jnp.int32))
        p1s.append(p1)
        p2s.append(p2)
        base = base + c
    pos_all = jnp.concatenate(p1s + p2s).astype(jnp.int32)
    pads = jnp.concatenate([starts + total, padded - total,
                            jnp.stack([ends[-1], n_tiles - ends[-1] // tmr])]).astype(jnp.int32)
    return pos, pos_all, pads, tile_expert, stages, tile_valid, n_rows


def _pad_to(a, shape):
    return jnp.pad(a, [(0, t - s) for s, t in zip(a.shape, shape)])


def _regroup_chunks(a, cb):
    r = a.shape[0]
    return a.reshape(r, 3, HY_W // cb, cb).transpose(2, 0, 1, 3).reshape(HY_W // cb, r, 3 * cb)


def kernel(x_prompt, x_sample, state_l0_lru, cache_l1_ckv, cache_l1_krope, c, c_ctx, l0_norm1, l0_norm2, l0_w_mod, l0_b_mod, l0_w_in, l0_conv_a, l0_lru_conv_w, l0_lru_conv_b, l0_lru_wa, l0_lru_ba, l0_lru_wi, l0_lru_bi, l0_lru_lambda, l0_w_out, l0_ffn_gate, l0_ffn_up, l0_ffn_down, l1_norm1, l1_norm2, l1_w_mod, l1_b_mod, l1_w_in, l1_q_norm, l1_kv_norm, l1_w_q_up, l1_w_kv_up, l1_hy_short_w, l1_hy_short_b, l1_hy_f_w1, l1_hy_f_b1, l1_hy_f_w2, l1_hy_f_b2, l1_hy_f_w3, l1_hy_bias, l1_w_out, l1_router_w, l1_router_b, l1_exp_gate, l1_exp_up, l1_exp_down, final_norm):
    batch, seq, _ = x_prompt.shape
    dec_batch, dec_seq, _ = x_sample.shape
    past_len = cache_l1_ckv.shape[1]

    cond8 = jnp.concatenate([c_ctx[None, :], c, jnp.zeros((V7X_SUBLANES - 1 - dec_batch, D), F32)], axis=0)
    wcat = jnp.concatenate([l0_lru_wa[0], l0_lru_wi[0], l0_lru_wa[1], l0_lru_wi[1]], axis=-1)
    hid = l1_hy_f_w2.shape[0]
    filter_pack = jnp.concatenate([_pad_to(l1_hy_f_w1, (V7X_LANES, hid)), l1_hy_f_b1[None, :], l1_hy_f_b2[None, :],
                                   jnp.zeros((V7X_SUBLANES - 2, hid), F32)], axis=0)
    short_w = _regroup_chunks(l1_hy_short_w, HY_CB)
    short_b = _regroup_chunks(l1_hy_short_b.reshape(1, -1), HY_CB)
    hy_bias = l1_hy_bias.reshape(1, HY_W)
    router_pack = _pad_to(jnp.concatenate([l1_router_w, l1_router_b[None, :]], axis=0),
                          (D + V7X_SUBLANES, V7X_LANES))

    mod0, mod1 = adaln_tables(cond8, ((l0_w_mod, l0_b_mod), (l1_w_mod, l1_b_mod)))

    kv_ctx = kv_up(cache_l1_ckv.reshape(dec_batch * past_len, KV_RANK), l1_w_kv_up)
    kr_ctx = cache_l1_krope.reshape(dec_batch * past_len, ROPE)

    conds = ((0, batch * seq), (1, dec_seq))
    seq_lens = (seq, dec_seq)
    xs = (x_prompt.reshape(batch * seq, D), x_sample.reshape(dec_batch * dec_seq, D))
    h0s = (jnp.zeros((batch, 2, LRU_W), F32), state_l0_lru)

    us = in0_proj(xs, l0_norm1, mod0, conds, l0_w_in)
    parts, lru_states = [], []
    for u, seq_len, h0 in zip(us, seq_lens, h0s):
        ya = conv_a(u, seq_len, l0_conv_a)
        yb, lru_state = rglru(u, seq_len, l0_lru_conv_w, l0_lru_conv_b, wcat, l0_lru_ba, l0_lru_bi,
                              l0_lru_lambda, h0)
        parts.append([(ya, 0), (yb, 0), (yb, 1)])
        lru_states.append(lru_state)
    xs = mix_ffn(parts, l0_w_out, xs, l0_norm2, mod0, conds, l0_ffn_gate, l0_ffn_up, l0_ffn_down)
    new_lru = lru_states[0]

    def layer1(x, seq_len, cond, latent):
        qnope, qpe, ckv, kr, kv, uh = in1_proj(x, l1_norm1, mod1, cond, l1_w_in, l1_q_norm, l1_kv_norm,
                                               l1_w_q_up, l1_w_kv_up)
        if latent:
            yc = attn_lat(qnope, qpe, kv_ctx, kr_ctx, kv, kr, seq_len, past_len)
        else:
            yc = attn_ctx(qnope, qpe, kv, kr, seq_len)
        cs = dft_tables(seq_len)
        k_r, k_s, k_ny = hy_filter(cs, filter_pack, l1_hy_f_w2, l1_hy_f_w3)
        yd = hyena(uh, seq_len, short_w, short_b, cs, k_r, k_s, k_ny, hy_bias)
        routed = mix_route([(yc, 0), (yc, 1), (yd, 0)], l1_w_out, x, l1_norm2, mod1, cond, router_pack)
        return routed, ckv, kr

    r_p, new_ckv, new_kr = layer1(xs[0], seq, conds[0], latent=False)
    r_s, _, _ = layer1(xs[1], dec_seq, conds[1], latent=True)

    routed = (r_p, r_s)
    pos, pos_all, pads, tile_expert, stages, tile_valid, n_rows = moe_plan([r[3] for r in routed],
                                                                          [r[4] for r in routed])
    hs = moe_dispatch(pos_all, pads, n_rows, r_p[1], r_s[1])
    y_rows = moe_experts(tile_expert, stages, tile_valid, hs, l1_exp_gate, l1_exp_up, l1_exp_down)
    y_p, y_s = [moe_combine(p, r[0], r[2], mod1, cond, final_norm, y_rows)
                for p, r, cond in zip(pos, routed, conds)]
    return (y_p.reshape(batch, seq, D), y_s.reshape(dec_batch, dec_seq, D), new_lru,
            new_ckv.reshape(batch, seq, KV_RANK), new_kr.reshape(batch, seq, ROPE))
```

```python
import functools
import math

import jax
import jax.numpy as jnp
from jax import lax
from jax.experimental import pallas as pl
from jax.experimental.pallas import tpu as pltpu

F32 = jnp.float32
BF16 = jnp.bfloat16

D = 1024
GRID_W = 64
EPS = 1e-6
CONV_W = 512
LRU_W = 1024
LRU_BW = 128
LRU_C = 8.0
MLA_HEADS = 8
Q_RANK = 384
KV_RANK = 256
NOPE = 128
ROPE = 64
VDIM = 128
QK_DIM = NOPE + ROPE
ROPE_THETA = 10000.0
HY_W = 512
HY_BANDS = 16
HY_TARGET = 1e-2
HY_FAST_DECAY = 0.3
HY_SLOW_DECAY = 1.5
D_FF = 2816
N_EXPERTS = 8
D_FF_EXPERT = 1408
IN0 = 3 * CONV_W + 2 * LRU_W
IN1 = Q_RANK + KV_RANK + ROPE + 3 * HY_W

V7X_LANES = 128
V7X_SUBLANES = 8
V7X_VMEM_LIMIT_BYTES = 56 * 1024 * 1024
V7X_VMEM_LIMIT_LARGE_BYTES = 60 * 1024 * 1024

TM = 512
TN_IN0 = 512
TK_IN0 = 256
TF_FFN = 256
MIX_SLAB = 512
MOE_CHUNK = 256
TM_ROUTE = 512
TM_EXPERT = 512
TM_COMBINE = 512
LRU_CB = 512
HY_CB = 512
TQ = 256
ATTN_CTX_SEQS = 4
CONV_A_ROWS = 1024
LRU_ROWS = 1024
HY_ROWS = 1024
TM_IN1 = 512


def _cparams(sem, vmem_limit_bytes=V7X_VMEM_LIMIT_BYTES):
    return pltpu.CompilerParams(dimension_semantics=sem, vmem_limit_bytes=vmem_limit_bytes)


def _sigmoid(x):
    return 0.5 * jnp.tanh(0.5 * x) + 0.5


def _silu(x):
    return x * _sigmoid(x)


def _norm_mod(x, g, shift, scale):
    ms = jnp.mean(x * x, axis=-1, keepdims=True)
    y = x * lax.rsqrt(ms + EPS) * g
    return y * (1.0 + scale) + shift


def _mod_spec(comp, cond, tm, width, col_fn, tile_fn=lambda *ids: ids[0]):
    row0, seg = cond
    assert seg % tm == 0
    return pl.BlockSpec((None, 1, width),
                        lambda *ids: (comp * 3 + row0 + (tile_fn(*ids) * tm) // seg, 0, col_fn(*ids)))


def _dot3(a, b):
    a_hi = a.astype(BF16)
    a_lo = (a - a_hi.astype(F32)).astype(BF16)
    b_hi = b.astype(BF16)
    b_lo = (b - b_hi.astype(F32)).astype(BF16)
    n = a.shape[0]
    y = jnp.dot(jnp.concatenate([a_hi, a_lo], axis=0), b_hi, preferred_element_type=F32)
    return y[:n] + y[n:] + jnp.dot(a_hi, b_lo, preferred_element_type=F32)


def _adaln_kernel(c_ref, w0_ref, b0_ref, w1_ref, b1_ref, o_ref):
    a = _silu(c_ref[...])
    for layer, (w_ref, b_ref) in enumerate(((w0_ref, b0_ref), (w1_ref, b1_ref))):
        @pl.when(pl.program_id(0) == layer)
        def _():
            o_ref[...] = _dot3(a, w_ref[...]) + b_ref[...]


def adaln_tables(cond8, mods):
    tn = 1536
    nj = 6 * D // tn
    (w0, b0), (w1, b1) = mods
    at0 = lambda l, j: (0, jnp.where(l == 0, j, nj - 1))
    at1 = lambda l, j: (0, jnp.where(l == 1, j, 0))
    m = pl.pallas_call(
        _adaln_kernel,
        out_shape=jax.ShapeDtypeStruct((2, V7X_SUBLANES, 6 * D), F32),
        grid=(2, nj),
        in_specs=[pl.BlockSpec((V7X_SUBLANES, D), lambda l, j: (0, 0)),
                  pl.BlockSpec((D, tn), at0), pl.BlockSpec((1, tn), at0),
                  pl.BlockSpec((D, tn), at1), pl.BlockSpec((1, tn), at1)],
        out_specs=pl.BlockSpec((None, V7X_SUBLANES, tn), lambda l, j: (l, 0, j)),
        compiler_params=_cparams(("arbitrary", "arbitrary")),
        name="adaln",
    )(cond8, w0, b0.reshape(1, 6 * D), w1, b1.reshape(1, 6 * D))
    return [m[l, :3].reshape(3, 6, D).transpose(1, 0, 2).reshape(18, 1, D) for l in range(2)]


def _tile_of(n_load):
    return lambda s: jnp.maximum(s - n_load, 0)


def _block_of(n_load):
    return lambda s: jnp.minimum(s, n_load - 1)


class _TwoSets:
    def __init__(self, n_load, tm, tokens, conds):
        self.n_load, self.tm, self.conds = n_load, tm, conds
        self.n_a, self.n_b = tokens[0] // tm, tokens[1] // tm
        self.steps = n_load + self.n_a + self.n_b

    def tile(self, s):
        return jnp.maximum(s - self.n_load, 0)

    def in_first(self, s):
        return s - self.n_load < self.n_a

    def idx_a(self, s):
        return jnp.minimum(self.tile(s), self.n_a - 1)

    def idx_b(self, s):
        return jnp.clip(self.tile(s) - self.n_a, 0, self.n_b - 1)

    def rows(self, width):
        return (pl.BlockSpec((self.tm, width), lambda s: (self.idx_a(s), 0)),
                pl.BlockSpec((self.tm, width), lambda s: (self.idx_b(s), 0)))

    def cols(self, width, col):
        return (pl.BlockSpec((self.tm, width), lambda s: (self.idx_a(s), col)),
                pl.BlockSpec((self.tm, width), lambda s: (self.idx_b(s), col)))

    def mod_spec(self, comp):
        (row_a, seg_a), (row_b, seg_b) = self.conds
        assert seg_a % self.tm == 0 and seg_b % self.tm == 0

        def row(s):
            return jnp.where(self.in_first(s), row_a + (self.idx_a(s) * self.tm) // seg_a,
                             row_b + (self.idx_b(s) * self.tm) // seg_b)

        return pl.BlockSpec((None, 1, D), lambda s: (comp * 3 + row(s), 0, 0))


def _in0_kernel(xa_ref, xb_ref, g_ref, sh_ref, sc_ref, w_ref, oa_ref, ob_ref, w_sc, *, n_a, tn):
    s = pl.program_id(0)
    tk = w_ref.shape[0]
    n_load = w_sc.shape[0] // tk

    @pl.when(s < n_load)
    def _():
        w_sc[pl.ds(pl.multiple_of(s * tk, tk), tk), :] = w_ref[...].astype(BF16)

    def tile(x_ref, o_ref):
        h = _norm_mod(x_ref[...], g_ref[...], sh_ref[...], sc_ref[...]).astype(BF16)
        for j in range(w_sc.shape[1] // tn):
            o_ref[:, j * tn:(j + 1) * tn] = jnp.dot(h, w_sc[:, j * tn:(j + 1) * tn],
                                                    preferred_element_type=F32).astype(BF16)

    @pl.when(jnp.logical_and(s >= n_load, s - n_load < n_a))
    def _():
        tile(xa_ref, oa_ref)

    @pl.when(s - n_load >= n_a)
    def _():
        tile(xb_ref, ob_ref)


def in0_proj(xs, g, modtab, conds, w_in):
    n = w_in.shape[1]
    n_load = D // TK_IN0
    ts = _TwoSets(n_load, TM, [x.shape[0] for x in xs], conds)
    blk = _block_of(n_load)
    return pl.pallas_call(
        functools.partial(_in0_kernel, n_a=ts.n_a, tn=TN_IN0),
        out_shape=tuple(jax.ShapeDtypeStruct((x.shape[0], n), BF16) for x in xs),
        grid=(ts.steps,),
        in_specs=[*ts.rows(D),
                  pl.BlockSpec((1, D), lambda s: (0, 0)),
                  ts.mod_spec(0), ts.mod_spec(1),
                  pl.BlockSpec((TK_IN0, n), lambda s: (blk(s), 0))],
        out_specs=ts.rows(n),
        scratch_shapes=[pltpu.VMEM((D, n), BF16)],
        compiler_params=_cparams(("arbitrary",)),
        name="in0_proj",
    )(*xs, g.reshape(1, D), modtab, modtab, w_in)


def _shift_rows(v, d, t, seq_len=None):
    n = v.shape[0]
    seq_len = n if seq_len is None else seq_len
    if d > 0:
        return jnp.where(t < d, 0.0, pltpu.roll(v, d, 0))
    return jnp.where(t >= seq_len + d, 0.0, pltpu.roll(v, n + d, 0))


def _conv_a_kernel(b_ref, c_ref, x_ref, w_ref, o_ref, *, seq_len):
    v = c_ref[...].astype(F32) * x_ref[...].astype(F32)
    t = lax.broadcasted_iota(jnp.int32, v.shape, 0) & (seq_len - 1)
    w = w_ref[...]
    y = w[0:1] * _shift_rows(v, 1, t, seq_len) + w[1:2] * v + w[2:3] * _shift_rows(v, -1, t, seq_len)
    o_ref[...] = (b_ref[...].astype(F32) * y).astype(o_ref.dtype)


def conv_a(u, seq_len, conv_w):
    tokens = u.shape[0]
    rows = max(seq_len, CONV_A_ROWS)
    assert seq_len & (seq_len - 1) == 0 and rows % seq_len == 0
    return pl.pallas_call(
        functools.partial(_conv_a_kernel, seq_len=seq_len),
        out_shape=jax.ShapeDtypeStruct((tokens, CONV_W), BF16),
        grid=(tokens // rows,),
        in_specs=[pl.BlockSpec((rows, CONV_W), lambda s: (s, 0)),
                  pl.BlockSpec((rows, CONV_W), lambda s: (s, 1)),
                  pl.BlockSpec((rows, CONV_W), lambda s: (s, 2)),
                  pl.BlockSpec((3, CONV_W), lambda s: (0, 0))],
        out_specs=pl.BlockSpec((rows, CONV_W), lambda s: (s, 0)),
        compiler_params=_cparams(("parallel",)),
        name="conv_a",
    )(u, u, u, conv_w)


def _group_scan(a_sc, b_sc, k, reverse):
    planes = a_sc.shape[1] // V7X_SUBLANES
    order = range(V7X_SUBLANES - 1, -1, -1) if reverse else range(V7X_SUBLANES)
    a_acc = b_acc = None
    for r in order:
        plane = (k, pl.ds(r, planes, stride=V7X_SUBLANES), slice(None))
        a_r, b_r = a_sc[plane], b_sc[plane]
        if a_acc is None:
            a_acc, b_acc = a_r, b_r
        else:
            b_acc = a_r * b_acc + b_r
            a_acc = a_r * a_acc
            a_sc[plane] = a_acc
            b_sc[plane] = b_acc


def _rglru_kernel(gate_ref, xb_ref, cw_ref, cb_ref, wcat_ref, ba_ref, bi_ref, lam_ref, h0_ref,
                  y_ref, st_ref, af_sc, bf_sc, ab_sc, bb_sc, hf_sc, hb_sc, *, seq_len):
    n, cb = xb_ref.shape
    n_seq = n // seq_len
    n_slab = cb // LRU_BW
    xb = xb_ref[...].astype(F32)
    t = lax.broadcasted_iota(jnp.int32, xb.shape, 0) & (seq_len - 1)
    cw = cw_ref[...]
    sh = lambda d: _shift_rows(xb, d, t, seq_len)
    xc = cb_ref[...] + cw[0:1] * sh(2) + cw[1:2] * sh(1) + cw[2:3] * xb + cw[3:4] * sh(-1)
    xcb = xc.astype(BF16)

    for k in range(n_slab):
        cols = slice(k * LRU_BW, (k + 1) * LRU_BW)
        gk = jnp.dot(xcb[:, cols], wcat_ref[k].astype(BF16), preferred_element_type=F32)
        for d, (a_sc, b_sc) in enumerate(((af_sc, bf_sc), (ab_sc, bb_sc))):
            ga = gk[:, (2 * d) * LRU_BW:(2 * d + 1) * LRU_BW]
            gi = gk[:, (2 * d + 1) * LRU_BW:(2 * d + 2) * LRU_BW]
            r = _sigmoid(ga + ba_ref[d:d + 1, cols])
            i = _sigmoid(gi + bi_ref[d:d + 1, cols])
            log_a = (-LRU_C * jax.nn.softplus(-lam_ref[d:d + 1, cols])) * r
            a = jnp.exp(log_a)
            m = 1.0 - a * a
            mult = m * lax.rsqrt(jnp.maximum(m, 1e-30))
            a_sc[k] = a
            b_sc[k] = mult * (i * xc[:, cols])
            _group_scan(a_sc, b_sc, k, reverse=(d == 1))

    ng = seq_len // V7X_SUBLANES
    bcast = lambda row: jnp.broadcast_to(row, (V7X_SUBLANES, LRU_BW))
    chains = [(q, k) for q in range(n_seq) for k in range(n_slab)]
    init = tuple((bcast(h0_ref[q, 0:1, k * LRU_BW:(k + 1) * LRU_BW]),
                  bcast(h0_ref[q, 1:2, k * LRU_BW:(k + 1) * LRU_BW])) for q, k in chains)

    def step(j, carry):
        out = []
        for (q, k), (hf_in, hb_in) in zip(chains, carry):
            rf = pl.ds(pl.multiple_of(q * seq_len + j * V7X_SUBLANES, V7X_SUBLANES), V7X_SUBLANES)
            rb = pl.ds(pl.multiple_of(q * seq_len + (ng - 1 - j) * V7X_SUBLANES, V7X_SUBLANES), V7X_SUBLANES)
            hf = af_sc[k, rf, :] * hf_in + bf_sc[k, rf, :]
            hb = ab_sc[k, rb, :] * hb_in + bb_sc[k, rb, :]
            hf_sc[k, rf, :] = hf
            hb_sc[k, rb, :] = hb
            out.append((bcast(hf[V7X_SUBLANES - 1:V7X_SUBLANES]), bcast(hb[0:1])))
        return tuple(out)

    final = lax.fori_loop(0, ng, step, init)
    for (q, k), (hf_last, hb_first) in zip(chains, final):
        st_ref[q, 0:1, k * LRU_BW:(k + 1) * LRU_BW] = hf_last[0:1]
        st_ref[q, 1:2, k * LRU_BW:(k + 1) * LRU_BW] = hb_first[0:1]

    gt = gate_ref[...].astype(F32)
    gelu = 0.5 * gt * (1.0 + jnp.tanh(math.sqrt(2.0 / math.pi) * (gt + 0.044715 * (gt * gt * gt))))
    h = jnp.concatenate([hf_sc[k] + hb_sc[k] for k in range(n_slab)], axis=1)
    y_ref[...] = (h * gelu).astype(y_ref.dtype)


def rglru(u, seq_len, conv_w, conv_b, wcat, ba, bi, lam, h0):
    tokens = u.shape[0]
    nseq = tokens // seq_len
    cb = LRU_CB
    rows = max(seq_len, LRU_ROWS)
    assert seq_len & (seq_len - 1) == 0 and rows % seq_len == 0
    per_blk = rows // seq_len
    gate_blk0 = 3 * CONV_W // cb
    xb_blk0 = (3 * CONV_W + LRU_W) // cb
    seq_scr = lambda: pltpu.VMEM((cb // LRU_BW, rows, LRU_BW), F32)
    return pl.pallas_call(
        functools.partial(_rglru_kernel, seq_len=seq_len),
        out_shape=(jax.ShapeDtypeStruct((tokens, LRU_W), BF16), jax.ShapeDtypeStruct((nseq, 2, LRU_W), F32)),
        grid=(tokens // rows, LRU_W // cb),
        in_specs=[pl.BlockSpec((rows, cb), lambda s, c: (s, gate_blk0 + c)),
                  pl.BlockSpec((rows, cb), lambda s, c: (s, xb_blk0 + c)),
                  pl.BlockSpec((4, cb), lambda s, c: (0, c)),
                  pl.BlockSpec((1, cb), lambda s, c: (0, c)),
                  pl.BlockSpec((cb // LRU_BW, LRU_BW, 4 * LRU_BW), lambda s, c: (c, 0, 0)),
                  pl.BlockSpec((2, cb), lambda s, c: (0, c)),
                  pl.BlockSpec((2, cb), lambda s, c: (0, c)),
                  pl.BlockSpec((2, cb), lambda s, c: (0, c)),
                  pl.BlockSpec((per_blk, 2, cb), lambda s, c: (s, 0, c))],
        out_specs=(pl.BlockSpec((rows, cb), lambda s, c: (s, c)),
                   pl.BlockSpec((per_blk, 2, cb), lambda s, c: (s, 0, c))),
        scratch_shapes=[seq_scr() for _ in range(6)],
        compiler_params=_cparams(("parallel", "parallel")),
        name="rglru",
    )(u, u, conv_w, conv_b.reshape(1, LRU_W), wcat, ba, bi, lam, h0)


def _mix_ffn_kernel(p0a_ref, p0b_ref, p1a_ref, p1b_ref, p2a_ref, p2b_ref, wo_ref, xa_ref, xb_ref,
                    g1_ref, g_ref, sh_ref, sc_ref, g2_ref, wg_ref, wu_ref, wd_ref, oa_ref, ob_ref,
                    wo_sc, wg_sc, wu_sc, wd_sc, *, n_a):
    s = pl.program_id(0)
    n_load = wg_sc.shape[0]
    n_out = wo_sc.shape[0]

    @pl.when(s < n_out)
    def _():
        wo_sc[s] = wo_ref[...].astype(BF16)

    @pl.when(s < n_load)
    def _():
        wg_sc[s] = wg_ref[...].astype(BF16)
        wu_sc[s] = wu_ref[...].astype(BF16)
        wd_sc[s] = wd_ref[...].astype(BF16)

    def tile(p0_ref, p1_ref, p2_ref, x_ref, o_ref):
        m = jnp.dot(p0_ref[...], wo_sc[0], preferred_element_type=F32)
        m += jnp.dot(p1_ref[...], wo_sc[1], preferred_element_type=F32)
        m += jnp.dot(p2_ref[...], wo_sc[2], preferred_element_type=F32)
        x = x_ref[...] + g1_ref[...] * m
        h = _norm_mod(x, g_ref[...], sh_ref[...], sc_ref[...]).astype(BF16)
        y = None
        for f in range(n_load):
            hg = jnp.dot(h, wg_sc[f], preferred_element_type=F32)
            hu = jnp.dot(h, wu_sc[f], preferred_element_type=F32)
            act = (_silu(hg) * hu).astype(BF16)
            yf = jnp.dot(act, wd_sc[f], preferred_element_type=F32)
            y = yf if y is None else y + yf
        o_ref[...] = x + g2_ref[...] * y

    @pl.when(jnp.logical_and(s >= n_load, s - n_load < n_a))
    def _():
        tile(p0a_ref, p1a_ref, p2a_ref, xa_ref, oa_ref)

    @pl.when(s - n_load >= n_a)
    def _():
        tile(p0b_ref, p1b_ref, p2b_ref, xb_ref, ob_ref)


def mix_ffn(parts, w_out, xs, g, modtab, conds, w_gate, w_up, w_down):
    tf = TF_FFN
    kb = MIX_SLAB
    n_load = D_FF // tf
    n_out = len(parts[0])
    assert n_out <= n_load
    ts = _TwoSets(n_load, TM, [x.shape[0] for x in xs], conds)
    blk = _block_of(n_load)
    oblk = _block_of(n_out)
    lhs_specs, lhs_args = [], []
    for (arr_a, col_a), (arr_b, col_b) in zip(*parts):
        assert col_a == col_b
        lhs_specs += ts.cols(kb, col_a)
        lhs_args += [arr_a, arr_b]
    return pl.pallas_call(
        functools.partial(_mix_ffn_kernel, n_a=ts.n_a),
        out_shape=tuple(jax.ShapeDtypeStruct(x.shape, F32) for x in xs),
        grid=(ts.steps,),
        in_specs=lhs_specs + [
            pl.BlockSpec((kb, D), lambda s: (oblk(s), 0)),
            *ts.rows(D),
            ts.mod_spec(2),
            pl.BlockSpec((1, D), lambda s: (0, 0)),
            ts.mod_spec(3), ts.mod_spec(4), ts.mod_spec(5),
            pl.BlockSpec((D, tf), lambda s: (0, blk(s))),
            pl.BlockSpec((D, tf), lambda s: (0, blk(s))),
            pl.BlockSpec((tf, D), lambda s: (blk(s), 0))],
        out_specs=ts.rows(D),
        scratch_shapes=[pltpu.VMEM((n_out, kb, D), BF16),
                        pltpu.VMEM((n_load, D, tf), BF16), pltpu.VMEM((n_load, D, tf), BF16),
                        pltpu.VMEM((n_load, tf, D), BF16)],
        compiler_params=_cparams(("arbitrary",), V7X_VMEM_LIMIT_LARGE_BYTES),
        name="mix_ffn",
    )(*lhs_args, w_out, *xs, modtab, g.reshape(1, D), modtab, modtab, modtab, w_gate, w_up, w_down)


def _rms(x, g):
    return x * lax.rsqrt(jnp.mean(x * x, axis=-1, keepdims=True) + EPS) * g


def _in1_kernel(x_ref, g_ref, sh_ref, sc_ref, w_ref, qn_ref, kvn_ref, wq_ref, wkv_ref,
                qnope_ref, qpe_ref, ckv_ref, kr_ref, kv_ref, uh_ref, w_sc, wq_sc, wkv_sc):
    @pl.when(pl.program_id(0) == 0)
    def _():
        w_sc[...] = w_ref[...].astype(BF16)
        for h in range(MLA_HEADS):
            c0 = h * QK_DIM
            wq_sc[:, h * NOPE:(h + 1) * NOPE] = wq_ref[:, c0:c0 + NOPE].astype(BF16)
            r0 = MLA_HEADS * NOPE + h * ROPE
            wq_sc[:, r0:r0 + ROPE] = wq_ref[:, c0 + NOPE:c0 + QK_DIM].astype(BF16)
        wkv_sc[...] = wkv_ref[...].astype(BF16)

    h = _norm_mod(x_ref[...], g_ref[...], sh_ref[...], sc_ref[...]).astype(BF16)
    u = lax.dot_general(h, w_sc[...], (((1,), (1,)), ((), ())), preferred_element_type=F32)
    o1, o2, o3 = Q_RANK, Q_RANK + KV_RANK, Q_RANK + KV_RANK + ROPE
    cq = _rms(u[:, :o1], qn_ref[...])
    q = jnp.dot(cq.astype(BF16), wq_sc[...], preferred_element_type=F32) * _SCALE
    qnope_ref[...] = q[:, :MLA_HEADS * NOPE].astype(qnope_ref.dtype)
    qpe_ref[...] = q[:, MLA_HEADS * NOPE:]
    ckv = _rms(u[:, o1:o2], kvn_ref[...])
    ckv_ref[...] = ckv
    kv_ref[...] = jnp.dot(ckv.astype(BF16), wkv_sc[...], preferred_element_type=F32).astype(kv_ref.dtype)
    kr_ref[...] = u[:, o2:o3]
    uh_ref[...] = u[:, o3:]


def in1_proj(x, g, modtab, cond, w_in, q_norm, kv_norm, w_q_up, w_kv_up):
    tokens = x.shape[0]
    tm = TM_IN1
    nkv = MLA_HEADS * (NOPE + VDIM)
    const = lambda i: (0, 0)
    zero = lambda i: 0
    once = pl.Buffered(1)
    outs = (jax.ShapeDtypeStruct((tokens, MLA_HEADS * NOPE), BF16),
            jax.ShapeDtypeStruct((tokens, MLA_HEADS * ROPE), F32),
            jax.ShapeDtypeStruct((tokens, KV_RANK), F32),
            jax.ShapeDtypeStruct((tokens, ROPE), F32),
            jax.ShapeDtypeStruct((tokens, nkv), BF16),
            jax.ShapeDtypeStruct((tokens, 3 * HY_W), F32))
    row = lambda w: pl.BlockSpec((tm, w), lambda i: (i, 0))
    return pl.pallas_call(
        _in1_kernel,
        out_shape=outs,
        grid=(tokens // tm,),
        in_specs=[row(D),
                  pl.BlockSpec((1, D), const),
                  _mod_spec(0, cond, tm, D, zero),
                  _mod_spec(1, cond, tm, D, zero),
                  pl.BlockSpec((IN1, D), const, pipeline_mode=once),
                  pl.BlockSpec((1, Q_RANK), const),
                  pl.BlockSpec((1, KV_RANK), const),
                  pl.BlockSpec((Q_RANK, MLA_HEADS * QK_DIM), const, pipeline_mode=once),
                  pl.BlockSpec((KV_RANK, nkv), const, pipeline_mode=once)],
        out_specs=tuple(row(o.shape[1]) for o in outs),
        scratch_shapes=[pltpu.VMEM((IN1, D), BF16), pltpu.VMEM((Q_RANK, MLA_HEADS * QK_DIM), BF16),
                        pltpu.VMEM((KV_RANK, nkv), BF16)],
        compiler_params=_cparams(("arbitrary",)),
        name="in1_proj",
    )(x, g.reshape(1, D), modtab, modtab, w_in.T, q_norm.reshape(1, Q_RANK), kv_norm.reshape(1, KV_RANK),
      w_q_up, w_kv_up)


def _mm_kernel(a_ref, w_ref, o_ref):
    o_ref[...] = jnp.dot(a_ref[...].astype(BF16), w_ref[...].astype(BF16),
                         preferred_element_type=F32).astype(o_ref.dtype)


def kv_up(ckv, w_kv_up):
    rows = ckv.shape[0]
    n = w_kv_up.shape[1]
    return pl.pallas_call(
        _mm_kernel,
        out_shape=jax.ShapeDtypeStruct((rows, n), BF16),
        grid=(rows // TM,),
        in_specs=[pl.BlockSpec((TM, KV_RANK), lambda i: (i, 0)), pl.BlockSpec((KV_RANK, n), lambda i: (0, 0))],
        out_specs=pl.BlockSpec((TM, n), lambda i: (i, 0)),
        compiler_params=_cparams(("parallel",)),
        name="kv_up",
    )(ckv, w_kv_up)


_NT = (((1,), (1,)), ((), ()))
_SCALE = 1.0 / math.sqrt(QK_DIM)


def _fill_rope_tables(cos_ref, sin_ref):
    n, width = cos_ref.shape
    n_grid_rows = n // GRID_W
    n_freq = ROPE // 4

    def trig(count):
        lane = lax.broadcasted_iota(jnp.int32, (count, width), 1)
        j = lane & (ROPE // 2 - 1)
        inv = jnp.exp((j & (n_freq - 1)).astype(F32) * (-math.log(ROPE_THETA) / n_freq))
        ang = lax.broadcasted_iota(jnp.int32, (count, width), 0).astype(F32) * inv
        return jnp.cos(ang), jnp.sin(ang), j < n_freq

    cos_c, sin_c, by_row = trig(GRID_W)
    cos_r, sin_r, _ = trig(n_grid_rows)
    for r in range(n_grid_rows):
        rows = slice(r * GRID_W, (r + 1) * GRID_W)
        cos_ref[rows, :] = jnp.where(by_row, jnp.broadcast_to(cos_r[r:r + 1], cos_c.shape), cos_c)
        sin_ref[rows, :] = jnp.where(by_row, jnp.broadcast_to(sin_r[r:r + 1], sin_c.shape), sin_c)


def _rope(x, cos, sin):
    width = x.shape[1]
    lane = lax.broadcasted_iota(jnp.int32, x.shape, 1)
    first_half = (lane & (ROPE - 1)) < ROPE // 2
    xr = jnp.where(first_half, -pltpu.roll(x, width - ROPE // 2, 1), pltpu.roll(x, ROPE // 2, 1))
    return x * cos + xr * sin


def _ones_column(n):
    lane = lax.broadcasted_iota(jnp.int32, (n, VDIM), 1)
    return jnp.where(lane == 0, 1.0, 0.0).astype(BF16)


def _head_attention(qcat, kcat, vaug):
    s = lax.dot_general(qcat, kcat, _NT, preferred_element_type=F32)
    p = jnp.exp(s - jnp.max(s, axis=-1, keepdims=True)).astype(BF16)
    oa = jnp.dot(p, vaug, preferred_element_type=F32)
    return oa[:, :VDIM] / oa[:, VDIM:VDIM + 1]


def _attn_ctx_kernel(qn_ref, qpe_ref, kv_ref, kr_ref, o_ref, *, seq_len):
    n = qn_ref.shape[0]
    n_seq = n // seq_len
    ones = _ones_column(n)
    kpe = kr_ref[...].astype(BF16)
    per_seq = lambda a: a.reshape(n_seq, seq_len, a.shape[-1])
    for h in range(MLA_HEADS):
        c0 = h * (NOPE + VDIM)
        qcat = per_seq(jnp.concatenate([qn_ref[:, h * NOPE:(h + 1) * NOPE],
                                        qpe_ref[:, h * ROPE:(h + 1) * ROPE].astype(BF16)], axis=1))
        kcat = per_seq(jnp.concatenate([kv_ref[:, c0:c0 + NOPE], kpe], axis=1))
        vaug = per_seq(jnp.concatenate([kv_ref[:, c0 + NOPE:c0 + NOPE + VDIM], ones], axis=1))
        s = jnp.einsum("bqd,bkd->bqk", qcat, kcat, preferred_element_type=F32)
        p = jnp.exp(s - jnp.max(s, axis=-1, keepdims=True)).astype(BF16)
        oa = jnp.einsum("bqk,bkd->bqd", p, vaug, preferred_element_type=F32)
        o = oa[:, :, :VDIM] / oa[:, :, VDIM:VDIM + 1]
        o_ref[:, h * VDIM:(h + 1) * VDIM] = o.reshape(n, VDIM).astype(o_ref.dtype)


def attn_ctx(qnope, qpe, kv, kr, seq_len):
    tokens = qnope.shape[0]
    rows = ATTN_CTX_SEQS * seq_len
    blk = lambda w: pl.BlockSpec((rows, w), lambda s: (s, 0))
    return pl.pallas_call(
        functools.partial(_attn_ctx_kernel, seq_len=seq_len),
        out_shape=jax.ShapeDtypeStruct((tokens, MLA_HEADS * VDIM), BF16),
        grid=(tokens // rows,),
        in_specs=[blk(MLA_HEADS * NOPE), blk(MLA_HEADS * ROPE), blk(MLA_HEADS * (NOPE + VDIM)), blk(ROPE)],
        out_specs=blk(MLA_HEADS * VDIM),
        compiler_params=_cparams(("parallel",)),
        name="attn_ctx",
    )(qnope, qpe, kv, kr)


def _attn_lat_kernel(qn_ref, qpe_ref, kvc_ref, krc_ref, kvl_ref, krl_ref, o_ref, kcat_sc, vaug_sc, cos_sc, sin_sc):
    tq = qn_ref.shape[0]
    n_ctx = krc_ref.shape[0]
    n_lat = krl_ref.shape[0]

    @pl.when(pl.program_id(1) == 0)
    def _():
        _fill_rope_tables(cos_sc, sin_sc)
        kr2 = jnp.concatenate([krl_ref[...], krl_ref[...]], axis=1)
        kpe_lat = _rope(kr2, cos_sc[...], sin_sc[...])[:, :ROPE].astype(BF16)
        kpe_ctx = krc_ref[...].astype(BF16)
        ones_c, ones_l = _ones_column(n_ctx), _ones_column(n_lat)
        for h in range(MLA_HEADS):
            c0 = h * (NOPE + VDIM)
            for r0, nr, kv_ref, kpe, ones in ((0, n_ctx, kvc_ref, kpe_ctx, ones_c), (n_ctx, n_lat, kvl_ref, kpe_lat, ones_l)):
                kcat_sc[h, r0:r0 + nr, 0:NOPE] = kv_ref[:, c0:c0 + NOPE]
                kcat_sc[h, r0:r0 + nr, NOPE:QK_DIM] = kpe
                vaug_sc[h, r0:r0 + nr, 0:VDIM] = kv_ref[:, c0 + NOPE:c0 + NOPE + VDIM]
                vaug_sc[h, r0:r0 + nr, VDIM:2 * VDIM] = ones

    q0 = pl.multiple_of(pl.program_id(1) * tq, tq)
    rep = lambda a: jnp.concatenate([a] * (MLA_HEADS // 2), axis=1)
    qp_all = _rope(qpe_ref[...], rep(cos_sc[pl.ds(q0, tq), :]), rep(sin_sc[pl.ds(q0, tq), :])).astype(BF16)
    for h in range(MLA_HEADS):
        qcat = jnp.concatenate([qn_ref[:, h * NOPE:(h + 1) * NOPE], qp_all[:, h * ROPE:(h + 1) * ROPE]], axis=1)
        o_ref[:, h * VDIM:(h + 1) * VDIM] = _head_attention(qcat, kcat_sc[h], vaug_sc[h]).astype(o_ref.dtype)


def attn_lat(qnope, qpe, kv_ctx, kr_ctx, kv_lat, kr_lat, seq_len, ctx_len):
    tokens = qnope.shape[0]
    nq = seq_len // TQ
    qblk = lambda w: pl.BlockSpec((TQ, w), lambda b, i: (b * nq + i, 0))
    seq = lambda n, w: pl.BlockSpec((n, w), lambda b, i: (b, 0))
    nkv = MLA_HEADS * (NOPE + VDIM)
    n_keys = ctx_len + seq_len
    return pl.pallas_call(
        _attn_lat_kernel,
        out_shape=jax.ShapeDtypeStruct((tokens, MLA_HEADS * VDIM), BF16),
        grid=(tokens // seq_len, nq),
        in_specs=[qblk(MLA_HEADS * NOPE), qblk(MLA_HEADS * ROPE), seq(ctx_len, nkv), seq(ctx_len, ROPE),
                  seq(seq_len, nkv), seq(seq_len, ROPE)],
        out_specs=qblk(MLA_HEADS * VDIM),
        scratch_shapes=[pltpu.VMEM((MLA_HEADS, n_keys, QK_DIM), BF16),
                        pltpu.VMEM((MLA_HEADS, n_keys, 2 * VDIM), BF16),
                        pltpu.VMEM((seq_len, 2 * ROPE), F32), pltpu.VMEM((seq_len, 2 * ROPE), F32)],
        compiler_params=_cparams(("parallel", "arbitrary")),
        name="attn_lat",
    )(qnope, qpe, kv_ctx, kr_ctx, kv_lat, kr_lat)


def _dft_kernel(o_ref):
    tr, n = o_ref.shape[1], o_ref.shape[2]
    nb = n // V7X_LANES
    f = pl.program_id(0) * tr + lax.broadcasted_iota(jnp.int32, (tr, V7X_LANES), 0)
    j = lax.broadcasted_iota(jnp.int32, (tr, V7X_LANES), 1)

    def cos_sin(m):
        ang = (m & (2 * n - 1)).astype(F32) * (math.pi / n)
        return jnp.cos(ang), jnp.sin(ang)

    cj, sj = cos_sin(f * j)
    cb, sb = cos_sin(f * (j * V7X_LANES))
    for b in range(nb):
        cbb, sbb = cb[:, b:b + 1], sb[:, b:b + 1]
        cols = slice(b * V7X_LANES, (b + 1) * V7X_LANES)
        o_ref[0, :, cols] = (cbb * cj - sbb * sj).astype(o_ref.dtype)
        o_ref[1, :, cols] = (sbb * cj + cbb * sj).astype(o_ref.dtype)


def dft_tables(n):
    tr = 128
    return pl.pallas_call(
        _dft_kernel,
        out_shape=jax.ShapeDtypeStruct((2, n, n), BF16),
        grid=(n // tr,),
        out_specs=pl.BlockSpec((2, tr, n), lambda i: (0, i, 0)),
        compiler_params=_cparams(("parallel",)),
        name="dft_tables",
    )()


def _split_dot(table, x):
    hi = x.astype(BF16)
    lo = (x - hi.astype(F32)).astype(BF16)
    return (jnp.dot(table, hi, preferred_element_type=F32) + jnp.dot(table, lo, preferred_element_type=F32))


def _hy_filter_kernel(cs_ref, pack_ref, w2_ref, w3_ref, kr_ref, ks_ref, kny_ref):
    n = cs_ref.shape[1]
    row = lax.broadcasted_iota(jnp.int32, (n, V7X_LANES), 0).astype(F32)
    lane = lax.broadcasted_iota(jnp.int32, (n, V7X_LANES), 1)
    t = row * (1.0 / (n - 1))
    w = (2.0 * math.pi) * row / n
    band = jnp.where(lane <= HY_BANDS, lane - 1, lane - 1 - HY_BANDS).astype(F32)
    freq = 1e-4 + band * ((HY_BANDS - 1 - 1e-4) / (HY_BANDS - 1))
    arg = jnp.where(lane <= HY_BANDS, freq * w + 0.5 * math.pi, -(freq * w))
    z = jnp.where(lane == 0, t, jnp.where(lane <= 2 * HY_BANDS, jnp.sin(arg), 0.0))
    hid = jnp.sin(_dot3(z, pack_ref[0:V7X_LANES, :]) + pack_ref[V7X_LANES:V7X_LANES + 1, :])
    hid = jnp.sin(_dot3(hid, w2_ref[...]) + pack_ref[V7X_LANES + 1:V7X_LANES + 2, :])
    hf = _dot3(hid, w3_ref[...])

    rowc = lax.broadcasted_iota(jnp.int32, (n, HY_W), 0)
    chan = lax.broadcasted_iota(jnp.int32, (n, HY_W), 1).astype(F32)
    max_decay = math.log(HY_TARGET) / HY_FAST_DECAY
    min_decay = math.log(HY_TARGET) / HY_SLOW_DECAY
    deltas = min_decay + chan * ((max_decay - min_decay) / (HY_W - 1))
    decay = jnp.exp(-(rowc.astype(F32) * (1.0 / (n - 1))) * jnp.abs(deltas))
    h_fwd = hf[:, :HY_W] * decay
    h_bwd = jnp.where(rowc == 0, 0.0, hf[:, HY_W:] * decay)
    norm = jnp.sum(jnp.abs(h_fwd) + jnp.abs(h_bwd), axis=0, keepdims=True)
    even = (h_fwd + h_bwd) / norm
    odd = (h_fwd - h_bwd) / norm
    cf = jnp.where(rowc == 0, 1.0, 2.0) * (1.0 / (2 * n))
    kr_ref[...] = cf * _split_dot(cs_ref[0], even)
    ks_ref[...] = cf * _split_dot(cs_ref[1], odd)
    sgn = jnp.where((rowc & 1) == 1, -1.0, 1.0)
    kny_ref[...] = jnp.sum(sgn * even, axis=0, keepdims=True) * (1.0 / (2 * n))


def hy_filter(cs, pack, w2, w3):
    n = cs.shape[1]
    full = lambda a: pl.BlockSpec(a.shape, lambda: (0,) * a.ndim)
    args = (cs, pack, w2, w3)
    return pl.pallas_call(
        _hy_filter_kernel,
        out_shape=(jax.ShapeDtypeStruct((n, HY_W), F32), jax.ShapeDtypeStruct((n, HY_W), F32),
                   jax.ShapeDtypeStruct((1, HY_W), F32)),
        in_specs=[full(a) for a in args],
        out_specs=(pl.BlockSpec((n, HY_W), lambda: (0, 0)), pl.BlockSpec((n, HY_W), lambda: (0, 0)),
                   pl.BlockSpec((1, HY_W), lambda: (0, 0))),
        compiler_params=pltpu.CompilerParams(vmem_limit_bytes=V7X_VMEM_LIMIT_BYTES),
        name="hy_filter",
    )(*args)


def _hyena_kernel(u0_ref, u1_ref, u2_ref, sw_ref, sb_ref, cs_ref, kr_ref, ks_ref, kny_ref, bias_ref, o_ref,
                  *, seq_len):
    n, cb = u0_ref.shape
    n_seq = n // seq_len
    t = lax.broadcasted_iota(jnp.int32, (n, cb), 0) & (seq_len - 1)

    def short_conv(u_ref, k):
        u = u_ref[...]
        w = sw_ref[:, k * cb:(k + 1) * cb]
        return (sb_ref[:, k * cb:(k + 1) * cb] + w[0:1] * _shift_rows(u, 1, t, seq_len) + w[1:2] * u
                + w[2:3] * _shift_rows(u, -1, t, seq_len))

    x0 = short_conv(u0_ref, 0)
    z = short_conv(u1_ref, 1) * short_conv(u2_ref, 2)
    wide = lambda a: jnp.concatenate([a[q * seq_len:(q + 1) * seq_len] for q in range(n_seq)], axis=1)
    rep = lambda a: jnp.concatenate([a] * n_seq, axis=1)
    zw = wide(z)
    zb = zw.astype(BF16)
    c, s = cs_ref[0], cs_ref[1]
    ur = jnp.dot(c, zb, preferred_element_type=F32)
    us = jnp.dot(s, zb, preferred_element_type=F32)
    sgn = jnp.where((lax.broadcasted_iota(jnp.int32, zw.shape, 0) & 1) == 1, -1.0, 1.0)
    uny = jnp.sum(sgn * zw, axis=0, keepdims=True)
    kr, ks = rep(kr_ref[...]), rep(ks_ref[...])
    yr = (ur * kr - us * ks).astype(BF16)
    ys = (ur * ks + us * kr).astype(BF16)
    yw = jnp.dot(c, yr, preferred_element_type=F32) + jnp.dot(s, ys, preferred_element_type=F32)
    yw = yw + sgn * (uny * rep(kny_ref[...]))
    y = jnp.concatenate([yw[:, q * cb:(q + 1) * cb] for q in range(n_seq)], axis=0)
    o_ref[...] = (x0 * (y + bias_ref[...] * z)).astype(o_ref.dtype)


def hyena(uh, seq_len, short_w, short_b, cs, kr, ks, kny, bias):
    tokens = uh.shape[0]
    cb = HY_CB
    nc = HY_W // cb
    rows = max(seq_len, HY_ROWS)
    assert seq_len & (seq_len - 1) == 0 and rows % seq_len == 0
    ublk = lambda k: pl.BlockSpec((rows, cb), lambda s, c: (s, k * nc + c))
    chan = lambda r: pl.BlockSpec((r, cb), lambda s, c: (0, c))
    return pl.pallas_call(
        functools.partial(_hyena_kernel, seq_len=seq_len),
        out_shape=jax.ShapeDtypeStruct((tokens, HY_W), BF16),
        grid=(tokens // rows, nc),
        in_specs=[ublk(0), ublk(1), ublk(2),
                  pl.BlockSpec((None, 3, 3 * cb), lambda s, c: (c, 0, 0)),
                  pl.BlockSpec((None, 1, 3 * cb), lambda s, c: (c, 0, 0)),
                  pl.BlockSpec((2, seq_len, seq_len), lambda s, c: (0, 0, 0)),
                  chan(seq_len), chan(seq_len), chan(1), chan(1)],
        out_specs=pl.BlockSpec((rows, cb), lambda s, c: (s, c)),
        compiler_params=_cparams(("parallel", "parallel")),
        name="hyena",
    )(uh, uh, uh, short_w, short_b, cs, kr, ks, kny, bias)


META_E1, META_E2, META_R1, META_R2, META_G1, META_G2 = range(6)


def _route_kernel(p0_ref, p1_ref, p2_ref, wo_ref, x_ref, g1_ref, g_ref, sh_ref, sc_ref, wr_ref,
                  x1_ref, h_ref, meta_ref, meta_t_ref, cnt_ref, run_sc, wo_sc):
    tm = x_ref.shape[0]
    lane = lax.broadcasted_iota(jnp.int32, (tm, V7X_LANES), 1)

    @pl.when(pl.program_id(0) == 0)
    def _():
        run_sc[...] = jnp.zeros_like(run_sc)
        wo_sc[...] = wo_ref[...].astype(BF16)

    kb = p0_ref.shape[1]
    m = jnp.dot(p0_ref[...], wo_sc[0:kb, :], preferred_element_type=F32)
    m += jnp.dot(p1_ref[...], wo_sc[kb:2 * kb, :], preferred_element_type=F32)
    m += jnp.dot(p2_ref[...], wo_sc[2 * kb:3 * kb, :], preferred_element_type=F32)
    x1 = x_ref[...] + g1_ref[...] * m
    x1_ref[...] = x1
    h = _norm_mod(x1, g_ref[...], sh_ref[...], sc_ref[...])
    h_ref[...] = h
    logits = _dot3(h, wr_ref[0:D, :]) + wr_ref[D:D + 1, :]
    lg = jnp.where(lane < N_EXPERTS, logits, -jnp.inf)
    l1 = jnp.max(lg, axis=-1, keepdims=True)
    i1 = jnp.min(jnp.where(lg == l1, lane, V7X_LANES), axis=-1, keepdims=True)
    rest = jnp.where(lane == i1, -jnp.inf, lg)
    l2 = jnp.max(rest, axis=-1, keepdims=True)
    i2 = jnp.min(jnp.where(rest == l2, lane, V7X_LANES), axis=-1, keepdims=True)
    gap = jnp.exp(l2 - l1)
    gate1 = 1.0 / (1.0 + gap)
    gate2 = gap * gate1
    m1 = lane == i1
    m2 = lane == i2
    chosen = jnp.where(m1 | m2, 1.0, 0.0)
    r = lax.broadcasted_iota(jnp.int32, (tm, tm), 0)
    c = lax.broadcasted_iota(jnp.int32, (tm, tm), 1)
    tri = jnp.where(c < r, 1.0, 0.0).astype(BF16)
    before = jnp.dot(tri, chosen.astype(BF16), preferred_element_type=F32) + run_sc[0:1, :]
    rank1 = jnp.sum(jnp.where(m1, before, 0.0), axis=-1, keepdims=True)
    rank2 = jnp.sum(jnp.where(m2, before, 0.0), axis=-1, keepdims=True)
    vals = (i1.astype(F32), i2.astype(F32), rank1, rank2, gate1, gate2)
    meta = jnp.zeros((tm, V7X_LANES), F32)
    for k, v in enumerate(vals):
        meta = jnp.where(lane == k, v, meta)
    meta_ref[...] = meta
    meta_t_ref[...] = meta.T[:V7X_SUBLANES]
    run_sc[...] = run_sc[...] + jnp.sum(chosen, axis=0, keepdims=True)
    cnt_ref[...] = run_sc[...]


def mix_route(parts, w_out, x, g, modtab, cond, router_pack):
    tokens = x.shape[0]
    tm = TM_ROUTE
    kb = MIX_SLAB
    zero = lambda i: 0
    const = lambda i: (0, 0)
    rows = lambda w: pl.BlockSpec((tm, w), lambda i: (i, 0))
    lhs_specs = [pl.BlockSpec((tm, kb), (lambda i, cbk=cbk: (i, cbk))) for _, cbk in parts]
    return pl.pallas_call(
        _route_kernel,
        out_shape=(jax.ShapeDtypeStruct((tokens, D), F32),
                   jax.ShapeDtypeStruct((tokens, D), F32),
                   jax.ShapeDtypeStruct((tokens, V7X_LANES), F32),
                   jax.ShapeDtypeStruct((V7X_SUBLANES, tokens), F32),
                   jax.ShapeDtypeStruct((V7X_SUBLANES, V7X_LANES), F32)),
        grid=(tokens // tm,),
        in_specs=lhs_specs + [
            pl.BlockSpec((len(parts) * kb, D), const, pipeline_mode=pl.Buffered(1)),
            rows(D),
            _mod_spec(2, cond, tm, D, zero),
            pl.BlockSpec((1, D), const),
            _mod_spec(3, cond, tm, D, zero),
            _mod_spec(4, cond, tm, D, zero),
            pl.BlockSpec((D + V7X_SUBLANES, V7X_LANES), const)],
        out_specs=(rows(D), rows(D), rows(V7X_LANES),
                   pl.BlockSpec((V7X_SUBLANES, tm), lambda i: (0, i)),
                   pl.BlockSpec((V7X_SUBLANES, V7X_LANES), const)),
        scratch_shapes=[pltpu.VMEM((V7X_SUBLANES, V7X_LANES), F32), pltpu.VMEM((len(parts) * kb, D), BF16)],
        compiler_params=_cparams(("arbitrary",)),
        name="mix_route",
    )(*[a for a, _ in parts], w_out, x, modtab, g.reshape(1, D), modtab, modtab, router_pack)


def _row_copy(src_ref, src_row, dst_ref, dst_row, sem):
    return pltpu.make_async_copy(src_ref.at[pl.ds(src_row, 1)], dst_ref.at[pl.ds(dst_row, 1)], sem)


_PAD_BULK = (256, 128, 64, 32, 16, 8)


def _zero_fill(hs_ref, zero_sc, sem, pads_ref, n_tail_max, wait):
    tmr = zero_sc.shape[0]

    def copy(rows, dst):
        cp = pltpu.make_async_copy(zero_sc.at[pl.ds(0, rows)], hs_ref.at[pl.ds(dst, rows)], sem)
        cp.wait() if wait else cp.start()

    for e in range(N_EXPERTS):
        start, n = pads_ref[e], pads_ref[N_EXPERTS + e]
        head = jnp.minimum((-start) & (V7X_SUBLANES - 1), n)
        for r in range(V7X_SUBLANES - 1):
            @pl.when(r < head)
            def _():
                copy(1, start + r)
        body = start + head
        rem = n - head
        for k in _PAD_BULK:
            @pl.when((rem & k) != 0)
            def _():
                copy(k, pl.multiple_of(body + (rem & ~(2 * k - 1)), V7X_SUBLANES))
    tail_start, tail_tiles = pads_ref[2 * N_EXPERTS], pads_ref[2 * N_EXPERTS + 1]
    for t in range(n_tail_max):
        @pl.when(t < tail_tiles)
        def _():
            copy(tmr, pl.multiple_of(tail_start + t * tmr, tmr))


def _dispatch_kernel(pos_ref, pads_ref, ha_ref, hb_ref, hs_ref, zero_sc, sem, zsem, *, n_a, n_tail_max):
    tm = ha_ref.shape[0]
    n_tok = pos_ref.shape[0] // 2
    i = pl.program_id(0)
    base = i * tm

    @pl.when(i == 0)
    def _():
        zero_sc[...] = jnp.zeros_like(zero_sc)
        _zero_fill(hs_ref, zero_sc, zsem, pads_ref, n_tail_max, wait=False)

    def scatter(h_ref):
        def issue(g, carry):
            r0 = pl.multiple_of(g * V7X_SUBLANES, V7X_SUBLANES)
            rows = h_ref.at[pl.ds(r0, V7X_SUBLANES)]
            for u in range(V7X_SUBLANES):
                _row_copy(rows, u, hs_ref, pos_ref[base + r0 + u], sem).start(priority=0)
                _row_copy(rows, u, hs_ref, pos_ref[n_tok + base + r0 + u], sem).start(priority=1)
            return carry

        lax.fori_loop(0, tm // V7X_SUBLANES, issue, 0)
        for _ in range(2):
            pltpu.make_async_copy(h_ref, hs_ref.at[pl.ds(0, tm)], sem).wait()

    @pl.when(i < n_a)
    def _():
        scatter(ha_ref)

    @pl.when(i >= n_a)
    def _():
        scatter(hb_ref)

    @pl.when(i == 0)
    def _():
        _zero_fill(hs_ref, zero_sc, zsem, pads_ref, n_tail_max, wait=True)


def moe_dispatch(pos, pads, hs_rows, h_a, h_b):
    tm = TM_ROUTE
    n_a, n_b = h_a.shape[0] // tm, h_b.shape[0] // tm
    n_tail_max = hs_rows // TM_EXPERT - (2 * (h_a.shape[0] + h_b.shape[0])) // TM_EXPERT
    return pl.pallas_call(
        functools.partial(_dispatch_kernel, n_a=n_a, n_tail_max=n_tail_max),
        out_shape=jax.ShapeDtypeStruct((hs_rows, D), F32),
        grid_spec=pltpu.PrefetchScalarGridSpec(
            num_scalar_prefetch=2,
            grid=(n_a + n_b,),
            in_specs=[pl.BlockSpec((tm, D), lambda i, *pf: (jnp.minimum(i, n_a - 1), 0)),
                      pl.BlockSpec((tm, D), lambda i, *pf: (jnp.clip(i - n_a, 0, n_b - 1), 0))],
            out_specs=pl.BlockSpec(memory_space=pl.ANY),
            scratch_shapes=[pltpu.VMEM((TM_EXPERT, D), F32), pltpu.SemaphoreType.DMA(()),
                            pltpu.SemaphoreType.DMA(())]),
        compiler_params=_cparams(("arbitrary",)),
        name="moe_dispatch",
    )(pos, pads, h_a, h_b)


def _experts_kernel(te_ref, sg_ref, su_ref, sd_ref, nv_ref, hs_ref, wg_ref, wu_ref, wd_ref, y_ref,
                    wg_sc, wu_sc, wd_sc):
    del sg_ref, su_ref, sd_ref
    j = pl.program_id(0)
    e = te_ref[j]
    e_prev = te_ref[jnp.maximum(j - 1, 0)]
    n_valid = nv_ref[j]
    half = y_ref.shape[0] // 2

    @pl.when((j == 0) | (e != e_prev))
    def _():
        wg_sc[...] = wg_ref[...].astype(BF16)
        wu_sc[...] = wu_ref[...].astype(BF16)
        wd_sc[...] = wd_ref[...].astype(BF16)

    def swiglu(rows):
        h = hs_ref[rows, :].astype(BF16)
        y = None
        for c0 in range(0, D_FF_EXPERT, MOE_CHUNK):
            c1 = min(c0 + MOE_CHUNK, D_FF_EXPERT)
            hg = jnp.dot(h, wg_sc[:, c0:c1], preferred_element_type=F32)
            hu = jnp.dot(h, wu_sc[:, c0:c1], preferred_element_type=F32)
            act = (_silu(hg) * hu).astype(BF16)
            yc = jnp.dot(act, wd_sc[c0:c1, :], preferred_element_type=F32)
            y = yc if y is None else y + yc
        y_ref[rows, :] = y

    @pl.when(n_valid > half)
    def _():
        swiglu(slice(None))

    @pl.when((n_valid > 0) & (n_valid <= half))
    def _():
        swiglu(slice(0, half))
        y_ref[half:, :] = jnp.zeros((half, D), F32)

    @pl.when(n_valid == 0)
    def _():
        y_ref[...] = jnp.zeros_like(y_ref)


def moe_experts(tile_expert, stages, tile_valid, hs, e_gate, e_up, e_down):
    rows = hs.shape[0]
    tmr = TM_EXPERT
    wspec = lambda shape, k: pl.BlockSpec((None,) + shape, lambda j, *pf: (pf[1 + k][j], 0, 0))
    return pl.pallas_call(
        _experts_kernel,
        out_shape=jax.ShapeDtypeStruct((rows, D), F32),
        grid_spec=pltpu.PrefetchScalarGridSpec(
            num_scalar_prefetch=5,
            grid=(rows // tmr,),
            in_specs=[pl.BlockSpec((tmr, D), lambda j, *pf: (j, 0)),
                      wspec((D, D_FF_EXPERT), 0), wspec((D, D_FF_EXPERT), 1), wspec((D_FF_EXPERT, D), 2)],
            out_specs=pl.BlockSpec((tmr, D), lambda j, *pf: (j, 0)),
            scratch_shapes=[pltpu.VMEM((D, D_FF_EXPERT), BF16), pltpu.VMEM((D, D_FF_EXPERT), BF16),
                            pltpu.VMEM((D_FF_EXPERT, D), BF16)]),
        compiler_params=_cparams(("arbitrary",)),
        name="moe_experts",
    )(tile_expert, *stages, tile_valid, hs, e_gate, e_up, e_down)


def _combine_kernel(pos_ref, x_ref, meta_ref, gt_ref, fg_ref, y_ref, o_ref, b1_sc, b2_sc, sem):
    tm = x_ref.shape[0]
    n_tok = pos_ref.shape[0] // 2
    i = pl.program_id(0)

    def gather(tile, slot):
        base = tile * tm

        def issue(g, carry):
            r0 = pl.multiple_of(g * V7X_SUBLANES, V7X_SUBLANES)
            rows1 = b1_sc.at[slot, pl.ds(r0, V7X_SUBLANES)]
            rows2 = b2_sc.at[slot, pl.ds(r0, V7X_SUBLANES)]
            for u in range(V7X_SUBLANES):
                _row_copy(y_ref, pos_ref[base + r0 + u], rows1, u, sem.at[slot]).start(priority=0)
                _row_copy(y_ref, pos_ref[n_tok + base + r0 + u], rows2, u, sem.at[slot]).start(priority=1)
            return carry

        lax.fori_loop(0, tm // V7X_SUBLANES, issue, 0)

    @pl.when(i == 0)
    def _():
        gather(0, 0)

    @pl.when(i + 1 < pl.num_programs(0))
    def _():
        gather(i + 1, (i + 1) % 2)

    slot = i % 2
    pltpu.make_async_copy(y_ref.at[pl.ds(0, tm)], b1_sc.at[slot], sem.at[slot]).wait()
    pltpu.make_async_copy(y_ref.at[pl.ds(0, tm)], b2_sc.at[slot], sem.at[slot]).wait()

    meta = meta_ref[...]
    lane = lax.broadcasted_iota(jnp.int32, meta.shape, 1)
    g1 = jnp.sum(jnp.where(lane == META_G1, meta, 0.0), axis=-1, keepdims=True)
    g2 = jnp.sum(jnp.where(lane == META_G2, meta, 0.0), axis=-1, keepdims=True)
    x = x_ref[...] + gt_ref[...] * (g1 * b1_sc[slot] + g2 * b2_sc[slot])
    o_ref[...] = _rms(x, fg_ref[...])


def moe_combine(pos, x, meta, modtab, cond, final_g, y):
    tokens = x.shape[0]
    tm = TM_COMBINE
    return pl.pallas_call(
        _combine_kernel,
        out_shape=jax.ShapeDtypeStruct((tokens, D), F32),
        grid_spec=pltpu.PrefetchScalarGridSpec(
            num_scalar_prefetch=1,
            grid=(tokens // tm,),
            in_specs=[pl.BlockSpec((tm, D), lambda i, pos: (i, 0)),
                      pl.BlockSpec((tm, V7X_LANES), lambda i, pos: (i, 0)),
                      _mod_spec(5, cond, tm, D, lambda i, pos: 0),
                      pl.BlockSpec((1, D), lambda i, pos: (0, 0)),
                      pl.BlockSpec(memory_space=pl.ANY)],
            out_specs=pl.BlockSpec((tm, D), lambda i, pos: (i, 0)),
            scratch_shapes=[pltpu.VMEM((2, tm, D), F32), pltpu.VMEM((2, tm, D), F32),
                            pltpu.SemaphoreType.DMA((2,))]),
        compiler_params=_cparams(("arbitrary",)),
        name="moe_combine",
    )(pos, x, meta, modtab, final_g.reshape(1, D), y)


def moe_plan(metas, counts):
    tmr = TM_EXPERT
    cnts = [c[0, :N_EXPERTS].astype(jnp.int32) for c in counts]
    total = functools.reduce(jnp.add, cnts)
    padded = ((total + tmr - 1) // tmr) * tmr
    ends = jnp.cumsum(padded)
    starts = ends - padded
    n_rows = sum(m.shape[1] for m in metas) * 2 + N_EXPERTS * tmr
    n_tiles = n_rows // tmr
    tile_start = jnp.arange(n_tiles, dtype=jnp.int32) * tmr
    tile_expert = jnp.minimum(jnp.sum(tile_start[:, None] >= ends[None, :], axis=1), N_EXPERTS - 1).astype(jnp.int32)
    group_of_tile = jnp.sum(tile_start[:, None] >= ends[None, :], axis=1)
    real_end = jnp.sum(jnp.where(group_of_tile[:, None] == jnp.arange(N_EXPERTS)[None, :],
                                 (starts + total)[None, :], 0), axis=1)
    tile_valid = jnp.clip(real_end - tile_start, 0, tmr).astype(jnp.int32)
    eid = jnp.arange(N_EXPERTS, dtype=jnp.int32)
    later = jnp.where((eid[None, :] > eid[:, None]) & (padded[None, :] > 0), eid[None, :], N_EXPERTS)
    nxt = jnp.min(later, axis=1)
    next_used = jnp.where(nxt == N_EXPERTS, eid, nxt)
    pick = lambda table: jnp.sum(jnp.where(tile_expert[:, None] == eid[None, :], table[None, :], 0), axis=1)
    k_in_group = (tile_start - pick(starts)) // tmr
    tile_next = pick(next_used)
    stages = [jnp.where(k_in_group < k, tile_expert, tile_next).astype(jnp.int32) for k in (1, 2, 3)]
    pos, p1s, p2s = [], [], []
    base = jnp.zeros((N_EXPERTS,), jnp.int32)
    for m, c in zip(metas, cnts):
        first = starts + base
        sel = lambda field: m[field].astype(jnp.int32)
        lookup = lambda e: jnp.sum(jnp.where(e[:, None] == jnp.arange(N_EXPERTS)[None, :], first[None, :], 0), axis=1)
        p1 = lookup(sel(META_E1)) + sel(META_R1)
        p2 = lookup(sel(META_E2)) + sel(META_R2)
        pos.append(jnp.concatenate([p1, p2]).astype(jnp.int32))
        p1s.append(p1)
        p2s.append(p2)
        base = base + c
    pos_all = jnp.concatenate(p1s + p2s).astype(jnp.int32)
    pads = jnp.concatenate([starts + total, padded - total,
                            jnp.stack([ends[-1], n_tiles - ends[-1] // tmr])]).astype(jnp.int32)
    return pos, pos_all, pads, tile_expert, stages, tile_valid, n_rows


def _pad_to(a, shape):
    return jnp.pad(a, [(0, t - s) for s, t in zip(a.shape, shape)])


def _regroup_chunks(a, cb):
    r = a.shape[0]
    return a.reshape(r, 3, HY_W // cb, cb).transpose(2, 0, 1, 3).reshape(HY_W // cb, r, 3 * cb)


def kernel(x_prompt, x_sample, state_l0_lru, cache_l1_ckv, cache_l1_krope, c, c_ctx, l0_norm1, l0_norm2, l0_w_mod, l0_b_mod, l0_w_in, l0_conv_a, l0_lru_conv_w, l0_lru_conv_b, l0_lru_wa, l0_lru_ba, l0_lru_wi, l0_lru_bi, l0_lru_lambda, l0_w_out, l0_ffn_gate, l0_ffn_up, l0_ffn_down, l1_norm1, l1_norm2, l1_w_mod, l1_b_mod, l1_w_in, l1_q_norm, l1_kv_norm, l1_w_q_up, l1_w_kv_up, l1_hy_short_w, l1_hy_short_b, l1_hy_f_w1, l1_hy_f_b1, l1_hy_f_w2, l1_hy_f_b2, l1_hy_f_w3, l1_hy_bias, l1_w_out, l1_router_w, l1_router_b, l1_exp_gate, l1_exp_up, l1_exp_down, final_norm):
    batch, seq, _ = x_prompt.shape
    dec_batch, dec_seq, _ = x_sample.shape
    past_len = cache_l1_ckv.shape[1]

    cond8 = jnp.concatenate([c_ctx[None, :], c, jnp.zeros((V7X_SUBLANES - 1 - dec_batch, D), F32)], axis=0)
    wcat = jnp.concatenate([l0_lru_wa[0], l0_lru_wi[0], l0_lru_wa[1], l0_lru_wi[1]], axis=-1)
    hid = l1_hy_f_w2.shape[0]
    filter_pack = jnp.concatenate([_pad_to(l1_hy_f_w1, (V7X_LANES, hid)), l1_hy_f_b1[None, :], l1_hy_f_b2[None, :],
                                   jnp.zeros((V7X_SUBLANES - 2, hid), F32)], axis=0)
    short_w = _regroup_chunks(l1_hy_short_w, HY_CB)
    short_b = _regroup_chunks(l1_hy_short_b.reshape(1, -1), HY_CB)
    hy_bias = l1_hy_bias.reshape(1, HY_W)
    router_pack = _pad_to(jnp.concatenate([l1_router_w, l1_router_b[None, :]], axis=0),
                          (D + V7X_SUBLANES, V7X_LANES))

    mod0, mod1 = adaln_tables(cond8, ((l0_w_mod, l0_b_mod), (l1_w_mod, l1_b_mod)))

    kv_ctx = kv_up(cache_l1_ckv.reshape(dec_batch * past_len, KV_RANK), l1_w_kv_up)
    kr_ctx = cache_l1_krope.reshape(dec_batch * past_len, ROPE)

    conds = ((0, batch * seq), (1, dec_seq))
    seq_lens = (seq, dec_seq)
    xs = (x_prompt.reshape(batch * seq, D), x_sample.reshape(dec_batch * dec_seq, D))
    h0s = (jnp.zeros((batch, 2, LRU_W), F32), state_l0_lru)

    us = in0_proj(xs, l0_norm1, mod0, conds, l0_w_in)
    parts, lru_states = [], []
    for u, seq_len, h0 in zip(us, seq_lens, h0s):
        ya = conv_a(u, seq_len, l0_conv_a)
        yb, lru_state = rglru(u, seq_len, l0_lru_conv_w, l0_lru_conv_b, wcat, l0_lru_ba, l0_lru_bi,
                              l0_lru_lambda, h0)
        parts.append([(ya, 0), (yb, 0), (yb, 1)])
        lru_states.append(lru_state)
    xs = mix_ffn(parts, l0_w_out, xs, l0_norm2, mod0, conds, l0_ffn_gate, l0_ffn_up, l0_ffn_down)
    new_lru = lru_states[0]

    def layer1(x, seq_len, cond, latent):
        qnope, qpe, ckv, kr, kv, uh = in1_proj(x, l1_norm1, mod1, cond, l1_w_in, l1_q_norm, l1_kv_norm,
                                               l1_w_q_up, l1_w_kv_up)
        if latent:
            yc = attn_lat(qnope, qpe, kv_ctx, kr_ctx, kv, kr, seq_len, past_len)
        else:
            yc = attn_ctx(qnope, qpe, kv, kr, seq_len)
        cs = dft_tables(seq_len)
        k_r, k_s, k_ny = hy_filter(cs, filter_pack, l1_hy_f_w2, l1_hy_f_w3)
        yd = hyena(uh, seq_len, short_w, short_b, cs, k_r, k_s, k_ny, hy_bias)
        routed = mix_route([(yc, 0), (yc, 1), (yd, 0)], l1_w_out, x, l1_norm2, mod1, cond, router_pack)
        return routed, ckv, kr

    r_p, new_ckv, new_kr = layer1(xs[0], seq, conds[0], latent=False)
    r_s, _, _ = layer1(xs[1], dec_seq, conds[1], latent=True)

    routed = (r_p, r_s)
    pos, pos_all, pads, tile_expert, stages, tile_valid, n_rows = moe_plan([r[3] for r in routed],
                                                                          [r[4] for r in routed])
    hs = moe_dispatch(pos_all, pads, n_rows, r_p[1], r_s[1])
    y_rows = moe_experts(tile_expert, stages, tile_valid, hs, l1_exp_gate, l1_exp_up, l1_exp_down)
    y_p, y_s = [moe_combine(p, r[0], r[2], mod1, cond, final_norm, y_rows)
                for p, r, cond in zip(pos, routed, conds)]
    return (y_p.reshape(batch, seq, D), y_s.reshape(dec_batch, dec_seq, D), new_lru,
            new_ckv.reshape(batch, seq, KV_RANK), new_kr.reshape(batch, seq, ROPE))
```

```python
import functools
import math

import jax
import jax.numpy as jnp
from jax import lax
from jax.experimental import pallas as pl
from jax.experimental.pallas import tpu as pltpu

F32 = jnp.float32
BF16 = jnp.bfloat16

D = 1024
GRID_W = 64
EPS = 1e-6
CONV_W = 512
LRU_W = 1024
LRU_BW = 128
LRU_C = 8.0
MLA_HEADS = 8
Q_RANK = 384
KV_RANK = 256
NOPE = 128
ROPE = 64
VDIM = 128
QK_DIM = NOPE + ROPE
ROPE_THETA = 10000.0
HY_W = 512
HY_BANDS = 16
HY_TARGET = 1e-2
HY_FAST_DECAY = 0.3
HY_SLOW_DECAY = 1.5
D_FF = 2816
N_EXPERTS = 8
D_FF_EXPERT = 1408
IN0 = 3 * CONV_W + 2 * LRU_W
IN1 = Q_RANK + KV_RANK + ROPE + 3 * HY_W

V7X_LANES = 128
V7X_SUBLANES = 8
V7X_VMEM_LIMIT_BYTES = 56 * 1024 * 1024
V7X_VMEM_LIMIT_LARGE_BYTES = 60 * 1024 * 1024

TM = 512
TN_IN0 = 512
TK_IN0 = 256
TF_FFN = 256
MIX_SLAB = 512
MOE_CHUNK = 256
TM_ROUTE = 512
TM_DISPATCH = 1024
TM_EXPERT = 512
TM_COMBINE = 512
LRU_CB = 512
HY_CB = 512
TQ = 256
ATTN_CTX_SEQS = 4
CONV_A_ROWS = 1024
LRU_ROWS = 1024
HY_ROWS = 1024
TM_IN1 = 512


def _cparams(sem, vmem_limit_bytes=V7X_VMEM_LIMIT_BYTES):
    return pltpu.CompilerParams(dimension_semantics=sem, vmem_limit_bytes=vmem_limit_bytes)


def _sigmoid(x):
    return 0.5 * jnp.tanh(0.5 * x) + 0.5


def _silu(x):
    return x * _sigmoid(x)


def _norm_mod(x, g, shift, scale):
    ms = jnp.mean(x * x, axis=-1, keepdims=True)
    y = x * lax.rsqrt(ms + EPS) * g
    return y * (1.0 + scale) + shift


def _mod_spec(comp, cond, tm, width, col_fn, tile_fn=lambda *ids: ids[0]):
    row0, seg = cond
    assert seg % tm == 0
    return pl.BlockSpec((None, 1, width),
                        lambda *ids: (comp * 3 + row0 + (tile_fn(*ids) * tm) // seg, 0, col_fn(*ids)))


def _dot3(a, b):
    a_hi = a.astype(BF16)
    a_lo = (a - a_hi.astype(F32)).astype(BF16)
    b_hi = b.astype(BF16)
    b_lo = (b - b_hi.astype(F32)).astype(BF16)
    n = a.shape[0]
    y = jnp.dot(jnp.concatenate([a_hi, a_lo], axis=0), b_hi, preferred_element_type=F32)
    return y[:n] + y[n:] + jnp.dot(a_hi, b_lo, preferred_element_type=F32)


def _adaln_kernel(c_ref, w0_ref, b0_ref, w1_ref, b1_ref, o_ref):
    a = _silu(c_ref[...])
    for layer, (w_ref, b_ref) in enumerate(((w0_ref, b0_ref), (w1_ref, b1_ref))):
        @pl.when(pl.program_id(0) == layer)
        def _():
            o_ref[...] = _dot3(a, w_ref[...]) + b_ref[...]


def adaln_tables(cond8, mods):
    tn = 1536
    nj = 6 * D // tn
    (w0, b0), (w1, b1) = mods
    at0 = lambda l, j: (0, jnp.where(l == 0, j, nj - 1))
    at1 = lambda l, j: (0, jnp.where(l == 1, j, 0))
    m = pl.pallas_call(
        _adaln_kernel,
        out_shape=jax.ShapeDtypeStruct((2, V7X_SUBLANES, 6 * D), F32),
        grid=(2, nj),
        in_specs=[pl.BlockSpec((V7X_SUBLANES, D), lambda l, j: (0, 0)),
                  pl.BlockSpec((D, tn), at0), pl.BlockSpec((1, tn), at0),
                  pl.BlockSpec((D, tn), at1), pl.BlockSpec((1, tn), at1)],
        out_specs=pl.BlockSpec((None, V7X_SUBLANES, tn), lambda l, j: (l, 0, j)),
        compiler_params=_cparams(("arbitrary", "arbitrary")),
        name="adaln",
    )(cond8, w0, b0.reshape(1, 6 * D), w1, b1.reshape(1, 6 * D))
    return [m[l, :3].reshape(3, 6, D).transpose(1, 0, 2).reshape(18, 1, D) for l in range(2)]


def _tile_of(n_load):
    return lambda s: jnp.maximum(s - n_load, 0)


def _block_of(n_load):
    return lambda s: jnp.minimum(s, n_load - 1)


class _TwoSets:
    def __init__(self, n_load, tm, tokens, conds):
        self.n_load, self.tm, self.conds = n_load, tm, conds
        self.n_a, self.n_b = tokens[0] // tm, tokens[1] // tm
        self.steps = n_load + self.n_a + self.n_b

    def tile(self, s):
        return jnp.maximum(s - self.n_load, 0)

    def in_first(self, s):
        return s - self.n_load < self.n_a

    def idx_a(self, s):
        return jnp.minimum(self.tile(s), self.n_a - 1)

    def idx_b(self, s):
        return jnp.clip(self.tile(s) - self.n_a, 0, self.n_b - 1)

    def rows(self, width):
        return (pl.BlockSpec((self.tm, width), lambda s: (self.idx_a(s), 0)),
                pl.BlockSpec((self.tm, width), lambda s: (self.idx_b(s), 0)))

    def cols(self, width, col):
        return (pl.BlockSpec((self.tm, width), lambda s: (self.idx_a(s), col)),
                pl.BlockSpec((self.tm, width), lambda s: (self.idx_b(s), col)))

    def mod_spec(self, comp):
        (row_a, seg_a), (row_b, seg_b) = self.conds
        assert seg_a % self.tm == 0 and seg_b % self.tm == 0

        def row(s):
            return jnp.where(self.in_first(s), row_a + (self.idx_a(s) * self.tm) // seg_a,
                             row_b + (self.idx_b(s) * self.tm) // seg_b)

        return pl.BlockSpec((None, 1, D), lambda s: (comp * 3 + row(s), 0, 0))


def _in0_kernel(xa_ref, xb_ref, g_ref, sh_ref, sc_ref, w_ref, oa_ref, ob_ref, w_sc, *, n_a, tn):
    s = pl.program_id(0)
    tk = w_ref.shape[0]
    n_load = w_sc.shape[0] // tk

    @pl.when(s < n_load)
    def _():
        w_sc[pl.ds(pl.multiple_of(s * tk, tk), tk), :] = w_ref[...].astype(BF16)

    def tile(x_ref, o_ref):
        h = _norm_mod(x_ref[...], g_ref[...], sh_ref[...], sc_ref[...]).astype(BF16)
        for j in range(w_sc.shape[1] // tn):
            o_ref[:, j * tn:(j + 1) * tn] = jnp.dot(h, w_sc[:, j * tn:(j + 1) * tn],
                                                    preferred_element_type=F32).astype(BF16)

    @pl.when(jnp.logical_and(s >= n_load, s - n_load < n_a))
    def _():
        tile(xa_ref, oa_ref)

    @pl.when(s - n_load >= n_a)
    def _():
        tile(xb_ref, ob_ref)


def in0_proj(xs, g, modtab, conds, w_in):
    n = w_in.shape[1]
    n_load = D // TK_IN0
    ts = _TwoSets(n_load, TM, [x.shape[0] for x in xs], conds)
    blk = _block_of(n_load)
    return pl.pallas_call(
        functools.partial(_in0_kernel, n_a=ts.n_a, tn=TN_IN0),
        out_shape=tuple(jax.ShapeDtypeStruct((x.shape[0], n), BF16) for x in xs),
        grid=(ts.steps,),
        in_specs=[*ts.rows(D),
                  pl.BlockSpec((1, D), lambda s: (0, 0)),
                  ts.mod_spec(0), ts.mod_spec(1),
                  pl.BlockSpec((TK_IN0, n), lambda s: (blk(s), 0))],
        out_specs=ts.rows(n),
        scratch_shapes=[pltpu.VMEM((D, n), BF16)],
        compiler_params=_cparams(("arbitrary",)),
        name="in0_proj",
    )(*xs, g.reshape(1, D), modtab, modtab, w_in)


def _shift_rows(v, d, t, seq_len=None):
    n = v.shape[0]
    seq_len = n if seq_len is None else seq_len
    if d > 0:
        return jnp.where(t < d, 0.0, pltpu.roll(v, d, 0))
    return jnp.where(t >= seq_len + d, 0.0, pltpu.roll(v, n + d, 0))


def _conv_a_kernel(b_ref, c_ref, x_ref, w_ref, o_ref, *, seq_len):
    v = c_ref[...].astype(F32) * x_ref[...].astype(F32)
    t = lax.broadcasted_iota(jnp.int32, v.shape, 0) & (seq_len - 1)
    w = w_ref[...]
    y = w[0:1] * _shift_rows(v, 1, t, seq_len) + w[1:2] * v + w[2:3] * _shift_rows(v, -1, t, seq_len)
    o_ref[...] = (b_ref[...].astype(F32) * y).astype(o_ref.dtype)


def conv_a(u, seq_len, conv_w):
    tokens = u.shape[0]
    rows = max(seq_len, CONV_A_ROWS)
    assert seq_len & (seq_len - 1) == 0 and rows % seq_len == 0
    return pl.pallas_call(
        functools.partial(_conv_a_kernel, seq_len=seq_len),
        out_shape=jax.ShapeDtypeStruct((tokens, CONV_W), BF16),
        grid=(tokens // rows,),
        in_specs=[pl.BlockSpec((rows, CONV_W), lambda s: (s, 0)),
                  pl.BlockSpec((rows, CONV_W), lambda s: (s, 1)),
                  pl.BlockSpec((rows, CONV_W), lambda s: (s, 2)),
                  pl.BlockSpec((3, CONV_W), lambda s: (0, 0))],
        out_specs=pl.BlockSpec((rows, CONV_W), lambda s: (s, 0)),
        compiler_params=_cparams(("parallel",)),
        name="conv_a",
    )(u, u, u, conv_w)


def _group_scan(a_sc, b_sc, k, reverse):
    planes = a_sc.shape[1] // V7X_SUBLANES
    order = range(V7X_SUBLANES - 1, -1, -1) if reverse else range(V7X_SUBLANES)
    a_acc = b_acc = None
    for r in order:
        plane = (k, pl.ds(r, planes, stride=V7X_SUBLANES), slice(None))
        a_r, b_r = a_sc[plane], b_sc[plane]
        if a_acc is None:
            a_acc, b_acc = a_r, b_r
        else:
            b_acc = a_r * b_acc + b_r
            a_acc = a_r * a_acc
            a_sc[plane] = a_acc
            b_sc[plane] = b_acc


def _rglru_kernel(gate_ref, xb_ref, cw_ref, cb_ref, wcat_ref, ba_ref, bi_ref, lam_ref, h0_ref,
                  y_ref, st_ref, af_sc, bf_sc, ab_sc, bb_sc, hf_sc, hb_sc, *, seq_len):
    n, cb = xb_ref.shape
    n_seq = n // seq_len
    n_slab = cb // LRU_BW
    xb = xb_ref[...].astype(F32)
    t = lax.broadcasted_iota(jnp.int32, xb.shape, 0) & (seq_len - 1)
    cw = cw_ref[...]
    sh = lambda d: _shift_rows(xb, d, t, seq_len)
    xc = cb_ref[...] + cw[0:1] * sh(2) + cw[1:2] * sh(1) + cw[2:3] * xb + cw[3:4] * sh(-1)
    xcb = xc.astype(BF16)

    for k in range(n_slab):
        cols = slice(k * LRU_BW, (k + 1) * LRU_BW)
        gk = jnp.dot(xcb[:, cols], wcat_ref[k].astype(BF16), preferred_element_type=F32)
        for d, (a_sc, b_sc) in enumerate(((af_sc, bf_sc), (ab_sc, bb_sc))):
            ga = gk[:, (2 * d) * LRU_BW:(2 * d + 1) * LRU_BW]
            gi = gk[:, (2 * d + 1) * LRU_BW:(2 * d + 2) * LRU_BW]
            r = _sigmoid(ga + ba_ref[d:d + 1, cols])
            i = _sigmoid(gi + bi_ref[d:d + 1, cols])
            log_a = (-LRU_C * jax.nn.softplus(-lam_ref[d:d + 1, cols])) * r
            a = jnp.exp(log_a)
            m = 1.0 - a * a
            mult = m * lax.rsqrt(jnp.maximum(m, 1e-30))
            a_sc[k] = a
            b_sc[k] = mult * (i * xc[:, cols])
            _group_scan(a_sc, b_sc, k, reverse=(d == 1))

    ng = seq_len // V7X_SUBLANES
    bcast = lambda row: jnp.broadcast_to(row, (V7X_SUBLANES, LRU_BW))
    chains = [(q, k) for q in range(n_seq) for k in range(n_slab)]
    init = tuple((bcast(h0_ref[q, 0:1, k * LRU_BW:(k + 1) * LRU_BW]),
                  bcast(h0_ref[q, 1:2, k * LRU_BW:(k + 1) * LRU_BW])) for q, k in chains)

    def step(j, carry):
        out = []
        for (q, k), (hf_in, hb_in) in zip(chains, carry):
            rf = pl.ds(pl.multiple_of(q * seq_len + j * V7X_SUBLANES, V7X_SUBLANES), V7X_SUBLANES)
            rb = pl.ds(pl.multiple_of(q * seq_len + (ng - 1 - j) * V7X_SUBLANES, V7X_SUBLANES), V7X_SUBLANES)
            hf = af_sc[k, rf, :] * hf_in + bf_sc[k, rf, :]
            hb = ab_sc[k, rb, :] * hb_in + bb_sc[k, rb, :]
            hf_sc[k, rf, :] = hf
            hb_sc[k, rb, :] = hb
            out.append((bcast(hf[V7X_SUBLANES - 1:V7X_SUBLANES]), bcast(hb[0:1])))
        return tuple(out)

    final = lax.fori_loop(0, ng, step, init)
    for (q, k), (hf_last, hb_first) in zip(chains, final):
        st_ref[q, 0:1, k * LRU_BW:(k + 1) * LRU_BW] = hf_last[0:1]
        st_ref[q, 1:2, k * LRU_BW:(k + 1) * LRU_BW] = hb_first[0:1]

    gt = gate_ref[...].astype(F32)
    gelu = 0.5 * gt * (1.0 + jnp.tanh(math.sqrt(2.0 / math.pi) * (gt + 0.044715 * (gt * gt * gt))))
    h = jnp.concatenate([hf_sc[k] + hb_sc[k] for k in range(n_slab)], axis=1)
    y_ref[...] = (h * gelu).astype(y_ref.dtype)


def rglru(u, seq_len, conv_w, conv_b, wcat, ba, bi, lam, h0):
    tokens = u.shape[0]
    nseq = tokens // seq_len
    cb = LRU_CB
    rows = max(seq_len, LRU_ROWS)
    assert seq_len & (seq_len - 1) == 0 and rows % seq_len == 0
    per_blk = rows // seq_len
    gate_blk0 = 3 * CONV_W // cb
    xb_blk0 = (3 * CONV_W + LRU_W) // cb
    seq_scr = lambda: pltpu.VMEM((cb // LRU_BW, rows, LRU_BW), F32)
    return pl.pallas_call(
        functools.partial(_rglru_kernel, seq_len=seq_len),
        out_shape=(jax.ShapeDtypeStruct((tokens, LRU_W), BF16), jax.ShapeDtypeStruct((nseq, 2, LRU_W), F32)),
        grid=(tokens // rows, LRU_W // cb),
        in_specs=[pl.BlockSpec((rows, cb), lambda s, c: (s, gate_blk0 + c)),
                  pl.BlockSpec((rows, cb), lambda s, c: (s, xb_blk0 + c)),
                  pl.BlockSpec((4, cb), lambda s, c: (0, c)),
                  pl.BlockSpec((1, cb), lambda s, c: (0, c)),
                  pl.BlockSpec((cb // LRU_BW, LRU_BW, 4 * LRU_BW), lambda s, c: (c, 0, 0)),
                  pl.BlockSpec((2, cb), lambda s, c: (0, c)),
                  pl.BlockSpec((2, cb), lambda s, c: (0, c)),
                  pl.BlockSpec((2, cb), lambda s, c: (0, c)),
                  pl.BlockSpec((per_blk, 2, cb), lambda s, c: (s, 0, c))],
        out_specs=(pl.BlockSpec((rows, cb), lambda s, c: (s, c)),
                   pl.BlockSpec((per_blk, 2, cb), lambda s, c: (s, 0, c))),
        scratch_shapes=[seq_scr() for _ in range(6)],
        compiler_params=_cparams(("parallel", "parallel")),
        name="rglru",
    )(u, u, conv_w, conv_b.reshape(1, LRU_W), wcat, ba, bi, lam, h0)


def _mix_ffn_kernel(p0a_ref, p0b_ref, p1a_ref, p1b_ref, p2a_ref, p2b_ref, wo_ref, xa_ref, xb_ref,
                    g1_ref, g_ref, sh_ref, sc_ref, g2_ref, wg_ref, wu_ref, wd_ref, oa_ref, ob_ref,
                    wo_sc, wg_sc, wu_sc, wd_sc, *, n_a):
    s = pl.program_id(0)
    n_load = wg_sc.shape[0]
    n_out = wo_sc.shape[0]

    @pl.when(s < n_out)
    def _():
        wo_sc[s] = wo_ref[...].astype(BF16)

    @pl.when(s < n_load)
    def _():
        wg_sc[s] = wg_ref[...].astype(BF16)
        wu_sc[s] = wu_ref[...].astype(BF16)
        wd_sc[s] = wd_ref[...].astype(BF16)

    def tile(p0_ref, p1_ref, p2_ref, x_ref, o_ref):
        m = jnp.dot(p0_ref[...], wo_sc[0], preferred_element_type=F32)
        m += jnp.dot(p1_ref[...], wo_sc[1], preferred_element_type=F32)
        m += jnp.dot(p2_ref[...], wo_sc[2], preferred_element_type=F32)
        x = x_ref[...] + g1_ref[...] * m
        h = _norm_mod(x, g_ref[...], sh_ref[...], sc_ref[...]).astype(BF16)
        y = None
        for f in range(n_load):
            hg = jnp.dot(h, wg_sc[f], preferred_element_type=F32)
            hu = jnp.dot(h, wu_sc[f], preferred_element_type=F32)
            act = (_silu(hg) * hu).astype(BF16)
            yf = jnp.dot(act, wd_sc[f], preferred_element_type=F32)
            y = yf if y is None else y + yf
        o_ref[...] = x + g2_ref[...] * y

    @pl.when(jnp.logical_and(s >= n_load, s - n_load < n_a))
    def _():
        tile(p0a_ref, p1a_ref, p2a_ref, xa_ref, oa_ref)

    @pl.when(s - n_load >= n_a)
    def _():
        tile(p0b_ref, p1b_ref, p2b_ref, xb_ref, ob_ref)


def mix_ffn(parts, w_out, xs, g, modtab, conds, w_gate, w_up, w_down):
    tf = TF_FFN
    kb = MIX_SLAB
    n_load = D_FF // tf
    n_out = len(parts[0])
    assert n_out <= n_load
    ts = _TwoSets(n_load, TM, [x.shape[0] for x in xs], conds)
    blk = _block_of(n_load)
    oblk = _block_of(n_out)
    lhs_specs, lhs_args = [], []
    for (arr_a, col_a), (arr_b, col_b) in zip(*parts):
        assert col_a == col_b
        lhs_specs += ts.cols(kb, col_a)
        lhs_args += [arr_a, arr_b]
    return pl.pallas_call(
        functools.partial(_mix_ffn_kernel, n_a=ts.n_a),
        out_shape=tuple(jax.ShapeDtypeStruct(x.shape, F32) for x in xs),
        grid=(ts.steps,),
        in_specs=lhs_specs + [
            pl.BlockSpec((kb, D), lambda s: (oblk(s), 0)),
            *ts.rows(D),
            ts.mod_spec(2),
            pl.BlockSpec((1, D), lambda s: (0, 0)),
            ts.mod_spec(3), ts.mod_spec(4), ts.mod_spec(5),
            pl.BlockSpec((D, tf), lambda s: (0, blk(s))),
            pl.BlockSpec((D, tf), lambda s: (0, blk(s))),
            pl.BlockSpec((tf, D), lambda s: (blk(s), 0))],
        out_specs=ts.rows(D),
        scratch_shapes=[pltpu.VMEM((n_out, kb, D), BF16),
                        pltpu.VMEM((n_load, D, tf), BF16), pltpu.VMEM((n_load, D, tf), BF16),
                        pltpu.VMEM((n_load, tf, D), BF16)],
        compiler_params=_cparams(("arbitrary",), V7X_VMEM_LIMIT_LARGE_BYTES),
        name="mix_ffn",
    )(*lhs_args, w_out, *xs, modtab, g.reshape(1, D), modtab, modtab, modtab, w_gate, w_up, w_down)


def _rms(x, g):
    return x * lax.rsqrt(jnp.mean(x * x, axis=-1, keepdims=True) + EPS) * g


def _in1_kernel(x_ref, g_ref, sh_ref, sc_ref, w_ref, qn_ref, kvn_ref, wq_ref, wkv_ref,
                qnope_ref, qpe_ref, ckv_ref, kr_ref, kv_ref, uh_ref, w_sc, wq_sc, wkv_sc):
    @pl.when(pl.program_id(0) == 0)
    def _():
        w_sc[...] = w_ref[...].astype(BF16)
        for h in range(MLA_HEADS):
            c0 = h * QK_DIM
            wq_sc[:, h * NOPE:(h + 1) * NOPE] = wq_ref[:, c0:c0 + NOPE].astype(BF16)
            r0 = MLA_HEADS * NOPE + h * ROPE
            wq_sc[:, r0:r0 + ROPE] = wq_ref[:, c0 + NOPE:c0 + QK_DIM].astype(BF16)
        wkv_sc[...] = wkv_ref[...].astype(BF16)

    h = _norm_mod(x_ref[...], g_ref[...], sh_ref[...], sc_ref[...]).astype(BF16)
    u = lax.dot_general(h, w_sc[...], (((1,), (1,)), ((), ())), preferred_element_type=F32)
    o1, o2, o3 = Q_RANK, Q_RANK + KV_RANK, Q_RANK + KV_RANK + ROPE
    cq = _rms(u[:, :o1], qn_ref[...])
    q = jnp.dot(cq.astype(BF16), wq_sc[...], preferred_element_type=F32) * _SCALE
    qnope_ref[...] = q[:, :MLA_HEADS * NOPE].astype(qnope_ref.dtype)
    qpe_ref[...] = q[:, MLA_HEADS * NOPE:]
    ckv = _rms(u[:, o1:o2], kvn_ref[...])
    ckv_ref[...] = ckv
    kv_ref[...] = jnp.dot(ckv.astype(BF16), wkv_sc[...], preferred_element_type=F32).astype(kv_ref.dtype)
    kr_ref[...] = u[:, o2:o3]
    uh_ref[...] = u[:, o3:]


def in1_proj(x, g, modtab, cond, w_in, q_norm, kv_norm, w_q_up, w_kv_up):
    tokens = x.shape[0]
    tm = TM_IN1
    nkv = MLA_HEADS * (NOPE + VDIM)
    const = lambda i: (0, 0)
    zero = lambda i: 0
    once = pl.Buffered(1)
    outs = (jax.ShapeDtypeStruct((tokens, MLA_HEADS * NOPE), BF16),
            jax.ShapeDtypeStruct((tokens, MLA_HEADS * ROPE), F32),
            jax.ShapeDtypeStruct((tokens, KV_RANK), F32),
            jax.ShapeDtypeStruct((tokens, ROPE), F32),
            jax.ShapeDtypeStruct((tokens, nkv), BF16),
            jax.ShapeDtypeStruct((tokens, 3 * HY_W), F32))
    row = lambda w: pl.BlockSpec((tm, w), lambda i: (i, 0))
    return pl.pallas_call(
        _in1_kernel,
        out_shape=outs,
        grid=(tokens // tm,),
        in_specs=[row(D),
                  pl.BlockSpec((1, D), const),
                  _mod_spec(0, cond, tm, D, zero),
                  _mod_spec(1, cond, tm, D, zero),
                  pl.BlockSpec((IN1, D), const, pipeline_mode=once),
                  pl.BlockSpec((1, Q_RANK), const),
                  pl.BlockSpec((1, KV_RANK), const),
                  pl.BlockSpec((Q_RANK, MLA_HEADS * QK_DIM), const, pipeline_mode=once),
                  pl.BlockSpec((KV_RANK, nkv), const, pipeline_mode=once)],
        out_specs=tuple(row(o.shape[1]) for o in outs),
        scratch_shapes=[pltpu.VMEM((IN1, D), BF16), pltpu.VMEM((Q_RANK, MLA_HEADS * QK_DIM), BF16),
                        pltpu.VMEM((KV_RANK, nkv), BF16)],
        compiler_params=_cparams(("arbitrary",)),
        name="in1_proj",
    )(x, g.reshape(1, D), modtab, modtab, w_in.T, q_norm.reshape(1, Q_RANK), kv_norm.reshape(1, KV_RANK),
      w_q_up, w_kv_up)


def _mm_kernel(a_ref, w_ref, o_ref):
    o_ref[...] = jnp.dot(a_ref[...].astype(BF16), w_ref[...].astype(BF16),
                         preferred_element_type=F32).astype(o_ref.dtype)


def kv_up(ckv, w_kv_up):
    rows = ckv.shape[0]
    n = w_kv_up.shape[1]
    return pl.pallas_call(
        _mm_kernel,
        out_shape=jax.ShapeDtypeStruct((rows, n), BF16),
        grid=(rows // TM,),
        in_specs=[pl.BlockSpec((TM, KV_RANK), lambda i: (i, 0)), pl.BlockSpec((KV_RANK, n), lambda i: (0, 0))],
        out_specs=pl.BlockSpec((TM, n), lambda i: (i, 0)),
        compiler_params=_cparams(("parallel",)),
        name="kv_up",
    )(ckv, w_kv_up)


_NT = (((1,), (1,)), ((), ()))
_SCALE = 1.0 / math.sqrt(QK_DIM)


def _fill_rope_tables(cos_ref, sin_ref):
    n, width = cos_ref.shape
    n_grid_rows = n // GRID_W
    n_freq = ROPE // 4

    def trig(count):
        lane = lax.broadcasted_iota(jnp.int32, (count, width), 1)
        j = lane & (ROPE // 2 - 1)
        inv = jnp.exp((j & (n_freq - 1)).astype(F32) * (-math.log(ROPE_THETA) / n_freq))
        ang = lax.broadcasted_iota(jnp.int32, (count, width), 0).astype(F32) * inv
        return jnp.cos(ang), jnp.sin(ang), j < n_freq

    cos_c, sin_c, by_row = trig(GRID_W)
    cos_r, sin_r, _ = trig(n_grid_rows)
    for r in range(n_grid_rows):
        rows = slice(r * GRID_W, (r + 1) * GRID_W)
        cos_ref[rows, :] = jnp.where(by_row, jnp.broadcast_to(cos_r[r:r + 1], cos_c.shape), cos_c)
        sin_ref[rows, :] = jnp.where(by_row, jnp.broadcast_to(sin_r[r:r + 1], sin_c.shape), sin_c)


def _rope(x, cos, sin):
    width = x.shape[1]
    lane = lax.broadcasted_iota(jnp.int32, x.shape, 1)
    first_half = (lane & (ROPE - 1)) < ROPE // 2
    xr = jnp.where(first_half, -pltpu.roll(x, width - ROPE // 2, 1), pltpu.roll(x, ROPE // 2, 1))
    return x * cos + xr * sin


def _ones_column(n):
    lane = lax.broadcasted_iota(jnp.int32, (n, VDIM), 1)
    return jnp.where(lane == 0, 1.0, 0.0).astype(BF16)


def _head_attention(qcat, kcat, vaug):
    s = lax.dot_general(qcat, kcat, _NT, preferred_element_type=F32)
    p = jnp.exp(s - jnp.max(s, axis=-1, keepdims=True)).astype(BF16)
    oa = jnp.dot(p, vaug, preferred_element_type=F32)
    return oa[:, :VDIM] / oa[:, VDIM:VDIM + 1]


def _attn_ctx_kernel(qn_ref, qpe_ref, kv_ref, kr_ref, o_ref, *, seq_len):
    n = qn_ref.shape[0]
    n_seq = n // seq_len
    ones = _ones_column(n)
    kpe = kr_ref[...].astype(BF16)
    per_seq = lambda a: a.reshape(n_seq, seq_len, a.shape[-1])
    for h in range(MLA_HEADS):
        c0 = h * (NOPE + VDIM)
        qcat = per_seq(jnp.concatenate([qn_ref[:, h * NOPE:(h + 1) * NOPE],
                                        qpe_ref[:, h * ROPE:(h + 1) * ROPE].astype(BF16)], axis=1))
        kcat = per_seq(jnp.concatenate([kv_ref[:, c0:c0 + NOPE], kpe], axis=1))
        vaug = per_seq(jnp.concatenate([kv_ref[:, c0 + NOPE:c0 + NOPE + VDIM], ones], axis=1))
        s = jnp.einsum("bqd,bkd->bqk", qcat, kcat, preferred_element_type=F32)
        p = jnp.exp(s - jnp.max(s, axis=-1, keepdims=True)).astype(BF16)
        oa = jnp.einsum("bqk,bkd->bqd", p, vaug, preferred_element_type=F32)
        o = oa[:, :, :VDIM] / oa[:, :, VDIM:VDIM + 1]
        o_ref[:, h * VDIM:(h + 1) * VDIM] = o.reshape(n, VDIM).astype(o_ref.dtype)


def attn_ctx(qnope, qpe, kv, kr, seq_len):
    tokens = qnope.shape[0]
    rows = ATTN_CTX_SEQS * seq_len
    blk = lambda w: pl.BlockSpec((rows, w), lambda s: (s, 0))
    return pl.pallas_call(
        functools.partial(_attn_ctx_kernel, seq_len=seq_len),
        out_shape=jax.ShapeDtypeStruct((tokens, MLA_HEADS * VDIM), BF16),
        grid=(tokens // rows,),
        in_specs=[blk(MLA_HEADS * NOPE), blk(MLA_HEADS * ROPE), blk(MLA_HEADS * (NOPE + VDIM)), blk(ROPE)],
        out_specs=blk(MLA_HEADS * VDIM),
        compiler_params=_cparams(("parallel",)),
        name="attn_ctx",
    )(qnope, qpe, kv, kr)


def _attn_lat_kernel(qn_ref, qpe_ref, kvc_ref, krc_ref, kvl_ref, krl_ref, o_ref, kcat_sc, vaug_sc, cos_sc, sin_sc):
    tq = qn_ref.shape[0]
    n_ctx = krc_ref.shape[0]
    n_lat = krl_ref.shape[0]

    @pl.when(pl.program_id(1) == 0)
    def _():
        _fill_rope_tables(cos_sc, sin_sc)
        kr2 = jnp.concatenate([krl_ref[...], krl_ref[...]], axis=1)
        kpe_lat = _rope(kr2, cos_sc[...], sin_sc[...])[:, :ROPE].astype(BF16)
        kpe_ctx = krc_ref[...].astype(BF16)
        ones_c, ones_l = _ones_column(n_ctx), _ones_column(n_lat)
        for h in range(MLA_HEADS):
            c0 = h * (NOPE + VDIM)
            for r0, nr, kv_ref, kpe, ones in ((0, n_ctx, kvc_ref, kpe_ctx, ones_c), (n_ctx, n_lat, kvl_ref, kpe_lat, ones_l)):
                kcat_sc[h, r0:r0 + nr, 0:NOPE] = kv_ref[:, c0:c0 + NOPE]
                kcat_sc[h, r0:r0 + nr, NOPE:QK_DIM] = kpe
                vaug_sc[h, r0:r0 + nr, 0:VDIM] = kv_ref[:, c0 + NOPE:c0 + NOPE + VDIM]
                vaug_sc[h, r0:r0 + nr, VDIM:2 * VDIM] = ones

    q0 = pl.multiple_of(pl.program_id(1) * tq, tq)
    rep = lambda a: jnp.concatenate([a] * (MLA_HEADS // 2), axis=1)
    qp_all = _rope(qpe_ref[...], rep(cos_sc[pl.ds(q0, tq), :]), rep(sin_sc[pl.ds(q0, tq), :])).astype(BF16)
    for h in range(MLA_HEADS):
        qcat = jnp.concatenate([qn_ref[:, h * NOPE:(h + 1) * NOPE], qp_all[:, h * ROPE:(h + 1) * ROPE]], axis=1)
        o_ref[:, h * VDIM:(h + 1) * VDIM] = _head_attention(qcat, kcat_sc[h], vaug_sc[h]).astype(o_ref.dtype)


def attn_lat(qnope, qpe, kv_ctx, kr_ctx, kv_lat, kr_lat, seq_len, ctx_len):
    tokens = qnope.shape[0]
    nq = seq_len // TQ
    qblk = lambda w: pl.BlockSpec((TQ, w), lambda b, i: (b * nq + i, 0))
    seq = lambda n, w: pl.BlockSpec((n, w), lambda b, i: (b, 0))
    nkv = MLA_HEADS * (NOPE + VDIM)
    n_keys = ctx_len + seq_len
    return pl.pallas_call(
        _attn_lat_kernel,
        out_shape=jax.ShapeDtypeStruct((tokens, MLA_HEADS * VDIM), BF16),
        grid=(tokens // seq_len, nq),
        in_specs=[qblk(MLA_HEADS * NOPE), qblk(MLA_HEADS * ROPE), seq(ctx_len, nkv), seq(ctx_len, ROPE),
                  seq(seq_len, nkv), seq(seq_len, ROPE)],
        out_specs=qblk(MLA_HEADS * VDIM),
        scratch_shapes=[pltpu.VMEM((MLA_HEADS, n_keys, QK_DIM), BF16),
                        pltpu.VMEM((MLA_HEADS, n_keys, 2 * VDIM), BF16),
                        pltpu.VMEM((seq_len, 2 * ROPE), F32), pltpu.VMEM((seq_len, 2 * ROPE), F32)],
        compiler_params=_cparams(("parallel", "arbitrary")),
        name="attn_lat",
    )(qnope, qpe, kv_ctx, kr_ctx, kv_lat, kr_lat)


def _dft_kernel(o_ref):
    tr, n = o_ref.shape[1], o_ref.shape[2]
    nb = n // V7X_LANES
    f = pl.program_id(0) * tr + lax.broadcasted_iota(jnp.int32, (tr, V7X_LANES), 0)
    j = lax.broadcasted_iota(jnp.int32, (tr, V7X_LANES), 1)

    def cos_sin(m):
        ang = (m & (2 * n - 1)).astype(F32) * (math.pi / n)
        return jnp.cos(ang), jnp.sin(ang)

    cj, sj = cos_sin(f * j)
    cb, sb = cos_sin(f * (j * V7X_LANES))
    for b in range(nb):
        cbb, sbb = cb[:, b:b + 1], sb[:, b:b + 1]
        cols = slice(b * V7X_LANES, (b + 1) * V7X_LANES)
        o_ref[0, :, cols] = (cbb * cj - sbb * sj).astype(o_ref.dtype)
        o_ref[1, :, cols] = (sbb * cj + cbb * sj).astype(o_ref.dtype)


def dft_tables(n):
    tr = 128
    return pl.pallas_call(
        _dft_kernel,
        out_shape=jax.ShapeDtypeStruct((2, n, n), BF16),
        grid=(n // tr,),
        out_specs=pl.BlockSpec((2, tr, n), lambda i: (0, i, 0)),
        compiler_params=_cparams(("parallel",)),
        name="dft_tables",
    )()


def _split_dot(table, x):
    hi = x.astype(BF16)
    lo = (x - hi.astype(F32)).astype(BF16)
    return (jnp.dot(table, hi, preferred_element_type=F32) + jnp.dot(table, lo, preferred_element_type=F32))


def _hy_filter_kernel(cs_ref, pack_ref, w2_ref, w3_ref, kr_ref, ks_ref, kny_ref):
    n = cs_ref.shape[1]
    row = lax.broadcasted_iota(jnp.int32, (n, V7X_LANES), 0).astype(F32)
    lane = lax.broadcasted_iota(jnp.int32, (n, V7X_LANES), 1)
    t = row * (1.0 / (n - 1))
    w = (2.0 * math.pi) * row / n
    band = jnp.where(lane <= HY_BANDS, lane - 1, lane - 1 - HY_BANDS).astype(F32)
    freq = 1e-4 + band * ((HY_BANDS - 1 - 1e-4) / (HY_BANDS - 1))
    arg = jnp.where(lane <= HY_BANDS, freq * w + 0.5 * math.pi, -(freq * w))
    z = jnp.where(lane == 0, t, jnp.where(lane <= 2 * HY_BANDS, jnp.sin(arg), 0.0))
    hid = jnp.sin(_dot3(z, pack_ref[0:V7X_LANES, :]) + pack_ref[V7X_LANES:V7X_LANES + 1, :])
    hid = jnp.sin(_dot3(hid, w2_ref[...]) + pack_ref[V7X_LANES + 1:V7X_LANES + 2, :])
    hf = _dot3(hid, w3_ref[...])

    rowc = lax.broadcasted_iota(jnp.int32, (n, HY_W), 0)
    chan = lax.broadcasted_iota(jnp.int32, (n, HY_W), 1).astype(F32)
    max_decay = math.log(HY_TARGET) / HY_FAST_DECAY
    min_decay = math.log(HY_TARGET) / HY_SLOW_DECAY
    deltas = min_decay + chan * ((max_decay - min_decay) / (HY_W - 1))
    decay = jnp.exp(-(rowc.astype(F32) * (1.0 / (n - 1))) * jnp.abs(deltas))
    h_fwd = hf[:, :HY_W] * decay
    h_bwd = jnp.where(rowc == 0, 0.0, hf[:, HY_W:] * decay)
    norm = jnp.sum(jnp.abs(h_fwd) + jnp.abs(h_bwd), axis=0, keepdims=True)
    even = (h_fwd + h_bwd) / norm
    odd = (h_fwd - h_bwd) / norm
    cf = jnp.where(rowc == 0, 1.0, 2.0) * (1.0 / (2 * n))
    kr_ref[...] = cf * _split_dot(cs_ref[0], even)
    ks_ref[...] = cf * _split_dot(cs_ref[1], odd)
    sgn = jnp.where((rowc & 1) == 1, -1.0, 1.0)
    kny_ref[...] = jnp.sum(sgn * even, axis=0, keepdims=True) * (1.0 / (2 * n))


def hy_filter(cs, pack, w2, w3):
    n = cs.shape[1]
    full = lambda a: pl.BlockSpec(a.shape, lambda: (0,) * a.ndim)
    args = (cs, pack, w2, w3)
    return pl.pallas_call(
        _hy_filter_kernel,
        out_shape=(jax.ShapeDtypeStruct((n, HY_W), F32), jax.ShapeDtypeStruct((n, HY_W), F32),
                   jax.ShapeDtypeStruct((1, HY_W), F32)),
        in_specs=[full(a) for a in args],
        out_specs=(pl.BlockSpec((n, HY_W), lambda: (0, 0)), pl.BlockSpec((n, HY_W), lambda: (0, 0)),
                   pl.BlockSpec((1, HY_W), lambda: (0, 0))),
        compiler_params=pltpu.CompilerParams(vmem_limit_bytes=V7X_VMEM_LIMIT_BYTES),
        name="hy_filter",
    )(*args)


def _hyena_kernel(u0_ref, u1_ref, u2_ref, sw_ref, sb_ref, cs_ref, kr_ref, ks_ref, kny_ref, bias_ref, o_ref,
                  *, seq_len):
    n, cb = u0_ref.shape
    n_seq = n // seq_len
    t = lax.broadcasted_iota(jnp.int32, (n, cb), 0) & (seq_len - 1)

    def short_conv(u_ref, k):
        u = u_ref[...]
        w = sw_ref[:, k * cb:(k + 1) * cb]
        return (sb_ref[:, k * cb:(k + 1) * cb] + w[0:1] * _shift_rows(u, 1, t, seq_len) + w[1:2] * u
                + w[2:3] * _shift_rows(u, -1, t, seq_len))

    x0 = short_conv(u0_ref, 0)
    z = short_conv(u1_ref, 1) * short_conv(u2_ref, 2)
    wide = lambda a: jnp.concatenate([a[q * seq_len:(q + 1) * seq_len] for q in range(n_seq)], axis=1)
    rep = lambda a: jnp.concatenate([a] * n_seq, axis=1)
    zw = wide(z)
    zb = zw.astype(BF16)
    c, s = cs_ref[0], cs_ref[1]
    ur = jnp.dot(c, zb, preferred_element_type=F32)
    us = jnp.dot(s, zb, preferred_element_type=F32)
    sgn = jnp.where((lax.broadcasted_iota(jnp.int32, zw.shape, 0) & 1) == 1, -1.0, 1.0)
    uny = jnp.sum(sgn * zw, axis=0, keepdims=True)
    kr, ks = rep(kr_ref[...]), rep(ks_ref[...])
    yr = (ur * kr - us * ks).astype(BF16)
    ys = (ur * ks + us * kr).astype(BF16)
    yw = jnp.dot(c, yr, preferred_element_type=F32) + jnp.dot(s, ys, preferred_element_type=F32)
    yw = yw + sgn * (uny * rep(kny_ref[...]))
    y = jnp.concatenate([yw[:, q * cb:(q + 1) * cb] for q in range(n_seq)], axis=0)
    o_ref[...] = (x0 * (y + bias_ref[...] * z)).astype(o_ref.dtype)


def hyena(uh, seq_len, short_w, short_b, cs, kr, ks, kny, bias):
    tokens = uh.shape[0]
    cb = HY_CB
    nc = HY_W // cb
    rows = max(seq_len, HY_ROWS)
    assert seq_len & (seq_len - 1) == 0 and rows % seq_len == 0
    ublk = lambda k: pl.BlockSpec((rows, cb), lambda s, c: (s, k * nc + c))
    chan = lambda r: pl.BlockSpec((r, cb), lambda s, c: (0, c))
    return pl.pallas_call(
        functools.partial(_hyena_kernel, seq_len=seq_len),
        out_shape=jax.ShapeDtypeStruct((tokens, HY_W), BF16),
        grid=(tokens // rows, nc),
        in_specs=[ublk(0), ublk(1), ublk(2),
                  pl.BlockSpec((None, 3, 3 * cb), lambda s, c: (c, 0, 0)),
                  pl.BlockSpec((None, 1, 3 * cb), lambda s, c: (c, 0, 0)),
                  pl.BlockSpec((2, seq_len, seq_len), lambda s, c: (0, 0, 0)),
                  chan(seq_len), chan(seq_len), chan(1), chan(1)],
        out_specs=pl.BlockSpec((rows, cb), lambda s, c: (s, c)),
        compiler_params=_cparams(("parallel", "parallel")),
        name="hyena",
    )(uh, uh, uh, short_w, short_b, cs, kr, ks, kny, bias)


META_E1, META_E2, META_R1, META_R2, META_G1, META_G2 = range(6)


def _route_kernel(p0_ref, p1_ref, p2_ref, wo_ref, x_ref, g1_ref, g_ref, sh_ref, sc_ref, wr_ref,
                  x1_ref, h_ref, meta_ref, meta_t_ref, cnt_ref, run_sc, wo_sc):
    tm = x_ref.shape[0]
    lane = lax.broadcasted_iota(jnp.int32, (tm, V7X_LANES), 1)

    @pl.when(pl.program_id(0) == 0)
    def _():
        run_sc[...] = jnp.zeros_like(run_sc)
        wo_sc[...] = wo_ref[...].astype(BF16)

    kb = p0_ref.shape[1]
    m = jnp.dot(p0_ref[...], wo_sc[0:kb, :], preferred_element_type=F32)
    m += jnp.dot(p1_ref[...], wo_sc[kb:2 * kb, :], preferred_element_type=F32)
    m += jnp.dot(p2_ref[...], wo_sc[2 * kb:3 * kb, :], preferred_element_type=F32)
    x1 = x_ref[...] + g1_ref[...] * m
    x1_ref[...] = x1
    h = _norm_mod(x1, g_ref[...], sh_ref[...], sc_ref[...])
    h_ref[...] = h
    logits = _dot3(h, wr_ref[0:D, :]) + wr_ref[D:D + 1, :]
    lg = jnp.where(lane < N_EXPERTS, logits, -jnp.inf)
    l1 = jnp.max(lg, axis=-1, keepdims=True)
    i1 = jnp.min(jnp.where(lg == l1, lane, V7X_LANES), axis=-1, keepdims=True)
    rest = jnp.where(lane == i1, -jnp.inf, lg)
    l2 = jnp.max(rest, axis=-1, keepdims=True)
    i2 = jnp.min(jnp.where(rest == l2, lane, V7X_LANES), axis=-1, keepdims=True)
    gap = jnp.exp(l2 - l1)
    gate1 = 1.0 / (1.0 + gap)
    gate2 = gap * gate1
    m1 = lane == i1
    m2 = lane == i2
    chosen = jnp.where(m1 | m2, 1.0, 0.0)
    r = lax.broadcasted_iota(jnp.int32, (tm, tm), 0)
    c = lax.broadcasted_iota(jnp.int32, (tm, tm), 1)
    tri = jnp.where(c < r, 1.0, 0.0).astype(BF16)
    before = jnp.dot(tri, chosen.astype(BF16), preferred_element_type=F32) + run_sc[0:1, :]
    rank1 = jnp.sum(jnp.where(m1, before, 0.0), axis=-1, keepdims=True)
    rank2 = jnp.sum(jnp.where(m2, before, 0.0), axis=-1, keepdims=True)
    vals = (i1.astype(F32), i2.astype(F32), rank1, rank2, gate1, gate2)
    meta = jnp.zeros((tm, V7X_LANES), F32)
    for k, v in enumerate(vals):
        meta = jnp.where(lane == k, v, meta)
    meta_ref[...] = meta
    meta_t_ref[...] = meta.T[:V7X_SUBLANES]
    run_sc[...] = run_sc[...] + jnp.sum(chosen, axis=0, keepdims=True)
    cnt_ref[...] = run_sc[...]


def mix_route(parts, w_out, x, g, modtab, cond, router_pack):
    tokens = x.shape[0]
    tm = TM_ROUTE
    kb = MIX_SLAB
    zero = lambda i: 0
    const = lambda i: (0, 0)
    rows = lambda w: pl.BlockSpec((tm, w), lambda i: (i, 0))
    lhs_specs = [pl.BlockSpec((tm, kb), (lambda i, cbk=cbk: (i, cbk))) for _, cbk in parts]
    return pl.pallas_call(
        _route_kernel,
        out_shape=(jax.ShapeDtypeStruct((tokens, D), F32),
                   jax.ShapeDtypeStruct((tokens, D), F32),
                   jax.ShapeDtypeStruct((tokens, V7X_LANES), F32),
                   jax.ShapeDtypeStruct((V7X_SUBLANES, tokens), F32),
                   jax.ShapeDtypeStruct((V7X_SUBLANES, V7X_LANES), F32)),
        grid=(tokens // tm,),
        in_specs=lhs_specs + [
            pl.BlockSpec((len(parts) * kb, D), const, pipeline_mode=pl.Buffered(1)),
            rows(D),
            _mod_spec(2, cond, tm, D, zero),
            pl.BlockSpec((1, D), const),
            _mod_spec(3, cond, tm, D, zero),
            _mod_spec(4, cond, tm, D, zero),
            pl.BlockSpec((D + V7X_SUBLANES, V7X_LANES), const)],
        out_specs=(rows(D), rows(D), rows(V7X_LANES),
                   pl.BlockSpec((V7X_SUBLANES, tm), lambda i: (0, i)),
                   pl.BlockSpec((V7X_SUBLANES, V7X_LANES), const)),
        scratch_shapes=[pltpu.VMEM((V7X_SUBLANES, V7X_LANES), F32), pltpu.VMEM((len(parts) * kb, D), BF16)],
        compiler_params=_cparams(("arbitrary",)),
        name="mix_route",
    )(*[a for a, _ in parts], w_out, x, modtab, g.reshape(1, D), modtab, modtab, router_pack)


def _row_copy(src_ref, src_row, dst_ref, dst_row, sem):
    return pltpu.make_async_copy(src_ref.at[pl.ds(src_row, 1)], dst_ref.at[pl.ds(dst_row, 1)], sem)


_PAD_BULK = (256, 128, 64, 32, 16, 8)


def _zero_fill(hs_ref, zero_sc, sem, pads_ref, n_tail_max, wait):
    tmr = zero_sc.shape[0]

    def copy(rows, dst):
        cp = pltpu.make_async_copy(zero_sc.at[pl.ds(0, rows)], hs_ref.at[pl.ds(dst, rows)], sem)
        cp.wait() if wait else cp.start()

    for e in range(N_EXPERTS):
        start, n = pads_ref[e], pads_ref[N_EXPERTS + e]
        head = jnp.minimum((-start) & (V7X_SUBLANES - 1), n)
        for r in range(V7X_SUBLANES - 1):
            @pl.when(r < head)
            def _():
                copy(1, start + r)
        body = start + head
        rem = n - head
        for k in _PAD_BULK:
            @pl.when((rem & k) != 0)
            def _():
                copy(k, pl.multiple_of(body + (rem & ~(2 * k - 1)), V7X_SUBLANES))
    tail_start, tail_tiles = pads_ref[2 * N_EXPERTS], pads_ref[2 * N_EXPERTS + 1]
    for t in range(n_tail_max):
        @pl.when(t < tail_tiles)
        def _():
            copy(tmr, pl.multiple_of(tail_start + t * tmr, tmr))


def _dispatch_kernel(pos_ref, pads_ref, ha_ref, hb_ref, hs_ref, zero_sc, sem, zsem, *, n_a, n_tail_max):
    tm = ha_ref.shape[0]
    n_tok = pos_ref.shape[0] // 2
    i = pl.program_id(0)
    base = i * tm

    @pl.when(i == 0)
    def _():
        zero_sc[...] = jnp.zeros_like(zero_sc)
        _zero_fill(hs_ref, zero_sc, zsem, pads_ref, n_tail_max, wait=False)

    def scatter(h_ref):
        def issue(g, carry):
            r0 = pl.multiple_of(g * V7X_SUBLANES, V7X_SUBLANES)
            rows = h_ref.at[pl.ds(r0, V7X_SUBLANES)]
            for u in range(V7X_SUBLANES):
                _row_copy(rows, u, hs_ref, pos_ref[base + r0 + u], sem).start(priority=0)
                _row_copy(rows, u, hs_ref, pos_ref[n_tok + base + r0 + u], sem).start(priority=1)
            return carry

        lax.fori_loop(0, tm // V7X_SUBLANES, issue, 0)
        for _ in range(2):
            pltpu.make_async_copy(h_ref, hs_ref.at[pl.ds(0, tm)], sem).wait()

    @pl.when(i < n_a)
    def _():
        scatter(ha_ref)

    @pl.when(i >= n_a)
    def _():
        scatter(hb_ref)

    @pl.when(i == 0)
    def _():
        _zero_fill(hs_ref, zero_sc, zsem, pads_ref, n_tail_max, wait=True)


def moe_dispatch(pos, pads, hs_rows, h_a, h_b):
    tm = TM_DISPATCH
    n_a, n_b = h_a.shape[0] // tm, h_b.shape[0] // tm
    n_tail_max = hs_rows // TM_EXPERT - (2 * (h_a.shape[0] + h_b.shape[0])) // TM_EXPERT
    return pl.pallas_call(
        functools.partial(_dispatch_kernel, n_a=n_a, n_tail_max=n_tail_max),
        out_shape=jax.ShapeDtypeStruct((hs_rows, D), F32),
        grid_spec=pltpu.PrefetchScalarGridSpec(
            num_scalar_prefetch=2,
            grid=(n_a + n_b,),
            in_specs=[pl.BlockSpec((tm, D), lambda i, *pf: (jnp.minimum(i, n_a - 1), 0)),
                      pl.BlockSpec((tm, D), lambda i, *pf: (jnp.clip(i - n_a, 0, n_b - 1), 0))],
            out_specs=pl.BlockSpec(memory_space=pl.ANY),
            scratch_shapes=[pltpu.VMEM((TM_EXPERT, D), F32), pltpu.SemaphoreType.DMA(()),
                            pltpu.SemaphoreType.DMA(())]),
        compiler_params=_cparams(("arbitrary",)),
        name="moe_dispatch",
    )(pos, pads, h_a, h_b)


def _experts_kernel(te_ref, sg_ref, su_ref, sd_ref, nv_ref, hs_ref, wg_ref, wu_ref, wd_ref, y_ref,
                    wg_sc, wu_sc, wd_sc):
    del sg_ref, su_ref, sd_ref
    j = pl.program_id(0)
    e = te_ref[j]
    e_prev = te_ref[jnp.maximum(j - 1, 0)]
    n_valid = nv_ref[j]
    half = y_ref.shape[0] // 2

    @pl.when((j == 0) | (e != e_prev))
    def _():
        wg_sc[...] = wg_ref[...].astype(BF16)
        wu_sc[...] = wu_ref[...].astype(BF16)
        wd_sc[...] = wd_ref[...].astype(BF16)

    def swiglu(rows):
        h = hs_ref[rows, :].astype(BF16)
        y = None
        for c0 in range(0, D_FF_EXPERT, MOE_CHUNK):
            c1 = min(c0 + MOE_CHUNK, D_FF_EXPERT)
            hg = jnp.dot(h, wg_sc[:, c0:c1], preferred_element_type=F32)
            hu = jnp.dot(h, wu_sc[:, c0:c1], preferred_element_type=F32)
            act = (_silu(hg) * hu).astype(BF16)
            yc = jnp.dot(act, wd_sc[c0:c1, :], preferred_element_type=F32)
            y = yc if y is None else y + yc
        y_ref[rows, :] = y

    @pl.when(n_valid > half)
    def _():
        swiglu(slice(None))

    @pl.when((n_valid > 0) & (n_valid <= half))
    def _():
        swiglu(slice(0, half))
        y_ref[half:, :] = jnp.zeros((half, D), F32)

    @pl.when(n_valid == 0)
    def _():
        y_ref[...] = jnp.zeros_like(y_ref)


def moe_experts(tile_expert, stages, tile_valid, hs, e_gate, e_up, e_down):
    rows = hs.shape[0]
    tmr = TM_EXPERT
    wspec = lambda shape, k: pl.BlockSpec((None,) + shape, lambda j, *pf: (pf[1 + k][j], 0, 0))
    return pl.pallas_call(
        _experts_kernel,
        out_shape=jax.ShapeDtypeStruct((rows, D), F32),
        grid_spec=pltpu.PrefetchScalarGridSpec(
            num_scalar_prefetch=5,
            grid=(rows // tmr,),
            in_specs=[pl.BlockSpec((tmr, D), lambda j, *pf: (j, 0)),
                      wspec((D, D_FF_EXPERT), 0), wspec((D, D_FF_EXPERT), 1), wspec((D_FF_EXPERT, D), 2)],
            out_specs=pl.BlockSpec((tmr, D), lambda j, *pf: (j, 0)),
            scratch_shapes=[pltpu.VMEM((D, D_FF_EXPERT), BF16), pltpu.VMEM((D, D_FF_EXPERT), BF16),
                            pltpu.VMEM((D_FF_EXPERT, D), BF16)]),
        compiler_params=_cparams(("arbitrary",)),
        name="moe_experts",
    )(tile_expert, *stages, tile_valid, hs, e_gate, e_up, e_down)


def _combine_kernel(pos_ref, x_ref, meta_ref, gt_ref, fg_ref, y_ref, o_ref, b1_sc, b2_sc, sem):
    tm = x_ref.shape[0]
    n_tok = pos_ref.shape[0] // 2
    i = pl.program_id(0)

    def gather(tile, slot):
        base = tile * tm

        def issue(g, carry):
            r0 = pl.multiple_of(g * V7X_SUBLANES, V7X_SUBLANES)
            rows1 = b1_sc.at[slot, pl.ds(r0, V7X_SUBLANES)]
            rows2 = b2_sc.at[slot, pl.ds(r0, V7X_SUBLANES)]
            for u in range(V7X_SUBLANES):
                _row_copy(y_ref, pos_ref[base + r0 + u], rows1, u, sem.at[slot]).start(priority=0)
                _row_copy(y_ref, pos_ref[n_tok + base + r0 + u], rows2, u, sem.at[slot]).start(priority=1)
            return carry

        lax.fori_loop(0, tm // V7X_SUBLANES, issue, 0)

    @pl.when(i == 0)
    def _():
        gather(0, 0)

    @pl.when(i + 1 < pl.num_programs(0))
    def _():
        gather(i + 1, (i + 1) % 2)

    slot = i % 2
    pltpu.make_async_copy(y_ref.at[pl.ds(0, tm)], b1_sc.at[slot], sem.at[slot]).wait()
    pltpu.make_async_copy(y_ref.at[pl.ds(0, tm)], b2_sc.at[slot], sem.at[slot]).wait()

    meta = meta_ref[...]
    lane = lax.broadcasted_iota(jnp.int32, meta.shape, 1)
    g1 = jnp.sum(jnp.where(lane == META_G1, meta, 0.0), axis=-1, keepdims=True)
    g2 = jnp.sum(jnp.where(lane == META_G2, meta, 0.0), axis=-1, keepdims=True)
    x = x_ref[...] + gt_ref[...] * (g1 * b1_sc[slot] + g2 * b2_sc[slot])
    o_ref[...] = _rms(x, fg_ref[...])


def moe_combine(pos, x, meta, modtab, cond, final_g, y):
    tokens = x.shape[0]
    tm = TM_COMBINE
    return pl.pallas_call(
        _combine_kernel,
        out_shape=jax.ShapeDtypeStruct((tokens, D), F32),
        grid_spec=pltpu.PrefetchScalarGridSpec(
            num_scalar_prefetch=1,
            grid=(tokens // tm,),
            in_specs=[pl.BlockSpec((tm, D), lambda i, pos: (i, 0)),
                      pl.BlockSpec((tm, V7X_LANES), lambda i, pos: (i, 0)),
                      _mod_spec(5, cond, tm, D, lambda i, pos: 0),
                      pl.BlockSpec((1, D), lambda i, pos: (0, 0)),
                      pl.BlockSpec(memory_space=pl.ANY)],
            out_specs=pl.BlockSpec((tm, D), lambda i, pos: (i, 0)),
            scratch_shapes=[pltpu.VMEM((2, tm, D), F32), pltpu.VMEM((2, tm, D), F32),
                            pltpu.SemaphoreType.DMA((2,))]),
        compiler_params=_cparams(("arbitrary",)),
        name="moe_combine",
    )(pos, x, meta, modtab, final_g.reshape(1, D), y)


def moe_plan(metas, counts):
    tmr = TM_EXPERT
    cnts = [c[0, :N_EXPERTS].astype(jnp.int32) for c in counts]
    total = functools.reduce(jnp.add, cnts)
    padded = ((total + tmr - 1) // tmr) * tmr
    ends = jnp.cumsum(padded)
    starts = ends - padded
    n_rows = sum(m.shape[1] for m in metas) * 2 + N_EXPERTS * tmr
    n_tiles = n_rows // tmr
    tile_start = jnp.arange(n_tiles, dtype=jnp.int32) * tmr
    tile_expert = jnp.minimum(jnp.sum(tile_start[:, None] >= ends[None, :], axis=1), N_EXPERTS - 1).astype(jnp.int32)
    group_of_tile = jnp.sum(tile_start[:, None] >= ends[None, :], axis=1)
    real_end = jnp.sum(jnp.where(group_of_tile[:, None] == jnp.arange(N_EXPERTS)[None, :],
                                 (starts + total)[None, :], 0), axis=1)
    tile_valid = jnp.clip(real_end - tile_start, 0, tmr).astype(jnp.int32)
    eid = jnp.arange(N_EXPERTS, dtype=jnp.int32)
    later = jnp.where((eid[None, :] > eid[:, None]) & (padded[None, :] > 0), eid[None, :], N_EXPERTS)
    nxt = jnp.min(later, axis=1)
    next_used = jnp.where(nxt == N_EXPERTS, eid, nxt)
    pick = lambda table: jnp.sum(jnp.where(tile_expert[:, None] == eid[None, :], table[None, :], 0), axis=1)
    k_in_group = (tile_start - pick(starts)) // tmr
    tile_next = pick(next_used)
    stages = [jnp.where(k_in_group < k, tile_expert, tile_next).astype(jnp.int32) for k in (1, 2, 3)]
    pos, p1s, p2s = [], [], []
    base = jnp.zeros((N_EXPERTS,), jnp.int32)
    for m, c in zip(metas, cnts):
        first = starts + base
        sel = lambda field: m[field].astype(jnp.int32)
        lookup = lambda e: jnp.sum(jnp.where(e[:, None] == jnp.arange(N_EXPERTS)[None, :], first[None, :], 0), axis=1)
        p1 = lookup(sel(META_E1)) + sel(META_R1)
        p2 = lookup(sel(META_E2)) + sel(META_R2)
        pos.append(jnp.concatenate([p1, p2]).astype(jnp.int32))
        p1s.append(p1)
        p2s.append(p2)
        base = base + c
    pos_all = jnp.concatenate(p1s + p2s).astype(jnp.int32)
    pads = jnp.concatenate([starts + total, padded - total,
                            jnp.stack([ends[-1], n_tiles - ends[-1] // tmr])]).astype(jnp.int32)
    return pos, pos_all, pads, tile_expert, stages, tile_valid, n_rows


def _pad_to(a, shape):
    return jnp.pad(a, [(0, t - s) for s, t in zip(a.shape, shape)])


def _regroup_chunks(a, cb):
    r = a.shape[0]
    return a.reshape(r, 3, HY_W // cb, cb).transpose(2, 0, 1, 3).reshape(HY_W // cb, r, 3 * cb)


def kernel(x_prompt, x_sample, state_l0_lru, cache_l1_ckv, cache_l1_krope, c, c_ctx, l0_norm1, l0_norm2, l0_w_mod, l0_b_mod, l0_w_in, l0_conv_a, l0_lru_conv_w, l0_lru_conv_b, l0_lru_wa, l0_lru_ba, l0_lru_wi, l0_lru_bi, l0_lru_lambda, l0_w_out, l0_ffn_gate, l0_ffn_up, l0_ffn_down, l1_norm1, l1_norm2, l1_w_mod, l1_b_mod, l1_w_in, l1_q_norm, l1_kv_norm, l1_w_q_up, l1_w_kv_up, l1_hy_short_w, l1_hy_short_b, l1_hy_f_w1, l1_hy_f_b1, l1_hy_f_w2, l1_hy_f_b2, l1_hy_f_w3, l1_hy_bias, l1_w_out, l1_router_w, l1_router_b, l1_exp_gate, l1_exp_up, l1_exp_down, final_norm):
    batch, seq, _ = x_prompt.shape
    dec_batch, dec_seq, _ = x_sample.shape
    past_len = cache_l1_ckv.shape[1]

    cond8 = jnp.concatenate([c_ctx[None, :], c, jnp.zeros((V7X_SUBLANES - 1 - dec_batch, D), F32)], axis=0)
    wcat = jnp.concatenate([l0_lru_wa[0], l0_lru_wi[0], l0_lru_wa[1], l0_lru_wi[1]], axis=-1)
    hid = l1_hy_f_w2.shape[0]
    filter_pack = jnp.concatenate([_pad_to(l1_hy_f_w1, (V7X_LANES, hid)), l1_hy_f_b1[None, :], l1_hy_f_b2[None, :],
                                   jnp.zeros((V7X_SUBLANES - 2, hid), F32)], axis=0)
    short_w = _regroup_chunks(l1_hy_short_w, HY_CB)
    short_b = _regroup_chunks(l1_hy_short_b.reshape(1, -1), HY_CB)
    hy_bias = l1_hy_bias.reshape(1, HY_W)
    router_pack = _pad_to(jnp.concatenate([l1_router_w, l1_router_b[None, :]], axis=0),
                          (D + V7X_SUBLANES, V7X_LANES))

    mod0, mod1 = adaln_tables(cond8, ((l0_w_mod, l0_b_mod), (l1_w_mod, l1_b_mod)))

    kv_ctx = kv_up(cache_l1_ckv.reshape(dec_batch * past_len, KV_RANK), l1_w_kv_up)
    kr_ctx = cache_l1_krope.reshape(dec_batch * past_len, ROPE)

    conds = ((0, batch * seq), (1, dec_seq))
    seq_lens = (seq, dec_seq)
    xs = (x_prompt.reshape(batch * seq, D), x_sample.reshape(dec_batch * dec_seq, D))
    h0s = (jnp.zeros((batch, 2, LRU_W), F32), state_l0_lru)

    us = in0_proj(xs, l0_norm1, mod0, conds, l0_w_in)
    parts, lru_states = [], []
    for u, seq_len, h0 in zip(us, seq_lens, h0s):
        ya = conv_a(u, seq_len, l0_conv_a)
        yb, lru_state = rglru(u, seq_len, l0_lru_conv_w, l0_lru_conv_b, wcat, l0_lru_ba, l0_lru_bi,
                              l0_lru_lambda, h0)
        parts.append([(ya, 0), (yb, 0), (yb, 1)])
        lru_states.append(lru_state)
    xs = mix_ffn(parts, l0_w_out, xs, l0_norm2, mod0, conds, l0_ffn_gate, l0_ffn_up, l0_ffn_down)
    new_lru = lru_states[0]

    def layer1(x, seq_len, cond, latent):
        qnope, qpe, ckv, kr, kv, uh = in1_proj(x, l1_norm1, mod1, cond, l1_w_in, l1_q_norm, l1_kv_norm,
                                               l1_w_q_up, l1_w_kv_up)
        if latent:
            yc = attn_lat(qnope, qpe, kv_ctx, kr_ctx, kv, kr, seq_len, past_len)
        else:
            yc = attn_ctx(qnope, qpe, kv, kr, seq_len)
        cs = dft_tables(seq_len)
        k_r, k_s, k_ny = hy_filter(cs, filter_pack, l1_hy_f_w2, l1_hy_f_w3)
        yd = hyena(uh, seq_len, short_w, short_b, cs, k_r, k_s, k_ny, hy_bias)
        routed = mix_route([(yc, 0), (yc, 1), (yd, 0)], l1_w_out, x, l1_norm2, mod1, cond, router_pack)
        return routed, ckv, kr

    r_p, new_ckv, new_kr = layer1(xs[0], seq, conds[0], latent=False)
    r_s, _, _ = layer1(xs[1], dec_seq, conds[1], latent=True)

    routed = (r_p, r_s)
    pos, pos_all, pads, tile_expert, stages, tile_valid, n_rows = moe_plan([r[3] for r in routed],
                                                                          [r[4] for r in routed])
    hs = moe_dispatch(pos_all, pads, n_rows, r_p[1], r_s[1])
    y_rows = moe_experts(tile_expert, stages, tile_valid, hs, l1_exp_gate, l1_exp_up, l1_exp_down)
    y_p, y_s = [moe_combine(p, r[0], r[2], mod1, cond, final_norm, y_rows)
                for p, r, cond in zip(pos, routed, conds)]
    return (y_p.reshape(batch, seq, D), y_s.reshape(dec_batch, dec_seq, D), new_lru,
            new_ckv.reshape(batch, seq, KV_RANK), new_kr.reshape(batch, seq, ROPE))
```
